```python
import math
import jax, jax.numpy as jnp
from jax import lax
import numpy as np

D_MODEL = 4096
BATCH = 8
SEQ = 4096
DEPTH = 1

GRID_W = 64
CTX_LEN = 256
DN = D_MODEL // 2
NH = 16
HD = DN // NH
CONF = D_MODEL - DN
SHORT_CONV = 7
CONF_K = 31
CHUNK = 64
D_FF = 4 * D_MODEL
MOD_INIT = 0.5
Z_OFF = 3 * DN
G_OFF = 4 * DN
CONF_OFF = 4 * DN + 4 * NH
IN_COLS = CONF_OFF + 2 * CONF

kernel_name = "hybrid_deltanet_conformer_dit_block"


def layer_norm(x, g, b, eps=1e-5):
    xf = x.astype(jnp.float32)
    mu = jnp.mean(xf, axis=-1, keepdims=True)
    var = jnp.mean(jnp.square(xf - mu), axis=-1, keepdims=True)
    return ((xf - mu) * lax.rsqrt(var + eps) * g.astype(jnp.float32) + b.astype(jnp.float32)).astype(x.dtype)


def l2norm(x, eps=1e-6):
    xf = x.astype(jnp.float32)
    return xf * lax.rsqrt(jnp.sum(jnp.square(xf), axis=-1, keepdims=True) + eps)


def modulate(x, shift, scale):
    return x * (1.0 + scale) + shift


def dwconv1d(x, w):
    k = w.shape[0]
    return lax.conv_general_dilated(
        x, w[:, None, :].astype(x.dtype), window_strides=(1,), padding=[(k // 2, k // 2)],
        dimension_numbers=('NWC', 'WIO', 'NWC'), feature_group_count=x.shape[-1])


def gated_delta_chunked(q, k, v, g, beta, s0):
    b_, l_, h_, dk = q.shape
    dv = v.shape[-1]
    n = l_ // CHUNK
    f32 = jnp.float32

    def chunks(t):
        t = t.astype(f32).reshape((b_, n, CHUNK, h_) + t.shape[3:])
        return jnp.moveaxis(t, (1, 3), (0, 2))

    q = chunks(q) * (dk ** -0.5)
    k, v, beta = chunks(k), chunks(v), chunks(beta)
    gc = jnp.cumsum(chunks(g), axis=-1)
    tri = jnp.tril(jnp.ones((CHUNK, CHUNK), dtype=bool))
    diff = gc[..., :, None] - gc[..., None, :]
    gamma = jnp.where(tri, jnp.exp(jnp.where(tri, diff, 0.0)), 0.0)
    kb = k * beta[..., None]
    a_mat = jnp.tril(jnp.einsum('nbhck,nbhdk->nbhcd', kb, k) * gamma, -1)
    eye = jnp.eye(CHUNK, dtype=f32)
    rhs = jnp.concatenate([v * beta[..., None], kb * jnp.exp(gc)[..., None]], axis=-1)
    sol = lax.linalg.triangular_solve(eye + a_mat, rhs, left_side=True, lower=True, unit_diagonal=True)
    u, w = sol[..., :dv], sol[..., dv:]
    attn = jnp.einsum('nbhck,nbhdk->nbhcd', q, k) * gamma
    q_dec = q * jnp.exp(gc)[..., None]
    g_last = gc[..., -1:]
    k_dec = k * jnp.exp(g_last - gc)[..., None]
    d_last = jnp.exp(g_last)[..., None]

    def step(s, xs):
        qd, kd, u_c, w_c, at, dl = xs
        v_new = u_c - jnp.einsum('bhck,bhkv->bhcv', w_c, s)
        o = jnp.einsum('bhck,bhkv->bhcv', qd, s) + jnp.einsum('bhcd,bhdv->bhcv', at, v_new)
        s = s * dl + jnp.einsum('bhck,bhcv->bhkv', kd, v_new)
        return s, o

    s_fin, o = lax.scan(step, s0.astype(f32), (q_dec, k_dec, u, w, attn, d_last))
    o = jnp.moveaxis(o, (0, 2), (1, 3)).reshape(b_, l_, h_, dv)
    return o, s_fin


def delta_inputs(h, w_conv, a_log_f, dt_f, a_log_b, dt_b):
    b_, l_, _ = h.shape
    qkv = jax.nn.silu(dwconv1d(h[..., :3 * DN], w_conv))
    q, k, v = [t.reshape(b_, l_, NH, HD) for t in jnp.split(qkv, 3, axis=-1)]
    q, k = l2norm(q), l2norm(k)
    gates = h[..., G_OFF:CONF_OFF].astype(jnp.float32)
    a_f, b_f, a_b, b_b = jnp.split(gates, 4, axis=-1)
    g_f = -jnp.exp(a_log_f.astype(jnp.float32)) * jax.nn.softplus(a_f + dt_f.astype(jnp.float32))
    g_b = -jnp.exp(a_log_b.astype(jnp.float32)) * jax.nn.softplus(a_b + dt_b.astype(jnp.float32))
    return q, k, v, g_f, jax.nn.sigmoid(b_f), g_b, jax.nn.sigmoid(b_b)


def bidir_delta(q, k, v, g_f, beta_f, g_b, beta_b, s0_f, s0_b):
    o_f, s_f = gated_delta_chunked(q, k, v, g_f, beta_f, s0_f)
    flip = lambda t: t[:, ::-1]
    o_b, s_b = gated_delta_chunked(flip(q), flip(k), flip(v), flip(g_b), flip(beta_b), s0_b)
    return o_f + flip(o_b), s_f, s_b


def conformer_conv(h, w_dw, b_dw, ln_g, ln_b, grid):
    val = h[..., CONF_OFF:CONF_OFF + CONF]
    gate = h[..., CONF_OFF + CONF:]
    y = val * jax.nn.sigmoid(gate)
    b_, l_, c_ = y.shape
    if grid:
        rows = l_ // GRID_W
        half = c_ // 2
        yh = dwconv1d(y[..., :half].reshape(b_ * rows, GRID_W, half), w_dw[:, :half]).reshape(b_, l_, half)
        yv = y[..., half:].reshape(b_, rows, GRID_W, half).transpose(0, 2, 1, 3).reshape(b_ * GRID_W, rows, half)
        yv = dwconv1d(yv, w_dw[:, half:]).reshape(b_, GRID_W, rows, half).transpose(0, 2, 1, 3).reshape(b_, l_, half)
        y = jnp.concatenate([yh, yv], axis=-1)
    else:
        y = dwconv1d(y, w_dw)
    y = layer_norm(y + b_dw, ln_g, ln_b)
    return jax.nn.silu(y)


def mixer_output(h, o, dn_norm_g, w_dw, b_dw, cln_g, cln_b, w_out, grid):
    b_, l_, _ = h.shape
    ms = jnp.mean(jnp.square(o), axis=-1, keepdims=True)
    o = (o * lax.rsqrt(ms + 1e-6) * dn_norm_g.astype(jnp.float32)).reshape(b_, l_, DN).astype(h.dtype)
    dn_out = o * jax.nn.silu(h[..., Z_OFF:G_OFF])
    conf_out = conformer_conv(h, w_dw, b_dw, cln_g, cln_b, grid)
    return jnp.concatenate([dn_out, conf_out], axis=-1) @ w_out


def sqrelu_mlp(u, w1, b1, w2, b2):
    return jnp.square(jax.nn.relu(u @ w1 + b1)) @ w2 + b2


def _fwd_setup_inputs(seed: int = 0) -> dict:
    key = jax.random.key(seed)
    ks = jax.random.split(key, 32)
    f32 = jnp.float32
    nrm = lambda k, shape, s: jax.random.normal(k, shape, f32) * s
    beta_dn = (8.0 * DEPTH) ** -0.25
    col_scale = jnp.concatenate([jnp.ones((2 * DN,), f32), jnp.full((DN,), beta_dn, f32),
                                 jnp.ones((IN_COLS - 3 * DN,), f32)])
    dt_f = jnp.exp(jax.random.uniform(ks[12], (DEPTH, NH), f32, math.log(1e-3), math.log(1e-1)))
    dt_b = jnp.exp(jax.random.uniform(ks[13], (DEPTH, NH), f32, math.log(1e-3), math.log(1e-1)))
    return {
        "x": nrm(ks[0], (BATCH, SEQ, D_MODEL), 1.0),
        "c": nrm(ks[1], (BATCH, D_MODEL), 1.0),
        "ctx": nrm(ks[2], (BATCH, CTX_LEN, D_MODEL), 1.0),
        "c_ctx": nrm(ks[3], (D_MODEL,), 1.0),
        "ln_in_g": 1.0 + nrm(ks[4], (D_MODEL,), 0.02),
        "ln_in_b": nrm(ks[5], (D_MODEL,), 0.02),
        "w_mod": nrm(ks[6], (DEPTH, D_MODEL, 6 * D_MODEL), MOD_INIT * D_MODEL ** -0.5),
        "b_mod": nrm(ks[7], (DEPTH, 6 * D_MODEL), 0.02),
        "w_in": nrm(ks[8], (DEPTH, D_MODEL, IN_COLS), D_MODEL ** -0.5) * col_scale,
        "w_qkv_conv": nrm(ks[9], (DEPTH, SHORT_CONV, 3 * DN), SHORT_CONV ** -0.5),
        "a_log_f": jnp.log(jax.random.uniform(ks[10], (DEPTH, NH), f32, 1.0, 16.0)),
        "dt_bias_f": dt_f + jnp.log(-jnp.expm1(-dt_f)),
        "a_log_b": jnp.log(jax.random.uniform(ks[11], (DEPTH, NH), f32, 1.0, 16.0)),
        "dt_bias_b": dt_b + jnp.log(-jnp.expm1(-dt_b)),
        "dn_norm_g": 1.0 + nrm(ks[14], (DEPTH, HD), 0.02),
        "conf_dw_w": nrm(ks[15], (DEPTH, CONF_K, CONF), CONF_K ** -0.5),
        "conf_dw_b": nrm(ks[16], (DEPTH, CONF), 0.02),
        "conf_ln_g": 1.0 + nrm(ks[17], (DEPTH, CONF), 0.02),
        "conf_ln_b": nrm(ks[18], (DEPTH, CONF), 0.02),
        "w_out": nrm(ks[19], (DEPTH, DN + CONF, D_MODEL), beta_dn * (DN + CONF) ** -0.5),
        "ln1_g": 1.0 + nrm(ks[20], (DEPTH, D_MODEL), 0.02),
        "ln1_b": nrm(ks[21], (DEPTH, D_MODEL), 0.02),
        "w_mlp1": nrm(ks[22], (DEPTH, D_MODEL, D_FF), D_MODEL ** -0.5),
        "b_mlp1": nrm(ks[23], (DEPTH, D_FF), 0.02),
        "w_mlp2": nrm(ks[24], (DEPTH, D_FF, D_MODEL), beta_dn * D_FF ** -0.5),
        "b_mlp2": nrm(ks[25], (DEPTH, D_MODEL), 0.02),
        "ln2_g": 1.0 + nrm(ks[26], (DEPTH, D_MODEL), 0.02),
        "ln2_b": nrm(ks[27], (DEPTH, D_MODEL), 0.02),
    }


def _fwd_reference(x, c, ctx, c_ctx, ln_in_g, ln_in_b, w_mod, b_mod, w_in, w_qkv_conv,
              a_log_f, dt_bias_f, a_log_b, dt_bias_b, dn_norm_g, conf_dw_w, conf_dw_b,
              conf_ln_g, conf_ln_b, w_out, ln1_g, ln1_b, w_mlp1, b_mlp1, w_mlp2, b_mlp2,
              ln2_g, ln2_b):
    alpha = (2.0 * DEPTH) ** 0.25
    b_ = x.shape[0]
    x = layer_norm(x, ln_in_g, ln_in_b)
    xc = layer_norm(ctx, ln_in_g, ln_in_b)
    for l in range(DEPTH):
        last = l == DEPTH - 1
        mod = (jax.nn.silu(c) @ w_mod[l] + b_mod[l])[:, None, :]
        mod_c = (jax.nn.silu(c_ctx) @ w_mod[l] + b_mod[l])[None, None, :]
        sh_a, sc_a, g_a, sh_m, sc_m, g_m = jnp.split(mod, 6, axis=-1)
        csh_a, csc_a, cg_a, csh_m, csc_m, cg_m = jnp.split(mod_c, 6, axis=-1)
        dn_par = (w_qkv_conv[l], a_log_f[l], dt_bias_f[l], a_log_b[l], dt_bias_b[l])
        conf_par = (conf_dw_w[l], conf_dw_b[l], conf_ln_g[l], conf_ln_b[l])

        hc = modulate(xc, csh_a, csc_a) @ w_in[l]
        s0 = jnp.zeros((b_, NH, HD, HD), jnp.float32)
        oc, s_f, s_b = bidir_delta(*delta_inputs(hc, *dn_par), s0, s0)

        h = modulate(x, sh_a, sc_a) @ w_in[l]
        o, _, _ = bidir_delta(*delta_inputs(h, *dn_par), s_f, s_b)
        y = mixer_output(h, o, dn_norm_g[l], *conf_par, w_out[l], True)
        x = layer_norm(alpha * x + g_a * y, ln1_g[l], ln1_b[l])
        y2 = sqrelu_mlp(modulate(x, sh_m, sc_m), w_mlp1[l], b_mlp1[l], w_mlp2[l], b_mlp2[l])
        x = layer_norm(alpha * x + g_m * y2, ln2_g[l], ln2_b[l])

        if not last:
            yc = mixer_output(hc, oc, dn_norm_g[l], *conf_par, w_out[l], False)
            xc = layer_norm(alpha * xc + cg_a * yc, ln1_g[l], ln1_b[l])
            yc2 = sqrelu_mlp(modulate(xc, csh_m, csc_m), w_mlp1[l], b_mlp1[l], w_mlp2[l], b_mlp2[l])
            xc = layer_norm(alpha * xc + cg_m * yc2, ln2_g[l], ln2_b[l])
    return x


import jax as _jax
import jax.numpy as _jnp

TWIN_FORMAT = 'train_step'
FWD_PARAMS = ['x', 'c', 'ctx', 'c_ctx', 'ln_in_g', 'ln_in_b', 'w_mod', 'b_mod', 'w_in', 'w_qkv_conv', 'a_log_f', 'dt_bias_f', 'a_log_b', 'dt_bias_b', 'dn_norm_g', 'conf_dw_w', 'conf_dw_b', 'conf_ln_g', 'conf_ln_b', 'w_out', 'ln1_g', 'ln1_b', 'w_mlp1', 'b_mlp1', 'w_mlp2', 'b_mlp2', 'ln2_g', 'ln2_b']
TWIN_WEIGHTS = ['c_ctx', 'ln_in_g', 'ln_in_b', 'w_mod', 'b_mod', 'w_in', 'w_qkv_conv', 'a_log_f', 'dt_bias_f', 'a_log_b', 'dt_bias_b', 'dn_norm_g', 'conf_dw_w', 'conf_dw_b', 'conf_ln_g', 'conf_ln_b', 'w_out', 'ln1_g', 'ln1_b', 'w_mlp1', 'b_mlp1', 'w_mlp2', 'b_mlp2', 'ln2_g', 'ln2_b']
TWIN_DIFF_INPUT = 'x'
TWIN_INPUTS = ['x', 'c', 'ctx', 'c_ctx', 'ln_in_g', 'ln_in_b', 'w_mod', 'b_mod', 'w_in', 'w_qkv_conv', 'a_log_f', 'dt_bias_f', 'a_log_b', 'dt_bias_b', 'dn_norm_g', 'conf_dw_w', 'conf_dw_b', 'conf_ln_g', 'conf_ln_b', 'w_out', 'ln1_g', 'ln1_b', 'w_mlp1', 'b_mlp1', 'w_mlp2', 'b_mlp2', 'ln2_g', 'ln2_b', 'loss_target', 'm_c_ctx', 'm_ln_in_g', 'm_ln_in_b', 'm_w_mod', 'm_b_mod', 'm_w_in', 'm_w_qkv_conv', 'm_a_log_f', 'm_dt_bias_f', 'm_a_log_b', 'm_dt_bias_b', 'm_dn_norm_g', 'm_conf_dw_w', 'm_conf_dw_b', 'm_conf_ln_g', 'm_conf_ln_b', 'm_w_out', 'm_ln1_g', 'm_ln1_b', 'm_w_mlp1', 'm_b_mlp1', 'm_w_mlp2', 'm_b_mlp2', 'm_ln2_g', 'm_ln2_b', 'v_c_ctx', 'v_ln_in_g', 'v_ln_in_b', 'v_w_mod', 'v_b_mod', 'v_w_in', 'v_w_qkv_conv', 'v_a_log_f', 'v_dt_bias_f', 'v_a_log_b', 'v_dt_bias_b', 'v_dn_norm_g', 'v_conf_dw_w', 'v_conf_dw_b', 'v_conf_ln_g', 'v_conf_ln_b', 'v_w_out', 'v_ln1_g', 'v_ln1_b', 'v_w_mlp1', 'v_b_mlp1', 'v_w_mlp2', 'v_b_mlp2', 'v_ln2_g', 'v_ln2_b']
TWIN_OUTPUTS = ['loss', 'grad_x', 'grad_c_ctx', 'grad_ln_in_g', 'grad_ln_in_b', 'grad_w_mod', 'grad_b_mod', 'grad_w_in', 'grad_w_qkv_conv', 'grad_a_log_f', 'grad_dt_bias_f', 'grad_a_log_b', 'grad_dt_bias_b', 'grad_dn_norm_g', 'grad_conf_dw_w', 'grad_conf_dw_b', 'grad_conf_ln_g', 'grad_conf_ln_b', 'grad_w_out', 'grad_ln1_g', 'grad_ln1_b', 'grad_w_mlp1', 'grad_b_mlp1', 'grad_w_mlp2', 'grad_b_mlp2', 'grad_ln2_g', 'grad_ln2_b', 'delta_c_ctx', 'delta_ln_in_g', 'delta_ln_in_b', 'delta_w_mod', 'delta_b_mod', 'delta_w_in', 'delta_w_qkv_conv', 'delta_a_log_f', 'delta_dt_bias_f', 'delta_a_log_b', 'delta_dt_bias_b', 'delta_dn_norm_g', 'delta_conf_dw_w', 'delta_conf_dw_b', 'delta_conf_ln_g', 'delta_conf_ln_b', 'delta_w_out', 'delta_ln1_g', 'delta_ln1_b', 'delta_w_mlp1', 'delta_b_mlp1', 'delta_w_mlp2', 'delta_b_mlp2', 'delta_ln2_g', 'delta_ln2_b', 'new_m_c_ctx', 'new_m_ln_in_g', 'new_m_ln_in_b', 'new_m_w_mod', 'new_m_b_mod', 'new_m_w_in', 'new_m_w_qkv_conv', 'new_m_a_log_f', 'new_m_dt_bias_f', 'new_m_a_log_b', 'new_m_dt_bias_b', 'new_m_dn_norm_g', 'new_m_conf_dw_w', 'new_m_conf_dw_b', 'new_m_conf_ln_g', 'new_m_conf_ln_b', 'new_m_w_out', 'new_m_ln1_g', 'new_m_ln1_b', 'new_m_w_mlp1', 'new_m_b_mlp1', 'new_m_w_mlp2', 'new_m_b_mlp2', 'new_m_ln2_g', 'new_m_ln2_b', 'new_v_c_ctx', 'new_v_ln_in_g', 'new_v_ln_in_b', 'new_v_w_mod', 'new_v_b_mod', 'new_v_w_in', 'new_v_w_qkv_conv', 'new_v_a_log_f', 'new_v_dt_bias_f', 'new_v_a_log_b', 'new_v_dt_bias_b', 'new_v_dn_norm_g', 'new_v_conf_dw_w', 'new_v_conf_dw_b', 'new_v_conf_ln_g', 'new_v_conf_ln_b', 'new_v_w_out', 'new_v_ln1_g', 'new_v_ln1_b', 'new_v_w_mlp1', 'new_v_b_mlp1', 'new_v_w_mlp2', 'new_v_b_mlp2', 'new_v_ln2_g', 'new_v_ln2_b']
TWIN_LEAF_KINDS = {'loss': 'loss', 'grad_x': 'grad_x', 'grad_c_ctx': 'grad_w', 'grad_ln_in_g': 'grad_w', 'grad_ln_in_b': 'grad_w', 'grad_w_mod': 'grad_w', 'grad_b_mod': 'grad_w', 'grad_w_in': 'grad_w', 'grad_w_qkv_conv': 'grad_w', 'grad_a_log_f': 'grad_w', 'grad_dt_bias_f': 'grad_w', 'grad_a_log_b': 'grad_w', 'grad_dt_bias_b': 'grad_w', 'grad_dn_norm_g': 'grad_w', 'grad_conf_dw_w': 'grad_w', 'grad_conf_dw_b': 'grad_w', 'grad_conf_ln_g': 'grad_w', 'grad_conf_ln_b': 'grad_w', 'grad_w_out': 'grad_w', 'grad_ln1_g': 'grad_w', 'grad_ln1_b': 'grad_w', 'grad_w_mlp1': 'grad_w', 'grad_b_mlp1': 'grad_w', 'grad_w_mlp2': 'grad_w', 'grad_b_mlp2': 'grad_w', 'grad_ln2_g': 'grad_w', 'grad_ln2_b': 'grad_w', 'delta_c_ctx': 'delta_w', 'delta_ln_in_g': 'delta_w', 'delta_ln_in_b': 'delta_w', 'delta_w_mod': 'delta_w', 'delta_b_mod': 'delta_w', 'delta_w_in': 'delta_w', 'delta_w_qkv_conv': 'delta_w', 'delta_a_log_f': 'delta_w', 'delta_dt_bias_f': 'delta_w', 'delta_a_log_b': 'delta_w', 'delta_dt_bias_b': 'delta_w', 'delta_dn_norm_g': 'delta_w', 'delta_conf_dw_w': 'delta_w', 'delta_conf_dw_b': 'delta_w', 'delta_conf_ln_g': 'delta_w', 'delta_conf_ln_b': 'delta_w', 'delta_w_out': 'delta_w', 'delta_ln1_g': 'delta_w', 'delta_ln1_b': 'delta_w', 'delta_w_mlp1': 'delta_w', 'delta_b_mlp1': 'delta_w', 'delta_w_mlp2': 'delta_w', 'delta_b_mlp2': 'delta_w', 'delta_ln2_g': 'delta_w', 'delta_ln2_b': 'delta_w', 'new_m_c_ctx': 'new_m', 'new_m_ln_in_g': 'new_m', 'new_m_ln_in_b': 'new_m', 'new_m_w_mod': 'new_m', 'new_m_b_mod': 'new_m', 'new_m_w_in': 'new_m', 'new_m_w_qkv_conv': 'new_m', 'new_m_a_log_f': 'new_m', 'new_m_dt_bias_f': 'new_m', 'new_m_a_log_b': 'new_m', 'new_m_dt_bias_b': 'new_m', 'new_m_dn_norm_g': 'new_m', 'new_m_conf_dw_w': 'new_m', 'new_m_conf_dw_b': 'new_m', 'new_m_conf_ln_g': 'new_m', 'new_m_conf_ln_b': 'new_m', 'new_m_w_out': 'new_m', 'new_m_ln1_g': 'new_m', 'new_m_ln1_b': 'new_m', 'new_m_w_mlp1': 'new_m', 'new_m_b_mlp1': 'new_m', 'new_m_w_mlp2': 'new_m', 'new_m_b_mlp2': 'new_m', 'new_m_ln2_g': 'new_m', 'new_m_ln2_b': 'new_m', 'new_v_c_ctx': 'new_v', 'new_v_ln_in_g': 'new_v', 'new_v_ln_in_b': 'new_v', 'new_v_w_mod': 'new_v', 'new_v_b_mod': 'new_v', 'new_v_w_in': 'new_v', 'new_v_w_qkv_conv': 'new_v', 'new_v_a_log_f': 'new_v', 'new_v_dt_bias_f': 'new_v', 'new_v_a_log_b': 'new_v', 'new_v_dt_bias_b': 'new_v', 'new_v_dn_norm_g': 'new_v', 'new_v_conf_dw_w': 'new_v', 'new_v_conf_dw_b': 'new_v', 'new_v_conf_ln_g': 'new_v', 'new_v_conf_ln_b': 'new_v', 'new_v_w_out': 'new_v', 'new_v_ln1_g': 'new_v', 'new_v_ln1_b': 'new_v', 'new_v_w_mlp1': 'new_v', 'new_v_b_mlp1': 'new_v', 'new_v_w_mlp2': 'new_v', 'new_v_b_mlp2': 'new_v', 'new_v_ln2_g': 'new_v', 'new_v_ln2_b': 'new_v'}


def _forward(args):
    return _fwd_reference(*[args[k] for k in FWD_PARAMS])


def _output_shape():
    out = _jax.eval_shape(lambda: _forward(_fwd_setup_inputs(0)))
    return out.shape, out.dtype

N_MICROBATCH = 1
ADAM_LR = 0.001
ADAM_B1 = 0.9
ADAM_B2 = 0.999
ADAM_EPS = 1e-08
ADAM_WD = 0.01
ADAM_STEP = 10
PER_EXAMPLE_BATCH_AXIS = {'x': 0, 'c': 0, 'ctx': 0, 'loss_target': 0}
SHARED_INPUTS = []
_WEIGHT_DTYPES = {'c_ctx': _jnp.float32, 'ln_in_g': _jnp.float32, 'ln_in_b': _jnp.float32, 'w_mod': _jnp.float32, 'b_mod': _jnp.float32, 'w_in': _jnp.float32, 'w_qkv_conv': _jnp.float32, 'a_log_f': _jnp.float32, 'dt_bias_f': _jnp.float32, 'a_log_b': _jnp.float32, 'dt_bias_b': _jnp.float32, 'dn_norm_g': _jnp.float32, 'conf_dw_w': _jnp.float32, 'conf_dw_b': _jnp.float32, 'conf_ln_g': _jnp.float32, 'conf_ln_b': _jnp.float32, 'w_out': _jnp.float32, 'ln1_g': _jnp.float32, 'ln1_b': _jnp.float32, 'w_mlp1': _jnp.float32, 'b_mlp1': _jnp.float32, 'w_mlp2': _jnp.float32, 'b_mlp2': _jnp.float32, 'ln2_g': _jnp.float32, 'ln2_b': _jnp.float32}
MOMENT_SCALE = {'c_ctx': 2.412750e-04, 'ln_in_g': 2.370679e-01, 'ln_in_b': 1.137078e-01, 'w_mod': 1.066333e-02, 'b_mod': 2.060346e-02, 'w_in': 4.055867e-03, 'w_qkv_conv': 3.270828e-03, 'a_log_f': 1.032129e-02, 'dt_bias_f': 1.012536e-02, 'a_log_b': 9.872353e-03, 'dt_bias_b': 9.502003e-03, 'dn_norm_g': 1.815744e-02, 'conf_dw_w': 4.275639e-03, 'conf_dw_b': 1.029581e-02, 'conf_ln_g': 5.549730e-03, 'conf_ln_b': 5.772613e-03, 'w_out': 7.338201e-03, 'ln1_g': 2.393018e-01, 'ln1_b': 1.139210e-01, 'w_mlp1': 5.480171e-03, 'b_mlp1': 6.177961e-03, 'w_mlp2': 1.775610e-02, 'b_mlp2': 1.771134e-02, 'ln2_g': 8.000468e+00, 'ln2_b': 4.363844e-01}


def _to_microbatches(a, axis):
    t = _jnp.moveaxis(a, axis, 0)
    t = t.reshape((N_MICROBATCH, t.shape[0] // N_MICROBATCH) + t.shape[1:])
    return _jnp.moveaxis(t, 1, axis + 1)


def setup_inputs(seed: int = 0) -> dict:
    inp = _fwd_setup_inputs(seed)
    key = _jax.random.fold_in(_jax.random.key(seed), 7919)
    shape, _ = _output_shape()
    out = dict(inp)
    out["loss_target"] = _jax.random.normal(_jax.random.fold_in(key, 0), shape, _jnp.float32)
    for i, name in enumerate(TWIN_WEIGHTS):
        w = inp[name].astype(_jnp.float32)
        if MOMENT_SCALE is None:
            s = _jnp.sqrt(_jnp.mean(_jnp.square(w)) + 1e-30)
        else:
            s = MOMENT_SCALE[name]
        km, kv = _jax.random.split(_jax.random.fold_in(key, i + 1))
        out[name] = w
        out["m_" + name] = s * _jax.random.normal(km, w.shape, _jnp.float32)
        out["v_" + name] = (s * s) * _jax.random.uniform(kv, w.shape, _jnp.float32, 0.5, 1.5)
    if N_MICROBATCH > 1:
        for name, axis in PER_EXAMPLE_BATCH_AXIS.items():
            out[name] = _to_microbatches(out[name], axis)
    return {'x': out['x'], 'c': out['c'], 'ctx': out['ctx'], 'c_ctx': out['c_ctx'], 'ln_in_g': out['ln_in_g'], 'ln_in_b': out['ln_in_b'], 'w_mod': out['w_mod'], 'b_mod': out['b_mod'], 'w_in': out['w_in'], 'w_qkv_conv': out['w_qkv_conv'], 'a_log_f': out['a_log_f'], 'dt_bias_f': out['dt_bias_f'], 'a_log_b': out['a_log_b'], 'dt_bias_b': out['dt_bias_b'], 'dn_norm_g': out['dn_norm_g'], 'conf_dw_w': out['conf_dw_w'], 'conf_dw_b': out['conf_dw_b'], 'conf_ln_g': out['conf_ln_g'], 'conf_ln_b': out['conf_ln_b'], 'w_out': out['w_out'], 'ln1_g': out['ln1_g'], 'ln1_b': out['ln1_b'], 'w_mlp1': out['w_mlp1'], 'b_mlp1': out['b_mlp1'], 'w_mlp2': out['w_mlp2'], 'b_mlp2': out['b_mlp2'], 'ln2_g': out['ln2_g'], 'ln2_b': out['ln2_b'], 'loss_target': out['loss_target'], 'm_c_ctx': out['m_c_ctx'], 'm_ln_in_g': out['m_ln_in_g'], 'm_ln_in_b': out['m_ln_in_b'], 'm_w_mod': out['m_w_mod'], 'm_b_mod': out['m_b_mod'], 'm_w_in': out['m_w_in'], 'm_w_qkv_conv': out['m_w_qkv_conv'], 'm_a_log_f': out['m_a_log_f'], 'm_dt_bias_f': out['m_dt_bias_f'], 'm_a_log_b': out['m_a_log_b'], 'm_dt_bias_b': out['m_dt_bias_b'], 'm_dn_norm_g': out['m_dn_norm_g'], 'm_conf_dw_w': out['m_conf_dw_w'], 'm_conf_dw_b': out['m_conf_dw_b'], 'm_conf_ln_g': out['m_conf_ln_g'], 'm_conf_ln_b': out['m_conf_ln_b'], 'm_w_out': out['m_w_out'], 'm_ln1_g': out['m_ln1_g'], 'm_ln1_b': out['m_ln1_b'], 'm_w_mlp1': out['m_w_mlp1'], 'm_b_mlp1': out['m_b_mlp1'], 'm_w_mlp2': out['m_w_mlp2'], 'm_b_mlp2': out['m_b_mlp2'], 'm_ln2_g': out['m_ln2_g'], 'm_ln2_b': out['m_ln2_b'], 'v_c_ctx': out['v_c_ctx'], 'v_ln_in_g': out['v_ln_in_g'], 'v_ln_in_b': out['v_ln_in_b'], 'v_w_mod': out['v_w_mod'], 'v_b_mod': out['v_b_mod'], 'v_w_in': out['v_w_in'], 'v_w_qkv_conv': out['v_w_qkv_conv'], 'v_a_log_f': out['v_a_log_f'], 'v_dt_bias_f': out['v_dt_bias_f'], 'v_a_log_b': out['v_a_log_b'], 'v_dt_bias_b': out['v_dt_bias_b'], 'v_dn_norm_g': out['v_dn_norm_g'], 'v_conf_dw_w': out['v_conf_dw_w'], 'v_conf_dw_b': out['v_conf_dw_b'], 'v_conf_ln_g': out['v_conf_ln_g'], 'v_conf_ln_b': out['v_conf_ln_b'], 'v_w_out': out['v_w_out'], 'v_ln1_g': out['v_ln1_g'], 'v_ln1_b': out['v_ln1_b'], 'v_w_mlp1': out['v_w_mlp1'], 'v_b_mlp1': out['v_b_mlp1'], 'v_w_mlp2': out['v_w_mlp2'], 'v_b_mlp2': out['v_b_mlp2'], 'v_ln2_g': out['v_ln2_g'], 'v_ln2_b': out['v_ln2_b']}


def _loss(weights, diff, rest, loss_target):
    with _jax.named_scope("forward"):
        args = {**rest, TWIN_DIFF_INPUT: diff, **{k: w.astype(_WEIGHT_DTYPES[k]) for k, w in weights.items()}}
        y = _forward(args)
    with _jax.named_scope("loss_head"):
        err = _jnp.square(y.astype(_jnp.float32) - loss_target)
        return 0.5 * _jnp.sum(_jnp.mean(err, axis=-1)) if err.ndim else 0.5 * err


def _adamw(w, g, m, v):
    m = ADAM_B1 * m + (1.0 - ADAM_B1) * g
    v = ADAM_B2 * v + (1.0 - ADAM_B2) * _jnp.square(g)
    m_hat = m / (1.0 - ADAM_B1 ** ADAM_STEP)
    v_hat = v / (1.0 - ADAM_B2 ** ADAM_STEP)
    delta = -ADAM_LR * (m_hat / (_jnp.sqrt(v_hat) + ADAM_EPS) + ADAM_WD * w)
    return delta, m, v


def reference(x, c, ctx, c_ctx, ln_in_g, ln_in_b, w_mod, b_mod, w_in, w_qkv_conv, a_log_f, dt_bias_f, a_log_b, dt_bias_b, dn_norm_g, conf_dw_w, conf_dw_b, conf_ln_g, conf_ln_b, w_out, ln1_g, ln1_b, w_mlp1, b_mlp1, w_mlp2, b_mlp2, ln2_g, ln2_b, loss_target, m_c_ctx, m_ln_in_g, m_ln_in_b, m_w_mod, m_b_mod, m_w_in, m_w_qkv_conv, m_a_log_f, m_dt_bias_f, m_a_log_b, m_dt_bias_b, m_dn_norm_g, m_conf_dw_w, m_conf_dw_b, m_conf_ln_g, m_conf_ln_b, m_w_out, m_ln1_g, m_ln1_b, m_w_mlp1, m_b_mlp1, m_w_mlp2, m_b_mlp2, m_ln2_g, m_ln2_b, v_c_ctx, v_ln_in_g, v_ln_in_b, v_w_mod, v_b_mod, v_w_in, v_w_qkv_conv, v_a_log_f, v_dt_bias_f, v_a_log_b, v_dt_bias_b, v_dn_norm_g, v_conf_dw_w, v_conf_dw_b, v_conf_ln_g, v_conf_ln_b, v_w_out, v_ln1_g, v_ln1_b, v_w_mlp1, v_b_mlp1, v_w_mlp2, v_b_mlp2, v_ln2_g, v_ln2_b):
    given = dict(x=x, c=c, ctx=ctx, c_ctx=c_ctx, ln_in_g=ln_in_g, ln_in_b=ln_in_b, w_mod=w_mod, b_mod=b_mod, w_in=w_in, w_qkv_conv=w_qkv_conv, a_log_f=a_log_f, dt_bias_f=dt_bias_f, a_log_b=a_log_b, dt_bias_b=dt_bias_b, dn_norm_g=dn_norm_g, conf_dw_w=conf_dw_w, conf_dw_b=conf_dw_b, conf_ln_g=conf_ln_g, conf_ln_b=conf_ln_b, w_out=w_out, ln1_g=ln1_g, ln1_b=ln1_b, w_mlp1=w_mlp1, b_mlp1=b_mlp1, w_mlp2=w_mlp2, b_mlp2=b_mlp2, ln2_g=ln2_g, ln2_b=ln2_b, loss_target=loss_target, m_c_ctx=m_c_ctx, m_ln_in_g=m_ln_in_g, m_ln_in_b=m_ln_in_b, m_w_mod=m_w_mod, m_b_mod=m_b_mod, m_w_in=m_w_in, m_w_qkv_conv=m_w_qkv_conv, m_a_log_f=m_a_log_f, m_dt_bias_f=m_dt_bias_f, m_a_log_b=m_a_log_b, m_dt_bias_b=m_dt_bias_b, m_dn_norm_g=m_dn_norm_g, m_conf_dw_w=m_conf_dw_w, m_conf_dw_b=m_conf_dw_b, m_conf_ln_g=m_conf_ln_g, m_conf_ln_b=m_conf_ln_b, m_w_out=m_w_out, m_ln1_g=m_ln1_g, m_ln1_b=m_ln1_b, m_w_mlp1=m_w_mlp1, m_b_mlp1=m_b_mlp1, m_w_mlp2=m_w_mlp2, m_b_mlp2=m_b_mlp2, m_ln2_g=m_ln2_g, m_ln2_b=m_ln2_b, v_c_ctx=v_c_ctx, v_ln_in_g=v_ln_in_g, v_ln_in_b=v_ln_in_b, v_w_mod=v_w_mod, v_b_mod=v_b_mod, v_w_in=v_w_in, v_w_qkv_conv=v_w_qkv_conv, v_a_log_f=v_a_log_f, v_dt_bias_f=v_dt_bias_f, v_a_log_b=v_a_log_b, v_dt_bias_b=v_dt_bias_b, v_dn_norm_g=v_dn_norm_g, v_conf_dw_w=v_conf_dw_w, v_conf_dw_b=v_conf_dw_b, v_conf_ln_g=v_conf_ln_g, v_conf_ln_b=v_conf_ln_b, v_w_out=v_w_out, v_ln1_g=v_ln1_g, v_ln1_b=v_ln1_b, v_w_mlp1=v_w_mlp1, v_b_mlp1=v_b_mlp1, v_w_mlp2=v_w_mlp2, v_b_mlp2=v_b_mlp2, v_ln2_g=v_ln2_g, v_ln2_b=v_ln2_b)
    weights = {n: given[n] for n in TWIN_WEIGHTS}
    shared = {n: given[n] for n in SHARED_INPUTS}
    per_example = {n: given[n] for n in ['x', 'c', 'ctx']}
    grad_fn = _jax.value_and_grad(_loss, argnums=(0, 1))

    def one_microbatch(ex, loss_target):
        ex = dict(ex)
        diff = ex.pop(TWIN_DIFF_INPUT)
        return grad_fn(weights, diff, {**shared, **ex}, loss_target)

    if N_MICROBATCH == 1:
        loss, (grad_w, grad_x) = one_microbatch(per_example, given["loss_target"])
    else:
        def body(carry, xs):
            loss_sum, grad_sum = carry
            l_k, (gw_k, gx_k) = one_microbatch(xs[0], xs[1])
            with _jax.named_scope("update"):
                return (loss_sum + l_k, _jax.tree.map(_jnp.add, grad_sum, gw_k)), gx_k

        init = (_jnp.zeros((), _jnp.float32), _jax.tree.map(_jnp.zeros_like, weights))
        (loss, grad_w), grad_x = _jax.lax.scan(body, init, (per_example, given["loss_target"]))
    with _jax.named_scope("update"):
        delta_w, new_m, new_v = {}, {}, {}
        for n in TWIN_WEIGHTS:
            delta_w[n], new_m[n], new_v[n] = _adamw(weights[n], grad_w[n], given["m_" + n], given["v_" + n])
    return (loss, grad_x, *[grad_w[n] for n in TWIN_WEIGHTS], *[delta_w[n] for n in TWIN_WEIGHTS],
            *[new_m[n] for n in TWIN_WEIGHTS], *[new_v[n] for n in TWIN_WEIGHTS])
```

```python
import functools
import math

import jax
import jax.numpy as jnp
from jax import lax
from jax.experimental import pallas as pl
from jax.experimental.pallas import tpu as pltpu

F32 = jnp.float32
BF16 = jnp.bfloat16
MXU_DTYPE = BF16
HIGHEST = lax.Precision.HIGHEST

HD = 128
CHUNK = 64
SHORT_CONV = 7
CONF_K = 31
ALPHA = 2.0 ** 0.25
LN_EPS = 1e-5
V7X_VMEM_BYTES = 64 * 1024 * 1024
LANES = 128

ADAM_LR = 0.001
ADAM_B1 = 0.9
ADAM_B2 = 0.999
ADAM_EPS = 1e-08
ADAM_WD = 0.01
ADAM_STEP = 10

MESH_AXES = ("x", "y", "c")
ANY = pl.BlockSpec(memory_space=pl.ANY)


def _pick(dim, pref, mult=LANES):
    best = None
    t = mult
    while t <= min(dim, pref):
        if dim % t == 0:
            best = t
        t += mult
    return best if best is not None else dim


def _cparams(sem, vmem_est):
    limit = int(min(V7X_VMEM_BYTES - 6 * 1024 * 1024, max(32 * 1024 * 1024, vmem_est + 8 * 1024 * 1024)))
    return pltpu.CompilerParams(dimension_semantics=sem, vmem_limit_bytes=limit)


def _nbytes(shape, dtype):
    return math.prod(shape) * jnp.dtype(dtype).itemsize


def _ln(x, g, b):
    mu = jnp.mean(x, axis=-1, keepdims=True)
    xc = x - mu
    var = jnp.mean(xc * xc, axis=-1, keepdims=True)
    return xc * lax.rsqrt(var + LN_EPS) * g + b


def _silu(x):
    return x * jax.nn.sigmoid(x)


def _softplus(x):
    return jnp.maximum(x, 0.0) + jnp.log1p(jnp.exp(-jnp.abs(x)))


_DIMS = {"nn": (((1,), (0,)), ((), ())), "nt": (((1,), (1,)), ((), ())), "tn": (((0,), (0,)), ((), ()))}


def _raw_dot(a, b, form, kind):
    if kind == "mxu" and MXU_DTYPE != F32:
        return lax.dot_general(a.astype(MXU_DTYPE), b.astype(MXU_DTYPE), _DIMS[form], preferred_element_type=F32)
    return lax.dot_general(a.astype(F32), b.astype(F32), _DIMS[form], precision=HIGHEST, preferred_element_type=F32)


@functools.partial(jax.custom_vjp, nondiff_argnums=(2, 3))
def _dot(a, b, form, kind):
    return _raw_dot(a, b, form, kind)


def _dot_fwd(a, b, form, kind):
    return _raw_dot(a, b, form, kind), (a, b)


def _dot_bwd(form, kind, res, dc):
    a, b = res
    if form == "nn":
        return _dot(dc, b, "nt", kind), _dot(a, dc, "tn", kind)
    if form == "nt":
        return _dot(dc, b, "nn", kind), _dot(dc, a, "tn", kind)
    return _dot(b, dc, "nt", kind), _dot(a, dc, "nn", kind)


_dot.defvjp(_dot_fwd, _dot_bwd)


def _spec2(block, idx, split=None):
    if split is None:
        return pl.BlockSpec(tuple(block), idx)
    per = split // block[1]

    def idx3(*g):
        r, cblk = idx(*g)
        return (cblk // per, r, cblk % per)

    return pl.BlockSpec((None,) + tuple(block), idx3)


def _matmul(a, b, *, form, m, n, k, tm, tn, tk, out_dtype, name, bias=None, a_fn=None, b_fn=None,
            epi=None, epi_in=None, colsum=False, kind="mxu", b_split=None, o_split=None):
    assert m % tm == 0 and n % tn == 0 and k % tk == 0, (name, m, n, k, tm, tn, tk)
    nk = k // tk
    grid = (n // tn, m // tm, nk)
    if form == "tn":
        a_spec = pl.BlockSpec((tk, tm), lambda j, i, kk: (kk, i))
    else:
        a_spec = pl.BlockSpec((tm, tk), lambda j, i, kk: (i, kk))
    if form == "nt":
        b_spec = _spec2((tn, tk), lambda j, i, kk: (j, kk), b_split)
    else:
        b_spec = _spec2((tk, tn), lambda j, i, kk: (kk, j), b_split)
    in_specs = [a_spec, b_spec]
    operands = [a, b]
    if bias is not None:
        in_specs.append(pl.BlockSpec((1, tn), lambda j, i, kk: (0, j)))
        operands.append(bias)
    if epi_in is not None:
        in_specs.append(pl.BlockSpec((tm, tn), lambda j, i, kk: (i, j)))
        operands.append(epi_in)
    out_specs = [_spec2((tm, tn), lambda j, i, kk: (i, j), o_split)]
    if o_split is None:
        out_shape = [jax.ShapeDtypeStruct((m, n), out_dtype)]
    else:
        out_shape = [jax.ShapeDtypeStruct((n // o_split, m, o_split), out_dtype)]
    if colsum:
        out_specs.append(pl.BlockSpec((1, tn), lambda j, i, kk: (0, j)))
        out_shape.append(jax.ShapeDtypeStruct((1, n), F32))
    has_bias, has_epi_in = bias is not None, epi_in is not None

    def body(*refs):
        refs = list(refs)
        a_ref, b_ref = refs[0], refs[1]
        pos = 2
        bias_ref = epi_ref = cs_ref = None
        if has_bias:
            bias_ref = refs[pos]
            pos += 1
        if has_epi_in:
            epi_ref = refs[pos]
            pos += 1
        o_ref = refs[pos]
        pos += 1
        if colsum:
            cs_ref = refs[pos]
            pos += 1
        acc_ref = refs[pos]
        i, kk = pl.program_id(1), pl.program_id(2)

        @pl.when(kk == 0)
        def _():
            acc_ref[...] = jnp.zeros_like(acc_ref)

        av = a_ref[...]
        if a_fn is not None:
            av = a_fn(av)
        bv = b_ref[...]
        if b_fn is not None:
            bv = b_fn(bv)
        acc_ref[...] += _raw_dot(av, bv, form, kind)

        @pl.when(kk == nk - 1)
        def _():
            r = acc_ref[...]
            if has_bias:
                r = r + bias_ref[...]
            if epi is not None:
                r = epi(r, epi_ref[...]) if has_epi_in else epi(r)
            o_ref[...] = r.astype(out_dtype)
            if colsum:
                s = jnp.sum(r, axis=0, keepdims=True)

                @pl.when(i == 0)
                def _():
                    cs_ref[...] = s

                @pl.when(i > 0)
                def _():
                    cs_ref[...] += s

    est = 2 * (_nbytes((tm, tk), a.dtype) + _nbytes((tk, tn), b.dtype) + _nbytes((tm, tn), out_dtype))
    est += _nbytes((tm, tn), F32) * 2 + (2 * _nbytes((tm, tn), epi_in.dtype) if has_epi_in else 0)
    res = pl.pallas_call(
        body, grid=grid, in_specs=in_specs, out_specs=out_specs, out_shape=out_shape,
        scratch_shapes=[pltpu.VMEM((tm, tn), F32)], name=name,
        compiler_params=_cparams(("arbitrary", "arbitrary", "arbitrary"), est),
    )(*operands)
    return res if colsum else res[0]


def _rowwise(fn, rows, params, row_outs, acc_outs, *, n_rows, tm, name):
    assert n_rows % tm == 0, (name, n_rows, tm)
    nr, npar, nro, nac = len(rows), len(params), len(row_outs), len(acc_outs)
    in_specs = [pl.BlockSpec((tm, w), functools.partial(lambda i, cb, ro: (i + ro, cb), cb=cb, ro=ro))
                for (_, w, cb, ro) in rows]
    in_specs += [pl.BlockSpec((1, p.shape[1]), lambda i: (0, 0)) for p in params]
    out_specs = [pl.BlockSpec((tm, w), lambda i: (i, 0)) for (w, _) in row_outs]
    out_specs += [pl.BlockSpec((1, w), lambda i: (0, 0)) for w in acc_outs]
    out_shape = [jax.ShapeDtypeStruct((n_rows, w), dt) for (w, dt) in row_outs]
    out_shape += [jax.ShapeDtypeStruct((1, w), F32) for w in acc_outs]

    def body(*refs):
        rv = [r[...] for r in refs[:nr]]
        pv = [r[...] for r in refs[nr:nr + npar]]
        ro_refs = refs[nr + npar:nr + npar + nro]
        ac_refs = refs[nr + npar + nro:]
        ro, ac = fn(rv, pv)
        for ref, val in zip(ro_refs, ro, strict=True):
            ref[...] = val.astype(ref.dtype)
        first = pl.program_id(0) == 0
        for ref, val in zip(ac_refs, ac, strict=True):
            val = jnp.broadcast_to(val.astype(F32), ref.shape)

            @pl.when(first)
            def _(ref=ref, val=val):
                ref[...] = val

            @pl.when(jnp.logical_not(first))
            def _(ref=ref, val=val):
                ref[...] += val

    io = sum(_nbytes((tm, w), a.dtype) for (a, w, _, _) in rows) + sum(_nbytes((tm, w), dt) for (w, dt) in row_outs)
    widest = max([w for (_, w, _, _) in rows] + [w for (w, _) in row_outs])
    est = 2 * io + 12 * _nbytes((tm, widest), F32)
    return pl.pallas_call(
        body, grid=(n_rows // tm,), in_specs=in_specs, out_specs=out_specs, out_shape=out_shape, name=name,
        compiler_params=_cparams(("arbitrary",), est),
    )(*[a for (a, _, _, _) in rows], *params)


def _conv7_tiles(n_lat, n_ctx):
    tt = min(256, n_ctx)
    assert n_lat % tt == 0 and n_ctx % tt == 0
    return tt, [(t0, 8 + t0) for t0 in range(0, n_lat, tt)] + [(n_lat + t0, 16 + n_lat + t0) for t0 in range(0, n_ctx, tt)]


def _conv7_fill(p_ref, x_ref, n_lat, n_ctx):
    z8 = jnp.zeros((8, LANES), F32)
    p_ref[0:8, :] = z8
    p_ref[8:8 + n_lat, :] = x_ref[0:n_lat, :]
    p_ref[8 + n_lat:16 + n_lat, :] = z8
    p_ref[16 + n_lat:16 + n_lat + n_ctx, :] = x_ref[n_lat:n_lat + n_ctx, :]
    p_ref[16 + n_lat + n_ctx:24 + n_lat + n_ctx, :] = z8


def _conv7_fwd(h, w_conv, *, n_lat, n_ctx, nh):
    nt = n_lat + n_ctx
    tt, tiles = _conv7_tiles(n_lat, n_ctx)
    half = SHORT_CONV // 2

    def body(x_ref, w_ref, o_ref, p_ref):
        _conv7_fill(p_ref, x_ref, n_lat, n_ctx)
        for (row, prow) in tiles:
            acc = jnp.zeros((tt, LANES), F32)
            for kk in range(SHORT_CONV):
                acc = acc + w_ref[kk:kk + 1, :] * p_ref[pl.ds(prow + kk - half, tt), :]
            o_ref[pl.ds(row, tt), :] = _silu(acc)

    return pl.pallas_call(
        body, grid=(3 * nh,),
        in_specs=[pl.BlockSpec((nt, LANES), lambda j: (0, j)), pl.BlockSpec((SHORT_CONV, LANES), lambda j: (0, j))],
        out_specs=pl.BlockSpec((None, None, nt, LANES), lambda j: (j // nh, j % nh, 0, 0)),
        out_shape=jax.ShapeDtypeStruct((3, nh, nt, LANES), F32),
        scratch_shapes=[pltpu.VMEM((nt + 24, LANES), F32)], name="conv7_fwd",
        compiler_params=_cparams(("arbitrary",), 5 * _nbytes((nt + 24, LANES), F32)),
    )(h, w_conv)


def _conv7_bwd(h, w_conv, dact_f, dact_b, *, n_lat, n_ctx, nh):
    nt = n_lat + n_ctx
    tt, tiles = _conv7_tiles(n_lat, n_ctx)
    half = SHORT_CONV // 2

    def body(x_ref, w_ref, df_ref, db_ref, dx_ref, dw_ref, p_ref, q_ref):
        _conv7_fill(p_ref, x_ref, n_lat, n_ctx)
        z8 = jnp.zeros((8, LANES), F32)
        q_ref[0:8, :] = z8
        q_ref[8 + n_lat:16 + n_lat, :] = z8
        q_ref[16 + n_lat + n_ctx:24 + n_lat + n_ctx, :] = z8
        dw = [jnp.zeros((1, LANES), F32) for _ in range(SHORT_CONV)]
        for (row, prow) in tiles:
            pre = jnp.zeros((tt, LANES), F32)
            for kk in range(SHORT_CONV):
                pre = pre + w_ref[kk:kk + 1, :] * p_ref[pl.ds(prow + kk - half, tt), :]
            s = jax.nn.sigmoid(pre)
            dpre = (df_ref[pl.ds(row, tt), :] + db_ref[pl.ds(row, tt), :]) * (s * (1.0 + pre * (1.0 - s)))
            q_ref[pl.ds(prow, tt), :] = dpre
            for kk in range(SHORT_CONV):
                dw[kk] = dw[kk] + jnp.sum(dpre * p_ref[pl.ds(prow + kk - half, tt), :], axis=0, keepdims=True)
        for (row, prow) in tiles:
            acc = jnp.zeros((tt, LANES), F32)
            for kk in range(SHORT_CONV):
                acc = acc + w_ref[kk:kk + 1, :] * q_ref[pl.ds(prow + half - kk, tt), :]
            dx_ref[pl.ds(row, tt), :] = acc.astype(dx_ref.dtype)
        for kk in range(SHORT_CONV):
            dw_ref[kk:kk + 1, :] = dw[kk]

    dspec = pl.BlockSpec((None, None, nt, LANES), lambda j: (j // nh, j % nh, 0, 0))
    return pl.pallas_call(
        body, grid=(3 * nh,),
        in_specs=[pl.BlockSpec((nt, LANES), lambda j: (0, j)), pl.BlockSpec((SHORT_CONV, LANES), lambda j: (0, j)), dspec, dspec],
        out_specs=[pl.BlockSpec((nt, LANES), lambda j: (0, j)), pl.BlockSpec((SHORT_CONV, LANES), lambda j: (0, j))],
        out_shape=[jax.ShapeDtypeStruct((nt, 3 * nh * LANES), MXU_DTYPE), jax.ShapeDtypeStruct((SHORT_CONV, 3 * nh * LANES), F32)],
        scratch_shapes=[pltpu.VMEM((nt + 24, LANES), F32), pltpu.VMEM((nt + 24, LANES), F32)], name="conv7_bwd",
        compiler_params=_cparams(("arbitrary",), 10 * _nbytes((nt + 24, LANES), F32)),
    )(h, w_conv, dact_f, dact_b)


_RT = 4
_CP = CONF_K // 2


def _conf_pad_shape(gw, vertical):
    return (gw + 2 * _CP, gw, LANES) if vertical else (gw, gw + 32, LANES)


def _conf_fill(p_ref, val, gw, vertical):
    if vertical:
        p_ref[0:_CP] = jnp.zeros((_CP, gw, LANES), F32)
        p_ref[_CP + gw:2 * _CP + gw] = jnp.zeros((_CP, gw, LANES), F32)
        p_ref[_CP:_CP + gw] = val
    else:
        p_ref[:, 0:16, :] = jnp.zeros((gw, 16, LANES), F32)
        p_ref[:, 16 + gw:32 + gw, :] = jnp.zeros((gw, 16, LANES), F32)
        p_ref[:, 16:16 + gw, :] = val


def _conf_window(p_ref, r0, shift, gw, vertical):
    if vertical:
        return p_ref[pl.ds(r0 + _CP + shift, _RT), :, :]
    return p_ref[pl.ds(r0, _RT), pl.ds(16 + shift, gw), :]


def _conf_conv_fwd(h3, w_dw, *, gw, col0, half, vertical):
    cb_val = (col0 + (half if vertical else 0)) // LANES
    cb_gate = cb_val + 2 * half // LANES
    cb_w = (half if vertical else 0) // LANES

    def body(v_ref, g_ref, w_ref, o_ref, p_ref):
        _conf_fill(p_ref, v_ref[...] * jax.nn.sigmoid(g_ref[...]), gw, vertical)

        def step(t, carry):
            r0 = t * _RT
            acc = jnp.zeros((_RT, gw, LANES), F32)
            for kk in range(CONF_K):
                acc = acc + w_ref[kk:kk + 1, :] * _conf_window(p_ref, r0, kk - _CP, gw, vertical)
            o_ref[pl.ds(r0, _RT)] = acc
            return carry

        lax.fori_loop(0, gw // _RT, step, 0)

    blk = (gw, gw, LANES)
    return pl.pallas_call(
        body, grid=(half // LANES,),
        in_specs=[pl.BlockSpec(blk, lambda j: (0, 0, cb_val + j)), pl.BlockSpec(blk, lambda j: (0, 0, cb_gate + j)),
                  pl.BlockSpec((CONF_K, LANES), lambda j: (0, cb_w + j))],
        out_specs=pl.BlockSpec(blk, lambda j: (0, 0, j)),
        out_shape=jax.ShapeDtypeStruct((gw, gw, half), F32),
        scratch_shapes=[pltpu.VMEM(_conf_pad_shape(gw, vertical), F32)],
        name="conf_conv_fwd_v" if vertical else "conf_conv_fwd_h",
        compiler_params=_cparams(("arbitrary",), 8 * _nbytes(_conf_pad_shape(gw, vertical), F32)),
    )(h3, h3, w_dw)


def _conf_conv_bwd(h3, w_dw, dyc, *, gw, col0, half, vertical):
    cb_val = (col0 + (half if vertical else 0)) // LANES
    cb_gate = cb_val + 2 * half // LANES
    cb_w = (half if vertical else 0) // LANES

    def body(v_ref, g_ref, w_ref, d_ref, dv_ref, dg_ref, dw_ref, py_ref, pd_ref):
        _conf_fill(py_ref, v_ref[...] * jax.nn.sigmoid(g_ref[...]), gw, vertical)
        _conf_fill(pd_ref, d_ref[...], gw, vertical)

        def step(t, carry):
            r0 = t * _RT
            acc = jnp.zeros((_RT, gw, LANES), F32)
            for kk in range(CONF_K):
                acc = acc + w_ref[kk:kk + 1, :] * _conf_window(pd_ref, r0, _CP - kk, gw, vertical)
            val = v_ref[pl.ds(r0, _RT)]
            sg = jax.nn.sigmoid(g_ref[pl.ds(r0, _RT)])
            dv_ref[pl.ds(r0, _RT)] = (acc * sg).astype(dv_ref.dtype)
            dg_ref[pl.ds(r0, _RT)] = (acc * val * sg * (1.0 - sg)).astype(dg_ref.dtype)
            return carry

        lax.fori_loop(0, gw // _RT, step, 0)

        for kk in range(CONF_K):
            def wstep(t, acc, kk=kk):
                r0 = t * _RT
                prod = d_ref[pl.ds(r0, _RT)] * _conf_window(py_ref, r0, kk - _CP, gw, vertical)
                return acc + jnp.sum(prod, axis=0)

            tot = lax.fori_loop(0, gw // _RT, wstep, jnp.zeros((gw, LANES), F32))
            dw_ref[kk:kk + 1, :] = jnp.sum(tot, axis=0, keepdims=True)

    blk = (gw, gw, LANES)
    pshape = _conf_pad_shape(gw, vertical)
    return pl.pallas_call(
        body, grid=(half // LANES,),
        in_specs=[pl.BlockSpec(blk, lambda j: (0, 0, cb_val + j)), pl.BlockSpec(blk, lambda j: (0, 0, cb_gate + j)),
                  pl.BlockSpec((CONF_K, LANES), lambda j: (0, cb_w + j)), pl.BlockSpec(blk, lambda j: (0, 0, j))],
        out_specs=[pl.BlockSpec(blk, lambda j: (0, 0, j)), pl.BlockSpec(blk, lambda j: (0, 0, j)),
                   pl.BlockSpec((CONF_K, LANES), lambda j: (0, j))],
        out_shape=[jax.ShapeDtypeStruct((gw, gw, half), MXU_DTYPE), jax.ShapeDtypeStruct((gw, gw, half), MXU_DTYPE),
                   jax.ShapeDtypeStruct((CONF_K, half), F32)],
        scratch_shapes=[pltpu.VMEM(pshape, F32), pltpu.VMEM(pshape, F32)],
        name="conf_conv_bwd_v" if vertical else "conf_conv_bwd_h",
        compiler_params=_cparams(("arbitrary",), 12 * _nbytes(pshape, F32)),
    )(h3, h3, w_dw, dyc)


def _delta_chunk(qa, ka, va, graw, braw, alog, dtb, s, *, rev):
    c = qa.shape[0]

    def l2n(t):
        return t * lax.rsqrt(jnp.sum(t * t, axis=-1, keepdims=True) + 1e-6)

    q = l2n(qa) * (HD ** -0.5)
    k = l2n(ka)
    g = -jnp.exp(alog) * _softplus(graw + dtb)
    beta = jax.nn.sigmoid(braw)
    ri = lax.broadcasted_iota(jnp.int32, (c, c), 0)
    ci = lax.broadcasted_iota(jnp.int32, (c, c), 1)
    incl = (ci >= ri) if rev else (ci <= ri)
    strict = (ci > ri) if rev else (ci < ri)
    tmat = incl.astype(F32)
    tmat_t = ((ri >= ci) if rev else (ri <= ci)).astype(F32)
    gc_rows = _dot(tmat, jnp.broadcast_to(g, (c, c)), "nn", "hi")
    gc_cols = _dot(jnp.broadcast_to(g, (c, c)), tmat_t, "tn", "hi")
    gc_wide = _dot(tmat, jnp.broadcast_to(g, (c, HD)), "nn", "hi")
    gamma = jnp.where(incl, jnp.exp(jnp.where(incl, gc_rows - gc_cols, 0.0)), 0.0)
    kb = k * beta
    a_mat = jnp.where(strict, _dot(kb, k, "nt", "mxu") * gamma, 0.0)
    eye = (ri == ci).astype(F32)
    minv = eye - a_mat
    pw = _dot(a_mat, a_mat, "nn", "hi")
    steps = max(1, math.ceil(math.log2(c))) - 1
    for st in range(steps):
        minv = minv + _dot(minv, pw, "nn", "hi")
        if st < steps - 1:
            pw = _dot(pw, pw, "nn", "hi")
    eg = jnp.exp(gc_wide)
    u = _dot(minv, va * beta, "nn", "hi")
    w = _dot(minv, kb * eg, "nn", "hi")
    attn = _dot(q, k, "nt", "mxu") * gamma
    tot = jnp.sum(g, axis=0, keepdims=True)
    q_dec = q * eg
    k_dec = k * jnp.exp(tot - gc_wide)
    v_new = u - _dot(w, s, "nn", "mxu")
    o = _dot(q_dec, s, "nn", "mxu") + _dot(attn, v_new, "nn", "mxu")
    s_new = s * jnp.exp(tot) + _dot(k_dec, v_new, "tn", "mxu")
    return o, s_new


def _delta_chunk_of_step(step, nch, nlc, rev):
    return (nch - 1 - step) if rev else (step + nlc) % nch


def _delta_fwd(act, gates_hm, alog_b, dtb_b, *, n_lat, n_ctx, nh, hb, rev):
    nt = n_lat + n_ctx
    nch, nlc = nt // CHUNK, n_lat // CHUNK
    gcol = 2 if rev else 0
    chunk = functools.partial(_delta_chunk_of_step, nch=nch, nlc=nlc, rev=rev)

    def body(a_ref, g_ref, al_ref, dt_ref, o_ref, st_ref, s_ref):
        n = pl.program_id(1)

        @pl.when(n == 0)
        def _():
            s_ref[...] = jnp.zeros_like(s_ref)

        for hh in range(hb):
            s = s_ref[hh]
            st_ref[hh] = s
            gts = g_ref[hh]
            o, s_new = _delta_chunk(a_ref[0, hh], a_ref[1, hh], a_ref[2, hh], gts[:, gcol:gcol + 1],
                                    gts[:, gcol + 1:gcol + 2], al_ref[hh, 0:1, 0:1], dt_ref[hh, 0:1, 0:1], s, rev=rev)
            o_ref[:, hh * HD:(hh + 1) * HD] = o
            s_ref[hh] = s_new

    par = pl.BlockSpec((hb, 8, LANES), lambda h, n: (h, 0, 0))
    return pl.pallas_call(
        body, grid=(nh // hb, nch),
        in_specs=[pl.BlockSpec((3, hb, CHUNK, HD), lambda h, n: (0, h, chunk(n), 0)),
                  pl.BlockSpec((hb, CHUNK, 4), lambda h, n: (h, chunk(n), 0)), par, par],
        out_specs=[pl.BlockSpec((CHUNK, hb * HD), lambda h, n: (chunk(n), h)),
                   pl.BlockSpec((hb, None, HD, HD), lambda h, n: (h, n, 0, 0))],
        out_shape=[jax.ShapeDtypeStruct((nt, nh * HD), F32), jax.ShapeDtypeStruct((nh, nch, HD, HD), F32)],
        scratch_shapes=[pltpu.VMEM((hb, HD, HD), F32)], name="delta_fwd_rev" if rev else "delta_fwd",
        compiler_params=_cparams(("arbitrary", "arbitrary"), 0),
    )(act, gates_hm, alog_b, dtb_b)


def _delta_bwd(act, gates_hm, alog_b, dtb_b, states, do, *, n_lat, n_ctx, nh, hb, rev):
    nt = n_lat + n_ctx
    nch, nlc = nt // CHUNK, n_lat // CHUNK
    gcol = 2 if rev else 0

    def chunk(m):
        return _delta_chunk_of_step(nch - 1 - m, nch, nlc, rev)

    def body(a_ref, g_ref, al_ref, dt_ref, st_ref, do_ref, da_ref, dg_ref, dal_ref, ddt_ref, ds_ref):
        m = pl.program_id(1)

        @pl.when(m == 0)
        def _():
            ds_ref[...] = jnp.zeros_like(ds_ref)
            dal_ref[...] = jnp.zeros_like(dal_ref)
            ddt_ref[...] = jnp.zeros_like(ddt_ref)

        is_lat = chunk(m) < nlc
        for hh in range(hb):
            gts = g_ref[hh]
            f = functools.partial(_delta_chunk, rev=rev)
            _, vjp = jax.vjp(f, a_ref[0, hh], a_ref[1, hh], a_ref[2, hh], gts[:, gcol:gcol + 1], gts[:, gcol + 1:gcol + 2],
                             al_ref[hh, 0:1, 0:1], dt_ref[hh, 0:1, 0:1], st_ref[hh])
            do_h = jnp.where(is_lat, do_ref[:, hh * HD:(hh + 1) * HD], 0.0)
            dq, dk, dv, dgr, dbr, dal, ddt, ds = vjp((do_h, ds_ref[hh]))
            da_ref[0, hh] = dq
            da_ref[1, hh] = dk
            da_ref[2, hh] = dv
            lane = lax.broadcasted_iota(jnp.int32, (CHUNK, 4), 1)
            dg_ref[hh] = jnp.where(lane == 0, dgr, jnp.where(lane == 1, dbr, 0.0))
            dal_ref[hh] += jnp.broadcast_to(dal, (8, LANES))
            ddt_ref[hh] += jnp.broadcast_to(ddt, (8, LANES))
            ds_ref[hh] = ds

    par = pl.BlockSpec((hb, 8, LANES), lambda h, m: (h, 0, 0))
    return pl.pallas_call(
        body, grid=(nh // hb, nch),
        in_specs=[pl.BlockSpec((3, hb, CHUNK, HD), lambda h, m: (0, h, chunk(m), 0)),
                  pl.BlockSpec((hb, CHUNK, 4), lambda h, m: (h, chunk(m), 0)), par, par,
                  pl.BlockSpec((hb, None, HD, HD), lambda h, m: (h, nch - 1 - m, 0, 0)),
                  pl.BlockSpec((CHUNK, hb * HD), lambda h, m: (jnp.minimum(chunk(m), nlc - 1), h))],
        out_specs=[pl.BlockSpec((3, hb, CHUNK, HD), lambda h, m: (0, h, chunk(m), 0)),
                   pl.BlockSpec((hb, CHUNK, 4), lambda h, m: (h, chunk(m), 0)), par, par],
        out_shape=[jax.ShapeDtypeStruct((3, nh, nt, HD), F32), jax.ShapeDtypeStruct((nh, nt, 4), F32),
                   jax.ShapeDtypeStruct((nh, 8, LANES), F32), jax.ShapeDtypeStruct((nh, 8, LANES), F32)],
        scratch_shapes=[pltpu.VMEM((hb, HD, HD), F32)], name="delta_bwd_rev" if rev else "delta_bwd",
        compiler_params=_cparams(("arbitrary", "arbitrary"), 0),
    )(act, gates_hm, alog_b, dtb_b, states, do)


def _my_pos():
    return lax.axis_index("x"), lax.axis_index("y"), lax.axis_index("c")


def _flip(v, d):
    return 1 - v if d else v


def _bcast8(block, name):
    r, w = block.shape
    assert r % 8 == 0 and w % LANES == 0

    def body(x_ref, o_ref, send_sems, recv_sems, local_sem):
        x, y, c = _my_pos()

        def rows(px, py, pc):
            return o_ref.at[pl.ds((4 * px + 2 * py + pc) * r, r), :]

        mine = pltpu.make_async_copy(x_ref, rows(x, y, c), local_sem)
        mine.start()
        copies = []
        for kk in range(1, 8):
            dx, dy, dc = (kk >> 2) & 1, (kk >> 1) & 1, kk & 1
            peer = (_flip(x, dx), _flip(y, dy), _flip(c, dc))
            cp = pltpu.make_async_remote_copy(src_ref=x_ref, dst_ref=rows(x, y, c), send_sem=send_sems.at[kk - 1],
                                              recv_sem=recv_sems.at[kk - 1], device_id=peer, device_id_type=pl.DeviceIdType.MESH)
            cp.start()
            copies.append((cp, peer))
        for kk, (cp, peer) in enumerate(copies):
            pltpu.make_async_remote_copy(src_ref=x_ref, dst_ref=rows(*peer), send_sem=send_sems.at[kk], recv_sem=recv_sems.at[kk],
                                         device_id=peer, device_id_type=pl.DeviceIdType.MESH).wait_recv()
        for cp, _ in copies:
            cp.wait_send()
        mine.wait()

    return pl.pallas_call(
        body, out_shape=jax.ShapeDtypeStruct((8 * r, w), block.dtype),
        in_specs=[pl.BlockSpec(memory_space=pltpu.VMEM)], out_specs=pl.BlockSpec(memory_space=pltpu.VMEM),
        scratch_shapes=[pltpu.SemaphoreType.DMA((7,)), pltpu.SemaphoreType.DMA((7,)), pltpu.SemaphoreType.DMA], name=name,
        compiler_params=pltpu.CompilerParams(vmem_limit_bytes=int(max(32 * 1024 * 1024, 12 * _nbytes((r, w), block.dtype)))),
    )(block)


_CHIP_PEERS = ((1, 0), (0, 1), (1, 1))


def _chip_gather(shards, name):
    n = len(shards)

    def body(*refs):
        ins, outs = refs[:n], refs[n:2 * n]
        send_sems, recv_sems, local_sems = refs[2 * n:]
        x, y, c = _my_pos()
        started = []
        for a in range(n):
            loc = pltpu.make_async_copy(ins[a], outs[a].at[2 * x + y], local_sems.at[a])
            loc.start()
            started.append(loc)
        sends = []
        for a in range(n):
            for kk, (dx, dy) in enumerate(_CHIP_PEERS):
                px, py = _flip(x, dx), _flip(y, dy)
                cp = pltpu.make_async_remote_copy(src_ref=ins[a], dst_ref=outs[a].at[2 * x + y], send_sem=send_sems.at[3 * a + kk],
                                                  recv_sem=recv_sems.at[3 * a + kk], device_id=(px, py, c),
                                                  device_id_type=pl.DeviceIdType.MESH)
                cp.start()
                sends.append(cp)
        for a in range(n):
            for kk, (dx, dy) in enumerate(_CHIP_PEERS):
                px, py = _flip(x, dx), _flip(y, dy)
                pltpu.make_async_remote_copy(src_ref=ins[a], dst_ref=outs[a].at[2 * px + py], send_sem=send_sems.at[3 * a + kk],
                                             recv_sem=recv_sems.at[3 * a + kk], device_id=(px, py, c),
                                             device_id_type=pl.DeviceIdType.MESH).wait_recv()
        for cp in sends:
            cp.wait_send()
        for loc in started:
            loc.wait()

    return pl.pallas_call(
        body, out_shape=[jax.ShapeDtypeStruct((4,) + s.shape, s.dtype) for s in shards],
        in_specs=[ANY] * n, out_specs=[ANY] * n,
        scratch_shapes=[pltpu.SemaphoreType.DMA((3 * n,)), pltpu.SemaphoreType.DMA((3 * n,)), pltpu.SemaphoreType.DMA((n,))],
        name=name,
    )(*shards)


def _chip_scatter(parts, name):
    n = len(parts)

    def body(*refs):
        ins, outs = refs[:n], refs[n:2 * n]
        send_sems, recv_sems, local_sems = refs[2 * n:]
        x, y, c = _my_pos()
        me = 2 * x + y
        started = []
        for a in range(n):
            loc = pltpu.make_async_copy(ins[a].at[me], outs[a].at[me], local_sems.at[a])
            loc.start()
            started.append(loc)
        sends = []
        for a in range(n):
            for kk, (dx, dy) in enumerate(_CHIP_PEERS):
                px, py = _flip(x, dx), _flip(y, dy)
                cp = pltpu.make_async_remote_copy(src_ref=ins[a].at[2 * px + py], dst_ref=outs[a].at[me], send_sem=send_sems.at[3 * a + kk],
                                                  recv_sem=recv_sems.at[3 * a + kk], device_id=(px, py, c),
                                                  device_id_type=pl.DeviceIdType.MESH)
                cp.start()
                sends.append(cp)
        for a in range(n):
            for kk, (dx, dy) in enumerate(_CHIP_PEERS):
                px, py = _flip(x, dx), _flip(y, dy)
                pltpu.make_async_remote_copy(src_ref=ins[a].at[me], dst_ref=outs[a].at[2 * px + py], send_sem=send_sems.at[3 * a + kk],
                                             recv_sem=recv_sems.at[3 * a + kk], device_id=(px, py, c),
                                             device_id_type=pl.DeviceIdType.MESH).wait_recv()
        for cp in sends:
            cp.wait_send()
        for loc in started:
            loc.wait()

    return pl.pallas_call(
        body, out_shape=[jax.ShapeDtypeStruct(p.shape, p.dtype) for p in parts],
        in_specs=[ANY] * n, out_specs=[ANY] * n,
        scratch_shapes=[pltpu.SemaphoreType.DMA((3 * n,)), pltpu.SemaphoreType.DMA((3 * n,)), pltpu.SemaphoreType.DMA((n,))],
        name=name,
    )(*parts)


def _pair_send_halves(grads, name):
    n = len(grads)

    def body(*refs):
        ins, outs = refs[:n], refs[n:2 * n]
        send_sems, recv_sems = refs[2 * n:]
        x, y, c = _my_pos()
        sib = (x, y, 1 - c)
        sends = []
        for a in range(n):
            rh = ins[a].shape[1] // 2
            cp = pltpu.make_async_remote_copy(src_ref=ins[a].at[:, pl.ds((1 - c) * rh, rh), :], dst_ref=outs[a], send_sem=send_sems.at[a],
                                              recv_sem=recv_sems.at[a], device_id=sib, device_id_type=pl.DeviceIdType.MESH)
            cp.start()
            sends.append(cp)
        for cp in sends:
            cp.wait_recv()
        for cp in sends:
            cp.wait_send()

    return pl.pallas_call(
        body, out_shape=[jax.ShapeDtypeStruct((4, g.shape[1] // 2, g.shape[2]), g.dtype) for g in grads],
        in_specs=[ANY] * n, out_specs=[ANY] * n,
        scratch_shapes=[pltpu.SemaphoreType.DMA((n,)), pltpu.SemaphoreType.DMA((n,))], name=name,
    )(*grads)


def _pair_join_halves(halves, name):
    n = len(halves)

    def body(*refs):
        ins, outs = refs[:n], refs[n:2 * n]
        send_sems, recv_sems, local_sems = refs[2 * n:]
        x, y, c = _my_pos()
        sib = (x, y, 1 - c)
        locs, sends = [], []
        for a in range(n):
            loc = pltpu.make_async_copy(ins[a], outs[a].at[c], local_sems.at[a])
            loc.start()
            locs.append(loc)
            cp = pltpu.make_async_remote_copy(src_ref=ins[a], dst_ref=outs[a].at[c], send_sem=send_sems.at[a], recv_sem=recv_sems.at[a],
                                              device_id=sib, device_id_type=pl.DeviceIdType.MESH)
            cp.start()
            sends.append(cp)
        for a in range(n):
            pltpu.make_async_remote_copy(src_ref=ins[a], dst_ref=outs[a].at[1 - c], send_sem=send_sems.at[a], recv_sem=recv_sems.at[a],
                                         device_id=sib, device_id_type=pl.DeviceIdType.MESH).wait_recv()
        for cp in sends:
            cp.wait_send()
        for loc in locs:
            loc.wait()

    return pl.pallas_call(
        body, out_shape=[jax.ShapeDtypeStruct((2,) + h.shape, h.dtype) for h in halves],
        in_specs=[ANY] * n, out_specs=[ANY] * n,
        scratch_shapes=[pltpu.SemaphoreType.DMA((n,)), pltpu.SemaphoreType.DMA((n,)), pltpu.SemaphoreType.DMA((n,))], name=name,
    )(*halves)


def _add_my_half(g, recv, c_idx, name):
    _, r, cc = g.shape
    rh = r // 2
    tc = _pick(cc, 512)

    def body(c_ref, g_ref, r_ref, o_ref):
        o_ref[...] = g_ref[...] + r_ref[...]

    grid_spec = pltpu.PrefetchScalarGridSpec(
        num_scalar_prefetch=1, grid=(4, cc // tc),
        in_specs=[pl.BlockSpec((None, rh, tc), lambda s, j, c_ref: (s, c_ref[0], j)),
                  pl.BlockSpec((None, rh, tc), lambda s, j, c_ref: (s, 0, j))],
        out_specs=pl.BlockSpec((None, rh, tc), lambda s, j, c_ref: (s, 0, j)))
    return pl.pallas_call(body, grid_spec=grid_spec, out_shape=jax.ShapeDtypeStruct((4, rh, cc), F32), name=name,
                          compiler_params=_cparams(("arbitrary", "arbitrary"), 6 * _nbytes((rh, tc), F32)))(c_idx, g, recv)


def _sum_slots(parts, name):
    _, rh, cc = parts.shape
    tc = _pick(cc, 512)

    def body(p_ref, o_ref):
        o_ref[...] = ((p_ref[0] + p_ref[1]) + p_ref[2]) + p_ref[3]

    return pl.pallas_call(
        body, grid=(cc // tc,), in_specs=[pl.BlockSpec((4, rh, tc), lambda j: (0, 0, j))],
        out_specs=pl.BlockSpec((rh, tc), lambda j: (0, j)), out_shape=jax.ShapeDtypeStruct((rh, cc), F32), name=name,
        compiler_params=_cparams(("arbitrary",), 10 * _nbytes((rh, tc), F32)))(parts)


def _adamw_math(w, g, m, v):
    m = ADAM_B1 * m + (1.0 - ADAM_B1) * g
    v = ADAM_B2 * v + (1.0 - ADAM_B2) * (g * g)
    m_hat = m / (1.0 - ADAM_B1 ** ADAM_STEP)
    v_hat = v / (1.0 - ADAM_B2 ** ADAM_STEP)
    delta = -ADAM_LR * (m_hat / (jnp.sqrt(v_hat) + ADAM_EPS) + ADAM_WD * w)
    return delta, m, v


def _adamw(w, g, m, v, name):
    r, cc = w.shape
    tr = _pick(r, max(8, (512 * 1024) // max(cc, 1)), 8)

    def body(w_ref, g_ref, m_ref, v_ref, d_ref, nm_ref, nv_ref):
        d, nm, nv = _adamw_math(w_ref[...], g_ref[...], m_ref[...], v_ref[...])
        d_ref[...] = d
        nm_ref[...] = nm
        nv_ref[...] = nv

    spec = pl.BlockSpec((tr, cc), lambda i: (i, 0))
    return pl.pallas_call(
        body, grid=(r // tr,), in_specs=[spec] * 4, out_specs=[spec] * 3,
        out_shape=[jax.ShapeDtypeStruct((r, cc), F32)] * 3, name=name,
        compiler_params=_cparams(("arbitrary",), 16 * _nbytes((tr, cc), F32)))(w, g, m, v)


def _sum8(allv, name):
    r8, w = allv.shape
    r = r8 // 8

    def body(a_ref, o_ref):
        acc = a_ref[0:r, :]
        for d in range(1, 8):
            acc = acc + a_ref[d * r:(d + 1) * r, :]
        o_ref[...] = acc

    return pl.pallas_call(body, out_shape=jax.ShapeDtypeStruct((r, w), F32), name=name,
                          compiler_params=_cparams((), 12 * _nbytes((r8, w), F32)))(allv)


def _dmod_prep(alld, name):
    _, w = alld.shape
    tw = _pick(w, 2048)

    def body(a_ref, d_ref, b_ref):
        ctx = a_ref[1:2, :]
        tot = a_ref[0:1, :]
        d_ref[0:1, :] = tot
        for d in range(1, 8):
            ctx = ctx + a_ref[8 * d + 1:8 * d + 2, :]
            tot = tot + a_ref[8 * d:8 * d + 1, :]
            d_ref[d:d + 1, :] = a_ref[8 * d:8 * d + 1, :]
        d_ref[8:9, :] = ctx
        d_ref[9:16, :] = jnp.zeros((7, tw), F32)
        b_ref[...] = tot + ctx

    return pl.pallas_call(
        body, grid=(w // tw,), in_specs=[pl.BlockSpec((64, tw), lambda j: (0, j))],
        out_specs=[pl.BlockSpec((16, tw), lambda j: (0, j)), pl.BlockSpec((1, tw), lambda j: (0, j))],
        out_shape=[jax.ShapeDtypeStruct((16, w), F32), jax.ShapeDtypeStruct((1, w), F32)], name=name,
        compiler_params=_cparams(("arbitrary",), 0))(alld)


def _pack(arrs, rows=8):
    flat = jnp.concatenate([a.reshape(-1).astype(F32) for a in arrs])
    per = rows * LANES
    n = flat.shape[0]
    padded = -(-n // per) * per
    flat = jnp.pad(flat, (0, padded - n))
    return flat.reshape(rows, padded // rows)


def _unpack(slab, shapes):
    flat = slab.reshape(-1)
    out, off = [], 0
    for s in shapes:
        n = math.prod(s)
        out.append(flat[off:off + n].reshape(s))
        off += n
    return out


def kernel(x, c, ctx, c_ctx, ln_in_g, ln_in_b, w_mod, b_mod, w_in, w_qkv_conv, a_log_f, dt_bias_f, a_log_b, dt_bias_b, dn_norm_g, conf_dw_w, conf_dw_b, conf_ln_g, conf_ln_b, w_out, ln1_g, ln1_b, w_mlp1, b_mlp1, w_mlp2, b_mlp2, ln2_g, ln2_b, loss_target, m_c_ctx, m_ln_in_g, m_ln_in_b, m_w_mod, m_b_mod, m_w_in, m_w_qkv_conv, m_a_log_f, m_dt_bias_f, m_a_log_b, m_dt_bias_b, m_dn_norm_g, m_conf_dw_w, m_conf_dw_b, m_conf_ln_g, m_conf_ln_b, m_w_out, m_ln1_g, m_ln1_b, m_w_mlp1, m_b_mlp1, m_w_mlp2, m_b_mlp2, m_ln2_g, m_ln2_b, v_c_ctx, v_ln_in_g, v_ln_in_b, v_w_mod, v_b_mod, v_w_in, v_w_qkv_conv, v_a_log_f, v_dt_bias_f, v_a_log_b, v_dt_bias_b, v_dn_norm_g, v_conf_dw_w, v_conf_dw_b, v_conf_ln_g, v_conf_ln_b, v_w_out, v_ln1_g, v_ln1_b, v_w_mlp1, v_b_mlp1, v_w_mlp2, v_b_mlp2, v_ln2_g, v_ln2_b):
    n_lat, d = x.shape[1], x.shape[2]
    n_ctx = ctx.shape[1]
    nt = n_lat + n_ctx
    dn = d // 2
    nh = dn // HD
    conf = d - dn
    half = conf // 2
    gw = math.isqrt(n_lat)
    ff = 4 * w_mlp1.shape[2]
    fl = w_mlp1.shape[2]
    in_cols = 4 * w_in.shape[2]
    r_in = w_in.shape[2]
    n_gate = 4 * nh
    gb = max(LANES, d // 8)
    g0 = 4 * dn + 2 * conf
    in_pad = g0 + gb
    ml = w_mod.shape[2]
    assert gw * gw == n_lat and nt % gw == 0 and in_cols == 4 * dn + n_gate + 2 * conf and nh * HD == dn
    hb = 2 if nh % 2 == 0 else 1

    mx, my, mc = _my_pos()
    chip = 2 * mx + my
    dev = 4 * mx + 2 * my + mc
    c_idx = jnp.reshape(mc, (1,)).astype(jnp.int32)

    x2, ctx2, tgt2 = x[0], ctx[0], loss_target[0]
    row = lambda a: a.reshape(1, -1)

    e0_shapes = [(d,), w_qkv_conv.shape[1:], conf_dw_w.shape[1:]]
    e0 = _bcast8(_pack([c[0], w_qkv_conv[0], conf_dw_w[0]]), "bcast_inputs")
    e0 = e0.reshape(8, -1)
    per_dev = [_unpack(e0[dd], e0_shapes) for dd in range(8)]
    all_c = jnp.stack([p[0] for p in per_dev])
    w_conv = jnp.concatenate([per_dev[2 * j][1] for j in range(4)], axis=1)
    w_dw = jnp.concatenate([per_dev[2 * j][2] for j in range(4)], axis=1)

    s16 = jnp.concatenate([all_c, c_ctx[None, :], jnp.zeros((7, d), F32)], axis=0)
    b_mod_loc = lax.dynamic_slice(b_mod, (0, chip * ml), (1, ml))
    to_mxu = lambda t: t.astype(MXU_DTYPE)
    modblk = _matmul(s16, w_mod[0], form="nn", m=16, n=ml, k=d, tm=16, tn=_pick(ml, 1024), tk=_pick(d, 1024),
                     out_dtype=F32, name="mod_fwd", bias=b_mod_loc, a_fn=lambda t: to_mxu(_silu(t)), b_fn=to_mxu)
    allmod = _bcast8(modblk, "bcast_mod").reshape(8, 16, ml)[0::2]
    mod_mine = lax.dynamic_index_in_dim(allmod, dev, axis=1, keepdims=False).reshape(1, 4 * ml)
    mod_ctx = allmod[:, 8, :].reshape(1, 4 * ml)
    sh_a, sc_a, g_a, sh_m, sc_m, g_m = [mod_mine[:, i * d:(i + 1) * d] for i in range(6)]
    csh_a, csc_a = mod_ctx[:, 0:d], mod_ctx[:, d:2 * d]

    win_t_loc = jnp.transpose(w_in[0]).astype(MXU_DTYPE)
    gw_in, gw_out, gw_1, gw_2 = _chip_gather(
        [win_t_loc, w_out[0].astype(MXU_DTYPE), w_mlp1[0].astype(MXU_DTYPE), w_mlp2[0].astype(MXU_DTYPE)], "gather_weights")
    win_t = gw_in.reshape(in_cols, d)
    win_t = jnp.concatenate([win_t[:4 * dn], win_t[4 * dn + n_gate:], win_t[4 * dn:4 * dn + n_gate],
                             jnp.zeros((gb - n_gate, d), MXU_DTYPE)], axis=0)
    wout_full = gw_out.reshape(d, d)
    w2_full = gw_2.reshape(ff, d)
    w1_sh = gw_1

    tmr = _pick(n_lat, 128, 8)
    tmc = _pick(n_ctx, 128, 8)
    ln_in = [row(ln_in_g), row(ln_in_b)]

    def stage_a(rv, pv):
        x0 = _ln(rv[0], pv[0], pv[1])
        return [x0 * (1.0 + pv[3]) + pv[2]], []

    xm_lat, = _rowwise(stage_a, [(x2, d, 0, 0)], ln_in + [sh_a, sc_a], [(d, MXU_DTYPE)], [], n_rows=n_lat, tm=tmr, name="ln_mod_lat")
    xm_ctx, = _rowwise(stage_a, [(ctx2, d, 0, 0)], ln_in + [csh_a, csc_a], [(d, MXU_DTYPE)], [], n_rows=n_ctx, tm=tmc, name="ln_mod_ctx")
    xm = jnp.concatenate([xm_lat, xm_ctx], axis=0)

    tm_nt = _pick(nt, 1280, 16)
    tn_in = _pick(in_pad, 1280)
    tk_d = _pick(d, 1024)
    h = _matmul(xm, win_t, form="nt", m=nt, n=in_pad, k=d, tm=tm_nt, tn=tn_in, tk=tk_d, out_dtype=F32, name="in_proj")

    act = _conv7_fwd(h, w_conv, n_lat=n_lat, n_ctx=n_ctx, nh=nh)
    gates_hm = jnp.transpose(h[:, g0:g0 + n_gate].reshape(nt, 4, nh), (2, 0, 1))
    bc = lambda a: jnp.broadcast_to(a.reshape(nh, 1, 1), (nh, 8, LANES))
    dk = dict(n_lat=n_lat, n_ctx=n_ctx, nh=nh, hb=hb)
    o_f, st_f = _delta_fwd(act, gates_hm, bc(a_log_f), bc(dt_bias_f), rev=False, **dk)
    o_b, st_b = _delta_fwd(act, gates_hm, bc(a_log_b), bc(dt_bias_b), rev=True, **dk)

    h3 = h.reshape(nt // gw, gw, in_pad)
    ck = dict(gw=gw, col0=4 * dn, half=half)
    yh = _conf_conv_fwd(h3, w_dw, vertical=False, **ck).reshape(n_lat, half)
    yv = _conf_conv_fwd(h3, w_dw, vertical=True, **ck).reshape(n_lat, half)

    def mix_fn(o_heads, z_heads, ya, yb, dng, ba, bb, ga, gbb, la, lb):
        outs = []
        for o, z in zip(o_heads, z_heads):
            ms = jnp.mean(o * o, axis=-1, keepdims=True)
            outs.append(o * lax.rsqrt(ms + 1e-6) * dng * _silu(z))
        ya, yb = ya + ba, yb + bb
        mu = (jnp.sum(ya, axis=-1, keepdims=True) + jnp.sum(yb, axis=-1, keepdims=True)) / conf
        ya, yb = ya - mu, yb - mu
        var = (jnp.sum(ya * ya, axis=-1, keepdims=True) + jnp.sum(yb * yb, axis=-1, keepdims=True)) / conf
        rs = lax.rsqrt(var + LN_EPS)
        ca, cb = ya * rs * ga + la, yb * rs * gbb + lb
        return outs, _silu(ca), _silu(cb)

    def heads(t):
        return [t[:, i * HD:(i + 1) * HD] for i in range(nh)]

    def halves(p):
        return p[:, :half], p[:, half:]

    mix_params = [row(dn_norm_g), row(conf_dw_b), row(conf_ln_g), row(conf_ln_b)]

    def mix_args(rv, pv):
        (ba, bb), (ga, gbb), (la, lb) = halves(pv[1]), halves(pv[2]), halves(pv[3])
        return (heads(rv[0] + rv[1]), heads(rv[2]), rv[3], rv[4], pv[0], ba, bb, ga, gbb, la, lb)

    def stage_e(rv, pv):
        outs, ca, cb = mix_fn(*mix_args(rv, pv))
        return [jnp.concatenate(outs + [ca, cb], axis=-1)], []

    mix_rows = [(o_f, dn, 0, 0), (o_b, dn, 0, 0), (h, dn, 3, 0), (yh, half, 0, 0), (yv, half, 0, 0)]
    mix, = _rowwise(stage_e, mix_rows, mix_params, [(d, MXU_DTYPE)], [], n_rows=n_lat, tm=tmr, name="mixer_fwd")

    tm_l = _pick(n_lat, 1024, 16)
    tn_d = _pick(d, 1024)
    y1 = _matmul(mix, wout_full, form="nn", m=n_lat, n=d, k=d, tm=tm_l, tn=tn_d, tk=tk_d, out_dtype=F32, name="out_proj")

    def res1(x0, y1v, ga, l1g, l1b, shm, scm):
        x1 = _ln(ALPHA * x0 + ga * y1v, l1g, l1b)
        return x1, x1 * (1.0 + scm) + shm

    f_params = ln_in + [g_a, row(ln1_g), row(ln1_b), sh_m, sc_m]

    def stage_f(rv, pv):
        x1, u = res1(_ln(rv[0], pv[0], pv[1]), rv[1], *pv[2:])
        return [x1, u], []

    x1, u = _rowwise(stage_f, [(x2, d, 0, 0), (y1, d, 0, 0)], f_params, [(d, F32), (d, MXU_DTYPE)], [], n_rows=n_lat, tm=tmr, name="res1_fwd")

    tn_f = _pick(fl, 1024)
    relu_h = _matmul(u, w1_sh, form="nn", m=n_lat, n=ff, k=d, tm=tm_l, tn=tn_f, tk=tk_d, out_dtype=MXU_DTYPE, name="mlp1",
                     bias=row(b_mlp1), epi=lambda r: jnp.maximum(r, 0.0), b_split=fl)
    sq = lambda t: (t.astype(F32) * t.astype(F32)).astype(MXU_DTYPE)
    tk_f = _pick(ff, 1024)
    y2 = _matmul(relu_h, w2_full, form="nn", m=n_lat, n=d, k=ff, tm=tm_l, tn=tn_d, tk=tk_f, out_dtype=F32, name="mlp2",
                 bias=row(b_mlp2), a_fn=sq)

    def loss_fn(x1v, y2v, gm, l2g, l2b, tgt):
        x2v = _ln(ALPHA * x1v + gm * y2v, l2g, l2b)
        return (0.5 / d) * jnp.sum(jnp.square(x2v - tgt))

    def stage_g(rv, pv):
        loss, grads = jax.value_and_grad(loss_fn, argnums=(0, 1, 2, 3, 4))(rv[0], rv[1], pv[0], pv[1], pv[2], rv[2])
        dx1, dy2r, dgm, dl2g, dl2b = grads
        dy2 = dy2r
        return [dx1, dy2], [jnp.reshape(loss, (1, 1)), dgm, dl2g, dl2b, jnp.sum(dy2, axis=0, keepdims=True)]

    dx1p, dy2, loss_acc, d_g_m, d_ln2_g, d_ln2_b, d_b_mlp2 = _rowwise(
        stage_g, [(x1, d, 0, 0), (y2, d, 0, 0), (tgt2, d, 0, 0)], [g_m, row(ln2_g), row(ln2_b)],
        [(d, F32), (d, MXU_DTYPE)], [LANES, d, d, d, d], n_rows=n_lat, tm=tmr, name="loss_res2_bwd")

    dhid, d_b_mlp1 = _matmul(dy2, w2_full, form="nt", m=n_lat, n=ff, k=d, tm=tm_l, tn=tn_f, tk=tk_d, out_dtype=MXU_DTYPE, name="mlp2_bwd_x",
                             epi=lambda r, rh: r * (2.0 * rh.astype(F32)), epi_in=relu_h, colsum=True)
    tk_l = _pick(n_lat, 1024, 16)
    d_w2 = _matmul(relu_h, dy2, form="tn", m=ff, n=d, k=n_lat, tm=tn_f, tn=tn_d, tk=tk_l, out_dtype=F32, name="mlp2_bwd_w", a_fn=sq)
    du = _matmul(dhid, w1_sh, form="nt", m=n_lat, n=d, k=ff, tm=tm_l, tn=tn_d, tk=tn_f, out_dtype=F32, name="mlp1_bwd_x", b_split=fl)
    d_w1 = _matmul(u, dhid, form="tn", m=d, n=ff, k=n_lat, tm=tn_d, tn=tn_f, tk=tk_l, out_dtype=F32, name="mlp1_bwd_w", o_split=fl)

    def stage_h(rv, pv):
        x0 = _ln(rv[0], pv[0], pv[1])
        _, vjp = jax.vjp(res1, x0, rv[1], *pv[2:])
        dx0, dy1r, dga, dl1g, dl1b, dshm, dscm = vjp((rv[3], rv[2]))
        return [dx0, dy1r], [dga, dl1g, dl1b, dshm, dscm]

    dx0p, dy1, d_g_a, d_ln1_g, d_ln1_b, d_sh_m, d_sc_m = _rowwise(
        stage_h, [(x2, d, 0, 0), (y1, d, 0, 0), (du, d, 0, 0), (dx1p, d, 0, 0)], f_params,
        [(d, F32), (d, MXU_DTYPE)], [d, d, d, d, d], n_rows=n_lat, tm=tmr, name="res1_bwd")

    dmix = _matmul(dy1, wout_full, form="nt", m=n_lat, n=d, k=d, tm=tm_l, tn=tn_d, tk=tk_d, out_dtype=F32, name="out_proj_bwd_x")
    d_wout = _matmul(mix, dy1, form="tn", m=d, n=d, k=n_lat, tm=tn_d, tn=tn_d, tk=tk_l, out_dtype=F32, name="out_proj_bwd_w")

    def stage_i(rv, pv):
        args = mix_args(rv, pv)
        _, vjp = jax.vjp(mix_fn, *args)
        dm = rv[5]
        d_outs = [dm[:, i * HD:(i + 1) * HD] for i in range(nh)]
        dca, dcb = dm[:, dn:dn + half], dm[:, dn + half:]
        do_h, dz_h, dya, dyb, ddng, dba, dbb, dga, dgbb, dla, dlb = vjp((d_outs, dca, dcb))
        cat = lambda *p: jnp.concatenate(p, axis=-1)
        return ([cat(*do_h), cat(*dz_h), dya, dyb], [ddng, cat(dba, dbb), cat(dga, dgbb), cat(dla, dlb)])

    do, dz, dyh, dyv, d_dn_norm_g, d_conf_dw_b, d_conf_ln_g, d_conf_ln_b = _rowwise(
        stage_i, mix_rows + [(dmix, d, 0, 0)], mix_params, [(dn, F32), (dn, MXU_DTYPE), (half, F32), (half, F32)],
        [HD, conf, conf, conf], n_rows=n_lat, tm=tmr, name="mixer_bwd")

    dval_h, dgate_h, d_wdw_h = _conf_conv_bwd(h3, w_dw, dyh.reshape(gw, gw, half), vertical=False, **ck)
    dval_v, dgate_v, d_wdw_v = _conf_conv_bwd(h3, w_dw, dyv.reshape(gw, gw, half), vertical=True, **ck)
    dact_f, dgt_f, d_alog_f, d_dt_f = _delta_bwd(act, gates_hm, bc(a_log_f), bc(dt_bias_f), st_f, do, rev=False, **dk)
    dact_b, dgt_b, d_alog_b, d_dt_b = _delta_bwd(act, gates_hm, bc(a_log_b), bc(dt_bias_b), st_b, do, rev=True, **dk)
    dh_qkv, d_wconv = _conv7_bwd(h, w_conv, dact_f, dact_b, n_lat=n_lat, n_ctx=n_ctx, nh=nh)

    dgates = jnp.stack([dgt_f[..., 0], dgt_f[..., 1], dgt_b[..., 0], dgt_b[..., 1]], axis=0)
    dgates = jnp.transpose(dgates, (2, 0, 1)).reshape(nt, n_gate)
    dgates = jnp.pad(dgates, ((0, 0), (0, gb - n_gate))).astype(MXU_DTYPE)
    zrows = lambda t: jnp.pad(t, ((0, n_ctx), (0, 0)))
    dh = jnp.concatenate([dh_qkv, zrows(dz), zrows(dval_h.reshape(n_lat, half)), zrows(dval_v.reshape(n_lat, half)),
                          zrows(dgate_h.reshape(n_lat, half)), zrows(dgate_v.reshape(n_lat, half)), dgates], axis=1)

    tk_in = tn_in
    dxm = _matmul(dh, win_t, form="nn", m=nt, n=d, k=in_pad, tm=tm_nt, tn=tn_d, tk=tk_in, out_dtype=F32, name="in_proj_bwd_x")
    tk_nt = _pick(nt, 1280, 16)
    d_win_t = _matmul(dh, xm, form="tn", m=in_pad, n=d, k=nt, tm=tn_in, tn=tn_d, tk=tk_nt, out_dtype=F32, name="in_proj_bwd_w")
    d_win_t = jnp.concatenate([d_win_t[:4 * dn], d_win_t[g0:g0 + n_gate], d_win_t[4 * dn:g0]], axis=0)

    def mod_in(xr, lg, lb, sh, sc):
        x0 = _ln(xr, lg, lb)
        return x0, x0 * (1.0 + sc) + sh

    def stage_j(rv, pv):
        _, vjp = jax.vjp(mod_in, rv[0], *pv)
        dx0 = rv[2] if len(rv) > 2 else jnp.zeros_like(rv[1])
        dxr, dlg, dlb, dsh, dsc = vjp((dx0, rv[1]))
        return [dxr], [dlg, dlb, dsh, dsc]

    grad_x, dlg_l, dlb_l, d_sh_a, d_sc_a = _rowwise(
        stage_j, [(x2, d, 0, 0), (dxm, d, 0, 0), (dx0p, d, 0, 0)], ln_in + [sh_a, sc_a], [(d, F32)], [d, d, d, d],
        n_rows=n_lat, tm=tmr, name="ln_mod_bwd_lat")
    _, dlg_c, dlb_c, d_csh_a, d_csc_a = _rowwise(
        stage_j, [(ctx2, d, 0, 0), (dxm, d, 0, n_lat // tmc)], ln_in + [csh_a, csc_a], [(d, F32)], [d, d, d, d],
        n_rows=n_ctx, tm=tmc, name="ln_mod_bwd_ctx")

    zd = jnp.zeros((1, d), F32)
    dmod_rows = jnp.concatenate([jnp.concatenate([d_sh_a, d_sc_a, d_g_a, d_sh_m, d_sc_m, d_g_m], axis=1),
                                 jnp.concatenate([d_csh_a, d_csc_a, zd, zd, zd, zd], axis=1), jnp.zeros((6, 6 * d), F32)], axis=0)
    d16, g_b_mod = _dmod_prep(_bcast8(dmod_rows, "bcast_dmod"), "dmod_prep")
    d16_loc = lax.dynamic_slice(d16, (0, chip * ml), (16, ml))
    g_w_mod = _matmul(s16, d16_loc, form="tn", m=d, n=ml, k=16, tm=_pick(d, 512), tn=_pick(ml, 1024), tk=16, out_dtype=F32,
                      name="mod_bwd_w", a_fn=_silu, kind="hi")
    dsilu = _matmul(d16_loc, w_mod[0], form="nt", m=16, n=d, k=ml, tm=16, tn=_pick(d, 1024), tk=_pick(ml, 1024), out_dtype=F32,
                    name="mod_bwd_c", a_fn=to_mxu, b_fn=to_mxu)
    d_cctx_part = dsilu[8:9, :] * (1 - mc).astype(F32)

    head_sum = lambda t: t[:, 0, 0]
    small = [d_cctx_part, dlg_l + dlg_c, dlb_l + dlb_c, d_wconv, head_sum(d_alog_f), head_sum(d_dt_f), head_sum(d_alog_b),
             head_sum(d_dt_b), d_dn_norm_g, jnp.concatenate([d_wdw_h, d_wdw_v], axis=1), d_conf_dw_b, d_conf_ln_g, d_conf_ln_b,
             d_ln1_g, d_ln1_b, d_b_mlp1, d_b_mlp2, d_ln2_g, d_ln2_b]
    small_shapes = [(d,), (d,), (d,), (SHORT_CONV, 3 * dn), (nh,), (nh,), (nh,), (nh,), (HD,), (CONF_K, conf), (conf,), (conf,), (conf,),
                    (d,), (d,), (ff,), (d,), (d,), (d,)]
    ssum = _sum8(_bcast8(_pack(small), "bcast_small_grads"), "sum_small_grads")
    (t_cctx, g_ln_in_g, g_ln_in_b, g_wconv_full, g_a_log_f, g_dt_bias_f, g_a_log_b, g_dt_bias_b, g_dn_norm_g, g_wdw_full, g_conf_dw_b,
     g_conf_ln_g, g_conf_ln_b, g_ln1_g, g_ln1_b, g_b_mlp1, g_b_mlp2, g_ln2_g, g_ln2_b) = _unpack(ssum, small_shapes)
    sg = jax.nn.sigmoid(c_ctx)
    g_c_ctx = t_cctx * (sg * (1.0 + c_ctx * (1.0 - sg)))
    g_w_qkv_conv = lax.dynamic_slice(g_wconv_full, (0, chip * w_qkv_conv.shape[2]), w_qkv_conv.shape[1:])
    g_conf_dw_w = lax.dynamic_slice(g_wdw_full, (0, chip * conf_dw_w.shape[2]), conf_dw_w.shape[1:])

    big = [d_win_t.reshape(4, r_in, d), d_wout.reshape(4, d // 4, d), d_w1, d_w2.reshape(4, fl, d)]
    recv1 = _pair_send_halves(big, "grads_pair_send")
    chipsum = [_add_my_half(g, r, c_idx, f"grads_pair_add_{i}") for i, (g, r) in enumerate(zip(big, recv1))]
    recv3 = _chip_scatter(chipsum, "grads_chip_scatter")
    mine = [_sum_slots(p, f"grads_chip_sum_{i}") for i, p in enumerate(recv3)]
    joined = _pair_join_halves(mine, "grads_pair_join")
    g_win_t, g_w_out, g_w_mlp1, g_w_mlp2 = [j.reshape(j.shape[0] * j.shape[1], j.shape[2]) for j in joined]
    g_w_in = jnp.transpose(g_win_t)

    grads = dict(c_ctx=g_c_ctx, ln_in_g=g_ln_in_g, ln_in_b=g_ln_in_b, w_mod=g_w_mod[None], b_mod=g_b_mod, w_in=g_w_in[None],
                 w_qkv_conv=g_w_qkv_conv[None], a_log_f=g_a_log_f[None], dt_bias_f=g_dt_bias_f[None], a_log_b=g_a_log_b[None],
                 dt_bias_b=g_dt_bias_b[None], dn_norm_g=g_dn_norm_g[None], conf_dw_w=g_conf_dw_w[None], conf_dw_b=g_conf_dw_b[None],
                 conf_ln_g=g_conf_ln_g[None], conf_ln_b=g_conf_ln_b[None], w_out=g_w_out[None], ln1_g=g_ln1_g[None], ln1_b=g_ln1_b[None],
                 w_mlp1=g_w_mlp1[None], b_mlp1=g_b_mlp1[None], w_mlp2=g_w_mlp2[None], b_mlp2=g_b_mlp2[None], ln2_g=g_ln2_g[None],
                 ln2_b=g_ln2_b[None])
    weights = dict(c_ctx=c_ctx, ln_in_g=ln_in_g, ln_in_b=ln_in_b, w_mod=w_mod, b_mod=b_mod, w_in=w_in, w_qkv_conv=w_qkv_conv,
                   a_log_f=a_log_f, dt_bias_f=dt_bias_f, a_log_b=a_log_b, dt_bias_b=dt_bias_b, dn_norm_g=dn_norm_g, conf_dw_w=conf_dw_w,
                   conf_dw_b=conf_dw_b, conf_ln_g=conf_ln_g, conf_ln_b=conf_ln_b, w_out=w_out, ln1_g=ln1_g, ln1_b=ln1_b, w_mlp1=w_mlp1,
                   b_mlp1=b_mlp1, w_mlp2=w_mlp2, b_mlp2=b_mlp2, ln2_g=ln2_g, ln2_b=ln2_b)
    m_in = dict(c_ctx=m_c_ctx, ln_in_g=m_ln_in_g, ln_in_b=m_ln_in_b, w_mod=m_w_mod, b_mod=m_b_mod, w_in=m_w_in, w_qkv_conv=m_w_qkv_conv,
                a_log_f=m_a_log_f, dt_bias_f=m_dt_bias_f, a_log_b=m_a_log_b, dt_bias_b=m_dt_bias_b, dn_norm_g=m_dn_norm_g,
                conf_dw_w=m_conf_dw_w, conf_dw_b=m_conf_dw_b, conf_ln_g=m_conf_ln_g, conf_ln_b=m_conf_ln_b, w_out=m_w_out, ln1_g=m_ln1_g,
                ln1_b=m_ln1_b, w_mlp1=m_w_mlp1, b_mlp1=m_b_mlp1, w_mlp2=m_w_mlp2, b_mlp2=m_b_mlp2, ln2_g=m_ln2_g, ln2_b=m_ln2_b)
    v_in = dict(c_ctx=v_c_ctx, ln_in_g=v_ln_in_g, ln_in_b=v_ln_in_b, w_mod=v_w_mod, b_mod=v_b_mod, w_in=v_w_in, w_qkv_conv=v_w_qkv_conv,
                a_log_f=v_a_log_f, dt_bias_f=v_dt_bias_f, a_log_b=v_a_log_b, dt_bias_b=v_dt_bias_b, dn_norm_g=v_dn_norm_g,
                conf_dw_w=v_conf_dw_w, conf_dw_b=v_conf_dw_b, conf_ln_g=v_conf_ln_g, conf_ln_b=v_conf_ln_b, w_out=v_w_out, ln1_g=v_ln1_g,
                ln1_b=v_ln1_b, w_mlp1=v_w_mlp1, b_mlp1=v_b_mlp1, w_mlp2=v_w_mlp2, b_mlp2=v_b_mlp2, ln2_g=v_ln2_g, ln2_b=v_ln2_b)
    names = list(weights)
    big_names = ("w_mod", "w_in", "w_out", "w_mlp1", "w_mlp2")
    delta, new_m, new_v = {}, {}, {}
    for nm in big_names:
        w2d = weights[nm][0]
        dl, mm, vv = _adamw(w2d, grads[nm][0], m_in[nm][0], v_in[nm][0], f"adamw_{nm}")
        delta[nm], new_m[nm], new_v[nm] = dl[None], mm[None], vv[None]
    small_names = [nm for nm in names if nm not in big_names]
    shapes = [weights[nm].shape for nm in small_names]
    grads = {nm: (grads[nm].reshape(weights[nm].shape) if nm in small_names else grads[nm]) for nm in names}
    packed = [_pack([src[nm] for nm in small_names]) for src in (weights, grads, m_in, v_in)]
    dl, mm, vv = _adamw(*packed, "adamw_small")
    for nm, a, b_, c_ in zip(small_names, _unpack(dl, shapes), _unpack(mm, shapes), _unpack(vv, shapes)):
        delta[nm], new_m[nm], new_v[nm] = a, b_, c_

    loss = lax.psum(loss_acc[0, 0], MESH_AXES)
    return (loss, grad_x[None], *[grads[nm] for nm in names], *[delta[nm] for nm in names],
            *[new_m[nm] for nm in names], *[new_v[nm] for nm in names])
```

```python
import functools
import math

import jax
import jax.numpy as jnp
from jax import lax
from jax.experimental import pallas as pl
from jax.experimental.pallas import tpu as pltpu

F32 = jnp.float32
BF16 = jnp.bfloat16
MXU_DTYPE = BF16
HIGHEST = lax.Precision.HIGHEST

HD = 128
CHUNK = 128
SHORT_CONV = 7
CONF_K = 31
ALPHA = 2.0 ** 0.25
LN_EPS = 1e-5
V7X_VMEM_BYTES = 64 * 1024 * 1024
LANES = 128

ADAM_LR = 0.001
ADAM_B1 = 0.9
ADAM_B2 = 0.999
ADAM_EPS = 1e-08
ADAM_WD = 0.01
ADAM_STEP = 10

MESH_AXES = ("x", "y", "c")
ANY = pl.BlockSpec(memory_space=pl.ANY)


def _pick(dim, pref, mult=LANES):
    best = None
    t = mult
    while t <= min(dim, pref):
        if dim % t == 0:
            best = t
        t += mult
    return best if best is not None else dim


def _cparams(sem, vmem_est):
    limit = int(min(V7X_VMEM_BYTES - 6 * 1024 * 1024, max(32 * 1024 * 1024, vmem_est + 8 * 1024 * 1024)))
    return pltpu.CompilerParams(dimension_semantics=sem, vmem_limit_bytes=limit)


def _nbytes(shape, dtype):
    return math.prod(shape) * jnp.dtype(dtype).itemsize


def _ln(x, g, b):
    mu = jnp.mean(x, axis=-1, keepdims=True)
    xc = x - mu
    var = jnp.mean(xc * xc, axis=-1, keepdims=True)
    return xc * lax.rsqrt(var + LN_EPS) * g + b


def _silu(x):
    return x * jax.nn.sigmoid(x)


def _softplus(x):
    return jnp.maximum(x, 0.0) + jnp.log1p(jnp.exp(-jnp.abs(x)))


_DIMS = {"nn": (((1,), (0,)), ((), ())), "nt": (((1,), (1,)), ((), ())), "tn": (((0,), (0,)), ((), ()))}


def _split_bf16(x, parts):
    out, rest = [], x.astype(F32)
    for _ in range(parts):
        bits = lax.bitcast_convert_type(rest, jnp.uint32) & jnp.uint32(0xFFFF0000)
        p = lax.bitcast_convert_type(bits, F32)
        out.append(p.astype(BF16))
        rest = rest - p
    return out


def _raw_dot(a, b, form, kind):
    if MXU_DTYPE != F32:
        if kind == "mxu":
            return lax.dot_general(a.astype(MXU_DTYPE), b.astype(MXU_DTYPE), _DIMS[form], preferred_element_type=F32)
        if kind == "hi3":
            (a0, a1), (b0, b1) = _split_bf16(a, 2), _split_bf16(b, 2)
            d = lambda p, q: lax.dot_general(p, q, _DIMS[form], preferred_element_type=F32)
            return d(a0, b0) + (d(a1, b0) + d(a0, b1))
    return lax.dot_general(a.astype(F32), b.astype(F32), _DIMS[form], precision=HIGHEST, preferred_element_type=F32)


@functools.partial(jax.custom_vjp, nondiff_argnums=(2, 3))
def _dot(a, b, form, kind):
    return _raw_dot(a, b, form, kind)


def _dot_fwd(a, b, form, kind):
    return _raw_dot(a, b, form, kind), (a, b)


def _dot_bwd(form, kind, res, dc):
    a, b = res
    if form == "nn":
        return _dot(dc, b, "nt", kind), _dot(a, dc, "tn", kind)
    if form == "nt":
        return _dot(dc, b, "nn", kind), _dot(dc, a, "tn", kind)
    return _dot(b, dc, "nt", kind), _dot(a, dc, "nn", kind)


_dot.defvjp(_dot_fwd, _dot_bwd)


def _raw_xdot(a, b, form, exact):
    if MXU_DTYPE == F32:
        return _raw_dot(a, b, form, "hi")
    d = lambda p, q: lax.dot_general(p, q, _DIMS[form], preferred_element_type=F32)
    if exact == "a":
        a16, (b0, b1, b2) = a.astype(BF16), _split_bf16(b, 3)
        return d(a16, b0) + (d(a16, b1) + d(a16, b2))
    b16, (a0, a1, a2) = b.astype(BF16), _split_bf16(a, 3)
    return d(a0, b16) + (d(a1, b16) + d(a2, b16))


@functools.partial(jax.custom_vjp, nondiff_argnums=(2, 3))
def _xdot(a, b, form, exact):
    return _raw_xdot(a, b, form, exact)


def _xdot_fwd(a, b, form, exact):
    return _raw_xdot(a, b, form, exact), (a, b)


def _xdot_bwd(form, exact, res, dc):
    a, b = res
    if form == "nn" and exact == "a":
        return jnp.zeros_like(a), _xdot(a, dc, "tn", "a")
    assert form == "tn" and exact == "b"
    return _xdot(b, dc, "nt", "a"), jnp.zeros_like(b)


_xdot.defvjp(_xdot_fwd, _xdot_bwd)


def _spec2(block, idx, split=None):
    if split is None:
        return pl.BlockSpec(tuple(block), idx)
    per = split // block[1]

    def idx3(*g):
        r, cblk = idx(*g)
        return (cblk // per, r, cblk % per)

    return pl.BlockSpec((None,) + tuple(block), idx3)


def _matmul(a, b, *, form, m, n, k, tm, tn, tk, out_dtype, name, bias=None, a_fn=None, b_fn=None,
            epi=None, epi_in=None, colsum=False, kind="mxu", b_split=None, o_split=None):
    assert m % tm == 0 and n % tn == 0 and k % tk == 0, (name, m, n, k, tm, tn, tk)
    nk = k // tk
    grid = (n // tn, m // tm, nk)
    if form == "tn":
        a_spec = pl.BlockSpec((tk, tm), lambda j, i, kk: (kk, i))
    else:
        a_spec = pl.BlockSpec((tm, tk), lambda j, i, kk: (i, kk))
    if form == "nt":
        b_spec = _spec2((tn, tk), lambda j, i, kk: (j, kk), b_split)
    else:
        b_spec = _spec2((tk, tn), lambda j, i, kk: (kk, j), b_split)
    in_specs = [a_spec, b_spec]
    operands = [a, b]
    if bias is not None:
        in_specs.append(pl.BlockSpec((1, tn), lambda j, i, kk: (0, j)))
        operands.append(bias)
    if epi_in is not None:
        in_specs.append(pl.BlockSpec((tm, tn), lambda j, i, kk: (i, j)))
        operands.append(epi_in)
    out_specs = [_spec2((tm, tn), lambda j, i, kk: (i, j), o_split)]
    if o_split is None:
        out_shape = [jax.ShapeDtypeStruct((m, n), out_dtype)]
    else:
        out_shape = [jax.ShapeDtypeStruct((n // o_split, m, o_split), out_dtype)]
    if colsum:
        out_specs.append(pl.BlockSpec((1, tn), lambda j, i, kk: (0, j)))
        out_shape.append(jax.ShapeDtypeStruct((1, n), F32))
    has_bias, has_epi_in = bias is not None, epi_in is not None

    def body(*refs):
        refs = list(refs)
        a_ref, b_ref = refs[0], refs[1]
        pos = 2
        bias_ref = epi_ref = cs_ref = None
        if has_bias:
            bias_ref = refs[pos]
            pos += 1
        if has_epi_in:
            epi_ref = refs[pos]
            pos += 1
        o_ref = refs[pos]
        pos += 1
        if colsum:
            cs_ref = refs[pos]
            pos += 1
        acc_ref = refs[pos]
        i, kk = pl.program_id(1), pl.program_id(2)

        @pl.when(kk == 0)
        def _():
            acc_ref[...] = jnp.zeros_like(acc_ref)

        av = a_ref[...]
        if a_fn is not None:
            av = a_fn(av)
        bv = b_ref[...]
        if b_fn is not None:
            bv = b_fn(bv)
        acc_ref[...] += _raw_dot(av, bv, form, kind)

        @pl.when(kk == nk - 1)
        def _():
            r = acc_ref[...]
            if has_bias:
                r = r + bias_ref[...]
            if epi is not None:
                r = epi(r, epi_ref[...]) if has_epi_in else epi(r)
            o_ref[...] = r.astype(out_dtype)
            if colsum:
                s = jnp.sum(r, axis=0, keepdims=True)

                @pl.when(i == 0)
                def _():
                    cs_ref[...] = s

                @pl.when(i > 0)
                def _():
                    cs_ref[...] += s

    est = 2 * (_nbytes((tm, tk), a.dtype) + _nbytes((tk, tn), b.dtype) + _nbytes((tm, tn), out_dtype))
    est += _nbytes((tm, tn), F32) * 2 + (2 * _nbytes((tm, tn), epi_in.dtype) if has_epi_in else 0)
    res = pl.pallas_call(
        body, grid=grid, in_specs=in_specs, out_specs=out_specs, out_shape=out_shape,
        scratch_shapes=[pltpu.VMEM((tm, tn), F32)], name=name,
        compiler_params=_cparams(("arbitrary", "arbitrary", "arbitrary"), est),
    )(*operands)
    return res if colsum else res[0]


def _rowwise(fn, rows, params, row_outs, acc_outs, *, n_rows, tm, name):
    assert n_rows % tm == 0, (name, n_rows, tm)
    nr, npar, nro, nac = len(rows), len(params), len(row_outs), len(acc_outs)
    in_specs = [pl.BlockSpec((tm, w), functools.partial(lambda i, cb, ro: (i + ro, cb), cb=cb, ro=ro))
                for (_, w, cb, ro) in rows]
    in_specs += [pl.BlockSpec((1, p.shape[1]), lambda i: (0, 0)) for p in params]
    out_specs = [pl.BlockSpec((tm, w), lambda i: (i, 0)) for (w, _) in row_outs]
    out_specs += [pl.BlockSpec((1, w), lambda i: (0, 0)) for w in acc_outs]
    out_shape = [jax.ShapeDtypeStruct((n_rows, w), dt) for (w, dt) in row_outs]
    out_shape += [jax.ShapeDtypeStruct((1, w), F32) for w in acc_outs]

    def body(*refs):
        rv = [r[...] for r in refs[:nr]]
        pv = [r[...] for r in refs[nr:nr + npar]]
        ro_refs = refs[nr + npar:nr + npar + nro]
        ac_refs = refs[nr + npar + nro:]
        ro, ac = fn(rv, pv)
        for ref, val in zip(ro_refs, ro, strict=True):
            ref[...] = val.astype(ref.dtype)
        first = pl.program_id(0) == 0
        for ref, val in zip(ac_refs, ac, strict=True):
            val = jnp.broadcast_to(val.astype(F32), ref.shape)

            @pl.when(first)
            def _(ref=ref, val=val):
                ref[...] = val

            @pl.when(jnp.logical_not(first))
            def _(ref=ref, val=val):
                ref[...] += val

    io = sum(_nbytes((tm, w), a.dtype) for (a, w, _, _) in rows) + sum(_nbytes((tm, w), dt) for (w, dt) in row_outs)
    widest = max([w for (_, w, _, _) in rows] + [w for (w, _) in row_outs])
    est = 2 * io + 12 * _nbytes((tm, widest), F32)
    return pl.pallas_call(
        body, grid=(n_rows // tm,), in_specs=in_specs, out_specs=out_specs, out_shape=out_shape, name=name,
        compiler_params=_cparams(("arbitrary",), est),
    )(*[a for (a, _, _, _) in rows], *params)


def _conv7_tiles(n_lat, n_ctx):
    tt = min(256, n_ctx)
    assert n_lat % tt == 0 and n_ctx % tt == 0
    return tt, [(t0, 8 + t0) for t0 in range(0, n_lat, tt)] + [(n_lat + t0, 16 + n_lat + t0) for t0 in range(0, n_ctx, tt)]


def _conv7_fill(p_ref, x_ref, n_lat, n_ctx):
    z8 = jnp.zeros((8, LANES), F32)
    p_ref[0:8, :] = z8
    p_ref[8:8 + n_lat, :] = x_ref[0:n_lat, :]
    p_ref[8 + n_lat:16 + n_lat, :] = z8
    p_ref[16 + n_lat:16 + n_lat + n_ctx, :] = x_ref[n_lat:n_lat + n_ctx, :]
    p_ref[16 + n_lat + n_ctx:24 + n_lat + n_ctx, :] = z8


def _conv7_fwd(h, w_conv, *, n_lat, n_ctx, nh):
    nt = n_lat + n_ctx
    tt, tiles = _conv7_tiles(n_lat, n_ctx)
    half = SHORT_CONV // 2

    def body(x_ref, w_ref, o_ref, p_ref):
        _conv7_fill(p_ref, x_ref, n_lat, n_ctx)
        for (row, prow) in tiles:
            acc = jnp.zeros((tt, LANES), F32)
            for kk in range(SHORT_CONV):
                acc = acc + w_ref[kk:kk + 1, :] * p_ref[pl.ds(prow + kk - half, tt), :]
            o_ref[pl.ds(row, tt), :] = _silu(acc)

    return pl.pallas_call(
        body, grid=(3 * nh,),
        in_specs=[pl.BlockSpec((nt, LANES), lambda j: (0, j)), pl.BlockSpec((SHORT_CONV, LANES), lambda j: (0, j))],
        out_specs=pl.BlockSpec((None, None, nt, LANES), lambda j: (j // nh, j % nh, 0, 0)),
        out_shape=jax.ShapeDtypeStruct((3, nh, nt, LANES), F32),
        scratch_shapes=[pltpu.VMEM((nt + 24, LANES), F32)], name="conv7_fwd",
        compiler_params=_cparams(("arbitrary",), 5 * _nbytes((nt + 24, LANES), F32)),
    )(h, w_conv)


def _conv7_bwd(h, w_conv, dact_f, dact_b, *, n_lat, n_ctx, nh):
    nt = n_lat + n_ctx
    tt, tiles = _conv7_tiles(n_lat, n_ctx)
    half = SHORT_CONV // 2

    def body(x_ref, w_ref, df_ref, db_ref, dx_ref, dw_ref, p_ref, q_ref):
        _conv7_fill(p_ref, x_ref, n_lat, n_ctx)
        z8 = jnp.zeros((8, LANES), F32)
        q_ref[0:8, :] = z8
        q_ref[8 + n_lat:16 + n_lat, :] = z8
        q_ref[16 + n_lat + n_ctx:24 + n_lat + n_ctx, :] = z8
        dw = [jnp.zeros((1, LANES), F32) for _ in range(SHORT_CONV)]
        for (row, prow) in tiles:
            pre = jnp.zeros((tt, LANES), F32)
            for kk in range(SHORT_CONV):
                pre = pre + w_ref[kk:kk + 1, :] * p_ref[pl.ds(prow + kk - half, tt), :]
            s = jax.nn.sigmoid(pre)
            dpre = (df_ref[pl.ds(row, tt), :] + db_ref[pl.ds(row, tt), :]) * (s * (1.0 + pre * (1.0 - s)))
            q_ref[pl.ds(prow, tt), :] = dpre
            for kk in range(SHORT_CONV):
                dw[kk] = dw[kk] + jnp.sum(dpre * p_ref[pl.ds(prow + kk - half, tt), :], axis=0, keepdims=True)
        for (row, prow) in tiles:
            acc = jnp.zeros((tt, LANES), F32)
            for kk in range(SHORT_CONV):
                acc = acc + w_ref[kk:kk + 1, :] * q_ref[pl.ds(prow + half - kk, tt), :]
            dx_ref[pl.ds(row, tt), :] = acc.astype(dx_ref.dtype)
        for kk in range(SHORT_CONV):
            dw_ref[kk:kk + 1, :] = dw[kk]

    dspec = pl.BlockSpec((None, None, nt, LANES), lambda j: (j // nh, j % nh, 0, 0))
    return pl.pallas_call(
        body, grid=(3 * nh,),
        in_specs=[pl.BlockSpec((nt, LANES), lambda j: (0, j)), pl.BlockSpec((SHORT_CONV, LANES), lambda j: (0, j)), dspec, dspec],
        out_specs=[pl.BlockSpec((nt, LANES), lambda j: (0, j)), pl.BlockSpec((SHORT_CONV, LANES), lambda j: (0, j))],
        out_shape=[jax.ShapeDtypeStruct((nt, 3 * nh * LANES), MXU_DTYPE), jax.ShapeDtypeStruct((SHORT_CONV, 3 * nh * LANES), F32)],
        scratch_shapes=[pltpu.VMEM((nt + 24, LANES), F32), pltpu.VMEM((nt + 24, LANES), F32)], name="conv7_bwd",
        compiler_params=_cparams(("arbitrary",), 10 * _nbytes((nt + 24, LANES), F32)),
    )(h, w_conv, dact_f, dact_b)


_RT = 4
_CP = CONF_K // 2


def _conf_pad_shape(gw, vertical):
    return (gw + 2 * _CP, gw, LANES) if vertical else (gw, gw + 32, LANES)


def _conf_fill(p_ref, val, gw, vertical):
    if vertical:
        p_ref[0:_CP] = jnp.zeros((_CP, gw, LANES), F32)
        p_ref[_CP + gw:2 * _CP + gw] = jnp.zeros((_CP, gw, LANES), F32)
        p_ref[_CP:_CP + gw] = val
    else:
        p_ref[:, 0:16, :] = jnp.zeros((gw, 16, LANES), F32)
        p_ref[:, 16 + gw:32 + gw, :] = jnp.zeros((gw, 16, LANES), F32)
        p_ref[:, 16:16 + gw, :] = val


def _conf_window(p_ref, r0, shift, gw, vertical):
    if vertical:
        return p_ref[pl.ds(r0 + _CP + shift, _RT), :, :]
    return p_ref[pl.ds(r0, _RT), pl.ds(16 + shift, gw), :]


def _conf_conv_fwd(h3, w_dw, *, gw, col0, half, vertical):
    cb_val = (col0 + (half if vertical else 0)) // LANES
    cb_gate = cb_val + 2 * half // LANES
    cb_w = (half if vertical else 0) // LANES

    def body(v_ref, g_ref, w_ref, o_ref, p_ref):
        _conf_fill(p_ref, v_ref[...] * jax.nn.sigmoid(g_ref[...]), gw, vertical)

        def step(t, carry):
            r0 = t * _RT
            acc = jnp.zeros((_RT, gw, LANES), F32)
            for kk in range(CONF_K):
                acc = acc + w_ref[kk:kk + 1, :] * _conf_window(p_ref, r0, kk - _CP, gw, vertical)
            o_ref[pl.ds(r0, _RT)] = acc
            return carry

        lax.fori_loop(0, gw // _RT, step, 0)

    blk = (gw, gw, LANES)
    return pl.pallas_call(
        body, grid=(half // LANES,),
        in_specs=[pl.BlockSpec(blk, lambda j: (0, 0, cb_val + j)), pl.BlockSpec(blk, lambda j: (0, 0, cb_gate + j)),
                  pl.BlockSpec((CONF_K, LANES), lambda j: (0, cb_w + j))],
        out_specs=pl.BlockSpec(blk, lambda j: (0, 0, j)),
        out_shape=jax.ShapeDtypeStruct((gw, gw, half), F32),
        scratch_shapes=[pltpu.VMEM(_conf_pad_shape(gw, vertical), F32)],
        name="conf_conv_fwd_v" if vertical else "conf_conv_fwd_h",
        compiler_params=_cparams(("arbitrary",), 8 * _nbytes(_conf_pad_shape(gw, vertical), F32)),
    )(h3, h3, w_dw)


def _conf_conv_bwd(h3, w_dw, dyc, *, gw, col0, half, vertical):
    cb_val = (col0 + (half if vertical else 0)) // LANES
    cb_gate = cb_val + 2 * half // LANES
    cb_w = (half if vertical else 0) // LANES

    def body(v_ref, g_ref, w_ref, d_ref, dv_ref, dg_ref, dw_ref, py_ref, pd_ref):
        _conf_fill(py_ref, v_ref[...] * jax.nn.sigmoid(g_ref[...]), gw, vertical)
        _conf_fill(pd_ref, d_ref[...], gw, vertical)

        def step(t, carry):
            r0 = t * _RT
            acc = jnp.zeros((_RT, gw, LANES), F32)
            for kk in range(CONF_K):
                acc = acc + w_ref[kk:kk + 1, :] * _conf_window(pd_ref, r0, _CP - kk, gw, vertical)
            val = v_ref[pl.ds(r0, _RT)]
            sg = jax.nn.sigmoid(g_ref[pl.ds(r0, _RT)])
            dv_ref[pl.ds(r0, _RT)] = (acc * sg).astype(dv_ref.dtype)
            dg_ref[pl.ds(r0, _RT)] = (acc * val * sg * (1.0 - sg)).astype(dg_ref.dtype)
            return carry

        lax.fori_loop(0, gw // _RT, step, 0)

        for kk in range(CONF_K):
            def wstep(t, acc, kk=kk):
                r0 = t * _RT
                prod = d_ref[pl.ds(r0, _RT)] * _conf_window(py_ref, r0, kk - _CP, gw, vertical)
                return acc + jnp.sum(prod, axis=0)

            tot = lax.fori_loop(0, gw // _RT, wstep, jnp.zeros((gw, LANES), F32))
            dw_ref[kk:kk + 1, :] = jnp.sum(tot, axis=0, keepdims=True)

    blk = (gw, gw, LANES)
    pshape = _conf_pad_shape(gw, vertical)
    return pl.pallas_call(
        body, grid=(half // LANES,),
        in_specs=[pl.BlockSpec(blk, lambda j: (0, 0, cb_val + j)), pl.BlockSpec(blk, lambda j: (0, 0, cb_gate + j)),
                  pl.BlockSpec((CONF_K, LANES), lambda j: (0, cb_w + j)), pl.BlockSpec(blk, lambda j: (0, 0, j))],
        out_specs=[pl.BlockSpec(blk, lambda j: (0, 0, j)), pl.BlockSpec(blk, lambda j: (0, 0, j)),
                   pl.BlockSpec((CONF_K, LANES), lambda j: (0, j))],
        out_shape=[jax.ShapeDtypeStruct((gw, gw, half), MXU_DTYPE), jax.ShapeDtypeStruct((gw, gw, half), MXU_DTYPE),
                   jax.ShapeDtypeStruct((CONF_K, half), F32)],
        scratch_shapes=[pltpu.VMEM(pshape, F32), pltpu.VMEM(pshape, F32)],
        name="conf_conv_bwd_v" if vertical else "conf_conv_bwd_h",
        compiler_params=_cparams(("arbitrary",), 12 * _nbytes(pshape, F32)),
    )(h3, h3, w_dw, dyc)


_INV_BASE = 8


def _tri_inv_raw(mats):
    c = mats[0].shape[0]
    ri = lax.broadcasted_iota(jnp.int32, (c, c), 0)
    ci = lax.broadcasted_iota(jnp.int32, (c, c), 1)
    eye = (ri == ci).astype(F32)
    base = min(_INV_BASE, c)
    same = lambda size: (ri // size) == (ci // size)
    dmat = [jnp.where(same(base), a, 0.0) for a in mats]
    x = [eye - dm for dm in dmat]
    pw = [_raw_dot(dm, dm, "nn", "hi3") for dm in dmat]
    span = 2
    while span < base:
        x = [xi + _raw_dot(xi, p, "nn", "hi3") for xi, p in zip(x, pw)]
        span *= 2
        if span < base:
            pw = [_raw_dot(p, p, "nn", "hi3") for p in pw]
    size = base
    while size < c:
        off = jnp.logical_and(jnp.logical_not(same(size)), same(2 * size))
        t = [_raw_dot(xi, jnp.where(off, a, 0.0), "nn", "hi3") for xi, a in zip(x, mats)]
        x = [xi - _raw_dot(ti, xi, "nn", "hi3") for xi, ti in zip(x, t)]
        size *= 2
    return x


@jax.custom_vjp
def _tri_inv(mats):
    return _tri_inv_raw(mats)


def _tri_inv_fwd(mats):
    x = _tri_inv_raw(mats)
    return x, x


def _tri_inv_bwd(x, dx):
    t = [_raw_dot(xi, di, "tn", "hi3") for xi, di in zip(x, dx)]
    return ([-_raw_dot(ti, xi, "nt", "hi3") for ti, xi in zip(t, x)],)


_tri_inv.defvjp(_tri_inv_fwd, _tri_inv_bwd)


def _delta_chunk(qa, ka, va, graw, braw, alog, dtb, s, *, rev):
    c = qa[0].shape[0]

    def each(f, *lists):
        return [f(*xs) for xs in zip(*lists, strict=True)]

    def l2n(t):
        return t * lax.rsqrt(jnp.sum(t * t, axis=-1, keepdims=True) + 1e-6)

    ri = lax.broadcasted_iota(jnp.int32, (c, c), 0)
    ci = lax.broadcasted_iota(jnp.int32, (c, c), 1)
    incl = (ci >= ri) if rev else (ci <= ri)
    strict = (ci > ri) if rev else (ci < ri)
    tmat = incl.astype(F32)
    tmat_t = ((ri >= ci) if rev else (ri <= ci)).astype(F32)

    q = each(lambda t: l2n(t) * (HD ** -0.5), qa)
    k = each(l2n, ka)
    g = each(lambda al, gr, dt: -jnp.exp(al) * _softplus(gr + dt), alog, graw, dtb)
    beta = each(jax.nn.sigmoid, braw)
    gc_wide = each(lambda t: _xdot(tmat, jnp.broadcast_to(t, (c, HD)), "nn", "a"), g)
    gc_rows = gc_wide if c == HD else each(lambda t: _xdot(tmat, jnp.broadcast_to(t, (c, c)), "nn", "a"), g)
    gc_cols = each(lambda t: _xdot(jnp.broadcast_to(t, (c, c)), tmat_t, "tn", "b"), g)
    gamma = each(lambda r, cc: jnp.where(incl, jnp.exp(jnp.where(incl, r - cc, 0.0)), 0.0), gc_rows, gc_cols)
    kb = each(lambda a, b: a * b, k, beta)
    a_mat = each(lambda a, b, gm: jnp.where(strict, _dot(a, b, "nt", "mxu") * gm, 0.0), kb, k, gamma)
    minv = _tri_inv(a_mat)
    eg = each(jnp.exp, gc_wide)
    u = each(lambda mi, v, b: _dot(mi, v * b, "nn", "hi3"), minv, va, beta)
    w = each(lambda mi, a, e: _dot(mi, a * e, "nn", "hi3"), minv, kb, eg)
    attn = each(lambda a, b, gm: _dot(a, b, "nt", "mxu") * gm, q, k, gamma)
    tot = each(lambda t: jnp.sum(t, axis=0, keepdims=True), g)
    q_dec = each(lambda a, e: a * e, q, eg)
    k_dec = each(lambda a, t, gw_: a * jnp.exp(t - gw_), k, tot, gc_wide)
    v_new = each(lambda uu, ww, ss: uu - _dot(ww, ss, "nn", "mxu"), u, w, s)
    o = each(lambda qd, ss, at, vn: _dot(qd, ss, "nn", "mxu") + _dot(at, vn, "nn", "mxu"), q_dec, s, attn, v_new)
    s_new = each(lambda ss, t, kd, vn: ss * jnp.exp(t) + _dot(kd, vn, "tn", "mxu"), s, tot, k_dec, v_new)
    return o, s_new


def _delta_args(a_ref, g_ref, al_ref, dt_ref, s_ref, hb, gcol):
    gts = [g_ref[hh] for hh in range(hb)]
    return ([a_ref[0, hh] for hh in range(hb)], [a_ref[1, hh] for hh in range(hb)], [a_ref[2, hh] for hh in range(hb)],
            [t[:, gcol:gcol + 1] for t in gts], [t[:, gcol + 1:gcol + 2] for t in gts],
            [al_ref[hh, 0:1, 0:1] for hh in range(hb)], [dt_ref[hh, 0:1, 0:1] for hh in range(hb)],
            [s_ref[hh] for hh in range(hb)])


def _delta_chunk_of_step(step, nch, nlc, rev):
    return (nch - 1 - step) if rev else (step + nlc) % nch


def _delta_fwd(act, gates_hm, alog_b, dtb_b, *, n_lat, n_ctx, nh, hb, rev):
    nt = n_lat + n_ctx
    nch, nlc = nt // CHUNK, n_lat // CHUNK
    gcol = 2 if rev else 0
    chunk = functools.partial(_delta_chunk_of_step, nch=nch, nlc=nlc, rev=rev)

    def body(a_ref, g_ref, al_ref, dt_ref, o_ref, st_ref, s_ref):
        n = pl.program_id(1)

        @pl.when(n == 0)
        def _():
            s_ref[...] = jnp.zeros_like(s_ref)

        args = _delta_args(a_ref, g_ref, al_ref, dt_ref, s_ref, hb, gcol)
        o, s_new = _delta_chunk(*args, rev=rev)
        for hh in range(hb):
            st_ref[hh] = args[7][hh]
            o_ref[:, hh * HD:(hh + 1) * HD] = o[hh]
            s_ref[hh] = s_new[hh]

    par = pl.BlockSpec((hb, 8, LANES), lambda h, n: (h, 0, 0))
    return pl.pallas_call(
        body, grid=(nh // hb, nch),
        in_specs=[pl.BlockSpec((3, hb, CHUNK, HD), lambda h, n: (0, h, chunk(n), 0)),
                  pl.BlockSpec((hb, CHUNK, 4), lambda h, n: (h, chunk(n), 0)), par, par],
        out_specs=[pl.BlockSpec((CHUNK, hb * HD), lambda h, n: (chunk(n), h)),
                   pl.BlockSpec((hb, None, HD, HD), lambda h, n: (h, n, 0, 0))],
        out_shape=[jax.ShapeDtypeStruct((nt, nh * HD), F32), jax.ShapeDtypeStruct((nh, nch, HD, HD), F32)],
        scratch_shapes=[pltpu.VMEM((hb, HD, HD), F32)], name="delta_fwd_rev" if rev else "delta_fwd",
        compiler_params=_cparams(("arbitrary", "arbitrary"), 0),
    )(act, gates_hm, alog_b, dtb_b)


def _delta_bwd(act, gates_hm, alog_b, dtb_b, states, do, *, n_lat, n_ctx, nh, hb, rev):
    nt = n_lat + n_ctx
    nch, nlc = nt // CHUNK, n_lat // CHUNK
    gcol = 2 if rev else 0

    def chunk(m):
        return _delta_chunk_of_step(nch - 1 - m, nch, nlc, rev)

    def body(a_ref, g_ref, al_ref, dt_ref, st_ref, do_ref, da_ref, dg_ref, dal_ref, ddt_ref, ds_ref):
        m = pl.program_id(1)

        @pl.when(m == 0)
        def _():
            ds_ref[...] = jnp.zeros_like(ds_ref)
            dal_ref[...] = jnp.zeros_like(dal_ref)
            ddt_ref[...] = jnp.zeros_like(ddt_ref)

        is_lat = chunk(m) < nlc
        args = _delta_args(a_ref, g_ref, al_ref, dt_ref, st_ref, hb, gcol)
        _, vjp = jax.vjp(functools.partial(_delta_chunk, rev=rev), *args)
        do_h = [jnp.where(is_lat, do_ref[:, hh * HD:(hh + 1) * HD], 0.0) for hh in range(hb)]
        grads = vjp((do_h, [ds_ref[hh] for hh in range(hb)]))
        for hh in range(hb):
            dq, dk, dv, dgr, dbr, dal, ddt, ds = [gr[hh] for gr in grads]
            da_ref[0, hh] = dq
            da_ref[1, hh] = dk
            da_ref[2, hh] = dv
            lane = lax.broadcasted_iota(jnp.int32, (CHUNK, 4), 1)
            dg_ref[hh] = jnp.where(lane == 0, dgr, jnp.where(lane == 1, dbr, 0.0))
            dal_ref[hh] += jnp.broadcast_to(dal, (8, LANES))
            ddt_ref[hh] += jnp.broadcast_to(ddt, (8, LANES))
            ds_ref[hh] = ds

    par = pl.BlockSpec((hb, 8, LANES), lambda h, m: (h, 0, 0))
    return pl.pallas_call(
        body, grid=(nh // hb, nch),
        in_specs=[pl.BlockSpec((3, hb, CHUNK, HD), lambda h, m: (0, h, chunk(m), 0)),
                  pl.BlockSpec((hb, CHUNK, 4), lambda h, m: (h, chunk(m), 0)), par, par,
                  pl.BlockSpec((hb, None, HD, HD), lambda h, m: (h, nch - 1 - m, 0, 0)),
                  pl.BlockSpec((CHUNK, hb * HD), lambda h, m: (jnp.minimum(chunk(m), nlc - 1), h))],
        out_specs=[pl.BlockSpec((3, hb, CHUNK, HD), lambda h, m: (0, h, chunk(m), 0)),
                   pl.BlockSpec((hb, CHUNK, 4), lambda h, m: (h, chunk(m), 0)), par, par],
        out_shape=[jax.ShapeDtypeStruct((3, nh, nt, HD), F32), jax.ShapeDtypeStruct((nh, nt, 4), F32),
                   jax.ShapeDtypeStruct((nh, 8, LANES), F32), jax.ShapeDtypeStruct((nh, 8, LANES), F32)],
        scratch_shapes=[pltpu.VMEM((hb, HD, HD), F32)], name="delta_bwd_rev" if rev else "delta_bwd",
        compiler_params=_cparams(("arbitrary", "arbitrary"), 0),
    )(act, gates_hm, alog_b, dtb_b, states, do)


def _my_pos():
    return lax.axis_index("x"), lax.axis_index("y"), lax.axis_index("c")


def _flip(v, d):
    return 1 - v if d else v


def _bcast8(block, name):
    r, w = block.shape
    assert r % 8 == 0 and w % LANES == 0

    def body(x_ref, o_ref, send_sems, recv_sems, local_sem):
        x, y, c = _my_pos()

        def rows(px, py, pc):
            return o_ref.at[pl.ds((4 * px + 2 * py + pc) * r, r), :]

        mine = pltpu.make_async_copy(x_ref, rows(x, y, c), local_sem)
        mine.start()
        copies = []
        for kk in range(1, 8):
            dx, dy, dc = (kk >> 2) & 1, (kk >> 1) & 1, kk & 1
            peer = (_flip(x, dx), _flip(y, dy), _flip(c, dc))
            cp = pltpu.make_async_remote_copy(src_ref=x_ref, dst_ref=rows(x, y, c), send_sem=send_sems.at[kk - 1],
                                              recv_sem=recv_sems.at[kk - 1], device_id=peer, device_id_type=pl.DeviceIdType.MESH)
            cp.start()
            copies.append((cp, peer))
        for kk, (cp, peer) in enumerate(copies):
            pltpu.make_async_remote_copy(src_ref=x_ref, dst_ref=rows(*peer), send_sem=send_sems.at[kk], recv_sem=recv_sems.at[kk],
                                         device_id=peer, device_id_type=pl.DeviceIdType.MESH).wait_recv()
        for cp, _ in copies:
            cp.wait_send()
        mine.wait()

    return pl.pallas_call(
        body, out_shape=jax.ShapeDtypeStruct((8 * r, w), block.dtype),
        in_specs=[pl.BlockSpec(memory_space=pltpu.VMEM)], out_specs=pl.BlockSpec(memory_space=pltpu.VMEM),
        scratch_shapes=[pltpu.SemaphoreType.DMA((7,)), pltpu.SemaphoreType.DMA((7,)), pltpu.SemaphoreType.DMA], name=name,
        compiler_params=pltpu.CompilerParams(vmem_limit_bytes=int(max(32 * 1024 * 1024, 12 * _nbytes((r, w), block.dtype)))),
    )(block)


_CHIP_PEERS = ((1, 0), (0, 1), (1, 1))


def _chip_gather(shards, name):
    n = len(shards)

    def body(*refs):
        ins, outs = refs[:n], refs[n:2 * n]
        send_sems, recv_sems, local_sems = refs[2 * n:]
        x, y, c = _my_pos()
        started = []
        for a in range(n):
            loc = pltpu.make_async_copy(ins[a], outs[a].at[2 * x + y], local_sems.at[a])
            loc.start()
            started.append(loc)
        sends = []
        for a in range(n):
            for kk, (dx, dy) in enumerate(_CHIP_PEERS):
                px, py = _flip(x, dx), _flip(y, dy)
                cp = pltpu.make_async_remote_copy(src_ref=ins[a], dst_ref=outs[a].at[2 * x + y], send_sem=send_sems.at[3 * a + kk],
                                                  recv_sem=recv_sems.at[3 * a + kk], device_id=(px, py, c),
                                                  device_id_type=pl.DeviceIdType.MESH)
                cp.start()
                sends.append(cp)
        for a in range(n):
            for kk, (dx, dy) in enumerate(_CHIP_PEERS):
                px, py = _flip(x, dx), _flip(y, dy)
                pltpu.make_async_remote_copy(src_ref=ins[a], dst_ref=outs[a].at[2 * px + py], send_sem=send_sems.at[3 * a + kk],
                                             recv_sem=recv_sems.at[3 * a + kk], device_id=(px, py, c),
                                             device_id_type=pl.DeviceIdType.MESH).wait_recv()
        for cp in sends:
            cp.wait_send()
        for loc in started:
            loc.wait()

    return pl.pallas_call(
        body, out_shape=[jax.ShapeDtypeStruct((4,) + s.shape, s.dtype) for s in shards],
        in_specs=[ANY] * n, out_specs=[ANY] * n,
        scratch_shapes=[pltpu.SemaphoreType.DMA((3 * n,)), pltpu.SemaphoreType.DMA((3 * n,)), pltpu.SemaphoreType.DMA((n,))],
        name=name,
    )(*shards)


def _chip_scatter(parts, name):
    n = len(parts)

    def body(*refs):
        ins, outs = refs[:n], refs[n:2 * n]
        send_sems, recv_sems, local_sems = refs[2 * n:]
        x, y, c = _my_pos()
        me = 2 * x + y
        started = []
        for a in range(n):
            loc = pltpu.make_async_copy(ins[a].at[me], outs[a].at[me], local_sems.at[a])
            loc.start()
            started.append(loc)
        sends = []
        for a in range(n):
            for kk, (dx, dy) in enumerate(_CHIP_PEERS):
                px, py = _flip(x, dx), _flip(y, dy)
                cp = pltpu.make_async_remote_copy(src_ref=ins[a].at[2 * px + py], dst_ref=outs[a].at[me], send_sem=send_sems.at[3 * a + kk],
                                                  recv_sem=recv_sems.at[3 * a + kk], device_id=(px, py, c),
                                                  device_id_type=pl.DeviceIdType.MESH)
                cp.start()
                sends.append(cp)
        for a in range(n):
            for kk, (dx, dy) in enumerate(_CHIP_PEERS):
                px, py = _flip(x, dx), _flip(y, dy)
                pltpu.make_async_remote_copy(src_ref=ins[a].at[me], dst_ref=outs[a].at[2 * px + py], send_sem=send_sems.at[3 * a + kk],
                                             recv_sem=recv_sems.at[3 * a + kk], device_id=(px, py, c),
                                             device_id_type=pl.DeviceIdType.MESH).wait_recv()
        for cp in sends:
            cp.wait_send()
        for loc in started:
            loc.wait()

    return pl.pallas_call(
        body, out_shape=[jax.ShapeDtypeStruct(p.shape, p.dtype) for p in parts],
        in_specs=[ANY] * n, out_specs=[ANY] * n,
        scratch_shapes=[pltpu.SemaphoreType.DMA((3 * n,)), pltpu.SemaphoreType.DMA((3 * n,)), pltpu.SemaphoreType.DMA((n,))],
        name=name,
    )(*parts)


def _pair_send_halves(grads, name):
    n = len(grads)

    def body(*refs):
        ins, outs = refs[:n], refs[n:2 * n]
        send_sems, recv_sems = refs[2 * n:]
        x, y, c = _my_pos()
        sib = (x, y, 1 - c)
        sends = []
        for a in range(n):
            rh = ins[a].shape[1] // 2
            cp = pltpu.make_async_remote_copy(src_ref=ins[a].at[:, pl.ds((1 - c) * rh, rh), :], dst_ref=outs[a], send_sem=send_sems.at[a],
                                              recv_sem=recv_sems.at[a], device_id=sib, device_id_type=pl.DeviceIdType.MESH)
            cp.start()
            sends.append(cp)
        for cp in sends:
            cp.wait_recv()
        for cp in sends:
            cp.wait_send()

    return pl.pallas_call(
        body, out_shape=[jax.ShapeDtypeStruct((4, g.shape[1] // 2, g.shape[2]), g.dtype) for g in grads],
        in_specs=[ANY] * n, out_specs=[ANY] * n,
        scratch_shapes=[pltpu.SemaphoreType.DMA((n,)), pltpu.SemaphoreType.DMA((n,))], name=name,
    )(*grads)


def _pair_join_halves(halves, name):
    n = len(halves)

    def body(*refs):
        ins, outs = refs[:n], refs[n:2 * n]
        send_sems, recv_sems, local_sems = refs[2 * n:]
        x, y, c = _my_pos()
        sib = (x, y, 1 - c)
        locs, sends = [], []
        for a in range(n):
            loc = pltpu.make_async_copy(ins[a], outs[a].at[c], local_sems.at[a])
            loc.start()
            locs.append(loc)
            cp = pltpu.make_async_remote_copy(src_ref=ins[a], dst_ref=outs[a].at[c], send_sem=send_sems.at[a], recv_sem=recv_sems.at[a],
                                              device_id=sib, device_id_type=pl.DeviceIdType.MESH)
            cp.start()
            sends.append(cp)
        for a in range(n):
            pltpu.make_async_remote_copy(src_ref=ins[a], dst_ref=outs[a].at[1 - c], send_sem=send_sems.at[a], recv_sem=recv_sems.at[a],
                                         device_id=sib, device_id_type=pl.DeviceIdType.MESH).wait_recv()
        for cp in sends:
            cp.wait_send()
        for loc in locs:
            loc.wait()

    return pl.pallas_call(
        body, out_shape=[jax.ShapeDtypeStruct((2,) + h.shape, h.dtype) for h in halves],
        in_specs=[ANY] * n, out_specs=[ANY] * n,
        scratch_shapes=[pltpu.SemaphoreType.DMA((n,)), pltpu.SemaphoreType.DMA((n,)), pltpu.SemaphoreType.DMA((n,))], name=name,
    )(*halves)


def _add_my_half(g, recv, c_idx, name):
    _, r, cc = g.shape
    rh = r // 2
    tc = _pick(cc, 512)

    def body(c_ref, g_ref, r_ref, o_ref):
        o_ref[...] = g_ref[...] + r_ref[...]

    grid_spec = pltpu.PrefetchScalarGridSpec(
        num_scalar_prefetch=1, grid=(4, cc // tc),
        in_specs=[pl.BlockSpec((None, rh, tc), lambda s, j, c_ref: (s, c_ref[0], j)),
                  pl.BlockSpec((None, rh, tc), lambda s, j, c_ref: (s, 0, j))],
        out_specs=pl.BlockSpec((None, rh, tc), lambda s, j, c_ref: (s, 0, j)))
    return pl.pallas_call(body, grid_spec=grid_spec, out_shape=jax.ShapeDtypeStruct((4, rh, cc), F32), name=name,
                          compiler_params=_cparams(("arbitrary", "arbitrary"), 6 * _nbytes((rh, tc), F32)))(c_idx, g, recv)


def _sum_slots(parts, name):
    _, rh, cc = parts.shape
    tc = _pick(cc, 512)

    def body(p_ref, o_ref):
        o_ref[...] = ((p_ref[0] + p_ref[1]) + p_ref[2]) + p_ref[3]

    return pl.pallas_call(
        body, grid=(cc // tc,), in_specs=[pl.BlockSpec((4, rh, tc), lambda j: (0, 0, j))],
        out_specs=pl.BlockSpec((rh, tc), lambda j: (0, j)), out_shape=jax.ShapeDtypeStruct((rh, cc), F32), name=name,
        compiler_params=_cparams(("arbitrary",), 10 * _nbytes((rh, tc), F32)))(parts)


def _adamw_math(w, g, m, v):
    m = ADAM_B1 * m + (1.0 - ADAM_B1) * g
    v = ADAM_B2 * v + (1.0 - ADAM_B2) * (g * g)
    m_hat = m / (1.0 - ADAM_B1 ** ADAM_STEP)
    v_hat = v / (1.0 - ADAM_B2 ** ADAM_STEP)
    delta = -ADAM_LR * (m_hat / (jnp.sqrt(v_hat) + ADAM_EPS) + ADAM_WD * w)
    return delta, m, v


def _adamw(w, g, m, v, name):
    r, cc = w.shape
    tr = _pick(r, max(8, (512 * 1024) // max(cc, 1)), 8)

    def body(w_ref, g_ref, m_ref, v_ref, d_ref, nm_ref, nv_ref):
        d, nm, nv = _adamw_math(w_ref[...], g_ref[...], m_ref[...], v_ref[...])
        d_ref[...] = d
        nm_ref[...] = nm
        nv_ref[...] = nv

    spec = pl.BlockSpec((tr, cc), lambda i: (i, 0))
    return pl.pallas_call(
        body, grid=(r // tr,), in_specs=[spec] * 4, out_specs=[spec] * 3,
        out_shape=[jax.ShapeDtypeStruct((r, cc), F32)] * 3, name=name,
        compiler_params=_cparams(("arbitrary",), 16 * _nbytes((tr, cc), F32)))(w, g, m, v)


def _sum8(allv, name):
    r8, w = allv.shape
    r = r8 // 8

    def body(a_ref, o_ref):
        acc = a_ref[0:r, :]
        for d in range(1, 8):
            acc = acc + a_ref[d * r:(d + 1) * r, :]
        o_ref[...] = acc

    return pl.pallas_call(body, out_shape=jax.ShapeDtypeStruct((r, w), F32), name=name,
                          compiler_params=_cparams((), 12 * _nbytes((r8, w), F32)))(allv)


def _dmod_prep(alld, name):
    _, w = alld.shape
    tw = _pick(w, 2048)

    def body(a_ref, d_ref, b_ref):
        ctx = a_ref[1:2, :]
        tot = a_ref[0:1, :]
        d_ref[0:1, :] = tot
        for d in range(1, 8):
            ctx = ctx + a_ref[8 * d + 1:8 * d + 2, :]
            tot = tot + a_ref[8 * d:8 * d + 1, :]
            d_ref[d:d + 1, :] = a_ref[8 * d:8 * d + 1, :]
        d_ref[8:9, :] = ctx
        d_ref[9:16, :] = jnp.zeros((7, tw), F32)
        b_ref[...] = tot + ctx

    return pl.pallas_call(
        body, grid=(w // tw,), in_specs=[pl.BlockSpec((64, tw), lambda j: (0, j))],
        out_specs=[pl.BlockSpec((16, tw), lambda j: (0, j)), pl.BlockSpec((1, tw), lambda j: (0, j))],
        out_shape=[jax.ShapeDtypeStruct((16, w), F32), jax.ShapeDtypeStruct((1, w), F32)], name=name,
        compiler_params=_cparams(("arbitrary",), 0))(alld)


def _pack(arrs, rows=8):
    flat = jnp.concatenate([a.reshape(-1).astype(F32) for a in arrs])
    per = rows * LANES
    n = flat.shape[0]
    padded = -(-n // per) * per
    flat = jnp.pad(flat, (0, padded - n))
    return flat.reshape(rows, padded // rows)


def _unpack(slab, shapes):
    flat = slab.reshape(-1)
    out, off = [], 0
    for s in shapes:
        n = math.prod(s)
        out.append(flat[off:off + n].reshape(s))
        off += n
    return out


def kernel(x, c, ctx, c_ctx, ln_in_g, ln_in_b, w_mod, b_mod, w_in, w_qkv_conv, a_log_f, dt_bias_f, a_log_b, dt_bias_b, dn_norm_g, conf_dw_w, conf_dw_b, conf_ln_g, conf_ln_b, w_out, ln1_g, ln1_b, w_mlp1, b_mlp1, w_mlp2, b_mlp2, ln2_g, ln2_b, loss_target, m_c_ctx, m_ln_in_g, m_ln_in_b, m_w_mod, m_b_mod, m_w_in, m_w_qkv_conv, m_a_log_f, m_dt_bias_f, m_a_log_b, m_dt_bias_b, m_dn_norm_g, m_conf_dw_w, m_conf_dw_b, m_conf_ln_g, m_conf_ln_b, m_w_out, m_ln1_g, m_ln1_b, m_w_mlp1, m_b_mlp1, m_w_mlp2, m_b_mlp2, m_ln2_g, m_ln2_b, v_c_ctx, v_ln_in_g, v_ln_in_b, v_w_mod, v_b_mod, v_w_in, v_w_qkv_conv, v_a_log_f, v_dt_bias_f, v_a_log_b, v_dt_bias_b, v_dn_norm_g, v_conf_dw_w, v_conf_dw_b, v_conf_ln_g, v_conf_ln_b, v_w_out, v_ln1_g, v_ln1_b, v_w_mlp1, v_b_mlp1, v_w_mlp2, v_b_mlp2, v_ln2_g, v_ln2_b):
    n_lat, d = x.shape[1], x.shape[2]
    n_ctx = ctx.shape[1]
    nt = n_lat + n_ctx
    dn = d // 2
    nh = dn // HD
    conf = d - dn
    half = conf // 2
    gw = math.isqrt(n_lat)
    ff = 4 * w_mlp1.shape[2]
    fl = w_mlp1.shape[2]
    in_cols = 4 * w_in.shape[2]
    r_in = w_in.shape[2]
    n_gate = 4 * nh
    gb = max(LANES, d // 8)
    g0 = 4 * dn + 2 * conf
    in_pad = g0 + gb
    ml = w_mod.shape[2]
    assert gw * gw == n_lat and nt % gw == 0 and in_cols == 4 * dn + n_gate + 2 * conf and nh * HD == dn
    hb = 4 if nh % 4 == 0 else (2 if nh % 2 == 0 else 1)

    mx, my, mc = _my_pos()
    chip = 2 * mx + my
    dev = 4 * mx + 2 * my + mc
    c_idx = jnp.reshape(mc, (1,)).astype(jnp.int32)

    x2, ctx2, tgt2 = x[0], ctx[0], loss_target[0]
    row = lambda a: a.reshape(1, -1)

    e0_shapes = [(d,), w_qkv_conv.shape[1:], conf_dw_w.shape[1:]]
    e0 = _bcast8(_pack([c[0], w_qkv_conv[0], conf_dw_w[0]]), "bcast_inputs")
    e0 = e0.reshape(8, -1)
    per_dev = [_unpack(e0[dd], e0_shapes) for dd in range(8)]
    all_c = jnp.stack([p[0] for p in per_dev])
    w_conv = jnp.concatenate([per_dev[2 * j][1] for j in range(4)], axis=1)
    w_dw = jnp.concatenate([per_dev[2 * j][2] for j in range(4)], axis=1)

    s16 = jnp.concatenate([all_c, c_ctx[None, :], jnp.zeros((7, d), F32)], axis=0)
    b_mod_loc = lax.dynamic_slice(b_mod, (0, chip * ml), (1, ml))
    to_mxu = lambda t: t.astype(MXU_DTYPE)
    modblk = _matmul(s16, w_mod[0], form="nn", m=16, n=ml, k=d, tm=16, tn=_pick(ml, 1024), tk=_pick(d, 1024),
                     out_dtype=F32, name="mod_fwd", bias=b_mod_loc, a_fn=lambda t: to_mxu(_silu(t)), b_fn=to_mxu)
    allmod = _bcast8(modblk, "bcast_mod").reshape(8, 16, ml)[0::2]
    mod_mine = lax.dynamic_index_in_dim(allmod, dev, axis=1, keepdims=False).reshape(1, 4 * ml)
    mod_ctx = allmod[:, 8, :].reshape(1, 4 * ml)
    sh_a, sc_a, g_a, sh_m, sc_m, g_m = [mod_mine[:, i * d:(i + 1) * d] for i in range(6)]
    csh_a, csc_a = mod_ctx[:, 0:d], mod_ctx[:, d:2 * d]

    win_t_loc = jnp.transpose(w_in[0]).astype(MXU_DTYPE)
    gw_in, gw_out, gw_1, gw_2 = _chip_gather(
        [win_t_loc, w_out[0].astype(MXU_DTYPE), w_mlp1[0].astype(MXU_DTYPE), w_mlp2[0].astype(MXU_DTYPE)], "gather_weights")
    win_t = gw_in.reshape(in_cols, d)
    win_t = jnp.concatenate([win_t[:4 * dn], win_t[4 * dn + n_gate:], win_t[4 * dn:4 * dn + n_gate],
                             jnp.zeros((gb - n_gate, d), MXU_DTYPE)], axis=0)
    wout_full = gw_out.reshape(d, d)
    w2_full = gw_2.reshape(ff, d)
    w1_sh = gw_1

    tmr = _pick(n_lat, 128, 8)
    tmc = _pick(n_ctx, 128, 8)
    ln_in = [row(ln_in_g), row(ln_in_b)]

    def stage_a(rv, pv):
        x0 = _ln(rv[0], pv[0], pv[1])
        return [x0 * (1.0 + pv[3]) + pv[2]], []

    xm_lat, = _rowwise(stage_a, [(x2, d, 0, 0)], ln_in + [sh_a, sc_a], [(d, MXU_DTYPE)], [], n_rows=n_lat, tm=tmr, name="ln_mod_lat")
    xm_ctx, = _rowwise(stage_a, [(ctx2, d, 0, 0)], ln_in + [csh_a, csc_a], [(d, MXU_DTYPE)], [], n_rows=n_ctx, tm=tmc, name="ln_mod_ctx")
    xm = jnp.concatenate([xm_lat, xm_ctx], axis=0)

    tm_nt = _pick(nt, 1280, 16)
    tn_in = _pick(in_pad, 1280)
    tk_d = _pick(d, 1024)
    h = _matmul(xm, win_t, form="nt", m=nt, n=in_pad, k=d, tm=tm_nt, tn=tn_in, tk=tk_d, out_dtype=F32, name="in_proj")

    act = _conv7_fwd(h, w_conv, n_lat=n_lat, n_ctx=n_ctx, nh=nh)
    gates_hm = jnp.transpose(h[:, g0:g0 + n_gate].reshape(nt, 4, nh), (2, 0, 1))
    bc = lambda a: jnp.broadcast_to(a.reshape(nh, 1, 1), (nh, 8, LANES))
    dk = dict(n_lat=n_lat, n_ctx=n_ctx, nh=nh, hb=hb)
    o_f, st_f = _delta_fwd(act, gates_hm, bc(a_log_f), bc(dt_bias_f), rev=False, **dk)
    o_b, st_b = _delta_fwd(act, gates_hm, bc(a_log_b), bc(dt_bias_b), rev=True, **dk)

    h3 = h.reshape(nt // gw, gw, in_pad)
    ck = dict(gw=gw, col0=4 * dn, half=half)
    yh = _conf_conv_fwd(h3, w_dw, vertical=False, **ck).reshape(n_lat, half)
    yv = _conf_conv_fwd(h3, w_dw, vertical=True, **ck).reshape(n_lat, half)

    def mix_fn(o_heads, z_heads, ya, yb, dng, ba, bb, ga, gbb, la, lb):
        outs = []
        for o, z in zip(o_heads, z_heads):
            ms = jnp.mean(o * o, axis=-1, keepdims=True)
            outs.append(o * lax.rsqrt(ms + 1e-6) * dng * _silu(z))
        ya, yb = ya + ba, yb + bb
        mu = (jnp.sum(ya, axis=-1, keepdims=True) + jnp.sum(yb, axis=-1, keepdims=True)) / conf
        ya, yb = ya - mu, yb - mu
        var = (jnp.sum(ya * ya, axis=-1, keepdims=True) + jnp.sum(yb * yb, axis=-1, keepdims=True)) / conf
        rs = lax.rsqrt(var + LN_EPS)
        ca, cb = ya * rs * ga + la, yb * rs * gbb + lb
        return outs, _silu(ca), _silu(cb)

    def heads(t):
        return [t[:, i * HD:(i + 1) * HD] for i in range(nh)]

    def halves(p):
        return p[:, :half], p[:, half:]

    mix_params = [row(dn_norm_g), row(conf_dw_b), row(conf_ln_g), row(conf_ln_b)]

    def mix_args(rv, pv):
        (ba, bb), (ga, gbb), (la, lb) = halves(pv[1]), halves(pv[2]), halves(pv[3])
        return (heads(rv[0] + rv[1]), heads(rv[2]), rv[3], rv[4], pv[0], ba, bb, ga, gbb, la, lb)

    def stage_e(rv, pv):
        outs, ca, cb = mix_fn(*mix_args(rv, pv))
        return [jnp.concatenate(outs + [ca, cb], axis=-1)], []

    mix_rows = [(o_f, dn, 0, 0), (o_b, dn, 0, 0), (h, dn, 3, 0), (yh, half, 0, 0), (yv, half, 0, 0)]
    mix, = _rowwise(stage_e, mix_rows, mix_params, [(d, MXU_DTYPE)], [], n_rows=n_lat, tm=tmr, name="mixer_fwd")

    tm_l = _pick(n_lat, 1024, 16)
    tn_d = _pick(d, 1024)
    y1 = _matmul(mix, wout_full, form="nn", m=n_lat, n=d, k=d, tm=tm_l, tn=tn_d, tk=tk_d, out_dtype=F32, name="out_proj")

    def res1(x0, y1v, ga, l1g, l1b, shm, scm):
        x1 = _ln(ALPHA * x0 + ga * y1v, l1g, l1b)
        return x1, x1 * (1.0 + scm) + shm

    f_params = ln_in + [g_a, row(ln1_g), row(ln1_b), sh_m, sc_m]

    def stage_f(rv, pv):
        x1, u = res1(_ln(rv[0], pv[0], pv[1]), rv[1], *pv[2:])
        return [x1, u], []

    x1, u = _rowwise(stage_f, [(x2, d, 0, 0), (y1, d, 0, 0)], f_params, [(d, F32), (d, MXU_DTYPE)], [], n_rows=n_lat, tm=tmr, name="res1_fwd")

    tn_f = _pick(fl, 1024)
    relu_h = _matmul(u, w1_sh, form="nn", m=n_lat, n=ff, k=d, tm=tm_l, tn=tn_f, tk=tk_d, out_dtype=MXU_DTYPE, name="mlp1",
                     bias=row(b_mlp1), epi=lambda r: jnp.maximum(r, 0.0), b_split=fl)
    sq = lambda t: (t.astype(F32) * t.astype(F32)).astype(MXU_DTYPE)
    tk_f = _pick(ff, 1024)
    y2 = _matmul(relu_h, w2_full, form="nn", m=n_lat, n=d, k=ff, tm=tm_l, tn=tn_d, tk=tk_f, out_dtype=F32, name="mlp2",
                 bias=row(b_mlp2), a_fn=sq)

    def loss_fn(x1v, y2v, gm, l2g, l2b, tgt):
        x2v = _ln(ALPHA * x1v + gm * y2v, l2g, l2b)
        return (0.5 / d) * jnp.sum(jnp.square(x2v - tgt))

    def stage_g(rv, pv):
        loss, grads = jax.value_and_grad(loss_fn, argnums=(0, 1, 2, 3, 4))(rv[0], rv[1], pv[0], pv[1], pv[2], rv[2])
        dx1, dy2r, dgm, dl2g, dl2b = grads
        dy2 = dy2r
        return [dx1, dy2], [jnp.reshape(loss, (1, 1)), dgm, dl2g, dl2b, jnp.sum(dy2, axis=0, keepdims=True)]

    dx1p, dy2, loss_acc, d_g_m, d_ln2_g, d_ln2_b, d_b_mlp2 = _rowwise(
        stage_g, [(x1, d, 0, 0), (y2, d, 0, 0), (tgt2, d, 0, 0)], [g_m, row(ln2_g), row(ln2_b)],
        [(d, F32), (d, MXU_DTYPE)], [LANES, d, d, d, d], n_rows=n_lat, tm=tmr, name="loss_res2_bwd")

    dhid, d_b_mlp1 = _matmul(dy2, w2_full, form="nt", m=n_lat, n=ff, k=d, tm=tm_l, tn=tn_f, tk=tk_d, out_dtype=MXU_DTYPE, name="mlp2_bwd_x",
                             epi=lambda r, rh: r * (2.0 * rh.astype(F32)), epi_in=relu_h, colsum=True)
    tk_l = _pick(n_lat, 1024, 16)
    d_w2 = _matmul(relu_h, dy2, form="tn", m=ff, n=d, k=n_lat, tm=tn_f, tn=tn_d, tk=tk_l, out_dtype=F32, name="mlp2_bwd_w", a_fn=sq)
    du = _matmul(dhid, w1_sh, form="nt", m=n_lat, n=d, k=ff, tm=tm_l, tn=tn_d, tk=tn_f, out_dtype=F32, name="mlp1_bwd_x", b_split=fl)
    d_w1 = _matmul(u, dhid, form="tn", m=d, n=ff, k=n_lat, tm=tn_d, tn=tn_f, tk=tk_l, out_dtype=F32, name="mlp1_bwd_w", o_split=fl)

    def stage_h(rv, pv):
        x0 = _ln(rv[0], pv[0], pv[1])
        _, vjp = jax.vjp(res1, x0, rv[1], *pv[2:])
        dx0, dy1r, dga, dl1g, dl1b, dshm, dscm = vjp((rv[3], rv[2]))
        return [dx0, dy1r], [dga, dl1g, dl1b, dshm, dscm]

    dx0p, dy1, d_g_a, d_ln1_g, d_ln1_b, d_sh_m, d_sc_m = _rowwise(
        stage_h, [(x2, d, 0, 0), (y1, d, 0, 0), (du, d, 0, 0), (dx1p, d, 0, 0)], f_params,
        [(d, F32), (d, MXU_DTYPE)], [d, d, d, d, d], n_rows=n_lat, tm=tmr, name="res1_bwd")

    dmix = _matmul(dy1, wout_full, form="nt", m=n_lat, n=d, k=d, tm=tm_l, tn=tn_d, tk=tk_d, out_dtype=F32, name="out_proj_bwd_x")
    d_wout = _matmul(mix, dy1, form="tn", m=d, n=d, k=n_lat, tm=tn_d, tn=tn_d, tk=tk_l, out_dtype=F32, name="out_proj_bwd_w")

    def stage_i(rv, pv):
        args = mix_args(rv, pv)
        _, vjp = jax.vjp(mix_fn, *args)
        dm = rv[5]
        d_outs = [dm[:, i * HD:(i + 1) * HD] for i in range(nh)]
        dca, dcb = dm[:, dn:dn + half], dm[:, dn + half:]
        do_h, dz_h, dya, dyb, ddng, dba, dbb, dga, dgbb, dla, dlb = vjp((d_outs, dca, dcb))
        cat = lambda *p: jnp.concatenate(p, axis=-1)
        return ([cat(*do_h), cat(*dz_h), dya, dyb], [ddng, cat(dba, dbb), cat(dga, dgbb), cat(dla, dlb)])

    do, dz, dyh, dyv, d_dn_norm_g, d_conf_dw_b, d_conf_ln_g, d_conf_ln_b = _rowwise(
        stage_i, mix_rows + [(dmix, d, 0, 0)], mix_params, [(dn, F32), (dn, MXU_DTYPE), (half, F32), (half, F32)],
        [HD, conf, conf, conf], n_rows=n_lat, tm=tmr, name="mixer_bwd")

    dval_h, dgate_h, d_wdw_h = _conf_conv_bwd(h3, w_dw, dyh.reshape(gw, gw, half), vertical=False, **ck)
    dval_v, dgate_v, d_wdw_v = _conf_conv_bwd(h3, w_dw, dyv.reshape(gw, gw, half), vertical=True, **ck)
    dact_f, dgt_f, d_alog_f, d_dt_f = _delta_bwd(act, gates_hm, bc(a_log_f), bc(dt_bias_f), st_f, do, rev=False, **dk)
    dact_b, dgt_b, d_alog_b, d_dt_b = _delta_bwd(act, gates_hm, bc(a_log_b), bc(dt_bias_b), st_b, do, rev=True, **dk)
    dh_qkv, d_wconv = _conv7_bwd(h, w_conv, dact_f, dact_b, n_lat=n_lat, n_ctx=n_ctx, nh=nh)

    dgates = jnp.stack([dgt_f[..., 0], dgt_f[..., 1], dgt_b[..., 0], dgt_b[..., 1]], axis=0)
    dgates = jnp.transpose(dgates, (2, 0, 1)).reshape(nt, n_gate)
    dgates = jnp.pad(dgates, ((0, 0), (0, gb - n_gate))).astype(MXU_DTYPE)
    zrows = lambda t: jnp.pad(t, ((0, n_ctx), (0, 0)))
    dh = jnp.concatenate([dh_qkv, zrows(dz), zrows(dval_h.reshape(n_lat, half)), zrows(dval_v.reshape(n_lat, half)),
                          zrows(dgate_h.reshape(n_lat, half)), zrows(dgate_v.reshape(n_lat, half)), dgates], axis=1)

    tk_in = tn_in
    dxm = _matmul(dh, win_t, form="nn", m=nt, n=d, k=in_pad, tm=tm_nt, tn=tn_d, tk=tk_in, out_dtype=F32, name="in_proj_bwd_x")
    tk_nt = _pick(nt, 1280, 16)
    d_win_t = _matmul(dh, xm, form="tn", m=in_pad, n=d, k=nt, tm=tn_in, tn=tn_d, tk=tk_nt, out_dtype=F32, name="in_proj_bwd_w")
    d_win_t = jnp.concatenate([d_win_t[:4 * dn], d_win_t[g0:g0 + n_gate], d_win_t[4 * dn:g0]], axis=0)

    def mod_in(xr, lg, lb, sh, sc):
        x0 = _ln(xr, lg, lb)
        return x0, x0 * (1.0 + sc) + sh

    def stage_j(rv, pv):
        _, vjp = jax.vjp(mod_in, rv[0], *pv)
        dx0 = rv[2] if len(rv) > 2 else jnp.zeros_like(rv[1])
        dxr, dlg, dlb, dsh, dsc = vjp((dx0, rv[1]))
        return [dxr], [dlg, dlb, dsh, dsc]

    grad_x, dlg_l, dlb_l, d_sh_a, d_sc_a = _rowwise(
        stage_j, [(x2, d, 0, 0), (dxm, d, 0, 0), (dx0p, d, 0, 0)], ln_in + [sh_a, sc_a], [(d, F32)], [d, d, d, d],
        n_rows=n_lat, tm=tmr, name="ln_mod_bwd_lat")
    _, dlg_c, dlb_c, d_csh_a, d_csc_a = _rowwise(
        stage_j, [(ctx2, d, 0, 0), (dxm, d, 0, n_lat // tmc)], ln_in + [csh_a, csc_a], [(d, F32)], [d, d, d, d],
        n_rows=n_ctx, tm=tmc, name="ln_mod_bwd_ctx")

    zd = jnp.zeros((1, d), F32)
    dmod_rows = jnp.concatenate([jnp.concatenate([d_sh_a, d_sc_a, d_g_a, d_sh_m, d_sc_m, d_g_m], axis=1),
                                 jnp.concatenate([d_csh_a, d_csc_a, zd, zd, zd, zd], axis=1), jnp.zeros((6, 6 * d), F32)], axis=0)
    d16, g_b_mod = _dmod_prep(_bcast8(dmod_rows, "bcast_dmod"), "dmod_prep")
    d16_loc = lax.dynamic_slice(d16, (0, chip * ml), (16, ml))
    g_w_mod = _matmul(s16, d16_loc, form="tn", m=d, n=ml, k=16, tm=_pick(d, 512), tn=_pick(ml, 1024), tk=16, out_dtype=F32,
                      name="mod_bwd_w", a_fn=_silu, kind="hi")
    dsilu = _matmul(d16_loc, w_mod[0], form="nt", m=16, n=d, k=ml, tm=16, tn=_pick(d, 1024), tk=_pick(ml, 1024), out_dtype=F32,
                    name="mod_bwd_c", a_fn=to_mxu, b_fn=to_mxu)
    d_cctx_part = dsilu[8:9, :] * (1 - mc).astype(F32)

    head_sum = lambda t: t[:, 0, 0]
    small = [d_cctx_part, dlg_l + dlg_c, dlb_l + dlb_c, d_wconv, head_sum(d_alog_f), head_sum(d_dt_f), head_sum(d_alog_b),
             head_sum(d_dt_b), d_dn_norm_g, jnp.concatenate([d_wdw_h, d_wdw_v], axis=1), d_conf_dw_b, d_conf_ln_g, d_conf_ln_b,
             d_ln1_g, d_ln1_b, d_b_mlp1, d_b_mlp2, d_ln2_g, d_ln2_b]
    small_shapes = [(d,), (d,), (d,), (SHORT_CONV, 3 * dn), (nh,), (nh,), (nh,), (nh,), (HD,), (CONF_K, conf), (conf,), (conf,), (conf,),
                    (d,), (d,), (ff,), (d,), (d,), (d,)]
    ssum = _sum8(_bcast8(_pack(small), "bcast_small_grads"), "sum_small_grads")
    (t_cctx, g_ln_in_g, g_ln_in_b, g_wconv_full, g_a_log_f, g_dt_bias_f, g_a_log_b, g_dt_bias_b, g_dn_norm_g, g_wdw_full, g_conf_dw_b,
     g_conf_ln_g, g_conf_ln_b, g_ln1_g, g_ln1_b, g_b_mlp1, g_b_mlp2, g_ln2_g, g_ln2_b) = _unpack(ssum, small_shapes)
    sg = jax.nn.sigmoid(c_ctx)
    g_c_ctx = t_cctx * (sg * (1.0 + c_ctx * (1.0 - sg)))
    g_w_qkv_conv = lax.dynamic_slice(g_wconv_full, (0, chip * w_qkv_conv.shape[2]), w_qkv_conv.shape[1:])
    g_conf_dw_w = lax.dynamic_slice(g_wdw_full, (0, chip * conf_dw_w.shape[2]), conf_dw_w.shape[1:])

    big = [d_win_t.reshape(4, r_in, d), d_wout.reshape(4, d // 4, d), d_w1, d_w2.reshape(4, fl, d)]
    recv1 = _pair_send_halves(big, "grads_pair_send")
    chipsum = [_add_my_half(g, r, c_idx, f"grads_pair_add_{i}") for i, (g, r) in enumerate(zip(big, recv1))]
    recv3 = _chip_scatter(chipsum, "grads_chip_scatter")
    mine = [_sum_slots(p, f"grads_chip_sum_{i}") for i, p in enumerate(recv3)]
    joined = _pair_join_halves(mine, "grads_pair_join")
    g_win_t, g_w_out, g_w_mlp1, g_w_mlp2 = [j.reshape(j.shape[0] * j.shape[1], j.shape[2]) for j in joined]
    g_w_in = jnp.transpose(g_win_t)

    grads = dict(c_ctx=g_c_ctx, ln_in_g=g_ln_in_g, ln_in_b=g_ln_in_b, w_mod=g_w_mod[None], b_mod=g_b_mod, w_in=g_w_in[None],
                 w_qkv_conv=g_w_qkv_conv[None], a_log_f=g_a_log_f[None], dt_bias_f=g_dt_bias_f[None], a_log_b=g_a_log_b[None],
                 dt_bias_b=g_dt_bias_b[None], dn_norm_g=g_dn_norm_g[None], conf_dw_w=g_conf_dw_w[None], conf_dw_b=g_conf_dw_b[None],
                 conf_ln_g=g_conf_ln_g[None], conf_ln_b=g_conf_ln_b[None], w_out=g_w_out[None], ln1_g=g_ln1_g[None], ln1_b=g_ln1_b[None],
                 w_mlp1=g_w_mlp1[None], b_mlp1=g_b_mlp1[None], w_mlp2=g_w_mlp2[None], b_mlp2=g_b_mlp2[None], ln2_g=g_ln2_g[None],
                 ln2_b=g_ln2_b[None])
    weights = dict(c_ctx=c_ctx, ln_in_g=ln_in_g, ln_in_b=ln_in_b, w_mod=w_mod, b_mod=b_mod, w_in=w_in, w_qkv_conv=w_qkv_conv,
                   a_log_f=a_log_f, dt_bias_f=dt_bias_f, a_log_b=a_log_b, dt_bias_b=dt_bias_b, dn_norm_g=dn_norm_g, conf_dw_w=conf_dw_w,
                   conf_dw_b=conf_dw_b, conf_ln_g=conf_ln_g, conf_ln_b=conf_ln_b, w_out=w_out, ln1_g=ln1_g, ln1_b=ln1_b, w_mlp1=w_mlp1,
                   b_mlp1=b_mlp1, w_mlp2=w_mlp2, b_mlp2=b_mlp2, ln2_g=ln2_g, ln2_b=ln2_b)
    m_in = dict(c_ctx=m_c_ctx, ln_in_g=m_ln_in_g, ln_in_b=m_ln_in_b, w_mod=m_w_mod, b_mod=m_b_mod, w_in=m_w_in, w_qkv_conv=m_w_qkv_conv,
                a_log_f=m_a_log_f, dt_bias_f=m_dt_bias_f, a_log_b=m_a_log_b, dt_bias_b=m_dt_bias_b, dn_norm_g=m_dn_norm_g,
                conf_dw_w=m_conf_dw_w, conf_dw_b=m_conf_dw_b, conf_ln_g=m_conf_ln_g, conf_ln_b=m_conf_ln_b, w_out=m_w_out, ln1_g=m_ln1_g,
                ln1_b=m_ln1_b, w_mlp1=m_w_mlp1, b_mlp1=m_b_mlp1, w_mlp2=m_w_mlp2, b_mlp2=m_b_mlp2, ln2_g=m_ln2_g, ln2_b=m_ln2_b)
    v_in = dict(c_ctx=v_c_ctx, ln_in_g=v_ln_in_g, ln_in_b=v_ln_in_b, w_mod=v_w_mod, b_mod=v_b_mod, w_in=v_w_in, w_qkv_conv=v_w_qkv_conv,
                a_log_f=v_a_log_f, dt_bias_f=v_dt_bias_f, a_log_b=v_a_log_b, dt_bias_b=v_dt_bias_b, dn_norm_g=v_dn_norm_g,
                conf_dw_w=v_conf_dw_w, conf_dw_b=v_conf_dw_b, conf_ln_g=v_conf_ln_g, conf_ln_b=v_conf_ln_b, w_out=v_w_out, ln1_g=v_ln1_g,
                ln1_b=v_ln1_b, w_mlp1=v_w_mlp1, b_mlp1=v_b_mlp1, w_mlp2=v_w_mlp2, b_mlp2=v_b_mlp2, ln2_g=v_ln2_g, ln2_b=v_ln2_b)
    names = list(weights)
    big_names = ("w_mod", "w_in", "w_out", "w_mlp1", "w_mlp2")
    delta, new_m, new_v = {}, {}, {}
    for nm in big_names:
        w2d = weights[nm][0]
        dl, mm, vv = _adamw(w2d, grads[nm][0], m_in[nm][0], v_in[nm][0], f"adamw_{nm}")
        delta[nm], new_m[nm], new_v[nm] = dl[None], mm[None], vv[None]
    small_names = [nm for nm in names if nm not in big_names]
    shapes = [weights[nm].shape for nm in small_names]
    grads = {nm: (grads[nm].reshape(weights[nm].shape) if nm in small_names else grads[nm]) for nm in names}
    packed = [_pack([src[nm] for nm in small_names]) for src in (weights, grads, m_in, v_in)]
    dl, mm, vv = _adamw(*packed, "adamw_small")
    for nm, a, b_, c_ in zip(small_names, _unpack(dl, shapes), _unpack(mm, shapes), _unpack(vv, shapes)):
        delta[nm], new_m[nm], new_v[nm] = a, b_, c_

    loss = lax.psum(loss_acc[0, 0], MESH_AXES)
    return (loss, grad_x[None], *[grads[nm] for nm in names], *[delta[nm] for nm in names],
            *[new_m[nm] for nm in names], *[new_v[nm] for nm in names])
```

```python
import functools
import math

import jax
import jax.numpy as jnp
from jax import lax
from jax.experimental import pallas as pl
from jax.experimental.pallas import tpu as pltpu

F32 = jnp.float32
BF16 = jnp.bfloat16
MXU_DTYPE = BF16
WIRE_DTYPE = BF16
HIGHEST = lax.Precision.HIGHEST

HD = 128
CHUNK = 128
SHORT_CONV = 7
CONF_K = 31
ALPHA = 2.0 ** 0.25
LN_EPS = 1e-5
V7X_VMEM_BYTES = 64 * 1024 * 1024
LANES = 128

ADAM_LR = 0.001
ADAM_B1 = 0.9
ADAM_B2 = 0.999
ADAM_EPS = 1e-08
ADAM_WD = 0.01
ADAM_STEP = 10

MESH_AXES = ("x", "y", "c")
ANY = pl.BlockSpec(memory_space=pl.ANY)


def _pick(dim, pref, mult=LANES):
    best = None
    t = mult
    while t <= min(dim, pref):
        if dim % t == 0:
            best = t
        t += mult
    return best if best is not None else dim


def _cparams(sem, vmem_est):
    limit = int(min(V7X_VMEM_BYTES - 6 * 1024 * 1024, max(32 * 1024 * 1024, vmem_est + 8 * 1024 * 1024)))
    return pltpu.CompilerParams(dimension_semantics=sem, vmem_limit_bytes=limit)


def _nbytes(shape, dtype):
    return math.prod(shape) * jnp.dtype(dtype).itemsize


def _ln(x, g, b):
    mu = jnp.mean(x, axis=-1, keepdims=True)
    xc = x - mu
    var = jnp.mean(xc * xc, axis=-1, keepdims=True)
    return xc * lax.rsqrt(var + LN_EPS) * g + b


def _silu(x):
    return x * jax.nn.sigmoid(x)


def _softplus(x):
    return jnp.maximum(x, 0.0) + jnp.log1p(jnp.exp(-jnp.abs(x)))


_DIMS = {"nn": (((1,), (0,)), ((), ())), "nt": (((1,), (1,)), ((), ())), "tn": (((0,), (0,)), ((), ()))}


def _split_bf16(x, parts):
    out, rest = [], x.astype(F32)
    for _ in range(parts):
        bits = lax.bitcast_convert_type(rest, jnp.uint32) & jnp.uint32(0xFFFF0000)
        p = lax.bitcast_convert_type(bits, F32)
        out.append(p.astype(BF16))
        rest = rest - p
    return out


def _raw_dot(a, b, form, kind):
    if MXU_DTYPE != F32:
        if kind == "mxu":
            return lax.dot_general(a.astype(MXU_DTYPE), b.astype(MXU_DTYPE), _DIMS[form], preferred_element_type=F32)
        if kind == "hi3":
            (a0, a1), (b0, b1) = _split_bf16(a, 2), _split_bf16(b, 2)
            d = lambda p, q: lax.dot_general(p, q, _DIMS[form], preferred_element_type=F32)
            return d(a0, b0) + (d(a1, b0) + d(a0, b1))
    return lax.dot_general(a.astype(F32), b.astype(F32), _DIMS[form], precision=HIGHEST, preferred_element_type=F32)


@functools.partial(jax.custom_vjp, nondiff_argnums=(2, 3))
def _dot(a, b, form, kind):
    return _raw_dot(a, b, form, kind)


def _dot_fwd(a, b, form, kind):
    return _raw_dot(a, b, form, kind), (a, b)


def _dot_bwd(form, kind, res, dc):
    a, b = res
    if form == "nn":
        return _dot(dc, b, "nt", kind), _dot(a, dc, "tn", kind)
    if form == "nt":
        return _dot(dc, b, "nn", kind), _dot(dc, a, "tn", kind)
    return _dot(b, dc, "nt", kind), _dot(a, dc, "nn", kind)


_dot.defvjp(_dot_fwd, _dot_bwd)


def _raw_xdot(a, b, form, exact):
    if MXU_DTYPE == F32:
        return _raw_dot(a, b, form, "hi")
    d = lambda p, q: lax.dot_general(p, q, _DIMS[form], preferred_element_type=F32)
    if exact == "a":
        a16, (b0, b1, b2) = a.astype(BF16), _split_bf16(b, 3)
        return d(a16, b0) + (d(a16, b1) + d(a16, b2))
    b16, (a0, a1, a2) = b.astype(BF16), _split_bf16(a, 3)
    return d(a0, b16) + (d(a1, b16) + d(a2, b16))


@functools.partial(jax.custom_vjp, nondiff_argnums=(2, 3))
def _xdot(a, b, form, exact):
    return _raw_xdot(a, b, form, exact)


def _xdot_fwd(a, b, form, exact):
    return _raw_xdot(a, b, form, exact), (a, b)


def _xdot_bwd(form, exact, res, dc):
    a, b = res
    if form == "nn" and exact == "a":
        return jnp.zeros_like(a), _xdot(a, dc, "tn", "a")
    assert form == "tn" and exact == "b"
    return _xdot(b, dc, "nt", "a"), jnp.zeros_like(b)


_xdot.defvjp(_xdot_fwd, _xdot_bwd)


def _spec2(block, idx, split=None):
    if split is None:
        return pl.BlockSpec(tuple(block), idx)
    per = split // block[1]

    def idx3(*g):
        r, cblk = idx(*g)
        return (cblk // per, r, cblk % per)

    return pl.BlockSpec((None,) + tuple(block), idx3)


def _matmul(a, b, *, form, m, n, k, tm, tn, tk, out_dtype, name, bias=None, a_fn=None, b_fn=None,
            epi=None, epi_in=None, colsum=False, kind="mxu", b_split=None, o_split=None):
    assert m % tm == 0 and n % tn == 0 and k % tk == 0, (name, m, n, k, tm, tn, tk)
    nk = k // tk
    grid = (n // tn, m // tm, nk)
    if form == "tn":
        a_spec = pl.BlockSpec((tk, tm), lambda j, i, kk: (kk, i))
    else:
        a_spec = pl.BlockSpec((tm, tk), lambda j, i, kk: (i, kk))
    if form == "nt":
        b_spec = _spec2((tn, tk), lambda j, i, kk: (j, kk), b_split)
    else:
        b_spec = _spec2((tk, tn), lambda j, i, kk: (kk, j), b_split)
    in_specs = [a_spec, b_spec]
    operands = [a, b]
    if bias is not None:
        in_specs.append(pl.BlockSpec((1, tn), lambda j, i, kk: (0, j)))
        operands.append(bias)
    if epi_in is not None:
        in_specs.append(pl.BlockSpec((tm, tn), lambda j, i, kk: (i, j)))
        operands.append(epi_in)
    out_specs = [_spec2((tm, tn), lambda j, i, kk: (i, j), o_split)]
    if o_split is None:
        out_shape = [jax.ShapeDtypeStruct((m, n), out_dtype)]
    else:
        out_shape = [jax.ShapeDtypeStruct((n // o_split, m, o_split), out_dtype)]
    if colsum:
        out_specs.append(pl.BlockSpec((1, tn), lambda j, i, kk: (0, j)))
        out_shape.append(jax.ShapeDtypeStruct((1, n), F32))
    has_bias, has_epi_in = bias is not None, epi_in is not None

    def body(*refs):
        refs = list(refs)
        a_ref, b_ref = refs[0], refs[1]
        pos = 2
        bias_ref = epi_ref = cs_ref = None
        if has_bias:
            bias_ref = refs[pos]
            pos += 1
        if has_epi_in:
            epi_ref = refs[pos]
            pos += 1
        o_ref = refs[pos]
        pos += 1
        if colsum:
            cs_ref = refs[pos]
            pos += 1
        acc_ref = refs[pos]
        i, kk = pl.program_id(1), pl.program_id(2)

        @pl.when(kk == 0)
        def _():
            acc_ref[...] = jnp.zeros_like(acc_ref)

        av = a_ref[...]
        if a_fn is not None:
            av = a_fn(av)
        bv = b_ref[...]
        if b_fn is not None:
            bv = b_fn(bv)
        acc_ref[...] += _raw_dot(av, bv, form, kind)

        @pl.when(kk == nk - 1)
        def _():
            r = acc_ref[...]
            if has_bias:
                r = r + bias_ref[...]
            if epi is not None:
                r = epi(r, epi_ref[...]) if has_epi_in else epi(r)
            o_ref[...] = r.astype(out_dtype)
            if colsum:
                s = jnp.sum(r, axis=0, keepdims=True)

                @pl.when(i == 0)
                def _():
                    cs_ref[...] = s

                @pl.when(i > 0)
                def _():
                    cs_ref[...] += s

    est = 2 * (_nbytes((tm, tk), a.dtype) + _nbytes((tk, tn), b.dtype) + _nbytes((tm, tn), out_dtype))
    est += _nbytes((tm, tn), F32) * 2 + (2 * _nbytes((tm, tn), epi_in.dtype) if has_epi_in else 0)
    res = pl.pallas_call(
        body, grid=grid, in_specs=in_specs, out_specs=out_specs, out_shape=out_shape,
        scratch_shapes=[pltpu.VMEM((tm, tn), F32)], name=name,
        compiler_params=_cparams(("arbitrary", "arbitrary", "arbitrary"), est),
    )(*operands)
    return res if colsum else res[0]


def _rowwise(fn, rows, params, row_outs, acc_outs, *, n_rows, tm, name):
    assert n_rows % tm == 0, (name, n_rows, tm)
    nr, npar, nro, nac = len(rows), len(params), len(row_outs), len(acc_outs)
    in_specs = [pl.BlockSpec((tm, w), functools.partial(lambda i, cb, ro: (i + ro, cb), cb=cb, ro=ro))
                for (_, w, cb, ro) in rows]
    in_specs += [pl.BlockSpec((1, p.shape[1]), lambda i: (0, 0)) for p in params]
    out_specs = [pl.BlockSpec((tm, w), lambda i: (i, 0)) for (w, _) in row_outs]
    out_specs += [pl.BlockSpec((1, w), lambda i: (0, 0)) for w in acc_outs]
    out_shape = [jax.ShapeDtypeStruct((n_rows, w), dt) for (w, dt) in row_outs]
    out_shape += [jax.ShapeDtypeStruct((1, w), F32) for w in acc_outs]

    def body(*refs):
        rv = [r[...] for r in refs[:nr]]
        pv = [r[...] for r in refs[nr:nr + npar]]
        ro_refs = refs[nr + npar:nr + npar + nro]
        ac_refs = refs[nr + npar + nro:]
        ro, ac = fn(rv, pv)
        for ref, val in zip(ro_refs, ro, strict=True):
            ref[...] = val.astype(ref.dtype)
        first = pl.program_id(0) == 0
        for ref, val in zip(ac_refs, ac, strict=True):
            val = jnp.broadcast_to(val.astype(F32), ref.shape)

            @pl.when(first)
            def _(ref=ref, val=val):
                ref[...] = val

            @pl.when(jnp.logical_not(first))
            def _(ref=ref, val=val):
                ref[...] += val

    io = sum(_nbytes((tm, w), a.dtype) for (a, w, _, _) in rows) + sum(_nbytes((tm, w), dt) for (w, dt) in row_outs)
    widest = max([w for (_, w, _, _) in rows] + [w for (w, _) in row_outs])
    est = 2 * io + 12 * _nbytes((tm, widest), F32)
    return pl.pallas_call(
        body, grid=(n_rows // tm,), in_specs=in_specs, out_specs=out_specs, out_shape=out_shape, name=name,
        compiler_params=_cparams(("arbitrary",), est),
    )(*[a for (a, _, _, _) in rows], *params)


def _conv7_tiles(n_lat, n_ctx):
    tt = min(256, n_ctx)
    assert n_lat % tt == 0 and n_ctx % tt == 0
    return tt, [(t0, 8 + t0) for t0 in range(0, n_lat, tt)] + [(n_lat + t0, 16 + n_lat + t0) for t0 in range(0, n_ctx, tt)]


def _conv7_fill(p_ref, x_ref, n_lat, n_ctx):
    z8 = jnp.zeros((8, LANES), F32)
    p_ref[0:8, :] = z8
    p_ref[8:8 + n_lat, :] = x_ref[0:n_lat, :]
    p_ref[8 + n_lat:16 + n_lat, :] = z8
    p_ref[16 + n_lat:16 + n_lat + n_ctx, :] = x_ref[n_lat:n_lat + n_ctx, :]
    p_ref[16 + n_lat + n_ctx:24 + n_lat + n_ctx, :] = z8


def _conv7_fwd(h, w_conv, *, n_lat, n_ctx, nh):
    nt = n_lat + n_ctx
    tt, tiles = _conv7_tiles(n_lat, n_ctx)
    half = SHORT_CONV // 2

    def body(x_ref, w_ref, o_ref, p_ref):
        _conv7_fill(p_ref, x_ref, n_lat, n_ctx)
        for (row, prow) in tiles:
            acc = jnp.zeros((tt, LANES), F32)
            for kk in range(SHORT_CONV):
                acc = acc + w_ref[kk:kk + 1, :] * p_ref[pl.ds(prow + kk - half, tt), :]
            o_ref[pl.ds(row, tt), :] = _silu(acc)

    return pl.pallas_call(
        body, grid=(3 * nh,),
        in_specs=[pl.BlockSpec((nt, LANES), lambda j: (0, j)), pl.BlockSpec((SHORT_CONV, LANES), lambda j: (0, j))],
        out_specs=pl.BlockSpec((None, None, nt, LANES), lambda j: (j // nh, j % nh, 0, 0)),
        out_shape=jax.ShapeDtypeStruct((3, nh, nt, LANES), F32),
        scratch_shapes=[pltpu.VMEM((nt + 24, LANES), F32)], name="conv7_fwd",
        compiler_params=_cparams(("arbitrary",), 5 * _nbytes((nt + 24, LANES), F32)),
    )(h, w_conv)


def _conv7_bwd(h, w_conv, dact_f, dact_b, *, n_lat, n_ctx, nh):
    nt = n_lat + n_ctx
    tt, tiles = _conv7_tiles(n_lat, n_ctx)
    half = SHORT_CONV // 2

    def body(x_ref, w_ref, df_ref, db_ref, dx_ref, dw_ref, p_ref, q_ref):
        _conv7_fill(p_ref, x_ref, n_lat, n_ctx)
        z8 = jnp.zeros((8, LANES), F32)
        q_ref[0:8, :] = z8
        q_ref[8 + n_lat:16 + n_lat, :] = z8
        q_ref[16 + n_lat + n_ctx:24 + n_lat + n_ctx, :] = z8
        dw = [jnp.zeros((1, LANES), F32) for _ in range(SHORT_CONV)]
        for (row, prow) in tiles:
            pre = jnp.zeros((tt, LANES), F32)
            for kk in range(SHORT_CONV):
                pre = pre + w_ref[kk:kk + 1, :] * p_ref[pl.ds(prow + kk - half, tt), :]
            s = jax.nn.sigmoid(pre)
            dpre = (df_ref[pl.ds(row, tt), :] + db_ref[pl.ds(row, tt), :]) * (s * (1.0 + pre * (1.0 - s)))
            q_ref[pl.ds(prow, tt), :] = dpre
            for kk in range(SHORT_CONV):
                dw[kk] = dw[kk] + jnp.sum(dpre * p_ref[pl.ds(prow + kk - half, tt), :], axis=0, keepdims=True)
        for (row, prow) in tiles:
            acc = jnp.zeros((tt, LANES), F32)
            for kk in range(SHORT_CONV):
                acc = acc + w_ref[kk:kk + 1, :] * q_ref[pl.ds(prow + half - kk, tt), :]
            dx_ref[pl.ds(row, tt), :] = acc.astype(dx_ref.dtype)
        for kk in range(SHORT_CONV):
            dw_ref[kk:kk + 1, :] = dw[kk]

    dspec = pl.BlockSpec((None, None, nt, LANES), lambda j: (j // nh, j % nh, 0, 0))
    return pl.pallas_call(
        body, grid=(3 * nh,),
        in_specs=[pl.BlockSpec((nt, LANES), lambda j: (0, j)), pl.BlockSpec((SHORT_CONV, LANES), lambda j: (0, j)), dspec, dspec],
        out_specs=[pl.BlockSpec((nt, LANES), lambda j: (0, j)), pl.BlockSpec((SHORT_CONV, LANES), lambda j: (0, j))],
        out_shape=[jax.ShapeDtypeStruct((nt, 3 * nh * LANES), MXU_DTYPE), jax.ShapeDtypeStruct((SHORT_CONV, 3 * nh * LANES), F32)],
        scratch_shapes=[pltpu.VMEM((nt + 24, LANES), F32), pltpu.VMEM((nt + 24, LANES), F32)], name="conv7_bwd",
        compiler_params=_cparams(("arbitrary",), 10 * _nbytes((nt + 24, LANES), F32)),
    )(h, w_conv, dact_f, dact_b)


_RT = 4
_CP = CONF_K // 2


def _conf_pad_shape(gw, vertical):
    return (gw + 2 * _CP, gw, LANES) if vertical else (gw, gw + 32, LANES)


def _conf_fill(p_ref, val, gw, vertical):
    if vertical:
        p_ref[0:_CP] = jnp.zeros((_CP, gw, LANES), F32)
        p_ref[_CP + gw:2 * _CP + gw] = jnp.zeros((_CP, gw, LANES), F32)
        p_ref[_CP:_CP + gw] = val
    else:
        p_ref[:, 0:16, :] = jnp.zeros((gw, 16, LANES), F32)
        p_ref[:, 16 + gw:32 + gw, :] = jnp.zeros((gw, 16, LANES), F32)
        p_ref[:, 16:16 + gw, :] = val


def _conf_window(p_ref, r0, shift, gw, vertical):
    if vertical:
        return p_ref[pl.ds(r0 + _CP + shift, _RT), :, :]
    return p_ref[pl.ds(r0, _RT), pl.ds(16 + shift, gw), :]


def _conf_conv_fwd(h3, w_dw, *, gw, col0, half, vertical):
    cb_val = (col0 + (half if vertical else 0)) // LANES
    cb_gate = cb_val + 2 * half // LANES
    cb_w = (half if vertical else 0) // LANES

    def body(v_ref, g_ref, w_ref, o_ref, p_ref):
        _conf_fill(p_ref, v_ref[...] * jax.nn.sigmoid(g_ref[...]), gw, vertical)

        def step(t, carry):
            r0 = t * _RT
            acc = jnp.zeros((_RT, gw, LANES), F32)
            for kk in range(CONF_K):
                acc = acc + w_ref[kk:kk + 1, :] * _conf_window(p_ref, r0, kk - _CP, gw, vertical)
            o_ref[pl.ds(r0, _RT)] = acc
            return carry

        lax.fori_loop(0, gw // _RT, step, 0)

    blk = (gw, gw, LANES)
    return pl.pallas_call(
        body, grid=(half // LANES,),
        in_specs=[pl.BlockSpec(blk, lambda j: (0, 0, cb_val + j)), pl.BlockSpec(blk, lambda j: (0, 0, cb_gate + j)),
                  pl.BlockSpec((CONF_K, LANES), lambda j: (0, cb_w + j))],
        out_specs=pl.BlockSpec(blk, lambda j: (0, 0, j)),
        out_shape=jax.ShapeDtypeStruct((gw, gw, half), F32),
        scratch_shapes=[pltpu.VMEM(_conf_pad_shape(gw, vertical), F32)],
        name="conf_conv_fwd_v" if vertical else "conf_conv_fwd_h",
        compiler_params=_cparams(("arbitrary",), 8 * _nbytes(_conf_pad_shape(gw, vertical), F32)),
    )(h3, h3, w_dw)


def _conf_conv_bwd(h3, w_dw, dyc, *, gw, col0, half, vertical):
    cb_val = (col0 + (half if vertical else 0)) // LANES
    cb_gate = cb_val + 2 * half // LANES
    cb_w = (half if vertical else 0) // LANES

    def body(v_ref, g_ref, w_ref, d_ref, dv_ref, dg_ref, dw_ref, py_ref, pd_ref):
        _conf_fill(py_ref, v_ref[...] * jax.nn.sigmoid(g_ref[...]), gw, vertical)
        _conf_fill(pd_ref, d_ref[...], gw, vertical)

        def step(t, carry):
            r0 = t * _RT
            acc = jnp.zeros((_RT, gw, LANES), F32)
            for kk in range(CONF_K):
                acc = acc + w_ref[kk:kk + 1, :] * _conf_window(pd_ref, r0, _CP - kk, gw, vertical)
            val = v_ref[pl.ds(r0, _RT)]
            sg = jax.nn.sigmoid(g_ref[pl.ds(r0, _RT)])
            dv_ref[pl.ds(r0, _RT)] = (acc * sg).astype(dv_ref.dtype)
            dg_ref[pl.ds(r0, _RT)] = (acc * val * sg * (1.0 - sg)).astype(dg_ref.dtype)
            return carry

        lax.fori_loop(0, gw // _RT, step, 0)

        for kk in range(CONF_K):
            def wstep(t, acc, kk=kk):
                r0 = t * _RT
                prod = d_ref[pl.ds(r0, _RT)] * _conf_window(py_ref, r0, kk - _CP, gw, vertical)
                return acc + jnp.sum(prod, axis=0)

            tot = lax.fori_loop(0, gw // _RT, wstep, jnp.zeros((gw, LANES), F32))
            dw_ref[kk:kk + 1, :] = jnp.sum(tot, axis=0, keepdims=True)

    blk = (gw, gw, LANES)
    pshape = _conf_pad_shape(gw, vertical)
    return pl.pallas_call(
        body, grid=(half // LANES,),
        in_specs=[pl.BlockSpec(blk, lambda j: (0, 0, cb_val + j)), pl.BlockSpec(blk, lambda j: (0, 0, cb_gate + j)),
                  pl.BlockSpec((CONF_K, LANES), lambda j: (0, cb_w + j)), pl.BlockSpec(blk, lambda j: (0, 0, j))],
        out_specs=[pl.BlockSpec(blk, lambda j: (0, 0, j)), pl.BlockSpec(blk, lambda j: (0, 0, j)),
                   pl.BlockSpec((CONF_K, LANES), lambda j: (0, j))],
        out_shape=[jax.ShapeDtypeStruct((gw, gw, half), MXU_DTYPE), jax.ShapeDtypeStruct((gw, gw, half), MXU_DTYPE),
                   jax.ShapeDtypeStruct((CONF_K, half), F32)],
        scratch_shapes=[pltpu.VMEM(pshape, F32), pltpu.VMEM(pshape, F32)],
        name="conf_conv_bwd_v" if vertical else "conf_conv_bwd_h",
        compiler_params=_cparams(("arbitrary",), 12 * _nbytes(pshape, F32)),
    )(h3, h3, w_dw, dyc)


_INV_BASE = 8


def _tri_inv_raw(mats):
    c = mats[0].shape[0]
    ri = lax.broadcasted_iota(jnp.int32, (c, c), 0)
    ci = lax.broadcasted_iota(jnp.int32, (c, c), 1)
    eye = (ri == ci).astype(F32)
    base = min(_INV_BASE, c)
    same = lambda size: (ri // size) == (ci // size)
    dmat = [jnp.where(same(base), a, 0.0) for a in mats]
    x = [eye - dm for dm in dmat]
    pw = [_raw_dot(dm, dm, "nn", "hi3") for dm in dmat]
    span = 2
    while span < base:
        x = [xi + _raw_dot(xi, p, "nn", "hi3") for xi, p in zip(x, pw)]
        span *= 2
        if span < base:
            pw = [_raw_dot(p, p, "nn", "hi3") for p in pw]
    size = base
    while size < c:
        off = jnp.logical_and(jnp.logical_not(same(size)), same(2 * size))
        t = [_raw_dot(xi, jnp.where(off, a, 0.0), "nn", "hi3") for xi, a in zip(x, mats)]
        x = [xi - _raw_dot(ti, xi, "nn", "hi3") for xi, ti in zip(x, t)]
        size *= 2
    return x


@jax.custom_vjp
def _tri_inv(mats):
    return _tri_inv_raw(mats)


def _tri_inv_fwd(mats):
    x = _tri_inv_raw(mats)
    return x, x


def _tri_inv_bwd(x, dx):
    t = [_raw_dot(xi, di, "tn", "hi3") for xi, di in zip(x, dx)]
    return ([-_raw_dot(ti, xi, "nt", "hi3") for ti, xi in zip(t, x)],)


_tri_inv.defvjp(_tri_inv_fwd, _tri_inv_bwd)


def _delta_chunk(qa, ka, va, graw, braw, alog, dtb, s, *, rev):
    c = qa[0].shape[0]

    def each(f, *lists):
        return [f(*xs) for xs in zip(*lists, strict=True)]

    def l2n(t):
        return t * lax.rsqrt(jnp.sum(t * t, axis=-1, keepdims=True) + 1e-6)

    ri = lax.broadcasted_iota(jnp.int32, (c, c), 0)
    ci = lax.broadcasted_iota(jnp.int32, (c, c), 1)
    incl = (ci >= ri) if rev else (ci <= ri)
    strict = (ci > ri) if rev else (ci < ri)
    tmat = incl.astype(F32)
    tmat_t = ((ri >= ci) if rev else (ri <= ci)).astype(F32)

    q = each(lambda t: l2n(t) * (HD ** -0.5), qa)
    k = each(l2n, ka)
    g = each(lambda al, gr, dt: -jnp.exp(al) * _softplus(gr + dt), alog, graw, dtb)
    beta = each(jax.nn.sigmoid, braw)
    gc_wide = each(lambda t: _xdot(tmat, jnp.broadcast_to(t, (c, HD)), "nn", "a"), g)
    gc_rows = gc_wide if c == HD else each(lambda t: _xdot(tmat, jnp.broadcast_to(t, (c, c)), "nn", "a"), g)
    gc_cols = each(lambda t: _xdot(jnp.broadcast_to(t, (c, c)), tmat_t, "tn", "b"), g)
    gamma = each(lambda r, cc: jnp.where(incl, jnp.exp(jnp.where(incl, r - cc, 0.0)), 0.0), gc_rows, gc_cols)
    kb = each(lambda a, b: a * b, k, beta)
    a_mat = each(lambda a, b, gm: jnp.where(strict, _dot(a, b, "nt", "mxu") * gm, 0.0), kb, k, gamma)
    minv = _tri_inv(a_mat)
    eg = each(jnp.exp, gc_wide)
    u = each(lambda mi, v, b: _dot(mi, v * b, "nn", "hi3"), minv, va, beta)
    w = each(lambda mi, a, e: _dot(mi, a * e, "nn", "hi3"), minv, kb, eg)
    attn = each(lambda a, b, gm: _dot(a, b, "nt", "mxu") * gm, q, k, gamma)
    tot = each(lambda t: jnp.sum(t, axis=0, keepdims=True), g)
    q_dec = each(lambda a, e: a * e, q, eg)
    k_dec = each(lambda a, t, gw_: a * jnp.exp(t - gw_), k, tot, gc_wide)
    v_new = each(lambda uu, ww, ss: uu - _dot(ww, ss, "nn", "mxu"), u, w, s)
    o = each(lambda qd, ss, at, vn: _dot(qd, ss, "nn", "mxu") + _dot(at, vn, "nn", "mxu"), q_dec, s, attn, v_new)
    s_new = each(lambda ss, t, kd, vn: ss * jnp.exp(t) + _dot(kd, vn, "tn", "mxu"), s, tot, k_dec, v_new)
    return o, s_new


def _delta_args(a_ref, g_ref, al_ref, dt_ref, s_ref, hb, gcol):
    gts = [g_ref[hh] for hh in range(hb)]
    return ([a_ref[0, hh] for hh in range(hb)], [a_ref[1, hh] for hh in range(hb)], [a_ref[2, hh] for hh in range(hb)],
            [t[:, gcol:gcol + 1] for t in gts], [t[:, gcol + 1:gcol + 2] for t in gts],
            [al_ref[hh, 0:1, 0:1] for hh in range(hb)], [dt_ref[hh, 0:1, 0:1] for hh in range(hb)],
            [s_ref[hh] for hh in range(hb)])


def _delta_chunk_of_step(step, nch, nlc, rev):
    return (nch - 1 - step) if rev else (step + nlc) % nch


def _delta_fwd(act, gates_hm, alog_b, dtb_b, *, n_lat, n_ctx, nh, hb, rev):
    nt = n_lat + n_ctx
    nch, nlc = nt // CHUNK, n_lat // CHUNK
    gcol = 2 if rev else 0
    chunk = functools.partial(_delta_chunk_of_step, nch=nch, nlc=nlc, rev=rev)

    def body(a_ref, g_ref, al_ref, dt_ref, o_ref, st_ref, s_ref):
        n = pl.program_id(1)

        @pl.when(n == 0)
        def _():
            s_ref[...] = jnp.zeros_like(s_ref)

        args = _delta_args(a_ref, g_ref, al_ref, dt_ref, s_ref, hb, gcol)
        o, s_new = _delta_chunk(*args, rev=rev)
        for hh in range(hb):
            st_ref[hh] = args[7][hh]
            o_ref[:, hh * HD:(hh + 1) * HD] = o[hh]
            s_ref[hh] = s_new[hh]

    par = pl.BlockSpec((hb, 8, LANES), lambda h, n: (h, 0, 0))
    return pl.pallas_call(
        body, grid=(nh // hb, nch),
        in_specs=[pl.BlockSpec((3, hb, CHUNK, HD), lambda h, n: (0, h, chunk(n), 0)),
                  pl.BlockSpec((hb, CHUNK, 4), lambda h, n: (h, chunk(n), 0)), par, par],
        out_specs=[pl.BlockSpec((CHUNK, hb * HD), lambda h, n: (chunk(n), h)),
                   pl.BlockSpec((hb, None, HD, HD), lambda h, n: (h, n, 0, 0))],
        out_shape=[jax.ShapeDtypeStruct((nt, nh * HD), F32), jax.ShapeDtypeStruct((nh, nch, HD, HD), F32)],
        scratch_shapes=[pltpu.VMEM((hb, HD, HD), F32)], name="delta_fwd_rev" if rev else "delta_fwd",
        compiler_params=_cparams(("arbitrary", "arbitrary"), 0),
    )(act, gates_hm, alog_b, dtb_b)


def _delta_bwd(act, gates_hm, alog_b, dtb_b, states, do, *, n_lat, n_ctx, nh, hb, rev):
    nt = n_lat + n_ctx
    nch, nlc = nt // CHUNK, n_lat // CHUNK
    gcol = 2 if rev else 0

    def chunk(m):
        return _delta_chunk_of_step(nch - 1 - m, nch, nlc, rev)

    def body(a_ref, g_ref, al_ref, dt_ref, st_ref, do_ref, da_ref, dg_ref, dal_ref, ddt_ref, ds_ref):
        m = pl.program_id(1)

        @pl.when(m == 0)
        def _():
            ds_ref[...] = jnp.zeros_like(ds_ref)
            dal_ref[...] = jnp.zeros_like(dal_ref)
            ddt_ref[...] = jnp.zeros_like(ddt_ref)

        is_lat = chunk(m) < nlc
        args = _delta_args(a_ref, g_ref, al_ref, dt_ref, st_ref, hb, gcol)
        _, vjp = jax.vjp(functools.partial(_delta_chunk, rev=rev), *args)
        do_h = [jnp.where(is_lat, do_ref[:, hh * HD:(hh + 1) * HD], 0.0) for hh in range(hb)]
        grads = vjp((do_h, [ds_ref[hh] for hh in range(hb)]))
        for hh in range(hb):
            dq, dk, dv, dgr, dbr, dal, ddt, ds = [gr[hh] for gr in grads]
            da_ref[0, hh] = dq
            da_ref[1, hh] = dk
            da_ref[2, hh] = dv
            lane = lax.broadcasted_iota(jnp.int32, (CHUNK, 4), 1)
            dg_ref[hh] = jnp.where(lane == 0, dgr, jnp.where(lane == 1, dbr, 0.0))
            dal_ref[hh] += jnp.broadcast_to(dal, (8, LANES))
            ddt_ref[hh] += jnp.broadcast_to(ddt, (8, LANES))
            ds_ref[hh] = ds

    par = pl.BlockSpec((hb, 8, LANES), lambda h, m: (h, 0, 0))
    return pl.pallas_call(
        body, grid=(nh // hb, nch),
        in_specs=[pl.BlockSpec((3, hb, CHUNK, HD), lambda h, m: (0, h, chunk(m), 0)),
                  pl.BlockSpec((hb, CHUNK, 4), lambda h, m: (h, chunk(m), 0)), par, par,
                  pl.BlockSpec((hb, None, HD, HD), lambda h, m: (h, nch - 1 - m, 0, 0)),
                  pl.BlockSpec((CHUNK, hb * HD), lambda h, m: (jnp.minimum(chunk(m), nlc - 1), h))],
        out_specs=[pl.BlockSpec((3, hb, CHUNK, HD), lambda h, m: (0, h, chunk(m), 0)),
                   pl.BlockSpec((hb, CHUNK, 4), lambda h, m: (h, chunk(m), 0)), par, par],
        out_shape=[jax.ShapeDtypeStruct((3, nh, nt, HD), F32), jax.ShapeDtypeStruct((nh, nt, 4), F32),
                   jax.ShapeDtypeStruct((nh, 8, LANES), F32), jax.ShapeDtypeStruct((nh, 8, LANES), F32)],
        scratch_shapes=[pltpu.VMEM((hb, HD, HD), F32)], name="delta_bwd_rev" if rev else "delta_bwd",
        compiler_params=_cparams(("arbitrary", "arbitrary"), 0),
    )(act, gates_hm, alog_b, dtb_b, states, do)


def _my_pos():
    return lax.axis_index("x"), lax.axis_index("y"), lax.axis_index("c")


def _flip(v, d):
    return 1 - v if d else v


def _bcast8(block, name):
    r, w = block.shape
    assert r % 8 == 0 and w % LANES == 0

    def body(x_ref, o_ref, send_sems, recv_sems, local_sem):
        x, y, c = _my_pos()

        def rows(px, py, pc):
            return o_ref.at[pl.ds((4 * px + 2 * py + pc) * r, r), :]

        mine = pltpu.make_async_copy(x_ref, rows(x, y, c), local_sem)
        mine.start()
        copies = []
        for kk in range(1, 8):
            dx, dy, dc = (kk >> 2) & 1, (kk >> 1) & 1, kk & 1
            peer = (_flip(x, dx), _flip(y, dy), _flip(c, dc))
            cp = pltpu.make_async_remote_copy(src_ref=x_ref, dst_ref=rows(x, y, c), send_sem=send_sems.at[kk - 1],
                                              recv_sem=recv_sems.at[kk - 1], device_id=peer, device_id_type=pl.DeviceIdType.MESH)
            cp.start()
            copies.append((cp, peer))
        for kk, (cp, peer) in enumerate(copies):
            pltpu.make_async_remote_copy(src_ref=x_ref, dst_ref=rows(*peer), send_sem=send_sems.at[kk], recv_sem=recv_sems.at[kk],
                                         device_id=peer, device_id_type=pl.DeviceIdType.MESH).wait_recv()
        for cp, _ in copies:
            cp.wait_send()
        mine.wait()

    return pl.pallas_call(
        body, out_shape=jax.ShapeDtypeStruct((8 * r, w), block.dtype),
        in_specs=[pl.BlockSpec(memory_space=pltpu.VMEM)], out_specs=pl.BlockSpec(memory_space=pltpu.VMEM),
        scratch_shapes=[pltpu.SemaphoreType.DMA((7,)), pltpu.SemaphoreType.DMA((7,)), pltpu.SemaphoreType.DMA], name=name,
        compiler_params=pltpu.CompilerParams(vmem_limit_bytes=int(max(32 * 1024 * 1024, 12 * _nbytes((r, w), block.dtype)))),
    )(block)


_CHIP_PEERS = ((1, 0), (0, 1), (1, 1))


def _col_half(ref, c, lead):
    hc = ref.shape[-1] // 2
    return ref.at[(*lead, slice(None), pl.ds(pl.multiple_of(c * hc, LANES), hc))]


def _into_slot(w, chip_idx, name):
    r, cc = w.shape
    tr = _pick(r, 512, 16)

    def body(i_ref, w_ref, o_ref):
        o_ref[...] = w_ref[...].astype(o_ref.dtype)

    grid_spec = pltpu.PrefetchScalarGridSpec(
        num_scalar_prefetch=1, grid=(r // tr,), in_specs=[pl.BlockSpec((tr, cc), lambda i, s: (i, 0))],
        out_specs=pl.BlockSpec((None, tr, cc), lambda i, s: (s[0], i, 0)))
    return pl.pallas_call(body, grid_spec=grid_spec, out_shape=jax.ShapeDtypeStruct((4, r, cc), MXU_DTYPE), name=name,
                          compiler_params=_cparams(("arbitrary",), 6 * _nbytes((tr, cc), F32)))(chip_idx, w)


def _chip_gather(bufs, name):
    n = len(bufs)

    def body(*refs):
        outs = refs[n:2 * n]
        ici_send, ici_recv, d2d_send, d2d_recv = refs[2 * n:]
        x, y, c = _my_pos()
        me, sib = 2 * x + y, (x, y, 1 - c)

        def remote(a, kk, slot, half, to, sems):
            piece = _col_half(outs[a], half, (slot,))
            return pltpu.make_async_remote_copy(src_ref=piece, dst_ref=piece, send_sem=sems[0].at[3 * a + kk], recv_sem=sems[1].at[3 * a + kk],
                                                device_id=to, device_id_type=pl.DeviceIdType.MESH)

        sends = []
        for a in range(n):
            for kk, (dx, dy) in enumerate(_CHIP_PEERS):
                cp = remote(a, kk, me, c, (_flip(x, dx), _flip(y, dy), c), (ici_send, ici_recv))
                cp.start()
                sends.append(cp)
        for a in range(n):
            for kk, (dx, dy) in enumerate(_CHIP_PEERS):
                px, py = _flip(x, dx), _flip(y, dy)
                remote(a, kk, 2 * px + py, c, (px, py, c), (ici_send, ici_recv)).wait_recv()
                fw = remote(a, kk, 2 * px + py, c, sib, (d2d_send, d2d_recv))
                fw.start()
                sends.append(fw)
        for a in range(n):
            for kk, (dx, dy) in enumerate(_CHIP_PEERS):
                remote(a, kk, 2 * _flip(x, dx) + _flip(y, dy), 1 - c, sib, (d2d_send, d2d_recv)).wait_recv()
        for cp in sends:
            cp.wait_send()

    return pl.pallas_call(
        body, out_shape=[jax.ShapeDtypeStruct(b.shape, b.dtype) for b in bufs],
        in_specs=[ANY] * n, out_specs=[ANY] * n, input_output_aliases={a: a for a in range(n)},
        scratch_shapes=[pltpu.SemaphoreType.DMA((3 * n,))] * 4, name=name,
    )(*bufs)


def _chip_scatter(parts, name):
    n = len(parts)

    def body(*refs):
        ins, outs = refs[:n], refs[n:2 * n]
        send_sems, recv_sems, local_sems = refs[2 * n:]
        x, y, c = _my_pos()
        me = 2 * x + y
        started = []
        for a in range(n):
            loc = pltpu.make_async_copy(ins[a].at[me], outs[a].at[me], local_sems.at[a])
            loc.start()
            started.append(loc)
        sends = []
        for a in range(n):
            for kk, (dx, dy) in enumerate(_CHIP_PEERS):
                px, py = _flip(x, dx), _flip(y, dy)
                cp = pltpu.make_async_remote_copy(src_ref=ins[a].at[2 * px + py], dst_ref=outs[a].at[me], send_sem=send_sems.at[3 * a + kk],
                                                  recv_sem=recv_sems.at[3 * a + kk], device_id=(px, py, c),
                                                  device_id_type=pl.DeviceIdType.MESH)
                cp.start()
                sends.append(cp)
        for a in range(n):
            for kk, (dx, dy) in enumerate(_CHIP_PEERS):
                px, py = _flip(x, dx), _flip(y, dy)
                pltpu.make_async_remote_copy(src_ref=ins[a].at[me], dst_ref=outs[a].at[2 * px + py], send_sem=send_sems.at[3 * a + kk],
                                             recv_sem=recv_sems.at[3 * a + kk], device_id=(px, py, c),
                                             device_id_type=pl.DeviceIdType.MESH).wait_recv()
        for cp in sends:
            cp.wait_send()
        for loc in started:
            loc.wait()

    return pl.pallas_call(
        body, out_shape=[jax.ShapeDtypeStruct(p.shape, p.dtype) for p in parts],
        in_specs=[ANY] * n, out_specs=[ANY] * n,
        scratch_shapes=[pltpu.SemaphoreType.DMA((3 * n,)), pltpu.SemaphoreType.DMA((3 * n,)), pltpu.SemaphoreType.DMA((n,))],
        name=name,
    )(*parts)


def _pair_send_halves(grads, name):
    n = len(grads)

    def body(*refs):
        ins, outs = refs[:n], refs[n:2 * n]
        send_sems, recv_sems = refs[2 * n:]
        x, y, c = _my_pos()
        sib = (x, y, 1 - c)
        sends = []
        for a in range(n):
            cp = pltpu.make_async_remote_copy(src_ref=_col_half(ins[a], 1 - c, (slice(None),)), dst_ref=outs[a], send_sem=send_sems.at[a],
                                              recv_sem=recv_sems.at[a], device_id=sib, device_id_type=pl.DeviceIdType.MESH)
            cp.start()
            sends.append(cp)
        for cp in sends:
            cp.wait_recv()
        for cp in sends:
            cp.wait_send()

    return pl.pallas_call(
        body, out_shape=[jax.ShapeDtypeStruct((4, g.shape[1], g.shape[2] // 2), g.dtype) for g in grads],
        in_specs=[ANY] * n, out_specs=[ANY] * n,
        scratch_shapes=[pltpu.SemaphoreType.DMA((n,)), pltpu.SemaphoreType.DMA((n,))], name=name,
    )(*grads)


def _pair_join_halves(bufs, name):
    n = len(bufs)

    def body(*refs):
        outs = refs[n:2 * n]
        send_sems, recv_sems = refs[2 * n:]
        x, y, c = _my_pos()
        sib = (x, y, 1 - c)
        sends = []
        for a in range(n):
            mine = _col_half(outs[a], c, ())
            cp = pltpu.make_async_remote_copy(src_ref=mine, dst_ref=mine, send_sem=send_sems.at[a], recv_sem=recv_sems.at[a],
                                              device_id=sib, device_id_type=pl.DeviceIdType.MESH)
            cp.start()
            sends.append(cp)
        for a in range(n):
            other = _col_half(outs[a], 1 - c, ())
            pltpu.make_async_remote_copy(src_ref=other, dst_ref=other, send_sem=send_sems.at[a], recv_sem=recv_sems.at[a],
                                         device_id=sib, device_id_type=pl.DeviceIdType.MESH).wait_recv()
        for cp in sends:
            cp.wait_send()

    return pl.pallas_call(
        body, out_shape=[jax.ShapeDtypeStruct(b.shape, b.dtype) for b in bufs],
        in_specs=[ANY] * n, out_specs=[ANY] * n, input_output_aliases={a: a for a in range(n)},
        scratch_shapes=[pltpu.SemaphoreType.DMA((n,)), pltpu.SemaphoreType.DMA((n,))], name=name,
    )(*bufs)


def _add_my_half(g, recv, c_idx, name):
    _, r, cc = g.shape
    hc = cc // 2
    tc = _pick(hc, 256)
    per = hc // tc

    def body(c_ref, g_ref, r_ref, o_ref):
        o_ref[...] = (g_ref[...] + r_ref[...]).astype(o_ref.dtype)

    grid_spec = pltpu.PrefetchScalarGridSpec(
        num_scalar_prefetch=1, grid=(4, per),
        in_specs=[pl.BlockSpec((None, r, tc), lambda s, j, c_ref: (s, 0, c_ref[0] * per + j)),
                  pl.BlockSpec((None, r, tc), lambda s, j, c_ref: (s, 0, j))],
        out_specs=pl.BlockSpec((None, r, tc), lambda s, j, c_ref: (s, 0, j)))
    return pl.pallas_call(body, grid_spec=grid_spec, out_shape=jax.ShapeDtypeStruct((4, r, hc), WIRE_DTYPE), name=name,
                          compiler_params=_cparams(("arbitrary", "arbitrary"), 6 * _nbytes((r, tc), F32)))(c_idx, g, recv)


def _sum_slots(parts, c_idx, name):
    _, r, hc = parts.shape
    tc = _pick(hc, 256)
    per = hc // tc

    def body(c_ref, p_ref, o_ref):
        f = lambda i: p_ref[i].astype(F32)
        o_ref[...] = ((f(0) + f(1)) + f(2)) + f(3)

    grid_spec = pltpu.PrefetchScalarGridSpec(
        num_scalar_prefetch=1, grid=(per,), in_specs=[pl.BlockSpec((4, r, tc), lambda j, c_ref: (0, 0, j))],
        out_specs=pl.BlockSpec((r, tc), lambda j, c_ref: (0, c_ref[0] * per + j)))
    return pl.pallas_call(body, grid_spec=grid_spec, out_shape=jax.ShapeDtypeStruct((r, 2 * hc), F32), name=name,
                          compiler_params=_cparams(("arbitrary",), 14 * _nbytes((r, tc), F32)))(c_idx, parts)


def _adamw_math(w, g, m, v):
    m = ADAM_B1 * m + (1.0 - ADAM_B1) * g
    v = ADAM_B2 * v + (1.0 - ADAM_B2) * (g * g)
    m_hat = m / (1.0 - ADAM_B1 ** ADAM_STEP)
    v_hat = v / (1.0 - ADAM_B2 ** ADAM_STEP)
    delta = -ADAM_LR * (m_hat / (jnp.sqrt(v_hat) + ADAM_EPS) + ADAM_WD * w)
    return delta, m, v


def _adamw(w, g, m, v, name):
    r, cc = w.shape
    tr = _pick(r, max(8, (512 * 1024) // max(cc, 1)), 8)

    def body(w_ref, g_ref, m_ref, v_ref, d_ref, nm_ref, nv_ref):
        d, nm, nv = _adamw_math(w_ref[...], g_ref[...], m_ref[...], v_ref[...])
        d_ref[...] = d
        nm_ref[...] = nm
        nv_ref[...] = nv

    spec = pl.BlockSpec((tr, cc), lambda i: (i, 0))
    return pl.pallas_call(
        body, grid=(r // tr,), in_specs=[spec] * 4, out_specs=[spec] * 3,
        out_shape=[jax.ShapeDtypeStruct((r, cc), F32)] * 3, name=name,
        compiler_params=_cparams(("arbitrary",), 16 * _nbytes((tr, cc), F32)))(w, g, m, v)


def _sum8(allv, name):
    r8, w = allv.shape
    r = r8 // 8

    def body(a_ref, o_ref):
        acc = a_ref[0:r, :]
        for d in range(1, 8):
            acc = acc + a_ref[d * r:(d + 1) * r, :]
        o_ref[...] = acc

    return pl.pallas_call(body, out_shape=jax.ShapeDtypeStruct((r, w), F32), name=name,
                          compiler_params=_cparams((), 12 * _nbytes((r8, w), F32)))(allv)


def _dmod_prep(alld, name):
    _, w = alld.shape
    tw = _pick(w, 2048)

    def body(a_ref, d_ref, b_ref):
        ctx = a_ref[1:2, :]
        tot = a_ref[0:1, :]
        d_ref[0:1, :] = tot
        for d in range(1, 8):
            ctx = ctx + a_ref[8 * d + 1:8 * d + 2, :]
            tot = tot + a_ref[8 * d:8 * d + 1, :]
            d_ref[d:d + 1, :] = a_ref[8 * d:8 * d + 1, :]
        d_ref[8:9, :] = ctx
        d_ref[9:16, :] = jnp.zeros((7, tw), F32)
        b_ref[...] = tot + ctx

    return pl.pallas_call(
        body, grid=(w // tw,), in_specs=[pl.BlockSpec((64, tw), lambda j: (0, j))],
        out_specs=[pl.BlockSpec((16, tw), lambda j: (0, j)), pl.BlockSpec((1, tw), lambda j: (0, j))],
        out_shape=[jax.ShapeDtypeStruct((16, w), F32), jax.ShapeDtypeStruct((1, w), F32)], name=name,
        compiler_params=_cparams(("arbitrary",), 0))(alld)


def _pack(arrs, rows=8):
    flat = jnp.concatenate([a.reshape(-1).astype(F32) for a in arrs])
    per = rows * LANES
    n = flat.shape[0]
    padded = -(-n // per) * per
    flat = jnp.pad(flat, (0, padded - n))
    return flat.reshape(rows, padded // rows)


def _unpack(slab, shapes):
    flat = slab.reshape(-1)
    out, off = [], 0
    for s in shapes:
        n = math.prod(s)
        out.append(flat[off:off + n].reshape(s))
        off += n
    return out


def kernel(x, c, ctx, c_ctx, ln_in_g, ln_in_b, w_mod, b_mod, w_in, w_qkv_conv, a_log_f, dt_bias_f, a_log_b, dt_bias_b, dn_norm_g, conf_dw_w, conf_dw_b, conf_ln_g, conf_ln_b, w_out, ln1_g, ln1_b, w_mlp1, b_mlp1, w_mlp2, b_mlp2, ln2_g, ln2_b, loss_target, m_c_ctx, m_ln_in_g, m_ln_in_b, m_w_mod, m_b_mod, m_w_in, m_w_qkv_conv, m_a_log_f, m_dt_bias_f, m_a_log_b, m_dt_bias_b, m_dn_norm_g, m_conf_dw_w, m_conf_dw_b, m_conf_ln_g, m_conf_ln_b, m_w_out, m_ln1_g, m_ln1_b, m_w_mlp1, m_b_mlp1, m_w_mlp2, m_b_mlp2, m_ln2_g, m_ln2_b, v_c_ctx, v_ln_in_g, v_ln_in_b, v_w_mod, v_b_mod, v_w_in, v_w_qkv_conv, v_a_log_f, v_dt_bias_f, v_a_log_b, v_dt_bias_b, v_dn_norm_g, v_conf_dw_w, v_conf_dw_b, v_conf_ln_g, v_conf_ln_b, v_w_out, v_ln1_g, v_ln1_b, v_w_mlp1, v_b_mlp1, v_w_mlp2, v_b_mlp2, v_ln2_g, v_ln2_b):
    n_lat, d = x.shape[1], x.shape[2]
    n_ctx = ctx.shape[1]
    nt = n_lat + n_ctx
    dn = d // 2
    nh = dn // HD
    conf = d - dn
    half = conf // 2
    gw = math.isqrt(n_lat)
    ff = 4 * w_mlp1.shape[2]
    fl = w_mlp1.shape[2]
    in_cols = 4 * w_in.shape[2]
    r_in = w_in.shape[2]
    n_gate = 4 * nh
    gb = max(LANES, d // 8)
    g0 = 4 * dn + 2 * conf
    in_pad = g0 + gb
    ml = w_mod.shape[2]
    assert gw * gw == n_lat and nt % gw == 0 and in_cols == 4 * dn + n_gate + 2 * conf and nh * HD == dn
    hb = 4 if nh % 4 == 0 else (2 if nh % 2 == 0 else 1)

    mx, my, mc = _my_pos()
    chip = 2 * mx + my
    dev = 4 * mx + 2 * my + mc
    c_idx = jnp.reshape(mc, (1,)).astype(jnp.int32)

    x2, ctx2, tgt2 = x[0], ctx[0], loss_target[0]
    row = lambda a: a.reshape(1, -1)

    e0_shapes = [(d,), w_qkv_conv.shape[1:], conf_dw_w.shape[1:]]
    e0 = _bcast8(_pack([c[0], w_qkv_conv[0], conf_dw_w[0]]), "bcast_inputs")
    e0 = e0.reshape(8, -1)
    per_dev = [_unpack(e0[dd], e0_shapes) for dd in range(8)]
    all_c = jnp.stack([p[0] for p in per_dev])
    w_conv = jnp.concatenate([per_dev[2 * j][1] for j in range(4)], axis=1)
    w_dw = jnp.concatenate([per_dev[2 * j][2] for j in range(4)], axis=1)

    s16 = jnp.concatenate([all_c, c_ctx[None, :], jnp.zeros((7, d), F32)], axis=0)
    b_mod_loc = lax.dynamic_slice(b_mod, (0, chip * ml), (1, ml))
    to_mxu = lambda t: t.astype(MXU_DTYPE)
    modblk = _matmul(s16, w_mod[0], form="nn", m=16, n=ml, k=d, tm=16, tn=_pick(ml, 1024), tk=_pick(d, 1024),
                     out_dtype=F32, name="mod_fwd", bias=b_mod_loc, a_fn=lambda t: to_mxu(_silu(t)), b_fn=to_mxu)
    allmod = _bcast8(modblk, "bcast_mod").reshape(8, 16, ml)[0::2]
    mod_mine = lax.dynamic_index_in_dim(allmod, dev, axis=1, keepdims=False).reshape(1, 4 * ml)
    mod_ctx = allmod[:, 8, :].reshape(1, 4 * ml)
    sh_a, sc_a, g_a, sh_m, sc_m, g_m = [mod_mine[:, i * d:(i + 1) * d] for i in range(6)]
    csh_a, csc_a = mod_ctx[:, 0:d], mod_ctx[:, d:2 * d]

    chip_idx = jnp.reshape(chip, (1,)).astype(jnp.int32)
    win_t_loc = jnp.transpose(w_in[0]).astype(MXU_DTYPE)
    gw_in, gw_out, gw_1, gw_2 = _chip_gather(
        [_into_slot(w, chip_idx, f"weights_into_slot_{i}") for i, w in enumerate([win_t_loc, w_out[0], w_mlp1[0], w_mlp2[0]])],
        "gather_weights")
    win_t = gw_in.reshape(in_cols, d)
    win_t = jnp.concatenate([win_t[:4 * dn], win_t[4 * dn + n_gate:], win_t[4 * dn:4 * dn + n_gate],
                             jnp.zeros((gb - n_gate, d), MXU_DTYPE)], axis=0)
    wout_full = gw_out.reshape(d, d)
    w2_full = gw_2.reshape(ff, d)
    w1_sh = gw_1

    tmr = _pick(n_lat, 128, 8)
    tmc = _pick(n_ctx, 128, 8)
    ln_in = [row(ln_in_g), row(ln_in_b)]

    def stage_a(rv, pv):
        x0 = _ln(rv[0], pv[0], pv[1])
        return [x0 * (1.0 + pv[3]) + pv[2]], []

    xm_lat, = _rowwise(stage_a, [(x2, d, 0, 0)], ln_in + [sh_a, sc_a], [(d, MXU_DTYPE)], [], n_rows=n_lat, tm=tmr, name="ln_mod_lat")
    xm_ctx, = _rowwise(stage_a, [(ctx2, d, 0, 0)], ln_in + [csh_a, csc_a], [(d, MXU_DTYPE)], [], n_rows=n_ctx, tm=tmc, name="ln_mod_ctx")
    xm = jnp.concatenate([xm_lat, xm_ctx], axis=0)

    tm_nt = _pick(nt, 1280, 16)
    tn_in = _pick(in_pad, 1280)
    tk_d = _pick(d, 1024)
    h = _matmul(xm, win_t, form="nt", m=nt, n=in_pad, k=d, tm=tm_nt, tn=tn_in, tk=tk_d, out_dtype=F32, name="in_proj")

    act = _conv7_fwd(h, w_conv, n_lat=n_lat, n_ctx=n_ctx, nh=nh)
    gates_hm = jnp.transpose(h[:, g0:g0 + n_gate].reshape(nt, 4, nh), (2, 0, 1))
    bc = lambda a: jnp.broadcast_to(a.reshape(nh, 1, 1), (nh, 8, LANES))
    dk = dict(n_lat=n_lat, n_ctx=n_ctx, nh=nh, hb=hb)
    o_f, st_f = _delta_fwd(act, gates_hm, bc(a_log_f), bc(dt_bias_f), rev=False, **dk)
    o_b, st_b = _delta_fwd(act, gates_hm, bc(a_log_b), bc(dt_bias_b), rev=True, **dk)

    h3 = h.reshape(nt // gw, gw, in_pad)
    ck = dict(gw=gw, col0=4 * dn, half=half)
    yh = _conf_conv_fwd(h3, w_dw, vertical=False, **ck).reshape(n_lat, half)
    yv = _conf_conv_fwd(h3, w_dw, vertical=True, **ck).reshape(n_lat, half)

    def mix_fn(o_heads, z_heads, ya, yb, dng, ba, bb, ga, gbb, la, lb):
        outs = []
        for o, z in zip(o_heads, z_heads):
            ms = jnp.mean(o * o, axis=-1, keepdims=True)
            outs.append(o * lax.rsqrt(ms + 1e-6) * dng * _silu(z))
        ya, yb = ya + ba, yb + bb
        mu = (jnp.sum(ya, axis=-1, keepdims=True) + jnp.sum(yb, axis=-1, keepdims=True)) / conf
        ya, yb = ya - mu, yb - mu
        var = (jnp.sum(ya * ya, axis=-1, keepdims=True) + jnp.sum(yb * yb, axis=-1, keepdims=True)) / conf
        rs = lax.rsqrt(var + LN_EPS)
        ca, cb = ya * rs * ga + la, yb * rs * gbb + lb
        return outs, _silu(ca), _silu(cb)

    def heads(t):
        return [t[:, i * HD:(i + 1) * HD] for i in range(nh)]

    def halves(p):
        return p[:, :half], p[:, half:]

    mix_params = [row(dn_norm_g), row(conf_dw_b), row(conf_ln_g), row(conf_ln_b)]

    def mix_args(rv, pv):
        (ba, bb), (ga, gbb), (la, lb) = halves(pv[1]), halves(pv[2]), halves(pv[3])
        return (heads(rv[0] + rv[1]), heads(rv[2]), rv[3], rv[4], pv[0], ba, bb, ga, gbb, la, lb)

    def stage_e(rv, pv):
        outs, ca, cb = mix_fn(*mix_args(rv, pv))
        return [jnp.concatenate(outs + [ca, cb], axis=-1)], []

    mix_rows = [(o_f, dn, 0, 0), (o_b, dn, 0, 0), (h, dn, 3, 0), (yh, half, 0, 0), (yv, half, 0, 0)]
    mix, = _rowwise(stage_e, mix_rows, mix_params, [(d, MXU_DTYPE)], [], n_rows=n_lat, tm=tmr, name="mixer_fwd")

    tm_l = _pick(n_lat, 1024, 16)
    tn_d = _pick(d, 1024)
    y1 = _matmul(mix, wout_full, form="nn", m=n_lat, n=d, k=d, tm=tm_l, tn=tn_d, tk=tk_d, out_dtype=F32, name="out_proj")

    def res1(x0, y1v, ga, l1g, l1b, shm, scm):
        x1 = _ln(ALPHA * x0 + ga * y1v, l1g, l1b)
        return x1, x1 * (1.0 + scm) + shm

    f_params = ln_in + [g_a, row(ln1_g), row(ln1_b), sh_m, sc_m]

    def stage_f(rv, pv):
        x1, u = res1(_ln(rv[0], pv[0], pv[1]), rv[1], *pv[2:])
        return [x1, u], []

    x1, u = _rowwise(stage_f, [(x2, d, 0, 0), (y1, d, 0, 0)], f_params, [(d, F32), (d, MXU_DTYPE)], [], n_rows=n_lat, tm=tmr, name="res1_fwd")

    tn_f = _pick(fl, 1024)
    relu_h = _matmul(u, w1_sh, form="nn", m=n_lat, n=ff, k=d, tm=tm_l, tn=tn_f, tk=tk_d, out_dtype=MXU_DTYPE, name="mlp1",
                     bias=row(b_mlp1), epi=lambda r: jnp.maximum(r, 0.0), b_split=fl)
    sq = lambda t: (t.astype(F32) * t.astype(F32)).astype(MXU_DTYPE)
    tk_f = _pick(ff, 1024)
    y2 = _matmul(relu_h, w2_full, form="nn", m=n_lat, n=d, k=ff, tm=tm_l, tn=tn_d, tk=tk_f, out_dtype=F32, name="mlp2",
                 bias=row(b_mlp2), a_fn=sq)

    def loss_fn(x1v, y2v, gm, l2g, l2b, tgt):
        x2v = _ln(ALPHA * x1v + gm * y2v, l2g, l2b)
        return (0.5 / d) * jnp.sum(jnp.square(x2v - tgt))

    def stage_g(rv, pv):
        loss, grads = jax.value_and_grad(loss_fn, argnums=(0, 1, 2, 3, 4))(rv[0], rv[1], pv[0], pv[1], pv[2], rv[2])
        dx1, dy2r, dgm, dl2g, dl2b = grads
        dy2 = dy2r
        return [dx1, dy2], [jnp.reshape(loss, (1, 1)), dgm, dl2g, dl2b, jnp.sum(dy2, axis=0, keepdims=True)]

    dx1p, dy2, loss_acc, d_g_m, d_ln2_g, d_ln2_b, d_b_mlp2 = _rowwise(
        stage_g, [(x1, d, 0, 0), (y2, d, 0, 0), (tgt2, d, 0, 0)], [g_m, row(ln2_g), row(ln2_b)],
        [(d, F32), (d, MXU_DTYPE)], [LANES, d, d, d, d], n_rows=n_lat, tm=tmr, name="loss_res2_bwd")

    dhid, d_b_mlp1 = _matmul(dy2, w2_full, form="nt", m=n_lat, n=ff, k=d, tm=tm_l, tn=tn_f, tk=tk_d, out_dtype=MXU_DTYPE, name="mlp2_bwd_x",
                             epi=lambda r, rh: r * (2.0 * rh.astype(F32)), epi_in=relu_h, colsum=True)
    tk_l = _pick(n_lat, 1024, 16)
    d_w2 = _matmul(relu_h, dy2, form="tn", m=ff, n=d, k=n_lat, tm=tn_f, tn=tn_d, tk=tk_l, out_dtype=F32, name="mlp2_bwd_w", a_fn=sq)
    du = _matmul(dhid, w1_sh, form="nt", m=n_lat, n=d, k=ff, tm=tm_l, tn=tn_d, tk=tn_f, out_dtype=F32, name="mlp1_bwd_x", b_split=fl)
    d_w1 = _matmul(u, dhid, form="tn", m=d, n=ff, k=n_lat, tm=tn_d, tn=tn_f, tk=tk_l, out_dtype=F32, name="mlp1_bwd_w", o_split=fl)

    def stage_h(rv, pv):
        x0 = _ln(rv[0], pv[0], pv[1])
        _, vjp = jax.vjp(res1, x0, rv[1], *pv[2:])
        dx0, dy1r, dga, dl1g, dl1b, dshm, dscm = vjp((rv[3], rv[2]))
        return [dx0, dy1r], [dga, dl1g, dl1b, dshm, dscm]

    dx0p, dy1, d_g_a, d_ln1_g, d_ln1_b, d_sh_m, d_sc_m = _rowwise(
        stage_h, [(x2, d, 0, 0), (y1, d, 0, 0), (du, d, 0, 0), (dx1p, d, 0, 0)], f_params,
        [(d, F32), (d, MXU_DTYPE)], [d, d, d, d, d], n_rows=n_lat, tm=tmr, name="res1_bwd")

    dmix = _matmul(dy1, wout_full, form="nt", m=n_lat, n=d, k=d, tm=tm_l, tn=tn_d, tk=tk_d, out_dtype=F32, name="out_proj_bwd_x")
    d_wout = _matmul(mix, dy1, form="tn", m=d, n=d, k=n_lat, tm=tn_d, tn=tn_d, tk=tk_l, out_dtype=F32, name="out_proj_bwd_w")

    def stage_i(rv, pv):
        args = mix_args(rv, pv)
        _, vjp = jax.vjp(mix_fn, *args)
        dm = rv[5]
        d_outs = [dm[:, i * HD:(i + 1) * HD] for i in range(nh)]
        dca, dcb = dm[:, dn:dn + half], dm[:, dn + half:]
        do_h, dz_h, dya, dyb, ddng, dba, dbb, dga, dgbb, dla, dlb = vjp((d_outs, dca, dcb))
        cat = lambda *p: jnp.concatenate(p, axis=-1)
        return ([cat(*do_h), cat(*dz_h), dya, dyb], [ddng, cat(dba, dbb), cat(dga, dgbb), cat(dla, dlb)])

    do, dz, dyh, dyv, d_dn_norm_g, d_conf_dw_b, d_conf_ln_g, d_conf_ln_b = _rowwise(
        stage_i, mix_rows + [(dmix, d, 0, 0)], mix_params, [(dn, F32), (dn, MXU_DTYPE), (half, F32), (half, F32)],
        [HD, conf, conf, conf], n_rows=n_lat, tm=tmr, name="mixer_bwd")

    dval_h, dgate_h, d_wdw_h = _conf_conv_bwd(h3, w_dw, dyh.reshape(gw, gw, half), vertical=False, **ck)
    dval_v, dgate_v, d_wdw_v = _conf_conv_bwd(h3, w_dw, dyv.reshape(gw, gw, half), vertical=True, **ck)
    dact_f, dgt_f, d_alog_f, d_dt_f = _delta_bwd(act, gates_hm, bc(a_log_f), bc(dt_bias_f), st_f, do, rev=False, **dk)
    dact_b, dgt_b, d_alog_b, d_dt_b = _delta_bwd(act, gates_hm, bc(a_log_b), bc(dt_bias_b), st_b, do, rev=True, **dk)
    dh_qkv, d_wconv = _conv7_bwd(h, w_conv, dact_f, dact_b, n_lat=n_lat, n_ctx=n_ctx, nh=nh)

    dgates = jnp.stack([dgt_f[..., 0], dgt_f[..., 1], dgt_b[..., 0], dgt_b[..., 1]], axis=0)
    dgates = jnp.transpose(dgates, (2, 0, 1)).reshape(nt, n_gate)
    dgates = jnp.pad(dgates, ((0, 0), (0, gb - n_gate))).astype(MXU_DTYPE)
    zrows = lambda t: jnp.pad(t, ((0, n_ctx), (0, 0)))
    dh = jnp.concatenate([dh_qkv, zrows(dz), zrows(dval_h.reshape(n_lat, half)), zrows(dval_v.reshape(n_lat, half)),
                          zrows(dgate_h.reshape(n_lat, half)), zrows(dgate_v.reshape(n_lat, half)), dgates], axis=1)

    tk_in = tn_in
    dxm = _matmul(dh, win_t, form="nn", m=nt, n=d, k=in_pad, tm=tm_nt, tn=tn_d, tk=tk_in, out_dtype=F32, name="in_proj_bwd_x")
    tk_nt = _pick(nt, 1280, 16)
    d_win_t = _matmul(dh, xm, form="tn", m=in_pad, n=d, k=nt, tm=tn_in, tn=tn_d, tk=tk_nt, out_dtype=F32, name="in_proj_bwd_w")
    d_win_t = jnp.concatenate([d_win_t[:4 * dn], d_win_t[g0:g0 + n_gate], d_win_t[4 * dn:g0]], axis=0)

    def mod_in(xr, lg, lb, sh, sc):
        x0 = _ln(xr, lg, lb)
        return x0, x0 * (1.0 + sc) + sh

    def stage_j(rv, pv):
        _, vjp = jax.vjp(mod_in, rv[0], *pv)
        dx0 = rv[2] if len(rv) > 2 else jnp.zeros_like(rv[1])
        dxr, dlg, dlb, dsh, dsc = vjp((dx0, rv[1]))
        return [dxr], [dlg, dlb, dsh, dsc]

    grad_x, dlg_l, dlb_l, d_sh_a, d_sc_a = _rowwise(
        stage_j, [(x2, d, 0, 0), (dxm, d, 0, 0), (dx0p, d, 0, 0)], ln_in + [sh_a, sc_a], [(d, F32)], [d, d, d, d],
        n_rows=n_lat, tm=tmr, name="ln_mod_bwd_lat")
    _, dlg_c, dlb_c, d_csh_a, d_csc_a = _rowwise(
        stage_j, [(ctx2, d, 0, 0), (dxm, d, 0, n_lat // tmc)], ln_in + [csh_a, csc_a], [(d, F32)], [d, d, d, d],
        n_rows=n_ctx, tm=tmc, name="ln_mod_bwd_ctx")

    zd = jnp.zeros((1, d), F32)
    dmod_rows = jnp.concatenate([jnp.concatenate([d_sh_a, d_sc_a, d_g_a, d_sh_m, d_sc_m, d_g_m], axis=1),
                                 jnp.concatenate([d_csh_a, d_csc_a, zd, zd, zd, zd], axis=1), jnp.zeros((6, 6 * d), F32)], axis=0)
    d16, g_b_mod = _dmod_prep(_bcast8(dmod_rows, "bcast_dmod"), "dmod_prep")
    d16_loc = lax.dynamic_slice(d16, (0, chip * ml), (16, ml))
    g_w_mod = _matmul(s16, d16_loc, form="tn", m=d, n=ml, k=16, tm=_pick(d, 512), tn=_pick(ml, 1024), tk=16, out_dtype=F32,
                      name="mod_bwd_w", a_fn=_silu, kind="hi")
    dsilu = _matmul(d16_loc, w_mod[0], form="nt", m=16, n=d, k=ml, tm=16, tn=_pick(d, 1024), tk=_pick(ml, 1024), out_dtype=F32,
                    name="mod_bwd_c", a_fn=to_mxu, b_fn=to_mxu)
    d_cctx_part = dsilu[8:9, :] * (1 - mc).astype(F32)

    head_sum = lambda t: t[:, 0, 0]
    small = [d_cctx_part, dlg_l + dlg_c, dlb_l + dlb_c, d_wconv, head_sum(d_alog_f), head_sum(d_dt_f), head_sum(d_alog_b),
             head_sum(d_dt_b), d_dn_norm_g, jnp.concatenate([d_wdw_h, d_wdw_v], axis=1), d_conf_dw_b, d_conf_ln_g, d_conf_ln_b,
             d_ln1_g, d_ln1_b, d_b_mlp1, d_b_mlp2, d_ln2_g, d_ln2_b]
    small_shapes = [(d,), (d,), (d,), (SHORT_CONV, 3 * dn), (nh,), (nh,), (nh,), (nh,), (HD,), (CONF_K, conf), (conf,), (conf,), (conf,),
                    (d,), (d,), (ff,), (d,), (d,), (d,)]
    ssum = _sum8(_bcast8(_pack(small), "bcast_small_grads"), "sum_small_grads")
    (t_cctx, g_ln_in_g, g_ln_in_b, g_wconv_full, g_a_log_f, g_dt_bias_f, g_a_log_b, g_dt_bias_b, g_dn_norm_g, g_wdw_full, g_conf_dw_b,
     g_conf_ln_g, g_conf_ln_b, g_ln1_g, g_ln1_b, g_b_mlp1, g_b_mlp2, g_ln2_g, g_ln2_b) = _unpack(ssum, small_shapes)
    sg = jax.nn.sigmoid(c_ctx)
    g_c_ctx = t_cctx * (sg * (1.0 + c_ctx * (1.0 - sg)))
    g_w_qkv_conv = lax.dynamic_slice(g_wconv_full, (0, chip * w_qkv_conv.shape[2]), w_qkv_conv.shape[1:])
    g_conf_dw_w = lax.dynamic_slice(g_wdw_full, (0, chip * conf_dw_w.shape[2]), conf_dw_w.shape[1:])

    big = [d_win_t.reshape(4, r_in, d), d_wout.reshape(4, d // 4, d), d_w1, d_w2.reshape(4, fl, d)]
    recv1 = _pair_send_halves(big, "grads_pair_send")
    chipsum = [_add_my_half(g, r, c_idx, f"grads_pair_add_{i}") for i, (g, r) in enumerate(zip(big, recv1))]
    recv3 = _chip_scatter(chipsum, "grads_chip_scatter")
    mine = [_sum_slots(p, c_idx, f"grads_chip_sum_{i}") for i, p in enumerate(recv3)]
    g_win_t, g_w_out, g_w_mlp1, g_w_mlp2 = _pair_join_halves(mine, "grads_pair_join")
    g_w_in = jnp.transpose(g_win_t)

    grads = dict(c_ctx=g_c_ctx, ln_in_g=g_ln_in_g, ln_in_b=g_ln_in_b, w_mod=g_w_mod[None], b_mod=g_b_mod, w_in=g_w_in[None],
                 w_qkv_conv=g_w_qkv_conv[None], a_log_f=g_a_log_f[None], dt_bias_f=g_dt_bias_f[None], a_log_b=g_a_log_b[None],
                 dt_bias_b=g_dt_bias_b[None], dn_norm_g=g_dn_norm_g[None], conf_dw_w=g_conf_dw_w[None], conf_dw_b=g_conf_dw_b[None],
                 conf_ln_g=g_conf_ln_g[None], conf_ln_b=g_conf_ln_b[None], w_out=g_w_out[None], ln1_g=g_ln1_g[None], ln1_b=g_ln1_b[None],
                 w_mlp1=g_w_mlp1[None], b_mlp1=g_b_mlp1[None], w_mlp2=g_w_mlp2[None], b_mlp2=g_b_mlp2[None], ln2_g=g_ln2_g[None],
                 ln2_b=g_ln2_b[None])
    weights = dict(c_ctx=c_ctx, ln_in_g=ln_in_g, ln_in_b=ln_in_b, w_mod=w_mod, b_mod=b_mod, w_in=w_in, w_qkv_conv=w_qkv_conv,
                   a_log_f=a_log_f, dt_bias_f=dt_bias_f, a_log_b=a_log_b, dt_bias_b=dt_bias_b, dn_norm_g=dn_norm_g, conf_dw_w=conf_dw_w,
                   conf_dw_b=conf_dw_b, conf_ln_g=conf_ln_g, conf_ln_b=conf_ln_b, w_out=w_out, ln1_g=ln1_g, ln1_b=ln1_b, w_mlp1=w_mlp1,
                   b_mlp1=b_mlp1, w_mlp2=w_mlp2, b_mlp2=b_mlp2, ln2_g=ln2_g, ln2_b=ln2_b)
    m_in = dict(c_ctx=m_c_ctx, ln_in_g=m_ln_in_g, ln_in_b=m_ln_in_b, w_mod=m_w_mod, b_mod=m_b_mod, w_in=m_w_in, w_qkv_conv=m_w_qkv_conv,
                a_log_f=m_a_log_f, dt_bias_f=m_dt_bias_f, a_log_b=m_a_log_b, dt_bias_b=m_dt_bias_b, dn_norm_g=m_dn_norm_g,
                conf_dw_w=m_conf_dw_w, conf_dw_b=m_conf_dw_b, conf_ln_g=m_conf_ln_g, conf_ln_b=m_conf_ln_b, w_out=m_w_out, ln1_g=m_ln1_g,
                ln1_b=m_ln1_b, w_mlp1=m_w_mlp1, b_mlp1=m_b_mlp1, w_mlp2=m_w_mlp2, b_mlp2=m_b_mlp2, ln2_g=m_ln2_g, ln2_b=m_ln2_b)
    v_in = dict(c_ctx=v_c_ctx, ln_in_g=v_ln_in_g, ln_in_b=v_ln_in_b, w_mod=v_w_mod, b_mod=v_b_mod, w_in=v_w_in, w_qkv_conv=v_w_qkv_conv,
                a_log_f=v_a_log_f, dt_bias_f=v_dt_bias_f, a_log_b=v_a_log_b, dt_bias_b=v_dt_bias_b, dn_norm_g=v_dn_norm_g,
                conf_dw_w=v_conf_dw_w, conf_dw_b=v_conf_dw_b, conf_ln_g=v_conf_ln_g, conf_ln_b=v_conf_ln_b, w_out=v_w_out, ln1_g=v_ln1_g,
                ln1_b=v_ln1_b, w_mlp1=v_w_mlp1, b_mlp1=v_b_mlp1, w_mlp2=v_w_mlp2, b_mlp2=v_b_mlp2, ln2_g=v_ln2_g, ln2_b=v_ln2_b)
    names = list(weights)
    big_names = ("w_mod", "w_in", "w_out", "w_mlp1", "w_mlp2")
    delta, new_m, new_v = {}, {}, {}
    for nm in big_names:
        w2d = weights[nm][0]
        dl, mm, vv = _adamw(w2d, grads[nm][0], m_in[nm][0], v_in[nm][0], f"adamw_{nm}")
        delta[nm], new_m[nm], new_v[nm] = dl[None], mm[None], vv[None]
    small_names = [nm for nm in names if nm not in big_names]
    shapes = [weights[nm].shape for nm in small_names]
    grads = {nm: (grads[nm].reshape(weights[nm].shape) if nm in small_names else grads[nm]) for nm in names}
    packed = [_pack([src[nm] for nm in small_names]) for src in (weights, grads, m_in, v_in)]
    dl, mm, vv = _adamw(*packed, "adamw_small")
    for nm, a, b_, c_ in zip(small_names, _unpack(dl, shapes), _unpack(mm, shapes), _unpack(vv, shapes)):
        delta[nm], new_m[nm], new_v[nm] = a, b_, c_

    loss = lax.psum(loss_acc[0, 0], MESH_AXES)
    return (loss, grad_x[None], *[grads[nm] for nm in names], *[delta[nm] for nm in names],
            *[new_m[nm] for nm in names], *[new_v[nm] for nm in names])
```

```python
import functools
import math

import jax
import jax.numpy as jnp
from jax import lax
from jax.experimental import pallas as pl
from jax.experimental.pallas import tpu as pltpu

F32 = jnp.float32
BF16 = jnp.bfloat16
MXU_DTYPE = BF16
WIRE_DTYPE = BF16
HIGHEST = lax.Precision.HIGHEST

HD = 128
CHUNK = 128
SHORT_CONV = 7
CONF_K = 31
ALPHA = 2.0 ** 0.25
LN_EPS = 1e-5
V7X_VMEM_BYTES = 64 * 1024 * 1024
LANES = 128

ADAM_LR = 0.001
ADAM_B1 = 0.9
ADAM_B2 = 0.999
ADAM_EPS = 1e-08
ADAM_WD = 0.01
ADAM_STEP = 10

MESH_AXES = ("x", "y", "c")
ANY = pl.BlockSpec(memory_space=pl.ANY)


def _pick(dim, pref, mult=LANES):
    best = None
    t = mult
    while t <= min(dim, pref):
        if dim % t == 0:
            best = t
        t += mult
    return best if best is not None else dim


def _cparams(sem, vmem_est):
    limit = int(min(V7X_VMEM_BYTES - 6 * 1024 * 1024, max(32 * 1024 * 1024, vmem_est + 8 * 1024 * 1024)))
    return pltpu.CompilerParams(dimension_semantics=sem, vmem_limit_bytes=limit)


def _nbytes(shape, dtype):
    return math.prod(shape) * jnp.dtype(dtype).itemsize


def _ln(x, g, b):
    mu = jnp.mean(x, axis=-1, keepdims=True)
    xc = x - mu
    var = jnp.mean(xc * xc, axis=-1, keepdims=True)
    return xc * lax.rsqrt(var + LN_EPS) * g + b


def _silu(x):
    return x * jax.nn.sigmoid(x)


def _softplus(x):
    return jnp.maximum(x, 0.0) + jnp.log1p(jnp.exp(-jnp.abs(x)))


_DIMS = {"nn": (((1,), (0,)), ((), ())), "nt": (((1,), (1,)), ((), ())), "tn": (((0,), (0,)), ((), ()))}


def _split_bf16(x, parts):
    out, rest = [], x.astype(F32)
    for _ in range(parts):
        bits = lax.bitcast_convert_type(rest, jnp.uint32) & jnp.uint32(0xFFFF0000)
        p = lax.bitcast_convert_type(bits, F32)
        out.append(p.astype(BF16))
        rest = rest - p
    return out


def _raw_dot(a, b, form, kind):
    if MXU_DTYPE != F32:
        if kind == "mxu":
            return lax.dot_general(a.astype(MXU_DTYPE), b.astype(MXU_DTYPE), _DIMS[form], preferred_element_type=F32)
        if kind == "hi3":
            (a0, a1), (b0, b1) = _split_bf16(a, 2), _split_bf16(b, 2)
            d = lambda p, q: lax.dot_general(p, q, _DIMS[form], preferred_element_type=F32)
            return d(a0, b0) + (d(a1, b0) + d(a0, b1))
    return lax.dot_general(a.astype(F32), b.astype(F32), _DIMS[form], precision=HIGHEST, preferred_element_type=F32)


@functools.partial(jax.custom_vjp, nondiff_argnums=(2, 3))
def _dot(a, b, form, kind):
    return _raw_dot(a, b, form, kind)


def _dot_fwd(a, b, form, kind):
    return _raw_dot(a, b, form, kind), (a, b)


def _dot_bwd(form, kind, res, dc):
    a, b = res
    if form == "nn":
        return _dot(dc, b, "nt", kind), _dot(a, dc, "tn", kind)
    if form == "nt":
        return _dot(dc, b, "nn", kind), _dot(dc, a, "tn", kind)
    return _dot(b, dc, "nt", kind), _dot(a, dc, "nn", kind)


_dot.defvjp(_dot_fwd, _dot_bwd)


def _raw_xdot(a, b, form, exact):
    if MXU_DTYPE == F32:
        return _raw_dot(a, b, form, "hi")
    d = lambda p, q: lax.dot_general(p, q, _DIMS[form], preferred_element_type=F32)
    if exact == "a":
        a16, (b0, b1, b2) = a.astype(BF16), _split_bf16(b, 3)
        return d(a16, b0) + (d(a16, b1) + d(a16, b2))
    b16, (a0, a1, a2) = b.astype(BF16), _split_bf16(a, 3)
    return d(a0, b16) + (d(a1, b16) + d(a2, b16))


@functools.partial(jax.custom_vjp, nondiff_argnums=(2, 3))
def _xdot(a, b, form, exact):
    return _raw_xdot(a, b, form, exact)


def _xdot_fwd(a, b, form, exact):
    return _raw_xdot(a, b, form, exact), (a, b)


def _xdot_bwd(form, exact, res, dc):
    a, b = res
    if form == "nn" and exact == "a":
        return jnp.zeros_like(a), _xdot(a, dc, "tn", "a")
    assert form == "tn" and exact == "b"
    return _xdot(b, dc, "nt", "a"), jnp.zeros_like(b)


_xdot.defvjp(_xdot_fwd, _xdot_bwd)


def _spec2(block, idx, split=None):
    if split is None:
        return pl.BlockSpec(tuple(block), idx)
    per = split // block[1]

    def idx3(*g):
        r, cblk = idx(*g)
        return (cblk // per, r, cblk % per)

    return pl.BlockSpec((None,) + tuple(block), idx3)


def _matmul(a, b, *, form, m, n, k, tm, tn, tk, out_dtype, name, bias=None, a_fn=None, b_fn=None,
            epi=None, epi_in=None, colsum=False, kind="mxu", b_split=None, o_split=None):
    assert m % tm == 0 and n % tn == 0 and k % tk == 0, (name, m, n, k, tm, tn, tk)
    nk = k // tk
    grid = (n // tn, m // tm, nk)
    if form == "tn":
        a_spec = pl.BlockSpec((tk, tm), lambda j, i, kk: (kk, i))
    else:
        a_spec = pl.BlockSpec((tm, tk), lambda j, i, kk: (i, kk))
    if form == "nt":
        b_spec = _spec2((tn, tk), lambda j, i, kk: (j, kk), b_split)
    else:
        b_spec = _spec2((tk, tn), lambda j, i, kk: (kk, j), b_split)
    in_specs = [a_spec, b_spec]
    operands = [a, b]
    if bias is not None:
        in_specs.append(pl.BlockSpec((1, tn), lambda j, i, kk: (0, j)))
        operands.append(bias)
    if epi_in is not None:
        in_specs.append(pl.BlockSpec((tm, tn), lambda j, i, kk: (i, j)))
        operands.append(epi_in)
    out_specs = [_spec2((tm, tn), lambda j, i, kk: (i, j), o_split)]
    if o_split is None:
        out_shape = [jax.ShapeDtypeStruct((m, n), out_dtype)]
    else:
        out_shape = [jax.ShapeDtypeStruct((n // o_split, m, o_split), out_dtype)]
    if colsum:
        out_specs.append(pl.BlockSpec((1, tn), lambda j, i, kk: (0, j)))
        out_shape.append(jax.ShapeDtypeStruct((1, n), F32))
    has_bias, has_epi_in = bias is not None, epi_in is not None

    def body(*refs):
        refs = list(refs)
        a_ref, b_ref = refs[0], refs[1]
        pos = 2
        bias_ref = epi_ref = cs_ref = None
        if has_bias:
            bias_ref = refs[pos]
            pos += 1
        if has_epi_in:
            epi_ref = refs[pos]
            pos += 1
        o_ref = refs[pos]
        pos += 1
        if colsum:
            cs_ref = refs[pos]
            pos += 1
        acc_ref = refs[pos]
        i, kk = pl.program_id(1), pl.program_id(2)

        @pl.when(kk == 0)
        def _():
            acc_ref[...] = jnp.zeros_like(acc_ref)

        av = a_ref[...]
        if a_fn is not None:
            av = a_fn(av)
        bv = b_ref[...]
        if b_fn is not None:
            bv = b_fn(bv)
        acc_ref[...] += _raw_dot(av, bv, form, kind)

        @pl.when(kk == nk - 1)
        def _():
            r = acc_ref[...]
            if has_bias:
                r = r + bias_ref[...]
            if epi is not None:
                r = epi(r, epi_ref[...]) if has_epi_in else epi(r)
            o_ref[...] = r.astype(out_dtype)
            if colsum:
                s = jnp.sum(r, axis=0, keepdims=True)

                @pl.when(i == 0)
                def _():
                    cs_ref[...] = s

                @pl.when(i > 0)
                def _():
                    cs_ref[...] += s

    est = 2 * (_nbytes((tm, tk), a.dtype) + _nbytes((tk, tn), b.dtype) + _nbytes((tm, tn), out_dtype))
    est += _nbytes((tm, tn), F32) * 2 + (2 * _nbytes((tm, tn), epi_in.dtype) if has_epi_in else 0)
    res = pl.pallas_call(
        body, grid=grid, in_specs=in_specs, out_specs=out_specs, out_shape=out_shape,
        scratch_shapes=[pltpu.VMEM((tm, tn), F32)], name=name,
        compiler_params=_cparams(("arbitrary", "arbitrary", "arbitrary"), est),
    )(*operands)
    return res if colsum else res[0]


def _rowwise(fn, rows, params, row_outs, acc_outs, *, n_rows, tm, name):
    assert n_rows % tm == 0, (name, n_rows, tm)
    nr, npar, nro, nac = len(rows), len(params), len(row_outs), len(acc_outs)
    in_specs = [pl.BlockSpec((tm, w), functools.partial(lambda i, cb, ro: (i + ro, cb), cb=cb, ro=ro))
                for (_, w, cb, ro) in rows]
    in_specs += [pl.BlockSpec((1, p.shape[1]), lambda i: (0, 0)) for p in params]
    out_specs = [pl.BlockSpec((tm, w), lambda i: (i, 0)) for (w, _) in row_outs]
    out_specs += [pl.BlockSpec((1, w), lambda i: (0, 0)) for w in acc_outs]
    out_shape = [jax.ShapeDtypeStruct((n_rows, w), dt) for (w, dt) in row_outs]
    out_shape += [jax.ShapeDtypeStruct((1, w), F32) for w in acc_outs]

    def body(*refs):
        rv = [r[...] for r in refs[:nr]]
        pv = [r[...] for r in refs[nr:nr + npar]]
        ro_refs = refs[nr + npar:nr + npar + nro]
        ac_refs = refs[nr + npar + nro:]
        ro, ac = fn(rv, pv)
        for ref, val in zip(ro_refs, ro, strict=True):
            ref[...] = val.astype(ref.dtype)
        first = pl.program_id(0) == 0
        for ref, val in zip(ac_refs, ac, strict=True):
            val = jnp.broadcast_to(val.astype(F32), ref.shape)

            @pl.when(first)
            def _(ref=ref, val=val):
                ref[...] = val

            @pl.when(jnp.logical_not(first))
            def _(ref=ref, val=val):
                ref[...] += val

    io = sum(_nbytes((tm, w), a.dtype) for (a, w, _, _) in rows) + sum(_nbytes((tm, w), dt) for (w, dt) in row_outs)
    widest = max([w for (_, w, _, _) in rows] + [w for (w, _) in row_outs])
    est = 2 * io + 12 * _nbytes((tm, widest), F32)
    return pl.pallas_call(
        body, grid=(n_rows // tm,), in_specs=in_specs, out_specs=out_specs, out_shape=out_shape, name=name,
        compiler_params=_cparams(("arbitrary",), est),
    )(*[a for (a, _, _, _) in rows], *params)


def _conv7_tiles(n_lat, n_ctx):
    tt = min(256, n_ctx)
    assert n_lat % tt == 0 and n_ctx % tt == 0
    return tt, [(t0, 8 + t0) for t0 in range(0, n_lat, tt)] + [(n_lat + t0, 16 + n_lat + t0) for t0 in range(0, n_ctx, tt)]


def _conv7_fill(p_ref, x_ref, n_lat, n_ctx):
    z8 = jnp.zeros((8, LANES), F32)
    p_ref[0:8, :] = z8
    p_ref[8:8 + n_lat, :] = x_ref[0:n_lat, :]
    p_ref[8 + n_lat:16 + n_lat, :] = z8
    p_ref[16 + n_lat:16 + n_lat + n_ctx, :] = x_ref[n_lat:n_lat + n_ctx, :]
    p_ref[16 + n_lat + n_ctx:24 + n_lat + n_ctx, :] = z8


def _conv7_fwd(h, w_conv, *, n_lat, n_ctx, nh):
    nt = n_lat + n_ctx
    tt, tiles = _conv7_tiles(n_lat, n_ctx)
    half = SHORT_CONV // 2

    def body(x_ref, w_ref, o_ref, p_ref):
        _conv7_fill(p_ref, x_ref, n_lat, n_ctx)
        for (row, prow) in tiles:
            acc = jnp.zeros((tt, LANES), F32)
            for kk in range(SHORT_CONV):
                acc = acc + w_ref[kk:kk + 1, :] * p_ref[pl.ds(prow + kk - half, tt), :]
            o_ref[pl.ds(row, tt), :] = _silu(acc)

    return pl.pallas_call(
        body, grid=(3 * nh,),
        in_specs=[pl.BlockSpec((nt, LANES), lambda j: (0, j)), pl.BlockSpec((SHORT_CONV, LANES), lambda j: (0, j))],
        out_specs=pl.BlockSpec((None, None, nt, LANES), lambda j: (j // nh, j % nh, 0, 0)),
        out_shape=jax.ShapeDtypeStruct((3, nh, nt, LANES), F32),
        scratch_shapes=[pltpu.VMEM((nt + 24, LANES), F32)], name="conv7_fwd",
        compiler_params=_cparams(("arbitrary",), 5 * _nbytes((nt + 24, LANES), F32)),
    )(h, w_conv)


def _conv7_bwd(h, w_conv, dact_f, dact_b, *, n_lat, n_ctx, nh):
    nt = n_lat + n_ctx
    tt, tiles = _conv7_tiles(n_lat, n_ctx)
    half = SHORT_CONV // 2

    def body(x_ref, w_ref, df_ref, db_ref, dx_ref, dw_ref, p_ref, q_ref):
        _conv7_fill(p_ref, x_ref, n_lat, n_ctx)
        z8 = jnp.zeros((8, LANES), F32)
        q_ref[0:8, :] = z8
        q_ref[8 + n_lat:16 + n_lat, :] = z8
        q_ref[16 + n_lat + n_ctx:24 + n_lat + n_ctx, :] = z8
        dw = [jnp.zeros((1, LANES), F32) for _ in range(SHORT_CONV)]
        for (row, prow) in tiles:
            pre = jnp.zeros((tt, LANES), F32)
            for kk in range(SHORT_CONV):
                pre = pre + w_ref[kk:kk + 1, :] * p_ref[pl.ds(prow + kk - half, tt), :]
            s = jax.nn.sigmoid(pre)
            dpre = (df_ref[pl.ds(row, tt), :] + db_ref[pl.ds(row, tt), :]) * (s * (1.0 + pre * (1.0 - s)))
            q_ref[pl.ds(prow, tt), :] = dpre
            for kk in range(SHORT_CONV):
                dw[kk] = dw[kk] + jnp.sum(dpre * p_ref[pl.ds(prow + kk - half, tt), :], axis=0, keepdims=True)
        for (row, prow) in tiles:
            acc = jnp.zeros((tt, LANES), F32)
            for kk in range(SHORT_CONV):
                acc = acc + w_ref[kk:kk + 1, :] * q_ref[pl.ds(prow + half - kk, tt), :]
            dx_ref[pl.ds(row, tt), :] = acc.astype(dx_ref.dtype)
        for kk in range(SHORT_CONV):
            dw_ref[kk:kk + 1, :] = dw[kk]

    dspec = pl.BlockSpec((None, None, nt, LANES), lambda j: (j // nh, j % nh, 0, 0))
    return pl.pallas_call(
        body, grid=(3 * nh,),
        in_specs=[pl.BlockSpec((nt, LANES), lambda j: (0, j)), pl.BlockSpec((SHORT_CONV, LANES), lambda j: (0, j)), dspec, dspec],
        out_specs=[pl.BlockSpec((nt, LANES), lambda j: (0, j)), pl.BlockSpec((SHORT_CONV, LANES), lambda j: (0, j))],
        out_shape=[jax.ShapeDtypeStruct((nt, 3 * nh * LANES), MXU_DTYPE), jax.ShapeDtypeStruct((SHORT_CONV, 3 * nh * LANES), F32)],
        scratch_shapes=[pltpu.VMEM((nt + 24, LANES), F32), pltpu.VMEM((nt + 24, LANES), F32)], name="conv7_bwd",
        compiler_params=_cparams(("arbitrary",), 10 * _nbytes((nt + 24, LANES), F32)),
    )(h, w_conv, dact_f, dact_b)


_RT = 4
_CP = CONF_K // 2


def _conf_pad_shape(gw, vertical):
    return (gw + 2 * _CP, gw, LANES) if vertical else (gw, gw + 32, LANES)


def _conf_fill(p_ref, val, gw, vertical):
    if vertical:
        p_ref[0:_CP] = jnp.zeros((_CP, gw, LANES), F32)
        p_ref[_CP + gw:2 * _CP + gw] = jnp.zeros((_CP, gw, LANES), F32)
        p_ref[_CP:_CP + gw] = val
    else:
        p_ref[:, 0:16, :] = jnp.zeros((gw, 16, LANES), F32)
        p_ref[:, 16 + gw:32 + gw, :] = jnp.zeros((gw, 16, LANES), F32)
        p_ref[:, 16:16 + gw, :] = val


def _conf_window(p_ref, r0, shift, gw, vertical):
    if vertical:
        return p_ref[pl.ds(r0 + _CP + shift, _RT), :, :]
    return p_ref[pl.ds(r0, _RT), pl.ds(16 + shift, gw), :]


def _conf_conv_fwd(h3, w_dw, *, gw, col0, half, vertical):
    cb_val = (col0 + (half if vertical else 0)) // LANES
    cb_gate = cb_val + 2 * half // LANES
    cb_w = (half if vertical else 0) // LANES

    def body(v_ref, g_ref, w_ref, o_ref, p_ref):
        _conf_fill(p_ref, v_ref[...] * jax.nn.sigmoid(g_ref[...]), gw, vertical)

        def step(t, carry):
            r0 = t * _RT
            acc = jnp.zeros((_RT, gw, LANES), F32)
            for kk in range(CONF_K):
                acc = acc + w_ref[kk:kk + 1, :] * _conf_window(p_ref, r0, kk - _CP, gw, vertical)
            o_ref[pl.ds(r0, _RT)] = acc
            return carry

        lax.fori_loop(0, gw // _RT, step, 0)

    blk = (gw, gw, LANES)
    return pl.pallas_call(
        body, grid=(half // LANES,),
        in_specs=[pl.BlockSpec(blk, lambda j: (0, 0, cb_val + j)), pl.BlockSpec(blk, lambda j: (0, 0, cb_gate + j)),
                  pl.BlockSpec((CONF_K, LANES), lambda j: (0, cb_w + j))],
        out_specs=pl.BlockSpec(blk, lambda j: (0, 0, j)),
        out_shape=jax.ShapeDtypeStruct((gw, gw, half), F32),
        scratch_shapes=[pltpu.VMEM(_conf_pad_shape(gw, vertical), F32)],
        name="conf_conv_fwd_v" if vertical else "conf_conv_fwd_h",
        compiler_params=_cparams(("arbitrary",), 8 * _nbytes(_conf_pad_shape(gw, vertical), F32)),
    )(h3, h3, w_dw)


def _conf_conv_bwd(h3, w_dw, dyc, *, gw, col0, half, vertical):
    cb_val = (col0 + (half if vertical else 0)) // LANES
    cb_gate = cb_val + 2 * half // LANES
    cb_w = (half if vertical else 0) // LANES

    def body(v_ref, g_ref, w_ref, d_ref, dv_ref, dg_ref, dw_ref, py_ref, pd_ref):
        _conf_fill(py_ref, v_ref[...] * jax.nn.sigmoid(g_ref[...]), gw, vertical)
        _conf_fill(pd_ref, d_ref[...], gw, vertical)

        def step(t, carry):
            r0 = t * _RT
            acc = jnp.zeros((_RT, gw, LANES), F32)
            for kk in range(CONF_K):
                acc = acc + w_ref[kk:kk + 1, :] * _conf_window(pd_ref, r0, _CP - kk, gw, vertical)
            val = v_ref[pl.ds(r0, _RT)]
            sg = jax.nn.sigmoid(g_ref[pl.ds(r0, _RT)])
            dv_ref[pl.ds(r0, _RT)] = (acc * sg).astype(dv_ref.dtype)
            dg_ref[pl.ds(r0, _RT)] = (acc * val * sg * (1.0 - sg)).astype(dg_ref.dtype)
            return carry

        lax.fori_loop(0, gw // _RT, step, 0)

        for kk in range(CONF_K):
            def wstep(t, acc, kk=kk):
                r0 = t * _RT
                prod = d_ref[pl.ds(r0, _RT)] * _conf_window(py_ref, r0, kk - _CP, gw, vertical)
                return acc + jnp.sum(prod, axis=0)

            tot = lax.fori_loop(0, gw // _RT, wstep, jnp.zeros((gw, LANES), F32))
            dw_ref[kk:kk + 1, :] = jnp.sum(tot, axis=0, keepdims=True)

    blk = (gw, gw, LANES)
    pshape = _conf_pad_shape(gw, vertical)
    return pl.pallas_call(
        body, grid=(half // LANES,),
        in_specs=[pl.BlockSpec(blk, lambda j: (0, 0, cb_val + j)), pl.BlockSpec(blk, lambda j: (0, 0, cb_gate + j)),
                  pl.BlockSpec((CONF_K, LANES), lambda j: (0, cb_w + j)), pl.BlockSpec(blk, lambda j: (0, 0, j))],
        out_specs=[pl.BlockSpec(blk, lambda j: (0, 0, j)), pl.BlockSpec(blk, lambda j: (0, 0, j)),
                   pl.BlockSpec((CONF_K, LANES), lambda j: (0, j))],
        out_shape=[jax.ShapeDtypeStruct((gw, gw, half), MXU_DTYPE), jax.ShapeDtypeStruct((gw, gw, half), MXU_DTYPE),
                   jax.ShapeDtypeStruct((CONF_K, half), F32)],
        scratch_shapes=[pltpu.VMEM(pshape, F32), pltpu.VMEM(pshape, F32)],
        name="conf_conv_bwd_v" if vertical else "conf_conv_bwd_h",
        compiler_params=_cparams(("arbitrary",), 12 * _nbytes(pshape, F32)),
    )(h3, h3, w_dw, dyc)


_INV_BASE = 8


def _tri_inv_raw(mats):
    c = mats[0].shape[0]
    ri = lax.broadcasted_iota(jnp.int32, (c, c), 0)
    ci = lax.broadcasted_iota(jnp.int32, (c, c), 1)
    eye = (ri == ci).astype(F32)
    base = min(_INV_BASE, c)
    same = lambda size: (ri // size) == (ci // size)
    dmat = [jnp.where(same(base), a, 0.0) for a in mats]
    x = [eye - dm for dm in dmat]
    pw = [_raw_dot(dm, dm, "nn", "hi3") for dm in dmat]
    span = 2
    while span < base:
        x = [xi + _raw_dot(xi, p, "nn", "hi3") for xi, p in zip(x, pw)]
        span *= 2
        if span < base:
            pw = [_raw_dot(p, p, "nn", "hi3") for p in pw]
    size = base
    while size < c:
        off = jnp.logical_and(jnp.logical_not(same(size)), same(2 * size))
        t = [_raw_dot(xi, jnp.where(off, a, 0.0), "nn", "hi3") for xi, a in zip(x, mats)]
        x = [xi - _raw_dot(ti, xi, "nn", "hi3") for xi, ti in zip(x, t)]
        size *= 2
    return x


@jax.custom_vjp
def _tri_inv(mats):
    return _tri_inv_raw(mats)


def _tri_inv_fwd(mats):
    x = _tri_inv_raw(mats)
    return x, x


def _tri_inv_bwd(x, dx):
    t = [_raw_dot(xi, di, "tn", "hi3") for xi, di in zip(x, dx)]
    return ([-_raw_dot(ti, xi, "nt", "hi3") for ti, xi in zip(t, x)],)


_tri_inv.defvjp(_tri_inv_fwd, _tri_inv_bwd)


def _delta_chunk(qa, ka, va, graw, braw, alog, dtb, s, *, rev):
    c = qa[0].shape[0]

    def each(f, *lists):
        return [f(*xs) for xs in zip(*lists, strict=True)]

    def l2n(t):
        return t * lax.rsqrt(jnp.sum(t * t, axis=-1, keepdims=True) + 1e-6)

    ri = lax.broadcasted_iota(jnp.int32, (c, c), 0)
    ci = lax.broadcasted_iota(jnp.int32, (c, c), 1)
    incl = (ci >= ri) if rev else (ci <= ri)
    strict = (ci > ri) if rev else (ci < ri)
    tmat = incl.astype(F32)
    tmat_t = ((ri >= ci) if rev else (ri <= ci)).astype(F32)

    q = each(lambda t: l2n(t) * (HD ** -0.5), qa)
    k = each(l2n, ka)
    g = each(lambda al, gr, dt: -jnp.exp(al) * _softplus(gr + dt), alog, graw, dtb)
    beta = each(jax.nn.sigmoid, braw)
    gc_wide = each(lambda t: _xdot(tmat, jnp.broadcast_to(t, (c, HD)), "nn", "a"), g)
    gc_rows = gc_wide if c == HD else each(lambda t: _xdot(tmat, jnp.broadcast_to(t, (c, c)), "nn", "a"), g)
    gc_cols = each(lambda t: _xdot(jnp.broadcast_to(t, (c, c)), tmat_t, "tn", "b"), g)
    gamma = each(lambda r, cc: jnp.where(incl, jnp.exp(jnp.where(incl, r - cc, 0.0)), 0.0), gc_rows, gc_cols)
    kb = each(lambda a, b: a * b, k, beta)
    a_mat = each(lambda a, b, gm: jnp.where(strict, _dot(a, b, "nt", "mxu") * gm, 0.0), kb, k, gamma)
    minv = _tri_inv(a_mat)
    eg = each(jnp.exp, gc_wide)
    u = each(lambda mi, v, b: _dot(mi, v * b, "nn", "hi3"), minv, va, beta)
    w = each(lambda mi, a, e: _dot(mi, a * e, "nn", "hi3"), minv, kb, eg)
    attn = each(lambda a, b, gm: _dot(a, b, "nt", "mxu") * gm, q, k, gamma)
    tot = each(lambda t: jnp.sum(t, axis=0, keepdims=True), g)
    q_dec = each(lambda a, e: a * e, q, eg)
    k_dec = each(lambda a, t, gw_: a * jnp.exp(t - gw_), k, tot, gc_wide)
    v_new = each(lambda uu, ww, ss: uu - _dot(ww, ss, "nn", "mxu"), u, w, s)
    o = each(lambda qd, ss, at, vn: _dot(qd, ss, "nn", "mxu") + _dot(at, vn, "nn", "mxu"), q_dec, s, attn, v_new)
    s_new = each(lambda ss, t, kd, vn: ss * jnp.exp(t) + _dot(kd, vn, "tn", "mxu"), s, tot, k_dec, v_new)
    return o, s_new


def _delta_args(a_ref, g_ref, al_ref, dt_ref, s_ref, hb, gcol):
    gts = [g_ref[hh] for hh in range(hb)]
    return ([a_ref[0, hh] for hh in range(hb)], [a_ref[1, hh] for hh in range(hb)], [a_ref[2, hh] for hh in range(hb)],
            [t[:, gcol:gcol + 1] for t in gts], [t[:, gcol + 1:gcol + 2] for t in gts],
            [al_ref[hh, 0:1, 0:1] for hh in range(hb)], [dt_ref[hh, 0:1, 0:1] for hh in range(hb)],
            [s_ref[hh] for hh in range(hb)])


def _delta_chunk_of_step(step, nch, nlc, rev):
    return (nch - 1 - step) if rev else (step + nlc) % nch


def _delta_fwd(act, gates_hm, alog_b, dtb_b, *, n_lat, n_ctx, nh, hb, rev):
    nt = n_lat + n_ctx
    nch, nlc = nt // CHUNK, n_lat // CHUNK
    gcol = 2 if rev else 0
    chunk = functools.partial(_delta_chunk_of_step, nch=nch, nlc=nlc, rev=rev)

    def body(a_ref, g_ref, al_ref, dt_ref, o_ref, st_ref, s_ref):
        n = pl.program_id(1)

        @pl.when(n == 0)
        def _():
            s_ref[...] = jnp.zeros_like(s_ref)

        args = _delta_args(a_ref, g_ref, al_ref, dt_ref, s_ref, hb, gcol)
        o, s_new = _delta_chunk(*args, rev=rev)
        for hh in range(hb):
            st_ref[hh] = args[7][hh]
            o_ref[:, hh * HD:(hh + 1) * HD] = o[hh]
            s_ref[hh] = s_new[hh]

    par = pl.BlockSpec((hb, 8, LANES), lambda h, n: (h, 0, 0))
    return pl.pallas_call(
        body, grid=(nh // hb, nch),
        in_specs=[pl.BlockSpec((3, hb, CHUNK, HD), lambda h, n: (0, h, chunk(n), 0)),
                  pl.BlockSpec((hb, CHUNK, 4), lambda h, n: (h, chunk(n), 0)), par, par],
        out_specs=[pl.BlockSpec((CHUNK, hb * HD), lambda h, n: (chunk(n), h)),
                   pl.BlockSpec((hb, None, HD, HD), lambda h, n: (h, n, 0, 0))],
        out_shape=[jax.ShapeDtypeStruct((nt, nh * HD), F32), jax.ShapeDtypeStruct((nh, nch, HD, HD), F32)],
        scratch_shapes=[pltpu.VMEM((hb, HD, HD), F32)], name="delta_fwd_rev" if rev else "delta_fwd",
        compiler_params=_cparams(("arbitrary", "arbitrary"), 0),
    )(act, gates_hm, alog_b, dtb_b)


def _delta_bwd(act, gates_hm, alog_b, dtb_b, states, do, *, n_lat, n_ctx, nh, hb, rev):
    nt = n_lat + n_ctx
    nch, nlc = nt // CHUNK, n_lat // CHUNK
    gcol = 2 if rev else 0

    def chunk(m):
        return _delta_chunk_of_step(nch - 1 - m, nch, nlc, rev)

    def body(a_ref, g_ref, al_ref, dt_ref, st_ref, do_ref, da_ref, dg_ref, dal_ref, ddt_ref, ds_ref):
        m = pl.program_id(1)

        @pl.when(m == 0)
        def _():
            ds_ref[...] = jnp.zeros_like(ds_ref)
            dal_ref[...] = jnp.zeros_like(dal_ref)
            ddt_ref[...] = jnp.zeros_like(ddt_ref)

        is_lat = chunk(m) < nlc
        args = _delta_args(a_ref, g_ref, al_ref, dt_ref, st_ref, hb, gcol)
        _, vjp = jax.vjp(functools.partial(_delta_chunk, rev=rev), *args)
        do_h = [jnp.where(is_lat, do_ref[:, hh * HD:(hh + 1) * HD], 0.0) for hh in range(hb)]
        grads = vjp((do_h, [ds_ref[hh] for hh in range(hb)]))
        for hh in range(hb):
            dq, dk, dv, dgr, dbr, dal, ddt, ds = [gr[hh] for gr in grads]
            da_ref[0, hh] = dq
            da_ref[1, hh] = dk
            da_ref[2, hh] = dv
            lane = lax.broadcasted_iota(jnp.int32, (CHUNK, 4), 1)
            dg_ref[hh] = jnp.where(lane == 0, dgr, jnp.where(lane == 1, dbr, 0.0))
            dal_ref[hh] += jnp.broadcast_to(dal, (8, LANES))
            ddt_ref[hh] += jnp.broadcast_to(ddt, (8, LANES))
            ds_ref[hh] = ds

    par = pl.BlockSpec((hb, 8, LANES), lambda h, m: (h, 0, 0))
    return pl.pallas_call(
        body, grid=(nh // hb, nch),
        in_specs=[pl.BlockSpec((3, hb, CHUNK, HD), lambda h, m: (0, h, chunk(m), 0)),
                  pl.BlockSpec((hb, CHUNK, 4), lambda h, m: (h, chunk(m), 0)), par, par,
                  pl.BlockSpec((hb, None, HD, HD), lambda h, m: (h, nch - 1 - m, 0, 0)),
                  pl.BlockSpec((CHUNK, hb * HD), lambda h, m: (jnp.minimum(chunk(m), nlc - 1), h))],
        out_specs=[pl.BlockSpec((3, hb, CHUNK, HD), lambda h, m: (0, h, chunk(m), 0)),
                   pl.BlockSpec((hb, CHUNK, 4), lambda h, m: (h, chunk(m), 0)), par, par],
        out_shape=[jax.ShapeDtypeStruct((3, nh, nt, HD), F32), jax.ShapeDtypeStruct((nh, nt, 4), F32),
                   jax.ShapeDtypeStruct((nh, 8, LANES), F32), jax.ShapeDtypeStruct((nh, 8, LANES), F32)],
        scratch_shapes=[pltpu.VMEM((hb, HD, HD), F32)], name="delta_bwd_rev" if rev else "delta_bwd",
        compiler_params=_cparams(("arbitrary", "arbitrary"), 0),
    )(act, gates_hm, alog_b, dtb_b, states, do)


def _my_pos():
    return lax.axis_index("x"), lax.axis_index("y"), lax.axis_index("c")


def _flip(v, d):
    return 1 - v if d else v


def _bcast8(block, name):
    r, w = block.shape
    assert r % 8 == 0 and w % LANES == 0

    def body(x_ref, o_ref, send_sems, recv_sems, local_sem):
        x, y, c = _my_pos()

        def rows(px, py, pc):
            return o_ref.at[pl.ds((4 * px + 2 * py + pc) * r, r), :]

        mine = pltpu.make_async_copy(x_ref, rows(x, y, c), local_sem)
        mine.start()
        copies = []
        for kk in range(1, 8):
            dx, dy, dc = (kk >> 2) & 1, (kk >> 1) & 1, kk & 1
            peer = (_flip(x, dx), _flip(y, dy), _flip(c, dc))
            cp = pltpu.make_async_remote_copy(src_ref=x_ref, dst_ref=rows(x, y, c), send_sem=send_sems.at[kk - 1],
                                              recv_sem=recv_sems.at[kk - 1], device_id=peer, device_id_type=pl.DeviceIdType.MESH)
            cp.start()
            copies.append((cp, peer))
        for kk, (cp, peer) in enumerate(copies):
            pltpu.make_async_remote_copy(src_ref=x_ref, dst_ref=rows(*peer), send_sem=send_sems.at[kk], recv_sem=recv_sems.at[kk],
                                         device_id=peer, device_id_type=pl.DeviceIdType.MESH).wait_recv()
        for cp, _ in copies:
            cp.wait_send()
        mine.wait()

    return pl.pallas_call(
        body, out_shape=jax.ShapeDtypeStruct((8 * r, w), block.dtype),
        in_specs=[pl.BlockSpec(memory_space=pltpu.VMEM)], out_specs=pl.BlockSpec(memory_space=pltpu.VMEM),
        scratch_shapes=[pltpu.SemaphoreType.DMA((7,)), pltpu.SemaphoreType.DMA((7,)), pltpu.SemaphoreType.DMA], name=name,
        compiler_params=pltpu.CompilerParams(vmem_limit_bytes=int(max(32 * 1024 * 1024, 12 * _nbytes((r, w), block.dtype)))),
    )(block)


_CHIP_PEERS = ((1, 0), (0, 1), (1, 1))


def _col_half(ref, c, lead):
    hc = ref.shape[-1] // 2
    return ref.at[(*lead, slice(None), pl.ds(pl.multiple_of(c * hc, LANES), hc))]


def _into_slot(w, chip_idx, name):
    r, cc = w.shape
    tr = _pick(r, 512, 16)

    def body(i_ref, w_ref, o_ref):
        o_ref[...] = w_ref[...].astype(o_ref.dtype)

    grid_spec = pltpu.PrefetchScalarGridSpec(
        num_scalar_prefetch=1, grid=(r // tr,), in_specs=[pl.BlockSpec((tr, cc), lambda i, s: (i, 0))],
        out_specs=pl.BlockSpec((None, tr, cc), lambda i, s: (s[0], i, 0)))
    return pl.pallas_call(body, grid_spec=grid_spec, out_shape=jax.ShapeDtypeStruct((4, r, cc), MXU_DTYPE), name=name,
                          compiler_params=_cparams(("arbitrary",), 6 * _nbytes((tr, cc), F32)))(chip_idx, w)


def _chip_gather(bufs, name):
    n = len(bufs)

    def body(*refs):
        outs = refs[n:2 * n]
        ici_send, ici_recv, d2d_send, d2d_recv = refs[2 * n:]
        x, y, c = _my_pos()
        me, sib = 2 * x + y, (x, y, 1 - c)

        def remote(a, kk, slot, half, to, sems):
            piece = _col_half(outs[a], half, (slot,))
            return pltpu.make_async_remote_copy(src_ref=piece, dst_ref=piece, send_sem=sems[0].at[3 * a + kk], recv_sem=sems[1].at[3 * a + kk],
                                                device_id=to, device_id_type=pl.DeviceIdType.MESH)

        sends = []
        for a in range(n):
            for kk, (dx, dy) in enumerate(_CHIP_PEERS):
                cp = remote(a, kk, me, c, (_flip(x, dx), _flip(y, dy), c), (ici_send, ici_recv))
                cp.start()
                sends.append(cp)
        for a in range(n):
            for kk, (dx, dy) in enumerate(_CHIP_PEERS):
                px, py = _flip(x, dx), _flip(y, dy)
                remote(a, kk, 2 * px + py, c, (px, py, c), (ici_send, ici_recv)).wait_recv()
                fw = remote(a, kk, 2 * px + py, c, sib, (d2d_send, d2d_recv))
                fw.start()
                sends.append(fw)
        for a in range(n):
            for kk, (dx, dy) in enumerate(_CHIP_PEERS):
                remote(a, kk, 2 * _flip(x, dx) + _flip(y, dy), 1 - c, sib, (d2d_send, d2d_recv)).wait_recv()
        for cp in sends:
            cp.wait_send()

    return pl.pallas_call(
        body, out_shape=[jax.ShapeDtypeStruct(b.shape, b.dtype) for b in bufs],
        in_specs=[ANY] * n, out_specs=[ANY] * n, input_output_aliases={a: a for a in range(n)},
        scratch_shapes=[pltpu.SemaphoreType.DMA((3 * n,))] * 4, name=name,
    )(*bufs)


_HBM = pl.BlockSpec(memory_space=pltpu.HBM)
_SEM = pl.BlockSpec(memory_space=pltpu.SEMAPHORE)
_DATAFLOW = pltpu.SideEffectType.DATAFLOW_SIDE_EFFECTING


def _in_hbm(a):
    return pltpu.with_memory_space_constraint(a, pltpu.HBM)


def _gather_start(bufs, after, name):
    n = len(bufs)

    def body(*refs):
        ins, send_sems, recv_sems, token = refs[:n], refs[n + 1], refs[n + 2], refs[-1]
        x, y, c = _my_pos()
        for a in range(n):
            piece = _col_half(ins[a], c, (2 * x + y,))
            for kk, (dx, dy) in enumerate(_CHIP_PEERS):
                pltpu.make_async_remote_copy(src_ref=piece, dst_ref=piece, send_sem=send_sems.at[3 * a + kk], recv_sem=recv_sems.at[3 * a + kk],
                                             device_id=(_flip(x, dx), _flip(y, dy), c), device_id_type=pl.DeviceIdType.MESH).start()
        token[...] = jnp.zeros_like(token)

    res = pl.pallas_call(
        body, name=name,
        out_shape=(pltpu.SemaphoreType.DMA((3 * n,)), pltpu.SemaphoreType.DMA((3 * n,)), *[pltpu.HBM(b.shape, b.dtype) for b in bufs],
                   jax.ShapeDtypeStruct((8, LANES), F32)),
        in_specs=[_HBM] * n + [ANY], out_specs=(_SEM, _SEM, *[_HBM] * n, pl.BlockSpec(memory_space=pltpu.VMEM)),
        input_output_aliases={a: 2 + a for a in range(n)}, compiler_params=pltpu.CompilerParams(has_side_effects=_DATAFLOW),
    )(*[_in_hbm(b) for b in bufs], after)
    return res[0], res[1], list(res[2:2 + n]), res[-1]


def _gather_wait(send_sems, recv_sems, bufs, after, name):
    n = len(bufs)

    def body(*refs):
        ins, s_sems, r_sems = refs[:n], refs[n], refs[n + 1]
        x, y, c = _my_pos()
        for a in range(n):
            for kk, (dx, dy) in enumerate(_CHIP_PEERS):
                px, py = _flip(x, dx), _flip(y, dy)
                cp = pltpu.make_async_remote_copy(src_ref=_col_half(ins[a], c, (2 * x + y,)), dst_ref=_col_half(ins[a], c, (2 * px + py,)),
                                                  send_sem=s_sems.at[3 * a + kk], recv_sem=r_sems.at[3 * a + kk], device_id=(px, py, c),
                                                  device_id_type=pl.DeviceIdType.MESH)
                cp.wait_send()
                cp.wait_recv()

    return pl.pallas_call(
        body, name=name, out_shape=[pltpu.HBM(b.shape, b.dtype) for b in bufs],
        in_specs=[_HBM] * n + [_SEM, _SEM, ANY], out_specs=[_HBM] * n, input_output_aliases={a: a for a in range(n)},
        compiler_params=pltpu.CompilerParams(has_side_effects=_DATAFLOW),
    )(*bufs, send_sems, recv_sems, after)


def _gather_forward(bufs, name):
    n = len(bufs)

    def body(*refs):
        outs = refs[n:2 * n]
        send_sems, recv_sems = refs[2 * n:]
        x, y, c = _my_pos()
        sib = (x, y, 1 - c)

        def remote(a, kk, half):
            dx, dy = _CHIP_PEERS[kk]
            piece = _col_half(outs[a], half, (2 * _flip(x, dx) + _flip(y, dy),))
            return pltpu.make_async_remote_copy(src_ref=piece, dst_ref=piece, send_sem=send_sems.at[3 * a + kk], recv_sem=recv_sems.at[3 * a + kk],
                                                device_id=sib, device_id_type=pl.DeviceIdType.MESH)

        sends = [remote(a, kk, c) for a in range(n) for kk in range(3)]
        for cp in sends:
            cp.start()
        for a in range(n):
            for kk in range(3):
                remote(a, kk, 1 - c).wait_recv()
        for cp in sends:
            cp.wait_send()

    return pl.pallas_call(
        body, out_shape=[jax.ShapeDtypeStruct(b.shape, b.dtype) for b in bufs],
        in_specs=[ANY] * n, out_specs=[ANY] * n, input_output_aliases={a: a for a in range(n)},
        scratch_shapes=[pltpu.SemaphoreType.DMA((3 * n,))] * 2, name=name,
    )(*bufs)


def _scatter_start(parts, after, name):
    n = len(parts)

    def body(*refs):
        ins, lands, send_sems, recv_sems, token = refs[:n], refs[n:2 * n], refs[2 * n + 1], refs[2 * n + 2], refs[-1]
        x, y, c = _my_pos()
        for a in range(n):
            for kk, (dx, dy) in enumerate(_CHIP_PEERS):
                px, py = _flip(x, dx), _flip(y, dy)
                pltpu.make_async_remote_copy(src_ref=ins[a].at[2 * px + py], dst_ref=lands[a].at[2 * x + y], send_sem=send_sems.at[3 * a + kk],
                                             recv_sem=recv_sems.at[3 * a + kk], device_id=(px, py, c), device_id_type=pl.DeviceIdType.MESH).start()
        token[...] = jnp.zeros_like(token)

    thru = [pltpu.HBM(p.shape, p.dtype) for p in parts]
    res = pl.pallas_call(
        body, name=name,
        out_shape=(pltpu.SemaphoreType.DMA((3 * n,)), pltpu.SemaphoreType.DMA((3 * n,)), *thru, *thru, jax.ShapeDtypeStruct((8, LANES), F32)),
        in_specs=[_HBM] * (2 * n) + [ANY], out_specs=(_SEM, _SEM, *[_HBM] * (2 * n), pl.BlockSpec(memory_space=pltpu.VMEM)),
        input_output_aliases={a: 2 + a for a in range(2 * n)}, compiler_params=pltpu.CompilerParams(has_side_effects=_DATAFLOW),
    )(*[_in_hbm(p) for p in parts], *[_in_hbm(lax.empty(p.shape, p.dtype)) for p in parts], after)
    return res[0], res[1], list(res[2:2 + n]), list(res[2 + n:2 + 2 * n]), res[-1]


def _scatter_wait(send_sems, recv_sems, parts, lands, after, name):
    n = len(parts)

    def body(*refs):
        ins, lnd, s_sems, r_sems = refs[:n], refs[n:2 * n], refs[2 * n], refs[2 * n + 1]
        x, y, c = _my_pos()
        for a in range(n):
            for kk, (dx, dy) in enumerate(_CHIP_PEERS):
                px, py = _flip(x, dx), _flip(y, dy)
                cp = pltpu.make_async_remote_copy(src_ref=ins[a].at[2 * px + py], dst_ref=lnd[a].at[2 * px + py], send_sem=s_sems.at[3 * a + kk],
                                                  recv_sem=r_sems.at[3 * a + kk], device_id=(px, py, c), device_id_type=pl.DeviceIdType.MESH)
                cp.wait_send()
                cp.wait_recv()

    thru = [pltpu.HBM(p.shape, p.dtype) for p in parts]
    res = pl.pallas_call(
        body, name=name, out_shape=[*thru, *thru],
        in_specs=[_HBM] * (2 * n) + [_SEM, _SEM, ANY], out_specs=[_HBM] * (2 * n), input_output_aliases={a: a for a in range(2 * n)},
        compiler_params=pltpu.CompilerParams(has_side_effects=_DATAFLOW),
    )(*parts, *lands, send_sems, recv_sems, after)
    return list(res[:n]), list(res[n:])


def _chip_scatter(parts, name):
    n = len(parts)

    def body(*refs):
        ins, outs = refs[:n], refs[n:2 * n]
        send_sems, recv_sems = refs[2 * n:]
        x, y, c = _my_pos()
        me = 2 * x + y
        sends = []
        for a in range(n):
            for kk, (dx, dy) in enumerate(_CHIP_PEERS):
                px, py = _flip(x, dx), _flip(y, dy)
                cp = pltpu.make_async_remote_copy(src_ref=ins[a].at[2 * px + py], dst_ref=outs[a].at[me], send_sem=send_sems.at[3 * a + kk],
                                                  recv_sem=recv_sems.at[3 * a + kk], device_id=(px, py, c),
                                                  device_id_type=pl.DeviceIdType.MESH)
                cp.start()
                sends.append(cp)
        for a in range(n):
            for kk, (dx, dy) in enumerate(_CHIP_PEERS):
                px, py = _flip(x, dx), _flip(y, dy)
                pltpu.make_async_remote_copy(src_ref=ins[a].at[me], dst_ref=outs[a].at[2 * px + py], send_sem=send_sems.at[3 * a + kk],
                                             recv_sem=recv_sems.at[3 * a + kk], device_id=(px, py, c),
                                             device_id_type=pl.DeviceIdType.MESH).wait_recv()
        for cp in sends:
            cp.wait_send()

    return pl.pallas_call(
        body, out_shape=[jax.ShapeDtypeStruct(p.shape, p.dtype) for p in parts],
        in_specs=[ANY] * n, out_specs=[ANY] * n,
        scratch_shapes=[pltpu.SemaphoreType.DMA((3 * n,)), pltpu.SemaphoreType.DMA((3 * n,))],
        name=name,
    )(*parts)


def _pair_send_halves(grads, name):
    n = len(grads)

    def body(*refs):
        ins, outs = refs[:n], refs[n:2 * n]
        send_sems, recv_sems = refs[2 * n:]
        x, y, c = _my_pos()
        sib = (x, y, 1 - c)
        sends = []
        for a in range(n):
            cp = pltpu.make_async_remote_copy(src_ref=_col_half(ins[a], 1 - c, (slice(None),)), dst_ref=outs[a], send_sem=send_sems.at[a],
                                              recv_sem=recv_sems.at[a], device_id=sib, device_id_type=pl.DeviceIdType.MESH)
            cp.start()
            sends.append(cp)
        for cp in sends:
            cp.wait_recv()
        for cp in sends:
            cp.wait_send()

    return pl.pallas_call(
        body, out_shape=[jax.ShapeDtypeStruct((4, g.shape[1], g.shape[2] // 2), g.dtype) for g in grads],
        in_specs=[ANY] * n, out_specs=[ANY] * n,
        scratch_shapes=[pltpu.SemaphoreType.DMA((n,)), pltpu.SemaphoreType.DMA((n,))], name=name,
    )(*grads)


def _pair_join_halves(bufs, name):
    n = len(bufs)

    def body(*refs):
        outs = refs[n:2 * n]
        send_sems, recv_sems = refs[2 * n:]
        x, y, c = _my_pos()
        sib = (x, y, 1 - c)
        sends = []
        for a in range(n):
            mine = _col_half(outs[a], c, ())
            cp = pltpu.make_async_remote_copy(src_ref=mine, dst_ref=mine, send_sem=send_sems.at[a], recv_sem=recv_sems.at[a],
                                              device_id=sib, device_id_type=pl.DeviceIdType.MESH)
            cp.start()
            sends.append(cp)
        for a in range(n):
            other = _col_half(outs[a], 1 - c, ())
            pltpu.make_async_remote_copy(src_ref=other, dst_ref=other, send_sem=send_sems.at[a], recv_sem=recv_sems.at[a],
                                         device_id=sib, device_id_type=pl.DeviceIdType.MESH).wait_recv()
        for cp in sends:
            cp.wait_send()

    return pl.pallas_call(
        body, out_shape=[jax.ShapeDtypeStruct(b.shape, b.dtype) for b in bufs],
        in_specs=[ANY] * n, out_specs=[ANY] * n, input_output_aliases={a: a for a in range(n)},
        scratch_shapes=[pltpu.SemaphoreType.DMA((n,)), pltpu.SemaphoreType.DMA((n,))], name=name,
    )(*bufs)


def _add_my_half(g, recv, c_idx, name):
    _, r, cc = g.shape
    hc = cc // 2
    tc = _pick(hc, 256)
    per = hc // tc

    def body(c_ref, g_ref, r_ref, o_ref):
        o_ref[...] = (g_ref[...] + r_ref[...]).astype(o_ref.dtype)

    grid_spec = pltpu.PrefetchScalarGridSpec(
        num_scalar_prefetch=1, grid=(4, per),
        in_specs=[pl.BlockSpec((None, r, tc), lambda s, j, c_ref: (s, 0, c_ref[0] * per + j)),
                  pl.BlockSpec((None, r, tc), lambda s, j, c_ref: (s, 0, j))],
        out_specs=pl.BlockSpec((None, r, tc), lambda s, j, c_ref: (s, 0, j)))
    return pl.pallas_call(body, grid_spec=grid_spec, out_shape=jax.ShapeDtypeStruct((4, r, hc), WIRE_DTYPE), name=name,
                          compiler_params=_cparams(("arbitrary", "arbitrary"), 6 * _nbytes((r, tc), F32)))(c_idx, g, recv)


def _sum_slots(own, recv, slots, name):
    _, r, hc = own.shape
    tc = _pick(hc, 256)
    per = hc // tc

    def body(s0, s1, s2, s3, s4, a_ref, b_ref, c_ref, d_ref, o_ref):
        f = lambda ref: ref[...].astype(F32)
        o_ref[...] = ((f(a_ref) + f(b_ref)) + f(c_ref)) + f(d_ref)

    slot = lambda i: pl.BlockSpec((None, r, tc), lambda j, *s: (s[i][0], 0, j))
    grid_spec = pltpu.PrefetchScalarGridSpec(
        num_scalar_prefetch=5, grid=(per,), in_specs=[slot(0), slot(1), slot(2), slot(3)],
        out_specs=pl.BlockSpec((r, tc), lambda j, *s: (0, s[4][0] * per + j)))
    return pl.pallas_call(body, grid_spec=grid_spec, out_shape=jax.ShapeDtypeStruct((r, 2 * hc), F32), name=name,
                          compiler_params=_cparams(("arbitrary",), 14 * _nbytes((r, tc), F32)))(*slots, own, recv, recv, recv)


def _adamw_math(w, g, m, v):
    m = ADAM_B1 * m + (1.0 - ADAM_B1) * g
    v = ADAM_B2 * v + (1.0 - ADAM_B2) * (g * g)
    m_hat = m / (1.0 - ADAM_B1 ** ADAM_STEP)
    v_hat = v / (1.0 - ADAM_B2 ** ADAM_STEP)
    delta = -ADAM_LR * (m_hat / (jnp.sqrt(v_hat) + ADAM_EPS) + ADAM_WD * w)
    return delta, m, v


def _adamw(w, g, m, v, name):
    r, cc = w.shape
    tr = _pick(r, max(8, (512 * 1024) // max(cc, 1)), 8)

    def body(w_ref, g_ref, m_ref, v_ref, d_ref, nm_ref, nv_ref):
        d, nm, nv = _adamw_math(w_ref[...], g_ref[...], m_ref[...], v_ref[...])
        d_ref[...] = d
        nm_ref[...] = nm
        nv_ref[...] = nv

    spec = pl.BlockSpec((tr, cc), lambda i: (i, 0))
    return pl.pallas_call(
        body, grid=(r // tr,), in_specs=[spec] * 4, out_specs=[spec] * 3,
        out_shape=[jax.ShapeDtypeStruct((r, cc), F32)] * 3, name=name,
        compiler_params=_cparams(("arbitrary",), 16 * _nbytes((tr, cc), F32)))(w, g, m, v)


def _sum8(allv, name):
    r8, w = allv.shape
    r = r8 // 8

    def body(a_ref, o_ref):
        acc = a_ref[0:r, :]
        for d in range(1, 8):
            acc = acc + a_ref[d * r:(d + 1) * r, :]
        o_ref[...] = acc

    return pl.pallas_call(body, out_shape=jax.ShapeDtypeStruct((r, w), F32), name=name,
                          compiler_params=_cparams((), 12 * _nbytes((r8, w), F32)))(allv)


def _dmod_prep(alld, name):
    _, w = alld.shape
    tw = _pick(w, 2048)

    def body(a_ref, d_ref, b_ref):
        ctx = a_ref[1:2, :]
        tot = a_ref[0:1, :]
        d_ref[0:1, :] = tot
        for d in range(1, 8):
            ctx = ctx + a_ref[8 * d + 1:8 * d + 2, :]
            tot = tot + a_ref[8 * d:8 * d + 1, :]
            d_ref[d:d + 1, :] = a_ref[8 * d:8 * d + 1, :]
        d_ref[8:9, :] = ctx
        d_ref[9:16, :] = jnp.zeros((7, tw), F32)
        b_ref[...] = tot + ctx

    return pl.pallas_call(
        body, grid=(w // tw,), in_specs=[pl.BlockSpec((64, tw), lambda j: (0, j))],
        out_specs=[pl.BlockSpec((16, tw), lambda j: (0, j)), pl.BlockSpec((1, tw), lambda j: (0, j))],
        out_shape=[jax.ShapeDtypeStruct((16, w), F32), jax.ShapeDtypeStruct((1, w), F32)], name=name,
        compiler_params=_cparams(("arbitrary",), 0))(alld)


def _pack(arrs, rows=8):
    flat = jnp.concatenate([a.reshape(-1).astype(F32) for a in arrs])
    per = rows * LANES
    n = flat.shape[0]
    padded = -(-n // per) * per
    flat = jnp.pad(flat, (0, padded - n))
    return flat.reshape(rows, padded // rows)


def _unpack(slab, shapes):
    flat = slab.reshape(-1)
    out, off = [], 0
    for s in shapes:
        n = math.prod(s)
        out.append(flat[off:off + n].reshape(s))
        off += n
    return out


def kernel(x, c, ctx, c_ctx, ln_in_g, ln_in_b, w_mod, b_mod, w_in, w_qkv_conv, a_log_f, dt_bias_f, a_log_b, dt_bias_b, dn_norm_g, conf_dw_w, conf_dw_b, conf_ln_g, conf_ln_b, w_out, ln1_g, ln1_b, w_mlp1, b_mlp1, w_mlp2, b_mlp2, ln2_g, ln2_b, loss_target, m_c_ctx, m_ln_in_g, m_ln_in_b, m_w_mod, m_b_mod, m_w_in, m_w_qkv_conv, m_a_log_f, m_dt_bias_f, m_a_log_b, m_dt_bias_b, m_dn_norm_g, m_conf_dw_w, m_conf_dw_b, m_conf_ln_g, m_conf_ln_b, m_w_out, m_ln1_g, m_ln1_b, m_w_mlp1, m_b_mlp1, m_w_mlp2, m_b_mlp2, m_ln2_g, m_ln2_b, v_c_ctx, v_ln_in_g, v_ln_in_b, v_w_mod, v_b_mod, v_w_in, v_w_qkv_conv, v_a_log_f, v_dt_bias_f, v_a_log_b, v_dt_bias_b, v_dn_norm_g, v_conf_dw_w, v_conf_dw_b, v_conf_ln_g, v_conf_ln_b, v_w_out, v_ln1_g, v_ln1_b, v_w_mlp1, v_b_mlp1, v_w_mlp2, v_b_mlp2, v_ln2_g, v_ln2_b):
    n_lat, d = x.shape[1], x.shape[2]
    n_ctx = ctx.shape[1]
    nt = n_lat + n_ctx
    dn = d // 2
    nh = dn // HD
    conf = d - dn
    half = conf // 2
    gw = math.isqrt(n_lat)
    ff = 4 * w_mlp1.shape[2]
    fl = w_mlp1.shape[2]
    in_cols = 4 * w_in.shape[2]
    r_in = w_in.shape[2]
    n_gate = 4 * nh
    gb = max(LANES, d // 8)
    g0 = 4 * dn + 2 * conf
    in_pad = g0 + gb
    ml = w_mod.shape[2]
    assert gw * gw == n_lat and nt % gw == 0 and in_cols == 4 * dn + n_gate + 2 * conf and nh * HD == dn
    hb = 4 if nh % 4 == 0 else (2 if nh % 2 == 0 else 1)

    mx, my, mc = _my_pos()
    chip = 2 * mx + my
    dev = 4 * mx + 2 * my + mc
    c_idx = jnp.reshape(mc, (1,)).astype(jnp.int32)

    x2, ctx2, tgt2 = x[0], ctx[0], loss_target[0]
    row = lambda a: a.reshape(1, -1)

    e0_shapes = [(d,), w_qkv_conv.shape[1:], conf_dw_w.shape[1:]]
    e0 = _bcast8(_pack([c[0], w_qkv_conv[0], conf_dw_w[0]]), "bcast_inputs")
    e0 = e0.reshape(8, -1)
    per_dev = [_unpack(e0[dd], e0_shapes) for dd in range(8)]
    all_c = jnp.stack([p[0] for p in per_dev])
    w_conv = jnp.concatenate([per_dev[2 * j][1] for j in range(4)], axis=1)
    w_dw = jnp.concatenate([per_dev[2 * j][2] for j in range(4)], axis=1)

    s16 = jnp.concatenate([all_c, c_ctx[None, :], jnp.zeros((7, d), F32)], axis=0)
    b_mod_loc = lax.dynamic_slice(b_mod, (0, chip * ml), (1, ml))
    to_mxu = lambda t: t.astype(MXU_DTYPE)
    modblk = _matmul(s16, w_mod[0], form="nn", m=16, n=ml, k=d, tm=16, tn=_pick(ml, 1024), tk=_pick(d, 1024),
                     out_dtype=F32, name="mod_fwd", bias=b_mod_loc, a_fn=lambda t: to_mxu(_silu(t)), b_fn=to_mxu)
    allmod = _bcast8(modblk, "bcast_mod").reshape(8, 16, ml)[0::2]
    mod_mine = lax.dynamic_index_in_dim(allmod, dev, axis=1, keepdims=False).reshape(1, 4 * ml)
    mod_ctx = allmod[:, 8, :].reshape(1, 4 * ml)
    sh_a, sc_a, g_a, sh_m, sc_m, g_m = [mod_mine[:, i * d:(i + 1) * d] for i in range(6)]
    csh_a, csc_a = mod_ctx[:, 0:d], mod_ctx[:, d:2 * d]

    chip_idx = jnp.reshape(chip, (1,)).astype(jnp.int32)
    win_t_loc = jnp.transpose(w_in[0]).astype(MXU_DTYPE)
    slots = [_into_slot(w, chip_idx, f"weights_into_slot_{i}") for i, w in enumerate([win_t_loc, w_out[0], w_mlp1[0], w_mlp2[0]])]
    gw_in, = _chip_gather(slots[:1], "gather_w_in")
    gs_send, gs_recv, gs_bufs, gs_token = _gather_start(slots[1:], gw_in, "gather_rest_start")
    sh_a = sh_a + gs_token[0:1, 0:1]
    win_t = gw_in.reshape(in_cols, d)
    win_t = jnp.concatenate([win_t[:4 * dn], win_t[4 * dn + n_gate:], win_t[4 * dn:4 * dn + n_gate],
                             jnp.zeros((gb - n_gate, d), MXU_DTYPE)], axis=0)

    tmr = _pick(n_lat, 128, 8)
    tmc = _pick(n_ctx, 128, 8)
    ln_in = [row(ln_in_g), row(ln_in_b)]

    def stage_a(rv, pv):
        x0 = _ln(rv[0], pv[0], pv[1])
        return [x0 * (1.0 + pv[3]) + pv[2]], []

    xm_lat, = _rowwise(stage_a, [(x2, d, 0, 0)], ln_in + [sh_a, sc_a], [(d, MXU_DTYPE)], [], n_rows=n_lat, tm=tmr, name="ln_mod_lat")
    xm_ctx, = _rowwise(stage_a, [(ctx2, d, 0, 0)], ln_in + [csh_a, csc_a], [(d, MXU_DTYPE)], [], n_rows=n_ctx, tm=tmc, name="ln_mod_ctx")
    xm = jnp.concatenate([xm_lat, xm_ctx], axis=0)

    tm_nt = _pick(nt, 1280, 16)
    tn_in = _pick(in_pad, 1280)
    tk_d = _pick(d, 1024)
    h = _matmul(xm, win_t, form="nt", m=nt, n=in_pad, k=d, tm=tm_nt, tn=tn_in, tk=tk_d, out_dtype=F32, name="in_proj")

    act = _conv7_fwd(h, w_conv, n_lat=n_lat, n_ctx=n_ctx, nh=nh)
    gates_hm = jnp.transpose(h[:, g0:g0 + n_gate].reshape(nt, 4, nh), (2, 0, 1))
    bc = lambda a: jnp.broadcast_to(a.reshape(nh, 1, 1), (nh, 8, LANES))
    dk = dict(n_lat=n_lat, n_ctx=n_ctx, nh=nh, hb=hb)
    o_f, st_f = _delta_fwd(act, gates_hm, bc(a_log_f), bc(dt_bias_f), rev=False, **dk)
    o_b, st_b = _delta_fwd(act, gates_hm, bc(a_log_b), bc(dt_bias_b), rev=True, **dk)

    h3 = h.reshape(nt // gw, gw, in_pad)
    ck = dict(gw=gw, col0=4 * dn, half=half)
    yh = _conf_conv_fwd(h3, w_dw, vertical=False, **ck).reshape(n_lat, half)
    yv = _conf_conv_fwd(h3, w_dw, vertical=True, **ck).reshape(n_lat, half)

    def mix_fn(o_heads, z_heads, ya, yb, dng, ba, bb, ga, gbb, la, lb):
        outs = []
        for o, z in zip(o_heads, z_heads):
            ms = jnp.mean(o * o, axis=-1, keepdims=True)
            outs.append(o * lax.rsqrt(ms + 1e-6) * dng * _silu(z))
        ya, yb = ya + ba, yb + bb
        mu = (jnp.sum(ya, axis=-1, keepdims=True) + jnp.sum(yb, axis=-1, keepdims=True)) / conf
        ya, yb = ya - mu, yb - mu
        var = (jnp.sum(ya * ya, axis=-1, keepdims=True) + jnp.sum(yb * yb, axis=-1, keepdims=True)) / conf
        rs = lax.rsqrt(var + LN_EPS)
        ca, cb = ya * rs * ga + la, yb * rs * gbb + lb
        return outs, _silu(ca), _silu(cb)

    def heads(t):
        return [t[:, i * HD:(i + 1) * HD] for i in range(nh)]

    def halves(p):
        return p[:, :half], p[:, half:]

    mix_params = [row(dn_norm_g), row(conf_dw_b), row(conf_ln_g), row(conf_ln_b)]

    def mix_args(rv, pv):
        (ba, bb), (ga, gbb), (la, lb) = halves(pv[1]), halves(pv[2]), halves(pv[3])
        return (heads(rv[0] + rv[1]), heads(rv[2]), rv[3], rv[4], pv[0], ba, bb, ga, gbb, la, lb)

    def stage_e(rv, pv):
        outs, ca, cb = mix_fn(*mix_args(rv, pv))
        return [jnp.concatenate(outs + [ca, cb], axis=-1)], []

    mix_rows = [(o_f, dn, 0, 0), (o_b, dn, 0, 0), (h, dn, 3, 0), (yh, half, 0, 0), (yv, half, 0, 0)]
    mix, = _rowwise(stage_e, mix_rows, mix_params, [(d, MXU_DTYPE)], [], n_rows=n_lat, tm=tmr, name="mixer_fwd")

    gw_out, gw_1, gw_2 = _gather_forward(_gather_wait(gs_send, gs_recv, gs_bufs, mix, "gather_rest_wait"), "gather_rest_forward")
    wout_full = gw_out.reshape(d, d)
    w2_full = gw_2.reshape(ff, d)
    w1_sh = gw_1

    tm_l = _pick(n_lat, 1024, 16)
    tn_d = _pick(d, 1024)
    y1 = _matmul(mix, wout_full, form="nn", m=n_lat, n=d, k=d, tm=tm_l, tn=tn_d, tk=tk_d, out_dtype=F32, name="out_proj")

    def res1(x0, y1v, ga, l1g, l1b, shm, scm):
        x1 = _ln(ALPHA * x0 + ga * y1v, l1g, l1b)
        return x1, x1 * (1.0 + scm) + shm

    f_params = ln_in + [g_a, row(ln1_g), row(ln1_b), sh_m, sc_m]

    def stage_f(rv, pv):
        x1, u = res1(_ln(rv[0], pv[0], pv[1]), rv[1], *pv[2:])
        return [x1, u], []

    x1, u = _rowwise(stage_f, [(x2, d, 0, 0), (y1, d, 0, 0)], f_params, [(d, F32), (d, MXU_DTYPE)], [], n_rows=n_lat, tm=tmr, name="res1_fwd")

    tn_f = _pick(fl, 1024)
    relu_h = _matmul(u, w1_sh, form="nn", m=n_lat, n=ff, k=d, tm=tm_l, tn=tn_f, tk=tk_d, out_dtype=MXU_DTYPE, name="mlp1",
                     bias=row(b_mlp1), epi=lambda r: jnp.maximum(r, 0.0), b_split=fl)
    sq = lambda t: (t.astype(F32) * t.astype(F32)).astype(MXU_DTYPE)
    tk_f = _pick(ff, 1024)
    y2 = _matmul(relu_h, w2_full, form="nn", m=n_lat, n=d, k=ff, tm=tm_l, tn=tn_d, tk=tk_f, out_dtype=F32, name="mlp2",
                 bias=row(b_mlp2), a_fn=sq)

    def loss_fn(x1v, y2v, gm, l2g, l2b, tgt):
        x2v = _ln(ALPHA * x1v + gm * y2v, l2g, l2b)
        return (0.5 / d) * jnp.sum(jnp.square(x2v - tgt))

    def stage_g(rv, pv):
        loss, grads = jax.value_and_grad(loss_fn, argnums=(0, 1, 2, 3, 4))(rv[0], rv[1], pv[0], pv[1], pv[2], rv[2])
        dx1, dy2r, dgm, dl2g, dl2b = grads
        dy2 = dy2r
        return [dx1, dy2], [jnp.reshape(loss, (1, 1)), dgm, dl2g, dl2b, jnp.sum(dy2, axis=0, keepdims=True)]

    dx1p, dy2, loss_acc, d_g_m, d_ln2_g, d_ln2_b, d_b_mlp2 = _rowwise(
        stage_g, [(x1, d, 0, 0), (y2, d, 0, 0), (tgt2, d, 0, 0)], [g_m, row(ln2_g), row(ln2_b)],
        [(d, F32), (d, MXU_DTYPE)], [LANES, d, d, d, d], n_rows=n_lat, tm=tmr, name="loss_res2_bwd")

    dhid, d_b_mlp1 = _matmul(dy2, w2_full, form="nt", m=n_lat, n=ff, k=d, tm=tm_l, tn=tn_f, tk=tk_d, out_dtype=MXU_DTYPE, name="mlp2_bwd_x",
                             epi=lambda r, rh: r * (2.0 * rh.astype(F32)), epi_in=relu_h, colsum=True)
    tk_l = _pick(n_lat, 1024, 16)
    d_w2 = _matmul(relu_h, dy2, form="tn", m=ff, n=d, k=n_lat, tm=tn_f, tn=tn_d, tk=tk_l, out_dtype=F32, name="mlp2_bwd_w", a_fn=sq)
    du = _matmul(dhid, w1_sh, form="nt", m=n_lat, n=d, k=ff, tm=tm_l, tn=tn_d, tk=tn_f, out_dtype=F32, name="mlp1_bwd_x", b_split=fl)
    d_w1 = _matmul(u, dhid, form="tn", m=d, n=ff, k=n_lat, tm=tn_d, tn=tn_f, tk=tk_l, out_dtype=F32, name="mlp1_bwd_w", o_split=fl)

    as_idx = lambda v: jnp.reshape(v, (1,)).astype(jnp.int32)
    slot_idx = [chip_idx, as_idx(2 * (1 - mx) + my), as_idx(2 * mx + (1 - my)), as_idx(2 * (1 - mx) + (1 - my)), c_idx]
    mlp_grads = [d_w1, d_w2.reshape(4, fl, d)]
    mlp_recv = _pair_send_halves(mlp_grads, "mlp_grads_pair_send")
    mlp_sums = [_add_my_half(g, r, c_idx, f"mlp_grads_pair_add_{i}") for i, (g, r) in enumerate(zip(mlp_grads, mlp_recv))]
    ms_send, ms_recv, mlp_sums, mlp_lands, ms_token = _scatter_start(mlp_sums, loss_acc, "mlp_grads_scatter_start")

    def stage_h(rv, pv):
        x0 = _ln(rv[0], pv[0], pv[1])
        _, vjp = jax.vjp(res1, x0, rv[1], *pv[2:])
        dx0, dy1r, dga, dl1g, dl1b, dshm, dscm = vjp((rv[3], rv[2]))
        return [dx0, dy1r], [dga, dl1g, dl1b, dshm, dscm]

    h_params = ln_in + [g_a + ms_token[0:1, 0:1]] + f_params[3:]
    dx0p, dy1, d_g_a, d_ln1_g, d_ln1_b, d_sh_m, d_sc_m = _rowwise(
        stage_h, [(x2, d, 0, 0), (y1, d, 0, 0), (du, d, 0, 0), (dx1p, d, 0, 0)], h_params,
        [(d, F32), (d, MXU_DTYPE)], [d, d, d, d, d], n_rows=n_lat, tm=tmr, name="res1_bwd")

    dmix = _matmul(dy1, wout_full, form="nt", m=n_lat, n=d, k=d, tm=tm_l, tn=tn_d, tk=tk_d, out_dtype=F32, name="out_proj_bwd_x")
    d_wout = _matmul(mix, dy1, form="tn", m=d, n=d, k=n_lat, tm=tn_d, tn=tn_d, tk=tk_l, out_dtype=F32, name="out_proj_bwd_w")

    def stage_i(rv, pv):
        args = mix_args(rv, pv)
        _, vjp = jax.vjp(mix_fn, *args)
        dm = rv[5]
        d_outs = [dm[:, i * HD:(i + 1) * HD] for i in range(nh)]
        dca, dcb = dm[:, dn:dn + half], dm[:, dn + half:]
        do_h, dz_h, dya, dyb, ddng, dba, dbb, dga, dgbb, dla, dlb = vjp((d_outs, dca, dcb))
        cat = lambda *p: jnp.concatenate(p, axis=-1)
        return ([cat(*do_h), cat(*dz_h), dya, dyb], [ddng, cat(dba, dbb), cat(dga, dgbb), cat(dla, dlb)])

    do, dz, dyh, dyv, d_dn_norm_g, d_conf_dw_b, d_conf_ln_g, d_conf_ln_b = _rowwise(
        stage_i, mix_rows + [(dmix, d, 0, 0)], mix_params, [(dn, F32), (dn, MXU_DTYPE), (half, F32), (half, F32)],
        [HD, conf, conf, conf], n_rows=n_lat, tm=tmr, name="mixer_bwd")

    dval_h, dgate_h, d_wdw_h = _conf_conv_bwd(h3, w_dw, dyh.reshape(gw, gw, half), vertical=False, **ck)
    dval_v, dgate_v, d_wdw_v = _conf_conv_bwd(h3, w_dw, dyv.reshape(gw, gw, half), vertical=True, **ck)
    dact_f, dgt_f, d_alog_f, d_dt_f = _delta_bwd(act, gates_hm, bc(a_log_f), bc(dt_bias_f), st_f, do, rev=False, **dk)
    dact_b, dgt_b, d_alog_b, d_dt_b = _delta_bwd(act, gates_hm, bc(a_log_b), bc(dt_bias_b), st_b, do, rev=True, **dk)
    dh_qkv, d_wconv = _conv7_bwd(h, w_conv, dact_f, dact_b, n_lat=n_lat, n_ctx=n_ctx, nh=nh)

    dgates = jnp.stack([dgt_f[..., 0], dgt_f[..., 1], dgt_b[..., 0], dgt_b[..., 1]], axis=0)
    dgates = jnp.transpose(dgates, (2, 0, 1)).reshape(nt, n_gate)
    dgates = jnp.pad(dgates, ((0, 0), (0, gb - n_gate))).astype(MXU_DTYPE)
    zrows = lambda t: jnp.pad(t, ((0, n_ctx), (0, 0)))
    dh = jnp.concatenate([dh_qkv, zrows(dz), zrows(dval_h.reshape(n_lat, half)), zrows(dval_v.reshape(n_lat, half)),
                          zrows(dgate_h.reshape(n_lat, half)), zrows(dgate_v.reshape(n_lat, half)), dgates], axis=1)

    tk_in = tn_in
    dxm = _matmul(dh, win_t, form="nn", m=nt, n=d, k=in_pad, tm=tm_nt, tn=tn_d, tk=tk_in, out_dtype=F32, name="in_proj_bwd_x")
    tk_nt = _pick(nt, 1280, 16)
    d_win_t = _matmul(dh, xm, form="tn", m=in_pad, n=d, k=nt, tm=tn_in, tn=tn_d, tk=tk_nt, out_dtype=F32, name="in_proj_bwd_w")
    d_win_t = jnp.concatenate([d_win_t[:4 * dn], d_win_t[g0:g0 + n_gate], d_win_t[4 * dn:g0]], axis=0)

    def mod_in(xr, lg, lb, sh, sc):
        x0 = _ln(xr, lg, lb)
        return x0, x0 * (1.0 + sc) + sh

    def stage_j(rv, pv):
        _, vjp = jax.vjp(mod_in, rv[0], *pv)
        dx0 = rv[2] if len(rv) > 2 else jnp.zeros_like(rv[1])
        dxr, dlg, dlb, dsh, dsc = vjp((dx0, rv[1]))
        return [dxr], [dlg, dlb, dsh, dsc]

    grad_x, dlg_l, dlb_l, d_sh_a, d_sc_a = _rowwise(
        stage_j, [(x2, d, 0, 0), (dxm, d, 0, 0), (dx0p, d, 0, 0)], ln_in + [sh_a, sc_a], [(d, F32)], [d, d, d, d],
        n_rows=n_lat, tm=tmr, name="ln_mod_bwd_lat")
    _, dlg_c, dlb_c, d_csh_a, d_csc_a = _rowwise(
        stage_j, [(ctx2, d, 0, 0), (dxm, d, 0, n_lat // tmc)], ln_in + [csh_a, csc_a], [(d, F32)], [d, d, d, d],
        n_rows=n_ctx, tm=tmc, name="ln_mod_bwd_ctx")

    zd = jnp.zeros((1, d), F32)
    dmod_rows = jnp.concatenate([jnp.concatenate([d_sh_a, d_sc_a, d_g_a, d_sh_m, d_sc_m, d_g_m], axis=1),
                                 jnp.concatenate([d_csh_a, d_csc_a, zd, zd, zd, zd], axis=1), jnp.zeros((6, 6 * d), F32)], axis=0)
    d16, g_b_mod = _dmod_prep(_bcast8(dmod_rows, "bcast_dmod"), "dmod_prep")
    d16_loc = lax.dynamic_slice(d16, (0, chip * ml), (16, ml))
    g_w_mod = _matmul(s16, d16_loc, form="tn", m=d, n=ml, k=16, tm=_pick(d, 512), tn=_pick(ml, 1024), tk=16, out_dtype=F32,
                      name="mod_bwd_w", a_fn=_silu, kind="hi")
    dsilu = _matmul(d16_loc, w_mod[0], form="nt", m=16, n=d, k=ml, tm=16, tn=_pick(d, 1024), tk=_pick(ml, 1024), out_dtype=F32,
                    name="mod_bwd_c", a_fn=to_mxu, b_fn=to_mxu)
    d_cctx_part = dsilu[8:9, :] * (1 - mc).astype(F32)

    head_sum = lambda t: t[:, 0, 0]
    small = [d_cctx_part, dlg_l + dlg_c, dlb_l + dlb_c, d_wconv, head_sum(d_alog_f), head_sum(d_dt_f), head_sum(d_alog_b),
             head_sum(d_dt_b), d_dn_norm_g, jnp.concatenate([d_wdw_h, d_wdw_v], axis=1), d_conf_dw_b, d_conf_ln_g, d_conf_ln_b,
             d_ln1_g, d_ln1_b, d_b_mlp1, d_b_mlp2, d_ln2_g, d_ln2_b]
    small_shapes = [(d,), (d,), (d,), (SHORT_CONV, 3 * dn), (nh,), (nh,), (nh,), (nh,), (HD,), (CONF_K, conf), (conf,), (conf,), (conf,),
                    (d,), (d,), (ff,), (d,), (d,), (d,)]
    ssum = _sum8(_bcast8(_pack(small), "bcast_small_grads"), "sum_small_grads")
    (t_cctx, g_ln_in_g, g_ln_in_b, g_wconv_full, g_a_log_f, g_dt_bias_f, g_a_log_b, g_dt_bias_b, g_dn_norm_g, g_wdw_full, g_conf_dw_b,
     g_conf_ln_g, g_conf_ln_b, g_ln1_g, g_ln1_b, g_b_mlp1, g_b_mlp2, g_ln2_g, g_ln2_b) = _unpack(ssum, small_shapes)
    sg = jax.nn.sigmoid(c_ctx)
    g_c_ctx = t_cctx * (sg * (1.0 + c_ctx * (1.0 - sg)))
    g_w_qkv_conv = lax.dynamic_slice(g_wconv_full, (0, chip * w_qkv_conv.shape[2]), w_qkv_conv.shape[1:])
    g_conf_dw_w = lax.dynamic_slice(g_wdw_full, (0, chip * conf_dw_w.shape[2]), conf_dw_w.shape[1:])

    proj_grads = [d_win_t.reshape(4, r_in, d), d_wout.reshape(4, d // 4, d)]
    mlp_sums, mlp_lands = _scatter_wait(ms_send, ms_recv, mlp_sums, mlp_lands, proj_grads[0], "mlp_grads_scatter_wait")
    proj_recv = _pair_send_halves(proj_grads, "proj_grads_pair_send")
    proj_sums = [_add_my_half(g, r, c_idx, f"proj_grads_pair_add_{i}") for i, (g, r) in enumerate(zip(proj_grads, proj_recv))]
    proj_lands = _chip_scatter(proj_sums, "proj_grads_chip_scatter")
    mine = [_sum_slots(own, rcv, slot_idx, f"grads_chip_sum_{i}")
            for i, (own, rcv) in enumerate(zip(list(proj_sums) + list(mlp_sums), list(proj_lands) + list(mlp_lands)))]
    g_win_t, g_w_out, g_w_mlp1, g_w_mlp2 = _pair_join_halves(mine, "grads_pair_join")
    g_w_in = jnp.transpose(g_win_t)

    grads = dict(c_ctx=g_c_ctx, ln_in_g=g_ln_in_g, ln_in_b=g_ln_in_b, w_mod=g_w_mod[None], b_mod=g_b_mod, w_in=g_w_in[None],
                 w_qkv_conv=g_w_qkv_conv[None], a_log_f=g_a_log_f[None], dt_bias_f=g_dt_bias_f[None], a_log_b=g_a_log_b[None],
                 dt_bias_b=g_dt_bias_b[None], dn_norm_g=g_dn_norm_g[None], conf_dw_w=g_conf_dw_w[None], conf_dw_b=g_conf_dw_b[None],
                 conf_ln_g=g_conf_ln_g[None], conf_ln_b=g_conf_ln_b[None], w_out=g_w_out[None], ln1_g=g_ln1_g[None], ln1_b=g_ln1_b[None],
                 w_mlp1=g_w_mlp1[None], b_mlp1=g_b_mlp1[None], w_mlp2=g_w_mlp2[None], b_mlp2=g_b_mlp2[None], ln2_g=g_ln2_g[None],
                 ln2_b=g_ln2_b[None])
    weights = dict(c_ctx=c_ctx, ln_in_g=ln_in_g, ln_in_b=ln_in_b, w_mod=w_mod, b_mod=b_mod, w_in=w_in, w_qkv_conv=w_qkv_conv,
                   a_log_f=a_log_f, dt_bias_f=dt_bias_f, a_log_b=a_log_b, dt_bias_b=dt_bias_b, dn_norm_g=dn_norm_g, conf_dw_w=conf_dw_w,
                   conf_dw_b=conf_dw_b, conf_ln_g=conf_ln_g, conf_ln_b=conf_ln_b, w_out=w_out, ln1_g=ln1_g, ln1_b=ln1_b, w_mlp1=w_mlp1,
                   b_mlp1=b_mlp1, w_mlp2=w_mlp2, b_mlp2=b_mlp2, ln2_g=ln2_g, ln2_b=ln2_b)
    m_in = dict(c_ctx=m_c_ctx, ln_in_g=m_ln_in_g, ln_in_b=m_ln_in_b, w_mod=m_w_mod, b_mod=m_b_mod, w_in=m_w_in, w_qkv_conv=m_w_qkv_conv,
                a_log_f=m_a_log_f, dt_bias_f=m_dt_bias_f, a_log_b=m_a_log_b, dt_bias_b=m_dt_bias_b, dn_norm_g=m_dn_norm_g,
                conf_dw_w=m_conf_dw_w, conf_dw_b=m_conf_dw_b, conf_ln_g=m_conf_ln_g, conf_ln_b=m_conf_ln_b, w_out=m_w_out, ln1_g=m_ln1_g,
                ln1_b=m_ln1_b, w_mlp1=m_w_mlp1, b_mlp1=m_b_mlp1, w_mlp2=m_w_mlp2, b_mlp2=m_b_mlp2, ln2_g=m_ln2_g, ln2_b=m_ln2_b)
    v_in = dict(c_ctx=v_c_ctx, ln_in_g=v_ln_in_g, ln_in_b=v_ln_in_b, w_mod=v_w_mod, b_mod=v_b_mod, w_in=v_w_in, w_qkv_conv=v_w_qkv_conv,
                a_log_f=v_a_log_f, dt_bias_f=v_dt_bias_f, a_log_b=v_a_log_b, dt_bias_b=v_dt_bias_b, dn_norm_g=v_dn_norm_g,
                conf_dw_w=v_conf_dw_w, conf_dw_b=v_conf_dw_b, conf_ln_g=v_conf_ln_g, conf_ln_b=v_conf_ln_b, w_out=v_w_out, ln1_g=v_ln1_g,
                ln1_b=v_ln1_b, w_mlp1=v_w_mlp1, b_mlp1=v_b_mlp1, w_mlp2=v_w_mlp2, b_mlp2=v_b_mlp2, ln2_g=v_ln2_g, ln2_b=v_ln2_b)
    names = list(weights)
    big_names = ("w_mod", "w_in", "w_out", "w_mlp1", "w_mlp2")
    delta, new_m, new_v = {}, {}, {}
    for nm in big_names:
        w2d = weights[nm][0]
        dl, mm, vv = _adamw(w2d, grads[nm][0], m_in[nm][0], v_in[nm][0], f"adamw_{nm}")
        delta[nm], new_m[nm], new_v[nm] = dl[None], mm[None], vv[None]
    small_names = [nm for nm in names if nm not in big_names]
    shapes = [weights[nm].shape for nm in small_names]
    grads = {nm: (grads[nm].reshape(weights[nm].shape) if nm in small_names else grads[nm]) for nm in names}
    packed = [_pack([src[nm] for nm in small_names]) for src in (weights, grads, m_in, v_in)]
    dl, mm, vv = _adamw(*packed, "adamw_small")
    for nm, a, b_, c_ in zip(small_names, _unpack(dl, shapes), _unpack(mm, shapes), _unpack(vv, shapes)):
        delta[nm], new_m[nm], new_v[nm] = a, b_, c_

    loss = lax.psum(loss_acc[0, 0], MESH_AXES)
    return (loss, grad_x[None], *[grads[nm] for nm in names], *[delta[nm] for nm in names],
            *[new_m[nm] for nm in names], *[new_v[nm] for nm in names])
```

```python
import functools
import math

import jax
import jax.numpy as jnp
from jax import lax
from jax.experimental import pallas as pl
from jax.experimental.pallas import tpu as pltpu

F32 = jnp.float32
BF16 = jnp.bfloat16
MXU_DTYPE = BF16
WIRE_DTYPE = BF16
HIGHEST = lax.Precision.HIGHEST

HD = 128
CHUNK = 128
SHORT_CONV = 7
CONF_K = 31
ALPHA = 2.0 ** 0.25
LN_EPS = 1e-5
V7X_VMEM_BYTES = 64 * 1024 * 1024
LANES = 128

ADAM_LR = 0.001
ADAM_B1 = 0.9
ADAM_B2 = 0.999
ADAM_EPS = 1e-08
ADAM_WD = 0.01
ADAM_STEP = 10

MESH_AXES = ("x", "y", "c")
ANY = pl.BlockSpec(memory_space=pl.ANY)


def _pick(dim, pref, mult=LANES):
    best = None
    t = mult
    while t <= min(dim, pref):
        if dim % t == 0:
            best = t
        t += mult
    return best if best is not None else dim


def _cparams(sem, vmem_est):
    limit = int(min(V7X_VMEM_BYTES - 6 * 1024 * 1024, max(32 * 1024 * 1024, vmem_est + 8 * 1024 * 1024)))
    return pltpu.CompilerParams(dimension_semantics=sem, vmem_limit_bytes=limit)


def _nbytes(shape, dtype):
    return math.prod(shape) * jnp.dtype(dtype).itemsize


def _ln(x, g, b):
    mu = jnp.mean(x, axis=-1, keepdims=True)
    xc = x - mu
    var = jnp.mean(xc * xc, axis=-1, keepdims=True)
    return xc * lax.rsqrt(var + LN_EPS) * g + b


def _silu(x):
    return x * jax.nn.sigmoid(x)


def _softplus(x):
    return jnp.maximum(x, 0.0) + jnp.log1p(jnp.exp(-jnp.abs(x)))


_DIMS = {"nn": (((1,), (0,)), ((), ())), "nt": (((1,), (1,)), ((), ())), "tn": (((0,), (0,)), ((), ()))}


def _split_bf16(x, parts):
    out, rest = [], x.astype(F32)
    for _ in range(parts):
        bits = lax.bitcast_convert_type(rest, jnp.uint32) & jnp.uint32(0xFFFF0000)
        p = lax.bitcast_convert_type(bits, F32)
        out.append(p.astype(BF16))
        rest = rest - p
    return out


def _raw_dot(a, b, form, kind):
    if MXU_DTYPE != F32:
        if kind == "mxu":
            return lax.dot_general(a.astype(MXU_DTYPE), b.astype(MXU_DTYPE), _DIMS[form], preferred_element_type=F32)
        if kind == "hi3":
            (a0, a1), (b0, b1) = _split_bf16(a, 2), _split_bf16(b, 2)
            d = lambda p, q: lax.dot_general(p, q, _DIMS[form], preferred_element_type=F32)
            return d(a0, b0) + (d(a1, b0) + d(a0, b1))
    return lax.dot_general(a.astype(F32), b.astype(F32), _DIMS[form], precision=HIGHEST, preferred_element_type=F32)


@functools.partial(jax.custom_vjp, nondiff_argnums=(2, 3))
def _dot(a, b, form, kind):
    return _raw_dot(a, b, form, kind)


def _dot_fwd(a, b, form, kind):
    return _raw_dot(a, b, form, kind), (a, b)


def _dot_bwd(form, kind, res, dc):
    a, b = res
    if form == "nn":
        return _dot(dc, b, "nt", kind), _dot(a, dc, "tn", kind)
    if form == "nt":
        return _dot(dc, b, "nn", kind), _dot(dc, a, "tn", kind)
    return _dot(b, dc, "nt", kind), _dot(a, dc, "nn", kind)


_dot.defvjp(_dot_fwd, _dot_bwd)


def _raw_xdot(a, b, form, exact):
    if MXU_DTYPE == F32:
        return _raw_dot(a, b, form, "hi")
    d = lambda p, q: lax.dot_general(p, q, _DIMS[form], preferred_element_type=F32)
    if exact == "a":
        a16, (b0, b1, b2) = a.astype(BF16), _split_bf16(b, 3)
        return d(a16, b0) + (d(a16, b1) + d(a16, b2))
    b16, (a0, a1, a2) = b.astype(BF16), _split_bf16(a, 3)
    return d(a0, b16) + (d(a1, b16) + d(a2, b16))


@functools.partial(jax.custom_vjp, nondiff_argnums=(2, 3))
def _xdot(a, b, form, exact):
    return _raw_xdot(a, b, form, exact)


def _xdot_fwd(a, b, form, exact):
    return _raw_xdot(a, b, form, exact), (a, b)


def _xdot_bwd(form, exact, res, dc):
    a, b = res
    if form == "nn" and exact == "a":
        return jnp.zeros_like(a), _xdot(a, dc, "tn", "a")
    assert form == "tn" and exact == "b"
    return _xdot(b, dc, "nt", "a"), jnp.zeros_like(b)


_xdot.defvjp(_xdot_fwd, _xdot_bwd)


def _spec2(block, idx, split=None):
    if split is None:
        return pl.BlockSpec(tuple(block), idx)
    per = split // block[1]

    def idx3(*g):
        r, cblk = idx(*g)
        return (cblk // per, r, cblk % per)

    return pl.BlockSpec((None,) + tuple(block), idx3)


def _matmul(a, b, *, form, m, n, k, tm, tn, tk, out_dtype, name, bias=None, a_fn=None, b_fn=None,
            epi=None, epi_in=None, colsum=False, kind="mxu", b_split=None, o_split=None):
    assert m % tm == 0 and n % tn == 0 and k % tk == 0, (name, m, n, k, tm, tn, tk)
    nk = k // tk
    grid = (n // tn, m // tm, nk)
    if form == "tn":
        a_spec = pl.BlockSpec((tk, tm), lambda j, i, kk: (kk, i))
    else:
        a_spec = pl.BlockSpec((tm, tk), lambda j, i, kk: (i, kk))
    if form == "nt":
        b_spec = _spec2((tn, tk), lambda j, i, kk: (j, kk), b_split)
    else:
        b_spec = _spec2((tk, tn), lambda j, i, kk: (kk, j), b_split)
    in_specs = [a_spec, b_spec]
    operands = [a, b]
    if bias is not None:
        in_specs.append(pl.BlockSpec((1, tn), lambda j, i, kk: (0, j)))
        operands.append(bias)
    if epi_in is not None:
        in_specs.append(pl.BlockSpec((tm, tn), lambda j, i, kk: (i, j)))
        operands.append(epi_in)
    out_specs = [_spec2((tm, tn), lambda j, i, kk: (i, j), o_split)]
    if o_split is None:
        out_shape = [jax.ShapeDtypeStruct((m, n), out_dtype)]
    else:
        out_shape = [jax.ShapeDtypeStruct((n // o_split, m, o_split), out_dtype)]
    if colsum:
        out_specs.append(pl.BlockSpec((1, tn), lambda j, i, kk: (0, j)))
        out_shape.append(jax.ShapeDtypeStruct((1, n), F32))
    has_bias, has_epi_in = bias is not None, epi_in is not None

    def body(*refs):
        refs = list(refs)
        a_ref, b_ref = refs[0], refs[1]
        pos = 2
        bias_ref = epi_ref = cs_ref = None
        if has_bias:
            bias_ref = refs[pos]
            pos += 1
        if has_epi_in:
            epi_ref = refs[pos]
            pos += 1
        o_ref = refs[pos]
        pos += 1
        if colsum:
            cs_ref = refs[pos]
            pos += 1
        acc_ref = refs[pos]
        i, kk = pl.program_id(1), pl.program_id(2)

        @pl.when(kk == 0)
        def _():
            acc_ref[...] = jnp.zeros_like(acc_ref)

        av = a_ref[...]
        if a_fn is not None:
            av = a_fn(av)
        bv = b_ref[...]
        if b_fn is not None:
            bv = b_fn(bv)
        acc_ref[...] += _raw_dot(av, bv, form, kind)

        @pl.when(kk == nk - 1)
        def _():
            r = acc_ref[...]
            if has_bias:
                r = r + bias_ref[...]
            if epi is not None:
                r = epi(r, epi_ref[...]) if has_epi_in else epi(r)
            o_ref[...] = r.astype(out_dtype)
            if colsum:
                s = jnp.sum(r, axis=0, keepdims=True)

                @pl.when(i == 0)
                def _():
                    cs_ref[...] = s

                @pl.when(i > 0)
                def _():
                    cs_ref[...] += s

    est = 2 * (_nbytes((tm, tk), a.dtype) + _nbytes((tk, tn), b.dtype) + _nbytes((tm, tn), out_dtype))
    est += _nbytes((tm, tn), F32) * 2 + (2 * _nbytes((tm, tn), epi_in.dtype) if has_epi_in else 0)
    res = pl.pallas_call(
        body, grid=grid, in_specs=in_specs, out_specs=out_specs, out_shape=out_shape,
        scratch_shapes=[pltpu.VMEM((tm, tn), F32)], name=name,
        compiler_params=_cparams(("arbitrary", "arbitrary", "arbitrary"), est),
    )(*operands)
    return res if colsum else res[0]


def _rowwise(fn, rows, params, row_outs, acc_outs, *, n_rows, tm, name):
    assert n_rows % tm == 0, (name, n_rows, tm)
    nr, npar, nro, nac = len(rows), len(params), len(row_outs), len(acc_outs)
    in_specs = [pl.BlockSpec((tm, w), functools.partial(lambda i, cb, ro: (i + ro, cb), cb=cb, ro=ro))
                for (_, w, cb, ro) in rows]
    in_specs += [pl.BlockSpec((1, p.shape[1]), lambda i: (0, 0)) for p in params]
    out_specs = [pl.BlockSpec((tm, w), lambda i: (i, 0)) for (w, _) in row_outs]
    out_specs += [pl.BlockSpec((1, w), lambda i: (0, 0)) for w in acc_outs]
    out_shape = [jax.ShapeDtypeStruct((n_rows, w), dt) for (w, dt) in row_outs]
    out_shape += [jax.ShapeDtypeStruct((1, w), F32) for w in acc_outs]

    def body(*refs):
        rv = [r[...] for r in refs[:nr]]
        pv = [r[...] for r in refs[nr:nr + npar]]
        ro_refs = refs[nr + npar:nr + npar + nro]
        ac_refs = refs[nr + npar + nro:]
        ro, ac = fn(rv, pv)
        for ref, val in zip(ro_refs, ro, strict=True):
            ref[...] = val.astype(ref.dtype)
        first = pl.program_id(0) == 0
        for ref, val in zip(ac_refs, ac, strict=True):
            val = jnp.broadcast_to(val.astype(F32), ref.shape)

            @pl.when(first)
            def _(ref=ref, val=val):
                ref[...] = val

            @pl.when(jnp.logical_not(first))
            def _(ref=ref, val=val):
                ref[...] += val

    io = sum(_nbytes((tm, w), a.dtype) for (a, w, _, _) in rows) + sum(_nbytes((tm, w), dt) for (w, dt) in row_outs)
    widest = max([w for (_, w, _, _) in rows] + [w for (w, _) in row_outs])
    est = 2 * io + 12 * _nbytes((tm, widest), F32)
    return pl.pallas_call(
        body, grid=(n_rows // tm,), in_specs=in_specs, out_specs=out_specs, out_shape=out_shape, name=name,
        compiler_params=_cparams(("arbitrary",), est),
    )(*[a for (a, _, _, _) in rows], *params)


def _conv7_tiles(n_lat, n_ctx):
    tt = min(256, n_ctx)
    assert n_lat % tt == 0 and n_ctx % tt == 0
    return tt, [(t0, 8 + t0) for t0 in range(0, n_lat, tt)] + [(n_lat + t0, 16 + n_lat + t0) for t0 in range(0, n_ctx, tt)]


def _conv7_fill(p_ref, x_ref, n_lat, n_ctx):
    z8 = jnp.zeros((8, LANES), F32)
    p_ref[0:8, :] = z8
    p_ref[8:8 + n_lat, :] = x_ref[0:n_lat, :]
    p_ref[8 + n_lat:16 + n_lat, :] = z8
    p_ref[16 + n_lat:16 + n_lat + n_ctx, :] = x_ref[n_lat:n_lat + n_ctx, :]
    p_ref[16 + n_lat + n_ctx:24 + n_lat + n_ctx, :] = z8


def _conv7_fwd(h, w_conv, *, n_lat, n_ctx, nh):
    nt = n_lat + n_ctx
    tt, tiles = _conv7_tiles(n_lat, n_ctx)
    half = SHORT_CONV // 2

    def body(x_ref, w_ref, o_ref, p_ref):
        _conv7_fill(p_ref, x_ref, n_lat, n_ctx)
        for (row, prow) in tiles:
            acc = jnp.zeros((tt, LANES), F32)
            for kk in range(SHORT_CONV):
                acc = acc + w_ref[kk:kk + 1, :] * p_ref[pl.ds(prow + kk - half, tt), :]
            o_ref[pl.ds(row, tt), :] = _silu(acc)

    return pl.pallas_call(
        body, grid=(3 * nh,),
        in_specs=[pl.BlockSpec((nt, LANES), lambda j: (0, j)), pl.BlockSpec((SHORT_CONV, LANES), lambda j: (0, j))],
        out_specs=pl.BlockSpec((None, None, nt, LANES), lambda j: (j // nh, j % nh, 0, 0)),
        out_shape=jax.ShapeDtypeStruct((3, nh, nt, LANES), F32),
        scratch_shapes=[pltpu.VMEM((nt + 24, LANES), F32)], name="conv7_fwd",
        compiler_params=_cparams(("arbitrary",), 5 * _nbytes((nt + 24, LANES), F32)),
    )(h, w_conv)


def _conv7_bwd(h, w_conv, dact_f, dact_b, *, n_lat, n_ctx, nh):
    nt = n_lat + n_ctx
    tt, tiles = _conv7_tiles(n_lat, n_ctx)
    half = SHORT_CONV // 2

    def body(x_ref, w_ref, df_ref, db_ref, dx_ref, dw_ref, p_ref, q_ref):
        _conv7_fill(p_ref, x_ref, n_lat, n_ctx)
        z8 = jnp.zeros((8, LANES), F32)
        q_ref[0:8, :] = z8
        q_ref[8 + n_lat:16 + n_lat, :] = z8
        q_ref[16 + n_lat + n_ctx:24 + n_lat + n_ctx, :] = z8
        dw = [jnp.zeros((1, LANES), F32) for _ in range(SHORT_CONV)]
        for (row, prow) in tiles:
            pre = jnp.zeros((tt, LANES), F32)
            for kk in range(SHORT_CONV):
                pre = pre + w_ref[kk:kk + 1, :] * p_ref[pl.ds(prow + kk - half, tt), :]
            s = jax.nn.sigmoid(pre)
            dpre = (df_ref[pl.ds(row, tt), :] + db_ref[pl.ds(row, tt), :]) * (s * (1.0 + pre * (1.0 - s)))
            q_ref[pl.ds(prow, tt), :] = dpre
            for kk in range(SHORT_CONV):
                dw[kk] = dw[kk] + jnp.sum(dpre * p_ref[pl.ds(prow + kk - half, tt), :], axis=0, keepdims=True)
        for (row, prow) in tiles:
            acc = jnp.zeros((tt, LANES), F32)
            for kk in range(SHORT_CONV):
                acc = acc + w_ref[kk:kk + 1, :] * q_ref[pl.ds(prow + half - kk, tt), :]
            dx_ref[pl.ds(row, tt), :] = acc.astype(dx_ref.dtype)
        for kk in range(SHORT_CONV):
            dw_ref[kk:kk + 1, :] = dw[kk]

    dspec = pl.BlockSpec((None, None, nt, LANES), lambda j: (j // nh, j % nh, 0, 0))
    return pl.pallas_call(
        body, grid=(3 * nh,),
        in_specs=[pl.BlockSpec((nt, LANES), lambda j: (0, j)), pl.BlockSpec((SHORT_CONV, LANES), lambda j: (0, j)), dspec, dspec],
        out_specs=[pl.BlockSpec((nt, LANES), lambda j: (0, j)), pl.BlockSpec((SHORT_CONV, LANES), lambda j: (0, j))],
        out_shape=[jax.ShapeDtypeStruct((nt, 3 * nh * LANES), MXU_DTYPE), jax.ShapeDtypeStruct((SHORT_CONV, 3 * nh * LANES), F32)],
        scratch_shapes=[pltpu.VMEM((nt + 24, LANES), F32), pltpu.VMEM((nt + 24, LANES), F32)], name="conv7_bwd",
        compiler_params=_cparams(("arbitrary",), 10 * _nbytes((nt + 24, LANES), F32)),
    )(h, w_conv, dact_f, dact_b)


_RT = 4
_CP = CONF_K // 2


def _conf_pad_shape(gw, vertical):
    return (gw + 2 * _CP, gw, LANES) if vertical else (gw, gw + 32, LANES)


def _conf_fill(p_ref, val, gw, vertical):
    if vertical:
        p_ref[0:_CP] = jnp.zeros((_CP, gw, LANES), F32)
        p_ref[_CP + gw:2 * _CP + gw] = jnp.zeros((_CP, gw, LANES), F32)
        p_ref[_CP:_CP + gw] = val
    else:
        p_ref[:, 0:16, :] = jnp.zeros((gw, 16, LANES), F32)
        p_ref[:, 16 + gw:32 + gw, :] = jnp.zeros((gw, 16, LANES), F32)
        p_ref[:, 16:16 + gw, :] = val


def _conf_window(p_ref, r0, shift, gw, vertical):
    if vertical:
        return p_ref[pl.ds(r0 + _CP + shift, _RT), :, :]
    return p_ref[pl.ds(r0, _RT), pl.ds(16 + shift, gw), :]


def _conf_conv_fwd(h3, w_dw, *, gw, col0, half, vertical):
    cb_val = (col0 + (half if vertical else 0)) // LANES
    cb_gate = cb_val + 2 * half // LANES
    cb_w = (half if vertical else 0) // LANES

    def body(v_ref, g_ref, w_ref, o_ref, p_ref):
        _conf_fill(p_ref, v_ref[...] * jax.nn.sigmoid(g_ref[...]), gw, vertical)

        def step(t, carry):
            r0 = t * _RT
            acc = jnp.zeros((_RT, gw, LANES), F32)
            for kk in range(CONF_K):
                acc = acc + w_ref[kk:kk + 1, :] * _conf_window(p_ref, r0, kk - _CP, gw, vertical)
            o_ref[pl.ds(r0, _RT)] = acc
            return carry

        lax.fori_loop(0, gw // _RT, step, 0)

    blk = (gw, gw, LANES)
    return pl.pallas_call(
        body, grid=(half // LANES,),
        in_specs=[pl.BlockSpec(blk, lambda j: (0, 0, cb_val + j)), pl.BlockSpec(blk, lambda j: (0, 0, cb_gate + j)),
                  pl.BlockSpec((CONF_K, LANES), lambda j: (0, cb_w + j))],
        out_specs=pl.BlockSpec(blk, lambda j: (0, 0, j)),
        out_shape=jax.ShapeDtypeStruct((gw, gw, half), F32),
        scratch_shapes=[pltpu.VMEM(_conf_pad_shape(gw, vertical), F32)],
        name="conf_conv_fwd_v" if vertical else "conf_conv_fwd_h",
        compiler_params=_cparams(("arbitrary",), 8 * _nbytes(_conf_pad_shape(gw, vertical), F32)),
    )(h3, h3, w_dw)


def _conf_conv_bwd(h3, w_dw, dyc, *, gw, col0, half, vertical):
    cb_val = (col0 + (half if vertical else 0)) // LANES
    cb_gate = cb_val + 2 * half // LANES
    cb_w = (half if vertical else 0) // LANES

    def body(v_ref, g_ref, w_ref, d_ref, dv_ref, dg_ref, dw_ref, py_ref, pd_ref):
        _conf_fill(py_ref, v_ref[...] * jax.nn.sigmoid(g_ref[...]), gw, vertical)
        _conf_fill(pd_ref, d_ref[...], gw, vertical)

        def step(t, carry):
            r0 = t * _RT
            acc = jnp.zeros((_RT, gw, LANES), F32)
            for kk in range(CONF_K):
                acc = acc + w_ref[kk:kk + 1, :] * _conf_window(pd_ref, r0, _CP - kk, gw, vertical)
            val = v_ref[pl.ds(r0, _RT)]
            sg = jax.nn.sigmoid(g_ref[pl.ds(r0, _RT)])
            dv_ref[pl.ds(r0, _RT)] = (acc * sg).astype(dv_ref.dtype)
            dg_ref[pl.ds(r0, _RT)] = (acc * val * sg * (1.0 - sg)).astype(dg_ref.dtype)
            return carry

        lax.fori_loop(0, gw // _RT, step, 0)

        for kk in range(CONF_K):
            def wstep(t, acc, kk=kk):
                r0 = t * _RT
                prod = d_ref[pl.ds(r0, _RT)] * _conf_window(py_ref, r0, kk - _CP, gw, vertical)
                return acc + jnp.sum(prod, axis=0)

            tot = lax.fori_loop(0, gw // _RT, wstep, jnp.zeros((gw, LANES), F32))
            dw_ref[kk:kk + 1, :] = jnp.sum(tot, axis=0, keepdims=True)

    blk = (gw, gw, LANES)
    pshape = _conf_pad_shape(gw, vertical)
    return pl.pallas_call(
        body, grid=(half // LANES,),
        in_specs=[pl.BlockSpec(blk, lambda j: (0, 0, cb_val + j)), pl.BlockSpec(blk, lambda j: (0, 0, cb_gate + j)),
                  pl.BlockSpec((CONF_K, LANES), lambda j: (0, cb_w + j)), pl.BlockSpec(blk, lambda j: (0, 0, j))],
        out_specs=[pl.BlockSpec(blk, lambda j: (0, 0, j)), pl.BlockSpec(blk, lambda j: (0, 0, j)),
                   pl.BlockSpec((CONF_K, LANES), lambda j: (0, j))],
        out_shape=[jax.ShapeDtypeStruct((gw, gw, half), MXU_DTYPE), jax.ShapeDtypeStruct((gw, gw, half), MXU_DTYPE),
                   jax.ShapeDtypeStruct((CONF_K, half), F32)],
        scratch_shapes=[pltpu.VMEM(pshape, F32), pltpu.VMEM(pshape, F32)],
        name="conf_conv_bwd_v" if vertical else "conf_conv_bwd_h",
        compiler_params=_cparams(("arbitrary",), 12 * _nbytes(pshape, F32)),
    )(h3, h3, w_dw, dyc)


_INV_BASE = 8


def _tri_inv_raw(mats):
    c = mats[0].shape[0]
    ri = lax.broadcasted_iota(jnp.int32, (c, c), 0)
    ci = lax.broadcasted_iota(jnp.int32, (c, c), 1)
    eye = (ri == ci).astype(F32)
    base = min(_INV_BASE, c)
    same = lambda size: (ri // size) == (ci // size)
    dmat = [jnp.where(same(base), a, 0.0) for a in mats]
    x = [eye - dm for dm in dmat]
    pw = [_raw_dot(dm, dm, "nn", "hi3") for dm in dmat]
    span = 2
    while span < base:
        x = [xi + _raw_dot(xi, p, "nn", "hi3") for xi, p in zip(x, pw)]
        span *= 2
        if span < base:
            pw = [_raw_dot(p, p, "nn", "hi3") for p in pw]
    size = base
    while size < c:
        off = jnp.logical_and(jnp.logical_not(same(size)), same(2 * size))
        t = [_raw_dot(xi, jnp.where(off, a, 0.0), "nn", "hi3") for xi, a in zip(x, mats)]
        x = [xi - _raw_dot(ti, xi, "nn", "hi3") for xi, ti in zip(x, t)]
        size *= 2
    return x


@jax.custom_vjp
def _tri_inv(mats):
    return _tri_inv_raw(mats)


def _tri_inv_fwd(mats):
    x = _tri_inv_raw(mats)
    return x, x


def _tri_inv_bwd(x, dx):
    t = [_raw_dot(xi, di, "tn", "hi3") for xi, di in zip(x, dx)]
    return ([-_raw_dot(ti, xi, "nt", "hi3") for ti, xi in zip(t, x)],)


_tri_inv.defvjp(_tri_inv_fwd, _tri_inv_bwd)


@jax.custom_vjp
def _tri_inv_saved(mats, saved):
    return saved


def _tri_inv_saved_fwd(mats, saved):
    return saved, saved


def _tri_inv_saved_bwd(x, dx):
    return _tri_inv_bwd(x, dx)[0], [jnp.zeros_like(xi) for xi in x]


_tri_inv_saved.defvjp(_tri_inv_saved_fwd, _tri_inv_saved_bwd)


def _delta_chunk(qa, ka, va, graw, braw, alog, dtb, s, *, rev, minv_saved=None, with_minv=False):
    c = qa[0].shape[0]

    def each(f, *lists):
        return [f(*xs) for xs in zip(*lists, strict=True)]

    def l2n(t):
        return t * lax.rsqrt(jnp.sum(t * t, axis=-1, keepdims=True) + 1e-6)

    ri = lax.broadcasted_iota(jnp.int32, (c, c), 0)
    ci = lax.broadcasted_iota(jnp.int32, (c, c), 1)
    incl = (ci >= ri) if rev else (ci <= ri)
    strict = (ci > ri) if rev else (ci < ri)
    tmat = incl.astype(F32)
    tmat_t = ((ri >= ci) if rev else (ri <= ci)).astype(F32)

    q = each(lambda t: l2n(t) * (HD ** -0.5), qa)
    k = each(l2n, ka)
    g = each(lambda al, gr, dt: -jnp.exp(al) * _softplus(gr + dt), alog, graw, dtb)
    beta = each(jax.nn.sigmoid, braw)
    gc_wide = each(lambda t: _xdot(tmat, jnp.broadcast_to(t, (c, HD)), "nn", "a"), g)
    gc_rows = gc_wide if c == HD else each(lambda t: _xdot(tmat, jnp.broadcast_to(t, (c, c)), "nn", "a"), g)
    gc_cols = each(lambda t: _xdot(jnp.broadcast_to(t, (c, c)), tmat_t, "tn", "b"), g)
    gamma = each(lambda r, cc: jnp.where(incl, jnp.exp(jnp.where(incl, r - cc, 0.0)), 0.0), gc_rows, gc_cols)
    kb = each(lambda a, b: a * b, k, beta)
    a_mat = each(lambda a, b, gm: jnp.where(strict, _dot(a, b, "nt", "mxu") * gm, 0.0), kb, k, gamma)
    minv = _tri_inv(a_mat) if minv_saved is None else _tri_inv_saved(a_mat, minv_saved)
    eg = each(jnp.exp, gc_wide)
    u = each(lambda mi, v, b: _dot(mi, v * b, "nn", "hi3"), minv, va, beta)
    w = each(lambda mi, a, e: _dot(mi, a * e, "nn", "hi3"), minv, kb, eg)
    attn = each(lambda a, b, gm: _dot(a, b, "nt", "mxu") * gm, q, k, gamma)
    tot = each(lambda t: jnp.sum(t, axis=0, keepdims=True), g)
    q_dec = each(lambda a, e: a * e, q, eg)
    k_dec = each(lambda a, t, gw_: a * jnp.exp(t - gw_), k, tot, gc_wide)
    v_new = each(lambda uu, ww, ss: uu - _dot(ww, ss, "nn", "mxu"), u, w, s)
    o = each(lambda qd, ss, at, vn: _dot(qd, ss, "nn", "mxu") + _dot(at, vn, "nn", "mxu"), q_dec, s, attn, v_new)
    s_new = each(lambda ss, t, kd, vn: ss * jnp.exp(t) + _dot(kd, vn, "tn", "mxu"), s, tot, k_dec, v_new)
    return (o, s_new, minv) if with_minv else (o, s_new)


def _delta_args(a_ref, g_ref, al_ref, dt_ref, s_ref, hb, gcol):
    gts = [g_ref[hh] for hh in range(hb)]
    return ([a_ref[0, hh] for hh in range(hb)], [a_ref[1, hh] for hh in range(hb)], [a_ref[2, hh] for hh in range(hb)],
            [t[:, gcol:gcol + 1] for t in gts], [t[:, gcol + 1:gcol + 2] for t in gts],
            [al_ref[hh, 0:1, 0:1] for hh in range(hb)], [dt_ref[hh, 0:1, 0:1] for hh in range(hb)],
            [s_ref[hh] for hh in range(hb)])


def _delta_chunk_of_step(step, nch, nlc, rev):
    return (nch - 1 - step) if rev else (step + nlc) % nch


def _delta_fwd(act, gates_hm, alog_b, dtb_b, *, n_lat, n_ctx, nh, hb, rev):
    nt = n_lat + n_ctx
    nch, nlc = nt // CHUNK, n_lat // CHUNK
    gcol = 2 if rev else 0
    chunk = functools.partial(_delta_chunk_of_step, nch=nch, nlc=nlc, rev=rev)

    def body(a_ref, g_ref, al_ref, dt_ref, o_ref, st_ref, mi_ref, s_ref):
        n = pl.program_id(1)

        @pl.when(n == 0)
        def _():
            s_ref[...] = jnp.zeros_like(s_ref)

        args = _delta_args(a_ref, g_ref, al_ref, dt_ref, s_ref, hb, gcol)
        o, s_new, minv = _delta_chunk(*args, rev=rev, with_minv=True)
        for hh in range(hb):
            st_ref[hh] = args[7][hh]
            mi_ref[hh] = minv[hh]
            o_ref[:, hh * HD:(hh + 1) * HD] = o[hh]
            s_ref[hh] = s_new[hh]

    par = pl.BlockSpec((hb, 8, LANES), lambda h, n: (h, 0, 0))
    return pl.pallas_call(
        body, grid=(nh // hb, nch),
        in_specs=[pl.BlockSpec((3, hb, CHUNK, HD), lambda h, n: (0, h, chunk(n), 0)),
                  pl.BlockSpec((hb, CHUNK, 4), lambda h, n: (h, chunk(n), 0)), par, par],
        out_specs=[pl.BlockSpec((CHUNK, hb * HD), lambda h, n: (chunk(n), h)),
                   pl.BlockSpec((hb, None, HD, HD), lambda h, n: (h, n, 0, 0)),
                   pl.BlockSpec((hb, None, CHUNK, CHUNK), lambda h, n: (h, n, 0, 0))],
        out_shape=[jax.ShapeDtypeStruct((nt, nh * HD), F32), jax.ShapeDtypeStruct((nh, nch, HD, HD), F32),
                   jax.ShapeDtypeStruct((nh, nch, CHUNK, CHUNK), F32)],
        scratch_shapes=[pltpu.VMEM((hb, HD, HD), F32)], name="delta_fwd_rev" if rev else "delta_fwd",
        compiler_params=_cparams(("arbitrary", "arbitrary"), 0),
    )(act, gates_hm, alog_b, dtb_b)


def _delta_bwd(act, gates_hm, alog_b, dtb_b, states, minvs, do, *, n_lat, n_ctx, nh, hb, rev):
    nt = n_lat + n_ctx
    nch, nlc = nt // CHUNK, n_lat // CHUNK
    gcol = 2 if rev else 0

    def chunk(m):
        return _delta_chunk_of_step(nch - 1 - m, nch, nlc, rev)

    def body(a_ref, g_ref, al_ref, dt_ref, st_ref, mi_ref, do_ref, da_ref, dg_ref, dal_ref, ddt_ref, ds_ref):
        m = pl.program_id(1)

        @pl.when(m == 0)
        def _():
            ds_ref[...] = jnp.zeros_like(ds_ref)
            dal_ref[...] = jnp.zeros_like(dal_ref)
            ddt_ref[...] = jnp.zeros_like(ddt_ref)

        is_lat = chunk(m) < nlc
        args = _delta_args(a_ref, g_ref, al_ref, dt_ref, st_ref, hb, gcol)
        _, vjp = jax.vjp(functools.partial(_delta_chunk, rev=rev, minv_saved=[mi_ref[hh] for hh in range(hb)]), *args)
        do_h = [jnp.where(is_lat, do_ref[:, hh * HD:(hh + 1) * HD], 0.0) for hh in range(hb)]
        grads = vjp((do_h, [ds_ref[hh] for hh in range(hb)]))
        for hh in range(hb):
            dq, dk, dv, dgr, dbr, dal, ddt, ds = [gr[hh] for gr in grads]
            da_ref[0, hh] = dq
            da_ref[1, hh] = dk
            da_ref[2, hh] = dv
            lane = lax.broadcasted_iota(jnp.int32, (CHUNK, 4), 1)
            dg_ref[hh] = jnp.where(lane == 0, dgr, jnp.where(lane == 1, dbr, 0.0))
            dal_ref[hh] += jnp.broadcast_to(dal, (8, LANES))
            ddt_ref[hh] += jnp.broadcast_to(ddt, (8, LANES))
            ds_ref[hh] = ds

    par = pl.BlockSpec((hb, 8, LANES), lambda h, m: (h, 0, 0))
    return pl.pallas_call(
        body, grid=(nh // hb, nch),
        in_specs=[pl.BlockSpec((3, hb, CHUNK, HD), lambda h, m: (0, h, chunk(m), 0)),
                  pl.BlockSpec((hb, CHUNK, 4), lambda h, m: (h, chunk(m), 0)), par, par,
                  pl.BlockSpec((hb, None, HD, HD), lambda h, m: (h, nch - 1 - m, 0, 0)),
                  pl.BlockSpec((hb, None, CHUNK, CHUNK), lambda h, m: (h, nch - 1 - m, 0, 0)),
                  pl.BlockSpec((CHUNK, hb * HD), lambda h, m: (jnp.minimum(chunk(m), nlc - 1), h))],
        out_specs=[pl.BlockSpec((3, hb, CHUNK, HD), lambda h, m: (0, h, chunk(m), 0)),
                   pl.BlockSpec((hb, CHUNK, 4), lambda h, m: (h, chunk(m), 0)), par, par],
        out_shape=[jax.ShapeDtypeStruct((3, nh, nt, HD), F32), jax.ShapeDtypeStruct((nh, nt, 4), F32),
                   jax.ShapeDtypeStruct((nh, 8, LANES), F32), jax.ShapeDtypeStruct((nh, 8, LANES), F32)],
        scratch_shapes=[pltpu.VMEM((hb, HD, HD), F32)], name="delta_bwd_rev" if rev else "delta_bwd",
        compiler_params=_cparams(("arbitrary", "arbitrary"), 0),
    )(act, gates_hm, alog_b, dtb_b, states, minvs, do)


def _my_pos():
    return lax.axis_index("x"), lax.axis_index("y"), lax.axis_index("c")


def _flip(v, d):
    return 1 - v if d else v


def _bcast8(block, name):
    r, w = block.shape
    assert r % 8 == 0 and w % LANES == 0

    def body(x_ref, o_ref, send_sems, recv_sems, local_sem):
        x, y, c = _my_pos()

        def rows(px, py, pc):
            return o_ref.at[pl.ds((4 * px + 2 * py + pc) * r, r), :]

        mine = pltpu.make_async_copy(x_ref, rows(x, y, c), local_sem)
        mine.start()
        copies = []
        for kk in range(1, 8):
            dx, dy, dc = (kk >> 2) & 1, (kk >> 1) & 1, kk & 1
            peer = (_flip(x, dx), _flip(y, dy), _flip(c, dc))
            cp = pltpu.make_async_remote_copy(src_ref=x_ref, dst_ref=rows(x, y, c), send_sem=send_sems.at[kk - 1],
                                              recv_sem=recv_sems.at[kk - 1], device_id=peer, device_id_type=pl.DeviceIdType.MESH)
            cp.start()
            copies.append((cp, peer))
        for kk, (cp, peer) in enumerate(copies):
            pltpu.make_async_remote_copy(src_ref=x_ref, dst_ref=rows(*peer), send_sem=send_sems.at[kk], recv_sem=recv_sems.at[kk],
                                         device_id=peer, device_id_type=pl.DeviceIdType.MESH).wait_recv()
        for cp, _ in copies:
            cp.wait_send()
        mine.wait()

    return pl.pallas_call(
        body, out_shape=jax.ShapeDtypeStruct((8 * r, w), block.dtype),
        in_specs=[pl.BlockSpec(memory_space=pltpu.VMEM)], out_specs=pl.BlockSpec(memory_space=pltpu.VMEM),
        scratch_shapes=[pltpu.SemaphoreType.DMA((7,)), pltpu.SemaphoreType.DMA((7,)), pltpu.SemaphoreType.DMA], name=name,
        compiler_params=pltpu.CompilerParams(vmem_limit_bytes=int(max(32 * 1024 * 1024, 12 * _nbytes((r, w), block.dtype)))),
    )(block)


_CHIP_PEERS = ((1, 0), (0, 1), (1, 1))


def _col_half(ref, c, lead):
    hc = ref.shape[-1] // 2
    return ref.at[(*lead, slice(None), pl.ds(pl.multiple_of(c * hc, LANES), hc))]


def _into_slot(w, chip_idx, name):
    r, cc = w.shape
    tr = _pick(r, 512, 16)

    def body(i_ref, w_ref, o_ref):
        o_ref[...] = w_ref[...].astype(o_ref.dtype)

    grid_spec = pltpu.PrefetchScalarGridSpec(
        num_scalar_prefetch=1, grid=(r // tr,), in_specs=[pl.BlockSpec((tr, cc), lambda i, s: (i, 0))],
        out_specs=pl.BlockSpec((None, tr, cc), lambda i, s: (s[0], i, 0)))
    return pl.pallas_call(body, grid_spec=grid_spec, out_shape=jax.ShapeDtypeStruct((4, r, cc), MXU_DTYPE), name=name,
                          compiler_params=_cparams(("arbitrary",), 6 * _nbytes((tr, cc), F32)))(chip_idx, w)


_HBM = pl.BlockSpec(memory_space=pltpu.HBM)
_SEM = pl.BlockSpec(memory_space=pltpu.SEMAPHORE)
_DATAFLOW = pltpu.SideEffectType.DATAFLOW_SIDE_EFFECTING


def _in_hbm(a):
    return pltpu.with_memory_space_constraint(a, pltpu.HBM)


def _gather_start(bufs, after, name):
    n = len(bufs)

    def body(*refs):
        ins, send_sems, recv_sems, token = refs[:n], refs[n + 1], refs[n + 2], refs[-1]
        x, y, c = _my_pos()
        for a in range(n):
            piece = _col_half(ins[a], c, (2 * x + y,))
            for kk, (dx, dy) in enumerate(_CHIP_PEERS):
                pltpu.make_async_remote_copy(src_ref=piece, dst_ref=piece, send_sem=send_sems.at[3 * a + kk], recv_sem=recv_sems.at[3 * a + kk],
                                             device_id=(_flip(x, dx), _flip(y, dy), c), device_id_type=pl.DeviceIdType.MESH).start()
        token[...] = jnp.zeros_like(token)

    res = pl.pallas_call(
        body, name=name,
        out_shape=(pltpu.SemaphoreType.DMA((3 * n,)), pltpu.SemaphoreType.DMA((3 * n,)), *[pltpu.HBM(b.shape, b.dtype) for b in bufs],
                   jax.ShapeDtypeStruct((8, LANES), F32)),
        in_specs=[_HBM] * n + [ANY], out_specs=(_SEM, _SEM, *[_HBM] * n, pl.BlockSpec(memory_space=pltpu.VMEM)),
        input_output_aliases={a: 2 + a for a in range(n)}, compiler_params=pltpu.CompilerParams(has_side_effects=_DATAFLOW),
    )(*[_in_hbm(b) for b in bufs], after)
    return res[0], res[1], list(res[2:2 + n]), res[-1]


def _gather_wait(send_sems, recv_sems, bufs, after, name):
    n = len(bufs)

    def body(*refs):
        ins, s_sems, r_sems = refs[:n], refs[n], refs[n + 1]
        x, y, c = _my_pos()
        for a in range(n):
            for kk, (dx, dy) in enumerate(_CHIP_PEERS):
                px, py = _flip(x, dx), _flip(y, dy)
                cp = pltpu.make_async_remote_copy(src_ref=_col_half(ins[a], c, (2 * x + y,)), dst_ref=_col_half(ins[a], c, (2 * px + py,)),
                                                  send_sem=s_sems.at[3 * a + kk], recv_sem=r_sems.at[3 * a + kk], device_id=(px, py, c),
                                                  device_id_type=pl.DeviceIdType.MESH)
                cp.wait_send()
                cp.wait_recv()

    return pl.pallas_call(
        body, name=name, out_shape=[pltpu.HBM(b.shape, b.dtype) for b in bufs],
        in_specs=[_HBM] * n + [_SEM, _SEM, ANY], out_specs=[_HBM] * n, input_output_aliases={a: a for a in range(n)},
        compiler_params=pltpu.CompilerParams(has_side_effects=_DATAFLOW),
    )(*bufs, send_sems, recv_sems, after)


def _gather_forward(bufs, name):
    n = len(bufs)

    def body(*refs):
        outs = refs[n:2 * n]
        send_sems, recv_sems = refs[2 * n:]
        x, y, c = _my_pos()
        sib = (x, y, 1 - c)

        def remote(a, kk, half):
            dx, dy = _CHIP_PEERS[kk]
            piece = _col_half(outs[a], half, (2 * _flip(x, dx) + _flip(y, dy),))
            return pltpu.make_async_remote_copy(src_ref=piece, dst_ref=piece, send_sem=send_sems.at[3 * a + kk], recv_sem=recv_sems.at[3 * a + kk],
                                                device_id=sib, device_id_type=pl.DeviceIdType.MESH)

        sends = [remote(a, kk, c) for a in range(n) for kk in range(3)]
        for cp in sends:
            cp.start()
        for a in range(n):
            for kk in range(3):
                remote(a, kk, 1 - c).wait_recv()
        for cp in sends:
            cp.wait_send()

    return pl.pallas_call(
        body, out_shape=[jax.ShapeDtypeStruct(b.shape, b.dtype) for b in bufs],
        in_specs=[ANY] * n, out_specs=[ANY] * n, input_output_aliases={a: a for a in range(n)},
        scratch_shapes=[pltpu.SemaphoreType.DMA((3 * n,))] * 2, name=name,
    )(*bufs)


def _scatter_start(parts, after, name):
    n = len(parts)

    def body(*refs):
        ins, lands, send_sems, recv_sems, token = refs[:n], refs[n:2 * n], refs[2 * n + 1], refs[2 * n + 2], refs[-1]
        x, y, c = _my_pos()
        for a in range(n):
            for kk, (dx, dy) in enumerate(_CHIP_PEERS):
                px, py = _flip(x, dx), _flip(y, dy)
                pltpu.make_async_remote_copy(src_ref=ins[a].at[2 * px + py], dst_ref=lands[a].at[2 * x + y], send_sem=send_sems.at[3 * a + kk],
                                             recv_sem=recv_sems.at[3 * a + kk], device_id=(px, py, c), device_id_type=pl.DeviceIdType.MESH).start()
        token[...] = jnp.zeros_like(token)

    thru = [pltpu.HBM(p.shape, p.dtype) for p in parts]
    res = pl.pallas_call(
        body, name=name,
        out_shape=(pltpu.SemaphoreType.DMA((3 * n,)), pltpu.SemaphoreType.DMA((3 * n,)), *thru, *thru, jax.ShapeDtypeStruct((8, LANES), F32)),
        in_specs=[_HBM] * (2 * n) + [ANY], out_specs=(_SEM, _SEM, *[_HBM] * (2 * n), pl.BlockSpec(memory_space=pltpu.VMEM)),
        input_output_aliases={a: 2 + a for a in range(2 * n)}, compiler_params=pltpu.CompilerParams(has_side_effects=_DATAFLOW),
    )(*[_in_hbm(p) for p in parts], *[_in_hbm(lax.empty(p.shape, p.dtype)) for p in parts], after)
    return res[0], res[1], list(res[2:2 + n]), list(res[2 + n:2 + 2 * n]), res[-1]


def _scatter_wait(send_sems, recv_sems, parts, lands, after, name):
    n = len(parts)

    def body(*refs):
        ins, lnd, s_sems, r_sems = refs[:n], refs[n:2 * n], refs[2 * n], refs[2 * n + 1]
        x, y, c = _my_pos()
        for a in range(n):
            for kk, (dx, dy) in enumerate(_CHIP_PEERS):
                px, py = _flip(x, dx), _flip(y, dy)
                cp = pltpu.make_async_remote_copy(src_ref=ins[a].at[2 * px + py], dst_ref=lnd[a].at[2 * px + py], send_sem=s_sems.at[3 * a + kk],
                                                  recv_sem=r_sems.at[3 * a + kk], device_id=(px, py, c), device_id_type=pl.DeviceIdType.MESH)
                cp.wait_send()
                cp.wait_recv()

    thru = [pltpu.HBM(p.shape, p.dtype) for p in parts]
    res = pl.pallas_call(
        body, name=name, out_shape=[*thru, *thru],
        in_specs=[_HBM] * (2 * n) + [_SEM, _SEM, ANY], out_specs=[_HBM] * (2 * n), input_output_aliases={a: a for a in range(2 * n)},
        compiler_params=pltpu.CompilerParams(has_side_effects=_DATAFLOW),
    )(*parts, *lands, send_sems, recv_sems, after)
    return list(res[:n]), list(res[n:])


def _pair_send_halves(grads, name):
    n = len(grads)

    def body(*refs):
        ins, outs = refs[:n], refs[n:2 * n]
        send_sems, recv_sems = refs[2 * n:]
        x, y, c = _my_pos()
        sib = (x, y, 1 - c)
        sends = []
        for a in range(n):
            cp = pltpu.make_async_remote_copy(src_ref=_col_half(ins[a], 1 - c, (slice(None),)), dst_ref=outs[a], send_sem=send_sems.at[a],
                                              recv_sem=recv_sems.at[a], device_id=sib, device_id_type=pl.DeviceIdType.MESH)
            cp.start()
            sends.append(cp)
        for cp in sends:
            cp.wait_recv()
        for cp in sends:
            cp.wait_send()

    return pl.pallas_call(
        body, out_shape=[jax.ShapeDtypeStruct((4, g.shape[1], g.shape[2] // 2), g.dtype) for g in grads],
        in_specs=[ANY] * n, out_specs=[ANY] * n,
        scratch_shapes=[pltpu.SemaphoreType.DMA((n,)), pltpu.SemaphoreType.DMA((n,))], name=name,
    )(*grads)


def _pair_join_halves(bufs, name):
    n = len(bufs)

    def body(*refs):
        outs = refs[n:2 * n]
        send_sems, recv_sems = refs[2 * n:]
        x, y, c = _my_pos()
        sib = (x, y, 1 - c)
        sends = []
        for a in range(n):
            mine = _col_half(outs[a], c, ())
            cp = pltpu.make_async_remote_copy(src_ref=mine, dst_ref=mine, send_sem=send_sems.at[a], recv_sem=recv_sems.at[a],
                                              device_id=sib, device_id_type=pl.DeviceIdType.MESH)
            cp.start()
            sends.append(cp)
        for a in range(n):
            other = _col_half(outs[a], 1 - c, ())
            pltpu.make_async_remote_copy(src_ref=other, dst_ref=other, send_sem=send_sems.at[a], recv_sem=recv_sems.at[a],
                                         device_id=sib, device_id_type=pl.DeviceIdType.MESH).wait_recv()
        for cp in sends:
            cp.wait_send()

    return pl.pallas_call(
        body, out_shape=[jax.ShapeDtypeStruct(b.shape, b.dtype) for b in bufs],
        in_specs=[ANY] * n, out_specs=[ANY] * n, input_output_aliases={a: a for a in range(n)},
        scratch_shapes=[pltpu.SemaphoreType.DMA((n,)), pltpu.SemaphoreType.DMA((n,))], name=name,
    )(*bufs)


def _add_my_half(g, recv, c_idx, name):
    _, r, cc = g.shape
    hc = cc // 2
    tc = _pick(hc, 256)
    per = hc // tc

    def body(c_ref, g_ref, r_ref, o_ref):
        o_ref[...] = (g_ref[...] + r_ref[...]).astype(o_ref.dtype)

    grid_spec = pltpu.PrefetchScalarGridSpec(
        num_scalar_prefetch=1, grid=(4, per),
        in_specs=[pl.BlockSpec((None, r, tc), lambda s, j, c_ref: (s, 0, c_ref[0] * per + j)),
                  pl.BlockSpec((None, r, tc), lambda s, j, c_ref: (s, 0, j))],
        out_specs=pl.BlockSpec((None, r, tc), lambda s, j, c_ref: (s, 0, j)))
    return pl.pallas_call(body, grid_spec=grid_spec, out_shape=jax.ShapeDtypeStruct((4, r, hc), WIRE_DTYPE), name=name,
                          compiler_params=_cparams(("arbitrary", "arbitrary"), 6 * _nbytes((r, tc), F32)))(c_idx, g, recv)


def _sum_slots(own, recv, slots, name):
    _, r, hc = own.shape
    tc = _pick(hc, 256)
    per = hc // tc

    def body(s0, s1, s2, s3, s4, a_ref, b_ref, c_ref, d_ref, o_ref):
        f = lambda ref: ref[...].astype(F32)
        o_ref[...] = ((f(a_ref) + f(b_ref)) + f(c_ref)) + f(d_ref)

    slot = lambda i: pl.BlockSpec((None, r, tc), lambda j, *s: (s[i][0], 0, j))
    grid_spec = pltpu.PrefetchScalarGridSpec(
        num_scalar_prefetch=5, grid=(per,), in_specs=[slot(0), slot(1), slot(2), slot(3)],
        out_specs=pl.BlockSpec((r, tc), lambda j, *s: (0, s[4][0] * per + j)))
    return pl.pallas_call(body, grid_spec=grid_spec, out_shape=jax.ShapeDtypeStruct((r, 2 * hc), F32), name=name,
                          compiler_params=_cparams(("arbitrary",), 14 * _nbytes((r, tc), F32)))(*slots, own, recv, recv, recv)


def _adamw_math(w, g, m, v):
    m = ADAM_B1 * m + (1.0 - ADAM_B1) * g
    v = ADAM_B2 * v + (1.0 - ADAM_B2) * (g * g)
    m_hat = m / (1.0 - ADAM_B1 ** ADAM_STEP)
    v_hat = v / (1.0 - ADAM_B2 ** ADAM_STEP)
    delta = -ADAM_LR * (m_hat / (jnp.sqrt(v_hat) + ADAM_EPS) + ADAM_WD * w)
    return delta, m, v


def _adamw(w, g, m, v, name):
    r, cc = w.shape
    tr = _pick(r, max(8, (512 * 1024) // max(cc, 1)), 8)

    def body(w_ref, g_ref, m_ref, v_ref, d_ref, nm_ref, nv_ref):
        d, nm, nv = _adamw_math(w_ref[...], g_ref[...], m_ref[...], v_ref[...])
        d_ref[...] = d
        nm_ref[...] = nm
        nv_ref[...] = nv

    spec = pl.BlockSpec((tr, cc), lambda i: (i, 0))
    return pl.pallas_call(
        body, grid=(r // tr,), in_specs=[spec] * 4, out_specs=[spec] * 3,
        out_shape=[jax.ShapeDtypeStruct((r, cc), F32)] * 3, name=name,
        compiler_params=_cparams(("arbitrary",), 16 * _nbytes((tr, cc), F32)))(w, g, m, v)


def _sum8(allv, name):
    r8, w = allv.shape
    r = r8 // 8

    def body(a_ref, o_ref):
        acc = a_ref[0:r, :]
        for d in range(1, 8):
            acc = acc + a_ref[d * r:(d + 1) * r, :]
        o_ref[...] = acc

    return pl.pallas_call(body, out_shape=jax.ShapeDtypeStruct((r, w), F32), name=name,
                          compiler_params=_cparams((), 12 * _nbytes((r8, w), F32)))(allv)


def _dmod_prep(alld, name):
    _, w = alld.shape
    tw = _pick(w, 2048)

    def body(a_ref, d_ref, b_ref):
        ctx = a_ref[1:2, :]
        tot = a_ref[0:1, :]
        d_ref[0:1, :] = tot
        for d in range(1, 8):
            ctx = ctx + a_ref[8 * d + 1:8 * d + 2, :]
            tot = tot + a_ref[8 * d:8 * d + 1, :]
            d_ref[d:d + 1, :] = a_ref[8 * d:8 * d + 1, :]
        d_ref[8:9, :] = ctx
        d_ref[9:16, :] = jnp.zeros((7, tw), F32)
        b_ref[...] = tot + ctx

    return pl.pallas_call(
        body, grid=(w // tw,), in_specs=[pl.BlockSpec((64, tw), lambda j: (0, j))],
        out_specs=[pl.BlockSpec((16, tw), lambda j: (0, j)), pl.BlockSpec((1, tw), lambda j: (0, j))],
        out_shape=[jax.ShapeDtypeStruct((16, w), F32), jax.ShapeDtypeStruct((1, w), F32)], name=name,
        compiler_params=_cparams(("arbitrary",), 0))(alld)


def _pack(arrs, rows=8):
    flat = jnp.concatenate([a.reshape(-1).astype(F32) for a in arrs])
    per = rows * LANES
    n = flat.shape[0]
    padded = -(-n // per) * per
    flat = jnp.pad(flat, (0, padded - n))
    return flat.reshape(rows, padded // rows)


def _unpack(slab, shapes):
    flat = slab.reshape(-1)
    out, off = [], 0
    for s in shapes:
        n = math.prod(s)
        out.append(flat[off:off + n].reshape(s))
        off += n
    return out


def kernel(x, c, ctx, c_ctx, ln_in_g, ln_in_b, w_mod, b_mod, w_in, w_qkv_conv, a_log_f, dt_bias_f, a_log_b, dt_bias_b, dn_norm_g, conf_dw_w, conf_dw_b, conf_ln_g, conf_ln_b, w_out, ln1_g, ln1_b, w_mlp1, b_mlp1, w_mlp2, b_mlp2, ln2_g, ln2_b, loss_target, m_c_ctx, m_ln_in_g, m_ln_in_b, m_w_mod, m_b_mod, m_w_in, m_w_qkv_conv, m_a_log_f, m_dt_bias_f, m_a_log_b, m_dt_bias_b, m_dn_norm_g, m_conf_dw_w, m_conf_dw_b, m_conf_ln_g, m_conf_ln_b, m_w_out, m_ln1_g, m_ln1_b, m_w_mlp1, m_b_mlp1, m_w_mlp2, m_b_mlp2, m_ln2_g, m_ln2_b, v_c_ctx, v_ln_in_g, v_ln_in_b, v_w_mod, v_b_mod, v_w_in, v_w_qkv_conv, v_a_log_f, v_dt_bias_f, v_a_log_b, v_dt_bias_b, v_dn_norm_g, v_conf_dw_w, v_conf_dw_b, v_conf_ln_g, v_conf_ln_b, v_w_out, v_ln1_g, v_ln1_b, v_w_mlp1, v_b_mlp1, v_w_mlp2, v_b_mlp2, v_ln2_g, v_ln2_b):
    n_lat, d = x.shape[1], x.shape[2]
    n_ctx = ctx.shape[1]
    nt = n_lat + n_ctx
    dn = d // 2
    nh = dn // HD
    conf = d - dn
    half = conf // 2
    gw = math.isqrt(n_lat)
    ff = 4 * w_mlp1.shape[2]
    fl = w_mlp1.shape[2]
    in_cols = 4 * w_in.shape[2]
    r_in = w_in.shape[2]
    n_gate = 4 * nh
    gb = max(LANES, d // 8)
    g0 = 4 * dn + 2 * conf
    in_pad = g0 + gb
    ml = w_mod.shape[2]
    assert gw * gw == n_lat and nt % gw == 0 and in_cols == 4 * dn + n_gate + 2 * conf and nh * HD == dn
    hb = 4 if nh % 4 == 0 else (2 if nh % 2 == 0 else 1)

    mx, my, mc = _my_pos()
    chip = 2 * mx + my
    dev = 4 * mx + 2 * my + mc
    c_idx = jnp.reshape(mc, (1,)).astype(jnp.int32)

    x2, ctx2, tgt2 = x[0], ctx[0], loss_target[0]
    row = lambda a: a.reshape(1, -1)

    e0_shapes = [(d,), w_qkv_conv.shape[1:], conf_dw_w.shape[1:]]
    e0 = _bcast8(_pack([c[0], w_qkv_conv[0], conf_dw_w[0]]), "bcast_inputs")
    e0 = e0.reshape(8, -1)
    per_dev = [_unpack(e0[dd], e0_shapes) for dd in range(8)]
    all_c = jnp.stack([p[0] for p in per_dev])
    w_conv = jnp.concatenate([per_dev[2 * j][1] for j in range(4)], axis=1)
    w_dw = jnp.concatenate([per_dev[2 * j][2] for j in range(4)], axis=1)

    chip_idx = jnp.reshape(chip, (1,)).astype(jnp.int32)
    win_t_loc = jnp.transpose(w_in[0]).astype(MXU_DTYPE)
    slots = [_into_slot(w, chip_idx, f"weights_into_slot_{i}") for i, w in enumerate([win_t_loc, w_out[0], w_mlp1[0], w_mlp2[0]])]
    gi_send, gi_recv, gi_bufs, gi_token = _gather_start(slots[:1], all_c, "gather_w_in_start")

    s16 = jnp.concatenate([all_c, c_ctx[None, :], jnp.zeros((7, d), F32)], axis=0) + gi_token[0:1, 0:1]
    b_mod_loc = lax.dynamic_slice(b_mod, (0, chip * ml), (1, ml))
    to_mxu = lambda t: t.astype(MXU_DTYPE)
    modblk = _matmul(s16, w_mod[0], form="nn", m=16, n=ml, k=d, tm=16, tn=_pick(ml, 1024), tk=_pick(d, 1024),
                     out_dtype=F32, name="mod_fwd", bias=b_mod_loc, a_fn=lambda t: to_mxu(_silu(t)), b_fn=to_mxu)
    allmod = _bcast8(modblk, "bcast_mod").reshape(8, 16, ml)[0::2]
    mod_mine = lax.dynamic_index_in_dim(allmod, dev, axis=1, keepdims=False).reshape(1, 4 * ml)
    mod_ctx = allmod[:, 8, :].reshape(1, 4 * ml)
    sh_a, sc_a, g_a, sh_m, sc_m, g_m = [mod_mine[:, i * d:(i + 1) * d] for i in range(6)]
    csh_a, csc_a = mod_ctx[:, 0:d], mod_ctx[:, d:2 * d]

    tmr = _pick(n_lat, 128, 8)
    tmc = _pick(n_ctx, 128, 8)
    ln_in = [row(ln_in_g), row(ln_in_b)]

    def stage_a(rv, pv):
        x0 = _ln(rv[0], pv[0], pv[1])
        return [x0 * (1.0 + pv[3]) + pv[2]], []

    xm_lat, = _rowwise(stage_a, [(x2, d, 0, 0)], ln_in + [sh_a, sc_a], [(d, MXU_DTYPE)], [], n_rows=n_lat, tm=tmr, name="ln_mod_lat")
    xm_ctx, = _rowwise(stage_a, [(ctx2, d, 0, 0)], ln_in + [csh_a, csc_a], [(d, MXU_DTYPE)], [], n_rows=n_ctx, tm=tmc, name="ln_mod_ctx")
    xm = jnp.concatenate([xm_lat, xm_ctx], axis=0)

    gw_in, = _gather_forward(_gather_wait(gi_send, gi_recv, gi_bufs, xm, "gather_w_in_wait"), "gather_w_in_forward")
    gs_send, gs_recv, gs_bufs, gs_token = _gather_start(slots[1:], gw_in, "gather_rest_start")
    win_t = gw_in.reshape(in_cols, d)
    win_t = jnp.concatenate([win_t[:4 * dn], win_t[4 * dn + n_gate:], win_t[4 * dn:4 * dn + n_gate],
                             jnp.zeros((gb - n_gate, d), MXU_DTYPE)], axis=0)

    tm_nt = _pick(nt, 1280, 16)
    tn_in = _pick(in_pad, 1280)
    tk_d = _pick(d, 2048)
    h = _matmul(xm, win_t, form="nt", m=nt, n=in_pad, k=d, tm=tm_nt, tn=tn_in, tk=tk_d, out_dtype=F32, name="in_proj",
                bias=jnp.zeros((1, in_pad), F32) + gs_token[0:1, 0:1])

    act = _conv7_fwd(h, w_conv, n_lat=n_lat, n_ctx=n_ctx, nh=nh)
    gates_hm = jnp.transpose(h[:, g0:g0 + n_gate].reshape(nt, 4, nh), (2, 0, 1))
    bc = lambda a: jnp.broadcast_to(a.reshape(nh, 1, 1), (nh, 8, LANES))
    dk = dict(n_lat=n_lat, n_ctx=n_ctx, nh=nh, hb=hb)
    o_f, st_f, mi_f = _delta_fwd(act, gates_hm, bc(a_log_f), bc(dt_bias_f), rev=False, **dk)
    o_b, st_b, mi_b = _delta_fwd(act, gates_hm, bc(a_log_b), bc(dt_bias_b), rev=True, **dk)

    h3 = h.reshape(nt // gw, gw, in_pad)
    ck = dict(gw=gw, col0=4 * dn, half=half)
    yh = _conf_conv_fwd(h3, w_dw, vertical=False, **ck).reshape(n_lat, half)
    yv = _conf_conv_fwd(h3, w_dw, vertical=True, **ck).reshape(n_lat, half)

    def mix_fn(o_heads, z_heads, ya, yb, dng, ba, bb, ga, gbb, la, lb):
        outs = []
        for o, z in zip(o_heads, z_heads):
            ms = jnp.mean(o * o, axis=-1, keepdims=True)
            outs.append(o * lax.rsqrt(ms + 1e-6) * dng * _silu(z))
        ya, yb = ya + ba, yb + bb
        mu = (jnp.sum(ya, axis=-1, keepdims=True) + jnp.sum(yb, axis=-1, keepdims=True)) / conf
        ya, yb = ya - mu, yb - mu
        var = (jnp.sum(ya * ya, axis=-1, keepdims=True) + jnp.sum(yb * yb, axis=-1, keepdims=True)) / conf
        rs = lax.rsqrt(var + LN_EPS)
        ca, cb = ya * rs * ga + la, yb * rs * gbb + lb
        return outs, _silu(ca), _silu(cb)

    def heads(t):
        return [t[:, i * HD:(i + 1) * HD] for i in range(nh)]

    def halves(p):
        return p[:, :half], p[:, half:]

    mix_params = [row(dn_norm_g), row(conf_dw_b), row(conf_ln_g), row(conf_ln_b)]

    def mix_args(rv, pv):
        (ba, bb), (ga, gbb), (la, lb) = halves(pv[1]), halves(pv[2]), halves(pv[3])
        return (heads(rv[0] + rv[1]), heads(rv[2]), rv[3], rv[4], pv[0], ba, bb, ga, gbb, la, lb)

    def stage_e(rv, pv):
        outs, ca, cb = mix_fn(*mix_args(rv, pv))
        return [jnp.concatenate(outs + [ca, cb], axis=-1)], []

    mix_rows = [(o_f, dn, 0, 0), (o_b, dn, 0, 0), (h, dn, 3, 0), (yh, half, 0, 0), (yv, half, 0, 0)]
    mix, = _rowwise(stage_e, mix_rows, mix_params, [(d, MXU_DTYPE)], [], n_rows=n_lat, tm=tmr, name="mixer_fwd")

    gw_out, gw_1, gw_2 = _gather_forward(_gather_wait(gs_send, gs_recv, gs_bufs, mix, "gather_rest_wait"), "gather_rest_forward")
    wout_full = gw_out.reshape(d, d)
    w2_full = gw_2.reshape(ff, d)
    w1_sh = gw_1

    tm_l = _pick(n_lat, 1024, 16)
    tn_d = _pick(d, 1024)
    y1 = _matmul(mix, wout_full, form="nn", m=n_lat, n=d, k=d, tm=tm_l, tn=tn_d, tk=tk_d, out_dtype=F32, name="out_proj")

    def res1(x0, y1v, ga, l1g, l1b, shm, scm):
        x1 = _ln(ALPHA * x0 + ga * y1v, l1g, l1b)
        return x1, x1 * (1.0 + scm) + shm

    f_params = ln_in + [g_a, row(ln1_g), row(ln1_b), sh_m, sc_m]

    def stage_f(rv, pv):
        x1, u = res1(_ln(rv[0], pv[0], pv[1]), rv[1], *pv[2:])
        return [x1, u], []

    x1, u = _rowwise(stage_f, [(x2, d, 0, 0), (y1, d, 0, 0)], f_params, [(d, F32), (d, MXU_DTYPE)], [], n_rows=n_lat, tm=tmr, name="res1_fwd")

    tn_f = _pick(fl, 1024)
    relu_h = _matmul(u, w1_sh, form="nn", m=n_lat, n=ff, k=d, tm=tm_l, tn=tn_f, tk=tk_d, out_dtype=MXU_DTYPE, name="mlp1",
                     bias=row(b_mlp1), epi=lambda r: jnp.maximum(r, 0.0), b_split=fl)
    sq = lambda t: (t.astype(F32) * t.astype(F32)).astype(MXU_DTYPE)
    tk_f = _pick(ff, 2048)
    y2 = _matmul(relu_h, w2_full, form="nn", m=n_lat, n=d, k=ff, tm=tm_l, tn=tn_d, tk=tk_f, out_dtype=F32, name="mlp2",
                 bias=row(b_mlp2), a_fn=sq)

    def loss_fn(x1v, y2v, gm, l2g, l2b, tgt):
        x2v = _ln(ALPHA * x1v + gm * y2v, l2g, l2b)
        return (0.5 / d) * jnp.sum(jnp.square(x2v - tgt))

    def stage_g(rv, pv):
        loss, grads = jax.value_and_grad(loss_fn, argnums=(0, 1, 2, 3, 4))(rv[0], rv[1], pv[0], pv[1], pv[2], rv[2])
        dx1, dy2r, dgm, dl2g, dl2b = grads
        dy2 = dy2r
        return [dx1, dy2], [jnp.reshape(loss, (1, 1)), dgm, dl2g, dl2b, jnp.sum(dy2, axis=0, keepdims=True)]

    dx1p, dy2, loss_acc, d_g_m, d_ln2_g, d_ln2_b, d_b_mlp2 = _rowwise(
        stage_g, [(x1, d, 0, 0), (y2, d, 0, 0), (tgt2, d, 0, 0)], [g_m, row(ln2_g), row(ln2_b)],
        [(d, F32), (d, MXU_DTYPE)], [LANES, d, d, d, d], n_rows=n_lat, tm=tmr, name="loss_res2_bwd")

    dhid, d_b_mlp1 = _matmul(dy2, w2_full, form="nt", m=n_lat, n=ff, k=d, tm=tm_l, tn=tn_f, tk=tk_d, out_dtype=MXU_DTYPE, name="mlp2_bwd_x",
                             epi=lambda r, rh: r * (2.0 * rh.astype(F32)), epi_in=relu_h, colsum=True)
    tk_l = _pick(n_lat, 2048, 16)
    d_w2 = _matmul(relu_h, dy2, form="tn", m=ff, n=d, k=n_lat, tm=tn_f, tn=tn_d, tk=tk_l, out_dtype=F32, name="mlp2_bwd_w", a_fn=sq)
    du = _matmul(dhid, w1_sh, form="nt", m=n_lat, n=d, k=ff, tm=tm_l, tn=tn_d, tk=_pick(fl, 2048), out_dtype=F32, name="mlp1_bwd_x", b_split=fl)
    d_w1 = _matmul(u, dhid, form="tn", m=d, n=ff, k=n_lat, tm=tn_d, tn=tn_f, tk=tk_l, out_dtype=F32, name="mlp1_bwd_w", o_split=fl)

    as_idx = lambda v: jnp.reshape(v, (1,)).astype(jnp.int32)
    slot_idx = [chip_idx, as_idx(2 * (1 - mx) + my), as_idx(2 * mx + (1 - my)), as_idx(2 * (1 - mx) + (1 - my)), c_idx]
    mlp_grads = [d_w1, d_w2.reshape(4, fl, d)]
    mlp_recv = _pair_send_halves(mlp_grads, "mlp_grads_pair_send")
    mlp_sums = [_add_my_half(g, r, c_idx, f"mlp_grads_pair_add_{i}") for i, (g, r) in enumerate(zip(mlp_grads, mlp_recv))]
    ms_send, ms_recv, mlp_sums, mlp_lands, ms_token = _scatter_start(mlp_sums, loss_acc, "mlp_grads_scatter_start")

    def stage_h(rv, pv):
        x0 = _ln(rv[0], pv[0], pv[1])
        _, vjp = jax.vjp(res1, x0, rv[1], *pv[2:])
        dx0, dy1r, dga, dl1g, dl1b, dshm, dscm = vjp((rv[3], rv[2]))
        return [dx0, dy1r], [dga, dl1g, dl1b, dshm, dscm]

    h_params = ln_in + [g_a + ms_token[0:1, 0:1]] + f_params[3:]
    dx0p, dy1, d_g_a, d_ln1_g, d_ln1_b, d_sh_m, d_sc_m = _rowwise(
        stage_h, [(x2, d, 0, 0), (y1, d, 0, 0), (du, d, 0, 0), (dx1p, d, 0, 0)], h_params,
        [(d, F32), (d, MXU_DTYPE)], [d, d, d, d, d], n_rows=n_lat, tm=tmr, name="res1_bwd")

    dmix = _matmul(dy1, wout_full, form="nt", m=n_lat, n=d, k=d, tm=tm_l, tn=tn_d, tk=tk_d, out_dtype=F32, name="out_proj_bwd_x")
    d_wout = _matmul(mix, dy1, form="tn", m=d, n=d, k=n_lat, tm=tn_d, tn=tn_d, tk=tk_l, out_dtype=F32, name="out_proj_bwd_w")

    def stage_i(rv, pv):
        args = mix_args(rv, pv)
        _, vjp = jax.vjp(mix_fn, *args)
        dm = rv[5]
        d_outs = [dm[:, i * HD:(i + 1) * HD] for i in range(nh)]
        dca, dcb = dm[:, dn:dn + half], dm[:, dn + half:]
        do_h, dz_h, dya, dyb, ddng, dba, dbb, dga, dgbb, dla, dlb = vjp((d_outs, dca, dcb))
        cat = lambda *p: jnp.concatenate(p, axis=-1)
        return ([cat(*do_h), cat(*dz_h), dya, dyb], [ddng, cat(dba, dbb), cat(dga, dgbb), cat(dla, dlb)])

    do, dz, dyh, dyv, d_dn_norm_g, d_conf_dw_b, d_conf_ln_g, d_conf_ln_b = _rowwise(
        stage_i, mix_rows + [(dmix, d, 0, 0)], mix_params, [(dn, F32), (dn, MXU_DTYPE), (half, F32), (half, F32)],
        [HD, conf, conf, conf], n_rows=n_lat, tm=tmr, name="mixer_bwd")

    dval_h, dgate_h, d_wdw_h = _conf_conv_bwd(h3, w_dw, dyh.reshape(gw, gw, half), vertical=False, **ck)
    dval_v, dgate_v, d_wdw_v = _conf_conv_bwd(h3, w_dw, dyv.reshape(gw, gw, half), vertical=True, **ck)
    dact_f, dgt_f, d_alog_f, d_dt_f = _delta_bwd(act, gates_hm, bc(a_log_f), bc(dt_bias_f), st_f, mi_f, do, rev=False, **dk)
    dact_b, dgt_b, d_alog_b, d_dt_b = _delta_bwd(act, gates_hm, bc(a_log_b), bc(dt_bias_b), st_b, mi_b, do, rev=True, **dk)
    dh_qkv, d_wconv = _conv7_bwd(h, w_conv, dact_f, dact_b, n_lat=n_lat, n_ctx=n_ctx, nh=nh)

    dgates = jnp.stack([dgt_f[..., 0], dgt_f[..., 1], dgt_b[..., 0], dgt_b[..., 1]], axis=0)
    dgates = jnp.transpose(dgates, (2, 0, 1)).reshape(nt, n_gate)
    dgates = jnp.pad(dgates, ((0, 0), (0, gb - n_gate))).astype(MXU_DTYPE)
    zrows = lambda t: jnp.pad(t, ((0, n_ctx), (0, 0)))
    dh = jnp.concatenate([dh_qkv, zrows(dz), zrows(dval_h.reshape(n_lat, half)), zrows(dval_v.reshape(n_lat, half)),
                          zrows(dgate_h.reshape(n_lat, half)), zrows(dgate_v.reshape(n_lat, half)), dgates], axis=1)

    tk_in = _pick(in_pad, 2560)
    tk_nt = _pick(nt, 2560, 16)
    d_win_t = _matmul(dh, xm, form="tn", m=in_pad, n=d, k=nt, tm=tn_in, tn=tn_d, tk=tk_nt, out_dtype=F32, name="in_proj_bwd_w")
    d_win_t = jnp.concatenate([d_win_t[:4 * dn], d_win_t[g0:g0 + n_gate], d_win_t[4 * dn:g0]], axis=0)

    proj_grads = [d_win_t.reshape(4, r_in, d), d_wout.reshape(4, d // 4, d)]
    proj_recv = _pair_send_halves(proj_grads, "proj_grads_pair_send")
    proj_sums = [_add_my_half(g, r, c_idx, f"proj_grads_pair_add_{i}") for i, (g, r) in enumerate(zip(proj_grads, proj_recv))]
    ps_send, ps_recv, proj_sums, proj_lands, ps_token = _scatter_start(proj_sums, loss_acc, "proj_grads_scatter_start")
    mlp_sums, mlp_lands = _scatter_wait(ms_send, ms_recv, mlp_sums, mlp_lands, ps_token, "mlp_grads_scatter_wait")
    mlp_mine = [_sum_slots(own, rcv, slot_idx, f"mlp_grads_chip_sum_{i}") for i, (own, rcv) in enumerate(zip(mlp_sums, mlp_lands))]
    g_w_mlp1, g_w_mlp2 = _pair_join_halves(mlp_mine, "mlp_grads_pair_join")

    dxm = _matmul(dh, win_t, form="nn", m=nt, n=d, k=in_pad, tm=tm_nt, tn=tn_d, tk=tk_in, out_dtype=F32, name="in_proj_bwd_x",
                  bias=jnp.zeros((1, d), F32) + ps_token[0:1, 0:1])

    def mod_in(xr, lg, lb, sh, sc):
        x0 = _ln(xr, lg, lb)
        return x0, x0 * (1.0 + sc) + sh

    def stage_j(rv, pv):
        _, vjp = jax.vjp(mod_in, rv[0], *pv)
        dx0 = rv[2] if len(rv) > 2 else jnp.zeros_like(rv[1])
        dxr, dlg, dlb, dsh, dsc = vjp((dx0, rv[1]))
        return [dxr], [dlg, dlb, dsh, dsc]

    grad_x, dlg_l, dlb_l, d_sh_a, d_sc_a = _rowwise(
        stage_j, [(x2, d, 0, 0), (dxm, d, 0, 0), (dx0p, d, 0, 0)], ln_in + [sh_a, sc_a], [(d, F32)], [d, d, d, d],
        n_rows=n_lat, tm=tmr, name="ln_mod_bwd_lat")
    _, dlg_c, dlb_c, d_csh_a, d_csc_a = _rowwise(
        stage_j, [(ctx2, d, 0, 0), (dxm, d, 0, n_lat // tmc)], ln_in + [csh_a, csc_a], [(d, F32)], [d, d, d, d],
        n_rows=n_ctx, tm=tmc, name="ln_mod_bwd_ctx")

    zd = jnp.zeros((1, d), F32)
    dmod_rows = jnp.concatenate([jnp.concatenate([d_sh_a, d_sc_a, d_g_a, d_sh_m, d_sc_m, d_g_m], axis=1),
                                 jnp.concatenate([d_csh_a, d_csc_a, zd, zd, zd, zd], axis=1), jnp.zeros((6, 6 * d), F32)], axis=0)
    d16, g_b_mod = _dmod_prep(_bcast8(dmod_rows, "bcast_dmod"), "dmod_prep")
    d16_loc = lax.dynamic_slice(d16, (0, chip * ml), (16, ml))
    g_w_mod = _matmul(s16, d16_loc, form="tn", m=d, n=ml, k=16, tm=_pick(d, 512), tn=_pick(ml, 1024), tk=16, out_dtype=F32,
                      name="mod_bwd_w", a_fn=_silu, kind="hi")
    dsilu = _matmul(d16_loc, w_mod[0], form="nt", m=16, n=d, k=ml, tm=16, tn=_pick(d, 1024), tk=_pick(ml, 1024), out_dtype=F32,
                    name="mod_bwd_c", a_fn=to_mxu, b_fn=to_mxu)
    d_cctx_part = dsilu[8:9, :] * (1 - mc).astype(F32)

    head_sum = lambda t: t[:, 0, 0]
    small = [d_cctx_part, dlg_l + dlg_c, dlb_l + dlb_c, d_wconv, head_sum(d_alog_f), head_sum(d_dt_f), head_sum(d_alog_b),
             head_sum(d_dt_b), d_dn_norm_g, jnp.concatenate([d_wdw_h, d_wdw_v], axis=1), d_conf_dw_b, d_conf_ln_g, d_conf_ln_b,
             d_ln1_g, d_ln1_b, d_b_mlp1, d_b_mlp2, d_ln2_g, d_ln2_b]
    small_shapes = [(d,), (d,), (d,), (SHORT_CONV, 3 * dn), (nh,), (nh,), (nh,), (nh,), (HD,), (CONF_K, conf), (conf,), (conf,), (conf,),
                    (d,), (d,), (ff,), (d,), (d,), (d,)]
    ssum = _sum8(_bcast8(_pack(small), "bcast_small_grads"), "sum_small_grads")
    (t_cctx, g_ln_in_g, g_ln_in_b, g_wconv_full, g_a_log_f, g_dt_bias_f, g_a_log_b, g_dt_bias_b, g_dn_norm_g, g_wdw_full, g_conf_dw_b,
     g_conf_ln_g, g_conf_ln_b, g_ln1_g, g_ln1_b, g_b_mlp1, g_b_mlp2, g_ln2_g, g_ln2_b) = _unpack(ssum, small_shapes)
    sg = jax.nn.sigmoid(c_ctx)
    g_c_ctx = t_cctx * (sg * (1.0 + c_ctx * (1.0 - sg)))
    g_w_qkv_conv = lax.dynamic_slice(g_wconv_full, (0, chip * w_qkv_conv.shape[2]), w_qkv_conv.shape[1:])
    g_conf_dw_w = lax.dynamic_slice(g_wdw_full, (0, chip * conf_dw_w.shape[2]), conf_dw_w.shape[1:])

    proj_sums, proj_lands = _scatter_wait(ps_send, ps_recv, proj_sums, proj_lands, g_w_mod, "proj_grads_scatter_wait")
    proj_mine = [_sum_slots(own, rcv, slot_idx, f"proj_grads_chip_sum_{i}") for i, (own, rcv) in enumerate(zip(proj_sums, proj_lands))]
    g_win_t, g_w_out = _pair_join_halves(proj_mine, "proj_grads_pair_join")
    g_w_in = jnp.transpose(g_win_t)

    grads = dict(c_ctx=g_c_ctx, ln_in_g=g_ln_in_g, ln_in_b=g_ln_in_b, w_mod=g_w_mod[None], b_mod=g_b_mod, w_in=g_w_in[None],
                 w_qkv_conv=g_w_qkv_conv[None], a_log_f=g_a_log_f[None], dt_bias_f=g_dt_bias_f[None], a_log_b=g_a_log_b[None],
                 dt_bias_b=g_dt_bias_b[None], dn_norm_g=g_dn_norm_g[None], conf_dw_w=g_conf_dw_w[None], conf_dw_b=g_conf_dw_b[None],
                 conf_ln_g=g_conf_ln_g[None], conf_ln_b=g_conf_ln_b[None], w_out=g_w_out[None], ln1_g=g_ln1_g[None], ln1_b=g_ln1_b[None],
                 w_mlp1=g_w_mlp1[None], b_mlp1=g_b_mlp1[None], w_mlp2=g_w_mlp2[None], b_mlp2=g_b_mlp2[None], ln2_g=g_ln2_g[None],
                 ln2_b=g_ln2_b[None])
    weights = dict(c_ctx=c_ctx, ln_in_g=ln_in_g, ln_in_b=ln_in_b, w_mod=w_mod, b_mod=b_mod, w_in=w_in, w_qkv_conv=w_qkv_conv,
                   a_log_f=a_log_f, dt_bias_f=dt_bias_f, a_log_b=a_log_b, dt_bias_b=dt_bias_b, dn_norm_g=dn_norm_g, conf_dw_w=conf_dw_w,
                   conf_dw_b=conf_dw_b, conf_ln_g=conf_ln_g, conf_ln_b=conf_ln_b, w_out=w_out, ln1_g=ln1_g, ln1_b=ln1_b, w_mlp1=w_mlp1,
                   b_mlp1=b_mlp1, w_mlp2=w_mlp2, b_mlp2=b_mlp2, ln2_g=ln2_g, ln2_b=ln2_b)
    m_in = dict(c_ctx=m_c_ctx, ln_in_g=m_ln_in_g, ln_in_b=m_ln_in_b, w_mod=m_w_mod, b_mod=m_b_mod, w_in=m_w_in, w_qkv_conv=m_w_qkv_conv,
                a_log_f=m_a_log_f, dt_bias_f=m_dt_bias_f, a_log_b=m_a_log_b, dt_bias_b=m_dt_bias_b, dn_norm_g=m_dn_norm_g,
                conf_dw_w=m_conf_dw_w, conf_dw_b=m_conf_dw_b, conf_ln_g=m_conf_ln_g, conf_ln_b=m_conf_ln_b, w_out=m_w_out, ln1_g=m_ln1_g,
                ln1_b=m_ln1_b, w_mlp1=m_w_mlp1, b_mlp1=m_b_mlp1, w_mlp2=m_w_mlp2, b_mlp2=m_b_mlp2, ln2_g=m_ln2_g, ln2_b=m_ln2_b)
    v_in = dict(c_ctx=v_c_ctx, ln_in_g=v_ln_in_g, ln_in_b=v_ln_in_b, w_mod=v_w_mod, b_mod=v_b_mod, w_in=v_w_in, w_qkv_conv=v_w_qkv_conv,
                a_log_f=v_a_log_f, dt_bias_f=v_dt_bias_f, a_log_b=v_a_log_b, dt_bias_b=v_dt_bias_b, dn_norm_g=v_dn_norm_g,
                conf_dw_w=v_conf_dw_w, conf_dw_b=v_conf_dw_b, conf_ln_g=v_conf_ln_g, conf_ln_b=v_conf_ln_b, w_out=v_w_out, ln1_g=v_ln1_g,
                ln1_b=v_ln1_b, w_mlp1=v_w_mlp1, b_mlp1=v_b_mlp1, w_mlp2=v_w_mlp2, b_mlp2=v_b_mlp2, ln2_g=v_ln2_g, ln2_b=v_ln2_b)
    names = list(weights)
    big_names = ("w_mod", "w_in", "w_out", "w_mlp1", "w_mlp2")
    delta, new_m, new_v = {}, {}, {}
    for nm in big_names:
        w2d = weights[nm][0]
        dl, mm, vv = _adamw(w2d, grads[nm][0], m_in[nm][0], v_in[nm][0], f"adamw_{nm}")
        delta[nm], new_m[nm], new_v[nm] = dl[None], mm[None], vv[None]
    small_names = [nm for nm in names if nm not in big_names]
    shapes = [weights[nm].shape for nm in small_names]
    grads = {nm: (grads[nm].reshape(weights[nm].shape) if nm in small_names else grads[nm]) for nm in names}
    packed = [_pack([src[nm] for nm in small_names]) for src in (weights, grads, m_in, v_in)]
    dl, mm, vv = _adamw(*packed, "adamw_small")
    for nm, a, b_, c_ in zip(small_names, _unpack(dl, shapes), _unpack(mm, shapes), _unpack(vv, shapes)):
        delta[nm], new_m[nm], new_v[nm] = a, b_, c_

    loss = lax.psum(loss_acc[0, 0], MESH_AXES)
    return (loss, grad_x[None], *[grads[nm] for nm in names], *[delta[nm] for nm in names],
            *[new_m[nm] for nm in names], *[new_v[nm] for nm in names])
```

```python
import functools
import math

import jax
import jax.numpy as jnp
from jax import lax
from jax.experimental import pallas as pl
from jax.experimental.pallas import tpu as pltpu

F32 = jnp.float32
BF16 = jnp.bfloat16
MXU_DTYPE = BF16
WIRE_DTYPE = BF16
HIGHEST = lax.Precision.HIGHEST

HD = 128
CHUNK = 128
SHORT_CONV = 7
CONF_K = 31
ALPHA = 2.0 ** 0.25
LN_EPS = 1e-5
V7X_VMEM_BYTES = 64 * 1024 * 1024
LANES = 128

ADAM_LR = 0.001
ADAM_B1 = 0.9
ADAM_B2 = 0.999
ADAM_EPS = 1e-08
ADAM_WD = 0.01
ADAM_STEP = 10

MESH_AXES = ("x", "y", "c")
ANY = pl.BlockSpec(memory_space=pl.ANY)


def _pick(dim, pref, mult=LANES):
    best = None
    t = mult
    while t <= min(dim, pref):
        if dim % t == 0:
            best = t
        t += mult
    return best if best is not None else dim


def _cparams(sem, vmem_est):
    limit = int(min(V7X_VMEM_BYTES - 6 * 1024 * 1024, max(32 * 1024 * 1024, vmem_est + 8 * 1024 * 1024)))
    return pltpu.CompilerParams(dimension_semantics=sem, vmem_limit_bytes=limit)


def _nbytes(shape, dtype):
    return math.prod(shape) * jnp.dtype(dtype).itemsize


def _ln(x, g, b):
    mu = jnp.mean(x, axis=-1, keepdims=True)
    xc = x - mu
    var = jnp.mean(xc * xc, axis=-1, keepdims=True)
    return xc * lax.rsqrt(var + LN_EPS) * g + b


def _silu(x):
    return x * jax.nn.sigmoid(x)


def _softplus(x):
    return jnp.maximum(x, 0.0) + jnp.log1p(jnp.exp(-jnp.abs(x)))


_DIMS = {"nn": (((1,), (0,)), ((), ())), "nt": (((1,), (1,)), ((), ())), "tn": (((0,), (0,)), ((), ()))}


def _split_bf16(x, parts):
    out, rest = [], x.astype(F32)
    for _ in range(parts):
        bits = lax.bitcast_convert_type(rest, jnp.uint32) & jnp.uint32(0xFFFF0000)
        p = lax.bitcast_convert_type(bits, F32)
        out.append(p.astype(BF16))
        rest = rest - p
    return out


def _raw_dot(a, b, form, kind):
    if MXU_DTYPE != F32:
        if kind == "mxu":
            return lax.dot_general(a.astype(MXU_DTYPE), b.astype(MXU_DTYPE), _DIMS[form], preferred_element_type=F32)
        if kind == "hi3":
            (a0, a1), (b0, b1) = _split_bf16(a, 2), _split_bf16(b, 2)
            d = lambda p, q: lax.dot_general(p, q, _DIMS[form], preferred_element_type=F32)
            return d(a0, b0) + (d(a1, b0) + d(a0, b1))
    return lax.dot_general(a.astype(F32), b.astype(F32), _DIMS[form], precision=HIGHEST, preferred_element_type=F32)


@functools.partial(jax.custom_vjp, nondiff_argnums=(2, 3))
def _dot(a, b, form, kind):
    return _raw_dot(a, b, form, kind)


def _dot_fwd(a, b, form, kind):
    return _raw_dot(a, b, form, kind), (a, b)


def _dot_bwd(form, kind, res, dc):
    a, b = res
    if form == "nn":
        return _dot(dc, b, "nt", kind), _dot(a, dc, "tn", kind)
    if form == "nt":
        return _dot(dc, b, "nn", kind), _dot(dc, a, "tn", kind)
    return _dot(b, dc, "nt", kind), _dot(a, dc, "nn", kind)


_dot.defvjp(_dot_fwd, _dot_bwd)


def _raw_xdot(a, b, form, exact):
    if MXU_DTYPE == F32:
        return _raw_dot(a, b, form, "hi")
    d = lambda p, q: lax.dot_general(p, q, _DIMS[form], preferred_element_type=F32)
    if exact == "a":
        a16, (b0, b1, b2) = a.astype(BF16), _split_bf16(b, 3)
        return d(a16, b0) + (d(a16, b1) + d(a16, b2))
    b16, (a0, a1, a2) = b.astype(BF16), _split_bf16(a, 3)
    return d(a0, b16) + (d(a1, b16) + d(a2, b16))


@functools.partial(jax.custom_vjp, nondiff_argnums=(2, 3))
def _xdot(a, b, form, exact):
    return _raw_xdot(a, b, form, exact)


def _xdot_fwd(a, b, form, exact):
    return _raw_xdot(a, b, form, exact), (a, b)


def _xdot_bwd(form, exact, res, dc):
    a, b = res
    if form == "nn" and exact == "a":
        return jnp.zeros_like(a), _xdot(a, dc, "tn", "a")
    assert form == "tn" and exact == "b"
    return _xdot(b, dc, "nt", "a"), jnp.zeros_like(b)


_xdot.defvjp(_xdot_fwd, _xdot_bwd)


def _spec2(block, idx, split=None):
    if split is None:
        return pl.BlockSpec(tuple(block), idx)
    per = split // block[1]

    def idx3(*g):
        r, cblk = idx(*g)
        return (cblk // per, r, cblk % per)

    return pl.BlockSpec((None,) + tuple(block), idx3)


def _matmul(a, b, *, form, m, n, k, tm, tn, tk, out_dtype, name, bias=None, a_fn=None, b_fn=None,
            epi=None, epi_in=None, colsum=False, kind="mxu", b_split=None, o_split=None):
    assert m % tm == 0 and n % tn == 0 and k % tk == 0, (name, m, n, k, tm, tn, tk)
    nk = k // tk
    grid = (n // tn, m // tm, nk)
    if form == "tn":
        a_spec = pl.BlockSpec((tk, tm), lambda j, i, kk: (kk, i))
    else:
        a_spec = pl.BlockSpec((tm, tk), lambda j, i, kk: (i, kk))
    if form == "nt":
        b_spec = _spec2((tn, tk), lambda j, i, kk: (j, kk), b_split)
    else:
        b_spec = _spec2((tk, tn), lambda j, i, kk: (kk, j), b_split)
    in_specs = [a_spec, b_spec]
    operands = [a, b]
    if bias is not None:
        in_specs.append(pl.BlockSpec((1, tn), lambda j, i, kk: (0, j)))
        operands.append(bias)
    if epi_in is not None:
        in_specs.append(pl.BlockSpec((tm, tn), lambda j, i, kk: (i, j)))
        operands.append(epi_in)
    out_specs = [_spec2((tm, tn), lambda j, i, kk: (i, j), o_split)]
    if o_split is None:
        out_shape = [jax.ShapeDtypeStruct((m, n), out_dtype)]
    else:
        out_shape = [jax.ShapeDtypeStruct((n // o_split, m, o_split), out_dtype)]
    if colsum:
        out_specs.append(pl.BlockSpec((1, tn), lambda j, i, kk: (0, j)))
        out_shape.append(jax.ShapeDtypeStruct((1, n), F32))
    has_bias, has_epi_in = bias is not None, epi_in is not None

    def body(*refs):
        refs = list(refs)
        a_ref, b_ref = refs[0], refs[1]
        pos = 2
        bias_ref = epi_ref = cs_ref = None
        if has_bias:
            bias_ref = refs[pos]
            pos += 1
        if has_epi_in:
            epi_ref = refs[pos]
            pos += 1
        o_ref = refs[pos]
        pos += 1
        if colsum:
            cs_ref = refs[pos]
            pos += 1
        acc_ref = refs[pos] if nk > 1 else None
        i, kk = pl.program_id(1), pl.program_id(2)

        av = a_ref[...]
        if a_fn is not None:
            av = a_fn(av)
        bv = b_ref[...]
        if b_fn is not None:
            bv = b_fn(bv)
        prod = _raw_dot(av, bv, form, kind)

        if nk > 1:
            @pl.when(kk == 0)
            def _():
                acc_ref[...] = prod

            @pl.when(jnp.logical_and(kk > 0, kk < nk - 1))
            def _():
                acc_ref[...] += prod

        @pl.when(kk == nk - 1)
        def _():
            r = acc_ref[...] + prod if nk > 1 else prod
            if has_bias:
                r = r + bias_ref[...]
            if epi is not None:
                r = epi(r, epi_ref[...]) if has_epi_in else epi(r)
            o_ref[...] = r.astype(out_dtype)
            if colsum:
                s = jnp.sum(r, axis=0, keepdims=True)

                @pl.when(i == 0)
                def _():
                    cs_ref[...] = s

                @pl.when(i > 0)
                def _():
                    cs_ref[...] += s

    est = 2 * (_nbytes((tm, tk), a.dtype) + _nbytes((tk, tn), b.dtype) + _nbytes((tm, tn), out_dtype))
    est += _nbytes((tm, tn), F32) * 2 + (2 * _nbytes((tm, tn), epi_in.dtype) if has_epi_in else 0)
    res = pl.pallas_call(
        body, grid=grid, in_specs=in_specs, out_specs=out_specs, out_shape=out_shape,
        scratch_shapes=[pltpu.VMEM((tm, tn), F32)] if nk > 1 else [], name=name,
        compiler_params=_cparams(("arbitrary", "arbitrary", "arbitrary"), est),
    )(*operands)
    return res if colsum else res[0]


def _rowwise(fn, rows, params, row_outs, acc_outs, *, n_rows, tm, name):
    assert n_rows % tm == 0, (name, n_rows, tm)
    nr, npar, nro, nac = len(rows), len(params), len(row_outs), len(acc_outs)
    in_specs = [pl.BlockSpec((tm, w), functools.partial(lambda i, cb, ro: (i + ro, cb), cb=cb, ro=ro))
                for (_, w, cb, ro) in rows]
    in_specs += [pl.BlockSpec((1, p.shape[1]), lambda i: (0, 0)) for p in params]
    out_specs = [pl.BlockSpec((tm, w), lambda i: (i, 0)) for (w, _) in row_outs]
    out_specs += [pl.BlockSpec((1, w), lambda i: (0, 0)) for w in acc_outs]
    out_shape = [jax.ShapeDtypeStruct((n_rows, w), dt) for (w, dt) in row_outs]
    out_shape += [jax.ShapeDtypeStruct((1, w), F32) for w in acc_outs]

    def body(*refs):
        rv = [r[...] for r in refs[:nr]]
        pv = [r[...] for r in refs[nr:nr + npar]]
        ro_refs = refs[nr + npar:nr + npar + nro]
        ac_refs = refs[nr + npar + nro:]
        ro, ac = fn(rv, pv)
        for ref, val in zip(ro_refs, ro, strict=True):
            ref[...] = val.astype(ref.dtype)
        first = pl.program_id(0) == 0
        for ref, val in zip(ac_refs, ac, strict=True):
            val = jnp.broadcast_to(val.astype(F32), ref.shape)

            @pl.when(first)
            def _(ref=ref, val=val):
                ref[...] = val

            @pl.when(jnp.logical_not(first))
            def _(ref=ref, val=val):
                ref[...] += val

    io = sum(_nbytes((tm, w), a.dtype) for (a, w, _, _) in rows) + sum(_nbytes((tm, w), dt) for (w, dt) in row_outs)
    widest = max([w for (_, w, _, _) in rows] + [w for (w, _) in row_outs])
    est = 2 * io + 12 * _nbytes((tm, widest), F32)
    return pl.pallas_call(
        body, grid=(n_rows // tm,), in_specs=in_specs, out_specs=out_specs, out_shape=out_shape, name=name,
        compiler_params=_cparams(("arbitrary",), est),
    )(*[a for (a, _, _, _) in rows], *params)


def _conv7_tiles(n_lat, n_ctx):
    tt = min(256, n_ctx)
    assert n_lat % tt == 0 and n_ctx % tt == 0
    return tt, [(t0, 8 + t0) for t0 in range(0, n_lat, tt)] + [(n_lat + t0, 16 + n_lat + t0) for t0 in range(0, n_ctx, tt)]


def _conv7_fill(p_ref, x_ref, n_lat, n_ctx):
    z8 = jnp.zeros((8, LANES), F32)
    p_ref[0:8, :] = z8
    p_ref[8:8 + n_lat, :] = x_ref[0:n_lat, :]
    p_ref[8 + n_lat:16 + n_lat, :] = z8
    p_ref[16 + n_lat:16 + n_lat + n_ctx, :] = x_ref[n_lat:n_lat + n_ctx, :]
    p_ref[16 + n_lat + n_ctx:24 + n_lat + n_ctx, :] = z8


def _conv7_fwd(h, w_conv, *, n_lat, n_ctx, nh):
    nt = n_lat + n_ctx
    tt, tiles = _conv7_tiles(n_lat, n_ctx)
    half = SHORT_CONV // 2

    def body(x_ref, w_ref, o_ref, p_ref):
        _conv7_fill(p_ref, x_ref, n_lat, n_ctx)
        for (row, prow) in tiles:
            acc = jnp.zeros((tt, LANES), F32)
            for kk in range(SHORT_CONV):
                acc = acc + w_ref[kk:kk + 1, :] * p_ref[pl.ds(prow + kk - half, tt), :]
            o_ref[pl.ds(row, tt), :] = _silu(acc)

    return pl.pallas_call(
        body, grid=(3 * nh,),
        in_specs=[pl.BlockSpec((nt, LANES), lambda j: (0, j)), pl.BlockSpec((SHORT_CONV, LANES), lambda j: (0, j))],
        out_specs=pl.BlockSpec((None, None, nt, LANES), lambda j: (j // nh, j % nh, 0, 0)),
        out_shape=jax.ShapeDtypeStruct((3, nh, nt, LANES), F32),
        scratch_shapes=[pltpu.VMEM((nt + 24, LANES), F32)], name="conv7_fwd",
        compiler_params=_cparams(("arbitrary",), 5 * _nbytes((nt + 24, LANES), F32)),
    )(h, w_conv)


def _conv7_bwd(h, w_conv, dact_f, dact_b, *, n_lat, n_ctx, nh):
    nt = n_lat + n_ctx
    tt, tiles = _conv7_tiles(n_lat, n_ctx)
    half = SHORT_CONV // 2

    def body(x_ref, w_ref, df_ref, db_ref, dx_ref, dw_ref, p_ref, q_ref):
        _conv7_fill(p_ref, x_ref, n_lat, n_ctx)
        z8 = jnp.zeros((8, LANES), F32)
        q_ref[0:8, :] = z8
        q_ref[8 + n_lat:16 + n_lat, :] = z8
        q_ref[16 + n_lat + n_ctx:24 + n_lat + n_ctx, :] = z8
        dw = [jnp.zeros((1, LANES), F32) for _ in range(SHORT_CONV)]
        for (row, prow) in tiles:
            pre = jnp.zeros((tt, LANES), F32)
            for kk in range(SHORT_CONV):
                pre = pre + w_ref[kk:kk + 1, :] * p_ref[pl.ds(prow + kk - half, tt), :]
            s = jax.nn.sigmoid(pre)
            dpre = (df_ref[pl.ds(row, tt), :] + db_ref[pl.ds(row, tt), :]) * (s * (1.0 + pre * (1.0 - s)))
            q_ref[pl.ds(prow, tt), :] = dpre
            for kk in range(SHORT_CONV):
                dw[kk] = dw[kk] + jnp.sum(dpre * p_ref[pl.ds(prow + kk - half, tt), :], axis=0, keepdims=True)
        for (row, prow) in tiles:
            acc = jnp.zeros((tt, LANES), F32)
            for kk in range(SHORT_CONV):
                acc = acc + w_ref[kk:kk + 1, :] * q_ref[pl.ds(prow + half - kk, tt), :]
            dx_ref[pl.ds(row, tt), :] = acc.astype(dx_ref.dtype)
        for kk in range(SHORT_CONV):
            dw_ref[kk:kk + 1, :] = dw[kk]

    dspec = pl.BlockSpec((None, None, nt, LANES), lambda j: (j // nh, j % nh, 0, 0))
    return pl.pallas_call(
        body, grid=(3 * nh,),
        in_specs=[pl.BlockSpec((nt, LANES), lambda j: (0, j)), pl.BlockSpec((SHORT_CONV, LANES), lambda j: (0, j)), dspec, dspec],
        out_specs=[pl.BlockSpec((nt, LANES), lambda j: (0, j)), pl.BlockSpec((SHORT_CONV, LANES), lambda j: (0, j))],
        out_shape=[jax.ShapeDtypeStruct((nt, 3 * nh * LANES), MXU_DTYPE), jax.ShapeDtypeStruct((SHORT_CONV, 3 * nh * LANES), F32)],
        scratch_shapes=[pltpu.VMEM((nt + 24, LANES), F32), pltpu.VMEM((nt + 24, LANES), F32)], name="conv7_bwd",
        compiler_params=_cparams(("arbitrary",), 10 * _nbytes((nt + 24, LANES), F32)),
    )(h, w_conv, dact_f, dact_b)


_RT = 4
_CP = CONF_K // 2


def _conf_pad_shape(gw, vertical):
    return (gw + 2 * _CP, gw, LANES) if vertical else (gw, gw + 32, LANES)


def _conf_fill(p_ref, val, gw, vertical):
    if vertical:
        p_ref[0:_CP] = jnp.zeros((_CP, gw, LANES), F32)
        p_ref[_CP + gw:2 * _CP + gw] = jnp.zeros((_CP, gw, LANES), F32)
        p_ref[_CP:_CP + gw] = val
    else:
        p_ref[:, 0:16, :] = jnp.zeros((gw, 16, LANES), F32)
        p_ref[:, 16 + gw:32 + gw, :] = jnp.zeros((gw, 16, LANES), F32)
        p_ref[:, 16:16 + gw, :] = val


def _conf_window(p_ref, r0, shift, gw, vertical):
    if vertical:
        return p_ref[pl.ds(r0 + _CP + shift, _RT), :, :]
    return p_ref[pl.ds(r0, _RT), pl.ds(16 + shift, gw), :]


def _conf_conv_fwd(h3, w_dw, *, gw, col0, half, vertical):
    cb_val = (col0 + (half if vertical else 0)) // LANES
    cb_gate = cb_val + 2 * half // LANES
    cb_w = (half if vertical else 0) // LANES

    def body(v_ref, g_ref, w_ref, o_ref, p_ref):
        _conf_fill(p_ref, v_ref[...] * jax.nn.sigmoid(g_ref[...]), gw, vertical)

        def step(t, carry):
            r0 = t * _RT
            acc = jnp.zeros((_RT, gw, LANES), F32)
            for kk in range(CONF_K):
                acc = acc + w_ref[kk:kk + 1, :] * _conf_window(p_ref, r0, kk - _CP, gw, vertical)
            o_ref[pl.ds(r0, _RT)] = acc
            return carry

        lax.fori_loop(0, gw // _RT, step, 0)

    blk = (gw, gw, LANES)
    return pl.pallas_call(
        body, grid=(half // LANES,),
        in_specs=[pl.BlockSpec(blk, lambda j: (0, 0, cb_val + j)), pl.BlockSpec(blk, lambda j: (0, 0, cb_gate + j)),
                  pl.BlockSpec((CONF_K, LANES), lambda j: (0, cb_w + j))],
        out_specs=pl.BlockSpec(blk, lambda j: (0, 0, j)),
        out_shape=jax.ShapeDtypeStruct((gw, gw, half), F32),
        scratch_shapes=[pltpu.VMEM(_conf_pad_shape(gw, vertical), F32)],
        name="conf_conv_fwd_v" if vertical else "conf_conv_fwd_h",
        compiler_params=_cparams(("arbitrary",), 8 * _nbytes(_conf_pad_shape(gw, vertical), F32)),
    )(h3, h3, w_dw)


def _conf_conv_bwd(h3, w_dw, dyc, *, gw, col0, half, vertical):
    cb_val = (col0 + (half if vertical else 0)) // LANES
    cb_gate = cb_val + 2 * half // LANES
    cb_w = (half if vertical else 0) // LANES

    def body(v_ref, g_ref, w_ref, d_ref, dv_ref, dg_ref, dw_ref, py_ref, pd_ref):
        _conf_fill(py_ref, v_ref[...] * jax.nn.sigmoid(g_ref[...]), gw, vertical)
        _conf_fill(pd_ref, d_ref[...], gw, vertical)

        def step(t, carry):
            r0 = t * _RT
            acc = jnp.zeros((_RT, gw, LANES), F32)
            for kk in range(CONF_K):
                acc = acc + w_ref[kk:kk + 1, :] * _conf_window(pd_ref, r0, _CP - kk, gw, vertical)
            val = v_ref[pl.ds(r0, _RT)]
            sg = jax.nn.sigmoid(g_ref[pl.ds(r0, _RT)])
            dv_ref[pl.ds(r0, _RT)] = (acc * sg).astype(dv_ref.dtype)
            dg_ref[pl.ds(r0, _RT)] = (acc * val * sg * (1.0 - sg)).astype(dg_ref.dtype)
            return carry

        lax.fori_loop(0, gw // _RT, step, 0)

        for kk in range(CONF_K):
            def wstep(t, acc, kk=kk):
                r0 = t * _RT
                prod = d_ref[pl.ds(r0, _RT)] * _conf_window(py_ref, r0, kk - _CP, gw, vertical)
                return acc + jnp.sum(prod, axis=0)

            tot = lax.fori_loop(0, gw // _RT, wstep, jnp.zeros((gw, LANES), F32))
            dw_ref[kk:kk + 1, :] = jnp.sum(tot, axis=0, keepdims=True)

    blk = (gw, gw, LANES)
    pshape = _conf_pad_shape(gw, vertical)
    return pl.pallas_call(
        body, grid=(half // LANES,),
        in_specs=[pl.BlockSpec(blk, lambda j: (0, 0, cb_val + j)), pl.BlockSpec(blk, lambda j: (0, 0, cb_gate + j)),
                  pl.BlockSpec((CONF_K, LANES), lambda j: (0, cb_w + j)), pl.BlockSpec(blk, lambda j: (0, 0, j))],
        out_specs=[pl.BlockSpec(blk, lambda j: (0, 0, j)), pl.BlockSpec(blk, lambda j: (0, 0, j)),
                   pl.BlockSpec((CONF_K, LANES), lambda j: (0, j))],
        out_shape=[jax.ShapeDtypeStruct((gw, gw, half), MXU_DTYPE), jax.ShapeDtypeStruct((gw, gw, half), MXU_DTYPE),
                   jax.ShapeDtypeStruct((CONF_K, half), F32)],
        scratch_shapes=[pltpu.VMEM(pshape, F32), pltpu.VMEM(pshape, F32)],
        name="conf_conv_bwd_v" if vertical else "conf_conv_bwd_h",
        compiler_params=_cparams(("arbitrary",), 12 * _nbytes(pshape, F32)),
    )(h3, h3, w_dw, dyc)


_INV_BASE = 8


def _tri_inv_raw(mats):
    c = mats[0].shape[0]
    ri = lax.broadcasted_iota(jnp.int32, (c, c), 0)
    ci = lax.broadcasted_iota(jnp.int32, (c, c), 1)
    eye = (ri == ci).astype(F32)
    base = min(_INV_BASE, c)
    same = lambda size: (ri // size) == (ci // size)
    dmat = [jnp.where(same(base), a, 0.0) for a in mats]
    x = [eye - dm for dm in dmat]
    pw = [_raw_dot(dm, dm, "nn", "hi3") for dm in dmat]
    span = 2
    while span < base:
        x = [xi + _raw_dot(xi, p, "nn", "hi3") for xi, p in zip(x, pw)]
        span *= 2
        if span < base:
            pw = [_raw_dot(p, p, "nn", "hi3") for p in pw]
    size = base
    while size < c:
        off = jnp.logical_and(jnp.logical_not(same(size)), same(2 * size))
        t = [_raw_dot(xi, jnp.where(off, a, 0.0), "nn", "hi3") for xi, a in zip(x, mats)]
        x = [xi - _raw_dot(ti, xi, "nn", "hi3") for xi, ti in zip(x, t)]
        size *= 2
    return x


@jax.custom_vjp
def _tri_inv(mats):
    return _tri_inv_raw(mats)


def _tri_inv_fwd(mats):
    x = _tri_inv_raw(mats)
    return x, x


def _tri_inv_bwd(x, dx):
    t = [_raw_dot(xi, di, "tn", "hi3") for xi, di in zip(x, dx)]
    return ([-_raw_dot(ti, xi, "nt", "hi3") for ti, xi in zip(t, x)],)


_tri_inv.defvjp(_tri_inv_fwd, _tri_inv_bwd)


@jax.custom_vjp
def _tri_inv_saved(mats, saved):
    return saved


def _tri_inv_saved_fwd(mats, saved):
    return saved, saved


def _tri_inv_saved_bwd(x, dx):
    return _tri_inv_bwd(x, dx)[0], [jnp.zeros_like(xi) for xi in x]


_tri_inv_saved.defvjp(_tri_inv_saved_fwd, _tri_inv_saved_bwd)


def _delta_chunk(qa, ka, va, graw, braw, alog, dtb, s, *, rev, minv_saved=None, with_minv=False):
    c = qa[0].shape[0]

    def each(f, *lists):
        return [f(*xs) for xs in zip(*lists, strict=True)]

    def l2n(t):
        return t * lax.rsqrt(jnp.sum(t * t, axis=-1, keepdims=True) + 1e-6)

    ri = lax.broadcasted_iota(jnp.int32, (c, c), 0)
    ci = lax.broadcasted_iota(jnp.int32, (c, c), 1)
    incl = (ci >= ri) if rev else (ci <= ri)
    strict = (ci > ri) if rev else (ci < ri)
    tmat = incl.astype(F32)
    tmat_t = ((ri >= ci) if rev else (ri <= ci)).astype(F32)

    q = each(lambda t: l2n(t) * (HD ** -0.5), qa)
    k = each(l2n, ka)
    g = each(lambda al, gr, dt: -jnp.exp(al) * _softplus(gr + dt), alog, graw, dtb)
    beta = each(jax.nn.sigmoid, braw)
    gc_wide = each(lambda t: _xdot(tmat, jnp.broadcast_to(t, (c, HD)), "nn", "a"), g)
    gc_rows = gc_wide if c == HD else each(lambda t: _xdot(tmat, jnp.broadcast_to(t, (c, c)), "nn", "a"), g)
    gc_cols = each(lambda t: _xdot(jnp.broadcast_to(t, (c, c)), tmat_t, "tn", "b"), g)
    gamma = each(lambda r, cc: jnp.where(incl, jnp.exp(jnp.where(incl, r - cc, 0.0)), 0.0), gc_rows, gc_cols)
    kb = each(lambda a, b: a * b, k, beta)
    a_mat = each(lambda a, b, gm: jnp.where(strict, _dot(a, b, "nt", "mxu") * gm, 0.0), kb, k, gamma)
    minv = _tri_inv(a_mat) if minv_saved is None else _tri_inv_saved(a_mat, minv_saved)
    eg = each(jnp.exp, gc_wide)
    u = each(lambda mi, v, b: _dot(mi, v * b, "nn", "hi3"), minv, va, beta)
    w = each(lambda mi, a, e: _dot(mi, a * e, "nn", "hi3"), minv, kb, eg)
    attn = each(lambda a, b, gm: _dot(a, b, "nt", "mxu") * gm, q, k, gamma)
    tot = each(lambda t: jnp.sum(t, axis=0, keepdims=True), g)
    q_dec = each(lambda a, e: a * e, q, eg)
    k_dec = each(lambda a, t, gw_: a * jnp.exp(t - gw_), k, tot, gc_wide)
    v_new = each(lambda uu, ww, ss: uu - _dot(ww, ss, "nn", "mxu"), u, w, s)
    o = each(lambda qd, ss, at, vn: _dot(qd, ss, "nn", "mxu") + _dot(at, vn, "nn", "mxu"), q_dec, s, attn, v_new)
    s_new = each(lambda ss, t, kd, vn: ss * jnp.exp(t) + _dot(kd, vn, "tn", "mxu"), s, tot, k_dec, v_new)
    return (o, s_new, minv) if with_minv else (o, s_new)


def _delta_args(a_ref, g_ref, al_ref, dt_ref, s_ref, hb, gcol):
    gts = [g_ref[hh] for hh in range(hb)]
    return ([a_ref[0, hh] for hh in range(hb)], [a_ref[1, hh] for hh in range(hb)], [a_ref[2, hh] for hh in range(hb)],
            [t[:, gcol:gcol + 1] for t in gts], [t[:, gcol + 1:gcol + 2] for t in gts],
            [al_ref[hh, 0:1, 0:1] for hh in range(hb)], [dt_ref[hh, 0:1, 0:1] for hh in range(hb)],
            [s_ref[hh] for hh in range(hb)])


def _delta_chunk_of_step(step, nch, nlc, rev):
    return (nch - 1 - step) if rev else (step + nlc) % nch


def _delta_fwd(act, gates_hm, alog_b, dtb_b, *, n_lat, n_ctx, nh, hb, rev):
    nt = n_lat + n_ctx
    nch, nlc = nt // CHUNK, n_lat // CHUNK
    gcol = 2 if rev else 0
    chunk = functools.partial(_delta_chunk_of_step, nch=nch, nlc=nlc, rev=rev)

    def body(a_ref, g_ref, al_ref, dt_ref, o_ref, st_ref, mi_ref, s_ref):
        n = pl.program_id(1)

        @pl.when(n == 0)
        def _():
            s_ref[...] = jnp.zeros_like(s_ref)

        args = _delta_args(a_ref, g_ref, al_ref, dt_ref, s_ref, hb, gcol)
        o, s_new, minv = _delta_chunk(*args, rev=rev, with_minv=True)
        for hh in range(hb):
            st_ref[hh] = args[7][hh]
            mi_ref[hh] = minv[hh]
            o_ref[:, hh * HD:(hh + 1) * HD] = o[hh]
            s_ref[hh] = s_new[hh]

    par = pl.BlockSpec((hb, 8, LANES), lambda h, n: (h, 0, 0))
    return pl.pallas_call(
        body, grid=(nh // hb, nch),
        in_specs=[pl.BlockSpec((3, hb, CHUNK, HD), lambda h, n: (0, h, chunk(n), 0)),
                  pl.BlockSpec((hb, CHUNK, 4), lambda h, n: (h, chunk(n), 0)), par, par],
        out_specs=[pl.BlockSpec((CHUNK, hb * HD), lambda h, n: (chunk(n), h)),
                   pl.BlockSpec((hb, None, HD, HD), lambda h, n: (h, n, 0, 0)),
                   pl.BlockSpec((hb, None, CHUNK, CHUNK), lambda h, n: (h, n, 0, 0))],
        out_shape=[jax.ShapeDtypeStruct((nt, nh * HD), F32), jax.ShapeDtypeStruct((nh, nch, HD, HD), F32),
                   jax.ShapeDtypeStruct((nh, nch, CHUNK, CHUNK), F32)],
        scratch_shapes=[pltpu.VMEM((hb, HD, HD), F32)], name="delta_fwd_rev" if rev else "delta_fwd",
        compiler_params=_cparams(("arbitrary", "arbitrary"), 0),
    )(act, gates_hm, alog_b, dtb_b)


def _delta_bwd(act, gates_hm, alog_b, dtb_b, states, minvs, do, *, n_lat, n_ctx, nh, hb, rev):
    nt = n_lat + n_ctx
    nch, nlc = nt // CHUNK, n_lat // CHUNK
    gcol = 2 if rev else 0

    def chunk(m):
        return _delta_chunk_of_step(nch - 1 - m, nch, nlc, rev)

    def body(a_ref, g_ref, al_ref, dt_ref, st_ref, mi_ref, do_ref, da_ref, dg_ref, dal_ref, ddt_ref, ds_ref):
        m = pl.program_id(1)

        @pl.when(m == 0)
        def _():
            ds_ref[...] = jnp.zeros_like(ds_ref)
            dal_ref[...] = jnp.zeros_like(dal_ref)
            ddt_ref[...] = jnp.zeros_like(ddt_ref)

        is_lat = chunk(m) < nlc
        args = _delta_args(a_ref, g_ref, al_ref, dt_ref, st_ref, hb, gcol)
        _, vjp = jax.vjp(functools.partial(_delta_chunk, rev=rev, minv_saved=[mi_ref[hh] for hh in range(hb)]), *args)
        do_h = [jnp.where(is_lat, do_ref[:, hh * HD:(hh + 1) * HD], 0.0) for hh in range(hb)]
        grads = vjp((do_h, [ds_ref[hh] for hh in range(hb)]))
        for hh in range(hb):
            dq, dk, dv, dgr, dbr, dal, ddt, ds = [gr[hh] for gr in grads]
            da_ref[0, hh] = dq
            da_ref[1, hh] = dk
            da_ref[2, hh] = dv
            lane = lax.broadcasted_iota(jnp.int32, (CHUNK, 4), 1)
            dg_ref[hh] = jnp.where(lane == 0, dgr, jnp.where(lane == 1, dbr, 0.0))
            dal_ref[hh] += jnp.broadcast_to(dal, (8, LANES))
            ddt_ref[hh] += jnp.broadcast_to(ddt, (8, LANES))
            ds_ref[hh] = ds

    par = pl.BlockSpec((hb, 8, LANES), lambda h, m: (h, 0, 0))
    return pl.pallas_call(
        body, grid=(nh // hb, nch),
        in_specs=[pl.BlockSpec((3, hb, CHUNK, HD), lambda h, m: (0, h, chunk(m), 0)),
                  pl.BlockSpec((hb, CHUNK, 4), lambda h, m: (h, chunk(m), 0)), par, par,
                  pl.BlockSpec((hb, None, HD, HD), lambda h, m: (h, nch - 1 - m, 0, 0)),
                  pl.BlockSpec((hb, None, CHUNK, CHUNK), lambda h, m: (h, nch - 1 - m, 0, 0)),
                  pl.BlockSpec((CHUNK, hb * HD), lambda h, m: (jnp.minimum(chunk(m), nlc - 1), h))],
        out_specs=[pl.BlockSpec((3, hb, CHUNK, HD), lambda h, m: (0, h, chunk(m), 0)),
                   pl.BlockSpec((hb, CHUNK, 4), lambda h, m: (h, chunk(m), 0)), par, par],
        out_shape=[jax.ShapeDtypeStruct((3, nh, nt, HD), F32), jax.ShapeDtypeStruct((nh, nt, 4), F32),
                   jax.ShapeDtypeStruct((nh, 8, LANES), F32), jax.ShapeDtypeStruct((nh, 8, LANES), F32)],
        scratch_shapes=[pltpu.VMEM((hb, HD, HD), F32)], name="delta_bwd_rev" if rev else "delta_bwd",
        compiler_params=_cparams(("arbitrary", "arbitrary"), 0),
    )(act, gates_hm, alog_b, dtb_b, states, minvs, do)


def _my_pos():
    return lax.axis_index("x"), lax.axis_index("y"), lax.axis_index("c")


def _flip(v, d):
    return 1 - v if d else v


def _bcast8(block, name):
    r, w = block.shape
    assert r % 8 == 0 and w % LANES == 0

    def body(x_ref, o_ref, send_sems, recv_sems, local_sem):
        x, y, c = _my_pos()

        def rows(px, py, pc):
            return o_ref.at[pl.ds((4 * px + 2 * py + pc) * r, r), :]

        mine = pltpu.make_async_copy(x_ref, rows(x, y, c), local_sem)
        mine.start()
        copies = []
        for kk in range(1, 8):
            dx, dy, dc = (kk >> 2) & 1, (kk >> 1) & 1, kk & 1
            peer = (_flip(x, dx), _flip(y, dy), _flip(c, dc))
            cp = pltpu.make_async_remote_copy(src_ref=x_ref, dst_ref=rows(x, y, c), send_sem=send_sems.at[kk - 1],
                                              recv_sem=recv_sems.at[kk - 1], device_id=peer, device_id_type=pl.DeviceIdType.MESH)
            cp.start()
            copies.append((cp, peer))
        for kk, (cp, peer) in enumerate(copies):
            pltpu.make_async_remote_copy(src_ref=x_ref, dst_ref=rows(*peer), send_sem=send_sems.at[kk], recv_sem=recv_sems.at[kk],
                                         device_id=peer, device_id_type=pl.DeviceIdType.MESH).wait_recv()
        for cp, _ in copies:
            cp.wait_send()
        mine.wait()

    return pl.pallas_call(
        body, out_shape=jax.ShapeDtypeStruct((8 * r, w), block.dtype),
        in_specs=[pl.BlockSpec(memory_space=pltpu.VMEM)], out_specs=pl.BlockSpec(memory_space=pltpu.VMEM),
        scratch_shapes=[pltpu.SemaphoreType.DMA((7,)), pltpu.SemaphoreType.DMA((7,)), pltpu.SemaphoreType.DMA], name=name,
        compiler_params=pltpu.CompilerParams(vmem_limit_bytes=int(max(32 * 1024 * 1024, 12 * _nbytes((r, w), block.dtype)))),
    )(block)


_CHIP_PEERS = ((1, 0), (0, 1), (1, 1))


def _col_half(ref, c, lead):
    hc = ref.shape[-1] // 2
    return ref.at[(*lead, slice(None), pl.ds(pl.multiple_of(c * hc, LANES), hc))]


def _into_slot(w, chip_idx, after, name):
    r, cc = w.shape
    tr = _pick(r, 512, 16)

    def body(i_ref, w_ref, after_ref, o_ref):
        o_ref[...] = w_ref[...].astype(o_ref.dtype)

    grid_spec = pltpu.PrefetchScalarGridSpec(
        num_scalar_prefetch=1, grid=(r // tr,), in_specs=[pl.BlockSpec((tr, cc), lambda i, s: (i, 0)), ANY],
        out_specs=pl.BlockSpec((None, tr, cc), lambda i, s: (s[0], i, 0)))
    return pl.pallas_call(body, grid_spec=grid_spec, out_shape=jax.ShapeDtypeStruct((4, r, cc), MXU_DTYPE), name=name,
                          compiler_params=_cparams(("arbitrary",), 6 * _nbytes((tr, cc), F32)))(chip_idx, w, after)


_HBM = pl.BlockSpec(memory_space=pltpu.HBM)
_SEM = pl.BlockSpec(memory_space=pltpu.SEMAPHORE)
_DATAFLOW = pltpu.SideEffectType.DATAFLOW_SIDE_EFFECTING


def _in_hbm(a):
    return pltpu.with_memory_space_constraint(a, pltpu.HBM)


def _gather_start(bufs, after, name):
    n = len(bufs)

    def body(*refs):
        ins, send_sems, recv_sems, token = refs[:n], refs[n + 1], refs[n + 2], refs[-1]
        x, y, c = _my_pos()
        for a in range(n):
            piece = _col_half(ins[a], c, (2 * x + y,))
            for kk, (dx, dy) in enumerate(_CHIP_PEERS):
                pltpu.make_async_remote_copy(src_ref=piece, dst_ref=piece, send_sem=send_sems.at[3 * a + kk], recv_sem=recv_sems.at[3 * a + kk],
                                             device_id=(_flip(x, dx), _flip(y, dy), c), device_id_type=pl.DeviceIdType.MESH).start()
        token[...] = jnp.zeros_like(token)

    res = pl.pallas_call(
        body, name=name,
        out_shape=(pltpu.SemaphoreType.DMA((3 * n,)), pltpu.SemaphoreType.DMA((3 * n,)), *[pltpu.HBM(b.shape, b.dtype) for b in bufs],
                   jax.ShapeDtypeStruct((8, LANES), F32)),
        in_specs=[_HBM] * n + [ANY], out_specs=(_SEM, _SEM, *[_HBM] * n, pl.BlockSpec(memory_space=pltpu.VMEM)),
        input_output_aliases={a: 2 + a for a in range(n)}, compiler_params=pltpu.CompilerParams(has_side_effects=_DATAFLOW),
    )(*[_in_hbm(b) for b in bufs], after)
    return res[0], res[1], list(res[2:2 + n]), res[-1]


def _gather_wait(send_sems, recv_sems, bufs, after, name):
    n = len(bufs)

    def body(*refs):
        ins, s_sems, r_sems = refs[:n], refs[n], refs[n + 1]
        x, y, c = _my_pos()
        for a in range(n):
            for kk, (dx, dy) in enumerate(_CHIP_PEERS):
                px, py = _flip(x, dx), _flip(y, dy)
                cp = pltpu.make_async_remote_copy(src_ref=_col_half(ins[a], c, (2 * x + y,)), dst_ref=_col_half(ins[a], c, (2 * px + py,)),
                                                  send_sem=s_sems.at[3 * a + kk], recv_sem=r_sems.at[3 * a + kk], device_id=(px, py, c),
                                                  device_id_type=pl.DeviceIdType.MESH)
                cp.wait_send()
                cp.wait_recv()

    return pl.pallas_call(
        body, name=name, out_shape=[pltpu.HBM(b.shape, b.dtype) for b in bufs],
        in_specs=[_HBM] * n + [_SEM, _SEM] + [ANY] * len(after), out_specs=[_HBM] * n, input_output_aliases={a: a for a in range(n)},
        compiler_params=pltpu.CompilerParams(has_side_effects=_DATAFLOW),
    )(*bufs, send_sems, recv_sems, *after)


def _gather_forward(bufs, name):
    n = len(bufs)

    def body(*refs):
        outs = refs[n:2 * n]
        send_sems, recv_sems = refs[2 * n:]
        x, y, c = _my_pos()
        sib = (x, y, 1 - c)

        def remote(a, kk, half):
            dx, dy = _CHIP_PEERS[kk]
            piece = _col_half(outs[a], half, (2 * _flip(x, dx) + _flip(y, dy),))
            return pltpu.make_async_remote_copy(src_ref=piece, dst_ref=piece, send_sem=send_sems.at[3 * a + kk], recv_sem=recv_sems.at[3 * a + kk],
                                                device_id=sib, device_id_type=pl.DeviceIdType.MESH)

        sends = [remote(a, kk, c) for a in range(n) for kk in range(3)]
        for cp in sends:
            cp.start()
        for a in range(n):
            for kk in range(3):
                remote(a, kk, 1 - c).wait_recv()
        for cp in sends:
            cp.wait_send()

    return pl.pallas_call(
        body, out_shape=[jax.ShapeDtypeStruct(b.shape, b.dtype) for b in bufs],
        in_specs=[ANY] * n, out_specs=[ANY] * n, input_output_aliases={a: a for a in range(n)},
        scratch_shapes=[pltpu.SemaphoreType.DMA((3 * n,))] * 2, name=name,
    )(*bufs)


def _scatter_start(parts, after, name):
    n = len(parts)

    def body(*refs):
        ins, lands, send_sems, recv_sems, token = refs[:n], refs[n:2 * n], refs[2 * n + 1], refs[2 * n + 2], refs[-1]
        x, y, c = _my_pos()
        for a in range(n):
            for kk, (dx, dy) in enumerate(_CHIP_PEERS):
                px, py = _flip(x, dx), _flip(y, dy)
                pltpu.make_async_remote_copy(src_ref=ins[a].at[2 * px + py], dst_ref=lands[a].at[2 * x + y], send_sem=send_sems.at[3 * a + kk],
                                             recv_sem=recv_sems.at[3 * a + kk], device_id=(px, py, c), device_id_type=pl.DeviceIdType.MESH).start()
        token[...] = jnp.zeros_like(token)

    thru = [pltpu.HBM(p.shape, p.dtype) for p in parts]
    res = pl.pallas_call(
        body, name=name,
        out_shape=(pltpu.SemaphoreType.DMA((3 * n,)), pltpu.SemaphoreType.DMA((3 * n,)), *thru, *thru, jax.ShapeDtypeStruct((8, LANES), F32)),
        in_specs=[_HBM] * (2 * n) + [ANY], out_specs=(_SEM, _SEM, *[_HBM] * (2 * n), pl.BlockSpec(memory_space=pltpu.VMEM)),
        input_output_aliases={a: 2 + a for a in range(2 * n)}, compiler_params=pltpu.CompilerParams(has_side_effects=_DATAFLOW),
    )(*[_in_hbm(p) for p in parts], *[_in_hbm(lax.empty(p.shape, p.dtype)) for p in parts], after)
    return res[0], res[1], list(res[2:2 + n]), list(res[2 + n:2 + 2 * n]), res[-1]


def _scatter_wait(send_sems, recv_sems, parts, lands, after, name):
    n = len(parts)

    def body(*refs):
        ins, lnd, s_sems, r_sems = refs[:n], refs[n:2 * n], refs[2 * n], refs[2 * n + 1]
        x, y, c = _my_pos()
        for a in range(n):
            for kk, (dx, dy) in enumerate(_CHIP_PEERS):
                px, py = _flip(x, dx), _flip(y, dy)
                cp = pltpu.make_async_remote_copy(src_ref=ins[a].at[2 * px + py], dst_ref=lnd[a].at[2 * px + py], send_sem=s_sems.at[3 * a + kk],
                                                  recv_sem=r_sems.at[3 * a + kk], device_id=(px, py, c), device_id_type=pl.DeviceIdType.MESH)
                cp.wait_send()
                cp.wait_recv()

    thru = [pltpu.HBM(p.shape, p.dtype) for p in parts]
    res = pl.pallas_call(
        body, name=name, out_shape=[*thru, *thru],
        in_specs=[_HBM] * (2 * n) + [_SEM, _SEM] + [ANY] * len(after), out_specs=[_HBM] * (2 * n),
        input_output_aliases={a: a for a in range(2 * n)}, compiler_params=pltpu.CompilerParams(has_side_effects=_DATAFLOW),
    )(*parts, *lands, send_sems, recv_sems, *after)
    return list(res[:n]), list(res[n:])


def _pair_send_start(grads, after, name):
    n = len(grads)

    def body(*refs):
        ins, lands, send_sems, recv_sems, token = refs[:n], refs[n:2 * n], refs[2 * n + 1], refs[2 * n + 2], refs[-1]
        x, y, c = _my_pos()
        for a in range(n):
            pltpu.make_async_remote_copy(src_ref=_col_half(ins[a], 1 - c, (slice(None),)), dst_ref=lands[a], send_sem=send_sems.at[a],
                                         recv_sem=recv_sems.at[a], device_id=(x, y, 1 - c), device_id_type=pl.DeviceIdType.MESH).start()
        token[...] = jnp.zeros_like(token)

    land_shapes = [(4, g.shape[1], g.shape[2] // 2) for g in grads]
    res = pl.pallas_call(
        body, name=name,
        out_shape=(pltpu.SemaphoreType.DMA((n,)), pltpu.SemaphoreType.DMA((n,)), *[pltpu.HBM(g.shape, g.dtype) for g in grads],
                   *[pltpu.HBM(s, g.dtype) for s, g in zip(land_shapes, grads)], jax.ShapeDtypeStruct((8, LANES), F32)),
        in_specs=[_HBM] * (2 * n) + [ANY], out_specs=(_SEM, _SEM, *[_HBM] * (2 * n), pl.BlockSpec(memory_space=pltpu.VMEM)),
        input_output_aliases={a: 2 + a for a in range(2 * n)}, compiler_params=pltpu.CompilerParams(has_side_effects=_DATAFLOW),
    )(*[_in_hbm(g) for g in grads], *[_in_hbm(lax.empty(s, g.dtype)) for s, g in zip(land_shapes, grads)], after)
    return res[0], res[1], list(res[2:2 + n]), list(res[2 + n:2 + 2 * n]), res[-1]


def _pair_send_wait(send_sems, recv_sems, grads, lands, after, name):
    n = len(grads)

    def body(*refs):
        ins, lnd, s_sems, r_sems = refs[:n], refs[n:2 * n], refs[2 * n], refs[2 * n + 1]
        x, y, c = _my_pos()
        for a in range(n):
            cp = pltpu.make_async_remote_copy(src_ref=_col_half(ins[a], 1 - c, (slice(None),)), dst_ref=lnd[a], send_sem=s_sems.at[a],
                                              recv_sem=r_sems.at[a], device_id=(x, y, 1 - c), device_id_type=pl.DeviceIdType.MESH)
            cp.wait_send()
            cp.wait_recv()

    res = pl.pallas_call(
        body, name=name, out_shape=[*[pltpu.HBM(g.shape, g.dtype) for g in grads], *[pltpu.HBM(l.shape, l.dtype) for l in lands]],
        in_specs=[_HBM] * (2 * n) + [_SEM, _SEM, ANY], out_specs=[_HBM] * (2 * n), input_output_aliases={a: a for a in range(2 * n)},
        compiler_params=pltpu.CompilerParams(has_side_effects=_DATAFLOW),
    )(*grads, *lands, send_sems, recv_sems, after)
    return list(res[:n]), list(res[n:])


def _pair_join_halves(bufs, name):
    n = len(bufs)

    def body(*refs):
        outs = refs[n:2 * n]
        send_sems, recv_sems = refs[2 * n:]
        x, y, c = _my_pos()
        sib = (x, y, 1 - c)
        sends = []
        for a in range(n):
            mine = _col_half(outs[a], c, ())
            cp = pltpu.make_async_remote_copy(src_ref=mine, dst_ref=mine, send_sem=send_sems.at[a], recv_sem=recv_sems.at[a],
                                              device_id=sib, device_id_type=pl.DeviceIdType.MESH)
            cp.start()
            sends.append(cp)
        for a in range(n):
            other = _col_half(outs[a], 1 - c, ())
            pltpu.make_async_remote_copy(src_ref=other, dst_ref=other, send_sem=send_sems.at[a], recv_sem=recv_sems.at[a],
                                         device_id=sib, device_id_type=pl.DeviceIdType.MESH).wait_recv()
        for cp in sends:
            cp.wait_send()

    return pl.pallas_call(
        body, out_shape=[jax.ShapeDtypeStruct(b.shape, b.dtype) for b in bufs],
        in_specs=[ANY] * n, out_specs=[ANY] * n, input_output_aliases={a: a for a in range(n)},
        scratch_shapes=[pltpu.SemaphoreType.DMA((n,)), pltpu.SemaphoreType.DMA((n,))], name=name,
    )(*bufs)


def _add_my_half(g, recv, c_idx, name):
    _, r, cc = g.shape
    hc = cc // 2
    tc = _pick(hc, 256)
    per = hc // tc

    def body(c_ref, g_ref, r_ref, o_ref):
        o_ref[...] = (g_ref[...] + r_ref[...]).astype(o_ref.dtype)

    grid_spec = pltpu.PrefetchScalarGridSpec(
        num_scalar_prefetch=1, grid=(4, per),
        in_specs=[pl.BlockSpec((None, r, tc), lambda s, j, c_ref: (s, 0, c_ref[0] * per + j)),
                  pl.BlockSpec((None, r, tc), lambda s, j, c_ref: (s, 0, j))],
        out_specs=pl.BlockSpec((None, r, tc), lambda s, j, c_ref: (s, 0, j)))
    return pl.pallas_call(body, grid_spec=grid_spec, out_shape=jax.ShapeDtypeStruct((4, r, hc), WIRE_DTYPE), name=name,
                          compiler_params=_cparams(("arbitrary", "arbitrary"), 6 * _nbytes((r, tc), F32)))(c_idx, g, recv)


def _sum_slots(own, recv, slots, name):
    _, r, hc = own.shape
    tc = _pick(hc, 256)
    per = hc // tc

    def body(s0, s1, s2, s3, s4, a_ref, b_ref, c_ref, d_ref, o_ref):
        f = lambda ref: ref[...].astype(F32)
        o_ref[...] = ((f(a_ref) + f(b_ref)) + f(c_ref)) + f(d_ref)

    slot = lambda i: pl.BlockSpec((None, r, tc), lambda j, *s: (s[i][0], 0, j))
    grid_spec = pltpu.PrefetchScalarGridSpec(
        num_scalar_prefetch=5, grid=(per,), in_specs=[slot(0), slot(1), slot(2), slot(3)],
        out_specs=pl.BlockSpec((r, tc), lambda j, *s: (0, s[4][0] * per + j)))
    return pl.pallas_call(body, grid_spec=grid_spec, out_shape=jax.ShapeDtypeStruct((r, 2 * hc), F32), name=name,
                          compiler_params=_cparams(("arbitrary",), 14 * _nbytes((r, tc), F32)))(*slots, own, recv, recv, recv)


def _adamw_math(w, g, m, v):
    m = ADAM_B1 * m + (1.0 - ADAM_B1) * g
    v = ADAM_B2 * v + (1.0 - ADAM_B2) * (g * g)
    m_hat = m / (1.0 - ADAM_B1 ** ADAM_STEP)
    v_hat = v / (1.0 - ADAM_B2 ** ADAM_STEP)
    delta = -ADAM_LR * (m_hat / (jnp.sqrt(v_hat) + ADAM_EPS) + ADAM_WD * w)
    return delta, m, v


def _adamw(w, g, m, v, name):
    r, cc = w.shape
    tr = _pick(r, max(8, (512 * 1024) // max(cc, 1)), 8)

    def body(w_ref, g_ref, m_ref, v_ref, d_ref, nm_ref, nv_ref):
        d, nm, nv = _adamw_math(w_ref[...], g_ref[...], m_ref[...], v_ref[...])
        d_ref[...] = d
        nm_ref[...] = nm
        nv_ref[...] = nv

    spec = pl.BlockSpec((tr, cc), lambda i: (i, 0))
    return pl.pallas_call(
        body, grid=(r // tr,), in_specs=[spec] * 4, out_specs=[spec] * 3,
        out_shape=[jax.ShapeDtypeStruct((r, cc), F32)] * 3, name=name,
        compiler_params=_cparams(("arbitrary",), 16 * _nbytes((tr, cc), F32)))(w, g, m, v)


def _sum8(allv, name):
    r8, w = allv.shape
    r = r8 // 8

    def body(a_ref, o_ref):
        acc = a_ref[0:r, :]
        for d in range(1, 8):
            acc = acc + a_ref[d * r:(d + 1) * r, :]
        o_ref[...] = acc

    return pl.pallas_call(body, out_shape=jax.ShapeDtypeStruct((r, w), F32), name=name,
                          compiler_params=_cparams((), 12 * _nbytes((r8, w), F32)))(allv)


def _dmod_prep(alld, name):
    _, w = alld.shape
    tw = _pick(w, 2048)

    def body(a_ref, d_ref, b_ref):
        ctx = a_ref[1:2, :]
        tot = a_ref[0:1, :]
        d_ref[0:1, :] = tot
        for d in range(1, 8):
            ctx = ctx + a_ref[8 * d + 1:8 * d + 2, :]
            tot = tot + a_ref[8 * d:8 * d + 1, :]
            d_ref[d:d + 1, :] = a_ref[8 * d:8 * d + 1, :]
        d_ref[8:9, :] = ctx
        d_ref[9:16, :] = jnp.zeros((7, tw), F32)
        b_ref[...] = tot + ctx

    return pl.pallas_call(
        body, grid=(w // tw,), in_specs=[pl.BlockSpec((64, tw), lambda j: (0, j))],
        out_specs=[pl.BlockSpec((16, tw), lambda j: (0, j)), pl.BlockSpec((1, tw), lambda j: (0, j))],
        out_shape=[jax.ShapeDtypeStruct((16, w), F32), jax.ShapeDtypeStruct((1, w), F32)], name=name,
        compiler_params=_cparams(("arbitrary",), 0))(alld)


def _pack(arrs, rows=8):
    flat = jnp.concatenate([a.reshape(-1).astype(F32) for a in arrs])
    per = rows * LANES
    n = flat.shape[0]
    padded = -(-n // per) * per
    flat = jnp.pad(flat, (0, padded - n))
    return flat.reshape(rows, padded // rows)


def _unpack(slab, shapes):
    flat = slab.reshape(-1)
    out, off = [], 0
    for s in shapes:
        n = math.prod(s)
        out.append(flat[off:off + n].reshape(s))
        off += n
    return out


def kernel(x, c, ctx, c_ctx, ln_in_g, ln_in_b, w_mod, b_mod, w_in, w_qkv_conv, a_log_f, dt_bias_f, a_log_b, dt_bias_b, dn_norm_g, conf_dw_w, conf_dw_b, conf_ln_g, conf_ln_b, w_out, ln1_g, ln1_b, w_mlp1, b_mlp1, w_mlp2, b_mlp2, ln2_g, ln2_b, loss_target, m_c_ctx, m_ln_in_g, m_ln_in_b, m_w_mod, m_b_mod, m_w_in, m_w_qkv_conv, m_a_log_f, m_dt_bias_f, m_a_log_b, m_dt_bias_b, m_dn_norm_g, m_conf_dw_w, m_conf_dw_b, m_conf_ln_g, m_conf_ln_b, m_w_out, m_ln1_g, m_ln1_b, m_w_mlp1, m_b_mlp1, m_w_mlp2, m_b_mlp2, m_ln2_g, m_ln2_b, v_c_ctx, v_ln_in_g, v_ln_in_b, v_w_mod, v_b_mod, v_w_in, v_w_qkv_conv, v_a_log_f, v_dt_bias_f, v_a_log_b, v_dt_bias_b, v_dn_norm_g, v_conf_dw_w, v_conf_dw_b, v_conf_ln_g, v_conf_ln_b, v_w_out, v_ln1_g, v_ln1_b, v_w_mlp1, v_b_mlp1, v_w_mlp2, v_b_mlp2, v_ln2_g, v_ln2_b):
    n_lat, d = x.shape[1], x.shape[2]
    n_ctx = ctx.shape[1]
    nt = n_lat + n_ctx
    dn = d // 2
    nh = dn // HD
    conf = d - dn
    half = conf // 2
    gw = math.isqrt(n_lat)
    ff = 4 * w_mlp1.shape[2]
    fl = w_mlp1.shape[2]
    in_cols = 4 * w_in.shape[2]
    r_in = w_in.shape[2]
    n_gate = 4 * nh
    gb = max(LANES, d // 8)
    g0 = 4 * dn + 2 * conf
    in_pad = g0 + gb
    ml = w_mod.shape[2]
    assert gw * gw == n_lat and nt % gw == 0 and in_cols == 4 * dn + n_gate + 2 * conf and nh * HD == dn
    hb = 4 if nh % 4 == 0 else (2 if nh % 2 == 0 else 1)

    mx, my, mc = _my_pos()
    chip = 2 * mx + my
    dev = 4 * mx + 2 * my + mc
    c_idx = jnp.reshape(mc, (1,)).astype(jnp.int32)

    x2, ctx2, tgt2 = x[0], ctx[0], loss_target[0]
    row = lambda a: a.reshape(1, -1)

    e0_shapes = [(d,), w_qkv_conv.shape[1:], conf_dw_w.shape[1:]]
    e0 = _bcast8(_pack([c[0], w_qkv_conv[0], conf_dw_w[0]]), "bcast_inputs")
    e0 = e0.reshape(8, -1)
    per_dev = [_unpack(e0[dd], e0_shapes) for dd in range(8)]
    all_c = jnp.stack([p[0] for p in per_dev])
    w_conv = jnp.concatenate([per_dev[2 * j][1] for j in range(4)], axis=1)
    w_dw = jnp.concatenate([per_dev[2 * j][2] for j in range(4)], axis=1)

    s16 = jnp.concatenate([all_c, c_ctx[None, :], jnp.zeros((7, d), F32)], axis=0)
    b_mod_loc = lax.dynamic_slice(b_mod, (0, chip * ml), (1, ml))
    to_mxu = lambda t: t.astype(MXU_DTYPE)
    modblk = _matmul(s16, w_mod[0], form="nn", m=16, n=ml, k=d, tm=16, tn=_pick(ml, 1024), tk=_pick(d, 1024),
                     out_dtype=F32, name="mod_fwd", bias=b_mod_loc, a_fn=lambda t: to_mxu(_silu(t)), b_fn=to_mxu)
    allmod = _bcast8(modblk, "bcast_mod").reshape(8, 16, ml)[0::2]
    mod_mine = lax.dynamic_index_in_dim(allmod, dev, axis=1, keepdims=False).reshape(1, 4 * ml)
    mod_ctx = allmod[:, 8, :].reshape(1, 4 * ml)
    sh_a, sc_a, g_a, sh_m, sc_m, g_m = [mod_mine[:, i * d:(i + 1) * d] for i in range(6)]
    csh_a, csc_a = mod_ctx[:, 0:d], mod_ctx[:, d:2 * d]

    chip_idx = jnp.reshape(chip, (1,)).astype(jnp.int32)
    win_t_loc = jnp.transpose(w_in[0]).astype(MXU_DTYPE)
    slot_in = _into_slot(win_t_loc, chip_idx, all_c, "weights_into_slot_0")
    gi_send, gi_recv, gi_bufs, gi_token = _gather_start([slot_in], allmod, "gather_w_in_start")
    slots_rest = [_into_slot(w, chip_idx, gi_token, f"weights_into_slot_{i + 1}") for i, w in enumerate([w_out[0], w_mlp1[0], w_mlp2[0]])]
    sh_a, csh_a = sh_a + gi_token[0:1, 0:1], csh_a + gi_token[0:1, 0:1]

    tmr = _pick(n_lat, 128, 8)
    tmc = _pick(n_ctx, 128, 8)
    ln_in = [row(ln_in_g), row(ln_in_b)]

    def stage_a(rv, pv):
        x0 = _ln(rv[0], pv[0], pv[1])
        return [x0 * (1.0 + pv[3]) + pv[2]], []

    xm_lat, = _rowwise(stage_a, [(x2, d, 0, 0)], ln_in + [sh_a, sc_a], [(d, MXU_DTYPE)], [], n_rows=n_lat, tm=tmr, name="ln_mod_lat")
    xm_ctx, = _rowwise(stage_a, [(ctx2, d, 0, 0)], ln_in + [csh_a, csc_a], [(d, MXU_DTYPE)], [], n_rows=n_ctx, tm=tmc, name="ln_mod_ctx")
    xm = jnp.concatenate([xm_lat, xm_ctx], axis=0)

    gw_in, = _gather_forward(_gather_wait(gi_send, gi_recv, gi_bufs, [xm] + slots_rest, "gather_w_in_wait"), "gather_w_in_forward")
    gs_send, gs_recv, gs_bufs, gs_token = _gather_start(slots_rest, gw_in, "gather_rest_start")
    win_t = gw_in.reshape(in_cols, d)
    win_t = jnp.concatenate([win_t[:4 * dn], win_t[4 * dn + n_gate:], win_t[4 * dn:4 * dn + n_gate],
                             jnp.zeros((gb - n_gate, d), MXU_DTYPE)], axis=0)

    tm_nt = _pick(nt, 1280, 16)
    tn_in = _pick(in_pad, 1280)
    tk_d = _pick(d, 4096)
    h = _matmul(xm, win_t, form="nt", m=nt, n=in_pad, k=d, tm=tm_nt, tn=tn_in, tk=_pick(d, 2048), out_dtype=F32, name="in_proj",
                bias=jnp.zeros((1, in_pad), F32) + gs_token[0:1, 0:1])

    act = _conv7_fwd(h, w_conv, n_lat=n_lat, n_ctx=n_ctx, nh=nh)
    gates_hm = jnp.transpose(h[:, g0:g0 + n_gate].reshape(nt, 4, nh), (2, 0, 1))
    bc = lambda a: jnp.broadcast_to(a.reshape(nh, 1, 1), (nh, 8, LANES))
    dk = dict(n_lat=n_lat, n_ctx=n_ctx, nh=nh, hb=hb)
    o_f, st_f, mi_f = _delta_fwd(act, gates_hm, bc(a_log_f), bc(dt_bias_f), rev=False, **dk)
    o_b, st_b, mi_b = _delta_fwd(act, gates_hm, bc(a_log_b), bc(dt_bias_b), rev=True, **dk)

    h3 = h.reshape(nt // gw, gw, in_pad)
    ck = dict(gw=gw, col0=4 * dn, half=half)
    yh = _conf_conv_fwd(h3, w_dw, vertical=False, **ck).reshape(n_lat, half)
    yv = _conf_conv_fwd(h3, w_dw, vertical=True, **ck).reshape(n_lat, half)

    def mix_fn(o_heads, z_heads, ya, yb, dng, ba, bb, ga, gbb, la, lb):
        outs = []
        for o, z in zip(o_heads, z_heads):
            ms = jnp.mean(o * o, axis=-1, keepdims=True)
            outs.append(o * lax.rsqrt(ms + 1e-6) * dng * _silu(z))
        ya, yb = ya + ba, yb + bb
        mu = (jnp.sum(ya, axis=-1, keepdims=True) + jnp.sum(yb, axis=-1, keepdims=True)) / conf
        ya, yb = ya - mu, yb - mu
        var = (jnp.sum(ya * ya, axis=-1, keepdims=True) + jnp.sum(yb * yb, axis=-1, keepdims=True)) / conf
        rs = lax.rsqrt(var + LN_EPS)
        ca, cb = ya * rs * ga + la, yb * rs * gbb + lb
        return outs, _silu(ca), _silu(cb)

    def heads(t):
        return [t[:, i * HD:(i + 1) * HD] for i in range(nh)]

    def halves(p):
        return p[:, :half], p[:, half:]

    mix_params = [row(dn_norm_g), row(conf_dw_b), row(conf_ln_g), row(conf_ln_b)]

    def mix_args(rv, pv):
        (ba, bb), (ga, gbb), (la, lb) = halves(pv[1]), halves(pv[2]), halves(pv[3])
        return (heads(rv[0] + rv[1]), heads(rv[2]), rv[3], rv[4], pv[0], ba, bb, ga, gbb, la, lb)

    def stage_e(rv, pv):
        outs, ca, cb = mix_fn(*mix_args(rv, pv))
        return [jnp.concatenate(outs + [ca, cb], axis=-1)], []

    mix_rows = [(o_f, dn, 0, 0), (o_b, dn, 0, 0), (h, dn, 3, 0), (yh, half, 0, 0), (yv, half, 0, 0)]
    mix, = _rowwise(stage_e, mix_rows, mix_params, [(d, MXU_DTYPE)], [], n_rows=n_lat, tm=tmr, name="mixer_fwd")

    gw_out, gw_1, gw_2 = _gather_forward(_gather_wait(gs_send, gs_recv, gs_bufs, [mix], "gather_rest_wait"), "gather_rest_forward")
    wout_full = gw_out.reshape(d, d)
    w2_full = gw_2.reshape(ff, d)
    w1_sh = gw_1

    tm_l = _pick(n_lat, 1024, 16)
    tn_d = _pick(d, 1024)
    y1 = _matmul(mix, wout_full, form="nn", m=n_lat, n=d, k=d, tm=tm_l, tn=tn_d, tk=tk_d, out_dtype=F32, name="out_proj")

    def res1(x0, y1v, ga, l1g, l1b, shm, scm):
        x1 = _ln(ALPHA * x0 + ga * y1v, l1g, l1b)
        return x1, x1 * (1.0 + scm) + shm

    f_params = ln_in + [g_a, row(ln1_g), row(ln1_b), sh_m, sc_m]

    def stage_f(rv, pv):
        x1, u = res1(_ln(rv[0], pv[0], pv[1]), rv[1], *pv[2:])
        return [x1, u], []

    x1, u = _rowwise(stage_f, [(x2, d, 0, 0), (y1, d, 0, 0)], f_params, [(d, F32), (d, MXU_DTYPE)], [], n_rows=n_lat, tm=tmr, name="res1_fwd")

    tn_f = _pick(fl, 1024)
    relu_h = _matmul(u, w1_sh, form="nn", m=n_lat, n=ff, k=d, tm=tm_l, tn=tn_f, tk=tk_d, out_dtype=MXU_DTYPE, name="mlp1",
                     bias=row(b_mlp1), epi=lambda r: jnp.maximum(r, 0.0), b_split=fl)
    sq = lambda t: (t.astype(F32) * t.astype(F32)).astype(MXU_DTYPE)
    tk_f = _pick(ff, 2048)
    y2 = _matmul(relu_h, w2_full, form="nn", m=n_lat, n=d, k=ff, tm=tm_l, tn=tn_d, tk=tk_f, out_dtype=F32, name="mlp2",
                 bias=row(b_mlp2), a_fn=sq)

    def loss_fn(x1v, y2v, gm, l2g, l2b, tgt):
        x2v = _ln(ALPHA * x1v + gm * y2v, l2g, l2b)
        return (0.5 / d) * jnp.sum(jnp.square(x2v - tgt))

    def stage_g(rv, pv):
        loss, grads = jax.value_and_grad(loss_fn, argnums=(0, 1, 2, 3, 4))(rv[0], rv[1], pv[0], pv[1], pv[2], rv[2])
        dx1, dy2r, dgm, dl2g, dl2b = grads
        dy2 = dy2r
        return [dx1, dy2], [jnp.reshape(loss, (1, 1)), dgm, dl2g, dl2b, jnp.sum(dy2, axis=0, keepdims=True)]

    dx1p, dy2, loss_acc, d_g_m, d_ln2_g, d_ln2_b, d_b_mlp2 = _rowwise(
        stage_g, [(x1, d, 0, 0), (y2, d, 0, 0), (tgt2, d, 0, 0)], [g_m, row(ln2_g), row(ln2_b)],
        [(d, F32), (d, MXU_DTYPE)], [LANES, d, d, d, d], n_rows=n_lat, tm=tmr, name="loss_res2_bwd")

    dhid, d_b_mlp1 = _matmul(dy2, w2_full, form="nt", m=n_lat, n=ff, k=d, tm=tm_l, tn=tn_f, tk=tk_d, out_dtype=MXU_DTYPE, name="mlp2_bwd_x",
                             epi=lambda r, rh: r * (2.0 * rh.astype(F32)), epi_in=relu_h, colsum=True)
    tk_l = _pick(n_lat, 4096, 16)
    d_w2 = _matmul(relu_h, dy2, form="tn", m=ff, n=d, k=n_lat, tm=tn_f, tn=tn_d, tk=tk_l, out_dtype=F32, name="mlp2_bwd_w", a_fn=sq)
    du = _matmul(dhid, w1_sh, form="nt", m=n_lat, n=d, k=ff, tm=tm_l, tn=tn_d, tk=_pick(fl, 2048), out_dtype=F32, name="mlp1_bwd_x", b_split=fl)
    d_w1 = _matmul(u, dhid, form="tn", m=d, n=ff, k=n_lat, tm=tn_d, tn=tn_f, tk=tk_l, out_dtype=F32, name="mlp1_bwd_w", o_split=fl)

    as_idx = lambda v: jnp.reshape(v, (1,)).astype(jnp.int32)
    slot_idx = [chip_idx, as_idx(2 * (1 - mx) + my), as_idx(2 * mx + (1 - my)), as_idx(2 * (1 - mx) + (1 - my)), c_idx]
    mp_send, mp_recv, mlp_grads, mlp_half, mp_token = _pair_send_start([d_w1, d_w2.reshape(4, fl, d)], loss_acc, "mlp_grads_pair_start")

    def stage_h(rv, pv):
        x0 = _ln(rv[0], pv[0], pv[1])
        _, vjp = jax.vjp(res1, x0, rv[1], *pv[2:])
        dx0, dy1r, dga, dl1g, dl1b, dshm, dscm = vjp((rv[3], rv[2]))
        return [dx0, dy1r], [dga, dl1g, dl1b, dshm, dscm]

    h_params = ln_in + [g_a + mp_token[0:1, 0:1]] + f_params[3:]
    dx0p, dy1, d_g_a, d_ln1_g, d_ln1_b, d_sh_m, d_sc_m = _rowwise(
        stage_h, [(x2, d, 0, 0), (y1, d, 0, 0), (du, d, 0, 0), (dx1p, d, 0, 0)], h_params,
        [(d, F32), (d, MXU_DTYPE)], [d, d, d, d, d], n_rows=n_lat, tm=tmr, name="res1_bwd")

    dmix = _matmul(dy1, wout_full, form="nt", m=n_lat, n=d, k=d, tm=tm_l, tn=tn_d, tk=tk_d, out_dtype=F32, name="out_proj_bwd_x")
    d_wout = _matmul(mix, dy1, form="tn", m=d, n=d, k=n_lat, tm=tn_d, tn=tn_d, tk=tk_l, out_dtype=F32, name="out_proj_bwd_w")

    def stage_i(rv, pv):
        args = mix_args(rv, pv)
        _, vjp = jax.vjp(mix_fn, *args)
        dm = rv[5]
        d_outs = [dm[:, i * HD:(i + 1) * HD] for i in range(nh)]
        dca, dcb = dm[:, dn:dn + half], dm[:, dn + half:]
        do_h, dz_h, dya, dyb, ddng, dba, dbb, dga, dgbb, dla, dlb = vjp((d_outs, dca, dcb))
        cat = lambda *p: jnp.concatenate(p, axis=-1)
        return ([cat(*do_h), cat(*dz_h), dya, dyb], [ddng, cat(dba, dbb), cat(dga, dgbb), cat(dla, dlb)])

    do, dz, dyh, dyv, d_dn_norm_g, d_conf_dw_b, d_conf_ln_g, d_conf_ln_b = _rowwise(
        stage_i, mix_rows + [(dmix, d, 0, 0)], mix_params, [(dn, F32), (dn, MXU_DTYPE), (half, F32), (half, F32)],
        [HD, conf, conf, conf], n_rows=n_lat, tm=tmr, name="mixer_bwd")

    mlp_grads, mlp_half = _pair_send_wait(mp_send, mp_recv, mlp_grads, mlp_half, do, "mlp_grads_pair_wait")
    mlp_sums = [_add_my_half(g, r, c_idx, f"mlp_grads_pair_add_{i}") for i, (g, r) in enumerate(zip(mlp_grads, mlp_half))]
    ms_send, ms_recv, mlp_sums, mlp_lands, ms_token = _scatter_start(mlp_sums, loss_acc, "mlp_grads_scatter_start")
    tie = ms_token[0:1, 0:1].reshape(1, 1, 1)

    dval_h, dgate_h, d_wdw_h = _conf_conv_bwd(h3, w_dw, dyh.reshape(gw, gw, half), vertical=False, **ck)
    dval_v, dgate_v, d_wdw_v = _conf_conv_bwd(h3, w_dw, dyv.reshape(gw, gw, half), vertical=True, **ck)
    dact_f, dgt_f, d_alog_f, d_dt_f = _delta_bwd(act, gates_hm, bc(a_log_f) + tie, bc(dt_bias_f), st_f, mi_f, do, rev=False, **dk)
    dact_b, dgt_b, d_alog_b, d_dt_b = _delta_bwd(act, gates_hm, bc(a_log_b) + tie, bc(dt_bias_b), st_b, mi_b, do, rev=True, **dk)
    dh_qkv, d_wconv = _conv7_bwd(h, w_conv, dact_f, dact_b, n_lat=n_lat, n_ctx=n_ctx, nh=nh)

    dgates = jnp.stack([dgt_f[..., 0], dgt_f[..., 1], dgt_b[..., 0], dgt_b[..., 1]], axis=0)
    dgates = jnp.transpose(dgates, (2, 0, 1)).reshape(nt, n_gate)
    dgates = jnp.pad(dgates, ((0, 0), (0, gb - n_gate))).astype(MXU_DTYPE)
    zrows = lambda t: jnp.pad(t, ((0, n_ctx), (0, 0)))
    dh = jnp.concatenate([dh_qkv, zrows(dz), zrows(dval_h.reshape(n_lat, half)), zrows(dval_v.reshape(n_lat, half)),
                          zrows(dgate_h.reshape(n_lat, half)), zrows(dgate_v.reshape(n_lat, half)), dgates], axis=1)

    tk_in = _pick(in_pad, 2560)
    tk_nt = _pick(nt, 2560, 16)
    d_win_t = _matmul(dh, xm, form="tn", m=in_pad, n=d, k=nt, tm=tn_in, tn=tn_d, tk=tk_nt, out_dtype=F32, name="in_proj_bwd_w")
    d_win_t = jnp.concatenate([d_win_t[:4 * dn], d_win_t[g0:g0 + n_gate], d_win_t[4 * dn:g0]], axis=0)

    proj_grads = [d_win_t.reshape(4, r_in, d), d_wout.reshape(4, d // 4, d)]
    pp_send, pp_recv, proj_grads, proj_half, pp_token = _pair_send_start(proj_grads, loss_acc, "proj_grads_pair_start")
    dxm = _matmul(dh, win_t, form="nn", m=nt, n=d, k=in_pad, tm=tm_nt, tn=tn_d, tk=tk_in, out_dtype=F32, name="in_proj_bwd_x",
                  bias=jnp.zeros((1, d), F32) + pp_token[0:1, 0:1])
    proj_grads, proj_half = _pair_send_wait(pp_send, pp_recv, proj_grads, proj_half, dxm, "proj_grads_pair_wait")
    proj_sums = [_add_my_half(g, r, c_idx, f"proj_grads_pair_add_{i}") for i, (g, r) in enumerate(zip(proj_grads, proj_half))]
    ps_send, ps_recv, proj_sums, proj_lands, ps_token = _scatter_start(proj_sums, loss_acc, "proj_grads_scatter_start")
    mlp_sums, mlp_lands = _scatter_wait(ms_send, ms_recv, mlp_sums, mlp_lands, [ps_token], "mlp_grads_scatter_wait")
    mlp_mine = [_sum_slots(own, rcv, slot_idx, f"mlp_grads_chip_sum_{i}") for i, (own, rcv) in enumerate(zip(mlp_sums, mlp_lands))]
    g_w_mlp1, g_w_mlp2 = _pair_join_halves(mlp_mine, "mlp_grads_pair_join")
    sc_a_tied = sc_a + ps_token[0:1, 0:1]

    def mod_in(xr, lg, lb, sh, sc):
        x0 = _ln(xr, lg, lb)
        return x0, x0 * (1.0 + sc) + sh

    def stage_j(rv, pv):
        _, vjp = jax.vjp(mod_in, rv[0], *pv)
        dx0 = rv[2] if len(rv) > 2 else jnp.zeros_like(rv[1])
        dxr, dlg, dlb, dsh, dsc = vjp((dx0, rv[1]))
        return [dxr], [dlg, dlb, dsh, dsc]

    grad_x, dlg_l, dlb_l, d_sh_a, d_sc_a = _rowwise(
        stage_j, [(x2, d, 0, 0), (dxm, d, 0, 0), (dx0p, d, 0, 0)], ln_in + [sh_a, sc_a_tied], [(d, F32)], [d, d, d, d],
        n_rows=n_lat, tm=tmr, name="ln_mod_bwd_lat")
    _, dlg_c, dlb_c, d_csh_a, d_csc_a = _rowwise(
        stage_j, [(ctx2, d, 0, 0), (dxm, d, 0, n_lat // tmc)], ln_in + [csh_a, csc_a], [(d, F32)], [d, d, d, d],
        n_rows=n_ctx, tm=tmc, name="ln_mod_bwd_ctx")

    zd = jnp.zeros((1, d), F32)
    dmod_rows = jnp.concatenate([jnp.concatenate([d_sh_a, d_sc_a, d_g_a, d_sh_m, d_sc_m, d_g_m], axis=1),
                                 jnp.concatenate([d_csh_a, d_csc_a, zd, zd, zd, zd], axis=1), jnp.zeros((6, 6 * d), F32)], axis=0)
    d16, g_b_mod = _dmod_prep(_bcast8(dmod_rows, "bcast_dmod"), "dmod_prep")
    d16_loc = lax.dynamic_slice(d16, (0, chip * ml), (16, ml))
    g_w_mod = _matmul(s16, d16_loc, form="tn", m=d, n=ml, k=16, tm=_pick(d, 512), tn=_pick(ml, 1024), tk=16, out_dtype=F32,
                      name="mod_bwd_w", a_fn=_silu, kind="hi")
    dsilu = _matmul(d16_loc, w_mod[0], form="nt", m=16, n=d, k=ml, tm=16, tn=_pick(d, 1024), tk=_pick(ml, 1024), out_dtype=F32,
                    name="mod_bwd_c", a_fn=to_mxu, b_fn=to_mxu)
    d_cctx_part = dsilu[8:9, :] * (1 - mc).astype(F32)

    head_sum = lambda t: t[:, 0, 0]
    small = [d_cctx_part, dlg_l + dlg_c, dlb_l + dlb_c, d_wconv, head_sum(d_alog_f), head_sum(d_dt_f), head_sum(d_alog_b),
             head_sum(d_dt_b), d_dn_norm_g, jnp.concatenate([d_wdw_h, d_wdw_v], axis=1), d_conf_dw_b, d_conf_ln_g, d_conf_ln_b,
             d_ln1_g, d_ln1_b, d_b_mlp1, d_b_mlp2, d_ln2_g, d_ln2_b]
    small_shapes = [(d,), (d,), (d,), (SHORT_CONV, 3 * dn), (nh,), (nh,), (nh,), (nh,), (HD,), (CONF_K, conf), (conf,), (conf,), (conf,),
                    (d,), (d,), (ff,), (d,), (d,), (d,)]
    ssum = _sum8(_bcast8(_pack(small), "bcast_small_grads"), "sum_small_grads")
    (t_cctx, g_ln_in_g, g_ln_in_b, g_wconv_full, g_a_log_f, g_dt_bias_f, g_a_log_b, g_dt_bias_b, g_dn_norm_g, g_wdw_full, g_conf_dw_b,
     g_conf_ln_g, g_conf_ln_b, g_ln1_g, g_ln1_b, g_b_mlp1, g_b_mlp2, g_ln2_g, g_ln2_b) = _unpack(ssum, small_shapes)
    sg = jax.nn.sigmoid(c_ctx)
    g_c_ctx = t_cctx * (sg * (1.0 + c_ctx * (1.0 - sg)))
    g_w_qkv_conv = lax.dynamic_slice(g_wconv_full, (0, chip * w_qkv_conv.shape[2]), w_qkv_conv.shape[1:])
    g_conf_dw_w = lax.dynamic_slice(g_wdw_full, (0, chip * conf_dw_w.shape[2]), conf_dw_w.shape[1:])

    grads = dict(c_ctx=g_c_ctx, ln_in_g=g_ln_in_g, ln_in_b=g_ln_in_b, w_mod=g_w_mod[None], b_mod=g_b_mod, w_in=None,
                 w_qkv_conv=g_w_qkv_conv[None], a_log_f=g_a_log_f[None], dt_bias_f=g_dt_bias_f[None], a_log_b=g_a_log_b[None],
                 dt_bias_b=g_dt_bias_b[None], dn_norm_g=g_dn_norm_g[None], conf_dw_w=g_conf_dw_w[None], conf_dw_b=g_conf_dw_b[None],
                 conf_ln_g=g_conf_ln_g[None], conf_ln_b=g_conf_ln_b[None], w_out=None, ln1_g=g_ln1_g[None], ln1_b=g_ln1_b[None],
                 w_mlp1=g_w_mlp1[None], b_mlp1=g_b_mlp1[None], w_mlp2=g_w_mlp2[None], b_mlp2=g_b_mlp2[None], ln2_g=g_ln2_g[None],
                 ln2_b=g_ln2_b[None])
    weights = dict(c_ctx=c_ctx, ln_in_g=ln_in_g, ln_in_b=ln_in_b, w_mod=w_mod, b_mod=b_mod, w_in=w_in, w_qkv_conv=w_qkv_conv,
                   a_log_f=a_log_f, dt_bias_f=dt_bias_f, a_log_b=a_log_b, dt_bias_b=dt_bias_b, dn_norm_g=dn_norm_g, conf_dw_w=conf_dw_w,
                   conf_dw_b=conf_dw_b, conf_ln_g=conf_ln_g, conf_ln_b=conf_ln_b, w_out=w_out, ln1_g=ln1_g, ln1_b=ln1_b, w_mlp1=w_mlp1,
                   b_mlp1=b_mlp1, w_mlp2=w_mlp2, b_mlp2=b_mlp2, ln2_g=ln2_g, ln2_b=ln2_b)
    m_in = dict(c_ctx=m_c_ctx, ln_in_g=m_ln_in_g, ln_in_b=m_ln_in_b, w_mod=m_w_mod, b_mod=m_b_mod, w_in=m_w_in, w_qkv_conv=m_w_qkv_conv,
                a_log_f=m_a_log_f, dt_bias_f=m_dt_bias_f, a_log_b=m_a_log_b, dt_bias_b=m_dt_bias_b, dn_norm_g=m_dn_norm_g,
                conf_dw_w=m_conf_dw_w, conf_dw_b=m_conf_dw_b, conf_ln_g=m_conf_ln_g, conf_ln_b=m_conf_ln_b, w_out=m_w_out, ln1_g=m_ln1_g,
                ln1_b=m_ln1_b, w_mlp1=m_w_mlp1, b_mlp1=m_b_mlp1, w_mlp2=m_w_mlp2, b_mlp2=m_b_mlp2, ln2_g=m_ln2_g, ln2_b=m_ln2_b)
    v_in = dict(c_ctx=v_c_ctx, ln_in_g=v_ln_in_g, ln_in_b=v_ln_in_b, w_mod=v_w_mod, b_mod=v_b_mod, w_in=v_w_in, w_qkv_conv=v_w_qkv_conv,
                a_log_f=v_a_log_f, dt_bias_f=v_dt_bias_f, a_log_b=v_a_log_b, dt_bias_b=v_dt_bias_b, dn_norm_g=v_dn_norm_g,
                conf_dw_w=v_conf_dw_w, conf_dw_b=v_conf_dw_b, conf_ln_g=v_conf_ln_g, conf_ln_b=v_conf_ln_b, w_out=v_w_out, ln1_g=v_ln1_g,
                ln1_b=v_ln1_b, w_mlp1=v_w_mlp1, b_mlp1=v_b_mlp1, w_mlp2=v_w_mlp2, b_mlp2=v_b_mlp2, ln2_g=v_ln2_g, ln2_b=v_ln2_b)
    names = list(weights)
    big_names = ("w_mod", "w_mlp1", "w_mlp2", "w_in", "w_out")
    delta, new_m, new_v = {}, {}, {}

    def big_adamw(nm):
        dl, mm, vv = _adamw(weights[nm][0], grads[nm][0], m_in[nm][0], v_in[nm][0], f"adamw_{nm}")
        delta[nm], new_m[nm], new_v[nm] = dl[None], mm[None], vv[None]

    for nm in big_names[:3]:
        big_adamw(nm)
    small_names = [nm for nm in names if nm not in big_names]
    shapes = [weights[nm].shape for nm in small_names]
    for nm in small_names:
        grads[nm] = grads[nm].reshape(weights[nm].shape)
    packed = [_pack([src[nm] for nm in small_names]) for src in (weights, grads, m_in, v_in)]
    dl, mm, vv = _adamw(*packed, "adamw_small")
    for nm, a, b_, c_ in zip(small_names, _unpack(dl, shapes), _unpack(mm, shapes), _unpack(vv, shapes)):
        delta[nm], new_m[nm], new_v[nm] = a, b_, c_

    done = [new_v[nm] for nm in big_names[:3]] + [vv]
    proj_sums, proj_lands = _scatter_wait(ps_send, ps_recv, proj_sums, proj_lands, done, "proj_grads_scatter_wait")
    proj_mine = [_sum_slots(own, rcv, slot_idx, f"proj_grads_chip_sum_{i}") for i, (own, rcv) in enumerate(zip(proj_sums, proj_lands))]
    g_win_t, g_w_out = _pair_join_halves(proj_mine, "proj_grads_pair_join")
    grads["w_in"], grads["w_out"] = jnp.transpose(g_win_t)[None], g_w_out[None]
    for nm in big_names[3:]:
        big_adamw(nm)

    loss = lax.psum(loss_acc[0, 0], MESH_AXES)
    return (loss, grad_x[None], *[grads[nm] for nm in names], *[delta[nm] for nm in names],
            *[new_m[nm] for nm in names], *[new_v[nm] for nm in names])
```

```python
import functools
import math

import jax
import jax.numpy as jnp
from jax import lax
from jax.experimental import pallas as pl
from jax.experimental.pallas import tpu as pltpu

F32 = jnp.float32
BF16 = jnp.bfloat16
MXU_DTYPE = BF16
WIRE_DTYPE = BF16
HIGHEST = lax.Precision.HIGHEST

HD = 128
CHUNK = 128
SHORT_CONV = 7
CONF_K = 31
ALPHA = 2.0 ** 0.25
LN_EPS = 1e-5
V7X_VMEM_BYTES = 64 * 1024 * 1024
LANES = 128

ADAM_LR = 0.001
ADAM_B1 = 0.9
ADAM_B2 = 0.999
ADAM_EPS = 1e-08
ADAM_WD = 0.01
ADAM_STEP = 10

MESH_AXES = ("x", "y", "c")
ANY = pl.BlockSpec(memory_space=pl.ANY)


def _pick(dim, pref, mult=LANES):
    best = None
    t = mult
    while t <= min(dim, pref):
        if dim % t == 0:
            best = t
        t += mult
    return best if best is not None else dim


def _cparams(sem, vmem_est):
    limit = int(min(V7X_VMEM_BYTES - 6 * 1024 * 1024, max(32 * 1024 * 1024, vmem_est + 8 * 1024 * 1024)))
    return pltpu.CompilerParams(dimension_semantics=sem, vmem_limit_bytes=limit)


def _nbytes(shape, dtype):
    return math.prod(shape) * jnp.dtype(dtype).itemsize


def _ln(x, g, b):
    mu = jnp.mean(x, axis=-1, keepdims=True)
    xc = x - mu
    var = jnp.mean(xc * xc, axis=-1, keepdims=True)
    return xc * lax.rsqrt(var + LN_EPS) * g + b


def _silu(x):
    return x * jax.nn.sigmoid(x)


def _softplus(x):
    return jnp.maximum(x, 0.0) + jnp.log1p(jnp.exp(-jnp.abs(x)))


_DIMS = {"nn": (((1,), (0,)), ((), ())), "nt": (((1,), (1,)), ((), ())), "tn": (((0,), (0,)), ((), ()))}


def _split_bf16(x, parts):
    out, rest = [], x.astype(F32)
    for _ in range(parts):
        bits = lax.bitcast_convert_type(rest, jnp.uint32) & jnp.uint32(0xFFFF0000)
        p = lax.bitcast_convert_type(bits, F32)
        out.append(p.astype(BF16))
        rest = rest - p
    return out


def _raw_dot(a, b, form, kind):
    if MXU_DTYPE != F32:
        if kind == "mxu":
            return lax.dot_general(a.astype(MXU_DTYPE), b.astype(MXU_DTYPE), _DIMS[form], preferred_element_type=F32)
        if kind == "hi3":
            (a0, a1), (b0, b1) = _split_bf16(a, 2), _split_bf16(b, 2)
            d = lambda p, q: lax.dot_general(p, q, _DIMS[form], preferred_element_type=F32)
            return d(a0, b0) + (d(a1, b0) + d(a0, b1))
    return lax.dot_general(a.astype(F32), b.astype(F32), _DIMS[form], precision=HIGHEST, preferred_element_type=F32)


@functools.partial(jax.custom_vjp, nondiff_argnums=(2, 3))
def _dot(a, b, form, kind):
    return _raw_dot(a, b, form, kind)


def _dot_fwd(a, b, form, kind):
    return _raw_dot(a, b, form, kind), (a, b)


def _dot_bwd(form, kind, res, dc):
    a, b = res
    if form == "nn":
        return _dot(dc, b, "nt", kind), _dot(a, dc, "tn", kind)
    if form == "nt":
        return _dot(dc, b, "nn", kind), _dot(dc, a, "tn", kind)
    return _dot(b, dc, "nt", kind), _dot(a, dc, "nn", kind)


_dot.defvjp(_dot_fwd, _dot_bwd)


def _raw_xdot(a, b, form, exact):
    if MXU_DTYPE == F32:
        return _raw_dot(a, b, form, "hi")
    d = lambda p, q: lax.dot_general(p, q, _DIMS[form], preferred_element_type=F32)
    if exact == "a":
        a16, (b0, b1, b2) = a.astype(BF16), _split_bf16(b, 3)
        return d(a16, b0) + (d(a16, b1) + d(a16, b2))
    b16, (a0, a1, a2) = b.astype(BF16), _split_bf16(a, 3)
    return d(a0, b16) + (d(a1, b16) + d(a2, b16))


@functools.partial(jax.custom_vjp, nondiff_argnums=(2, 3))
def _xdot(a, b, form, exact):
    return _raw_xdot(a, b, form, exact)


def _xdot_fwd(a, b, form, exact):
    return _raw_xdot(a, b, form, exact), (a, b)


def _xdot_bwd(form, exact, res, dc):
    a, b = res
    if form == "nn" and exact == "a":
        return jnp.zeros_like(a), _xdot(a, dc, "tn", "a")
    assert form == "tn" and exact == "b"
    return _xdot(b, dc, "nt", "a"), jnp.zeros_like(b)


_xdot.defvjp(_xdot_fwd, _xdot_bwd)


def _spec2(block, idx, split=None):
    if split is None:
        return pl.BlockSpec(tuple(block), idx)
    per = split // block[1]

    def idx3(*g):
        r, cblk = idx(*g)
        return (cblk // per, r, cblk % per)

    return pl.BlockSpec((None,) + tuple(block), idx3)


def _matmul(a, b, *, form, m, n, k, tm, tn, tk, out_dtype, name, bias=None, a_fn=None, b_fn=None,
            epi=None, epi_in=None, colsum=False, kind="mxu", b_split=None, o_split=None):
    assert m % tm == 0 and n % tn == 0 and k % tk == 0, (name, m, n, k, tm, tn, tk)
    nk = k // tk
    grid = (n // tn, m // tm, nk)
    if form == "tn":
        a_spec = pl.BlockSpec((tk, tm), lambda j, i, kk: (kk, i))
    else:
        a_spec = pl.BlockSpec((tm, tk), lambda j, i, kk: (i, kk))
    if form == "nt":
        b_spec = _spec2((tn, tk), lambda j, i, kk: (j, kk), b_split)
    else:
        b_spec = _spec2((tk, tn), lambda j, i, kk: (kk, j), b_split)
    in_specs = [a_spec, b_spec]
    operands = [a, b]
    if bias is not None:
        in_specs.append(pl.BlockSpec((1, tn), lambda j, i, kk: (0, j)))
        operands.append(bias)
    if epi_in is not None:
        in_specs.append(pl.BlockSpec((tm, tn), lambda j, i, kk: (i, j)))
        operands.append(epi_in)
    out_specs = [_spec2((tm, tn), lambda j, i, kk: (i, j), o_split)]
    if o_split is None:
        out_shape = [jax.ShapeDtypeStruct((m, n), out_dtype)]
    else:
        out_shape = [jax.ShapeDtypeStruct((n // o_split, m, o_split), out_dtype)]
    if colsum:
        out_specs.append(pl.BlockSpec((1, tn), lambda j, i, kk: (0, j)))
        out_shape.append(jax.ShapeDtypeStruct((1, n), F32))
    has_bias, has_epi_in = bias is not None, epi_in is not None

    def body(*refs):
        refs = list(refs)
        a_ref, b_ref = refs[0], refs[1]
        pos = 2
        bias_ref = epi_ref = cs_ref = None
        if has_bias:
            bias_ref = refs[pos]
            pos += 1
        if has_epi_in:
            epi_ref = refs[pos]
            pos += 1
        o_ref = refs[pos]
        pos += 1
        if colsum:
            cs_ref = refs[pos]
            pos += 1
        acc_ref = refs[pos] if nk > 1 else None
        i, kk = pl.program_id(1), pl.program_id(2)

        av = a_ref[...]
        if a_fn is not None:
            av = a_fn(av)
        bv = b_ref[...]
        if b_fn is not None:
            bv = b_fn(bv)
        prod = _raw_dot(av, bv, form, kind)

        if nk > 1:
            @pl.when(kk == 0)
            def _():
                acc_ref[...] = prod

            @pl.when(jnp.logical_and(kk > 0, kk < nk - 1))
            def _():
                acc_ref[...] += prod

        @pl.when(kk == nk - 1)
        def _():
            r = acc_ref[...] + prod if nk > 1 else prod
            if has_bias:
                r = r + bias_ref[...]
            if epi is not None:
                r = epi(r, epi_ref[...]) if has_epi_in else epi(r)
            o_ref[...] = r.astype(out_dtype)
            if colsum:
                s = jnp.sum(r, axis=0, keepdims=True)

                @pl.when(i == 0)
                def _():
                    cs_ref[...] = s

                @pl.when(i > 0)
                def _():
                    cs_ref[...] += s

    est = 2 * (_nbytes((tm, tk), a.dtype) + _nbytes((tk, tn), b.dtype) + _nbytes((tm, tn), out_dtype))
    est += _nbytes((tm, tn), F32) * 2 + (2 * _nbytes((tm, tn), epi_in.dtype) if has_epi_in else 0)
    res = pl.pallas_call(
        body, grid=grid, in_specs=in_specs, out_specs=out_specs, out_shape=out_shape,
        scratch_shapes=[pltpu.VMEM((tm, tn), F32)] if nk > 1 else [], name=name,
        compiler_params=_cparams(("arbitrary", "arbitrary", "arbitrary"), est),
    )(*operands)
    return res if colsum else res[0]


def _rowwise(fn, rows, params, row_outs, acc_outs, *, n_rows, tm, name, after=()):
    assert n_rows % tm == 0, (name, n_rows, tm)
    nr, npar, nro, nac, naf = len(rows), len(params), len(row_outs), len(acc_outs), len(after)
    in_specs = [pl.BlockSpec((tm, w), functools.partial(lambda i, cb, ro: (i + ro, cb), cb=cb, ro=ro))
                for (_, w, cb, ro) in rows]
    in_specs += [pl.BlockSpec((1, p.shape[1]), lambda i: (0, 0)) for p in params] + [ANY] * naf
    out_specs = [pl.BlockSpec((tm, w), lambda i: (i, 0)) for (w, _) in row_outs]
    out_specs += [pl.BlockSpec((1, w), lambda i: (0, 0)) for w in acc_outs]
    out_shape = [jax.ShapeDtypeStruct((n_rows, w), dt) for (w, dt) in row_outs]
    out_shape += [jax.ShapeDtypeStruct((1, w), F32) for w in acc_outs]

    def body(*refs):
        rv = [r[...] for r in refs[:nr]]
        pv = [r[...] for r in refs[nr:nr + npar]]
        ro_refs = refs[nr + npar + naf:nr + npar + naf + nro]
        ac_refs = refs[nr + npar + naf + nro:]
        ro, ac = fn(rv, pv)
        for ref, val in zip(ro_refs, ro, strict=True):
            ref[...] = val.astype(ref.dtype)
        first = pl.program_id(0) == 0
        for ref, val in zip(ac_refs, ac, strict=True):
            val = jnp.broadcast_to(val.astype(F32), ref.shape)

            @pl.when(first)
            def _(ref=ref, val=val):
                ref[...] = val

            @pl.when(jnp.logical_not(first))
            def _(ref=ref, val=val):
                ref[...] += val

    io = sum(_nbytes((tm, w), a.dtype) for (a, w, _, _) in rows) + sum(_nbytes((tm, w), dt) for (w, dt) in row_outs)
    widest = max([w for (_, w, _, _) in rows] + [w for (w, _) in row_outs])
    est = 2 * io + 12 * _nbytes((tm, widest), F32)
    return pl.pallas_call(
        body, grid=(n_rows // tm,), in_specs=in_specs, out_specs=out_specs, out_shape=out_shape, name=name,
        compiler_params=_cparams(("arbitrary",), est),
    )(*[a for (a, _, _, _) in rows], *params, *after)


def _conv7_tiles(n_lat, n_ctx):
    tt = min(256, n_ctx)
    assert n_lat % tt == 0 and n_ctx % tt == 0
    return tt, [(t0, 8 + t0) for t0 in range(0, n_lat, tt)] + [(n_lat + t0, 16 + n_lat + t0) for t0 in range(0, n_ctx, tt)]


def _conv7_fill(p_ref, x_ref, n_lat, n_ctx):
    z8 = jnp.zeros((8, LANES), F32)
    p_ref[0:8, :] = z8
    p_ref[8:8 + n_lat, :] = x_ref[0:n_lat, :]
    p_ref[8 + n_lat:16 + n_lat, :] = z8
    p_ref[16 + n_lat:16 + n_lat + n_ctx, :] = x_ref[n_lat:n_lat + n_ctx, :]
    p_ref[16 + n_lat + n_ctx:24 + n_lat + n_ctx, :] = z8


def _conv7_fwd(h, w_conv, *, n_lat, n_ctx, nh):
    nt = n_lat + n_ctx
    tt, tiles = _conv7_tiles(n_lat, n_ctx)
    half = SHORT_CONV // 2

    def body(x_ref, w_ref, o_ref, p_ref):
        _conv7_fill(p_ref, x_ref, n_lat, n_ctx)
        for (row, prow) in tiles:
            acc = jnp.zeros((tt, LANES), F32)
            for kk in range(SHORT_CONV):
                acc = acc + w_ref[kk:kk + 1, :] * p_ref[pl.ds(prow + kk - half, tt), :]
            o_ref[pl.ds(row, tt), :] = _silu(acc)

    return pl.pallas_call(
        body, grid=(3 * nh,),
        in_specs=[pl.BlockSpec((nt, LANES), lambda j: (0, j)), pl.BlockSpec((SHORT_CONV, LANES), lambda j: (0, j))],
        out_specs=pl.BlockSpec((None, None, nt, LANES), lambda j: (j // nh, j % nh, 0, 0)),
        out_shape=jax.ShapeDtypeStruct((3, nh, nt, LANES), F32),
        scratch_shapes=[pltpu.VMEM((nt + 24, LANES), F32)], name="conv7_fwd",
        compiler_params=_cparams(("arbitrary",), 5 * _nbytes((nt + 24, LANES), F32)),
    )(h, w_conv)


def _conv7_bwd(h, w_conv, dact_f, dact_b, *, n_lat, n_ctx, nh):
    nt = n_lat + n_ctx
    tt, tiles = _conv7_tiles(n_lat, n_ctx)
    half = SHORT_CONV // 2

    def body(x_ref, w_ref, df_ref, db_ref, dx_ref, dw_ref, p_ref, q_ref):
        _conv7_fill(p_ref, x_ref, n_lat, n_ctx)
        z8 = jnp.zeros((8, LANES), F32)
        q_ref[0:8, :] = z8
        q_ref[8 + n_lat:16 + n_lat, :] = z8
        q_ref[16 + n_lat + n_ctx:24 + n_lat + n_ctx, :] = z8
        dw = [jnp.zeros((1, LANES), F32) for _ in range(SHORT_CONV)]
        for (row, prow) in tiles:
            pre = jnp.zeros((tt, LANES), F32)
            for kk in range(SHORT_CONV):
                pre = pre + w_ref[kk:kk + 1, :] * p_ref[pl.ds(prow + kk - half, tt), :]
            s = jax.nn.sigmoid(pre)
            dpre = (df_ref[pl.ds(row, tt), :] + db_ref[pl.ds(row, tt), :]) * (s * (1.0 + pre * (1.0 - s)))
            q_ref[pl.ds(prow, tt), :] = dpre
            for kk in range(SHORT_CONV):
                dw[kk] = dw[kk] + jnp.sum(dpre * p_ref[pl.ds(prow + kk - half, tt), :], axis=0, keepdims=True)
        for (row, prow) in tiles:
            acc = jnp.zeros((tt, LANES), F32)
            for kk in range(SHORT_CONV):
                acc = acc + w_ref[kk:kk + 1, :] * q_ref[pl.ds(prow + half - kk, tt), :]
            dx_ref[pl.ds(row, tt), :] = acc.astype(dx_ref.dtype)
        for kk in range(SHORT_CONV):
            dw_ref[kk:kk + 1, :] = dw[kk]

    dspec = pl.BlockSpec((None, None, nt, LANES), lambda j: (j // nh, j % nh, 0, 0))
    return pl.pallas_call(
        body, grid=(3 * nh,),
        in_specs=[pl.BlockSpec((nt, LANES), lambda j: (0, j)), pl.BlockSpec((SHORT_CONV, LANES), lambda j: (0, j)), dspec, dspec],
        out_specs=[pl.BlockSpec((nt, LANES), lambda j: (0, j)), pl.BlockSpec((SHORT_CONV, LANES), lambda j: (0, j))],
        out_shape=[jax.ShapeDtypeStruct((nt, 3 * nh * LANES), MXU_DTYPE), jax.ShapeDtypeStruct((SHORT_CONV, 3 * nh * LANES), F32)],
        scratch_shapes=[pltpu.VMEM((nt + 24, LANES), F32), pltpu.VMEM((nt + 24, LANES), F32)], name="conv7_bwd",
        compiler_params=_cparams(("arbitrary",), 10 * _nbytes((nt + 24, LANES), F32)),
    )(h, w_conv, dact_f, dact_b)


_RT = 4
_CP = CONF_K // 2


def _conf_pad_shape(gw, vertical):
    return (gw + 2 * _CP, gw, LANES) if vertical else (gw, gw + 32, LANES)


def _conf_fill(p_ref, val, gw, vertical):
    if vertical:
        p_ref[0:_CP] = jnp.zeros((_CP, gw, LANES), F32)
        p_ref[_CP + gw:2 * _CP + gw] = jnp.zeros((_CP, gw, LANES), F32)
        p_ref[_CP:_CP + gw] = val
    else:
        p_ref[:, 0:16, :] = jnp.zeros((gw, 16, LANES), F32)
        p_ref[:, 16 + gw:32 + gw, :] = jnp.zeros((gw, 16, LANES), F32)
        p_ref[:, 16:16 + gw, :] = val


def _conf_window(p_ref, r0, shift, gw, vertical):
    if vertical:
        return p_ref[pl.ds(r0 + _CP + shift, _RT), :, :]
    return p_ref[pl.ds(r0, _RT), pl.ds(16 + shift, gw), :]


def _conf_conv_fwd(h3, w_dw, *, gw, col0, half, vertical):
    cb_val = (col0 + (half if vertical else 0)) // LANES
    cb_gate = cb_val + 2 * half // LANES
    cb_w = (half if vertical else 0) // LANES

    def body(v_ref, g_ref, w_ref, o_ref, p_ref):
        _conf_fill(p_ref, v_ref[...] * jax.nn.sigmoid(g_ref[...]), gw, vertical)

        def step(t, carry):
            r0 = t * _RT
            acc = jnp.zeros((_RT, gw, LANES), F32)
            for kk in range(CONF_K):
                acc = acc + w_ref[kk:kk + 1, :] * _conf_window(p_ref, r0, kk - _CP, gw, vertical)
            o_ref[pl.ds(r0, _RT)] = acc
            return carry

        lax.fori_loop(0, gw // _RT, step, 0)

    blk = (gw, gw, LANES)
    return pl.pallas_call(
        body, grid=(half // LANES,),
        in_specs=[pl.BlockSpec(blk, lambda j: (0, 0, cb_val + j)), pl.BlockSpec(blk, lambda j: (0, 0, cb_gate + j)),
                  pl.BlockSpec((CONF_K, LANES), lambda j: (0, cb_w + j))],
        out_specs=pl.BlockSpec(blk, lambda j: (0, 0, j)),
        out_shape=jax.ShapeDtypeStruct((gw, gw, half), F32),
        scratch_shapes=[pltpu.VMEM(_conf_pad_shape(gw, vertical), F32)],
        name="conf_conv_fwd_v" if vertical else "conf_conv_fwd_h",
        compiler_params=_cparams(("arbitrary",), 8 * _nbytes(_conf_pad_shape(gw, vertical), F32)),
    )(h3, h3, w_dw)


def _conf_conv_bwd(h3, w_dw, dyc, *, gw, col0, half, vertical):
    cb_val = (col0 + (half if vertical else 0)) // LANES
    cb_gate = cb_val + 2 * half // LANES
    cb_w = (half if vertical else 0) // LANES

    def body(v_ref, g_ref, w_ref, d_ref, dv_ref, dg_ref, dw_ref, py_ref, pd_ref):
        _conf_fill(py_ref, v_ref[...] * jax.nn.sigmoid(g_ref[...]), gw, vertical)
        _conf_fill(pd_ref, d_ref[...], gw, vertical)

        def step(t, carry):
            r0 = t * _RT
            acc = jnp.zeros((_RT, gw, LANES), F32)
            for kk in range(CONF_K):
                acc = acc + w_ref[kk:kk + 1, :] * _conf_window(pd_ref, r0, _CP - kk, gw, vertical)
            val = v_ref[pl.ds(r0, _RT)]
            sg = jax.nn.sigmoid(g_ref[pl.ds(r0, _RT)])
            dv_ref[pl.ds(r0, _RT)] = (acc * sg).astype(dv_ref.dtype)
            dg_ref[pl.ds(r0, _RT)] = (acc * val * sg * (1.0 - sg)).astype(dg_ref.dtype)
            return carry

        lax.fori_loop(0, gw // _RT, step, 0)

        for kk in range(CONF_K):
            def wstep(t, acc, kk=kk):
                r0 = t * _RT
                prod = d_ref[pl.ds(r0, _RT)] * _conf_window(py_ref, r0, kk - _CP, gw, vertical)
                return acc + jnp.sum(prod, axis=0)

            tot = lax.fori_loop(0, gw // _RT, wstep, jnp.zeros((gw, LANES), F32))
            dw_ref[kk:kk + 1, :] = jnp.sum(tot, axis=0, keepdims=True)

    blk = (gw, gw, LANES)
    pshape = _conf_pad_shape(gw, vertical)
    return pl.pallas_call(
        body, grid=(half // LANES,),
        in_specs=[pl.BlockSpec(blk, lambda j: (0, 0, cb_val + j)), pl.BlockSpec(blk, lambda j: (0, 0, cb_gate + j)),
                  pl.BlockSpec((CONF_K, LANES), lambda j: (0, cb_w + j)), pl.BlockSpec(blk, lambda j: (0, 0, j))],
        out_specs=[pl.BlockSpec(blk, lambda j: (0, 0, j)), pl.BlockSpec(blk, lambda j: (0, 0, j)),
                   pl.BlockSpec((CONF_K, LANES), lambda j: (0, j))],
        out_shape=[jax.ShapeDtypeStruct((gw, gw, half), MXU_DTYPE), jax.ShapeDtypeStruct((gw, gw, half), MXU_DTYPE),
                   jax.ShapeDtypeStruct((CONF_K, half), F32)],
        scratch_shapes=[pltpu.VMEM(pshape, F32), pltpu.VMEM(pshape, F32)],
        name="conf_conv_bwd_v" if vertical else "conf_conv_bwd_h",
        compiler_params=_cparams(("arbitrary",), 12 * _nbytes(pshape, F32)),
    )(h3, h3, w_dw, dyc)


_INV_BASE = 8


def _tri_inv_raw(mats):
    c = mats[0].shape[0]
    ri = lax.broadcasted_iota(jnp.int32, (c, c), 0)
    ci = lax.broadcasted_iota(jnp.int32, (c, c), 1)
    eye = (ri == ci).astype(F32)
    base = min(_INV_BASE, c)
    same = lambda size: (ri // size) == (ci // size)
    dmat = [jnp.where(same(base), a, 0.0) for a in mats]
    x = [eye - dm for dm in dmat]
    pw = [_raw_dot(dm, dm, "nn", "hi3") for dm in dmat]
    span = 2
    while span < base:
        x = [xi + _raw_dot(xi, p, "nn", "hi3") for xi, p in zip(x, pw)]
        span *= 2
        if span < base:
            pw = [_raw_dot(p, p, "nn", "hi3") for p in pw]
    size = base
    while size < c:
        off = jnp.logical_and(jnp.logical_not(same(size)), same(2 * size))
        t = [_raw_dot(xi, jnp.where(off, a, 0.0), "nn", "hi3") for xi, a in zip(x, mats)]
        x = [xi - _raw_dot(ti, xi, "nn", "hi3") for xi, ti in zip(x, t)]
        size *= 2
    return x


@jax.custom_vjp
def _tri_inv(mats):
    return _tri_inv_raw(mats)


def _tri_inv_fwd(mats):
    x = _tri_inv_raw(mats)
    return x, x


def _tri_inv_bwd(x, dx):
    t = [_raw_dot(xi, di, "tn", "hi3") for xi, di in zip(x, dx)]
    return ([-_raw_dot(ti, xi, "nt", "hi3") for ti, xi in zip(t, x)],)


_tri_inv.defvjp(_tri_inv_fwd, _tri_inv_bwd)


@jax.custom_vjp
def _tri_inv_saved(mats, saved):
    return saved


def _tri_inv_saved_fwd(mats, saved):
    return saved, saved


def _tri_inv_saved_bwd(x, dx):
    return _tri_inv_bwd(x, dx)[0], [jnp.zeros_like(xi) for xi in x]


_tri_inv_saved.defvjp(_tri_inv_saved_fwd, _tri_inv_saved_bwd)


def _delta_chunk(qa, ka, va, graw, braw, alog, dtb, s, *, rev, minv_saved=None, with_minv=False):
    c = qa[0].shape[0]

    def each(f, *lists):
        return [f(*xs) for xs in zip(*lists, strict=True)]

    def l2n(t):
        return t * lax.rsqrt(jnp.sum(t * t, axis=-1, keepdims=True) + 1e-6)

    ri = lax.broadcasted_iota(jnp.int32, (c, c), 0)
    ci = lax.broadcasted_iota(jnp.int32, (c, c), 1)
    incl = (ci >= ri) if rev else (ci <= ri)
    strict = (ci > ri) if rev else (ci < ri)
    tmat = incl.astype(F32)
    tmat_t = ((ri >= ci) if rev else (ri <= ci)).astype(F32)

    q = each(lambda t: l2n(t) * (HD ** -0.5), qa)
    k = each(l2n, ka)
    g = each(lambda al, gr, dt: -jnp.exp(al) * _softplus(gr + dt), alog, graw, dtb)
    beta = each(jax.nn.sigmoid, braw)
    gc_wide = each(lambda t: _xdot(tmat, jnp.broadcast_to(t, (c, HD)), "nn", "a"), g)
    gc_rows = gc_wide if c == HD else each(lambda t: _xdot(tmat, jnp.broadcast_to(t, (c, c)), "nn", "a"), g)
    gc_cols = each(lambda t: _xdot(jnp.broadcast_to(t, (c, c)), tmat_t, "tn", "b"), g)
    gamma = each(lambda r, cc: jnp.where(incl, jnp.exp(jnp.where(incl, r - cc, 0.0)), 0.0), gc_rows, gc_cols)
    kb = each(lambda a, b: a * b, k, beta)
    a_mat = each(lambda a, b, gm: jnp.where(strict, _dot(a, b, "nt", "mxu") * gm, 0.0), kb, k, gamma)
    minv = _tri_inv(a_mat) if minv_saved is None else _tri_inv_saved(a_mat, minv_saved)
    eg = each(jnp.exp, gc_wide)
    u = each(lambda mi, v, b: _dot(mi, v * b, "nn", "hi3"), minv, va, beta)
    w = each(lambda mi, a, e: _dot(mi, a * e, "nn", "hi3"), minv, kb, eg)
    attn = each(lambda a, b, gm: _dot(a, b, "nt", "mxu") * gm, q, k, gamma)
    tot = each(lambda t: jnp.sum(t, axis=0, keepdims=True), g)
    q_dec = each(lambda a, e: a * e, q, eg)
    k_dec = each(lambda a, t, gw_: a * jnp.exp(t - gw_), k, tot, gc_wide)
    v_new = each(lambda uu, ww, ss: uu - _dot(ww, ss, "nn", "mxu"), u, w, s)
    o = each(lambda qd, ss, at, vn: _dot(qd, ss, "nn", "mxu") + _dot(at, vn, "nn", "mxu"), q_dec, s, attn, v_new)
    s_new = each(lambda ss, t, kd, vn: ss * jnp.exp(t) + _dot(kd, vn, "tn", "mxu"), s, tot, k_dec, v_new)
    return (o, s_new, minv) if with_minv else (o, s_new)


def _delta_args(a_ref, g_ref, al_ref, dt_ref, s_ref, hb, gcol):
    gts = [g_ref[hh] for hh in range(hb)]
    return ([a_ref[0, hh] for hh in range(hb)], [a_ref[1, hh] for hh in range(hb)], [a_ref[2, hh] for hh in range(hb)],
            [t[:, gcol:gcol + 1] for t in gts], [t[:, gcol + 1:gcol + 2] for t in gts],
            [al_ref[hh, 0:1, 0:1] for hh in range(hb)], [dt_ref[hh, 0:1, 0:1] for hh in range(hb)],
            [s_ref[hh] for hh in range(hb)])


def _delta_chunk_of_step(step, nch, nlc, rev):
    return (nch - 1 - step) if rev else (step + nlc) % nch


def _delta_fwd(act, gates_hm, alog_b, dtb_b, *, n_lat, n_ctx, nh, hb, rev):
    nt = n_lat + n_ctx
    nch, nlc = nt // CHUNK, n_lat // CHUNK
    gcol = 2 if rev else 0
    chunk = functools.partial(_delta_chunk_of_step, nch=nch, nlc=nlc, rev=rev)

    def body(a_ref, g_ref, al_ref, dt_ref, o_ref, st_ref, mi_ref, s_ref):
        n = pl.program_id(1)

        @pl.when(n == 0)
        def _():
            s_ref[...] = jnp.zeros_like(s_ref)

        args = _delta_args(a_ref, g_ref, al_ref, dt_ref, s_ref, hb, gcol)
        o, s_new, minv = _delta_chunk(*args, rev=rev, with_minv=True)
        for hh in range(hb):
            st_ref[hh] = args[7][hh]
            mi_ref[hh] = minv[hh]
            o_ref[:, hh * HD:(hh + 1) * HD] = o[hh]
            s_ref[hh] = s_new[hh]

    par = pl.BlockSpec((hb, 8, LANES), lambda h, n: (h, 0, 0))
    return pl.pallas_call(
        body, grid=(nh // hb, nch),
        in_specs=[pl.BlockSpec((3, hb, CHUNK, HD), lambda h, n: (0, h, chunk(n), 0)),
                  pl.BlockSpec((hb, CHUNK, 4), lambda h, n: (h, chunk(n), 0)), par, par],
        out_specs=[pl.BlockSpec((CHUNK, hb * HD), lambda h, n: (chunk(n), h)),
                   pl.BlockSpec((hb, None, HD, HD), lambda h, n: (h, n, 0, 0)),
                   pl.BlockSpec((hb, None, CHUNK, CHUNK), lambda h, n: (h, n, 0, 0))],
        out_shape=[jax.ShapeDtypeStruct((nt, nh * HD), F32), jax.ShapeDtypeStruct((nh, nch, HD, HD), F32),
                   jax.ShapeDtypeStruct((nh, nch, CHUNK, CHUNK), F32)],
        scratch_shapes=[pltpu.VMEM((hb, HD, HD), F32)], name="delta_fwd_rev" if rev else "delta_fwd",
        compiler_params=_cparams(("arbitrary", "arbitrary"), 0),
    )(act, gates_hm, alog_b, dtb_b)


def _delta_bwd(act, gates_hm, alog_b, dtb_b, states, minvs, do, *, n_lat, n_ctx, nh, hb, rev):
    nt = n_lat + n_ctx
    nch, nlc = nt // CHUNK, n_lat // CHUNK
    gcol = 2 if rev else 0

    def chunk(m):
        return _delta_chunk_of_step(nch - 1 - m, nch, nlc, rev)

    def body(a_ref, g_ref, al_ref, dt_ref, st_ref, mi_ref, do_ref, da_ref, dg_ref, dal_ref, ddt_ref, ds_ref):
        m = pl.program_id(1)

        @pl.when(m == 0)
        def _():
            ds_ref[...] = jnp.zeros_like(ds_ref)
            dal_ref[...] = jnp.zeros_like(dal_ref)
            ddt_ref[...] = jnp.zeros_like(ddt_ref)

        is_lat = chunk(m) < nlc
        args = _delta_args(a_ref, g_ref, al_ref, dt_ref, st_ref, hb, gcol)
        _, vjp = jax.vjp(functools.partial(_delta_chunk, rev=rev, minv_saved=[mi_ref[hh] for hh in range(hb)]), *args)
        do_h = [jnp.where(is_lat, do_ref[:, hh * HD:(hh + 1) * HD], 0.0) for hh in range(hb)]
        grads = vjp((do_h, [ds_ref[hh] for hh in range(hb)]))
        for hh in range(hb):
            dq, dk, dv, dgr, dbr, dal, ddt, ds = [gr[hh] for gr in grads]
            da_ref[0, hh] = dq
            da_ref[1, hh] = dk
            da_ref[2, hh] = dv
            lane = lax.broadcasted_iota(jnp.int32, (CHUNK, 4), 1)
            dg_ref[hh] = jnp.where(lane == 0, dgr, jnp.where(lane == 1, dbr, 0.0))
            dal_ref[hh] += jnp.broadcast_to(dal, (8, LANES))
            ddt_ref[hh] += jnp.broadcast_to(ddt, (8, LANES))
            ds_ref[hh] = ds

    par = pl.BlockSpec((hb, 8, LANES), lambda h, m: (h, 0, 0))
    return pl.pallas_call(
        body, grid=(nh // hb, nch),
        in_specs=[pl.BlockSpec((3, hb, CHUNK, HD), lambda h, m: (0, h, chunk(m), 0)),
                  pl.BlockSpec((hb, CHUNK, 4), lambda h, m: (h, chunk(m), 0)), par, par,
                  pl.BlockSpec((hb, None, HD, HD), lambda h, m: (h, nch - 1 - m, 0, 0)),
                  pl.BlockSpec((hb, None, CHUNK, CHUNK), lambda h, m: (h, nch - 1 - m, 0, 0)),
                  pl.BlockSpec((CHUNK, hb * HD), lambda h, m: (jnp.minimum(chunk(m), nlc - 1), h))],
        out_specs=[pl.BlockSpec((3, hb, CHUNK, HD), lambda h, m: (0, h, chunk(m), 0)),
                   pl.BlockSpec((hb, CHUNK, 4), lambda h, m: (h, chunk(m), 0)), par, par],
        out_shape=[jax.ShapeDtypeStruct((3, nh, nt, HD), F32), jax.ShapeDtypeStruct((nh, nt, 4), F32),
                   jax.ShapeDtypeStruct((nh, 8, LANES), F32), jax.ShapeDtypeStruct((nh, 8, LANES), F32)],
        scratch_shapes=[pltpu.VMEM((hb, HD, HD), F32)], name="delta_bwd_rev" if rev else "delta_bwd",
        compiler_params=_cparams(("arbitrary", "arbitrary"), 0),
    )(act, gates_hm, alog_b, dtb_b, states, minvs, do)


def _my_pos():
    return lax.axis_index("x"), lax.axis_index("y"), lax.axis_index("c")


def _flip(v, d):
    return 1 - v if d else v


def _bcast8(block, name):
    r, w = block.shape
    assert r % 8 == 0 and w % LANES == 0

    def body(x_ref, o_ref, send_sems, recv_sems, local_sem):
        x, y, c = _my_pos()

        def rows(px, py, pc):
            return o_ref.at[pl.ds((4 * px + 2 * py + pc) * r, r), :]

        mine = pltpu.make_async_copy(x_ref, rows(x, y, c), local_sem)
        mine.start()
        copies = []
        for kk in range(1, 8):
            dx, dy, dc = (kk >> 2) & 1, (kk >> 1) & 1, kk & 1
            peer = (_flip(x, dx), _flip(y, dy), _flip(c, dc))
            cp = pltpu.make_async_remote_copy(src_ref=x_ref, dst_ref=rows(x, y, c), send_sem=send_sems.at[kk - 1],
                                              recv_sem=recv_sems.at[kk - 1], device_id=peer, device_id_type=pl.DeviceIdType.MESH)
            cp.start()
            copies.append((cp, peer))
        for kk, (cp, peer) in enumerate(copies):
            pltpu.make_async_remote_copy(src_ref=x_ref, dst_ref=rows(*peer), send_sem=send_sems.at[kk], recv_sem=recv_sems.at[kk],
                                         device_id=peer, device_id_type=pl.DeviceIdType.MESH).wait_recv()
        for cp, _ in copies:
            cp.wait_send()
        mine.wait()

    return pl.pallas_call(
        body, out_shape=jax.ShapeDtypeStruct((8 * r, w), block.dtype),
        in_specs=[pl.BlockSpec(memory_space=pltpu.VMEM)], out_specs=pl.BlockSpec(memory_space=pltpu.VMEM),
        scratch_shapes=[pltpu.SemaphoreType.DMA((7,)), pltpu.SemaphoreType.DMA((7,)), pltpu.SemaphoreType.DMA], name=name,
        compiler_params=pltpu.CompilerParams(vmem_limit_bytes=int(max(32 * 1024 * 1024, 12 * _nbytes((r, w), block.dtype)))),
    )(block)


_CHIP_PEERS = ((1, 0), (0, 1), (1, 1))


def _col_half(ref, c, lead):
    hc = ref.shape[-1] // 2
    return ref.at[(*lead, slice(None), pl.ds(pl.multiple_of(c * hc, LANES), hc))]


def _into_slot(w, chip_idx, after, name):
    r, cc = w.shape
    tr = _pick(r, 512, 16)

    def body(i_ref, w_ref, after_ref, o_ref):
        o_ref[...] = w_ref[...].astype(o_ref.dtype)

    grid_spec = pltpu.PrefetchScalarGridSpec(
        num_scalar_prefetch=1, grid=(r // tr,), in_specs=[pl.BlockSpec((tr, cc), lambda i, s: (i, 0)), ANY],
        out_specs=pl.BlockSpec((None, tr, cc), lambda i, s: (s[0], i, 0)))
    return pl.pallas_call(body, grid_spec=grid_spec, out_shape=jax.ShapeDtypeStruct((4, r, cc), MXU_DTYPE), name=name,
                          compiler_params=_cparams(("arbitrary",), 6 * _nbytes((tr, cc), F32)))(chip_idx, w, after)


_HBM = pl.BlockSpec(memory_space=pltpu.HBM)
_SEM = pl.BlockSpec(memory_space=pltpu.SEMAPHORE)
_DATAFLOW = pltpu.SideEffectType.DATAFLOW_SIDE_EFFECTING


def _in_hbm(a):
    return pltpu.with_memory_space_constraint(a, pltpu.HBM)


def _gather_start(bufs, after, name):
    n = len(bufs)

    def body(*refs):
        ins, send_sems, recv_sems, token = refs[:n], refs[n + 1], refs[n + 2], refs[-1]
        x, y, c = _my_pos()
        for a in range(n):
            piece = _col_half(ins[a], c, (2 * x + y,))
            for kk, (dx, dy) in enumerate(_CHIP_PEERS):
                pltpu.make_async_remote_copy(src_ref=piece, dst_ref=piece, send_sem=send_sems.at[3 * a + kk], recv_sem=recv_sems.at[3 * a + kk],
                                             device_id=(_flip(x, dx), _flip(y, dy), c), device_id_type=pl.DeviceIdType.MESH).start()
        token[...] = jnp.zeros_like(token)

    res = pl.pallas_call(
        body, name=name,
        out_shape=(pltpu.SemaphoreType.DMA((3 * n,)), pltpu.SemaphoreType.DMA((3 * n,)), *[pltpu.HBM(b.shape, b.dtype) for b in bufs],
                   jax.ShapeDtypeStruct((8, LANES), F32)),
        in_specs=[_HBM] * n + [ANY], out_specs=(_SEM, _SEM, *[_HBM] * n, pl.BlockSpec(memory_space=pltpu.VMEM)),
        input_output_aliases={a: 2 + a for a in range(n)}, compiler_params=pltpu.CompilerParams(has_side_effects=_DATAFLOW),
    )(*[_in_hbm(b) for b in bufs], after)
    return res[0], res[1], list(res[2:2 + n]), res[-1]


def _gather_wait(send_sems, recv_sems, bufs, after, name):
    n = len(bufs)

    def body(*refs):
        ins, s_sems, r_sems = refs[:n], refs[n], refs[n + 1]
        x, y, c = _my_pos()
        for a in range(n):
            for kk, (dx, dy) in enumerate(_CHIP_PEERS):
                px, py = _flip(x, dx), _flip(y, dy)
                cp = pltpu.make_async_remote_copy(src_ref=_col_half(ins[a], c, (2 * x + y,)), dst_ref=_col_half(ins[a], c, (2 * px + py,)),
                                                  send_sem=s_sems.at[3 * a + kk], recv_sem=r_sems.at[3 * a + kk], device_id=(px, py, c),
                                                  device_id_type=pl.DeviceIdType.MESH)
                cp.wait_send()
                cp.wait_recv()

    return pl.pallas_call(
        body, name=name, out_shape=[pltpu.HBM(b.shape, b.dtype) for b in bufs],
        in_specs=[_HBM] * n + [_SEM, _SEM] + [ANY] * len(after), out_specs=[_HBM] * n, input_output_aliases={a: a for a in range(n)},
        compiler_params=pltpu.CompilerParams(has_side_effects=_DATAFLOW),
    )(*bufs, send_sems, recv_sems, *after)


def _gather_forward(bufs, name):
    n = len(bufs)

    def body(*refs):
        outs = refs[n:2 * n]
        send_sems, recv_sems = refs[2 * n:]
        x, y, c = _my_pos()
        sib = (x, y, 1 - c)

        def remote(a, kk, half):
            dx, dy = _CHIP_PEERS[kk]
            piece = _col_half(outs[a], half, (2 * _flip(x, dx) + _flip(y, dy),))
            return pltpu.make_async_remote_copy(src_ref=piece, dst_ref=piece, send_sem=send_sems.at[3 * a + kk], recv_sem=recv_sems.at[3 * a + kk],
                                                device_id=sib, device_id_type=pl.DeviceIdType.MESH)

        sends = [remote(a, kk, c) for a in range(n) for kk in range(3)]
        for cp in sends:
            cp.start()
        for a in range(n):
            for kk in range(3):
                remote(a, kk, 1 - c).wait_recv()
        for cp in sends:
            cp.wait_send()

    return pl.pallas_call(
        body, out_shape=[jax.ShapeDtypeStruct(b.shape, b.dtype) for b in bufs],
        in_specs=[ANY] * n, out_specs=[ANY] * n, input_output_aliases={a: a for a in range(n)},
        scratch_shapes=[pltpu.SemaphoreType.DMA((3 * n,))] * 2, name=name,
    )(*bufs)


def _scatter_start(parts, after, name):
    n = len(parts)

    def body(*refs):
        ins, lands, send_sems, recv_sems, token = refs[:n], refs[n:2 * n], refs[2 * n + 1], refs[2 * n + 2], refs[-1]
        x, y, c = _my_pos()
        for a in range(n):
            for kk, (dx, dy) in enumerate(_CHIP_PEERS):
                px, py = _flip(x, dx), _flip(y, dy)
                pltpu.make_async_remote_copy(src_ref=ins[a].at[2 * px + py], dst_ref=lands[a].at[2 * x + y], send_sem=send_sems.at[3 * a + kk],
                                             recv_sem=recv_sems.at[3 * a + kk], device_id=(px, py, c), device_id_type=pl.DeviceIdType.MESH).start()
        token[...] = jnp.zeros_like(token)

    thru = [pltpu.HBM(p.shape, p.dtype) for p in parts]
    res = pl.pallas_call(
        body, name=name,
        out_shape=(pltpu.SemaphoreType.DMA((3 * n,)), pltpu.SemaphoreType.DMA((3 * n,)), *thru, *thru, jax.ShapeDtypeStruct((8, LANES), F32)),
        in_specs=[_HBM] * (2 * n) + [ANY], out_specs=(_SEM, _SEM, *[_HBM] * (2 * n), pl.BlockSpec(memory_space=pltpu.VMEM)),
        input_output_aliases={a: 2 + a for a in range(2 * n)}, compiler_params=pltpu.CompilerParams(has_side_effects=_DATAFLOW),
    )(*[_in_hbm(p) for p in parts], *[_in_hbm(lax.empty(p.shape, p.dtype)) for p in parts], after)
    return res[0], res[1], list(res[2:2 + n]), list(res[2 + n:2 + 2 * n]), res[-1]


def _scatter_wait(send_sems, recv_sems, parts, lands, after, name):
    n = len(parts)

    def body(*refs):
        ins, lnd, s_sems, r_sems = refs[:n], refs[n:2 * n], refs[2 * n], refs[2 * n + 1]
        x, y, c = _my_pos()
        for a in range(n):
            for kk, (dx, dy) in enumerate(_CHIP_PEERS):
                px, py = _flip(x, dx), _flip(y, dy)
                cp = pltpu.make_async_remote_copy(src_ref=ins[a].at[2 * px + py], dst_ref=lnd[a].at[2 * px + py], send_sem=s_sems.at[3 * a + kk],
                                                  recv_sem=r_sems.at[3 * a + kk], device_id=(px, py, c), device_id_type=pl.DeviceIdType.MESH)
                cp.wait_send()
                cp.wait_recv()

    thru = [pltpu.HBM(p.shape, p.dtype) for p in parts]
    res = pl.pallas_call(
        body, name=name, out_shape=[*thru, *thru],
        in_specs=[_HBM] * (2 * n) + [_SEM, _SEM] + [ANY] * len(after), out_specs=[_HBM] * (2 * n),
        input_output_aliases={a: a for a in range(2 * n)}, compiler_params=pltpu.CompilerParams(has_side_effects=_DATAFLOW),
    )(*parts, *lands, send_sems, recv_sems, *after)
    return list(res[:n]), list(res[n:])


def _pair_send_start(grads, after, name):
    n = len(grads)

    def body(*refs):
        ins, lands, send_sems, recv_sems, token = refs[:n], refs[n:2 * n], refs[2 * n + 1], refs[2 * n + 2], refs[-1]
        x, y, c = _my_pos()
        for a in range(n):
            pltpu.make_async_remote_copy(src_ref=_col_half(ins[a], 1 - c, (slice(None),)), dst_ref=lands[a], send_sem=send_sems.at[a],
                                         recv_sem=recv_sems.at[a], device_id=(x, y, 1 - c), device_id_type=pl.DeviceIdType.MESH).start()
        token[...] = jnp.zeros_like(token)

    land_shapes = [(4, g.shape[1], g.shape[2] // 2) for g in grads]
    res = pl.pallas_call(
        body, name=name,
        out_shape=(pltpu.SemaphoreType.DMA((n,)), pltpu.SemaphoreType.DMA((n,)), *[pltpu.HBM(g.shape, g.dtype) for g in grads],
                   *[pltpu.HBM(s, g.dtype) for s, g in zip(land_shapes, grads)], jax.ShapeDtypeStruct((8, LANES), F32)),
        in_specs=[_HBM] * (2 * n) + [ANY], out_specs=(_SEM, _SEM, *[_HBM] * (2 * n), pl.BlockSpec(memory_space=pltpu.VMEM)),
        input_output_aliases={a: 2 + a for a in range(2 * n)}, compiler_params=pltpu.CompilerParams(has_side_effects=_DATAFLOW),
    )(*[_in_hbm(g) for g in grads], *[_in_hbm(lax.empty(s, g.dtype)) for s, g in zip(land_shapes, grads)], after)
    return res[0], res[1], list(res[2:2 + n]), list(res[2 + n:2 + 2 * n]), res[-1]


def _pair_send_wait(send_sems, recv_sems, grads, lands, after, name):
    n = len(grads)

    def body(*refs):
        ins, lnd, s_sems, r_sems = refs[:n], refs[n:2 * n], refs[2 * n], refs[2 * n + 1]
        x, y, c = _my_pos()
        for a in range(n):
            cp = pltpu.make_async_remote_copy(src_ref=_col_half(ins[a], 1 - c, (slice(None),)), dst_ref=lnd[a], send_sem=s_sems.at[a],
                                              recv_sem=r_sems.at[a], device_id=(x, y, 1 - c), device_id_type=pl.DeviceIdType.MESH)
            cp.wait_send()
            cp.wait_recv()

    res = pl.pallas_call(
        body, name=name, out_shape=[*[pltpu.HBM(g.shape, g.dtype) for g in grads], *[pltpu.HBM(l.shape, l.dtype) for l in lands]],
        in_specs=[_HBM] * (2 * n) + [_SEM, _SEM, ANY], out_specs=[_HBM] * (2 * n), input_output_aliases={a: a for a in range(2 * n)},
        compiler_params=pltpu.CompilerParams(has_side_effects=_DATAFLOW),
    )(*grads, *lands, send_sems, recv_sems, after)
    return list(res[:n]), list(res[n:])


def _pair_join_halves(bufs, name):
    n = len(bufs)

    def body(*refs):
        outs = refs[n:2 * n]
        send_sems, recv_sems = refs[2 * n:]
        x, y, c = _my_pos()
        sib = (x, y, 1 - c)
        sends = []
        for a in range(n):
            mine = _col_half(outs[a], c, ())
            cp = pltpu.make_async_remote_copy(src_ref=mine, dst_ref=mine, send_sem=send_sems.at[a], recv_sem=recv_sems.at[a],
                                              device_id=sib, device_id_type=pl.DeviceIdType.MESH)
            cp.start()
            sends.append(cp)
        for a in range(n):
            other = _col_half(outs[a], 1 - c, ())
            pltpu.make_async_remote_copy(src_ref=other, dst_ref=other, send_sem=send_sems.at[a], recv_sem=recv_sems.at[a],
                                         device_id=sib, device_id_type=pl.DeviceIdType.MESH).wait_recv()
        for cp in sends:
            cp.wait_send()

    return pl.pallas_call(
        body, out_shape=[jax.ShapeDtypeStruct(b.shape, b.dtype) for b in bufs],
        in_specs=[ANY] * n, out_specs=[ANY] * n, input_output_aliases={a: a for a in range(n)},
        scratch_shapes=[pltpu.SemaphoreType.DMA((n,)), pltpu.SemaphoreType.DMA((n,))], name=name,
    )(*bufs)


def _add_my_half(g, recv, c_idx, name):
    _, r, cc = g.shape
    hc = cc // 2
    tc = _pick(hc, 256)
    per = hc // tc

    def body(c_ref, g_ref, r_ref, o_ref):
        o_ref[...] = (g_ref[...] + r_ref[...]).astype(o_ref.dtype)

    grid_spec = pltpu.PrefetchScalarGridSpec(
        num_scalar_prefetch=1, grid=(4, per),
        in_specs=[pl.BlockSpec((None, r, tc), lambda s, j, c_ref: (s, 0, c_ref[0] * per + j)),
                  pl.BlockSpec((None, r, tc), lambda s, j, c_ref: (s, 0, j))],
        out_specs=pl.BlockSpec((None, r, tc), lambda s, j, c_ref: (s, 0, j)))
    return pl.pallas_call(body, grid_spec=grid_spec, out_shape=jax.ShapeDtypeStruct((4, r, hc), WIRE_DTYPE), name=name,
                          compiler_params=_cparams(("arbitrary", "arbitrary"), 6 * _nbytes((r, tc), F32)))(c_idx, g, recv)


def _sum_slots(own, recv, slots, name):
    _, r, hc = own.shape
    tc = _pick(hc, 256)
    per = hc // tc

    def body(s0, s1, s2, s3, s4, a_ref, b_ref, c_ref, d_ref, o_ref):
        f = lambda ref: ref[...].astype(F32)
        o_ref[...] = ((f(a_ref) + f(b_ref)) + f(c_ref)) + f(d_ref)

    slot = lambda i: pl.BlockSpec((None, r, tc), lambda j, *s: (s[i][0], 0, j))
    grid_spec = pltpu.PrefetchScalarGridSpec(
        num_scalar_prefetch=5, grid=(per,), in_specs=[slot(0), slot(1), slot(2), slot(3)],
        out_specs=pl.BlockSpec((r, tc), lambda j, *s: (0, s[4][0] * per + j)))
    return pl.pallas_call(body, grid_spec=grid_spec, out_shape=jax.ShapeDtypeStruct((r, 2 * hc), F32), name=name,
                          compiler_params=_cparams(("arbitrary",), 14 * _nbytes((r, tc), F32)))(*slots, own, recv, recv, recv)


def _adamw_math(w, g, m, v):
    m = ADAM_B1 * m + (1.0 - ADAM_B1) * g
    v = ADAM_B2 * v + (1.0 - ADAM_B2) * (g * g)
    m_hat = m / (1.0 - ADAM_B1 ** ADAM_STEP)
    v_hat = v / (1.0 - ADAM_B2 ** ADAM_STEP)
    delta = -ADAM_LR * (m_hat / (jnp.sqrt(v_hat) + ADAM_EPS) + ADAM_WD * w)
    return delta, m, v


def _adamw(w, g, m, v, name):
    r, cc = w.shape
    tr = _pick(r, max(8, (512 * 1024) // max(cc, 1)), 8)

    def body(w_ref, g_ref, m_ref, v_ref, d_ref, nm_ref, nv_ref):
        d, nm, nv = _adamw_math(w_ref[...], g_ref[...], m_ref[...], v_ref[...])
        d_ref[...] = d
        nm_ref[...] = nm
        nv_ref[...] = nv

    spec = pl.BlockSpec((tr, cc), lambda i: (i, 0))
    return pl.pallas_call(
        body, grid=(r // tr,), in_specs=[spec] * 4, out_specs=[spec] * 3,
        out_shape=[jax.ShapeDtypeStruct((r, cc), F32)] * 3, name=name,
        compiler_params=_cparams(("arbitrary",), 16 * _nbytes((tr, cc), F32)))(w, g, m, v)


def _sum8(allv, name):
    r8, w = allv.shape
    r = r8 // 8

    def body(a_ref, o_ref):
        acc = a_ref[0:r, :]
        for d in range(1, 8):
            acc = acc + a_ref[d * r:(d + 1) * r, :]
        o_ref[...] = acc

    return pl.pallas_call(body, out_shape=jax.ShapeDtypeStruct((r, w), F32), name=name,
                          compiler_params=_cparams((), 12 * _nbytes((r8, w), F32)))(allv)


def _dmod_prep(alld, name):
    _, w = alld.shape
    tw = _pick(w, 2048)

    def body(a_ref, d_ref, b_ref):
        ctx = a_ref[1:2, :]
        tot = a_ref[0:1, :]
        d_ref[0:1, :] = tot
        for d in range(1, 8):
            ctx = ctx + a_ref[8 * d + 1:8 * d + 2, :]
            tot = tot + a_ref[8 * d:8 * d + 1, :]
            d_ref[d:d + 1, :] = a_ref[8 * d:8 * d + 1, :]
        d_ref[8:9, :] = ctx
        d_ref[9:16, :] = jnp.zeros((7, tw), F32)
        b_ref[...] = tot + ctx

    return pl.pallas_call(
        body, grid=(w // tw,), in_specs=[pl.BlockSpec((64, tw), lambda j: (0, j))],
        out_specs=[pl.BlockSpec((16, tw), lambda j: (0, j)), pl.BlockSpec((1, tw), lambda j: (0, j))],
        out_shape=[jax.ShapeDtypeStruct((16, w), F32), jax.ShapeDtypeStruct((1, w), F32)], name=name,
        compiler_params=_cparams(("arbitrary",), 0))(alld)


def _pack(arrs, rows=8):
    flat = jnp.concatenate([a.reshape(-1).astype(F32) for a in arrs])
    per = rows * LANES
    n = flat.shape[0]
    padded = -(-n // per) * per
    flat = jnp.pad(flat, (0, padded - n))
    return flat.reshape(rows, padded // rows)


def _unpack(slab, shapes):
    flat = slab.reshape(-1)
    out, off = [], 0
    for s in shapes:
        n = math.prod(s)
        out.append(flat[off:off + n].reshape(s))
        off += n
    return out


def kernel(x, c, ctx, c_ctx, ln_in_g, ln_in_b, w_mod, b_mod, w_in, w_qkv_conv, a_log_f, dt_bias_f, a_log_b, dt_bias_b, dn_norm_g, conf_dw_w, conf_dw_b, conf_ln_g, conf_ln_b, w_out, ln1_g, ln1_b, w_mlp1, b_mlp1, w_mlp2, b_mlp2, ln2_g, ln2_b, loss_target, m_c_ctx, m_ln_in_g, m_ln_in_b, m_w_mod, m_b_mod, m_w_in, m_w_qkv_conv, m_a_log_f, m_dt_bias_f, m_a_log_b, m_dt_bias_b, m_dn_norm_g, m_conf_dw_w, m_conf_dw_b, m_conf_ln_g, m_conf_ln_b, m_w_out, m_ln1_g, m_ln1_b, m_w_mlp1, m_b_mlp1, m_w_mlp2, m_b_mlp2, m_ln2_g, m_ln2_b, v_c_ctx, v_ln_in_g, v_ln_in_b, v_w_mod, v_b_mod, v_w_in, v_w_qkv_conv, v_a_log_f, v_dt_bias_f, v_a_log_b, v_dt_bias_b, v_dn_norm_g, v_conf_dw_w, v_conf_dw_b, v_conf_ln_g, v_conf_ln_b, v_w_out, v_ln1_g, v_ln1_b, v_w_mlp1, v_b_mlp1, v_w_mlp2, v_b_mlp2, v_ln2_g, v_ln2_b):
    n_lat, d = x.shape[1], x.shape[2]
    n_ctx = ctx.shape[1]
    nt = n_lat + n_ctx
    dn = d // 2
    nh = dn // HD
    conf = d - dn
    half = conf // 2
    gw = math.isqrt(n_lat)
    ff = 4 * w_mlp1.shape[2]
    fl = w_mlp1.shape[2]
    in_cols = 4 * w_in.shape[2]
    r_in = w_in.shape[2]
    n_gate = 4 * nh
    gb = max(LANES, d // 8)
    g0 = 4 * dn + 2 * conf
    in_pad = g0 + gb
    ml = w_mod.shape[2]
    assert gw * gw == n_lat and nt % gw == 0 and in_cols == 4 * dn + n_gate + 2 * conf and nh * HD == dn
    hb = 4 if nh % 4 == 0 else (2 if nh % 2 == 0 else 1)

    mx, my, mc = _my_pos()
    chip = 2 * mx + my
    dev = 4 * mx + 2 * my + mc
    c_idx = jnp.reshape(mc, (1,)).astype(jnp.int32)

    x2, ctx2, tgt2 = x[0], ctx[0], loss_target[0]
    row = lambda a: a.reshape(1, -1)

    e0_shapes = [(d,), w_qkv_conv.shape[1:], conf_dw_w.shape[1:]]
    e0 = _bcast8(_pack([c[0], w_qkv_conv[0], conf_dw_w[0]]), "bcast_inputs")
    e0 = e0.reshape(8, -1)
    per_dev = [_unpack(e0[dd], e0_shapes) for dd in range(8)]
    all_c = jnp.stack([p[0] for p in per_dev])
    w_conv = jnp.concatenate([per_dev[2 * j][1] for j in range(4)], axis=1)
    w_dw = jnp.concatenate([per_dev[2 * j][2] for j in range(4)], axis=1)

    s16 = jnp.concatenate([all_c, c_ctx[None, :], jnp.zeros((7, d), F32)], axis=0)
    b_mod_loc = lax.dynamic_slice(b_mod, (0, chip * ml), (1, ml))
    to_mxu = lambda t: t.astype(MXU_DTYPE)
    modblk = _matmul(s16, w_mod[0], form="nn", m=16, n=ml, k=d, tm=16, tn=_pick(ml, 1024), tk=_pick(d, 1024),
                     out_dtype=F32, name="mod_fwd", bias=b_mod_loc, a_fn=lambda t: to_mxu(_silu(t)), b_fn=to_mxu)
    allmod = _bcast8(modblk, "bcast_mod").reshape(8, 16, ml)[0::2]
    mod_mine = lax.dynamic_index_in_dim(allmod, dev, axis=1, keepdims=False).reshape(1, 4 * ml)
    mod_ctx = allmod[:, 8, :].reshape(1, 4 * ml)
    sh_a, sc_a, g_a, sh_m, sc_m, g_m = [mod_mine[:, i * d:(i + 1) * d] for i in range(6)]
    csh_a, csc_a = mod_ctx[:, 0:d], mod_ctx[:, d:2 * d]

    chip_idx = jnp.reshape(chip, (1,)).astype(jnp.int32)
    win_t_loc = jnp.transpose(w_in[0]).astype(MXU_DTYPE)
    slot_in = _into_slot(win_t_loc, chip_idx, all_c, "weights_into_slot_0")
    gi_send, gi_recv, gi_bufs, gi_token = _gather_start([slot_in], allmod, "gather_w_in_start")
    slots_rest = [_into_slot(w, chip_idx, gi_token, f"weights_into_slot_{i + 1}") for i, w in enumerate([w_out[0], w_mlp1[0], w_mlp2[0]])]
    sh_a, csh_a = sh_a + gi_token[0:1, 0:1], csh_a + gi_token[0:1, 0:1]

    tmr = _pick(n_lat, 128, 8)
    tmc = _pick(n_ctx, 128, 8)
    ln_in = [row(ln_in_g), row(ln_in_b)]

    def stage_a(rv, pv):
        x0 = _ln(rv[0], pv[0], pv[1])
        return [x0 * (1.0 + pv[3]) + pv[2]], []

    xm_lat, = _rowwise(stage_a, [(x2, d, 0, 0)], ln_in + [sh_a, sc_a], [(d, MXU_DTYPE)], [], n_rows=n_lat, tm=tmr, name="ln_mod_lat")
    xm_ctx, = _rowwise(stage_a, [(ctx2, d, 0, 0)], ln_in + [csh_a, csc_a], [(d, MXU_DTYPE)], [], n_rows=n_ctx, tm=tmc, name="ln_mod_ctx")
    xm = jnp.concatenate([xm_lat, xm_ctx], axis=0)

    gw_in, = _gather_forward(_gather_wait(gi_send, gi_recv, gi_bufs, [xm] + slots_rest, "gather_w_in_wait"), "gather_w_in_forward")
    gs_send, gs_recv, gs_bufs, gs_token = _gather_start(slots_rest, gw_in, "gather_rest_start")
    win_t = gw_in.reshape(in_cols, d)
    win_t = jnp.concatenate([win_t[:4 * dn], win_t[4 * dn + n_gate:], win_t[4 * dn:4 * dn + n_gate],
                             jnp.zeros((gb - n_gate, d), MXU_DTYPE)], axis=0)

    tm_nt = _pick(nt, 1280, 16)
    tn_in = _pick(in_pad, 1280)
    tk_d = _pick(d, 4096)
    h = _matmul(xm, win_t, form="nt", m=nt, n=in_pad, k=d, tm=tm_nt, tn=tn_in, tk=_pick(d, 2048), out_dtype=F32, name="in_proj",
                bias=jnp.zeros((1, in_pad), F32) + gs_token[0:1, 0:1])

    act = _conv7_fwd(h, w_conv, n_lat=n_lat, n_ctx=n_ctx, nh=nh)
    gates_hm = jnp.transpose(h[:, g0:g0 + n_gate].reshape(nt, 4, nh), (2, 0, 1))
    bc = lambda a: jnp.broadcast_to(a.reshape(nh, 1, 1), (nh, 8, LANES))
    dk = dict(n_lat=n_lat, n_ctx=n_ctx, nh=nh, hb=hb)
    o_f, st_f, mi_f = _delta_fwd(act, gates_hm, bc(a_log_f), bc(dt_bias_f), rev=False, **dk)
    o_b, st_b, mi_b = _delta_fwd(act, gates_hm, bc(a_log_b), bc(dt_bias_b), rev=True, **dk)

    h3 = h.reshape(nt // gw, gw, in_pad)
    ck = dict(gw=gw, col0=4 * dn, half=half)
    yh = _conf_conv_fwd(h3, w_dw, vertical=False, **ck).reshape(n_lat, half)
    yv = _conf_conv_fwd(h3, w_dw, vertical=True, **ck).reshape(n_lat, half)

    def mix_fn(o_heads, z_heads, ya, yb, dng, ba, bb, ga, gbb, la, lb):
        outs = []
        for o, z in zip(o_heads, z_heads):
            ms = jnp.mean(o * o, axis=-1, keepdims=True)
            outs.append(o * lax.rsqrt(ms + 1e-6) * dng * _silu(z))
        ya, yb = ya + ba, yb + bb
        mu = (jnp.sum(ya, axis=-1, keepdims=True) + jnp.sum(yb, axis=-1, keepdims=True)) / conf
        ya, yb = ya - mu, yb - mu
        var = (jnp.sum(ya * ya, axis=-1, keepdims=True) + jnp.sum(yb * yb, axis=-1, keepdims=True)) / conf
        rs = lax.rsqrt(var + LN_EPS)
        ca, cb = ya * rs * ga + la, yb * rs * gbb + lb
        return outs, _silu(ca), _silu(cb)

    def heads(t):
        return [t[:, i * HD:(i + 1) * HD] for i in range(nh)]

    def halves(p):
        return p[:, :half], p[:, half:]

    mix_params = [row(dn_norm_g), row(conf_dw_b), row(conf_ln_g), row(conf_ln_b)]

    def mix_args(rv, pv):
        (ba, bb), (ga, gbb), (la, lb) = halves(pv[1]), halves(pv[2]), halves(pv[3])
        return (heads(rv[0] + rv[1]), heads(rv[2]), rv[3], rv[4], pv[0], ba, bb, ga, gbb, la, lb)

    def stage_e(rv, pv):
        outs, ca, cb = mix_fn(*mix_args(rv, pv))
        return [jnp.concatenate(outs + [ca, cb], axis=-1)], []

    mix_rows = [(o_f, dn, 0, 0), (o_b, dn, 0, 0), (h, dn, 3, 0), (yh, half, 0, 0), (yv, half, 0, 0)]
    mix, = _rowwise(stage_e, mix_rows, mix_params, [(d, MXU_DTYPE)], [], n_rows=n_lat, tm=tmr, name="mixer_fwd")

    gw_out, gw_1, gw_2 = _gather_forward(_gather_wait(gs_send, gs_recv, gs_bufs, [mix], "gather_rest_wait"), "gather_rest_forward")
    wout_full = gw_out.reshape(d, d)
    w2_full = gw_2.reshape(ff, d)
    w1_sh = gw_1

    tm_l = _pick(n_lat, 1024, 16)
    tn_d = _pick(d, 1024)
    y1 = _matmul(mix, wout_full, form="nn", m=n_lat, n=d, k=d, tm=tm_l, tn=tn_d, tk=tk_d, out_dtype=F32, name="out_proj")

    def res1(x0, y1v, ga, l1g, l1b, shm, scm):
        x1 = _ln(ALPHA * x0 + ga * y1v, l1g, l1b)
        return x1, x1 * (1.0 + scm) + shm

    f_params = ln_in + [g_a, row(ln1_g), row(ln1_b), sh_m, sc_m]

    def stage_f(rv, pv):
        x1, u = res1(_ln(rv[0], pv[0], pv[1]), rv[1], *pv[2:])
        return [x1, u], []

    x1, u = _rowwise(stage_f, [(x2, d, 0, 0), (y1, d, 0, 0)], f_params, [(d, F32), (d, MXU_DTYPE)], [], n_rows=n_lat, tm=tmr, name="res1_fwd")

    tn_f = _pick(fl, 1024)
    relu_h = _matmul(u, w1_sh, form="nn", m=n_lat, n=ff, k=d, tm=tm_l, tn=tn_f, tk=tk_d, out_dtype=MXU_DTYPE, name="mlp1",
                     bias=row(b_mlp1), epi=lambda r: jnp.maximum(r, 0.0), b_split=fl)
    sq = lambda t: (t.astype(F32) * t.astype(F32)).astype(MXU_DTYPE)
    tk_f = _pick(ff, 4096)
    y2 = _matmul(relu_h, w2_full, form="nn", m=n_lat, n=d, k=ff, tm=tm_l, tn=tn_d, tk=tk_f, out_dtype=F32, name="mlp2",
                 bias=row(b_mlp2), a_fn=sq)

    def loss_fn(x1v, y2v, gm, l2g, l2b, tgt):
        x2v = _ln(ALPHA * x1v + gm * y2v, l2g, l2b)
        return (0.5 / d) * jnp.sum(jnp.square(x2v - tgt))

    def stage_g(rv, pv):
        loss, grads = jax.value_and_grad(loss_fn, argnums=(0, 1, 2, 3, 4))(rv[0], rv[1], pv[0], pv[1], pv[2], rv[2])
        dx1, dy2r, dgm, dl2g, dl2b = grads
        dy2 = dy2r
        return [dx1, dy2], [jnp.reshape(loss, (1, 1)), dgm, dl2g, dl2b, jnp.sum(dy2, axis=0, keepdims=True)]

    dx1p, dy2, loss_acc, d_g_m, d_ln2_g, d_ln2_b, d_b_mlp2 = _rowwise(
        stage_g, [(x1, d, 0, 0), (y2, d, 0, 0), (tgt2, d, 0, 0)], [g_m, row(ln2_g), row(ln2_b)],
        [(d, F32), (d, MXU_DTYPE)], [LANES, d, d, d, d], n_rows=n_lat, tm=tmr, name="loss_res2_bwd")

    dhid, d_b_mlp1 = _matmul(dy2, w2_full, form="nt", m=n_lat, n=ff, k=d, tm=tm_l, tn=tn_f, tk=tk_d, out_dtype=MXU_DTYPE, name="mlp2_bwd_x",
                             epi=lambda r, rh: r * (2.0 * rh.astype(F32)), epi_in=relu_h, colsum=True)
    tk_l = _pick(n_lat, 4096, 16)
    d_w2 = _matmul(relu_h, dy2, form="tn", m=ff, n=d, k=n_lat, tm=tn_f, tn=tn_d, tk=tk_l, out_dtype=F32, name="mlp2_bwd_w", a_fn=sq)
    du = _matmul(dhid, w1_sh, form="nt", m=n_lat, n=d, k=ff, tm=tm_l, tn=tn_d, tk=_pick(fl, 4096), out_dtype=F32, name="mlp1_bwd_x", b_split=fl)
    d_w1 = _matmul(u, dhid, form="tn", m=d, n=ff, k=n_lat, tm=tn_d, tn=tn_f, tk=tk_l, out_dtype=F32, name="mlp1_bwd_w", o_split=fl)

    as_idx = lambda v: jnp.reshape(v, (1,)).astype(jnp.int32)
    slot_idx = [chip_idx, as_idx(2 * (1 - mx) + my), as_idx(2 * mx + (1 - my)), as_idx(2 * (1 - mx) + (1 - my)), c_idx]
    mp_send, mp_recv, mlp_grads, mlp_half, mp_token = _pair_send_start([d_w1, d_w2.reshape(4, fl, d)], loss_acc, "mlp_grads_pair_start")

    def stage_h(rv, pv):
        x0 = _ln(rv[0], pv[0], pv[1])
        _, vjp = jax.vjp(res1, x0, rv[1], *pv[2:])
        dx0, dy1r, dga, dl1g, dl1b, dshm, dscm = vjp((rv[3], rv[2]))
        return [dx0, dy1r], [dga, dl1g, dl1b, dshm, dscm]

    h_params = ln_in + [g_a + mp_token[0:1, 0:1]] + f_params[3:]
    dx0p, dy1, d_g_a, d_ln1_g, d_ln1_b, d_sh_m, d_sc_m = _rowwise(
        stage_h, [(x2, d, 0, 0), (y1, d, 0, 0), (du, d, 0, 0), (dx1p, d, 0, 0)], h_params,
        [(d, F32), (d, MXU_DTYPE)], [d, d, d, d, d], n_rows=n_lat, tm=tmr, name="res1_bwd")

    dmix = _matmul(dy1, wout_full, form="nt", m=n_lat, n=d, k=d, tm=tm_l, tn=tn_d, tk=tk_d, out_dtype=F32, name="out_proj_bwd_x")
    d_wout = _matmul(mix, dy1, form="tn", m=d, n=d, k=n_lat, tm=tn_d, tn=tn_d, tk=tk_l, out_dtype=F32, name="out_proj_bwd_w")

    def stage_i(rv, pv):
        args = mix_args(rv, pv)
        _, vjp = jax.vjp(mix_fn, *args)
        dm = rv[5]
        d_outs = [dm[:, i * HD:(i + 1) * HD] for i in range(nh)]
        dca, dcb = dm[:, dn:dn + half], dm[:, dn + half:]
        do_h, dz_h, dya, dyb, ddng, dba, dbb, dga, dgbb, dla, dlb = vjp((d_outs, dca, dcb))
        cat = lambda *p: jnp.concatenate(p, axis=-1)
        return ([cat(*do_h), cat(*dz_h), dya, dyb], [ddng, cat(dba, dbb), cat(dga, dgbb), cat(dla, dlb)])

    do, dz, dyh, dyv, d_dn_norm_g, d_conf_dw_b, d_conf_ln_g, d_conf_ln_b = _rowwise(
        stage_i, mix_rows + [(dmix, d, 0, 0)], mix_params, [(dn, F32), (dn, MXU_DTYPE), (half, F32), (half, F32)],
        [HD, conf, conf, conf], n_rows=n_lat, tm=tmr, name="mixer_bwd")

    mlp_grads, mlp_half = _pair_send_wait(mp_send, mp_recv, mlp_grads, mlp_half, do, "mlp_grads_pair_wait")
    mlp_sums = [_add_my_half(g, r, c_idx, f"mlp_grads_pair_add_{i}") for i, (g, r) in enumerate(zip(mlp_grads, mlp_half))]
    ms_send, ms_recv, mlp_sums, mlp_lands, ms_token = _scatter_start(mlp_sums, loss_acc, "mlp_grads_scatter_start")
    tie = ms_token[0:1, 0:1].reshape(1, 1, 1)

    dval_h, dgate_h, d_wdw_h = _conf_conv_bwd(h3, w_dw, dyh.reshape(gw, gw, half), vertical=False, **ck)
    dval_v, dgate_v, d_wdw_v = _conf_conv_bwd(h3, w_dw, dyv.reshape(gw, gw, half), vertical=True, **ck)
    dact_f, dgt_f, d_alog_f, d_dt_f = _delta_bwd(act, gates_hm, bc(a_log_f) + tie, bc(dt_bias_f), st_f, mi_f, do, rev=False, **dk)
    dact_b, dgt_b, d_alog_b, d_dt_b = _delta_bwd(act, gates_hm, bc(a_log_b) + tie, bc(dt_bias_b), st_b, mi_b, do, rev=True, **dk)
    dh_qkv, d_wconv = _conv7_bwd(h, w_conv, dact_f, dact_b, n_lat=n_lat, n_ctx=n_ctx, nh=nh)

    dgates = jnp.stack([dgt_f[..., 0], dgt_f[..., 1], dgt_b[..., 0], dgt_b[..., 1]], axis=0)
    dgates = jnp.transpose(dgates, (2, 0, 1)).reshape(nt, n_gate)
    dgates = jnp.pad(dgates, ((0, 0), (0, gb - n_gate))).astype(MXU_DTYPE)
    zrows = lambda t: jnp.pad(t, ((0, n_ctx), (0, 0)))
    dh = jnp.concatenate([dh_qkv, zrows(dz), zrows(dval_h.reshape(n_lat, half)), zrows(dval_v.reshape(n_lat, half)),
                          zrows(dgate_h.reshape(n_lat, half)), zrows(dgate_v.reshape(n_lat, half)), dgates], axis=1)

    tk_in = _pick(in_pad, 2560)
    tk_nt = _pick(nt, 2560, 16)
    d_win_t = _matmul(dh, xm, form="tn", m=in_pad, n=d, k=nt, tm=tn_in, tn=tn_d, tk=tk_nt, out_dtype=F32, name="in_proj_bwd_w")
    d_win_t = jnp.concatenate([d_win_t[:4 * dn], d_win_t[g0:g0 + n_gate], d_win_t[4 * dn:g0]], axis=0)

    proj_grads = [d_win_t.reshape(4, r_in, d), d_wout.reshape(4, d // 4, d)]
    pp_send, pp_recv, proj_grads, proj_half, pp_token = _pair_send_start(proj_grads, loss_acc, "proj_grads_pair_start")
    dxm = _matmul(dh, win_t, form="nn", m=nt, n=d, k=in_pad, tm=tm_nt, tn=tn_d, tk=tk_in, out_dtype=F32, name="in_proj_bwd_x",
                  bias=jnp.zeros((1, d), F32) + pp_token[0:1, 0:1])
    proj_grads, proj_half = _pair_send_wait(pp_send, pp_recv, proj_grads, proj_half, dxm, "proj_grads_pair_wait")
    proj_sums = [_add_my_half(g, r, c_idx, f"proj_grads_pair_add_{i}") for i, (g, r) in enumerate(zip(proj_grads, proj_half))]
    ps_send, ps_recv, proj_sums, proj_lands, ps_token = _scatter_start(proj_sums, loss_acc, "proj_grads_scatter_start")
    mlp_sums, mlp_lands = _scatter_wait(ms_send, ms_recv, mlp_sums, mlp_lands, [ps_token], "mlp_grads_scatter_wait")
    mlp_mine = [_sum_slots(own, rcv, slot_idx, f"mlp_grads_chip_sum_{i}") for i, (own, rcv) in enumerate(zip(mlp_sums, mlp_lands))]
    g_w_mlp1, g_w_mlp2 = _pair_join_halves(mlp_mine, "mlp_grads_pair_join")
    mlp_adam = {"w_mlp1": _adamw(w_mlp1[0], g_w_mlp1, m_w_mlp1[0], v_w_mlp1[0], "adamw_w_mlp1"),
                "w_mlp2": _adamw(w_mlp2[0], g_w_mlp2, m_w_mlp2[0], v_w_mlp2[0], "adamw_w_mlp2")}
    mlp_done = [mlp_adam["w_mlp1"][2], mlp_adam["w_mlp2"][2]]
    sc_a_tied = sc_a + ps_token[0:1, 0:1]

    def mod_in(xr, lg, lb, sh, sc):
        x0 = _ln(xr, lg, lb)
        return x0, x0 * (1.0 + sc) + sh

    def stage_j(rv, pv):
        _, vjp = jax.vjp(mod_in, rv[0], *pv)
        dx0 = rv[2] if len(rv) > 2 else jnp.zeros_like(rv[1])
        dxr, dlg, dlb, dsh, dsc = vjp((dx0, rv[1]))
        return [dxr], [dlg, dlb, dsh, dsc]

    grad_x, dlg_l, dlb_l, d_sh_a, d_sc_a = _rowwise(
        stage_j, [(x2, d, 0, 0), (dxm, d, 0, 0), (dx0p, d, 0, 0)], ln_in + [sh_a, sc_a_tied], [(d, F32)], [d, d, d, d],
        n_rows=n_lat, tm=tmr, name="ln_mod_bwd_lat", after=mlp_done)
    _, dlg_c, dlb_c, d_csh_a, d_csc_a = _rowwise(
        stage_j, [(ctx2, d, 0, 0), (dxm, d, 0, n_lat // tmc)], ln_in + [csh_a, csc_a], [(d, F32)], [d, d, d, d],
        n_rows=n_ctx, tm=tmc, name="ln_mod_bwd_ctx", after=mlp_done)

    zd = jnp.zeros((1, d), F32)
    dmod_rows = jnp.concatenate([jnp.concatenate([d_sh_a, d_sc_a, d_g_a, d_sh_m, d_sc_m, d_g_m], axis=1),
                                 jnp.concatenate([d_csh_a, d_csc_a, zd, zd, zd, zd], axis=1), jnp.zeros((6, 6 * d), F32)], axis=0)
    d16, g_b_mod = _dmod_prep(_bcast8(dmod_rows, "bcast_dmod"), "dmod_prep")
    d16_loc = lax.dynamic_slice(d16, (0, chip * ml), (16, ml))
    g_w_mod = _matmul(s16, d16_loc, form="tn", m=d, n=ml, k=16, tm=_pick(d, 512), tn=_pick(ml, 1024), tk=16, out_dtype=F32,
                      name="mod_bwd_w", a_fn=_silu, kind="hi")
    dsilu = _matmul(d16_loc, w_mod[0], form="nt", m=16, n=d, k=ml, tm=16, tn=_pick(d, 1024), tk=_pick(ml, 1024), out_dtype=F32,
                    name="mod_bwd_c", a_fn=to_mxu, b_fn=to_mxu)
    d_cctx_part = dsilu[8:9, :] * (1 - mc).astype(F32)

    head_sum = lambda t: t[:, 0, 0]
    small = [d_cctx_part, dlg_l + dlg_c, dlb_l + dlb_c, d_wconv, head_sum(d_alog_f), head_sum(d_dt_f), head_sum(d_alog_b),
             head_sum(d_dt_b), d_dn_norm_g, jnp.concatenate([d_wdw_h, d_wdw_v], axis=1), d_conf_dw_b, d_conf_ln_g, d_conf_ln_b,
             d_ln1_g, d_ln1_b, d_b_mlp1, d_b_mlp2, d_ln2_g, d_ln2_b]
    small_shapes = [(d,), (d,), (d,), (SHORT_CONV, 3 * dn), (nh,), (nh,), (nh,), (nh,), (HD,), (CONF_K, conf), (conf,), (conf,), (conf,),
                    (d,), (d,), (ff,), (d,), (d,), (d,)]
    ssum = _sum8(_bcast8(_pack(small), "bcast_small_grads"), "sum_small_grads")
    (t_cctx, g_ln_in_g, g_ln_in_b, g_wconv_full, g_a_log_f, g_dt_bias_f, g_a_log_b, g_dt_bias_b, g_dn_norm_g, g_wdw_full, g_conf_dw_b,
     g_conf_ln_g, g_conf_ln_b, g_ln1_g, g_ln1_b, g_b_mlp1, g_b_mlp2, g_ln2_g, g_ln2_b) = _unpack(ssum, small_shapes)
    sg = jax.nn.sigmoid(c_ctx)
    g_c_ctx = t_cctx * (sg * (1.0 + c_ctx * (1.0 - sg)))
    g_w_qkv_conv = lax.dynamic_slice(g_wconv_full, (0, chip * w_qkv_conv.shape[2]), w_qkv_conv.shape[1:])
    g_conf_dw_w = lax.dynamic_slice(g_wdw_full, (0, chip * conf_dw_w.shape[2]), conf_dw_w.shape[1:])

    grads = dict(c_ctx=g_c_ctx, ln_in_g=g_ln_in_g, ln_in_b=g_ln_in_b, w_mod=g_w_mod[None], b_mod=g_b_mod, w_in=None,
                 w_qkv_conv=g_w_qkv_conv[None], a_log_f=g_a_log_f[None], dt_bias_f=g_dt_bias_f[None], a_log_b=g_a_log_b[None],
                 dt_bias_b=g_dt_bias_b[None], dn_norm_g=g_dn_norm_g[None], conf_dw_w=g_conf_dw_w[None], conf_dw_b=g_conf_dw_b[None],
                 conf_ln_g=g_conf_ln_g[None], conf_ln_b=g_conf_ln_b[None], w_out=None, ln1_g=g_ln1_g[None], ln1_b=g_ln1_b[None],
                 w_mlp1=g_w_mlp1[None], b_mlp1=g_b_mlp1[None], w_mlp2=g_w_mlp2[None], b_mlp2=g_b_mlp2[None], ln2_g=g_ln2_g[None],
                 ln2_b=g_ln2_b[None])
    weights = dict(c_ctx=c_ctx, ln_in_g=ln_in_g, ln_in_b=ln_in_b, w_mod=w_mod, b_mod=b_mod, w_in=w_in, w_qkv_conv=w_qkv_conv,
                   a_log_f=a_log_f, dt_bias_f=dt_bias_f, a_log_b=a_log_b, dt_bias_b=dt_bias_b, dn_norm_g=dn_norm_g, conf_dw_w=conf_dw_w,
                   conf_dw_b=conf_dw_b, conf_ln_g=conf_ln_g, conf_ln_b=conf_ln_b, w_out=w_out, ln1_g=ln1_g, ln1_b=ln1_b, w_mlp1=w_mlp1,
                   b_mlp1=b_mlp1, w_mlp2=w_mlp2, b_mlp2=b_mlp2, ln2_g=ln2_g, ln2_b=ln2_b)
    m_in = dict(c_ctx=m_c_ctx, ln_in_g=m_ln_in_g, ln_in_b=m_ln_in_b, w_mod=m_w_mod, b_mod=m_b_mod, w_in=m_w_in, w_qkv_conv=m_w_qkv_conv,
                a_log_f=m_a_log_f, dt_bias_f=m_dt_bias_f, a_log_b=m_a_log_b, dt_bias_b=m_dt_bias_b, dn_norm_g=m_dn_norm_g,
                conf_dw_w=m_conf_dw_w, conf_dw_b=m_conf_dw_b, conf_ln_g=m_conf_ln_g, conf_ln_b=m_conf_ln_b, w_out=m_w_out, ln1_g=m_ln1_g,
                ln1_b=m_ln1_b, w_mlp1=m_w_mlp1, b_mlp1=m_b_mlp1, w_mlp2=m_w_mlp2, b_mlp2=m_b_mlp2, ln2_g=m_ln2_g, ln2_b=m_ln2_b)
    v_in = dict(c_ctx=v_c_ctx, ln_in_g=v_ln_in_g, ln_in_b=v_ln_in_b, w_mod=v_w_mod, b_mod=v_b_mod, w_in=v_w_in, w_qkv_conv=v_w_qkv_conv,
                a_log_f=v_a_log_f, dt_bias_f=v_dt_bias_f, a_log_b=v_a_log_b, dt_bias_b=v_dt_bias_b, dn_norm_g=v_dn_norm_g,
                conf_dw_w=v_conf_dw_w, conf_dw_b=v_conf_dw_b, conf_ln_g=v_conf_ln_g, conf_ln_b=v_conf_ln_b, w_out=v_w_out, ln1_g=v_ln1_g,
                ln1_b=v_ln1_b, w_mlp1=v_w_mlp1, b_mlp1=v_b_mlp1, w_mlp2=v_w_mlp2, b_mlp2=v_b_mlp2, ln2_g=v_ln2_g, ln2_b=v_ln2_b)
    names = list(weights)
    big_names = ("w_mod", "w_mlp1", "w_mlp2", "w_in", "w_out")
    delta, new_m, new_v = {}, {}, {}

    def big_adamw(nm):
        dl, mm, vv = mlp_adam[nm] if nm in mlp_adam else _adamw(weights[nm][0], grads[nm][0], m_in[nm][0], v_in[nm][0], f"adamw_{nm}")
        delta[nm], new_m[nm], new_v[nm] = dl[None], mm[None], vv[None]

    for nm in big_names[:3]:
        big_adamw(nm)
    small_names = [nm for nm in names if nm not in big_names]
    shapes = [weights[nm].shape for nm in small_names]
    for nm in small_names:
        grads[nm] = grads[nm].reshape(weights[nm].shape)
    packed = [_pack([src[nm] for nm in small_names]) for src in (weights, grads, m_in, v_in)]
    dl, mm, vv = _adamw(*packed, "adamw_small")
    for nm, a, b_, c_ in zip(small_names, _unpack(dl, shapes), _unpack(mm, shapes), _unpack(vv, shapes)):
        delta[nm], new_m[nm], new_v[nm] = a, b_, c_

    done = [new_v[nm] for nm in big_names[:3]] + [vv]
    proj_sums, proj_lands = _scatter_wait(ps_send, ps_recv, proj_sums, proj_lands, done, "proj_grads_scatter_wait")
    proj_mine = [_sum_slots(own, rcv, slot_idx, f"proj_grads_chip_sum_{i}") for i, (own, rcv) in enumerate(zip(proj_sums, proj_lands))]
    g_win_t, g_w_out = _pair_join_halves(proj_mine, "proj_grads_pair_join")
    grads["w_in"], grads["w_out"] = jnp.transpose(g_win_t)[None], g_w_out[None]
    for nm in big_names[3:]:
        big_adamw(nm)

    loss = lax.psum(loss_acc[0, 0], MESH_AXES)
    return (loss, grad_x[None], *[grads[nm] for nm in names], *[delta[nm] for nm in names],
            *[new_m[nm] for nm in names], *[new_v[nm] for nm in names])
```

```python
import functools
import math

import jax
import jax.numpy as jnp
from jax import lax
from jax.experimental import pallas as pl
from jax.experimental.pallas import tpu as pltpu

F32 = jnp.float32
BF16 = jnp.bfloat16
MXU_DTYPE = BF16
WIRE_DTYPE = BF16
HIGHEST = lax.Precision.HIGHEST

HD = 128
CHUNK = 128
SHORT_CONV = 7
CONF_K = 31
ALPHA = 2.0 ** 0.25
LN_EPS = 1e-5
V7X_VMEM_BYTES = 64 * 1024 * 1024
LANES = 128

ADAM_LR = 0.001
ADAM_B1 = 0.9
ADAM_B2 = 0.999
ADAM_EPS = 1e-08
ADAM_WD = 0.01
ADAM_STEP = 10

MESH_AXES = ("x", "y", "c")
ANY = pl.BlockSpec(memory_space=pl.ANY)


def _pick(dim, pref, mult=LANES):
    best = None
    t = mult
    while t <= min(dim, pref):
        if dim % t == 0:
            best = t
        t += mult
    return best if best is not None else dim


def _cparams(sem, vmem_est):
    limit = int(min(V7X_VMEM_BYTES - 6 * 1024 * 1024, max(32 * 1024 * 1024, vmem_est + 8 * 1024 * 1024)))
    return pltpu.CompilerParams(dimension_semantics=sem, vmem_limit_bytes=limit)


def _nbytes(shape, dtype):
    return math.prod(shape) * jnp.dtype(dtype).itemsize


def _ln(x, g, b):
    mu = jnp.mean(x, axis=-1, keepdims=True)
    xc = x - mu
    var = jnp.mean(xc * xc, axis=-1, keepdims=True)
    return xc * lax.rsqrt(var + LN_EPS) * g + b


def _silu(x):
    return x * jax.nn.sigmoid(x)


def _softplus(x):
    return jnp.maximum(x, 0.0) + jnp.log1p(jnp.exp(-jnp.abs(x)))


_DIMS = {"nn": (((1,), (0,)), ((), ())), "nt": (((1,), (1,)), ((), ())), "tn": (((0,), (0,)), ((), ()))}


def _split_bf16(x, parts):
    out, rest = [], x.astype(F32)
    for _ in range(parts):
        bits = lax.bitcast_convert_type(rest, jnp.uint32) & jnp.uint32(0xFFFF0000)
        p = lax.bitcast_convert_type(bits, F32)
        out.append(p.astype(BF16))
        rest = rest - p
    return out


def _raw_dot(a, b, form, kind):
    if MXU_DTYPE != F32:
        if kind == "mxu":
            return lax.dot_general(a.astype(MXU_DTYPE), b.astype(MXU_DTYPE), _DIMS[form], preferred_element_type=F32)
        if kind == "hi3":
            (a0, a1), (b0, b1) = _split_bf16(a, 2), _split_bf16(b, 2)
            d = lambda p, q: lax.dot_general(p, q, _DIMS[form], preferred_element_type=F32)
            return d(a0, b0) + (d(a1, b0) + d(a0, b1))
    return lax.dot_general(a.astype(F32), b.astype(F32), _DIMS[form], precision=HIGHEST, preferred_element_type=F32)


@functools.partial(jax.custom_vjp, nondiff_argnums=(2, 3))
def _dot(a, b, form, kind):
    return _raw_dot(a, b, form, kind)


def _dot_fwd(a, b, form, kind):
    return _raw_dot(a, b, form, kind), (a, b)


def _dot_bwd(form, kind, res, dc):
    a, b = res
    if form == "nn":
        return _dot(dc, b, "nt", kind), _dot(a, dc, "tn", kind)
    if form == "nt":
        return _dot(dc, b, "nn", kind), _dot(dc, a, "tn", kind)
    return _dot(b, dc, "nt", kind), _dot(a, dc, "nn", kind)


_dot.defvjp(_dot_fwd, _dot_bwd)


def _raw_xdot(a, b, form, exact):
    if MXU_DTYPE == F32:
        return _raw_dot(a, b, form, "hi")
    d = lambda p, q: lax.dot_general(p, q, _DIMS[form], preferred_element_type=F32)
    if exact == "a":
        a16, (b0, b1, b2) = a.astype(BF16), _split_bf16(b, 3)
        return d(a16, b0) + (d(a16, b1) + d(a16, b2))
    b16, (a0, a1, a2) = b.astype(BF16), _split_bf16(a, 3)
    return d(a0, b16) + (d(a1, b16) + d(a2, b16))


@functools.partial(jax.custom_vjp, nondiff_argnums=(2, 3))
def _xdot(a, b, form, exact):
    return _raw_xdot(a, b, form, exact)


def _xdot_fwd(a, b, form, exact):
    return _raw_xdot(a, b, form, exact), (a, b)


def _xdot_bwd(form, exact, res, dc):
    a, b = res
    if form == "nn" and exact == "a":
        return jnp.zeros_like(a), _xdot(a, dc, "tn", "a")
    assert form == "tn" and exact == "b"
    return _xdot(b, dc, "nt", "a"), jnp.zeros_like(b)


_xdot.defvjp(_xdot_fwd, _xdot_bwd)


def _spec2(block, idx, split=None):
    if split is None:
        return pl.BlockSpec(tuple(block), idx)
    per = split // block[1]

    def idx3(*g):
        r, cblk = idx(*g)
        return (cblk // per, r, cblk % per)

    return pl.BlockSpec((None,) + tuple(block), idx3)


def _matmul(a, b, *, form, m, n, k, tm, tn, tk, out_dtype, name, bias=None, a_fn=None, b_fn=None,
            epi=None, epi_in=None, colsum=False, kind="mxu", b_split=None, o_split=None):
    assert m % tm == 0 and n % tn == 0 and k % tk == 0, (name, m, n, k, tm, tn, tk)
    nk = k // tk
    grid = (n // tn, m // tm, nk)
    if form == "tn":
        a_spec = pl.BlockSpec((tk, tm), lambda j, i, kk: (kk, i))
    else:
        a_spec = pl.BlockSpec((tm, tk), lambda j, i, kk: (i, kk))
    if form == "nt":
        b_spec = _spec2((tn, tk), lambda j, i, kk: (j, kk), b_split)
    else:
        b_spec = _spec2((tk, tn), lambda j, i, kk: (kk, j), b_split)
    in_specs = [a_spec, b_spec]
    operands = [a, b]
    if bias is not None:
        in_specs.append(pl.BlockSpec((1, tn), lambda j, i, kk: (0, j)))
        operands.append(bias)
    if epi_in is not None:
        in_specs.append(pl.BlockSpec((tm, tn), lambda j, i, kk: (i, j)))
        operands.append(epi_in)
    out_specs = [_spec2((tm, tn), lambda j, i, kk: (i, j), o_split)]
    if o_split is None:
        out_shape = [jax.ShapeDtypeStruct((m, n), out_dtype)]
    else:
        out_shape = [jax.ShapeDtypeStruct((n // o_split, m, o_split), out_dtype)]
    if colsum:
        out_specs.append(pl.BlockSpec((1, tn), lambda j, i, kk: (0, j)))
        out_shape.append(jax.ShapeDtypeStruct((1, n), F32))
    has_bias, has_epi_in = bias is not None, epi_in is not None

    def body(*refs):
        refs = list(refs)
        a_ref, b_ref = refs[0], refs[1]
        pos = 2
        bias_ref = epi_ref = cs_ref = None
        if has_bias:
            bias_ref = refs[pos]
            pos += 1
        if has_epi_in:
            epi_ref = refs[pos]
            pos += 1
        o_ref = refs[pos]
        pos += 1
        if colsum:
            cs_ref = refs[pos]
            pos += 1
        acc_ref = refs[pos] if nk > 1 else None
        i, kk = pl.program_id(1), pl.program_id(2)

        av = a_ref[...]
        if a_fn is not None:
            av = a_fn(av)
        bv = b_ref[...]
        if b_fn is not None:
            bv = b_fn(bv)
        prod = _raw_dot(av, bv, form, kind)

        if nk > 1:
            @pl.when(kk == 0)
            def _():
                acc_ref[...] = prod

            @pl.when(jnp.logical_and(kk > 0, kk < nk - 1))
            def _():
                acc_ref[...] += prod

        @pl.when(kk == nk - 1)
        def _():
            r = acc_ref[...] + prod if nk > 1 else prod
            if has_bias:
                r = r + bias_ref[...]
            if epi is not None:
                r = epi(r, epi_ref[...]) if has_epi_in else epi(r)
            o_ref[...] = r.astype(out_dtype)
            if colsum:
                s = jnp.sum(r, axis=0, keepdims=True)

                @pl.when(i == 0)
                def _():
                    cs_ref[...] = s

                @pl.when(i > 0)
                def _():
                    cs_ref[...] += s

    est = 2 * (_nbytes((tm, tk), a.dtype) + _nbytes((tk, tn), b.dtype) + _nbytes((tm, tn), out_dtype))
    est += _nbytes((tm, tn), F32) * 2 + (2 * _nbytes((tm, tn), epi_in.dtype) if has_epi_in else 0)
    res = pl.pallas_call(
        body, grid=grid, in_specs=in_specs, out_specs=out_specs, out_shape=out_shape,
        scratch_shapes=[pltpu.VMEM((tm, tn), F32)] if nk > 1 else [], name=name,
        compiler_params=_cparams(("arbitrary", "arbitrary", "arbitrary"), est),
    )(*operands)
    return res if colsum else res[0]


def _rowwise(fn, rows, params, row_outs, acc_outs, *, n_rows, tm, name, after=()):
    assert n_rows % tm == 0, (name, n_rows, tm)
    nr, npar, nro, nac, naf = len(rows), len(params), len(row_outs), len(acc_outs), len(after)
    in_specs = [pl.BlockSpec((tm, w), functools.partial(lambda i, cb, ro: (i + ro, cb), cb=cb, ro=ro))
                for (_, w, cb, ro) in rows]
    in_specs += [pl.BlockSpec((1, p.shape[1]), lambda i: (0, 0)) for p in params] + [ANY] * naf
    out_specs = [pl.BlockSpec((tm, w), lambda i: (i, 0)) for (w, _) in row_outs]
    out_specs += [pl.BlockSpec((1, w), lambda i: (0, 0)) for w in acc_outs]
    out_shape = [jax.ShapeDtypeStruct((n_rows, w), dt) for (w, dt) in row_outs]
    out_shape += [jax.ShapeDtypeStruct((1, w), F32) for w in acc_outs]

    def body(*refs):
        rv = [r[...] for r in refs[:nr]]
        pv = [r[...] for r in refs[nr:nr + npar]]
        ro_refs = refs[nr + npar + naf:nr + npar + naf + nro]
        ac_refs = refs[nr + npar + naf + nro:]
        ro, ac = fn(rv, pv)
        for ref, val in zip(ro_refs, ro, strict=True):
            ref[...] = val.astype(ref.dtype)
        first = pl.program_id(0) == 0
        for ref, val in zip(ac_refs, ac, strict=True):
            val = jnp.broadcast_to(val.astype(F32), ref.shape)

            @pl.when(first)
            def _(ref=ref, val=val):
                ref[...] = val

            @pl.when(jnp.logical_not(first))
            def _(ref=ref, val=val):
                ref[...] += val

    io = sum(_nbytes((tm, w), a.dtype) for (a, w, _, _) in rows) + sum(_nbytes((tm, w), dt) for (w, dt) in row_outs)
    widest = max([w for (_, w, _, _) in rows] + [w for (w, _) in row_outs])
    est = 2 * io + 12 * _nbytes((tm, widest), F32)
    return pl.pallas_call(
        body, grid=(n_rows // tm,), in_specs=in_specs, out_specs=out_specs, out_shape=out_shape, name=name,
        compiler_params=_cparams(("arbitrary",), est),
    )(*[a for (a, _, _, _) in rows], *params, *after)


def _conv7_tiles(n_lat, n_ctx):
    tt = min(256, n_ctx)
    assert n_lat % tt == 0 and n_ctx % tt == 0
    return tt, [(t0, 8 + t0) for t0 in range(0, n_lat, tt)] + [(n_lat + t0, 16 + n_lat + t0) for t0 in range(0, n_ctx, tt)]


def _conv7_fill(p_ref, x_ref, n_lat, n_ctx):
    z8 = jnp.zeros((8, LANES), F32)
    p_ref[0:8, :] = z8
    p_ref[8:8 + n_lat, :] = x_ref[0:n_lat, :]
    p_ref[8 + n_lat:16 + n_lat, :] = z8
    p_ref[16 + n_lat:16 + n_lat + n_ctx, :] = x_ref[n_lat:n_lat + n_ctx, :]
    p_ref[16 + n_lat + n_ctx:24 + n_lat + n_ctx, :] = z8


def _conv7_fwd(h, w_conv, *, n_lat, n_ctx, nh):
    nt = n_lat + n_ctx
    tt, tiles = _conv7_tiles(n_lat, n_ctx)
    half = SHORT_CONV // 2

    def body(x_ref, w_ref, o_ref, p_ref):
        _conv7_fill(p_ref, x_ref, n_lat, n_ctx)
        for (row, prow) in tiles:
            acc = jnp.zeros((tt, LANES), F32)
            for kk in range(SHORT_CONV):
                acc = acc + w_ref[kk:kk + 1, :] * p_ref[pl.ds(prow + kk - half, tt), :]
            o_ref[pl.ds(row, tt), :] = _silu(acc)

    return pl.pallas_call(
        body, grid=(3 * nh,),
        in_specs=[pl.BlockSpec((nt, LANES), lambda j: (0, j)), pl.BlockSpec((SHORT_CONV, LANES), lambda j: (0, j))],
        out_specs=pl.BlockSpec((None, None, nt, LANES), lambda j: (j // nh, j % nh, 0, 0)),
        out_shape=jax.ShapeDtypeStruct((3, nh, nt, LANES), F32),
        scratch_shapes=[pltpu.VMEM((nt + 24, LANES), F32)], name="conv7_fwd",
        compiler_params=_cparams(("arbitrary",), 5 * _nbytes((nt + 24, LANES), F32)),
    )(h, w_conv)


def _conv7_bwd(h, w_conv, dact_f, dact_b, *, n_lat, n_ctx, nh):
    nt = n_lat + n_ctx
    tt, tiles = _conv7_tiles(n_lat, n_ctx)
    half = SHORT_CONV // 2

    def body(x_ref, w_ref, df_ref, db_ref, dx_ref, dw_ref, p_ref, q_ref):
        _conv7_fill(p_ref, x_ref, n_lat, n_ctx)
        z8 = jnp.zeros((8, LANES), F32)
        q_ref[0:8, :] = z8
        q_ref[8 + n_lat:16 + n_lat, :] = z8
        q_ref[16 + n_lat + n_ctx:24 + n_lat + n_ctx, :] = z8
        dw = [jnp.zeros((1, LANES), F32) for _ in range(SHORT_CONV)]
        for (row, prow) in tiles:
            pre = jnp.zeros((tt, LANES), F32)
            for kk in range(SHORT_CONV):
                pre = pre + w_ref[kk:kk + 1, :] * p_ref[pl.ds(prow + kk - half, tt), :]
            s = jax.nn.sigmoid(pre)
            dpre = (df_ref[pl.ds(row, tt), :] + db_ref[pl.ds(row, tt), :]) * (s * (1.0 + pre * (1.0 - s)))
            q_ref[pl.ds(prow, tt), :] = dpre
            for kk in range(SHORT_CONV):
                dw[kk] = dw[kk] + jnp.sum(dpre * p_ref[pl.ds(prow + kk - half, tt), :], axis=0, keepdims=True)
        for (row, prow) in tiles:
            acc = jnp.zeros((tt, LANES), F32)
            for kk in range(SHORT_CONV):
                acc = acc + w_ref[kk:kk + 1, :] * q_ref[pl.ds(prow + half - kk, tt), :]
            dx_ref[pl.ds(row, tt), :] = acc.astype(dx_ref.dtype)
        for kk in range(SHORT_CONV):
            dw_ref[kk:kk + 1, :] = dw[kk]

    dspec = pl.BlockSpec((None, None, nt, LANES), lambda j: (j // nh, j % nh, 0, 0))
    return pl.pallas_call(
        body, grid=(3 * nh,),
        in_specs=[pl.BlockSpec((nt, LANES), lambda j: (0, j)), pl.BlockSpec((SHORT_CONV, LANES), lambda j: (0, j)), dspec, dspec],
        out_specs=[pl.BlockSpec((nt, LANES), lambda j: (0, j)), pl.BlockSpec((SHORT_CONV, LANES), lambda j: (0, j))],
        out_shape=[jax.ShapeDtypeStruct((nt, 3 * nh * LANES), MXU_DTYPE), jax.ShapeDtypeStruct((SHORT_CONV, 3 * nh * LANES), F32)],
        scratch_shapes=[pltpu.VMEM((nt + 24, LANES), F32), pltpu.VMEM((nt + 24, LANES), F32)], name="conv7_bwd",
        compiler_params=_cparams(("arbitrary",), 10 * _nbytes((nt + 24, LANES), F32)),
    )(h, w_conv, dact_f, dact_b)


_RT = 4
_CP = CONF_K // 2


def _conf_pad_shape(gw, vertical):
    return (gw + 2 * _CP, gw, LANES) if vertical else (gw, gw + 32, LANES)


def _conf_fill(p_ref, val, gw, vertical):
    if vertical:
        p_ref[0:_CP] = jnp.zeros((_CP, gw, LANES), F32)
        p_ref[_CP + gw:2 * _CP + gw] = jnp.zeros((_CP, gw, LANES), F32)
        p_ref[_CP:_CP + gw] = val
    else:
        p_ref[:, 0:16, :] = jnp.zeros((gw, 16, LANES), F32)
        p_ref[:, 16 + gw:32 + gw, :] = jnp.zeros((gw, 16, LANES), F32)
        p_ref[:, 16:16 + gw, :] = val


def _conf_window(p_ref, r0, shift, gw, vertical):
    if vertical:
        return p_ref[pl.ds(r0 + _CP + shift, _RT), :, :]
    return p_ref[pl.ds(r0, _RT), pl.ds(16 + shift, gw), :]


def _conf_conv_fwd(h3, w_dw, *, gw, col0, half, vertical):
    cb_val = (col0 + (half if vertical else 0)) // LANES
    cb_gate = cb_val + 2 * half // LANES
    cb_w = (half if vertical else 0) // LANES

    def body(v_ref, g_ref, w_ref, o_ref, p_ref):
        _conf_fill(p_ref, v_ref[...] * jax.nn.sigmoid(g_ref[...]), gw, vertical)

        def step(t, carry):
            r0 = t * _RT
            acc = jnp.zeros((_RT, gw, LANES), F32)
            for kk in range(CONF_K):
                acc = acc + w_ref[kk:kk + 1, :] * _conf_window(p_ref, r0, kk - _CP, gw, vertical)
            o_ref[pl.ds(r0, _RT)] = acc
            return carry

        lax.fori_loop(0, gw // _RT, step, 0)

    blk = (gw, gw, LANES)
    return pl.pallas_call(
        body, grid=(half // LANES,),
        in_specs=[pl.BlockSpec(blk, lambda j: (0, 0, cb_val + j)), pl.BlockSpec(blk, lambda j: (0, 0, cb_gate + j)),
                  pl.BlockSpec((CONF_K, LANES), lambda j: (0, cb_w + j))],
        out_specs=pl.BlockSpec(blk, lambda j: (0, 0, j)),
        out_shape=jax.ShapeDtypeStruct((gw, gw, half), F32),
        scratch_shapes=[pltpu.VMEM(_conf_pad_shape(gw, vertical), F32)],
        name="conf_conv_fwd_v" if vertical else "conf_conv_fwd_h",
        compiler_params=_cparams(("arbitrary",), 8 * _nbytes(_conf_pad_shape(gw, vertical), F32)),
    )(h3, h3, w_dw)


def _conf_conv_bwd(h3, w_dw, dyc, *, gw, col0, half, vertical):
    cb_val = (col0 + (half if vertical else 0)) // LANES
    cb_gate = cb_val + 2 * half // LANES
    cb_w = (half if vertical else 0) // LANES

    def body(v_ref, g_ref, w_ref, d_ref, dv_ref, dg_ref, dw_ref, py_ref, pd_ref):
        _conf_fill(py_ref, v_ref[...] * jax.nn.sigmoid(g_ref[...]), gw, vertical)
        _conf_fill(pd_ref, d_ref[...], gw, vertical)

        def step(t, carry):
            r0 = t * _RT
            acc = jnp.zeros((_RT, gw, LANES), F32)
            for kk in range(CONF_K):
                acc = acc + w_ref[kk:kk + 1, :] * _conf_window(pd_ref, r0, _CP - kk, gw, vertical)
            val = v_ref[pl.ds(r0, _RT)]
            sg = jax.nn.sigmoid(g_ref[pl.ds(r0, _RT)])
            dv_ref[pl.ds(r0, _RT)] = (acc * sg).astype(dv_ref.dtype)
            dg_ref[pl.ds(r0, _RT)] = (acc * val * sg * (1.0 - sg)).astype(dg_ref.dtype)
            return carry

        lax.fori_loop(0, gw // _RT, step, 0)

        for kk in range(CONF_K):
            def wstep(t, acc, kk=kk):
                r0 = t * _RT
                prod = d_ref[pl.ds(r0, _RT)] * _conf_window(py_ref, r0, kk - _CP, gw, vertical)
                return acc + jnp.sum(prod, axis=0)

            tot = lax.fori_loop(0, gw // _RT, wstep, jnp.zeros((gw, LANES), F32))
            dw_ref[kk:kk + 1, :] = jnp.sum(tot, axis=0, keepdims=True)

    blk = (gw, gw, LANES)
    pshape = _conf_pad_shape(gw, vertical)
    return pl.pallas_call(
        body, grid=(half // LANES,),
        in_specs=[pl.BlockSpec(blk, lambda j: (0, 0, cb_val + j)), pl.BlockSpec(blk, lambda j: (0, 0, cb_gate + j)),
                  pl.BlockSpec((CONF_K, LANES), lambda j: (0, cb_w + j)), pl.BlockSpec(blk, lambda j: (0, 0, j))],
        out_specs=[pl.BlockSpec(blk, lambda j: (0, 0, j)), pl.BlockSpec(blk, lambda j: (0, 0, j)),
                   pl.BlockSpec((CONF_K, LANES), lambda j: (0, j))],
        out_shape=[jax.ShapeDtypeStruct((gw, gw, half), MXU_DTYPE), jax.ShapeDtypeStruct((gw, gw, half), MXU_DTYPE),
                   jax.ShapeDtypeStruct((CONF_K, half), F32)],
        scratch_shapes=[pltpu.VMEM(pshape, F32), pltpu.VMEM(pshape, F32)],
        name="conf_conv_bwd_v" if vertical else "conf_conv_bwd_h",
        compiler_params=_cparams(("arbitrary",), 12 * _nbytes(pshape, F32)),
    )(h3, h3, w_dw, dyc)


_INV_BASE = 8


def _tri_inv_raw(mats):
    c = mats[0].shape[0]
    ri = lax.broadcasted_iota(jnp.int32, (c, c), 0)
    ci = lax.broadcasted_iota(jnp.int32, (c, c), 1)
    eye = (ri == ci).astype(F32)
    base = min(_INV_BASE, c)
    same = lambda size: (ri // size) == (ci // size)
    dmat = [jnp.where(same(base), a, 0.0) for a in mats]
    x = [eye - dm for dm in dmat]
    pw = [_raw_dot(dm, dm, "nn", "hi3") for dm in dmat]
    span = 2
    while span < base:
        x = [xi + _raw_dot(xi, p, "nn", "hi3") for xi, p in zip(x, pw)]
        span *= 2
        if span < base:
            pw = [_raw_dot(p, p, "nn", "hi3") for p in pw]
    size = base
    while size < c:
        off = jnp.logical_and(jnp.logical_not(same(size)), same(2 * size))
        t = [_raw_dot(xi, jnp.where(off, a, 0.0), "nn", "hi3") for xi, a in zip(x, mats)]
        x = [xi - _raw_dot(ti, xi, "nn", "hi3") for xi, ti in zip(x, t)]
        size *= 2
    return x


@jax.custom_vjp
def _tri_inv(mats):
    return _tri_inv_raw(mats)


def _tri_inv_fwd(mats):
    x = _tri_inv_raw(mats)
    return x, x


def _tri_inv_bwd(x, dx):
    t = [_raw_dot(xi, di, "tn", "hi3") for xi, di in zip(x, dx)]
    return ([-_raw_dot(ti, xi, "nt", "hi3") for ti, xi in zip(t, x)],)


_tri_inv.defvjp(_tri_inv_fwd, _tri_inv_bwd)


@jax.custom_vjp
def _tri_inv_saved(mats, saved):
    return saved


def _tri_inv_saved_fwd(mats, saved):
    return saved, saved


def _tri_inv_saved_bwd(x, dx):
    return _tri_inv_bwd(x, dx)[0], [jnp.zeros_like(xi) for xi in x]


_tri_inv_saved.defvjp(_tri_inv_saved_fwd, _tri_inv_saved_bwd)


def _delta_chunk(qa, ka, va, graw, braw, alog, dtb, s, *, rev, minv_saved=None, with_minv=False):
    c = qa[0].shape[0]

    def each(f, *lists):
        return [f(*xs) for xs in zip(*lists, strict=True)]

    def l2n(t):
        return t * lax.rsqrt(jnp.sum(t * t, axis=-1, keepdims=True) + 1e-6)

    ri = lax.broadcasted_iota(jnp.int32, (c, c), 0)
    ci = lax.broadcasted_iota(jnp.int32, (c, c), 1)
    incl = (ci >= ri) if rev else (ci <= ri)
    strict = (ci > ri) if rev else (ci < ri)
    tmat = incl.astype(F32)
    tmat_t = ((ri >= ci) if rev else (ri <= ci)).astype(F32)

    q = each(lambda t: l2n(t) * (HD ** -0.5), qa)
    k = each(l2n, ka)
    g = each(lambda al, gr, dt: -jnp.exp(al) * _softplus(gr + dt), alog, graw, dtb)
    beta = each(jax.nn.sigmoid, braw)
    gc_wide = each(lambda t: _xdot(tmat, jnp.broadcast_to(t, (c, HD)), "nn", "a"), g)
    gc_rows = gc_wide if c == HD else each(lambda t: _xdot(tmat, jnp.broadcast_to(t, (c, c)), "nn", "a"), g)
    gc_cols = each(lambda t: _xdot(jnp.broadcast_to(t, (c, c)), tmat_t, "tn", "b"), g)
    gamma = each(lambda r, cc: jnp.where(incl, jnp.exp(jnp.where(incl, r - cc, 0.0)), 0.0), gc_rows, gc_cols)
    kb = each(lambda a, b: a * b, k, beta)
    a_mat = each(lambda a, b, gm: jnp.where(strict, _dot(a, b, "nt", "mxu") * gm, 0.0), kb, k, gamma)
    minv = _tri_inv(a_mat) if minv_saved is None else _tri_inv_saved(a_mat, minv_saved)
    eg = each(jnp.exp, gc_wide)
    u = each(lambda mi, v, b: _dot(mi, v * b, "nn", "hi3"), minv, va, beta)
    w = each(lambda mi, a, e: _dot(mi, a * e, "nn", "hi3"), minv, kb, eg)
    attn = each(lambda a, b, gm: _dot(a, b, "nt", "mxu") * gm, q, k, gamma)
    tot = each(lambda t: jnp.sum(t, axis=0, keepdims=True), g)
    q_dec = each(lambda a, e: a * e, q, eg)
    k_dec = each(lambda a, t, gw_: a * jnp.exp(t - gw_), k, tot, gc_wide)
    v_new = each(lambda uu, ww, ss: uu - _dot(ww, ss, "nn", "mxu"), u, w, s)
    o = each(lambda qd, ss, at, vn: _dot(qd, ss, "nn", "mxu") + _dot(at, vn, "nn", "mxu"), q_dec, s, attn, v_new)
    s_new = each(lambda ss, t, kd, vn: ss * jnp.exp(t) + _dot(kd, vn, "tn", "mxu"), s, tot, k_dec, v_new)
    return (o, s_new, minv) if with_minv else (o, s_new)


def _delta_args(a_ref, g_ref, al_ref, dt_ref, s_ref, hb, gcol):
    gts = [g_ref[hh] for hh in range(hb)]
    return ([a_ref[0, hh] for hh in range(hb)], [a_ref[1, hh] for hh in range(hb)], [a_ref[2, hh] for hh in range(hb)],
            [t[:, gcol:gcol + 1] for t in gts], [t[:, gcol + 1:gcol + 2] for t in gts],
            [al_ref[hh, 0:1, 0:1] for hh in range(hb)], [dt_ref[hh, 0:1, 0:1] for hh in range(hb)],
            [s_ref[hh] for hh in range(hb)])


def _delta_chunk_of_step(step, nch, nlc, rev):
    return (nch - 1 - step) if rev else (step + nlc) % nch


def _delta_fwd(act, gates_hm, alog_b, dtb_b, *, n_lat, n_ctx, nh, hb, rev):
    nt = n_lat + n_ctx
    nch, nlc = nt // CHUNK, n_lat // CHUNK
    gcol = 2 if rev else 0
    chunk = functools.partial(_delta_chunk_of_step, nch=nch, nlc=nlc, rev=rev)

    def body(a_ref, g_ref, al_ref, dt_ref, o_ref, st_ref, mi_ref, s_ref):
        n = pl.program_id(1)

        @pl.when(n == 0)
        def _():
            s_ref[...] = jnp.zeros_like(s_ref)

        args = _delta_args(a_ref, g_ref, al_ref, dt_ref, s_ref, hb, gcol)
        o, s_new, minv = _delta_chunk(*args, rev=rev, with_minv=True)
        for hh in range(hb):
            st_ref[hh] = args[7][hh]
            mi_ref[hh] = minv[hh]
            o_ref[:, hh * HD:(hh + 1) * HD] = o[hh]
            s_ref[hh] = s_new[hh]

    par = pl.BlockSpec((hb, 8, LANES), lambda h, n: (h, 0, 0))
    return pl.pallas_call(
        body, grid=(nh // hb, nch),
        in_specs=[pl.BlockSpec((3, hb, CHUNK, HD), lambda h, n: (0, h, chunk(n), 0)),
                  pl.BlockSpec((hb, CHUNK, 4), lambda h, n: (h, chunk(n), 0)), par, par],
        out_specs=[pl.BlockSpec((CHUNK, hb * HD), lambda h, n: (chunk(n), h)),
                   pl.BlockSpec((hb, None, HD, HD), lambda h, n: (h, n, 0, 0)),
                   pl.BlockSpec((hb, None, CHUNK, CHUNK), lambda h, n: (h, n, 0, 0))],
        out_shape=[jax.ShapeDtypeStruct((nt, nh * HD), F32), jax.ShapeDtypeStruct((nh, nch, HD, HD), F32),
                   jax.ShapeDtypeStruct((nh, nch, CHUNK, CHUNK), F32)],
        scratch_shapes=[pltpu.VMEM((hb, HD, HD), F32)], name="delta_fwd_rev" if rev else "delta_fwd",
        compiler_params=_cparams(("arbitrary", "arbitrary"), 0),
    )(act, gates_hm, alog_b, dtb_b)


def _delta_bwd(act, gates_hm, alog_b, dtb_b, states, minvs, do, *, n_lat, n_ctx, nh, hb, rev):
    nt = n_lat + n_ctx
    nch, nlc = nt // CHUNK, n_lat // CHUNK
    gcol = 2 if rev else 0

    def chunk(m):
        return _delta_chunk_of_step(nch - 1 - m, nch, nlc, rev)

    def body(a_ref, g_ref, al_ref, dt_ref, st_ref, mi_ref, do_ref, da_ref, dg_ref, dal_ref, ddt_ref, ds_ref):
        m = pl.program_id(1)

        @pl.when(m == 0)
        def _():
            ds_ref[...] = jnp.zeros_like(ds_ref)
            dal_ref[...] = jnp.zeros_like(dal_ref)
            ddt_ref[...] = jnp.zeros_like(ddt_ref)

        is_lat = chunk(m) < nlc
        args = _delta_args(a_ref, g_ref, al_ref, dt_ref, st_ref, hb, gcol)
        _, vjp = jax.vjp(functools.partial(_delta_chunk, rev=rev, minv_saved=[mi_ref[hh] for hh in range(hb)]), *args)
        do_h = [jnp.where(is_lat, do_ref[:, hh * HD:(hh + 1) * HD], 0.0) for hh in range(hb)]
        grads = vjp((do_h, [ds_ref[hh] for hh in range(hb)]))
        for hh in range(hb):
            dq, dk, dv, dgr, dbr, dal, ddt, ds = [gr[hh] for gr in grads]
            da_ref[0, hh] = dq
            da_ref[1, hh] = dk
            da_ref[2, hh] = dv
            lane = lax.broadcasted_iota(jnp.int32, (CHUNK, 4), 1)
            dg_ref[hh] = jnp.where(lane == 0, dgr, jnp.where(lane == 1, dbr, 0.0))
            dal_ref[hh] += jnp.broadcast_to(dal, (8, LANES))
            ddt_ref[hh] += jnp.broadcast_to(ddt, (8, LANES))
            ds_ref[hh] = ds

    par = pl.BlockSpec((hb, 8, LANES), lambda h, m: (h, 0, 0))
    return pl.pallas_call(
        body, grid=(nh // hb, nch),
        in_specs=[pl.BlockSpec((3, hb, CHUNK, HD), lambda h, m: (0, h, chunk(m), 0)),
                  pl.BlockSpec((hb, CHUNK, 4), lambda h, m: (h, chunk(m), 0)), par, par,
                  pl.BlockSpec((hb, None, HD, HD), lambda h, m: (h, nch - 1 - m, 0, 0)),
                  pl.BlockSpec((hb, None, CHUNK, CHUNK), lambda h, m: (h, nch - 1 - m, 0, 0)),
                  pl.BlockSpec((CHUNK, hb * HD), lambda h, m: (jnp.minimum(chunk(m), nlc - 1), h))],
        out_specs=[pl.BlockSpec((3, hb, CHUNK, HD), lambda h, m: (0, h, chunk(m), 0)),
                   pl.BlockSpec((hb, CHUNK, 4), lambda h, m: (h, chunk(m), 0)), par, par],
        out_shape=[jax.ShapeDtypeStruct((3, nh, nt, HD), F32), jax.ShapeDtypeStruct((nh, nt, 4), F32),
                   jax.ShapeDtypeStruct((nh, 8, LANES), F32), jax.ShapeDtypeStruct((nh, 8, LANES), F32)],
        scratch_shapes=[pltpu.VMEM((hb, HD, HD), F32)], name="delta_bwd_rev" if rev else "delta_bwd",
        compiler_params=_cparams(("arbitrary", "arbitrary"), 0),
    )(act, gates_hm, alog_b, dtb_b, states, minvs, do)


def _my_pos():
    return lax.axis_index("x"), lax.axis_index("y"), lax.axis_index("c")


def _flip(v, d):
    return 1 - v if d else v


def _bcast8(block, name):
    r, w = block.shape
    assert r % 8 == 0 and w % LANES == 0

    def body(x_ref, o_ref, send_sems, recv_sems, local_sem):
        x, y, c = _my_pos()

        def rows(px, py, pc):
            return o_ref.at[pl.ds((4 * px + 2 * py + pc) * r, r), :]

        mine = pltpu.make_async_copy(x_ref, rows(x, y, c), local_sem)
        mine.start()
        copies = []
        for kk in range(1, 8):
            dx, dy, dc = (kk >> 2) & 1, (kk >> 1) & 1, kk & 1
            peer = (_flip(x, dx), _flip(y, dy), _flip(c, dc))
            cp = pltpu.make_async_remote_copy(src_ref=x_ref, dst_ref=rows(x, y, c), send_sem=send_sems.at[kk - 1],
                                              recv_sem=recv_sems.at[kk - 1], device_id=peer, device_id_type=pl.DeviceIdType.MESH)
            cp.start()
            copies.append((cp, peer))
        for kk, (cp, peer) in enumerate(copies):
            pltpu.make_async_remote_copy(src_ref=x_ref, dst_ref=rows(*peer), send_sem=send_sems.at[kk], recv_sem=recv_sems.at[kk],
                                         device_id=peer, device_id_type=pl.DeviceIdType.MESH).wait_recv()
        for cp, _ in copies:
            cp.wait_send()
        mine.wait()

    return pl.pallas_call(
        body, out_shape=jax.ShapeDtypeStruct((8 * r, w), block.dtype),
        in_specs=[pl.BlockSpec(memory_space=pltpu.VMEM)], out_specs=pl.BlockSpec(memory_space=pltpu.VMEM),
        scratch_shapes=[pltpu.SemaphoreType.DMA((7,)), pltpu.SemaphoreType.DMA((7,)), pltpu.SemaphoreType.DMA], name=name,
        compiler_params=pltpu.CompilerParams(vmem_limit_bytes=int(max(32 * 1024 * 1024, 12 * _nbytes((r, w), block.dtype)))),
    )(block)


_CHIP_PEERS = ((1, 0), (0, 1), (1, 1))


def _col_half(ref, c, lead):
    hc = ref.shape[-1] // 2
    return ref.at[(*lead, slice(None), pl.ds(pl.multiple_of(c * hc, LANES), hc))]


def _into_slot(w, chip_idx, after, name):
    r, cc = w.shape
    tr = _pick(r, 512, 16)

    def body(i_ref, w_ref, after_ref, o_ref):
        o_ref[...] = w_ref[...].astype(o_ref.dtype)

    grid_spec = pltpu.PrefetchScalarGridSpec(
        num_scalar_prefetch=1, grid=(r // tr,), in_specs=[pl.BlockSpec((tr, cc), lambda i, s: (i, 0)), ANY],
        out_specs=pl.BlockSpec((None, tr, cc), lambda i, s: (s[0], i, 0)))
    return pl.pallas_call(body, grid_spec=grid_spec, out_shape=jax.ShapeDtypeStruct((4, r, cc), MXU_DTYPE), name=name,
                          compiler_params=_cparams(("arbitrary",), 6 * _nbytes((tr, cc), F32)))(chip_idx, w, after)


_HBM = pl.BlockSpec(memory_space=pltpu.HBM)
_SEM = pl.BlockSpec(memory_space=pltpu.SEMAPHORE)
_DATAFLOW = pltpu.SideEffectType.DATAFLOW_SIDE_EFFECTING


def _in_hbm(a):
    return pltpu.with_memory_space_constraint(a, pltpu.HBM)


def _gather_start(bufs, after, name):
    n = len(bufs)

    def body(*refs):
        ins, send_sems, recv_sems, token = refs[:n], refs[n + 1], refs[n + 2], refs[-1]
        x, y, c = _my_pos()
        for a in range(n):
            piece = _col_half(ins[a], c, (2 * x + y,))
            for kk, (dx, dy) in enumerate(_CHIP_PEERS):
                pltpu.make_async_remote_copy(src_ref=piece, dst_ref=piece, send_sem=send_sems.at[3 * a + kk], recv_sem=recv_sems.at[3 * a + kk],
                                             device_id=(_flip(x, dx), _flip(y, dy), c), device_id_type=pl.DeviceIdType.MESH).start()
        token[...] = jnp.zeros_like(token)

    res = pl.pallas_call(
        body, name=name,
        out_shape=(pltpu.SemaphoreType.DMA((3 * n,)), pltpu.SemaphoreType.DMA((3 * n,)), *[pltpu.HBM(b.shape, b.dtype) for b in bufs],
                   jax.ShapeDtypeStruct((8, LANES), F32)),
        in_specs=[_HBM] * n + [ANY], out_specs=(_SEM, _SEM, *[_HBM] * n, pl.BlockSpec(memory_space=pltpu.VMEM)),
        input_output_aliases={a: 2 + a for a in range(n)}, compiler_params=pltpu.CompilerParams(has_side_effects=_DATAFLOW),
    )(*[_in_hbm(b) for b in bufs], after)
    return res[0], res[1], list(res[2:2 + n]), res[-1]


def _gather_wait(send_sems, recv_sems, bufs, after, name):
    n = len(bufs)

    def body(*refs):
        ins, s_sems, r_sems = refs[:n], refs[n], refs[n + 1]
        x, y, c = _my_pos()
        for a in range(n):
            for kk, (dx, dy) in enumerate(_CHIP_PEERS):
                px, py = _flip(x, dx), _flip(y, dy)
                cp = pltpu.make_async_remote_copy(src_ref=_col_half(ins[a], c, (2 * x + y,)), dst_ref=_col_half(ins[a], c, (2 * px + py,)),
                                                  send_sem=s_sems.at[3 * a + kk], recv_sem=r_sems.at[3 * a + kk], device_id=(px, py, c),
                                                  device_id_type=pl.DeviceIdType.MESH)
                cp.wait_send()
                cp.wait_recv()

    return pl.pallas_call(
        body, name=name, out_shape=[pltpu.HBM(b.shape, b.dtype) for b in bufs],
        in_specs=[_HBM] * n + [_SEM, _SEM] + [ANY] * len(after), out_specs=[_HBM] * n, input_output_aliases={a: a for a in range(n)},
        compiler_params=pltpu.CompilerParams(has_side_effects=_DATAFLOW),
    )(*bufs, send_sems, recv_sems, *after)


def _gather_forward(bufs, name):
    n = len(bufs)

    def body(*refs):
        outs = refs[n:2 * n]
        send_sems, recv_sems = refs[2 * n:]
        x, y, c = _my_pos()
        sib = (x, y, 1 - c)

        def remote(a, kk, half):
            dx, dy = _CHIP_PEERS[kk]
            piece = _col_half(outs[a], half, (2 * _flip(x, dx) + _flip(y, dy),))
            return pltpu.make_async_remote_copy(src_ref=piece, dst_ref=piece, send_sem=send_sems.at[3 * a + kk], recv_sem=recv_sems.at[3 * a + kk],
                                                device_id=sib, device_id_type=pl.DeviceIdType.MESH)

        sends = [remote(a, kk, c) for a in range(n) for kk in range(3)]
        for cp in sends:
            cp.start()
        for a in range(n):
            for kk in range(3):
                remote(a, kk, 1 - c).wait_recv()
        for cp in sends:
            cp.wait_send()

    return pl.pallas_call(
        body, out_shape=[jax.ShapeDtypeStruct(b.shape, b.dtype) for b in bufs],
        in_specs=[ANY] * n, out_specs=[ANY] * n, input_output_aliases={a: a for a in range(n)},
        scratch_shapes=[pltpu.SemaphoreType.DMA((3 * n,))] * 2, name=name,
    )(*bufs)


def _forward_start(bufs, after, name):
    n = len(bufs)

    def body(*refs):
        ins, send_sems, recv_sems, token = refs[:n], refs[n + 1], refs[n + 2], refs[-1]
        x, y, c = _my_pos()
        for a in range(n):
            for kk, (dx, dy) in enumerate(_CHIP_PEERS):
                piece = _col_half(ins[a], c, (2 * _flip(x, dx) + _flip(y, dy),))
                pltpu.make_async_remote_copy(src_ref=piece, dst_ref=piece, send_sem=send_sems.at[3 * a + kk], recv_sem=recv_sems.at[3 * a + kk],
                                             device_id=(x, y, 1 - c), device_id_type=pl.DeviceIdType.MESH).start()
        token[...] = jnp.zeros_like(token)

    res = pl.pallas_call(
        body, name=name,
        out_shape=(pltpu.SemaphoreType.DMA((3 * n,)), pltpu.SemaphoreType.DMA((3 * n,)), *[pltpu.HBM(b.shape, b.dtype) for b in bufs],
                   jax.ShapeDtypeStruct((8, LANES), F32)),
        in_specs=[_HBM] * n + [ANY], out_specs=(_SEM, _SEM, *[_HBM] * n, pl.BlockSpec(memory_space=pltpu.VMEM)),
        input_output_aliases={a: 2 + a for a in range(n)}, compiler_params=pltpu.CompilerParams(has_side_effects=_DATAFLOW),
    )(*[_in_hbm(b) for b in bufs], after)
    return res[0], res[1], list(res[2:2 + n]), res[-1]


def _forward_wait(send_sems, recv_sems, bufs, after, name):
    n = len(bufs)

    def body(*refs):
        ins, s_sems, r_sems = refs[:n], refs[n], refs[n + 1]
        x, y, c = _my_pos()
        for a in range(n):
            for kk, (dx, dy) in enumerate(_CHIP_PEERS):
                slot = 2 * _flip(x, dx) + _flip(y, dy)
                cp = pltpu.make_async_remote_copy(src_ref=_col_half(ins[a], c, (slot,)), dst_ref=_col_half(ins[a], 1 - c, (slot,)),
                                                  send_sem=s_sems.at[3 * a + kk], recv_sem=r_sems.at[3 * a + kk], device_id=(x, y, 1 - c),
                                                  device_id_type=pl.DeviceIdType.MESH)
                cp.wait_send()
                cp.wait_recv()

    return pl.pallas_call(
        body, name=name, out_shape=[pltpu.HBM(b.shape, b.dtype) for b in bufs],
        in_specs=[_HBM] * n + [_SEM, _SEM] + [ANY] * len(after), out_specs=[_HBM] * n, input_output_aliases={a: a for a in range(n)},
        compiler_params=pltpu.CompilerParams(has_side_effects=_DATAFLOW),
    )(*bufs, send_sems, recv_sems, *after)


def _scatter_start(parts, after, name):
    n = len(parts)

    def body(*refs):
        ins, lands, send_sems, recv_sems, token = refs[:n], refs[n:2 * n], refs[2 * n + 1], refs[2 * n + 2], refs[-1]
        x, y, c = _my_pos()
        for a in range(n):
            for kk, (dx, dy) in enumerate(_CHIP_PEERS):
                px, py = _flip(x, dx), _flip(y, dy)
                pltpu.make_async_remote_copy(src_ref=ins[a].at[2 * px + py], dst_ref=lands[a].at[2 * x + y], send_sem=send_sems.at[3 * a + kk],
                                             recv_sem=recv_sems.at[3 * a + kk], device_id=(px, py, c), device_id_type=pl.DeviceIdType.MESH).start()
        token[...] = jnp.zeros_like(token)

    thru = [pltpu.HBM(p.shape, p.dtype) for p in parts]
    res = pl.pallas_call(
        body, name=name,
        out_shape=(pltpu.SemaphoreType.DMA((3 * n,)), pltpu.SemaphoreType.DMA((3 * n,)), *thru, *thru, jax.ShapeDtypeStruct((8, LANES), F32)),
        in_specs=[_HBM] * (2 * n) + [ANY], out_specs=(_SEM, _SEM, *[_HBM] * (2 * n), pl.BlockSpec(memory_space=pltpu.VMEM)),
        input_output_aliases={a: 2 + a for a in range(2 * n)}, compiler_params=pltpu.CompilerParams(has_side_effects=_DATAFLOW),
    )(*[_in_hbm(p) for p in parts], *[_in_hbm(lax.empty(p.shape, p.dtype)) for p in parts], after)
    return res[0], res[1], list(res[2:2 + n]), list(res[2 + n:2 + 2 * n]), res[-1]


def _scatter_wait(send_sems, recv_sems, parts, lands, after, name):
    n = len(parts)

    def body(*refs):
        ins, lnd, s_sems, r_sems = refs[:n], refs[n:2 * n], refs[2 * n], refs[2 * n + 1]
        x, y, c = _my_pos()
        for a in range(n):
            for kk, (dx, dy) in enumerate(_CHIP_PEERS):
                px, py = _flip(x, dx), _flip(y, dy)
                cp = pltpu.make_async_remote_copy(src_ref=ins[a].at[2 * px + py], dst_ref=lnd[a].at[2 * px + py], send_sem=s_sems.at[3 * a + kk],
                                                  recv_sem=r_sems.at[3 * a + kk], device_id=(px, py, c), device_id_type=pl.DeviceIdType.MESH)
                cp.wait_send()
                cp.wait_recv()

    thru = [pltpu.HBM(p.shape, p.dtype) for p in parts]
    res = pl.pallas_call(
        body, name=name, out_shape=[*thru, *thru],
        in_specs=[_HBM] * (2 * n) + [_SEM, _SEM] + [ANY] * len(after), out_specs=[_HBM] * (2 * n),
        input_output_aliases={a: a for a in range(2 * n)}, compiler_params=pltpu.CompilerParams(has_side_effects=_DATAFLOW),
    )(*parts, *lands, send_sems, recv_sems, *after)
    return list(res[:n]), list(res[n:])


def _pair_send_start(grads, after, name):
    n = len(grads)

    def body(*refs):
        ins, lands, send_sems, recv_sems, token = refs[:n], refs[n:2 * n], refs[2 * n + 1], refs[2 * n + 2], refs[-1]
        x, y, c = _my_pos()
        for a in range(n):
            pltpu.make_async_remote_copy(src_ref=_col_half(ins[a], 1 - c, (slice(None),)), dst_ref=lands[a], send_sem=send_sems.at[a],
                                         recv_sem=recv_sems.at[a], device_id=(x, y, 1 - c), device_id_type=pl.DeviceIdType.MESH).start()
        token[...] = jnp.zeros_like(token)

    land_shapes = [(4, g.shape[1], g.shape[2] // 2) for g in grads]
    res = pl.pallas_call(
        body, name=name,
        out_shape=(pltpu.SemaphoreType.DMA((n,)), pltpu.SemaphoreType.DMA((n,)), *[pltpu.HBM(g.shape, g.dtype) for g in grads],
                   *[pltpu.HBM(s, g.dtype) for s, g in zip(land_shapes, grads)], jax.ShapeDtypeStruct((8, LANES), F32)),
        in_specs=[_HBM] * (2 * n) + [ANY], out_specs=(_SEM, _SEM, *[_HBM] * (2 * n), pl.BlockSpec(memory_space=pltpu.VMEM)),
        input_output_aliases={a: 2 + a for a in range(2 * n)}, compiler_params=pltpu.CompilerParams(has_side_effects=_DATAFLOW),
    )(*[_in_hbm(g) for g in grads], *[_in_hbm(lax.empty(s, g.dtype)) for s, g in zip(land_shapes, grads)], after)
    return res[0], res[1], list(res[2:2 + n]), list(res[2 + n:2 + 2 * n]), res[-1]


def _pair_send_wait(send_sems, recv_sems, grads, lands, after, name):
    n = len(grads)

    def body(*refs):
        ins, lnd, s_sems, r_sems = refs[:n], refs[n:2 * n], refs[2 * n], refs[2 * n + 1]
        x, y, c = _my_pos()
        for a in range(n):
            cp = pltpu.make_async_remote_copy(src_ref=_col_half(ins[a], 1 - c, (slice(None),)), dst_ref=lnd[a], send_sem=s_sems.at[a],
                                              recv_sem=r_sems.at[a], device_id=(x, y, 1 - c), device_id_type=pl.DeviceIdType.MESH)
            cp.wait_send()
            cp.wait_recv()

    res = pl.pallas_call(
        body, name=name, out_shape=[*[pltpu.HBM(g.shape, g.dtype) for g in grads], *[pltpu.HBM(l.shape, l.dtype) for l in lands]],
        in_specs=[_HBM] * (2 * n) + [_SEM, _SEM, ANY], out_specs=[_HBM] * (2 * n), input_output_aliases={a: a for a in range(2 * n)},
        compiler_params=pltpu.CompilerParams(has_side_effects=_DATAFLOW),
    )(*grads, *lands, send_sems, recv_sems, after)
    return list(res[:n]), list(res[n:])


def _pair_join_halves(bufs, name):
    n = len(bufs)

    def body(*refs):
        outs = refs[n:2 * n]
        send_sems, recv_sems = refs[2 * n:]
        x, y, c = _my_pos()
        sib = (x, y, 1 - c)
        sends = []
        for a in range(n):
            mine = _col_half(outs[a], c, ())
            cp = pltpu.make_async_remote_copy(src_ref=mine, dst_ref=mine, send_sem=send_sems.at[a], recv_sem=recv_sems.at[a],
                                              device_id=sib, device_id_type=pl.DeviceIdType.MESH)
            cp.start()
            sends.append(cp)
        for a in range(n):
            other = _col_half(outs[a], 1 - c, ())
            pltpu.make_async_remote_copy(src_ref=other, dst_ref=other, send_sem=send_sems.at[a], recv_sem=recv_sems.at[a],
                                         device_id=sib, device_id_type=pl.DeviceIdType.MESH).wait_recv()
        for cp in sends:
            cp.wait_send()

    return pl.pallas_call(
        body, out_shape=[jax.ShapeDtypeStruct(b.shape, b.dtype) for b in bufs],
        in_specs=[ANY] * n, out_specs=[ANY] * n, input_output_aliases={a: a for a in range(n)},
        scratch_shapes=[pltpu.SemaphoreType.DMA((n,)), pltpu.SemaphoreType.DMA((n,))], name=name,
    )(*bufs)


def _add_my_half(g, recv, c_idx, name):
    _, r, cc = g.shape
    hc = cc // 2
    tc = _pick(hc, 256)
    per = hc // tc

    def body(c_ref, g_ref, r_ref, o_ref):
        o_ref[...] = (g_ref[...] + r_ref[...]).astype(o_ref.dtype)

    grid_spec = pltpu.PrefetchScalarGridSpec(
        num_scalar_prefetch=1, grid=(4, per),
        in_specs=[pl.BlockSpec((None, r, tc), lambda s, j, c_ref: (s, 0, c_ref[0] * per + j)),
                  pl.BlockSpec((None, r, tc), lambda s, j, c_ref: (s, 0, j))],
        out_specs=pl.BlockSpec((None, r, tc), lambda s, j, c_ref: (s, 0, j)))
    return pl.pallas_call(body, grid_spec=grid_spec, out_shape=jax.ShapeDtypeStruct((4, r, hc), WIRE_DTYPE), name=name,
                          compiler_params=_cparams(("arbitrary", "arbitrary"), 6 * _nbytes((r, tc), F32)))(c_idx, g, recv)


def _sum_slots(own, recv, slots, name):
    _, r, hc = own.shape
    tc = _pick(hc, 256)
    per = hc // tc

    def body(s0, s1, s2, s3, s4, a_ref, b_ref, c_ref, d_ref, o_ref):
        f = lambda ref: ref[...].astype(F32)
        o_ref[...] = ((f(a_ref) + f(b_ref)) + f(c_ref)) + f(d_ref)

    slot = lambda i: pl.BlockSpec((None, r, tc), lambda j, *s: (s[i][0], 0, j))
    grid_spec = pltpu.PrefetchScalarGridSpec(
        num_scalar_prefetch=5, grid=(per,), in_specs=[slot(0), slot(1), slot(2), slot(3)],
        out_specs=pl.BlockSpec((r, tc), lambda j, *s: (0, s[4][0] * per + j)))
    return pl.pallas_call(body, grid_spec=grid_spec, out_shape=jax.ShapeDtypeStruct((r, 2 * hc), F32), name=name,
                          compiler_params=_cparams(("arbitrary",), 14 * _nbytes((r, tc), F32)))(*slots, own, recv, recv, recv)


def _adamw_math(w, g, m, v):
    m = ADAM_B1 * m + (1.0 - ADAM_B1) * g
    v = ADAM_B2 * v + (1.0 - ADAM_B2) * (g * g)
    m_hat = m / (1.0 - ADAM_B1 ** ADAM_STEP)
    v_hat = v / (1.0 - ADAM_B2 ** ADAM_STEP)
    delta = -ADAM_LR * (m_hat / (jnp.sqrt(v_hat) + ADAM_EPS) + ADAM_WD * w)
    return delta, m, v


def _adamw(w, g, m, v, name):
    r, cc = w.shape[-2:]
    tr = _pick(r, max(8, (512 * 1024) // max(cc, 1)), 8)

    def body(w_ref, g_ref, m_ref, v_ref, d_ref, nm_ref, nv_ref):
        d, nm, nv = _adamw_math(w_ref[...], g_ref[...], m_ref[...], v_ref[...])
        d_ref[...] = d
        nm_ref[...] = nm
        nv_ref[...] = nv

    if w.ndim == 3:
        spec = pl.BlockSpec((None, tr, cc), lambda i: (0, i, 0))
    else:
        spec = pl.BlockSpec((tr, cc), lambda i: (i, 0))
    return pl.pallas_call(
        body, grid=(r // tr,), in_specs=[spec] * 4, out_specs=[spec] * 3,
        out_shape=[jax.ShapeDtypeStruct(w.shape, F32)] * 3, name=name,
        compiler_params=_cparams(("arbitrary",), 16 * _nbytes((tr, cc), F32)))(w, g, m, v)


def _sum8(allv, name):
    r8, w = allv.shape
    r = r8 // 8

    def body(a_ref, o_ref):
        acc = a_ref[0:r, :]
        for d in range(1, 8):
            acc = acc + a_ref[d * r:(d + 1) * r, :]
        o_ref[...] = acc

    return pl.pallas_call(body, out_shape=jax.ShapeDtypeStruct((r, w), F32), name=name,
                          compiler_params=_cparams((), 12 * _nbytes((r8, w), F32)))(allv)


def _dmod_prep(alld, name):
    _, w = alld.shape
    tw = _pick(w, 2048)

    def body(a_ref, d_ref, b_ref):
        ctx = a_ref[1:2, :]
        tot = a_ref[0:1, :]
        d_ref[0:1, :] = tot
        for d in range(1, 8):
            ctx = ctx + a_ref[8 * d + 1:8 * d + 2, :]
            tot = tot + a_ref[8 * d:8 * d + 1, :]
            d_ref[d:d + 1, :] = a_ref[8 * d:8 * d + 1, :]
        d_ref[8:9, :] = ctx
        d_ref[9:16, :] = jnp.zeros((7, tw), F32)
        b_ref[...] = tot + ctx

    return pl.pallas_call(
        body, grid=(w // tw,), in_specs=[pl.BlockSpec((64, tw), lambda j: (0, j))],
        out_specs=[pl.BlockSpec((16, tw), lambda j: (0, j)), pl.BlockSpec((1, tw), lambda j: (0, j))],
        out_shape=[jax.ShapeDtypeStruct((16, w), F32), jax.ShapeDtypeStruct((1, w), F32)], name=name,
        compiler_params=_cparams(("arbitrary",), 0))(alld)


def _pack(arrs, rows=8):
    flat = jnp.concatenate([a.reshape(-1).astype(F32) for a in arrs])
    per = rows * LANES
    n = flat.shape[0]
    padded = -(-n // per) * per
    flat = jnp.pad(flat, (0, padded - n))
    return flat.reshape(rows, padded // rows)


def _unpack(slab, shapes):
    flat = slab.reshape(-1)
    out, off = [], 0
    for s in shapes:
        n = math.prod(s)
        out.append(flat[off:off + n].reshape(s))
        off += n
    return out


def kernel(x, c, ctx, c_ctx, ln_in_g, ln_in_b, w_mod, b_mod, w_in, w_qkv_conv, a_log_f, dt_bias_f, a_log_b, dt_bias_b, dn_norm_g, conf_dw_w, conf_dw_b, conf_ln_g, conf_ln_b, w_out, ln1_g, ln1_b, w_mlp1, b_mlp1, w_mlp2, b_mlp2, ln2_g, ln2_b, loss_target, m_c_ctx, m_ln_in_g, m_ln_in_b, m_w_mod, m_b_mod, m_w_in, m_w_qkv_conv, m_a_log_f, m_dt_bias_f, m_a_log_b, m_dt_bias_b, m_dn_norm_g, m_conf_dw_w, m_conf_dw_b, m_conf_ln_g, m_conf_ln_b, m_w_out, m_ln1_g, m_ln1_b, m_w_mlp1, m_b_mlp1, m_w_mlp2, m_b_mlp2, m_ln2_g, m_ln2_b, v_c_ctx, v_ln_in_g, v_ln_in_b, v_w_mod, v_b_mod, v_w_in, v_w_qkv_conv, v_a_log_f, v_dt_bias_f, v_a_log_b, v_dt_bias_b, v_dn_norm_g, v_conf_dw_w, v_conf_dw_b, v_conf_ln_g, v_conf_ln_b, v_w_out, v_ln1_g, v_ln1_b, v_w_mlp1, v_b_mlp1, v_w_mlp2, v_b_mlp2, v_ln2_g, v_ln2_b):
    n_lat, d = x.shape[1], x.shape[2]
    n_ctx = ctx.shape[1]
    nt = n_lat + n_ctx
    dn = d // 2
    nh = dn // HD
    conf = d - dn
    half = conf // 2
    gw = math.isqrt(n_lat)
    ff = 4 * w_mlp1.shape[2]
    fl = w_mlp1.shape[2]
    in_cols = 4 * w_in.shape[2]
    r_in = w_in.shape[2]
    n_gate = 4 * nh
    gb = max(LANES, d // 8)
    g0 = 4 * dn + 2 * conf
    in_pad = g0 + gb
    ml = w_mod.shape[2]
    assert gw * gw == n_lat and nt % gw == 0 and in_cols == 4 * dn + n_gate + 2 * conf and nh * HD == dn
    hb = 4 if nh % 4 == 0 else (2 if nh % 2 == 0 else 1)

    mx, my, mc = _my_pos()
    chip = 2 * mx + my
    dev = 4 * mx + 2 * my + mc
    c_idx = jnp.reshape(mc, (1,)).astype(jnp.int32)

    x2, ctx2, tgt2 = x[0], ctx[0], loss_target[0]
    row = lambda a: a.reshape(1, -1)

    e0_shapes = [(d,), w_qkv_conv.shape[1:], conf_dw_w.shape[1:]]
    e0 = _bcast8(_pack([c[0], w_qkv_conv[0], conf_dw_w[0]]), "bcast_inputs")
    e0 = e0.reshape(8, -1)
    per_dev = [_unpack(e0[dd], e0_shapes) for dd in range(8)]
    all_c = jnp.stack([p[0] for p in per_dev])
    w_conv = jnp.concatenate([per_dev[2 * j][1] for j in range(4)], axis=1)
    w_dw = jnp.concatenate([per_dev[2 * j][2] for j in range(4)], axis=1)

    s16 = jnp.concatenate([all_c, c_ctx[None, :], jnp.zeros((7, d), F32)], axis=0)
    b_mod_loc = lax.dynamic_slice(b_mod, (0, chip * ml), (1, ml))
    to_mxu = lambda t: t.astype(MXU_DTYPE)
    modblk = _matmul(s16, w_mod[0], form="nn", m=16, n=ml, k=d, tm=16, tn=_pick(ml, 1024), tk=_pick(d, 1024),
                     out_dtype=F32, name="mod_fwd", bias=b_mod_loc, a_fn=lambda t: to_mxu(_silu(t)), b_fn=to_mxu)
    allmod = _bcast8(modblk, "bcast_mod").reshape(8, 16, ml)[0::2]
    mod_mine = lax.dynamic_index_in_dim(allmod, dev, axis=1, keepdims=False).reshape(1, 4 * ml)
    mod_ctx = allmod[:, 8, :].reshape(1, 4 * ml)
    sh_a, sc_a, g_a, sh_m, sc_m, g_m = [mod_mine[:, i * d:(i + 1) * d] for i in range(6)]
    csh_a, csc_a = mod_ctx[:, 0:d], mod_ctx[:, d:2 * d]

    chip_idx = jnp.reshape(chip, (1,)).astype(jnp.int32)
    win_t_loc = jnp.transpose(w_in[0]).astype(MXU_DTYPE)
    slot_in = _into_slot(win_t_loc, chip_idx, all_c, "weights_into_slot_0")
    gi_send, gi_recv, gi_bufs, gi_token = _gather_start([slot_in], allmod, "gather_w_in_start")
    slots_rest = [_into_slot(w, chip_idx, gi_token, f"weights_into_slot_{i + 1}") for i, w in enumerate([w_out[0], w_mlp1[0], w_mlp2[0]])]
    sh_a, csh_a = sh_a + gi_token[0:1, 0:1], csh_a + gi_token[0:1, 0:1]

    tmr = _pick(n_lat, 128, 8)
    tmc = _pick(n_ctx, 128, 8)
    ln_in = [row(ln_in_g), row(ln_in_b)]

    def stage_a(rv, pv):
        x0 = _ln(rv[0], pv[0], pv[1])
        return [x0 * (1.0 + pv[3]) + pv[2]], []

    xm_lat, = _rowwise(stage_a, [(x2, d, 0, 0)], ln_in + [sh_a, sc_a], [(d, MXU_DTYPE)], [], n_rows=n_lat, tm=tmr, name="ln_mod_lat")
    xm_ctx, = _rowwise(stage_a, [(ctx2, d, 0, 0)], ln_in + [csh_a, csc_a], [(d, MXU_DTYPE)], [], n_rows=n_ctx, tm=tmc, name="ln_mod_ctx")
    xm = jnp.concatenate([xm_lat, xm_ctx], axis=0)

    gw_in, = _gather_forward(_gather_wait(gi_send, gi_recv, gi_bufs, [xm] + slots_rest, "gather_w_in_wait"), "gather_w_in_forward")
    gs_send, gs_recv, gs_bufs, gs_token = _gather_start(slots_rest, gw_in, "gather_rest_start")
    win_t = gw_in.reshape(in_cols, d)
    win_t = jnp.concatenate([win_t[:4 * dn], win_t[4 * dn + n_gate:], win_t[4 * dn:4 * dn + n_gate],
                             jnp.zeros((gb - n_gate, d), MXU_DTYPE)], axis=0)

    tm_nt = _pick(nt, 1280, 16)
    tn_in = _pick(in_pad, 1280)
    tk_d = _pick(d, 4096)
    tn_in_s = _pick(in_pad, 640)
    h = _matmul(xm, win_t, form="nt", m=nt, n=in_pad, k=d, tm=tm_nt, tn=tn_in_s, tk=tk_d, out_dtype=F32, name="in_proj",
                bias=jnp.zeros((1, in_pad), F32) + gs_token[0:1, 0:1])

    act = _conv7_fwd(h, w_conv, n_lat=n_lat, n_ctx=n_ctx, nh=nh)
    gates_hm = jnp.transpose(h[:, g0:g0 + n_gate].reshape(nt, 4, nh), (2, 0, 1))
    bc = lambda a: jnp.broadcast_to(a.reshape(nh, 1, 1), (nh, 8, LANES))
    dk = dict(n_lat=n_lat, n_ctx=n_ctx, nh=nh, hb=hb)
    o_f, st_f, mi_f = _delta_fwd(act, gates_hm, bc(a_log_f), bc(dt_bias_f), rev=False, **dk)
    o_b, st_b, mi_b = _delta_fwd(act, gates_hm, bc(a_log_b), bc(dt_bias_b), rev=True, **dk)

    h3 = h.reshape(nt // gw, gw, in_pad)
    ck = dict(gw=gw, col0=4 * dn, half=half)
    yh = _conf_conv_fwd(h3, w_dw, vertical=False, **ck).reshape(n_lat, half)
    yv = _conf_conv_fwd(h3, w_dw, vertical=True, **ck).reshape(n_lat, half)

    def mix_fn(o_heads, z_heads, ya, yb, dng, ba, bb, ga, gbb, la, lb):
        outs = []
        for o, z in zip(o_heads, z_heads):
            ms = jnp.mean(o * o, axis=-1, keepdims=True)
            outs.append(o * lax.rsqrt(ms + 1e-6) * dng * _silu(z))
        ya, yb = ya + ba, yb + bb
        mu = (jnp.sum(ya, axis=-1, keepdims=True) + jnp.sum(yb, axis=-1, keepdims=True)) / conf
        ya, yb = ya - mu, yb - mu
        var = (jnp.sum(ya * ya, axis=-1, keepdims=True) + jnp.sum(yb * yb, axis=-1, keepdims=True)) / conf
        rs = lax.rsqrt(var + LN_EPS)
        ca, cb = ya * rs * ga + la, yb * rs * gbb + lb
        return outs, _silu(ca), _silu(cb)

    def heads(t):
        return [t[:, i * HD:(i + 1) * HD] for i in range(nh)]

    def halves(p):
        return p[:, :half], p[:, half:]

    mix_params = [row(dn_norm_g), row(conf_dw_b), row(conf_ln_g), row(conf_ln_b)]

    def mix_args(rv, pv):
        (ba, bb), (ga, gbb), (la, lb) = halves(pv[1]), halves(pv[2]), halves(pv[3])
        return (heads(rv[0] + rv[1]), heads(rv[2]), rv[3], rv[4], pv[0], ba, bb, ga, gbb, la, lb)

    def stage_e(rv, pv):
        outs, ca, cb = mix_fn(*mix_args(rv, pv))
        return [jnp.concatenate(outs + [ca, cb], axis=-1)], []

    mix_rows = [(o_f, dn, 0, 0), (o_b, dn, 0, 0), (h, dn, 3, 0), (yh, half, 0, 0), (yv, half, 0, 0)]
    mix, = _rowwise(stage_e, mix_rows, mix_params, [(d, MXU_DTYPE)], [], n_rows=n_lat, tm=tmr, name="mixer_fwd")

    gs_bufs = _gather_wait(gs_send, gs_recv, gs_bufs, [mix], "gather_rest_wait")
    gw_out, = _gather_forward(gs_bufs[:1], "gather_w_out_forward")
    gf_send, gf_recv, gf_bufs, gf_token = _forward_start(gs_bufs[1:], gw_out, "gather_mlp_forward_start")
    wout_full = gw_out.reshape(d, d)

    tm_l = _pick(n_lat, 1024, 16)
    tn_d = _pick(d, 1024)
    y1 = _matmul(mix, wout_full, form="nn", m=n_lat, n=d, k=d, tm=tm_l, tn=tn_d, tk=tk_d, out_dtype=F32, name="out_proj",
                 bias=jnp.zeros((1, d), F32) + gf_token[0:1, 0:1])

    def res1(x0, y1v, ga, l1g, l1b, shm, scm):
        x1 = _ln(ALPHA * x0 + ga * y1v, l1g, l1b)
        return x1, x1 * (1.0 + scm) + shm

    f_params = ln_in + [g_a, row(ln1_g), row(ln1_b), sh_m, sc_m]

    def stage_f(rv, pv):
        x1, u = res1(_ln(rv[0], pv[0], pv[1]), rv[1], *pv[2:])
        return [x1, u], []

    x1, u = _rowwise(stage_f, [(x2, d, 0, 0), (y1, d, 0, 0)], f_params, [(d, F32), (d, MXU_DTYPE)], [], n_rows=n_lat, tm=tmr, name="res1_fwd")

    gw_1, gw_2 = _forward_wait(gf_send, gf_recv, gf_bufs, [u], "gather_mlp_forward_wait")
    w2_full = gw_2.reshape(ff, d)
    w1_sh = gw_1
    tn_f = _pick(fl, 1024)
    relu_h = _matmul(u, w1_sh, form="nn", m=n_lat, n=ff, k=d, tm=tm_l, tn=tn_f, tk=tk_d, out_dtype=MXU_DTYPE, name="mlp1",
                     bias=row(b_mlp1), epi=lambda r: jnp.maximum(r, 0.0), b_split=fl)
    sq = lambda t: (t.astype(F32) * t.astype(F32)).astype(MXU_DTYPE)
    tk_f = _pick(ff, 4096)
    y2 = _matmul(relu_h, w2_full, form="nn", m=n_lat, n=d, k=ff, tm=tm_l, tn=tn_d, tk=tk_f, out_dtype=F32, name="mlp2",
                 bias=row(b_mlp2), a_fn=sq)

    def loss_fn(x1v, y2v, gm, l2g, l2b, tgt):
        x2v = _ln(ALPHA * x1v + gm * y2v, l2g, l2b)
        return (0.5 / d) * jnp.sum(jnp.square(x2v - tgt))

    def stage_g(rv, pv):
        loss, grads = jax.value_and_grad(loss_fn, argnums=(0, 1, 2, 3, 4))(rv[0], rv[1], pv[0], pv[1], pv[2], rv[2])
        dx1, dy2r, dgm, dl2g, dl2b = grads
        dy2 = dy2r
        return [dx1, dy2], [jnp.reshape(loss, (1, 1)), dgm, dl2g, dl2b, jnp.sum(dy2, axis=0, keepdims=True)]

    dx1p, dy2, loss_acc, d_g_m, d_ln2_g, d_ln2_b, d_b_mlp2 = _rowwise(
        stage_g, [(x1, d, 0, 0), (y2, d, 0, 0), (tgt2, d, 0, 0)], [g_m, row(ln2_g), row(ln2_b)],
        [(d, F32), (d, MXU_DTYPE)], [LANES, d, d, d, d], n_rows=n_lat, tm=tmr, name="loss_res2_bwd")

    dhid, d_b_mlp1 = _matmul(dy2, w2_full, form="nt", m=n_lat, n=ff, k=d, tm=tm_l, tn=tn_f, tk=tk_d, out_dtype=MXU_DTYPE, name="mlp2_bwd_x",
                             epi=lambda r, rh: r * (2.0 * rh.astype(F32)), epi_in=relu_h, colsum=True)
    tk_l = _pick(n_lat, 4096, 16)
    d_w2 = _matmul(relu_h, dy2, form="tn", m=ff, n=d, k=n_lat, tm=tn_f, tn=tn_d, tk=tk_l, out_dtype=F32, name="mlp2_bwd_w", a_fn=sq)
    du = _matmul(dhid, w1_sh, form="nt", m=n_lat, n=d, k=ff, tm=tm_l, tn=tn_d, tk=_pick(fl, 4096), out_dtype=F32, name="mlp1_bwd_x", b_split=fl)
    d_w1 = _matmul(u, dhid, form="tn", m=d, n=ff, k=n_lat, tm=tn_d, tn=tn_f, tk=tk_l, out_dtype=F32, name="mlp1_bwd_w", o_split=fl)

    as_idx = lambda v: jnp.reshape(v, (1,)).astype(jnp.int32)
    slot_idx = [chip_idx, as_idx(2 * (1 - mx) + my), as_idx(2 * mx + (1 - my)), as_idx(2 * (1 - mx) + (1 - my)), c_idx]
    mp_send, mp_recv, mlp_grads, mlp_half, mp_token = _pair_send_start([d_w1, d_w2.reshape(4, fl, d)], loss_acc, "mlp_grads_pair_start")

    def stage_h(rv, pv):
        x0 = _ln(rv[0], pv[0], pv[1])
        _, vjp = jax.vjp(res1, x0, rv[1], *pv[2:])
        dx0, dy1r, dga, dl1g, dl1b, dshm, dscm = vjp((rv[3], rv[2]))
        return [dx0, dy1r], [dga, dl1g, dl1b, dshm, dscm]

    h_params = ln_in + [g_a + mp_token[0:1, 0:1]] + f_params[3:]
    dx0p, dy1, d_g_a, d_ln1_g, d_ln1_b, d_sh_m, d_sc_m = _rowwise(
        stage_h, [(x2, d, 0, 0), (y1, d, 0, 0), (du, d, 0, 0), (dx1p, d, 0, 0)], h_params,
        [(d, F32), (d, MXU_DTYPE)], [d, d, d, d, d], n_rows=n_lat, tm=tmr, name="res1_bwd")

    dmix = _matmul(dy1, wout_full, form="nt", m=n_lat, n=d, k=d, tm=tm_l, tn=tn_d, tk=tk_d, out_dtype=F32, name="out_proj_bwd_x")
    d_wout = _matmul(mix, dy1, form="tn", m=d, n=d, k=n_lat, tm=tn_d, tn=tn_d, tk=tk_l, out_dtype=F32, name="out_proj_bwd_w")

    def stage_i(rv, pv):
        args = mix_args(rv, pv)
        _, vjp = jax.vjp(mix_fn, *args)
        dm = rv[5]
        d_outs = [dm[:, i * HD:(i + 1) * HD] for i in range(nh)]
        dca, dcb = dm[:, dn:dn + half], dm[:, dn + half:]
        do_h, dz_h, dya, dyb, ddng, dba, dbb, dga, dgbb, dla, dlb = vjp((d_outs, dca, dcb))
        cat = lambda *p: jnp.concatenate(p, axis=-1)
        return ([cat(*do_h), cat(*dz_h), dya, dyb], [ddng, cat(dba, dbb), cat(dga, dgbb), cat(dla, dlb)])

    do, dz, dyh, dyv, d_dn_norm_g, d_conf_dw_b, d_conf_ln_g, d_conf_ln_b = _rowwise(
        stage_i, mix_rows + [(dmix, d, 0, 0)], mix_params, [(dn, F32), (dn, MXU_DTYPE), (half, F32), (half, F32)],
        [HD, conf, conf, conf], n_rows=n_lat, tm=tmr, name="mixer_bwd")

    mlp_grads, mlp_half = _pair_send_wait(mp_send, mp_recv, mlp_grads, mlp_half, do, "mlp_grads_pair_wait")
    mlp_sums = [_add_my_half(g, r, c_idx, f"mlp_grads_pair_add_{i}") for i, (g, r) in enumerate(zip(mlp_grads, mlp_half))]
    ms_send, ms_recv, mlp_sums, mlp_lands, ms_token = _scatter_start(mlp_sums, loss_acc, "mlp_grads_scatter_start")
    tie = ms_token[0:1, 0:1].reshape(1, 1, 1)

    dval_h, dgate_h, d_wdw_h = _conf_conv_bwd(h3, w_dw, dyh.reshape(gw, gw, half), vertical=False, **ck)
    dval_v, dgate_v, d_wdw_v = _conf_conv_bwd(h3, w_dw, dyv.reshape(gw, gw, half), vertical=True, **ck)
    dact_f, dgt_f, d_alog_f, d_dt_f = _delta_bwd(act, gates_hm, bc(a_log_f) + tie, bc(dt_bias_f), st_f, mi_f, do, rev=False, **dk)
    dact_b, dgt_b, d_alog_b, d_dt_b = _delta_bwd(act, gates_hm, bc(a_log_b) + tie, bc(dt_bias_b), st_b, mi_b, do, rev=True, **dk)
    dh_qkv, d_wconv = _conv7_bwd(h, w_conv, dact_f, dact_b, n_lat=n_lat, n_ctx=n_ctx, nh=nh)

    dgates = jnp.stack([dgt_f[..., 0], dgt_f[..., 1], dgt_b[..., 0], dgt_b[..., 1]], axis=0)
    dgates = jnp.transpose(dgates, (2, 0, 1)).reshape(nt, n_gate)
    dgates = jnp.pad(dgates, ((0, 0), (0, gb - n_gate))).astype(MXU_DTYPE)
    zrows = lambda t: jnp.pad(t, ((0, n_ctx), (0, 0)))
    dh = jnp.concatenate([dh_qkv, zrows(dz), zrows(dval_h.reshape(n_lat, half)), zrows(dval_v.reshape(n_lat, half)),
                          zrows(dgate_h.reshape(n_lat, half)), zrows(dgate_v.reshape(n_lat, half)), dgates], axis=1)

    tk_in = _pick(in_pad, 2560)
    d_win_t = _matmul(dh, xm, form="tn", m=in_pad, n=d, k=nt, tm=tn_in_s, tn=tn_d, tk=nt, out_dtype=F32, name="in_proj_bwd_w")
    d_win_t = jnp.concatenate([d_win_t[:4 * dn], d_win_t[g0:g0 + n_gate], d_win_t[4 * dn:g0]], axis=0)

    proj_grads = [d_win_t.reshape(4, r_in, d), d_wout.reshape(4, d // 4, d)]
    pp_send, pp_recv, proj_grads, proj_half, pp_token = _pair_send_start(proj_grads, loss_acc, "proj_grads_pair_start")
    dxm = _matmul(dh, win_t, form="nn", m=nt, n=d, k=in_pad, tm=tm_nt, tn=tn_d, tk=tk_in, out_dtype=F32, name="in_proj_bwd_x",
                  bias=jnp.zeros((1, d), F32) + pp_token[0:1, 0:1])
    proj_grads, proj_half = _pair_send_wait(pp_send, pp_recv, proj_grads, proj_half, dxm, "proj_grads_pair_wait")
    proj_sums = [_add_my_half(g, r, c_idx, f"proj_grads_pair_add_{i}") for i, (g, r) in enumerate(zip(proj_grads, proj_half))]
    ps_send, ps_recv, proj_sums, proj_lands, ps_token = _scatter_start(proj_sums, loss_acc, "proj_grads_scatter_start")
    mlp_sums, mlp_lands = _scatter_wait(ms_send, ms_recv, mlp_sums, mlp_lands, [ps_token], "mlp_grads_scatter_wait")
    mlp_mine = [_sum_slots(own, rcv, slot_idx, f"mlp_grads_chip_sum_{i}") for i, (own, rcv) in enumerate(zip(mlp_sums, mlp_lands))]
    g_w_mlp1, g_w_mlp2 = _pair_join_halves(mlp_mine, "mlp_grads_pair_join")
    mlp_adam = {"w_mlp1": _adamw(w_mlp1, g_w_mlp1[None], m_w_mlp1, v_w_mlp1, "adamw_w_mlp1"),
                "w_mlp2": _adamw(w_mlp2, g_w_mlp2[None], m_w_mlp2, v_w_mlp2, "adamw_w_mlp2")}
    mlp_done = [mlp_adam["w_mlp1"][2], mlp_adam["w_mlp2"][2]]
    sc_a_tied = sc_a + ps_token[0:1, 0:1]

    def mod_in(xr, lg, lb, sh, sc):
        x0 = _ln(xr, lg, lb)
        return x0, x0 * (1.0 + sc) + sh

    def stage_j(rv, pv):
        _, vjp = jax.vjp(mod_in, rv[0], *pv)
        dx0 = rv[2] if len(rv) > 2 else jnp.zeros_like(rv[1])
        dxr, dlg, dlb, dsh, dsc = vjp((dx0, rv[1]))
        return [dxr], [dlg, dlb, dsh, dsc]

    grad_x, dlg_l, dlb_l, d_sh_a, d_sc_a = _rowwise(
        stage_j, [(x2, d, 0, 0), (dxm, d, 0, 0), (dx0p, d, 0, 0)], ln_in + [sh_a, sc_a_tied], [(d, F32)], [d, d, d, d],
        n_rows=n_lat, tm=tmr, name="ln_mod_bwd_lat", after=mlp_done)
    _, dlg_c, dlb_c, d_csh_a, d_csc_a = _rowwise(
        stage_j, [(ctx2, d, 0, 0), (dxm, d, 0, n_lat // tmc)], ln_in + [csh_a, csc_a], [(d, F32)], [d, d, d, d],
        n_rows=n_ctx, tm=tmc, name="ln_mod_bwd_ctx", after=mlp_done)

    zd = jnp.zeros((1, d), F32)
    dmod_rows = jnp.concatenate([jnp.concatenate([d_sh_a, d_sc_a, d_g_a, d_sh_m, d_sc_m, d_g_m], axis=1),
                                 jnp.concatenate([d_csh_a, d_csc_a, zd, zd, zd, zd], axis=1), jnp.zeros((6, 6 * d), F32)], axis=0)
    d16, g_b_mod = _dmod_prep(_bcast8(dmod_rows, "bcast_dmod"), "dmod_prep")
    d16_loc = lax.dynamic_slice(d16, (0, chip * ml), (16, ml))
    g_w_mod = _matmul(s16, d16_loc, form="tn", m=d, n=ml, k=16, tm=_pick(d, 512), tn=_pick(ml, 1024), tk=16, out_dtype=F32,
                      name="mod_bwd_w", a_fn=_silu, kind="hi")
    dsilu = _matmul(d16_loc, w_mod[0], form="nt", m=16, n=d, k=ml, tm=16, tn=_pick(d, 1024), tk=_pick(ml, 1024), out_dtype=F32,
                    name="mod_bwd_c", a_fn=to_mxu, b_fn=to_mxu)
    d_cctx_part = dsilu[8:9, :] * (1 - mc).astype(F32)

    head_sum = lambda t: t[:, 0, 0]
    small = [d_cctx_part, dlg_l + dlg_c, dlb_l + dlb_c, d_wconv, head_sum(d_alog_f), head_sum(d_dt_f), head_sum(d_alog_b),
             head_sum(d_dt_b), d_dn_norm_g, jnp.concatenate([d_wdw_h, d_wdw_v], axis=1), d_conf_dw_b, d_conf_ln_g, d_conf_ln_b,
             d_ln1_g, d_ln1_b, d_b_mlp1, d_b_mlp2, d_ln2_g, d_ln2_b]
    small_shapes = [(d,), (d,), (d,), (SHORT_CONV, 3 * dn), (nh,), (nh,), (nh,), (nh,), (HD,), (CONF_K, conf), (conf,), (conf,), (conf,),
                    (d,), (d,), (ff,), (d,), (d,), (d,)]
    ssum = _sum8(_bcast8(_pack(small), "bcast_small_grads"), "sum_small_grads")
    (t_cctx, g_ln_in_g, g_ln_in_b, g_wconv_full, g_a_log_f, g_dt_bias_f, g_a_log_b, g_dt_bias_b, g_dn_norm_g, g_wdw_full, g_conf_dw_b,
     g_conf_ln_g, g_conf_ln_b, g_ln1_g, g_ln1_b, g_b_mlp1, g_b_mlp2, g_ln2_g, g_ln2_b) = _unpack(ssum, small_shapes)
    sg = jax.nn.sigmoid(c_ctx)
    g_c_ctx = t_cctx * (sg * (1.0 + c_ctx * (1.0 - sg)))
    g_w_qkv_conv = lax.dynamic_slice(g_wconv_full, (0, chip * w_qkv_conv.shape[2]), w_qkv_conv.shape[1:])
    g_conf_dw_w = lax.dynamic_slice(g_wdw_full, (0, chip * conf_dw_w.shape[2]), conf_dw_w.shape[1:])

    grads = dict(c_ctx=g_c_ctx, ln_in_g=g_ln_in_g, ln_in_b=g_ln_in_b, w_mod=g_w_mod[None], b_mod=g_b_mod, w_in=None,
                 w_qkv_conv=g_w_qkv_conv[None], a_log_f=g_a_log_f[None], dt_bias_f=g_dt_bias_f[None], a_log_b=g_a_log_b[None],
                 dt_bias_b=g_dt_bias_b[None], dn_norm_g=g_dn_norm_g[None], conf_dw_w=g_conf_dw_w[None], conf_dw_b=g_conf_dw_b[None],
                 conf_ln_g=g_conf_ln_g[None], conf_ln_b=g_conf_ln_b[None], w_out=None, ln1_g=g_ln1_g[None], ln1_b=g_ln1_b[None],
                 w_mlp1=g_w_mlp1[None], b_mlp1=g_b_mlp1[None], w_mlp2=g_w_mlp2[None], b_mlp2=g_b_mlp2[None], ln2_g=g_ln2_g[None],
                 ln2_b=g_ln2_b[None])
    weights = dict(c_ctx=c_ctx, ln_in_g=ln_in_g, ln_in_b=ln_in_b, w_mod=w_mod, b_mod=b_mod, w_in=w_in, w_qkv_conv=w_qkv_conv,
                   a_log_f=a_log_f, dt_bias_f=dt_bias_f, a_log_b=a_log_b, dt_bias_b=dt_bias_b, dn_norm_g=dn_norm_g, conf_dw_w=conf_dw_w,
                   conf_dw_b=conf_dw_b, conf_ln_g=conf_ln_g, conf_ln_b=conf_ln_b, w_out=w_out, ln1_g=ln1_g, ln1_b=ln1_b, w_mlp1=w_mlp1,
                   b_mlp1=b_mlp1, w_mlp2=w_mlp2, b_mlp2=b_mlp2, ln2_g=ln2_g, ln2_b=ln2_b)
    m_in = dict(c_ctx=m_c_ctx, ln_in_g=m_ln_in_g, ln_in_b=m_ln_in_b, w_mod=m_w_mod, b_mod=m_b_mod, w_in=m_w_in, w_qkv_conv=m_w_qkv_conv,
                a_log_f=m_a_log_f, dt_bias_f=m_dt_bias_f, a_log_b=m_a_log_b, dt_bias_b=m_dt_bias_b, dn_norm_g=m_dn_norm_g,
                conf_dw_w=m_conf_dw_w, conf_dw_b=m_conf_dw_b, conf_ln_g=m_conf_ln_g, conf_ln_b=m_conf_ln_b, w_out=m_w_out, ln1_g=m_ln1_g,
                ln1_b=m_ln1_b, w_mlp1=m_w_mlp1, b_mlp1=m_b_mlp1, w_mlp2=m_w_mlp2, b_mlp2=m_b_mlp2, ln2_g=m_ln2_g, ln2_b=m_ln2_b)
    v_in = dict(c_ctx=v_c_ctx, ln_in_g=v_ln_in_g, ln_in_b=v_ln_in_b, w_mod=v_w_mod, b_mod=v_b_mod, w_in=v_w_in, w_qkv_conv=v_w_qkv_conv,
                a_log_f=v_a_log_f, dt_bias_f=v_dt_bias_f, a_log_b=v_a_log_b, dt_bias_b=v_dt_bias_b, dn_norm_g=v_dn_norm_g,
                conf_dw_w=v_conf_dw_w, conf_dw_b=v_conf_dw_b, conf_ln_g=v_conf_ln_g, conf_ln_b=v_conf_ln_b, w_out=v_w_out, ln1_g=v_ln1_g,
                ln1_b=v_ln1_b, w_mlp1=v_w_mlp1, b_mlp1=v_b_mlp1, w_mlp2=v_w_mlp2, b_mlp2=v_b_mlp2, ln2_g=v_ln2_g, ln2_b=v_ln2_b)
    names = list(weights)
    big_names = ("w_mod", "w_mlp1", "w_mlp2", "w_in", "w_out")
    delta, new_m, new_v = {}, {}, {}

    def big_adamw(nm):
        delta[nm], new_m[nm], new_v[nm] = mlp_adam[nm] if nm in mlp_adam else _adamw(weights[nm], grads[nm], m_in[nm], v_in[nm], f"adamw_{nm}")

    for nm in big_names[:3]:
        big_adamw(nm)
    small_names = [nm for nm in names if nm not in big_names]
    shapes = [weights[nm].shape for nm in small_names]
    for nm in small_names:
        grads[nm] = grads[nm].reshape(weights[nm].shape)
    packed = [_pack([src[nm] for nm in small_names]) for src in (weights, grads, m_in, v_in)]
    dl, mm, vv = _adamw(*packed, "adamw_small")
    for nm, a, b_, c_ in zip(small_names, _unpack(dl, shapes), _unpack(mm, shapes), _unpack(vv, shapes)):
        delta[nm], new_m[nm], new_v[nm] = a, b_, c_

    done = [new_v[nm] for nm in big_names[:3]] + [vv]
    proj_sums, proj_lands = _scatter_wait(ps_send, ps_recv, proj_sums, proj_lands, done, "proj_grads_scatter_wait")
    proj_mine = [_sum_slots(own, rcv, slot_idx, f"proj_grads_chip_sum_{i}") for i, (own, rcv) in enumerate(zip(proj_sums, proj_lands))]
    g_win_t, g_w_out = _pair_join_halves(proj_mine, "proj_grads_pair_join")
    grads["w_in"], grads["w_out"] = jnp.transpose(g_win_t)[None], g_w_out[None]
    for nm in big_names[3:]:
        big_adamw(nm)

    loss = lax.psum(loss_acc[0, 0], MESH_AXES)
    return (loss, grad_x[None], *[grads[nm] for nm in names], *[delta[nm] for nm in names],
            *[new_m[nm] for nm in names], *[new_v[nm] for nm in names])
```

```python
import functools
import math

import jax
import jax.numpy as jnp
from jax import lax
from jax.experimental import pallas as pl
from jax.experimental.pallas import tpu as pltpu

F32 = jnp.float32
BF16 = jnp.bfloat16
MXU_DTYPE = BF16
WIRE_DTYPE = BF16
HIGHEST = lax.Precision.HIGHEST

HD = 128
CHUNK = 128
SHORT_CONV = 7
CONF_K = 31
ALPHA = 2.0 ** 0.25
LN_EPS = 1e-5
V7X_VMEM_BYTES = 64 * 1024 * 1024
LANES = 128

ADAM_LR = 0.001
ADAM_B1 = 0.9
ADAM_B2 = 0.999
ADAM_EPS = 1e-08
ADAM_WD = 0.01
ADAM_STEP = 10

MESH_AXES = ("x", "y", "c")
ANY = pl.BlockSpec(memory_space=pl.ANY)


def _pick(dim, pref, mult=LANES):
    best = None
    t = mult
    while t <= min(dim, pref):
        if dim % t == 0:
            best = t
        t += mult
    return best if best is not None else dim


def _cparams(sem, vmem_est):
    limit = int(min(V7X_VMEM_BYTES - 6 * 1024 * 1024, max(32 * 1024 * 1024, vmem_est + 8 * 1024 * 1024)))
    return pltpu.CompilerParams(dimension_semantics=sem, vmem_limit_bytes=limit)


def _nbytes(shape, dtype):
    return math.prod(shape) * jnp.dtype(dtype).itemsize


def _ln(x, g, b):
    mu = jnp.mean(x, axis=-1, keepdims=True)
    xc = x - mu
    var = jnp.mean(xc * xc, axis=-1, keepdims=True)
    return xc * lax.rsqrt(var + LN_EPS) * g + b


def _silu(x):
    return x * jax.nn.sigmoid(x)


def _softplus(x):
    return jnp.maximum(x, 0.0) + jnp.log1p(jnp.exp(-jnp.abs(x)))


_DIMS = {"nn": (((1,), (0,)), ((), ())), "nt": (((1,), (1,)), ((), ())), "tn": (((0,), (0,)), ((), ()))}


def _split_bf16(x, parts):
    out, rest = [], x.astype(F32)
    for _ in range(parts):
        bits = lax.bitcast_convert_type(rest, jnp.uint32) & jnp.uint32(0xFFFF0000)
        p = lax.bitcast_convert_type(bits, F32)
        out.append(p.astype(BF16))
        rest = rest - p
    return out


def _raw_dot(a, b, form, kind):
    if MXU_DTYPE != F32:
        if kind == "mxu":
            return lax.dot_general(a.astype(MXU_DTYPE), b.astype(MXU_DTYPE), _DIMS[form], preferred_element_type=F32)
        if kind == "hi3":
            (a0, a1), (b0, b1) = _split_bf16(a, 2), _split_bf16(b, 2)
            d = lambda p, q: lax.dot_general(p, q, _DIMS[form], preferred_element_type=F32)
            return d(a0, b0) + (d(a1, b0) + d(a0, b1))
    return lax.dot_general(a.astype(F32), b.astype(F32), _DIMS[form], precision=HIGHEST, preferred_element_type=F32)


@functools.partial(jax.custom_vjp, nondiff_argnums=(2, 3))
def _dot(a, b, form, kind):
    return _raw_dot(a, b, form, kind)


def _dot_fwd(a, b, form, kind):
    return _raw_dot(a, b, form, kind), (a, b)


def _dot_bwd(form, kind, res, dc):
    a, b = res
    if form == "nn":
        return _dot(dc, b, "nt", kind), _dot(a, dc, "tn", kind)
    if form == "nt":
        return _dot(dc, b, "nn", kind), _dot(dc, a, "tn", kind)
    return _dot(b, dc, "nt", kind), _dot(a, dc, "nn", kind)


_dot.defvjp(_dot_fwd, _dot_bwd)


def _raw_xdot(a, b, form, exact):
    if MXU_DTYPE == F32:
        return _raw_dot(a, b, form, "hi")
    d = lambda p, q: lax.dot_general(p, q, _DIMS[form], preferred_element_type=F32)
    if exact == "a":
        a16, (b0, b1, b2) = a.astype(BF16), _split_bf16(b, 3)
        return d(a16, b0) + (d(a16, b1) + d(a16, b2))
    b16, (a0, a1, a2) = b.astype(BF16), _split_bf16(a, 3)
    return d(a0, b16) + (d(a1, b16) + d(a2, b16))


@functools.partial(jax.custom_vjp, nondiff_argnums=(2, 3))
def _xdot(a, b, form, exact):
    return _raw_xdot(a, b, form, exact)


def _xdot_fwd(a, b, form, exact):
    return _raw_xdot(a, b, form, exact), (a, b)


def _xdot_bwd(form, exact, res, dc):
    a, b = res
    if form == "nn" and exact == "a":
        return jnp.zeros_like(a), _xdot(a, dc, "tn", "a")
    assert form == "tn" and exact == "b"
    return _xdot(b, dc, "nt", "a"), jnp.zeros_like(b)


_xdot.defvjp(_xdot_fwd, _xdot_bwd)


def _spec2(block, idx, split=None):
    if split is None:
        return pl.BlockSpec(tuple(block), idx)
    per = split // block[1]

    def idx3(*g):
        r, cblk = idx(*g)
        return (cblk // per, r, cblk % per)

    return pl.BlockSpec((None,) + tuple(block), idx3)


def _matmul(a, b, *, form, m, n, k, tm, tn, tk, out_dtype, name, bias=None, a_fn=None, b_fn=None,
            epi=None, epi_in=None, colsum=False, kind="mxu", b_split=None, o_split=None):
    assert m % tm == 0 and n % tn == 0 and k % tk == 0, (name, m, n, k, tm, tn, tk)
    nk = k // tk
    grid = (n // tn, m // tm, nk)
    if form == "tn":
        a_spec = pl.BlockSpec((tk, tm), lambda j, i, kk: (kk, i))
    else:
        a_spec = pl.BlockSpec((tm, tk), lambda j, i, kk: (i, kk))
    if form == "nt":
        b_spec = _spec2((tn, tk), lambda j, i, kk: (j, kk), b_split)
    else:
        b_spec = _spec2((tk, tn), lambda j, i, kk: (kk, j), b_split)
    in_specs = [a_spec, b_spec]
    operands = [a, b]
    if bias is not None:
        in_specs.append(pl.BlockSpec((1, tn), lambda j, i, kk: (0, j)))
        operands.append(bias)
    if epi_in is not None:
        in_specs.append(pl.BlockSpec((tm, tn), lambda j, i, kk: (i, j)))
        operands.append(epi_in)
    out_specs = [_spec2((tm, tn), lambda j, i, kk: (i, j), o_split)]
    if o_split is None:
        out_shape = [jax.ShapeDtypeStruct((m, n), out_dtype)]
    else:
        out_shape = [jax.ShapeDtypeStruct((n // o_split, m, o_split), out_dtype)]
    if colsum:
        out_specs.append(pl.BlockSpec((1, tn), lambda j, i, kk: (0, j)))
        out_shape.append(jax.ShapeDtypeStruct((1, n), F32))
    has_bias, has_epi_in = bias is not None, epi_in is not None

    def body(*refs):
        refs = list(refs)
        a_ref, b_ref = refs[0], refs[1]
        pos = 2
        bias_ref = epi_ref = cs_ref = None
        if has_bias:
            bias_ref = refs[pos]
            pos += 1
        if has_epi_in:
            epi_ref = refs[pos]
            pos += 1
        o_ref = refs[pos]
        pos += 1
        if colsum:
            cs_ref = refs[pos]
            pos += 1
        acc_ref = refs[pos] if nk > 1 else None
        i, kk = pl.program_id(1), pl.program_id(2)

        av = a_ref[...]
        if a_fn is not None:
            av = a_fn(av)
        bv = b_ref[...]
        if b_fn is not None:
            bv = b_fn(bv)
        prod = _raw_dot(av, bv, form, kind)

        if nk > 1:
            @pl.when(kk == 0)
            def _():
                acc_ref[...] = prod

            @pl.when(jnp.logical_and(kk > 0, kk < nk - 1))
            def _():
                acc_ref[...] += prod

        @pl.when(kk == nk - 1)
        def _():
            r = acc_ref[...] + prod if nk > 1 else prod
            if has_bias:
                r = r + bias_ref[...]
            if epi is not None:
                r = epi(r, epi_ref[...]) if has_epi_in else epi(r)
            o_ref[...] = r.astype(out_dtype)
            if colsum:
                s = jnp.sum(r, axis=0, keepdims=True)

                @pl.when(i == 0)
                def _():
                    cs_ref[...] = s

                @pl.when(i > 0)
                def _():
                    cs_ref[...] += s

    est = 2 * (_nbytes((tm, tk), a.dtype) + _nbytes((tk, tn), b.dtype) + _nbytes((tm, tn), out_dtype))
    est += _nbytes((tm, tn), F32) * 2 + (2 * _nbytes((tm, tn), epi_in.dtype) if has_epi_in else 0)
    res = pl.pallas_call(
        body, grid=grid, in_specs=in_specs, out_specs=out_specs, out_shape=out_shape,
        scratch_shapes=[pltpu.VMEM((tm, tn), F32)] if nk > 1 else [], name=name,
        compiler_params=_cparams(("arbitrary", "arbitrary", "arbitrary"), est),
    )(*operands)
    return res if colsum else res[0]


def _rowwise(fn, rows, params, row_outs, acc_outs, *, n_rows, tm, name, after=()):
    assert n_rows % tm == 0, (name, n_rows, tm)
    nr, npar, nro, nac, naf = len(rows), len(params), len(row_outs), len(acc_outs), len(after)
    in_specs = [pl.BlockSpec((tm, w), functools.partial(lambda i, cb, ro: (i + ro, cb), cb=cb, ro=ro))
                for (_, w, cb, ro) in rows]
    in_specs += [pl.BlockSpec((1, p.shape[1]), lambda i: (0, 0)) for p in params] + [ANY] * naf
    out_specs = [pl.BlockSpec((tm, w), lambda i: (i, 0)) for (w, _) in row_outs]
    out_specs += [pl.BlockSpec((1, w), lambda i: (0, 0)) for w in acc_outs]
    out_shape = [jax.ShapeDtypeStruct((n_rows, w), dt) for (w, dt) in row_outs]
    out_shape += [jax.ShapeDtypeStruct((1, w), F32) for w in acc_outs]

    def body(*refs):
        rv = [r[...] for r in refs[:nr]]
        pv = [r[...] for r in refs[nr:nr + npar]]
        ro_refs = refs[nr + npar + naf:nr + npar + naf + nro]
        ac_refs = refs[nr + npar + naf + nro:]
        ro, ac = fn(rv, pv)
        for ref, val in zip(ro_refs, ro, strict=True):
            ref[...] = val.astype(ref.dtype)
        first = pl.program_id(0) == 0
        for ref, val in zip(ac_refs, ac, strict=True):
            val = jnp.broadcast_to(val.astype(F32), ref.shape)

            @pl.when(first)
            def _(ref=ref, val=val):
                ref[...] = val

            @pl.when(jnp.logical_not(first))
            def _(ref=ref, val=val):
                ref[...] += val

    io = sum(_nbytes((tm, w), a.dtype) for (a, w, _, _) in rows) + sum(_nbytes((tm, w), dt) for (w, dt) in row_outs)
    widest = max([w for (_, w, _, _) in rows] + [w for (w, _) in row_outs])
    est = 2 * io + 12 * _nbytes((tm, widest), F32)
    return pl.pallas_call(
        body, grid=(n_rows // tm,), in_specs=in_specs, out_specs=out_specs, out_shape=out_shape, name=name,
        compiler_params=_cparams(("arbitrary",), est),
    )(*[a for (a, _, _, _) in rows], *params, *after)


def _conv7_tiles(n_lat, n_ctx):
    tt = min(256, n_ctx)
    assert n_lat % tt == 0 and n_ctx % tt == 0
    return tt, [(t0, 8 + t0) for t0 in range(0, n_lat, tt)] + [(n_lat + t0, 16 + n_lat + t0) for t0 in range(0, n_ctx, tt)]


def _conv7_fill(p_ref, x_ref, n_lat, n_ctx):
    z8 = jnp.zeros((8, LANES), F32)
    p_ref[0:8, :] = z8
    p_ref[8:8 + n_lat, :] = x_ref[0:n_lat, :]
    p_ref[8 + n_lat:16 + n_lat, :] = z8
    p_ref[16 + n_lat:16 + n_lat + n_ctx, :] = x_ref[n_lat:n_lat + n_ctx, :]
    p_ref[16 + n_lat + n_ctx:24 + n_lat + n_ctx, :] = z8


def _conv7_fwd(h, w_conv, *, n_lat, n_ctx, nh):
    nt = n_lat + n_ctx
    tt, tiles = _conv7_tiles(n_lat, n_ctx)
    half = SHORT_CONV // 2

    def body(x_ref, w_ref, o_ref, p_ref):
        _conv7_fill(p_ref, x_ref, n_lat, n_ctx)
        for (row, prow) in tiles:
            acc = jnp.zeros((tt, LANES), F32)
            for kk in range(SHORT_CONV):
                acc = acc + w_ref[kk:kk + 1, :] * p_ref[pl.ds(prow + kk - half, tt), :]
            o_ref[pl.ds(row, tt), :] = _silu(acc)

    return pl.pallas_call(
        body, grid=(3 * nh,),
        in_specs=[pl.BlockSpec((nt, LANES), lambda j: (0, j)), pl.BlockSpec((SHORT_CONV, LANES), lambda j: (0, j))],
        out_specs=pl.BlockSpec((None, None, nt, LANES), lambda j: (j // nh, j % nh, 0, 0)),
        out_shape=jax.ShapeDtypeStruct((3, nh, nt, LANES), F32),
        scratch_shapes=[pltpu.VMEM((nt + 24, LANES), F32)], name="conv7_fwd",
        compiler_params=_cparams(("arbitrary",), 5 * _nbytes((nt + 24, LANES), F32)),
    )(h, w_conv)


def _conv7_bwd(h, w_conv, dact_f, dact_b, *, n_lat, n_ctx, nh):
    nt = n_lat + n_ctx
    tt, tiles = _conv7_tiles(n_lat, n_ctx)
    half = SHORT_CONV // 2

    def body(x_ref, w_ref, df_ref, db_ref, dx_ref, dw_ref, p_ref, q_ref):
        _conv7_fill(p_ref, x_ref, n_lat, n_ctx)
        z8 = jnp.zeros((8, LANES), F32)
        q_ref[0:8, :] = z8
        q_ref[8 + n_lat:16 + n_lat, :] = z8
        q_ref[16 + n_lat + n_ctx:24 + n_lat + n_ctx, :] = z8
        dw = [jnp.zeros((1, LANES), F32) for _ in range(SHORT_CONV)]
        for (row, prow) in tiles:
            pre = jnp.zeros((tt, LANES), F32)
            for kk in range(SHORT_CONV):
                pre = pre + w_ref[kk:kk + 1, :] * p_ref[pl.ds(prow + kk - half, tt), :]
            s = jax.nn.sigmoid(pre)
            dpre = (df_ref[pl.ds(row, tt), :] + db_ref[pl.ds(row, tt), :]) * (s * (1.0 + pre * (1.0 - s)))
            q_ref[pl.ds(prow, tt), :] = dpre
            for kk in range(SHORT_CONV):
                dw[kk] = dw[kk] + jnp.sum(dpre * p_ref[pl.ds(prow + kk - half, tt), :], axis=0, keepdims=True)
        for (row, prow) in tiles:
            acc = jnp.zeros((tt, LANES), F32)
            for kk in range(SHORT_CONV):
                acc = acc + w_ref[kk:kk + 1, :] * q_ref[pl.ds(prow + half - kk, tt), :]
            dx_ref[pl.ds(row, tt), :] = acc.astype(dx_ref.dtype)
        for kk in range(SHORT_CONV):
            dw_ref[kk:kk + 1, :] = dw[kk]

    dspec = pl.BlockSpec((None, None, nt, LANES), lambda j: (j // nh, j % nh, 0, 0))
    return pl.pallas_call(
        body, grid=(3 * nh,),
        in_specs=[pl.BlockSpec((nt, LANES), lambda j: (0, j)), pl.BlockSpec((SHORT_CONV, LANES), lambda j: (0, j)), dspec, dspec],
        out_specs=[pl.BlockSpec((nt, LANES), lambda j: (0, j)), pl.BlockSpec((SHORT_CONV, LANES), lambda j: (0, j))],
        out_shape=[jax.ShapeDtypeStruct((nt, 3 * nh * LANES), MXU_DTYPE), jax.ShapeDtypeStruct((SHORT_CONV, 3 * nh * LANES), F32)],
        scratch_shapes=[pltpu.VMEM((nt + 24, LANES), F32), pltpu.VMEM((nt + 24, LANES), F32)], name="conv7_bwd",
        compiler_params=_cparams(("arbitrary",), 10 * _nbytes((nt + 24, LANES), F32)),
    )(h, w_conv, dact_f, dact_b)


_RT = 4
_CP = CONF_K // 2


def _conf_pad_shape(gw, vertical):
    return (gw + 2 * _CP, gw, LANES) if vertical else (gw, gw + 32, LANES)


def _conf_fill(p_ref, val, gw, vertical):
    if vertical:
        p_ref[0:_CP] = jnp.zeros((_CP, gw, LANES), F32)
        p_ref[_CP + gw:2 * _CP + gw] = jnp.zeros((_CP, gw, LANES), F32)
        p_ref[_CP:_CP + gw] = val
    else:
        p_ref[:, 0:16, :] = jnp.zeros((gw, 16, LANES), F32)
        p_ref[:, 16 + gw:32 + gw, :] = jnp.zeros((gw, 16, LANES), F32)
        p_ref[:, 16:16 + gw, :] = val


def _conf_window(p_ref, r0, shift, gw, vertical):
    if vertical:
        return p_ref[pl.ds(r0 + _CP + shift, _RT), :, :]
    return p_ref[pl.ds(r0, _RT), pl.ds(16 + shift, gw), :]


def _conf_conv_fwd(h3, w_dw, *, gw, col0, half, vertical):
    cb_val = (col0 + (half if vertical else 0)) // LANES
    cb_gate = cb_val + 2 * half // LANES
    cb_w = (half if vertical else 0) // LANES

    def body(v_ref, g_ref, w_ref, o_ref, p_ref):
        _conf_fill(p_ref, v_ref[...] * jax.nn.sigmoid(g_ref[...]), gw, vertical)

        def step(t, carry):
            r0 = t * _RT
            acc = jnp.zeros((_RT, gw, LANES), F32)
            for kk in range(CONF_K):
                acc = acc + w_ref[kk:kk + 1, :] * _conf_window(p_ref, r0, kk - _CP, gw, vertical)
            o_ref[pl.ds(r0, _RT)] = acc
            return carry

        lax.fori_loop(0, gw // _RT, step, 0)

    blk = (gw, gw, LANES)
    return pl.pallas_call(
        body, grid=(half // LANES,),
        in_specs=[pl.BlockSpec(blk, lambda j: (0, 0, cb_val + j)), pl.BlockSpec(blk, lambda j: (0, 0, cb_gate + j)),
                  pl.BlockSpec((CONF_K, LANES), lambda j: (0, cb_w + j))],
        out_specs=pl.BlockSpec(blk, lambda j: (0, 0, j)),
        out_shape=jax.ShapeDtypeStruct((gw, gw, half), F32),
        scratch_shapes=[pltpu.VMEM(_conf_pad_shape(gw, vertical), F32)],
        name="conf_conv_fwd_v" if vertical else "conf_conv_fwd_h",
        compiler_params=_cparams(("arbitrary",), 8 * _nbytes(_conf_pad_shape(gw, vertical), F32)),
    )(h3, h3, w_dw)


def _conf_conv_bwd(h3, w_dw, dyc, *, gw, col0, half, vertical):
    cb_val = (col0 + (half if vertical else 0)) // LANES
    cb_gate = cb_val + 2 * half // LANES
    cb_w = (half if vertical else 0) // LANES

    def body(v_ref, g_ref, w_ref, d_ref, dv_ref, dg_ref, dw_ref, py_ref, pd_ref):
        _conf_fill(py_ref, v_ref[...] * jax.nn.sigmoid(g_ref[...]), gw, vertical)
        _conf_fill(pd_ref, d_ref[...], gw, vertical)

        def step(t, carry):
            r0 = t * _RT
            acc = jnp.zeros((_RT, gw, LANES), F32)
            for kk in range(CONF_K):
                acc = acc + w_ref[kk:kk + 1, :] * _conf_window(pd_ref, r0, _CP - kk, gw, vertical)
            val = v_ref[pl.ds(r0, _RT)]
            sg = jax.nn.sigmoid(g_ref[pl.ds(r0, _RT)])
            dv_ref[pl.ds(r0, _RT)] = (acc * sg).astype(dv_ref.dtype)
            dg_ref[pl.ds(r0, _RT)] = (acc * val * sg * (1.0 - sg)).astype(dg_ref.dtype)
            return carry

        lax.fori_loop(0, gw // _RT, step, 0)

        for kk in range(CONF_K):
            def wstep(t, acc, kk=kk):
                r0 = t * _RT
                prod = d_ref[pl.ds(r0, _RT)] * _conf_window(py_ref, r0, kk - _CP, gw, vertical)
                return acc + jnp.sum(prod, axis=0)

            tot = lax.fori_loop(0, gw // _RT, wstep, jnp.zeros((gw, LANES), F32))
            dw_ref[kk:kk + 1, :] = jnp.sum(tot, axis=0, keepdims=True)

    blk = (gw, gw, LANES)
    pshape = _conf_pad_shape(gw, vertical)
    return pl.pallas_call(
        body, grid=(half // LANES,),
        in_specs=[pl.BlockSpec(blk, lambda j: (0, 0, cb_val + j)), pl.BlockSpec(blk, lambda j: (0, 0, cb_gate + j)),
                  pl.BlockSpec((CONF_K, LANES), lambda j: (0, cb_w + j)), pl.BlockSpec(blk, lambda j: (0, 0, j))],
        out_specs=[pl.BlockSpec(blk, lambda j: (0, 0, j)), pl.BlockSpec(blk, lambda j: (0, 0, j)),
                   pl.BlockSpec((CONF_K, LANES), lambda j: (0, j))],
        out_shape=[jax.ShapeDtypeStruct((gw, gw, half), MXU_DTYPE), jax.ShapeDtypeStruct((gw, gw, half), MXU_DTYPE),
                   jax.ShapeDtypeStruct((CONF_K, half), F32)],
        scratch_shapes=[pltpu.VMEM(pshape, F32), pltpu.VMEM(pshape, F32)],
        name="conf_conv_bwd_v" if vertical else "conf_conv_bwd_h",
        compiler_params=_cparams(("arbitrary",), 12 * _nbytes(pshape, F32)),
    )(h3, h3, w_dw, dyc)


_INV_BASE = 8


def _tri_inv_raw(mats):
    c = mats[0].shape[0]
    ri = lax.broadcasted_iota(jnp.int32, (c, c), 0)
    ci = lax.broadcasted_iota(jnp.int32, (c, c), 1)
    eye = (ri == ci).astype(F32)
    base = min(_INV_BASE, c)
    same = lambda size: (ri // size) == (ci // size)
    dmat = [jnp.where(same(base), a, 0.0) for a in mats]
    x = [eye - dm for dm in dmat]
    pw = [_raw_dot(dm, dm, "nn", "hi3") for dm in dmat]
    span = 2
    while span < base:
        x = [xi + _raw_dot(xi, p, "nn", "hi3") for xi, p in zip(x, pw)]
        span *= 2
        if span < base:
            pw = [_raw_dot(p, p, "nn", "hi3") for p in pw]
    size = base
    while size < c:
        off = jnp.logical_and(jnp.logical_not(same(size)), same(2 * size))
        t = [_raw_dot(xi, jnp.where(off, a, 0.0), "nn", "hi3") for xi, a in zip(x, mats)]
        x = [xi - _raw_dot(ti, xi, "nn", "hi3") for xi, ti in zip(x, t)]
        size *= 2
    return x


@jax.custom_vjp
def _tri_inv(mats):
    return _tri_inv_raw(mats)


def _tri_inv_fwd(mats):
    x = _tri_inv_raw(mats)
    return x, x


def _tri_inv_bwd(x, dx):
    t = [_raw_dot(xi, di, "tn", "hi3") for xi, di in zip(x, dx)]
    return ([-_raw_dot(ti, xi, "nt", "hi3") for ti, xi in zip(t, x)],)


_tri_inv.defvjp(_tri_inv_fwd, _tri_inv_bwd)


@jax.custom_vjp
def _tri_inv_saved(mats, saved):
    return saved


def _tri_inv_saved_fwd(mats, saved):
    return saved, saved


def _tri_inv_saved_bwd(x, dx):
    return _tri_inv_bwd(x, dx)[0], [jnp.zeros_like(xi) for xi in x]


_tri_inv_saved.defvjp(_tri_inv_saved_fwd, _tri_inv_saved_bwd)


def _delta_chunk(qa, ka, va, graw, braw, alog, dtb, s, *, rev, minv_saved=None, with_minv=False):
    c = qa[0].shape[0]

    def each(f, *lists):
        return [f(*xs) for xs in zip(*lists, strict=True)]

    def l2n(t):
        return t * lax.rsqrt(jnp.sum(t * t, axis=-1, keepdims=True) + 1e-6)

    ri = lax.broadcasted_iota(jnp.int32, (c, c), 0)
    ci = lax.broadcasted_iota(jnp.int32, (c, c), 1)
    incl = (ci >= ri) if rev else (ci <= ri)
    strict = (ci > ri) if rev else (ci < ri)
    tmat = incl.astype(F32)
    tmat_t = ((ri >= ci) if rev else (ri <= ci)).astype(F32)

    q = each(lambda t: l2n(t) * (HD ** -0.5), qa)
    k = each(l2n, ka)
    g = each(lambda al, gr, dt: -jnp.exp(al) * _softplus(gr + dt), alog, graw, dtb)
    beta = each(jax.nn.sigmoid, braw)
    gc_wide = each(lambda t: _xdot(tmat, jnp.broadcast_to(t, (c, HD)), "nn", "a"), g)
    gc_rows = gc_wide if c == HD else each(lambda t: _xdot(tmat, jnp.broadcast_to(t, (c, c)), "nn", "a"), g)
    gc_cols = each(lambda t: _xdot(jnp.broadcast_to(t, (c, c)), tmat_t, "tn", "b"), g)
    gamma = each(lambda r, cc: jnp.where(incl, jnp.exp(jnp.where(incl, r - cc, 0.0)), 0.0), gc_rows, gc_cols)
    kb = each(lambda a, b: a * b, k, beta)
    a_mat = each(lambda a, b, gm: jnp.where(strict, _dot(a, b, "nt", "mxu") * gm, 0.0), kb, k, gamma)
    minv = _tri_inv(a_mat) if minv_saved is None else _tri_inv_saved(a_mat, minv_saved)
    eg = each(jnp.exp, gc_wide)
    u = each(lambda mi, v, b: _dot(mi, v * b, "nn", "hi3"), minv, va, beta)
    w = each(lambda mi, a, e: _dot(mi, a * e, "nn", "hi3"), minv, kb, eg)
    attn = each(lambda a, b, gm: _dot(a, b, "nt", "mxu") * gm, q, k, gamma)
    tot = each(lambda t: jnp.sum(t, axis=0, keepdims=True), g)
    q_dec = each(lambda a, e: a * e, q, eg)
    k_dec = each(lambda a, t, gw_: a * jnp.exp(t - gw_), k, tot, gc_wide)
    v_new = each(lambda uu, ww, ss: uu - _dot(ww, ss, "nn", "mxu"), u, w, s)
    o = each(lambda qd, ss, at, vn: _dot(qd, ss, "nn", "mxu") + _dot(at, vn, "nn", "mxu"), q_dec, s, attn, v_new)
    s_new = each(lambda ss, t, kd, vn: ss * jnp.exp(t) + _dot(kd, vn, "tn", "mxu"), s, tot, k_dec, v_new)
    return (o, s_new, minv) if with_minv else (o, s_new)


def _delta_args(a_ref, g_ref, al_ref, dt_ref, s_ref, hb, gcol):
    gts = [g_ref[hh] for hh in range(hb)]
    return ([a_ref[0, hh] for hh in range(hb)], [a_ref[1, hh] for hh in range(hb)], [a_ref[2, hh] for hh in range(hb)],
            [t[:, gcol:gcol + 1] for t in gts], [t[:, gcol + 1:gcol + 2] for t in gts],
            [al_ref[hh, 0:1, 0:1] for hh in range(hb)], [dt_ref[hh, 0:1, 0:1] for hh in range(hb)],
            [s_ref[hh] for hh in range(hb)])


def _delta_chunk_of_step(step, nch, nlc, rev):
    return (nch - 1 - step) if rev else (step + nlc) % nch


def _delta_fwd(act, gates_hm, alog_b, dtb_b, *, n_lat, n_ctx, nh, hb, rev):
    nt = n_lat + n_ctx
    nch, nlc = nt // CHUNK, n_lat // CHUNK
    gcol = 2 if rev else 0
    chunk = functools.partial(_delta_chunk_of_step, nch=nch, nlc=nlc, rev=rev)

    def body(a_ref, g_ref, al_ref, dt_ref, o_ref, st_ref, mi_ref, s_ref):
        n = pl.program_id(1)

        @pl.when(n == 0)
        def _():
            s_ref[...] = jnp.zeros_like(s_ref)

        args = _delta_args(a_ref, g_ref, al_ref, dt_ref, s_ref, hb, gcol)
        o, s_new, minv = _delta_chunk(*args, rev=rev, with_minv=True)
        for hh in range(hb):
            st_ref[hh] = args[7][hh]
            mi_ref[hh] = minv[hh]
            o_ref[:, hh * HD:(hh + 1) * HD] = o[hh]
            s_ref[hh] = s_new[hh]

    par = pl.BlockSpec((hb, 8, LANES), lambda h, n: (h, 0, 0))
    return pl.pallas_call(
        body, grid=(nh // hb, nch),
        in_specs=[pl.BlockSpec((3, hb, CHUNK, HD), lambda h, n: (0, h, chunk(n), 0)),
                  pl.BlockSpec((hb, CHUNK, 4), lambda h, n: (h, chunk(n), 0)), par, par],
        out_specs=[pl.BlockSpec((CHUNK, hb * HD), lambda h, n: (chunk(n), h)),
                   pl.BlockSpec((hb, None, HD, HD), lambda h, n: (h, n, 0, 0)),
                   pl.BlockSpec((hb, None, CHUNK, CHUNK), lambda h, n: (h, n, 0, 0))],
        out_shape=[jax.ShapeDtypeStruct((nt, nh * HD), F32), jax.ShapeDtypeStruct((nh, nch, HD, HD), F32),
                   jax.ShapeDtypeStruct((nh, nch, CHUNK, CHUNK), F32)],
        scratch_shapes=[pltpu.VMEM((hb, HD, HD), F32)], name="delta_fwd_rev" if rev else "delta_fwd",
        compiler_params=_cparams(("arbitrary", "arbitrary"), 0),
    )(act, gates_hm, alog_b, dtb_b)


def _delta_bwd(act, gates_hm, alog_b, dtb_b, states, minvs, do, *, n_lat, n_ctx, nh, hb, rev):
    nt = n_lat + n_ctx
    nch, nlc = nt // CHUNK, n_lat // CHUNK
    gcol = 2 if rev else 0

    def chunk(m):
        return _delta_chunk_of_step(nch - 1 - m, nch, nlc, rev)

    def body(a_ref, g_ref, al_ref, dt_ref, st_ref, mi_ref, do_ref, da_ref, dg_ref, dal_ref, ddt_ref, ds_ref):
        m = pl.program_id(1)

        @pl.when(m == 0)
        def _():
            ds_ref[...] = jnp.zeros_like(ds_ref)
            dal_ref[...] = jnp.zeros_like(dal_ref)
            ddt_ref[...] = jnp.zeros_like(ddt_ref)

        is_lat = chunk(m) < nlc
        args = _delta_args(a_ref, g_ref, al_ref, dt_ref, st_ref, hb, gcol)
        _, vjp = jax.vjp(functools.partial(_delta_chunk, rev=rev, minv_saved=[mi_ref[hh] for hh in range(hb)]), *args)
        do_h = [jnp.where(is_lat, do_ref[:, hh * HD:(hh + 1) * HD], 0.0) for hh in range(hb)]
        grads = vjp((do_h, [ds_ref[hh] for hh in range(hb)]))
        for hh in range(hb):
            dq, dk, dv, dgr, dbr, dal, ddt, ds = [gr[hh] for gr in grads]
            da_ref[0, hh] = dq
            da_ref[1, hh] = dk
            da_ref[2, hh] = dv
            lane = lax.broadcasted_iota(jnp.int32, (CHUNK, 4), 1)
            dg_ref[hh] = jnp.where(lane == 0, dgr, jnp.where(lane == 1, dbr, 0.0))
            dal_ref[hh] += jnp.broadcast_to(dal, (8, LANES))
            ddt_ref[hh] += jnp.broadcast_to(ddt, (8, LANES))
            ds_ref[hh] = ds

    par = pl.BlockSpec((hb, 8, LANES), lambda h, m: (h, 0, 0))
    return pl.pallas_call(
        body, grid=(nh // hb, nch),
        in_specs=[pl.BlockSpec((3, hb, CHUNK, HD), lambda h, m: (0, h, chunk(m), 0)),
                  pl.BlockSpec((hb, CHUNK, 4), lambda h, m: (h, chunk(m), 0)), par, par,
                  pl.BlockSpec((hb, None, HD, HD), lambda h, m: (h, nch - 1 - m, 0, 0)),
                  pl.BlockSpec((hb, None, CHUNK, CHUNK), lambda h, m: (h, nch - 1 - m, 0, 0)),
                  pl.BlockSpec((CHUNK, hb * HD), lambda h, m: (jnp.minimum(chunk(m), nlc - 1), h))],
        out_specs=[pl.BlockSpec((3, hb, CHUNK, HD), lambda h, m: (0, h, chunk(m), 0)),
                   pl.BlockSpec((hb, CHUNK, 4), lambda h, m: (h, chunk(m), 0)), par, par],
        out_shape=[jax.ShapeDtypeStruct((3, nh, nt, HD), F32), jax.ShapeDtypeStruct((nh, nt, 4), F32),
                   jax.ShapeDtypeStruct((nh, 8, LANES), F32), jax.ShapeDtypeStruct((nh, 8, LANES), F32)],
        scratch_shapes=[pltpu.VMEM((hb, HD, HD), F32)], name="delta_bwd_rev" if rev else "delta_bwd",
        compiler_params=_cparams(("arbitrary", "arbitrary"), 0),
    )(act, gates_hm, alog_b, dtb_b, states, minvs, do)


def _my_pos():
    return lax.axis_index("x"), lax.axis_index("y"), lax.axis_index("c")


def _flip(v, d):
    return 1 - v if d else v


def _bcast8(block, name):
    r, w = block.shape
    assert r % 8 == 0 and w % LANES == 0

    def body(x_ref, o_ref, send_sems, recv_sems, local_sem):
        x, y, c = _my_pos()

        def rows(px, py, pc):
            return o_ref.at[pl.ds((4 * px + 2 * py + pc) * r, r), :]

        mine = pltpu.make_async_copy(x_ref, rows(x, y, c), local_sem)
        mine.start()
        copies = []
        for kk in range(1, 8):
            dx, dy, dc = (kk >> 2) & 1, (kk >> 1) & 1, kk & 1
            peer = (_flip(x, dx), _flip(y, dy), _flip(c, dc))
            cp = pltpu.make_async_remote_copy(src_ref=x_ref, dst_ref=rows(x, y, c), send_sem=send_sems.at[kk - 1],
                                              recv_sem=recv_sems.at[kk - 1], device_id=peer, device_id_type=pl.DeviceIdType.MESH)
            cp.start()
            copies.append((cp, peer))
        for kk, (cp, peer) in enumerate(copies):
            pltpu.make_async_remote_copy(src_ref=x_ref, dst_ref=rows(*peer), send_sem=send_sems.at[kk], recv_sem=recv_sems.at[kk],
                                         device_id=peer, device_id_type=pl.DeviceIdType.MESH).wait_recv()
        for cp, _ in copies:
            cp.wait_send()
        mine.wait()

    return pl.pallas_call(
        body, out_shape=jax.ShapeDtypeStruct((8 * r, w), block.dtype),
        in_specs=[pl.BlockSpec(memory_space=pltpu.VMEM)], out_specs=pl.BlockSpec(memory_space=pltpu.VMEM),
        scratch_shapes=[pltpu.SemaphoreType.DMA((7,)), pltpu.SemaphoreType.DMA((7,)), pltpu.SemaphoreType.DMA], name=name,
        compiler_params=pltpu.CompilerParams(vmem_limit_bytes=int(max(32 * 1024 * 1024, 12 * _nbytes((r, w), block.dtype)))),
    )(block)


_CHIP_PEERS = ((1, 0), (0, 1), (1, 1))


def _col_half(ref, c, lead):
    hc = ref.shape[-1] // 2
    return ref.at[(*lead, slice(None), pl.ds(pl.multiple_of(c * hc, LANES), hc))]


def _into_slot(w, chip_idx, after, name):
    r, cc = w.shape
    tc = _pick(cc, 512)

    def body(i_ref, w_ref, after_ref, o_ref):
        o_ref[...] = w_ref[...].astype(o_ref.dtype)

    grid_spec = pltpu.PrefetchScalarGridSpec(
        num_scalar_prefetch=1, grid=(cc // tc,), in_specs=[pl.BlockSpec((r, tc), lambda j, s: (0, j)), ANY],
        out_specs=pl.BlockSpec((None, r, tc), lambda j, s: (s[0], 0, j)))
    return pl.pallas_call(body, grid_spec=grid_spec, out_shape=jax.ShapeDtypeStruct((4, r, cc), MXU_DTYPE), name=name,
                          compiler_params=_cparams(("arbitrary",), 6 * _nbytes((r, tc), F32)))(chip_idx, w, after)


_HBM = pl.BlockSpec(memory_space=pltpu.HBM)
_SEM = pl.BlockSpec(memory_space=pltpu.SEMAPHORE)
_DATAFLOW = pltpu.SideEffectType.DATAFLOW_SIDE_EFFECTING


def _in_hbm(a):
    return pltpu.with_memory_space_constraint(a, pltpu.HBM)


def _gather_start(bufs, after, name):
    n = len(bufs)

    def body(*refs):
        ins, send_sems, recv_sems, token = refs[:n], refs[n + 1], refs[n + 2], refs[-1]
        x, y, c = _my_pos()
        for a in range(n):
            piece = _col_half(ins[a], c, (2 * x + y,))
            for kk, (dx, dy) in enumerate(_CHIP_PEERS):
                pltpu.make_async_remote_copy(src_ref=piece, dst_ref=piece, send_sem=send_sems.at[3 * a + kk], recv_sem=recv_sems.at[3 * a + kk],
                                             device_id=(_flip(x, dx), _flip(y, dy), c), device_id_type=pl.DeviceIdType.MESH).start()
        token[...] = jnp.zeros_like(token)

    res = pl.pallas_call(
        body, name=name,
        out_shape=(pltpu.SemaphoreType.DMA((3 * n,)), pltpu.SemaphoreType.DMA((3 * n,)), *[pltpu.HBM(b.shape, b.dtype) for b in bufs],
                   jax.ShapeDtypeStruct((8, LANES), F32)),
        in_specs=[_HBM] * n + [ANY], out_specs=(_SEM, _SEM, *[_HBM] * n, pl.BlockSpec(memory_space=pltpu.VMEM)),
        input_output_aliases={a: 2 + a for a in range(n)}, compiler_params=pltpu.CompilerParams(has_side_effects=_DATAFLOW),
    )(*[_in_hbm(b) for b in bufs], after)
    return res[0], res[1], list(res[2:2 + n]), res[-1]


def _gather_wait(send_sems, recv_sems, bufs, after, name):
    n = len(bufs)

    def body(*refs):
        ins, s_sems, r_sems = refs[:n], refs[n], refs[n + 1]
        x, y, c = _my_pos()
        for a in range(n):
            for kk, (dx, dy) in enumerate(_CHIP_PEERS):
                px, py = _flip(x, dx), _flip(y, dy)
                cp = pltpu.make_async_remote_copy(src_ref=_col_half(ins[a], c, (2 * x + y,)), dst_ref=_col_half(ins[a], c, (2 * px + py,)),
                                                  send_sem=s_sems.at[3 * a + kk], recv_sem=r_sems.at[3 * a + kk], device_id=(px, py, c),
                                                  device_id_type=pl.DeviceIdType.MESH)
                cp.wait_send()
                cp.wait_recv()

    return pl.pallas_call(
        body, name=name, out_shape=[pltpu.HBM(b.shape, b.dtype) for b in bufs],
        in_specs=[_HBM] * n + [_SEM, _SEM] + [ANY] * len(after), out_specs=[_HBM] * n, input_output_aliases={a: a for a in range(n)},
        compiler_params=pltpu.CompilerParams(has_side_effects=_DATAFLOW),
    )(*bufs, send_sems, recv_sems, *after)


def _gather_forward(bufs, name):
    n = len(bufs)

    def body(*refs):
        outs = refs[n:2 * n]
        send_sems, recv_sems = refs[2 * n:]
        x, y, c = _my_pos()
        sib = (x, y, 1 - c)

        def remote(a, kk, half):
            dx, dy = _CHIP_PEERS[kk]
            piece = _col_half(outs[a], half, (2 * _flip(x, dx) + _flip(y, dy),))
            return pltpu.make_async_remote_copy(src_ref=piece, dst_ref=piece, send_sem=send_sems.at[3 * a + kk], recv_sem=recv_sems.at[3 * a + kk],
                                                device_id=sib, device_id_type=pl.DeviceIdType.MESH)

        sends = [remote(a, kk, c) for a in range(n) for kk in range(3)]
        for cp in sends:
            cp.start()
        for a in range(n):
            for kk in range(3):
                remote(a, kk, 1 - c).wait_recv()
        for cp in sends:
            cp.wait_send()

    return pl.pallas_call(
        body, out_shape=[jax.ShapeDtypeStruct(b.shape, b.dtype) for b in bufs],
        in_specs=[ANY] * n, out_specs=[ANY] * n, input_output_aliases={a: a for a in range(n)},
        scratch_shapes=[pltpu.SemaphoreType.DMA((3 * n,))] * 2, name=name,
    )(*bufs)


def _forward_start(bufs, after, name):
    n = len(bufs)

    def body(*refs):
        ins, send_sems, recv_sems, token = refs[:n], refs[n + 1], refs[n + 2], refs[-1]
        x, y, c = _my_pos()
        for a in range(n):
            for kk, (dx, dy) in enumerate(_CHIP_PEERS):
                piece = _col_half(ins[a], c, (2 * _flip(x, dx) + _flip(y, dy),))
                pltpu.make_async_remote_copy(src_ref=piece, dst_ref=piece, send_sem=send_sems.at[3 * a + kk], recv_sem=recv_sems.at[3 * a + kk],
                                             device_id=(x, y, 1 - c), device_id_type=pl.DeviceIdType.MESH).start()
        token[...] = jnp.zeros_like(token)

    res = pl.pallas_call(
        body, name=name,
        out_shape=(pltpu.SemaphoreType.DMA((3 * n,)), pltpu.SemaphoreType.DMA((3 * n,)), *[pltpu.HBM(b.shape, b.dtype) for b in bufs],
                   jax.ShapeDtypeStruct((8, LANES), F32)),
        in_specs=[_HBM] * n + [ANY], out_specs=(_SEM, _SEM, *[_HBM] * n, pl.BlockSpec(memory_space=pltpu.VMEM)),
        input_output_aliases={a: 2 + a for a in range(n)}, compiler_params=pltpu.CompilerParams(has_side_effects=_DATAFLOW),
    )(*[_in_hbm(b) for b in bufs], after)
    return res[0], res[1], list(res[2:2 + n]), res[-1]


def _forward_wait(send_sems, recv_sems, bufs, after, name):
    n = len(bufs)

    def body(*refs):
        ins, s_sems, r_sems = refs[:n], refs[n], refs[n + 1]
        x, y, c = _my_pos()
        for a in range(n):
            for kk, (dx, dy) in enumerate(_CHIP_PEERS):
                slot = 2 * _flip(x, dx) + _flip(y, dy)
                cp = pltpu.make_async_remote_copy(src_ref=_col_half(ins[a], c, (slot,)), dst_ref=_col_half(ins[a], 1 - c, (slot,)),
                                                  send_sem=s_sems.at[3 * a + kk], recv_sem=r_sems.at[3 * a + kk], device_id=(x, y, 1 - c),
                                                  device_id_type=pl.DeviceIdType.MESH)
                cp.wait_send()
                cp.wait_recv()

    return pl.pallas_call(
        body, name=name, out_shape=[pltpu.HBM(b.shape, b.dtype) for b in bufs],
        in_specs=[_HBM] * n + [_SEM, _SEM] + [ANY] * len(after), out_specs=[_HBM] * n, input_output_aliases={a: a for a in range(n)},
        compiler_params=pltpu.CompilerParams(has_side_effects=_DATAFLOW),
    )(*bufs, send_sems, recv_sems, *after)


def _scatter_start(parts, after, name):
    n = len(parts)

    def body(*refs):
        ins, lands, send_sems, recv_sems, token = refs[:n], refs[n:2 * n], refs[2 * n + 1], refs[2 * n + 2], refs[-1]
        x, y, c = _my_pos()
        for a in range(n):
            for kk, (dx, dy) in enumerate(_CHIP_PEERS):
                px, py = _flip(x, dx), _flip(y, dy)
                pltpu.make_async_remote_copy(src_ref=ins[a].at[2 * px + py], dst_ref=lands[a].at[2 * x + y], send_sem=send_sems.at[3 * a + kk],
                                             recv_sem=recv_sems.at[3 * a + kk], device_id=(px, py, c), device_id_type=pl.DeviceIdType.MESH).start()
        token[...] = jnp.zeros_like(token)

    thru = [pltpu.HBM(p.shape, p.dtype) for p in parts]
    res = pl.pallas_call(
        body, name=name,
        out_shape=(pltpu.SemaphoreType.DMA((3 * n,)), pltpu.SemaphoreType.DMA((3 * n,)), *thru, *thru, jax.ShapeDtypeStruct((8, LANES), F32)),
        in_specs=[_HBM] * (2 * n) + [ANY], out_specs=(_SEM, _SEM, *[_HBM] * (2 * n), pl.BlockSpec(memory_space=pltpu.VMEM)),
        input_output_aliases={a: 2 + a for a in range(2 * n)}, compiler_params=pltpu.CompilerParams(has_side_effects=_DATAFLOW),
    )(*[_in_hbm(p) for p in parts], *[_in_hbm(lax.empty(p.shape, p.dtype)) for p in parts], after)
    return res[0], res[1], list(res[2:2 + n]), list(res[2 + n:2 + 2 * n]), res[-1]


def _scatter_wait(send_sems, recv_sems, parts, lands, after, name):
    n = len(parts)

    def body(*refs):
        ins, lnd, s_sems, r_sems = refs[:n], refs[n:2 * n], refs[2 * n], refs[2 * n + 1]
        x, y, c = _my_pos()
        for a in range(n):
            for kk, (dx, dy) in enumerate(_CHIP_PEERS):
                px, py = _flip(x, dx), _flip(y, dy)
                cp = pltpu.make_async_remote_copy(src_ref=ins[a].at[2 * px + py], dst_ref=lnd[a].at[2 * px + py], send_sem=s_sems.at[3 * a + kk],
                                                  recv_sem=r_sems.at[3 * a + kk], device_id=(px, py, c), device_id_type=pl.DeviceIdType.MESH)
                cp.wait_send()
                cp.wait_recv()

    thru = [pltpu.HBM(p.shape, p.dtype) for p in parts]
    res = pl.pallas_call(
        body, name=name, out_shape=[*thru, *thru],
        in_specs=[_HBM] * (2 * n) + [_SEM, _SEM] + [ANY] * len(after), out_specs=[_HBM] * (2 * n),
        input_output_aliases={a: a for a in range(2 * n)}, compiler_params=pltpu.CompilerParams(has_side_effects=_DATAFLOW),
    )(*parts, *lands, send_sems, recv_sems, *after)
    return list(res[:n]), list(res[n:])


def _pair_send_start(grads, after, name):
    n = len(grads)

    def body(*refs):
        ins, lands, send_sems, recv_sems, token = refs[:n], refs[n:2 * n], refs[2 * n + 1], refs[2 * n + 2], refs[-1]
        x, y, c = _my_pos()
        for a in range(n):
            pltpu.make_async_remote_copy(src_ref=_col_half(ins[a], 1 - c, (slice(None),)), dst_ref=lands[a], send_sem=send_sems.at[a],
                                         recv_sem=recv_sems.at[a], device_id=(x, y, 1 - c), device_id_type=pl.DeviceIdType.MESH).start()
        token[...] = jnp.zeros_like(token)

    land_shapes = [(4, g.shape[1], g.shape[2] // 2) for g in grads]
    res = pl.pallas_call(
        body, name=name,
        out_shape=(pltpu.SemaphoreType.DMA((n,)), pltpu.SemaphoreType.DMA((n,)), *[pltpu.HBM(g.shape, g.dtype) for g in grads],
                   *[pltpu.HBM(s, g.dtype) for s, g in zip(land_shapes, grads)], jax.ShapeDtypeStruct((8, LANES), F32)),
        in_specs=[_HBM] * (2 * n) + [ANY], out_specs=(_SEM, _SEM, *[_HBM] * (2 * n), pl.BlockSpec(memory_space=pltpu.VMEM)),
        input_output_aliases={a: 2 + a for a in range(2 * n)}, compiler_params=pltpu.CompilerParams(has_side_effects=_DATAFLOW),
    )(*[_in_hbm(g) for g in grads], *[_in_hbm(lax.empty(s, g.dtype)) for s, g in zip(land_shapes, grads)], after)
    return res[0], res[1], list(res[2:2 + n]), list(res[2 + n:2 + 2 * n]), res[-1]


def _pair_send_wait(send_sems, recv_sems, grads, lands, after, name):
    n = len(grads)

    def body(*refs):
        ins, lnd, s_sems, r_sems = refs[:n], refs[n:2 * n], refs[2 * n], refs[2 * n + 1]
        x, y, c = _my_pos()
        for a in range(n):
            cp = pltpu.make_async_remote_copy(src_ref=_col_half(ins[a], 1 - c, (slice(None),)), dst_ref=lnd[a], send_sem=s_sems.at[a],
                                              recv_sem=r_sems.at[a], device_id=(x, y, 1 - c), device_id_type=pl.DeviceIdType.MESH)
            cp.wait_send()
            cp.wait_recv()

    res = pl.pallas_call(
        body, name=name, out_shape=[*[pltpu.HBM(g.shape, g.dtype) for g in grads], *[pltpu.HBM(l.shape, l.dtype) for l in lands]],
        in_specs=[_HBM] * (2 * n) + [_SEM, _SEM, ANY], out_specs=[_HBM] * (2 * n), input_output_aliases={a: a for a in range(2 * n)},
        compiler_params=pltpu.CompilerParams(has_side_effects=_DATAFLOW),
    )(*grads, *lands, send_sems, recv_sems, after)
    return list(res[:n]), list(res[n:])


def _pair_join_halves(bufs, name):
    n = len(bufs)

    def body(*refs):
        outs = refs[n:2 * n]
        send_sems, recv_sems = refs[2 * n:]
        x, y, c = _my_pos()
        sib = (x, y, 1 - c)
        sends = []
        for a in range(n):
            mine = _col_half(outs[a], c, ())
            cp = pltpu.make_async_remote_copy(src_ref=mine, dst_ref=mine, send_sem=send_sems.at[a], recv_sem=recv_sems.at[a],
                                              device_id=sib, device_id_type=pl.DeviceIdType.MESH)
            cp.start()
            sends.append(cp)
        for a in range(n):
            other = _col_half(outs[a], 1 - c, ())
            pltpu.make_async_remote_copy(src_ref=other, dst_ref=other, send_sem=send_sems.at[a], recv_sem=recv_sems.at[a],
                                         device_id=sib, device_id_type=pl.DeviceIdType.MESH).wait_recv()
        for cp in sends:
            cp.wait_send()

    return pl.pallas_call(
        body, out_shape=[jax.ShapeDtypeStruct(b.shape, b.dtype) for b in bufs],
        in_specs=[ANY] * n, out_specs=[ANY] * n, input_output_aliases={a: a for a in range(n)},
        scratch_shapes=[pltpu.SemaphoreType.DMA((n,)), pltpu.SemaphoreType.DMA((n,))], name=name,
    )(*bufs)


def _add_my_half(g, recv, c_idx, name):
    _, r, cc = g.shape
    hc = cc // 2
    tc = _pick(hc, 256)
    per = hc // tc

    def body(c_ref, g_ref, r_ref, o_ref):
        o_ref[...] = (g_ref[...] + r_ref[...]).astype(o_ref.dtype)

    grid_spec = pltpu.PrefetchScalarGridSpec(
        num_scalar_prefetch=1, grid=(4, per),
        in_specs=[pl.BlockSpec((None, r, tc), lambda s, j, c_ref: (s, 0, c_ref[0] * per + j)),
                  pl.BlockSpec((None, r, tc), lambda s, j, c_ref: (s, 0, j))],
        out_specs=pl.BlockSpec((None, r, tc), lambda s, j, c_ref: (s, 0, j)))
    return pl.pallas_call(body, grid_spec=grid_spec, out_shape=jax.ShapeDtypeStruct((4, r, hc), WIRE_DTYPE), name=name,
                          compiler_params=_cparams(("arbitrary", "arbitrary"), 6 * _nbytes((r, tc), F32)))(c_idx, g, recv)


def _sum_slots(own, recv, slots, name):
    _, r, hc = own.shape
    tc = _pick(hc, 256)
    per = hc // tc

    def body(s0, s1, s2, s3, s4, a_ref, b_ref, c_ref, d_ref, o_ref):
        f = lambda ref: ref[...].astype(F32)
        o_ref[...] = ((f(a_ref) + f(b_ref)) + f(c_ref)) + f(d_ref)

    slot = lambda i: pl.BlockSpec((None, r, tc), lambda j, *s: (s[i][0], 0, j))
    grid_spec = pltpu.PrefetchScalarGridSpec(
        num_scalar_prefetch=5, grid=(per,), in_specs=[slot(0), slot(1), slot(2), slot(3)],
        out_specs=pl.BlockSpec((r, tc), lambda j, *s: (0, s[4][0] * per + j)))
    return pl.pallas_call(body, grid_spec=grid_spec, out_shape=jax.ShapeDtypeStruct((r, 2 * hc), F32), name=name,
                          compiler_params=_cparams(("arbitrary",), 14 * _nbytes((r, tc), F32)))(*slots, own, recv, recv, recv)


def _adamw_math(w, g, m, v):
    m = ADAM_B1 * m + (1.0 - ADAM_B1) * g
    v = ADAM_B2 * v + (1.0 - ADAM_B2) * (g * g)
    m_hat = m / (1.0 - ADAM_B1 ** ADAM_STEP)
    v_hat = v / (1.0 - ADAM_B2 ** ADAM_STEP)
    delta = -ADAM_LR * (m_hat / (jnp.sqrt(v_hat) + ADAM_EPS) + ADAM_WD * w)
    return delta, m, v


def _adamw(w, g, m, v, name):
    r, cc = w.shape[-2:]
    tr = _pick(r, max(8, (512 * 1024) // max(cc, 1)), 8)

    def body(w_ref, g_ref, m_ref, v_ref, d_ref, nm_ref, nv_ref):
        d, nm, nv = _adamw_math(w_ref[...], g_ref[...], m_ref[...], v_ref[...])
        d_ref[...] = d
        nm_ref[...] = nm
        nv_ref[...] = nv

    if w.ndim == 3:
        spec = pl.BlockSpec((None, tr, cc), lambda i: (0, i, 0))
    else:
        spec = pl.BlockSpec((tr, cc), lambda i: (i, 0))
    return pl.pallas_call(
        body, grid=(r // tr,), in_specs=[spec] * 4, out_specs=[spec] * 3,
        out_shape=[jax.ShapeDtypeStruct(w.shape, F32)] * 3, name=name,
        compiler_params=_cparams(("arbitrary",), 16 * _nbytes((tr, cc), F32)))(w, g, m, v)


def _sum8(allv, name):
    r8, w = allv.shape
    r = r8 // 8

    def body(a_ref, o_ref):
        acc = a_ref[0:r, :]
        for d in range(1, 8):
            acc = acc + a_ref[d * r:(d + 1) * r, :]
        o_ref[...] = acc

    return pl.pallas_call(body, out_shape=jax.ShapeDtypeStruct((r, w), F32), name=name,
                          compiler_params=_cparams((), 12 * _nbytes((r8, w), F32)))(allv)


def _dmod_prep(alld, name):
    _, w = alld.shape
    tw = _pick(w, 2048)

    def body(a_ref, d_ref, b_ref):
        ctx = a_ref[1:2, :]
        tot = a_ref[0:1, :]
        d_ref[0:1, :] = tot
        for d in range(1, 8):
            ctx = ctx + a_ref[8 * d + 1:8 * d + 2, :]
            tot = tot + a_ref[8 * d:8 * d + 1, :]
            d_ref[d:d + 1, :] = a_ref[8 * d:8 * d + 1, :]
        d_ref[8:9, :] = ctx
        d_ref[9:16, :] = jnp.zeros((7, tw), F32)
        b_ref[...] = tot + ctx

    return pl.pallas_call(
        body, grid=(w // tw,), in_specs=[pl.BlockSpec((64, tw), lambda j: (0, j))],
        out_specs=[pl.BlockSpec((16, tw), lambda j: (0, j)), pl.BlockSpec((1, tw), lambda j: (0, j))],
        out_shape=[jax.ShapeDtypeStruct((16, w), F32), jax.ShapeDtypeStruct((1, w), F32)], name=name,
        compiler_params=_cparams(("arbitrary",), 0))(alld)


def _pack(arrs, rows=8):
    flat = jnp.concatenate([a.reshape(-1).astype(F32) for a in arrs])
    per = rows * LANES
    n = flat.shape[0]
    padded = -(-n // per) * per
    flat = jnp.pad(flat, (0, padded - n))
    return flat.reshape(rows, padded // rows)


def _unpack(slab, shapes):
    flat = slab.reshape(-1)
    out, off = [], 0
    for s in shapes:
        n = math.prod(s)
        out.append(flat[off:off + n].reshape(s))
        off += n
    return out


def kernel(x, c, ctx, c_ctx, ln_in_g, ln_in_b, w_mod, b_mod, w_in, w_qkv_conv, a_log_f, dt_bias_f, a_log_b, dt_bias_b, dn_norm_g, conf_dw_w, conf_dw_b, conf_ln_g, conf_ln_b, w_out, ln1_g, ln1_b, w_mlp1, b_mlp1, w_mlp2, b_mlp2, ln2_g, ln2_b, loss_target, m_c_ctx, m_ln_in_g, m_ln_in_b, m_w_mod, m_b_mod, m_w_in, m_w_qkv_conv, m_a_log_f, m_dt_bias_f, m_a_log_b, m_dt_bias_b, m_dn_norm_g, m_conf_dw_w, m_conf_dw_b, m_conf_ln_g, m_conf_ln_b, m_w_out, m_ln1_g, m_ln1_b, m_w_mlp1, m_b_mlp1, m_w_mlp2, m_b_mlp2, m_ln2_g, m_ln2_b, v_c_ctx, v_ln_in_g, v_ln_in_b, v_w_mod, v_b_mod, v_w_in, v_w_qkv_conv, v_a_log_f, v_dt_bias_f, v_a_log_b, v_dt_bias_b, v_dn_norm_g, v_conf_dw_w, v_conf_dw_b, v_conf_ln_g, v_conf_ln_b, v_w_out, v_ln1_g, v_ln1_b, v_w_mlp1, v_b_mlp1, v_w_mlp2, v_b_mlp2, v_ln2_g, v_ln2_b):
    n_lat, d = x.shape[1], x.shape[2]
    n_ctx = ctx.shape[1]
    nt = n_lat + n_ctx
    dn = d // 2
    nh = dn // HD
    conf = d - dn
    half = conf // 2
    gw = math.isqrt(n_lat)
    ff = 4 * w_mlp1.shape[2]
    fl = w_mlp1.shape[2]
    in_cols = 4 * w_in.shape[2]
    r_in = w_in.shape[2]
    n_gate = 4 * nh
    gb = max(LANES, d // 8)
    g0 = 4 * dn + 2 * conf
    in_pad = g0 + gb
    ml = w_mod.shape[2]
    assert gw * gw == n_lat and nt % gw == 0 and in_cols == 4 * dn + n_gate + 2 * conf and nh * HD == dn
    hb = 8 if nh % 8 == 0 else (2 if nh % 2 == 0 else 1)

    mx, my, mc = _my_pos()
    chip = 2 * mx + my
    dev = 4 * mx + 2 * my + mc
    c_idx = jnp.reshape(mc, (1,)).astype(jnp.int32)

    x2, ctx2, tgt2 = x[0], ctx[0], loss_target[0]
    row = lambda a: a.reshape(1, -1)

    e0_shapes = [(d,), w_qkv_conv.shape[1:], conf_dw_w.shape[1:]]
    e0 = _bcast8(_pack([c[0], w_qkv_conv[0], conf_dw_w[0]]), "bcast_inputs")
    e0 = e0.reshape(8, -1)
    per_dev = [_unpack(e0[dd], e0_shapes) for dd in range(8)]
    all_c = jnp.stack([p[0] for p in per_dev])
    w_conv = jnp.concatenate([per_dev[2 * j][1] for j in range(4)], axis=1)
    w_dw = jnp.concatenate([per_dev[2 * j][2] for j in range(4)], axis=1)

    s16 = jnp.concatenate([all_c, c_ctx[None, :], jnp.zeros((7, d), F32)], axis=0)
    b_mod_loc = lax.dynamic_slice(b_mod, (0, chip * ml), (1, ml))
    to_mxu = lambda t: t.astype(MXU_DTYPE)
    modblk = _matmul(s16, w_mod[0], form="nn", m=16, n=ml, k=d, tm=16, tn=_pick(ml, 1024), tk=_pick(d, 1024),
                     out_dtype=F32, name="mod_fwd", bias=b_mod_loc, a_fn=lambda t: to_mxu(_silu(t)), b_fn=to_mxu)
    allmod = _bcast8(modblk, "bcast_mod").reshape(8, 16, ml)[0::2]
    mod_mine = lax.dynamic_index_in_dim(allmod, dev, axis=1, keepdims=False).reshape(1, 4 * ml)
    mod_ctx = allmod[:, 8, :].reshape(1, 4 * ml)
    sh_a, sc_a, g_a, sh_m, sc_m, g_m = [mod_mine[:, i * d:(i + 1) * d] for i in range(6)]
    csh_a, csc_a = mod_ctx[:, 0:d], mod_ctx[:, d:2 * d]

    chip_idx = jnp.reshape(chip, (1,)).astype(jnp.int32)
    win_t_loc = jnp.transpose(w_in[0]).astype(MXU_DTYPE)
    slot_in = _into_slot(win_t_loc, chip_idx, all_c, "weights_into_slot_0")
    gi_send, gi_recv, gi_bufs, gi_token = _gather_start([slot_in], allmod, "gather_w_in_start")
    slots_rest = [_into_slot(w, chip_idx, gi_token, f"weights_into_slot_{i + 1}") for i, w in enumerate([w_out[0], w_mlp1[0], w_mlp2[0]])]
    sh_a, csh_a = sh_a + gi_token[0:1, 0:1], csh_a + gi_token[0:1, 0:1]

    tmr = _pick(n_lat, 128, 8)
    tmc = _pick(n_ctx, 128, 8)
    ln_in = [row(ln_in_g), row(ln_in_b)]

    def stage_a(rv, pv):
        x0 = _ln(rv[0], pv[0], pv[1])
        return [x0 * (1.0 + pv[3]) + pv[2]], []

    xm_lat, = _rowwise(stage_a, [(x2, d, 0, 0)], ln_in + [sh_a, sc_a], [(d, MXU_DTYPE)], [], n_rows=n_lat, tm=tmr, name="ln_mod_lat")
    xm_ctx, = _rowwise(stage_a, [(ctx2, d, 0, 0)], ln_in + [csh_a, csc_a], [(d, MXU_DTYPE)], [], n_rows=n_ctx, tm=tmc, name="ln_mod_ctx")
    xm = jnp.concatenate([xm_lat, xm_ctx], axis=0)

    gw_in, = _gather_forward(_gather_wait(gi_send, gi_recv, gi_bufs, [xm] + slots_rest, "gather_w_in_wait"), "gather_w_in_forward")
    gs_send, gs_recv, gs_bufs, gs_token = _gather_start(slots_rest, gw_in, "gather_rest_start")
    win_t = gw_in.reshape(in_cols, d)
    win_t = jnp.concatenate([win_t[:4 * dn], win_t[4 * dn + n_gate:], win_t[4 * dn:4 * dn + n_gate],
                             jnp.zeros((gb - n_gate, d), MXU_DTYPE)], axis=0)

    tm_nt = _pick(nt, 1280, 16)
    tn_in = _pick(in_pad, 1280)
    tk_d = _pick(d, 4096)
    tn_in_s = _pick(in_pad, 640)
    h = _matmul(xm, win_t, form="nt", m=nt, n=in_pad, k=d, tm=tm_nt, tn=tn_in, tk=_pick(d, 2048), out_dtype=F32, name="in_proj",
                bias=jnp.zeros((1, in_pad), F32) + gs_token[0:1, 0:1])

    act = _conv7_fwd(h, w_conv, n_lat=n_lat, n_ctx=n_ctx, nh=nh)
    gates_hm = jnp.transpose(h[:, g0:g0 + n_gate].reshape(nt, 4, nh), (2, 0, 1))
    bc = lambda a: jnp.broadcast_to(a.reshape(nh, 1, 1), (nh, 8, LANES))
    dk = dict(n_lat=n_lat, n_ctx=n_ctx, nh=nh, hb=hb)
    o_f, st_f, mi_f = _delta_fwd(act, gates_hm, bc(a_log_f), bc(dt_bias_f), rev=False, **dk)
    o_b, st_b, mi_b = _delta_fwd(act, gates_hm, bc(a_log_b), bc(dt_bias_b), rev=True, **dk)

    h3 = h.reshape(nt // gw, gw, in_pad)
    ck = dict(gw=gw, col0=4 * dn, half=half)
    yh = _conf_conv_fwd(h3, w_dw, vertical=False, **ck).reshape(n_lat, half)
    yv = _conf_conv_fwd(h3, w_dw, vertical=True, **ck).reshape(n_lat, half)

    def mix_fn(o_heads, z_heads, ya, yb, dng, ba, bb, ga, gbb, la, lb):
        outs = []
        for o, z in zip(o_heads, z_heads):
            ms = jnp.mean(o * o, axis=-1, keepdims=True)
            outs.append(o * lax.rsqrt(ms + 1e-6) * dng * _silu(z))
        ya, yb = ya + ba, yb + bb
        mu = (jnp.sum(ya, axis=-1, keepdims=True) + jnp.sum(yb, axis=-1, keepdims=True)) / conf
        ya, yb = ya - mu, yb - mu
        var = (jnp.sum(ya * ya, axis=-1, keepdims=True) + jnp.sum(yb * yb, axis=-1, keepdims=True)) / conf
        rs = lax.rsqrt(var + LN_EPS)
        ca, cb = ya * rs * ga + la, yb * rs * gbb + lb
        return outs, _silu(ca), _silu(cb)

    def heads(t):
        return [t[:, i * HD:(i + 1) * HD] for i in range(nh)]

    def halves(p):
        return p[:, :half], p[:, half:]

    mix_params = [row(dn_norm_g), row(conf_dw_b), row(conf_ln_g), row(conf_ln_b)]

    def mix_args(rv, pv):
        (ba, bb), (ga, gbb), (la, lb) = halves(pv[1]), halves(pv[2]), halves(pv[3])
        return (heads(rv[0] + rv[1]), heads(rv[2]), rv[3], rv[4], pv[0], ba, bb, ga, gbb, la, lb)

    def stage_e(rv, pv):
        outs, ca, cb = mix_fn(*mix_args(rv, pv))
        return [jnp.concatenate(outs + [ca, cb], axis=-1)], []

    mix_rows = [(o_f, dn, 0, 0), (o_b, dn, 0, 0), (h, dn, 3, 0), (yh, half, 0, 0), (yv, half, 0, 0)]
    mix, = _rowwise(stage_e, mix_rows, mix_params, [(d, MXU_DTYPE)], [], n_rows=n_lat, tm=tmr, name="mixer_fwd")

    gs_bufs = _gather_wait(gs_send, gs_recv, gs_bufs, [mix], "gather_rest_wait")
    gw_out, = _gather_forward(gs_bufs[:1], "gather_w_out_forward")
    gf_send, gf_recv, gf_bufs, gf_token = _forward_start(gs_bufs[1:], gw_out, "gather_mlp_forward_start")
    wout_full = gw_out.reshape(d, d)

    tm_l = _pick(n_lat, 1024, 16)
    tn_d = _pick(d, 1024)
    y1 = _matmul(mix, wout_full, form="nn", m=n_lat, n=d, k=d, tm=tm_l, tn=tn_d, tk=tk_d, out_dtype=F32, name="out_proj",
                 bias=jnp.zeros((1, d), F32) + gf_token[0:1, 0:1])

    def res1(x0, y1v, ga, l1g, l1b, shm, scm):
        x1 = _ln(ALPHA * x0 + ga * y1v, l1g, l1b)
        return x1, x1 * (1.0 + scm) + shm

    f_params = ln_in + [g_a, row(ln1_g), row(ln1_b), sh_m, sc_m]

    def stage_f(rv, pv):
        x1, u = res1(_ln(rv[0], pv[0], pv[1]), rv[1], *pv[2:])
        return [x1, u], []

    x1, u = _rowwise(stage_f, [(x2, d, 0, 0), (y1, d, 0, 0)], f_params, [(d, F32), (d, MXU_DTYPE)], [], n_rows=n_lat, tm=tmr, name="res1_fwd")

    gw_1, gw_2 = _forward_wait(gf_send, gf_recv, gf_bufs, [u], "gather_mlp_forward_wait")
    w2_full = gw_2.reshape(ff, d)
    w1_sh = gw_1
    tn_f = _pick(fl, 1024)
    relu_h = _matmul(u, w1_sh, form="nn", m=n_lat, n=ff, k=d, tm=tm_l, tn=tn_f, tk=tk_d, out_dtype=MXU_DTYPE, name="mlp1",
                     bias=row(b_mlp1), epi=lambda r: jnp.maximum(r, 0.0), b_split=fl)
    sq = lambda t: (t.astype(F32) * t.astype(F32)).astype(MXU_DTYPE)
    tk_f = _pick(ff, 4096)
    y2 = _matmul(relu_h, w2_full, form="nn", m=n_lat, n=d, k=ff, tm=tm_l, tn=tn_d, tk=tk_f, out_dtype=F32, name="mlp2",
                 bias=row(b_mlp2), a_fn=sq)

    def loss_fn(x1v, y2v, gm, l2g, l2b, tgt):
        x2v = _ln(ALPHA * x1v + gm * y2v, l2g, l2b)
        return (0.5 / d) * jnp.sum(jnp.square(x2v - tgt))

    def stage_g(rv, pv):
        loss, grads = jax.value_and_grad(loss_fn, argnums=(0, 1, 2, 3, 4))(rv[0], rv[1], pv[0], pv[1], pv[2], rv[2])
        dx1, dy2r, dgm, dl2g, dl2b = grads
        dy2 = dy2r
        return [dx1, dy2], [jnp.reshape(loss, (1, 1)), dgm, dl2g, dl2b, jnp.sum(dy2, axis=0, keepdims=True)]

    dx1p, dy2, loss_acc, d_g_m, d_ln2_g, d_ln2_b, d_b_mlp2 = _rowwise(
        stage_g, [(x1, d, 0, 0), (y2, d, 0, 0), (tgt2, d, 0, 0)], [g_m, row(ln2_g), row(ln2_b)],
        [(d, F32), (d, MXU_DTYPE)], [LANES, d, d, d, d], n_rows=n_lat, tm=tmr, name="loss_res2_bwd")

    dhid, d_b_mlp1 = _matmul(dy2, w2_full, form="nt", m=n_lat, n=ff, k=d, tm=tm_l, tn=tn_f, tk=tk_d, out_dtype=MXU_DTYPE, name="mlp2_bwd_x",
                             epi=lambda r, rh: r * (2.0 * rh.astype(F32)), epi_in=relu_h, colsum=True)
    tk_l = _pick(n_lat, 4096, 16)
    d_w2 = _matmul(relu_h, dy2, form="tn", m=ff, n=d, k=n_lat, tm=tn_f, tn=tn_d, tk=tk_l, out_dtype=F32, name="mlp2_bwd_w", a_fn=sq)
    du = _matmul(dhid, w1_sh, form="nt", m=n_lat, n=d, k=ff, tm=tm_l, tn=tn_d, tk=_pick(fl, 4096), out_dtype=F32, name="mlp1_bwd_x", b_split=fl)
    d_w1 = _matmul(u, dhid, form="tn", m=d, n=ff, k=n_lat, tm=tn_d, tn=tn_f, tk=tk_l, out_dtype=F32, name="mlp1_bwd_w", o_split=fl)

    as_idx = lambda v: jnp.reshape(v, (1,)).astype(jnp.int32)
    slot_idx = [chip_idx, as_idx(2 * (1 - mx) + my), as_idx(2 * mx + (1 - my)), as_idx(2 * (1 - mx) + (1 - my)), c_idx]
    mp_send, mp_recv, mlp_grads, mlp_half, mp_token = _pair_send_start([d_w1, d_w2.reshape(4, fl, d)], loss_acc, "mlp_grads_pair_start")

    def stage_h(rv, pv):
        x0 = _ln(rv[0], pv[0], pv[1])
        _, vjp = jax.vjp(res1, x0, rv[1], *pv[2:])
        dx0, dy1r, dga, dl1g, dl1b, dshm, dscm = vjp((rv[3], rv[2]))
        return [dx0, dy1r], [dga, dl1g, dl1b, dshm, dscm]

    h_params = ln_in + [g_a + mp_token[0:1, 0:1]] + f_params[3:]
    dx0p, dy1, d_g_a, d_ln1_g, d_ln1_b, d_sh_m, d_sc_m = _rowwise(
        stage_h, [(x2, d, 0, 0), (y1, d, 0, 0), (du, d, 0, 0), (dx1p, d, 0, 0)], h_params,
        [(d, F32), (d, MXU_DTYPE)], [d, d, d, d, d], n_rows=n_lat, tm=tmr, name="res1_bwd")

    dmix = _matmul(dy1, wout_full, form="nt", m=n_lat, n=d, k=d, tm=tm_l, tn=tn_d, tk=tk_d, out_dtype=F32, name="out_proj_bwd_x")
    d_wout = _matmul(mix, dy1, form="tn", m=d, n=d, k=n_lat, tm=tn_d, tn=tn_d, tk=tk_l, out_dtype=F32, name="out_proj_bwd_w")

    def stage_i(rv, pv):
        args = mix_args(rv, pv)
        _, vjp = jax.vjp(mix_fn, *args)
        dm = rv[5]
        d_outs = [dm[:, i * HD:(i + 1) * HD] for i in range(nh)]
        dca, dcb = dm[:, dn:dn + half], dm[:, dn + half:]
        do_h, dz_h, dya, dyb, ddng, dba, dbb, dga, dgbb, dla, dlb = vjp((d_outs, dca, dcb))
        cat = lambda *p: jnp.concatenate(p, axis=-1)
        return ([cat(*do_h), cat(*dz_h), dya, dyb], [ddng, cat(dba, dbb), cat(dga, dgbb), cat(dla, dlb)])

    do, dz, dyh, dyv, d_dn_norm_g, d_conf_dw_b, d_conf_ln_g, d_conf_ln_b = _rowwise(
        stage_i, mix_rows + [(dmix, d, 0, 0)], mix_params, [(dn, F32), (dn, MXU_DTYPE), (half, F32), (half, F32)],
        [HD, conf, conf, conf], n_rows=n_lat, tm=tmr, name="mixer_bwd")

    mlp_grads, mlp_half = _pair_send_wait(mp_send, mp_recv, mlp_grads, mlp_half, do, "mlp_grads_pair_wait")
    mlp_sums = [_add_my_half(g, r, c_idx, f"mlp_grads_pair_add_{i}") for i, (g, r) in enumerate(zip(mlp_grads, mlp_half))]
    ms_send, ms_recv, mlp_sums, mlp_lands, ms_token = _scatter_start(mlp_sums, loss_acc, "mlp_grads_scatter_start")
    tie = ms_token[0:1, 0:1].reshape(1, 1, 1)

    dval_h, dgate_h, d_wdw_h = _conf_conv_bwd(h3, w_dw, dyh.reshape(gw, gw, half), vertical=False, **ck)
    dval_v, dgate_v, d_wdw_v = _conf_conv_bwd(h3, w_dw, dyv.reshape(gw, gw, half), vertical=True, **ck)
    dact_f, dgt_f, d_alog_f, d_dt_f = _delta_bwd(act, gates_hm, bc(a_log_f) + tie, bc(dt_bias_f), st_f, mi_f, do, rev=False, **dk)
    dact_b, dgt_b, d_alog_b, d_dt_b = _delta_bwd(act, gates_hm, bc(a_log_b) + tie, bc(dt_bias_b), st_b, mi_b, do, rev=True, **dk)
    dh_qkv, d_wconv = _conv7_bwd(h, w_conv, dact_f, dact_b, n_lat=n_lat, n_ctx=n_ctx, nh=nh)

    dgates = jnp.stack([dgt_f[..., 0], dgt_f[..., 1], dgt_b[..., 0], dgt_b[..., 1]], axis=0)
    dgates = jnp.transpose(dgates, (2, 0, 1)).reshape(nt, n_gate)
    dgates = jnp.pad(dgates, ((0, 0), (0, gb - n_gate))).astype(MXU_DTYPE)
    zrows = lambda t: jnp.pad(t, ((0, n_ctx), (0, 0)))
    dh = jnp.concatenate([dh_qkv, zrows(dz), zrows(dval_h.reshape(n_lat, half)), zrows(dval_v.reshape(n_lat, half)),
                          zrows(dgate_h.reshape(n_lat, half)), zrows(dgate_v.reshape(n_lat, half)), dgates], axis=1)

    tk_in = _pick(in_pad, 2560)
    d_win_t = _matmul(dh, xm, form="tn", m=in_pad, n=d, k=nt, tm=tn_in_s, tn=tn_d, tk=nt, out_dtype=F32, name="in_proj_bwd_w")
    d_win_t = jnp.concatenate([d_win_t[:4 * dn], d_win_t[g0:g0 + n_gate], d_win_t[4 * dn:g0]], axis=0)

    proj_grads = [d_win_t.reshape(4, r_in, d), d_wout.reshape(4, d // 4, d)]
    pp_send, pp_recv, proj_grads, proj_half, pp_token = _pair_send_start(proj_grads, loss_acc, "proj_grads_pair_start")
    dxm = _matmul(dh, win_t, form="nn", m=nt, n=d, k=in_pad, tm=tm_nt, tn=tn_d, tk=tk_in, out_dtype=F32, name="in_proj_bwd_x",
                  bias=jnp.zeros((1, d), F32) + pp_token[0:1, 0:1])
    proj_grads, proj_half = _pair_send_wait(pp_send, pp_recv, proj_grads, proj_half, dxm, "proj_grads_pair_wait")
    proj_sums = [_add_my_half(g, r, c_idx, f"proj_grads_pair_add_{i}") for i, (g, r) in enumerate(zip(proj_grads, proj_half))]
    ps_send, ps_recv, proj_sums, proj_lands, ps_token = _scatter_start(proj_sums, loss_acc, "proj_grads_scatter_start")
    mlp_sums, mlp_lands = _scatter_wait(ms_send, ms_recv, mlp_sums, mlp_lands, [ps_token], "mlp_grads_scatter_wait")
    mlp_mine = [_sum_slots(own, rcv, slot_idx, f"mlp_grads_chip_sum_{i}") for i, (own, rcv) in enumerate(zip(mlp_sums, mlp_lands))]
    g_w_mlp1, g_w_mlp2 = _pair_join_halves(mlp_mine, "mlp_grads_pair_join")
    mlp_adam = {"w_mlp1": _adamw(w_mlp1, g_w_mlp1[None], m_w_mlp1, v_w_mlp1, "adamw_w_mlp1"),
                "w_mlp2": _adamw(w_mlp2, g_w_mlp2[None], m_w_mlp2, v_w_mlp2, "adamw_w_mlp2")}
    mlp_done = [mlp_adam["w_mlp1"][2], mlp_adam["w_mlp2"][2]]
    sc_a_tied = sc_a + ps_token[0:1, 0:1]

    def mod_in(xr, lg, lb, sh, sc):
        x0 = _ln(xr, lg, lb)
        return x0, x0 * (1.0 + sc) + sh

    def stage_j(rv, pv):
        _, vjp = jax.vjp(mod_in, rv[0], *pv)
        dx0 = rv[2] if len(rv) > 2 else jnp.zeros_like(rv[1])
        dxr, dlg, dlb, dsh, dsc = vjp((dx0, rv[1]))
        return [dxr], [dlg, dlb, dsh, dsc]

    grad_x, dlg_l, dlb_l, d_sh_a, d_sc_a = _rowwise(
        stage_j, [(x2, d, 0, 0), (dxm, d, 0, 0), (dx0p, d, 0, 0)], ln_in + [sh_a, sc_a_tied], [(d, F32)], [d, d, d, d],
        n_rows=n_lat, tm=tmr, name="ln_mod_bwd_lat", after=mlp_done)
    _, dlg_c, dlb_c, d_csh_a, d_csc_a = _rowwise(
        stage_j, [(ctx2, d, 0, 0), (dxm, d, 0, n_lat // tmc)], ln_in + [csh_a, csc_a], [(d, F32)], [d, d, d, d],
        n_rows=n_ctx, tm=tmc, name="ln_mod_bwd_ctx", after=mlp_done)

    zd = jnp.zeros((1, d), F32)
    dmod_rows = jnp.concatenate([jnp.concatenate([d_sh_a, d_sc_a, d_g_a, d_sh_m, d_sc_m, d_g_m], axis=1),
                                 jnp.concatenate([d_csh_a, d_csc_a, zd, zd, zd, zd], axis=1), jnp.zeros((6, 6 * d), F32)], axis=0)
    d16, g_b_mod = _dmod_prep(_bcast8(dmod_rows, "bcast_dmod"), "dmod_prep")
    d16_loc = lax.dynamic_slice(d16, (0, chip * ml), (16, ml))
    g_w_mod = _matmul(s16, d16_loc, form="tn", m=d, n=ml, k=16, tm=_pick(d, 512), tn=_pick(ml, 1024), tk=16, out_dtype=F32,
                      name="mod_bwd_w", a_fn=_silu, kind="hi")
    dsilu = _matmul(d16_loc, w_mod[0], form="nt", m=16, n=d, k=ml, tm=16, tn=_pick(d, 1024), tk=_pick(ml, 1024), out_dtype=F32,
                    name="mod_bwd_c", a_fn=to_mxu, b_fn=to_mxu)
    d_cctx_part = dsilu[8:9, :] * (1 - mc).astype(F32)

    head_sum = lambda t: t[:, 0, 0]
    small = [d_cctx_part, dlg_l + dlg_c, dlb_l + dlb_c, d_wconv, head_sum(d_alog_f), head_sum(d_dt_f), head_sum(d_alog_b),
             head_sum(d_dt_b), d_dn_norm_g, jnp.concatenate([d_wdw_h, d_wdw_v], axis=1), d_conf_dw_b, d_conf_ln_g, d_conf_ln_b,
             d_ln1_g, d_ln1_b, d_b_mlp1, d_b_mlp2, d_ln2_g, d_ln2_b]
    small_shapes = [(d,), (d,), (d,), (SHORT_CONV, 3 * dn), (nh,), (nh,), (nh,), (nh,), (HD,), (CONF_K, conf), (conf,), (conf,), (conf,),
                    (d,), (d,), (ff,), (d,), (d,), (d,)]
    ssum = _sum8(_bcast8(_pack(small), "bcast_small_grads"), "sum_small_grads")
    (t_cctx, g_ln_in_g, g_ln_in_b, g_wconv_full, g_a_log_f, g_dt_bias_f, g_a_log_b, g_dt_bias_b, g_dn_norm_g, g_wdw_full, g_conf_dw_b,
     g_conf_ln_g, g_conf_ln_b, g_ln1_g, g_ln1_b, g_b_mlp1, g_b_mlp2, g_ln2_g, g_ln2_b) = _unpack(ssum, small_shapes)
    sg = jax.nn.sigmoid(c_ctx)
    g_c_ctx = t_cctx * (sg * (1.0 + c_ctx * (1.0 - sg)))
    g_w_qkv_conv = lax.dynamic_slice(g_wconv_full, (0, chip * w_qkv_conv.shape[2]), w_qkv_conv.shape[1:])
    g_conf_dw_w = lax.dynamic_slice(g_wdw_full, (0, chip * conf_dw_w.shape[2]), conf_dw_w.shape[1:])

    grads = dict(c_ctx=g_c_ctx, ln_in_g=g_ln_in_g, ln_in_b=g_ln_in_b, w_mod=g_w_mod[None], b_mod=g_b_mod, w_in=None,
                 w_qkv_conv=g_w_qkv_conv[None], a_log_f=g_a_log_f[None], dt_bias_f=g_dt_bias_f[None], a_log_b=g_a_log_b[None],
                 dt_bias_b=g_dt_bias_b[None], dn_norm_g=g_dn_norm_g[None], conf_dw_w=g_conf_dw_w[None], conf_dw_b=g_conf_dw_b[None],
                 conf_ln_g=g_conf_ln_g[None], conf_ln_b=g_conf_ln_b[None], w_out=None, ln1_g=g_ln1_g[None], ln1_b=g_ln1_b[None],
                 w_mlp1=g_w_mlp1[None], b_mlp1=g_b_mlp1[None], w_mlp2=g_w_mlp2[None], b_mlp2=g_b_mlp2[None], ln2_g=g_ln2_g[None],
                 ln2_b=g_ln2_b[None])
    weights = dict(c_ctx=c_ctx, ln_in_g=ln_in_g, ln_in_b=ln_in_b, w_mod=w_mod, b_mod=b_mod, w_in=w_in, w_qkv_conv=w_qkv_conv,
                   a_log_f=a_log_f, dt_bias_f=dt_bias_f, a_log_b=a_log_b, dt_bias_b=dt_bias_b, dn_norm_g=dn_norm_g, conf_dw_w=conf_dw_w,
                   conf_dw_b=conf_dw_b, conf_ln_g=conf_ln_g, conf_ln_b=conf_ln_b, w_out=w_out, ln1_g=ln1_g, ln1_b=ln1_b, w_mlp1=w_mlp1,
                   b_mlp1=b_mlp1, w_mlp2=w_mlp2, b_mlp2=b_mlp2, ln2_g=ln2_g, ln2_b=ln2_b)
    m_in = dict(c_ctx=m_c_ctx, ln_in_g=m_ln_in_g, ln_in_b=m_ln_in_b, w_mod=m_w_mod, b_mod=m_b_mod, w_in=m_w_in, w_qkv_conv=m_w_qkv_conv,
                a_log_f=m_a_log_f, dt_bias_f=m_dt_bias_f, a_log_b=m_a_log_b, dt_bias_b=m_dt_bias_b, dn_norm_g=m_dn_norm_g,
                conf_dw_w=m_conf_dw_w, conf_dw_b=m_conf_dw_b, conf_ln_g=m_conf_ln_g, conf_ln_b=m_conf_ln_b, w_out=m_w_out, ln1_g=m_ln1_g,
                ln1_b=m_ln1_b, w_mlp1=m_w_mlp1, b_mlp1=m_b_mlp1, w_mlp2=m_w_mlp2, b_mlp2=m_b_mlp2, ln2_g=m_ln2_g, ln2_b=m_ln2_b)
    v_in = dict(c_ctx=v_c_ctx, ln_in_g=v_ln_in_g, ln_in_b=v_ln_in_b, w_mod=v_w_mod, b_mod=v_b_mod, w_in=v_w_in, w_qkv_conv=v_w_qkv_conv,
                a_log_f=v_a_log_f, dt_bias_f=v_dt_bias_f, a_log_b=v_a_log_b, dt_bias_b=v_dt_bias_b, dn_norm_g=v_dn_norm_g,
                conf_dw_w=v_conf_dw_w, conf_dw_b=v_conf_dw_b, conf_ln_g=v_conf_ln_g, conf_ln_b=v_conf_ln_b, w_out=v_w_out, ln1_g=v_ln1_g,
                ln1_b=v_ln1_b, w_mlp1=v_w_mlp1, b_mlp1=v_b_mlp1, w_mlp2=v_w_mlp2, b_mlp2=v_b_mlp2, ln2_g=v_ln2_g, ln2_b=v_ln2_b)
    names = list(weights)
    big_names = ("w_mod", "w_mlp1", "w_mlp2", "w_in", "w_out")
    delta, new_m, new_v = {}, {}, {}

    def big_adamw(nm):
        delta[nm], new_m[nm], new_v[nm] = mlp_adam[nm] if nm in mlp_adam else _adamw(weights[nm], grads[nm], m_in[nm], v_in[nm], f"adamw_{nm}")

    for nm in big_names[:3]:
        big_adamw(nm)
    small_names = [nm for nm in names if nm not in big_names]
    shapes = [weights[nm].shape for nm in small_names]
    for nm in small_names:
        grads[nm] = grads[nm].reshape(weights[nm].shape)
    packed = [_pack([src[nm] for nm in small_names]) for src in (weights, grads, m_in, v_in)]
    dl, mm, vv = _adamw(*packed, "adamw_small")
    for nm, a, b_, c_ in zip(small_names, _unpack(dl, shapes), _unpack(mm, shapes), _unpack(vv, shapes)):
        delta[nm], new_m[nm], new_v[nm] = a, b_, c_

    done = [new_v[nm] for nm in big_names[:3]] + [vv]
    proj_sums, proj_lands = _scatter_wait(ps_send, ps_recv, proj_sums, proj_lands, done, "proj_grads_scatter_wait")
    proj_mine = [_sum_slots(own, rcv, slot_idx, f"proj_grads_chip_sum_{i}") for i, (own, rcv) in enumerate(zip(proj_sums, proj_lands))]
    g_win_t, g_w_out = _pair_join_halves(proj_mine, "proj_grads_pair_join")
    grads["w_in"], grads["w_out"] = jnp.transpose(g_win_t)[None], g_w_out[None]
    for nm in big_names[3:]:
        big_adamw(nm)

    loss = lax.psum(loss_acc[0, 0], MESH_AXES)
    return (loss, grad_x[None], *[grads[nm] for nm in names], *[delta[nm] for nm in names],
            *[new_m[nm] for nm in names], *[new_v[nm] for nm in names])
```

```python
import functools
import math

import jax
import jax.numpy as jnp
from jax import lax
from jax.experimental import pallas as pl
from jax.experimental.pallas import tpu as pltpu

F32 = jnp.float32
BF16 = jnp.bfloat16
MXU_DTYPE = BF16
WIRE_DTYPE = BF16
HIGHEST = lax.Precision.HIGHEST

HD = 128
CHUNK = 128
SHORT_CONV = 7
CONF_K = 31
ALPHA = 2.0 ** 0.25
LN_EPS = 1e-5
V7X_VMEM_BYTES = 64 * 1024 * 1024
LANES = 128

ADAM_LR = 0.001
ADAM_B1 = 0.9
ADAM_B2 = 0.999
ADAM_EPS = 1e-08
ADAM_WD = 0.01
ADAM_STEP = 10

MESH_AXES = ("x", "y", "c")
ANY = pl.BlockSpec(memory_space=pl.ANY)


def _pick(dim, pref, mult=LANES):
    best = None
    t = mult
    while t <= min(dim, pref):
        if dim % t == 0:
            best = t
        t += mult
    return best if best is not None else dim


def _cparams(sem, vmem_est):
    limit = int(min(V7X_VMEM_BYTES - 6 * 1024 * 1024, max(32 * 1024 * 1024, vmem_est + 8 * 1024 * 1024)))
    return pltpu.CompilerParams(dimension_semantics=sem, vmem_limit_bytes=limit)


def _nbytes(shape, dtype):
    return math.prod(shape) * jnp.dtype(dtype).itemsize


def _ln(x, g, b):
    mu = jnp.mean(x, axis=-1, keepdims=True)
    xc = x - mu
    var = jnp.mean(xc * xc, axis=-1, keepdims=True)
    return xc * lax.rsqrt(var + LN_EPS) * g + b


def _silu(x):
    return x * jax.nn.sigmoid(x)


def _softplus(x):
    return jnp.maximum(x, 0.0) + jnp.log1p(jnp.exp(-jnp.abs(x)))


_DIMS = {"nn": (((1,), (0,)), ((), ())), "nt": (((1,), (1,)), ((), ())), "tn": (((0,), (0,)), ((), ()))}


def _split_bf16(x, parts):
    out, rest = [], x.astype(F32)
    for _ in range(parts):
        bits = lax.bitcast_convert_type(rest, jnp.uint32) & jnp.uint32(0xFFFF0000)
        p = lax.bitcast_convert_type(bits, F32)
        out.append(p.astype(BF16))
        rest = rest - p
    return out


def _raw_dot(a, b, form, kind):
    if MXU_DTYPE != F32:
        if kind == "mxu":
            return lax.dot_general(a.astype(MXU_DTYPE), b.astype(MXU_DTYPE), _DIMS[form], preferred_element_type=F32)
        if kind == "hi3":
            (a0, a1), (b0, b1) = _split_bf16(a, 2), _split_bf16(b, 2)
            d = lambda p, q: lax.dot_general(p, q, _DIMS[form], preferred_element_type=F32)
            return d(a0, b0) + (d(a1, b0) + d(a0, b1))
    return lax.dot_general(a.astype(F32), b.astype(F32), _DIMS[form], precision=HIGHEST, preferred_element_type=F32)


@functools.partial(jax.custom_vjp, nondiff_argnums=(2, 3))
def _dot(a, b, form, kind):
    return _raw_dot(a, b, form, kind)


def _dot_fwd(a, b, form, kind):
    return _raw_dot(a, b, form, kind), (a, b)


def _dot_bwd(form, kind, res, dc):
    a, b = res
    if form == "nn":
        return _dot(dc, b, "nt", kind), _dot(a, dc, "tn", kind)
    if form == "nt":
        return _dot(dc, b, "nn", kind), _dot(dc, a, "tn", kind)
    return _dot(b, dc, "nt", kind), _dot(a, dc, "nn", kind)


_dot.defvjp(_dot_fwd, _dot_bwd)


def _raw_xdot(a, b, form, exact):
    if MXU_DTYPE == F32:
        return _raw_dot(a, b, form, "hi")
    d = lambda p, q: lax.dot_general(p, q, _DIMS[form], preferred_element_type=F32)
    if exact == "a":
        a16, (b0, b1, b2) = a.astype(BF16), _split_bf16(b, 3)
        return d(a16, b0) + (d(a16, b1) + d(a16, b2))
    b16, (a0, a1, a2) = b.astype(BF16), _split_bf16(a, 3)
    return d(a0, b16) + (d(a1, b16) + d(a2, b16))


@functools.partial(jax.custom_vjp, nondiff_argnums=(2, 3))
def _xdot(a, b, form, exact):
    return _raw_xdot(a, b, form, exact)


def _xdot_fwd(a, b, form, exact):
    return _raw_xdot(a, b, form, exact), (a, b)


def _xdot_bwd(form, exact, res, dc):
    a, b = res
    if form == "nn" and exact == "a":
        return jnp.zeros_like(a), _xdot(a, dc, "tn", "a")
    assert form == "tn" and exact == "b"
    return _xdot(b, dc, "nt", "a"), jnp.zeros_like(b)


_xdot.defvjp(_xdot_fwd, _xdot_bwd)


def _spec2(block, idx, split=None):
    if split is None:
        return pl.BlockSpec(tuple(block), idx)
    per = split // block[1]

    def idx3(*g):
        r, cblk = idx(*g)
        return (cblk // per, r, cblk % per)

    return pl.BlockSpec((None,) + tuple(block), idx3)


def _matmul(a, b, *, form, m, n, k, tm, tn, tk, out_dtype, name, bias=None, a_fn=None, b_fn=None,
            epi=None, epi_in=None, colsum=False, kind="mxu", b_split=None, o_split=None):
    assert m % tm == 0 and n % tn == 0 and k % tk == 0, (name, m, n, k, tm, tn, tk)
    nk = k // tk
    grid = (n // tn, m // tm, nk)
    if form == "tn":
        a_spec = pl.BlockSpec((tk, tm), lambda j, i, kk: (kk, i))
    else:
        a_spec = pl.BlockSpec((tm, tk), lambda j, i, kk: (i, kk))
    if form == "nt":
        b_spec = _spec2((tn, tk), lambda j, i, kk: (j, kk), b_split)
    else:
        b_spec = _spec2((tk, tn), lambda j, i, kk: (kk, j), b_split)
    in_specs = [a_spec, b_spec]
    operands = [a, b]
    if bias is not None:
        in_specs.append(pl.BlockSpec((1, tn), lambda j, i, kk: (0, j)))
        operands.append(bias)
    if epi_in is not None:
        in_specs.append(pl.BlockSpec((tm, tn), lambda j, i, kk: (i, j)))
        operands.append(epi_in)
    out_specs = [_spec2((tm, tn), lambda j, i, kk: (i, j), o_split)]
    if o_split is None:
        out_shape = [jax.ShapeDtypeStruct((m, n), out_dtype)]
    else:
        out_shape = [jax.ShapeDtypeStruct((n // o_split, m, o_split), out_dtype)]
    if colsum:
        out_specs.append(pl.BlockSpec((1, tn), lambda j, i, kk: (0, j)))
        out_shape.append(jax.ShapeDtypeStruct((1, n), F32))
    has_bias, has_epi_in = bias is not None, epi_in is not None

    def body(*refs):
        refs = list(refs)
        a_ref, b_ref = refs[0], refs[1]
        pos = 2
        bias_ref = epi_ref = cs_ref = None
        if has_bias:
            bias_ref = refs[pos]
            pos += 1
        if has_epi_in:
            epi_ref = refs[pos]
            pos += 1
        o_ref = refs[pos]
        pos += 1
        if colsum:
            cs_ref = refs[pos]
            pos += 1
        acc_ref = refs[pos] if nk > 1 else None
        i, kk = pl.program_id(1), pl.program_id(2)

        av = a_ref[...]
        if a_fn is not None:
            av = a_fn(av)
        bv = b_ref[...]
        if b_fn is not None:
            bv = b_fn(bv)
        prod = _raw_dot(av, bv, form, kind)

        if nk > 1:
            @pl.when(kk == 0)
            def _():
                acc_ref[...] = prod

            @pl.when(jnp.logical_and(kk > 0, kk < nk - 1))
            def _():
                acc_ref[...] += prod

        @pl.when(kk == nk - 1)
        def _():
            r = acc_ref[...] + prod if nk > 1 else prod
            if has_bias:
                r = r + bias_ref[...]
            if epi is not None:
                r = epi(r, epi_ref[...]) if has_epi_in else epi(r)
            o_ref[...] = r.astype(out_dtype)
            if colsum:
                s = jnp.sum(r, axis=0, keepdims=True)

                @pl.when(i == 0)
                def _():
                    cs_ref[...] = s

                @pl.when(i > 0)
                def _():
                    cs_ref[...] += s

    est = 2 * (_nbytes((tm, tk), a.dtype) + _nbytes((tk, tn), b.dtype) + _nbytes((tm, tn), out_dtype))
    est += _nbytes((tm, tn), F32) * 2 + (2 * _nbytes((tm, tn), epi_in.dtype) if has_epi_in else 0)
    res = pl.pallas_call(
        body, grid=grid, in_specs=in_specs, out_specs=out_specs, out_shape=out_shape,
        scratch_shapes=[pltpu.VMEM((tm, tn), F32)] if nk > 1 else [], name=name,
        compiler_params=_cparams(("arbitrary", "arbitrary", "arbitrary"), est),
    )(*operands)
    return res if colsum else res[0]


def _rowwise(fn, rows, params, row_outs, acc_outs, *, n_rows, tm, name, after=()):
    assert n_rows % tm == 0, (name, n_rows, tm)
    nr, npar, nro, nac, naf = len(rows), len(params), len(row_outs), len(acc_outs), len(after)
    in_specs = [pl.BlockSpec((tm, w), functools.partial(lambda i, cb, ro: (i + ro, cb), cb=cb, ro=ro))
                for (_, w, cb, ro) in rows]
    in_specs += [pl.BlockSpec((1, p.shape[1]), lambda i: (0, 0)) for p in params] + [ANY] * naf
    out_specs = [pl.BlockSpec((tm, w), lambda i: (i, 0)) for (w, _) in row_outs]
    out_specs += [pl.BlockSpec((1, w), lambda i: (0, 0)) for w in acc_outs]
    out_shape = [jax.ShapeDtypeStruct((n_rows, w), dt) for (w, dt) in row_outs]
    out_shape += [jax.ShapeDtypeStruct((1, w), F32) for w in acc_outs]

    def body(*refs):
        rv = [r[...] for r in refs[:nr]]
        pv = [r[...] for r in refs[nr:nr + npar]]
        ro_refs = refs[nr + npar + naf:nr + npar + naf + nro]
        ac_refs = refs[nr + npar + naf + nro:]
        ro, ac = fn(rv, pv)
        for ref, val in zip(ro_refs, ro, strict=True):
            ref[...] = val.astype(ref.dtype)
        first = pl.program_id(0) == 0
        for ref, val in zip(ac_refs, ac, strict=True):
            val = jnp.broadcast_to(val.astype(F32), ref.shape)

            @pl.when(first)
            def _(ref=ref, val=val):
                ref[...] = val

            @pl.when(jnp.logical_not(first))
            def _(ref=ref, val=val):
                ref[...] += val

    io = sum(_nbytes((tm, w), a.dtype) for (a, w, _, _) in rows) + sum(_nbytes((tm, w), dt) for (w, dt) in row_outs)
    widest = max([w for (_, w, _, _) in rows] + [w for (w, _) in row_outs])
    est = 2 * io + 12 * _nbytes((tm, widest), F32)
    return pl.pallas_call(
        body, grid=(n_rows // tm,), in_specs=in_specs, out_specs=out_specs, out_shape=out_shape, name=name,
        compiler_params=_cparams(("arbitrary",), est),
    )(*[a for (a, _, _, _) in rows], *params, *after)


def _conv7_tiles(n_lat, n_ctx):
    tt = min(256, n_ctx)
    assert n_lat % tt == 0 and n_ctx % tt == 0
    return tt, [(t0, 8 + t0) for t0 in range(0, n_lat, tt)] + [(n_lat + t0, 16 + n_lat + t0) for t0 in range(0, n_ctx, tt)]


def _conv7_fill(p_ref, x_ref, n_lat, n_ctx):
    z8 = jnp.zeros((8, LANES), F32)
    p_ref[0:8, :] = z8
    p_ref[8:8 + n_lat, :] = x_ref[0:n_lat, :]
    p_ref[8 + n_lat:16 + n_lat, :] = z8
    p_ref[16 + n_lat:16 + n_lat + n_ctx, :] = x_ref[n_lat:n_lat + n_ctx, :]
    p_ref[16 + n_lat + n_ctx:24 + n_lat + n_ctx, :] = z8


def _conv7_fwd(h, w_conv, *, n_lat, n_ctx, nh):
    nt = n_lat + n_ctx
    tt, tiles = _conv7_tiles(n_lat, n_ctx)
    half = SHORT_CONV // 2

    def body(x_ref, w_ref, o_ref, p_ref):
        _conv7_fill(p_ref, x_ref, n_lat, n_ctx)
        for (row, prow) in tiles:
            acc = jnp.zeros((tt, LANES), F32)
            for kk in range(SHORT_CONV):
                acc = acc + w_ref[kk:kk + 1, :] * p_ref[pl.ds(prow + kk - half, tt), :]
            o_ref[pl.ds(row, tt), :] = _silu(acc)

    return pl.pallas_call(
        body, grid=(3 * nh,),
        in_specs=[pl.BlockSpec((nt, LANES), lambda j: (0, j)), pl.BlockSpec((SHORT_CONV, LANES), lambda j: (0, j))],
        out_specs=pl.BlockSpec((None, None, nt, LANES), lambda j: (j // nh, j % nh, 0, 0)),
        out_shape=jax.ShapeDtypeStruct((3, nh, nt, LANES), F32),
        scratch_shapes=[pltpu.VMEM((nt + 24, LANES), F32)], name="conv7_fwd",
        compiler_params=_cparams(("arbitrary",), 5 * _nbytes((nt + 24, LANES), F32)),
    )(h, w_conv)


def _conv7_bwd(h, w_conv, dact_f, dact_b, *, n_lat, n_ctx, nh):
    nt = n_lat + n_ctx
    tt, tiles = _conv7_tiles(n_lat, n_ctx)
    half = SHORT_CONV // 2

    def body(x_ref, w_ref, df_ref, db_ref, dx_ref, dw_ref, p_ref, q_ref):
        _conv7_fill(p_ref, x_ref, n_lat, n_ctx)
        z8 = jnp.zeros((8, LANES), F32)
        q_ref[0:8, :] = z8
        q_ref[8 + n_lat:16 + n_lat, :] = z8
        q_ref[16 + n_lat + n_ctx:24 + n_lat + n_ctx, :] = z8
        dw = [jnp.zeros((1, LANES), F32) for _ in range(SHORT_CONV)]
        for (row, prow) in tiles:
            pre = jnp.zeros((tt, LANES), F32)
            for kk in range(SHORT_CONV):
                pre = pre + w_ref[kk:kk + 1, :] * p_ref[pl.ds(prow + kk - half, tt), :]
            s = jax.nn.sigmoid(pre)
            dpre = (df_ref[pl.ds(row, tt), :] + db_ref[pl.ds(row, tt), :]) * (s * (1.0 + pre * (1.0 - s)))
            q_ref[pl.ds(prow, tt), :] = dpre
            for kk in range(SHORT_CONV):
                dw[kk] = dw[kk] + jnp.sum(dpre * p_ref[pl.ds(prow + kk - half, tt), :], axis=0, keepdims=True)
        for (row, prow) in tiles:
            acc = jnp.zeros((tt, LANES), F32)
            for kk in range(SHORT_CONV):
                acc = acc + w_ref[kk:kk + 1, :] * q_ref[pl.ds(prow + half - kk, tt), :]
            dx_ref[pl.ds(row, tt), :] = acc.astype(dx_ref.dtype)
        for kk in range(SHORT_CONV):
            dw_ref[kk:kk + 1, :] = dw[kk]

    dspec = pl.BlockSpec((None, None, nt, LANES), lambda j: (j // nh, j % nh, 0, 0))
    return pl.pallas_call(
        body, grid=(3 * nh,),
        in_specs=[pl.BlockSpec((nt, LANES), lambda j: (0, j)), pl.BlockSpec((SHORT_CONV, LANES), lambda j: (0, j)), dspec, dspec],
        out_specs=[pl.BlockSpec((nt, LANES), lambda j: (0, j)), pl.BlockSpec((SHORT_CONV, LANES), lambda j: (0, j))],
        out_shape=[jax.ShapeDtypeStruct((nt, 3 * nh * LANES), MXU_DTYPE), jax.ShapeDtypeStruct((SHORT_CONV, 3 * nh * LANES), F32)],
        scratch_shapes=[pltpu.VMEM((nt + 24, LANES), F32), pltpu.VMEM((nt + 24, LANES), F32)], name="conv7_bwd",
        compiler_params=_cparams(("arbitrary",), 10 * _nbytes((nt + 24, LANES), F32)),
    )(h, w_conv, dact_f, dact_b)


_RT = 4
_CP = CONF_K // 2


def _conf_pad_shape(gw, vertical):
    return (gw + 2 * _CP, gw, LANES) if vertical else (gw, gw + 32, LANES)


def _conf_fill(p_ref, val, gw, vertical):
    if vertical:
        p_ref[0:_CP] = jnp.zeros((_CP, gw, LANES), F32)
        p_ref[_CP + gw:2 * _CP + gw] = jnp.zeros((_CP, gw, LANES), F32)
        p_ref[_CP:_CP + gw] = val
    else:
        p_ref[:, 0:16, :] = jnp.zeros((gw, 16, LANES), F32)
        p_ref[:, 16 + gw:32 + gw, :] = jnp.zeros((gw, 16, LANES), F32)
        p_ref[:, 16:16 + gw, :] = val


def _conf_window(p_ref, r0, shift, gw, vertical):
    if vertical:
        return p_ref[pl.ds(r0 + _CP + shift, _RT), :, :]
    return p_ref[pl.ds(r0, _RT), pl.ds(16 + shift, gw), :]


def _conf_conv_fwd(h3, w_dw, *, gw, col0, half, vertical):
    cb_val = (col0 + (half if vertical else 0)) // LANES
    cb_gate = cb_val + 2 * half // LANES
    cb_w = (half if vertical else 0) // LANES

    def body(v_ref, g_ref, w_ref, o_ref, p_ref):
        _conf_fill(p_ref, v_ref[...] * jax.nn.sigmoid(g_ref[...]), gw, vertical)

        def step(t, carry):
            r0 = t * _RT
            acc = jnp.zeros((_RT, gw, LANES), F32)
            for kk in range(CONF_K):
                acc = acc + w_ref[kk:kk + 1, :] * _conf_window(p_ref, r0, kk - _CP, gw, vertical)
            o_ref[pl.ds(r0, _RT)] = acc
            return carry

        lax.fori_loop(0, gw // _RT, step, 0)

    blk = (gw, gw, LANES)
    return pl.pallas_call(
        body, grid=(half // LANES,),
        in_specs=[pl.BlockSpec(blk, lambda j: (0, 0, cb_val + j)), pl.BlockSpec(blk, lambda j: (0, 0, cb_gate + j)),
                  pl.BlockSpec((CONF_K, LANES), lambda j: (0, cb_w + j))],
        out_specs=pl.BlockSpec(blk, lambda j: (0, 0, j)),
        out_shape=jax.ShapeDtypeStruct((gw, gw, half), F32),
        scratch_shapes=[pltpu.VMEM(_conf_pad_shape(gw, vertical), F32)],
        name="conf_conv_fwd_v" if vertical else "conf_conv_fwd_h",
        compiler_params=_cparams(("arbitrary",), 8 * _nbytes(_conf_pad_shape(gw, vertical), F32)),
    )(h3, h3, w_dw)


def _conf_conv_bwd(h3, w_dw, dyc, *, gw, col0, half, vertical):
    cb_val = (col0 + (half if vertical else 0)) // LANES
    cb_gate = cb_val + 2 * half // LANES
    cb_w = (half if vertical else 0) // LANES

    def body(v_ref, g_ref, w_ref, d_ref, dv_ref, dg_ref, dw_ref, py_ref, pd_ref):
        _conf_fill(py_ref, v_ref[...] * jax.nn.sigmoid(g_ref[...]), gw, vertical)
        _conf_fill(pd_ref, d_ref[...], gw, vertical)

        def step(t, carry):
            r0 = t * _RT
            acc = jnp.zeros((_RT, gw, LANES), F32)
            for kk in range(CONF_K):
                acc = acc + w_ref[kk:kk + 1, :] * _conf_window(pd_ref, r0, _CP - kk, gw, vertical)
            val = v_ref[pl.ds(r0, _RT)]
            sg = jax.nn.sigmoid(g_ref[pl.ds(r0, _RT)])
            dv_ref[pl.ds(r0, _RT)] = (acc * sg).astype(dv_ref.dtype)
            dg_ref[pl.ds(r0, _RT)] = (acc * val * sg * (1.0 - sg)).astype(dg_ref.dtype)
            return carry

        lax.fori_loop(0, gw // _RT, step, 0)

        for kk in range(CONF_K):
            def wstep(t, acc, kk=kk):
                r0 = t * _RT
                prod = d_ref[pl.ds(r0, _RT)] * _conf_window(py_ref, r0, kk - _CP, gw, vertical)
                return acc + jnp.sum(prod, axis=0)

            tot = lax.fori_loop(0, gw // _RT, wstep, jnp.zeros((gw, LANES), F32))
            dw_ref[kk:kk + 1, :] = jnp.sum(tot, axis=0, keepdims=True)

    blk = (gw, gw, LANES)
    pshape = _conf_pad_shape(gw, vertical)
    return pl.pallas_call(
        body, grid=(half // LANES,),
        in_specs=[pl.BlockSpec(blk, lambda j: (0, 0, cb_val + j)), pl.BlockSpec(blk, lambda j: (0, 0, cb_gate + j)),
                  pl.BlockSpec((CONF_K, LANES), lambda j: (0, cb_w + j)), pl.BlockSpec(blk, lambda j: (0, 0, j))],
        out_specs=[pl.BlockSpec(blk, lambda j: (0, 0, j)), pl.BlockSpec(blk, lambda j: (0, 0, j)),
                   pl.BlockSpec((CONF_K, LANES), lambda j: (0, j))],
        out_shape=[jax.ShapeDtypeStruct((gw, gw, half), MXU_DTYPE), jax.ShapeDtypeStruct((gw, gw, half), MXU_DTYPE),
                   jax.ShapeDtypeStruct((CONF_K, half), F32)],
        scratch_shapes=[pltpu.VMEM(pshape, F32), pltpu.VMEM(pshape, F32)],
        name="conf_conv_bwd_v" if vertical else "conf_conv_bwd_h",
        compiler_params=_cparams(("arbitrary",), 12 * _nbytes(pshape, F32)),
    )(h3, h3, w_dw, dyc)


_INV_BASE = 8


def _tri_inv_raw(mats):
    c = mats[0].shape[0]
    ri = lax.broadcasted_iota(jnp.int32, (c, c), 0)
    ci = lax.broadcasted_iota(jnp.int32, (c, c), 1)
    eye = (ri == ci).astype(F32)
    base = min(_INV_BASE, c)
    same = lambda size: (ri // size) == (ci // size)
    dmat = [jnp.where(same(base), a, 0.0) for a in mats]
    x = [eye - dm for dm in dmat]
    pw = [_raw_dot(dm, dm, "nn", "hi3") for dm in dmat]
    span = 2
    while span < base:
        x = [xi + _raw_dot(xi, p, "nn", "hi3") for xi, p in zip(x, pw)]
        span *= 2
        if span < base:
            pw = [_raw_dot(p, p, "nn", "hi3") for p in pw]
    size = base
    while size < c:
        off = jnp.logical_and(jnp.logical_not(same(size)), same(2 * size))
        t = [_raw_dot(xi, jnp.where(off, a, 0.0), "nn", "hi3") for xi, a in zip(x, mats)]
        x = [xi - _raw_dot(ti, xi, "nn", "hi3") for xi, ti in zip(x, t)]
        size *= 2
    return x


@jax.custom_vjp
def _tri_inv(mats):
    return _tri_inv_raw(mats)


def _tri_inv_fwd(mats):
    x = _tri_inv_raw(mats)
    return x, x


def _tri_inv_bwd(x, dx):
    t = [_raw_dot(xi, di, "tn", "hi3") for xi, di in zip(x, dx)]
    return ([-_raw_dot(ti, xi, "nt", "hi3") for ti, xi in zip(t, x)],)


_tri_inv.defvjp(_tri_inv_fwd, _tri_inv_bwd)


@jax.custom_vjp
def _tri_inv_saved(mats, saved):
    return saved


def _tri_inv_saved_fwd(mats, saved):
    return saved, saved


def _tri_inv_saved_bwd(x, dx):
    return _tri_inv_bwd(x, dx)[0], [jnp.zeros_like(xi) for xi in x]


_tri_inv_saved.defvjp(_tri_inv_saved_fwd, _tri_inv_saved_bwd)


def _delta_chunk(qa, ka, va, graw, braw, alog, dtb, s, *, rev, minv_saved=None, with_minv=False):
    c = qa[0].shape[0]

    def each(f, *lists):
        return [f(*xs) for xs in zip(*lists, strict=True)]

    def l2n(t):
        return t * lax.rsqrt(jnp.sum(t * t, axis=-1, keepdims=True) + 1e-6)

    ri = lax.broadcasted_iota(jnp.int32, (c, c), 0)
    ci = lax.broadcasted_iota(jnp.int32, (c, c), 1)
    incl = (ci >= ri) if rev else (ci <= ri)
    strict = (ci > ri) if rev else (ci < ri)
    tmat = incl.astype(F32)
    tmat_t = ((ri >= ci) if rev else (ri <= ci)).astype(F32)

    q = each(lambda t: l2n(t) * (HD ** -0.5), qa)
    k = each(l2n, ka)
    g = each(lambda al, gr, dt: -jnp.exp(al) * _softplus(gr + dt), alog, graw, dtb)
    beta = each(jax.nn.sigmoid, braw)
    gc_wide = each(lambda t: _xdot(tmat, jnp.broadcast_to(t, (c, HD)), "nn", "a"), g)
    gc_rows = gc_wide if c == HD else each(lambda t: _xdot(tmat, jnp.broadcast_to(t, (c, c)), "nn", "a"), g)
    gc_cols = each(lambda t: _xdot(jnp.broadcast_to(t, (c, c)), tmat_t, "tn", "b"), g)
    gamma = each(lambda r, cc: jnp.where(incl, jnp.exp(jnp.where(incl, r - cc, 0.0)), 0.0), gc_rows, gc_cols)
    kb = each(lambda a, b: a * b, k, beta)
    a_mat = each(lambda a, b, gm: jnp.where(strict, _dot(a, b, "nt", "mxu") * gm, 0.0), kb, k, gamma)
    minv = _tri_inv(a_mat) if minv_saved is None else _tri_inv_saved(a_mat, minv_saved)
    eg = each(jnp.exp, gc_wide)
    u = each(lambda mi, v, b: _dot(mi, v * b, "nn", "hi3"), minv, va, beta)
    w = each(lambda mi, a, e: _dot(mi, a * e, "nn", "hi3"), minv, kb, eg)
    attn = each(lambda a, b, gm: _dot(a, b, "nt", "mxu") * gm, q, k, gamma)
    tot = each(lambda t: jnp.sum(t, axis=0, keepdims=True), g)
    q_dec = each(lambda a, e: a * e, q, eg)
    k_dec = each(lambda a, t, gw_: a * jnp.exp(t - gw_), k, tot, gc_wide)
    v_new = each(lambda uu, ww, ss: uu - _dot(ww, ss, "nn", "mxu"), u, w, s)
    o = each(lambda qd, ss, at, vn: _dot(qd, ss, "nn", "mxu") + _dot(at, vn, "nn", "mxu"), q_dec, s, attn, v_new)
    s_new = each(lambda ss, t, kd, vn: ss * jnp.exp(t) + _dot(kd, vn, "tn", "mxu"), s, tot, k_dec, v_new)
    return (o, s_new, minv) if with_minv else (o, s_new)


def _delta_args(a_ref, g_ref, al_ref, dt_ref, s_ref, hb, gcol):
    gts = [g_ref[hh] for hh in range(hb)]
    return ([a_ref[0, hh] for hh in range(hb)], [a_ref[1, hh] for hh in range(hb)], [a_ref[2, hh] for hh in range(hb)],
            [t[:, gcol:gcol + 1] for t in gts], [t[:, gcol + 1:gcol + 2] for t in gts],
            [al_ref[hh, 0:1, 0:1] for hh in range(hb)], [dt_ref[hh, 0:1, 0:1] for hh in range(hb)],
            [s_ref[hh] for hh in range(hb)])


def _delta_chunk_of_step(step, nch, nlc, rev):
    return (nch - 1 - step) if rev else (step + nlc) % nch


def _delta_fwd(act, gates_hm, alog_b, dtb_b, *, n_lat, n_ctx, nh, hb, rev):
    nt = n_lat + n_ctx
    nch, nlc = nt // CHUNK, n_lat // CHUNK
    gcol = 2 if rev else 0
    chunk = functools.partial(_delta_chunk_of_step, nch=nch, nlc=nlc, rev=rev)

    def body(a_ref, g_ref, al_ref, dt_ref, o_ref, st_ref, mi_ref, s_ref):
        n = pl.program_id(1)

        @pl.when(n == 0)
        def _():
            s_ref[...] = jnp.zeros_like(s_ref)

        args = _delta_args(a_ref, g_ref, al_ref, dt_ref, s_ref, hb, gcol)
        o, s_new, minv = _delta_chunk(*args, rev=rev, with_minv=True)
        for hh in range(hb):
            st_ref[hh] = args[7][hh]
            mi_ref[hh] = minv[hh]
            o_ref[:, hh * HD:(hh + 1) * HD] = o[hh]
            s_ref[hh] = s_new[hh]

    par = pl.BlockSpec((hb, 8, LANES), lambda h, n: (h, 0, 0))
    return pl.pallas_call(
        body, grid=(nh // hb, nch),
        in_specs=[pl.BlockSpec((3, hb, CHUNK, HD), lambda h, n: (0, h, chunk(n), 0)),
                  pl.BlockSpec((hb, CHUNK, 4), lambda h, n: (h, chunk(n), 0)), par, par],
        out_specs=[pl.BlockSpec((CHUNK, hb * HD), lambda h, n: (chunk(n), h)),
                   pl.BlockSpec((hb, None, HD, HD), lambda h, n: (h, n, 0, 0)),
                   pl.BlockSpec((hb, None, CHUNK, CHUNK), lambda h, n: (h, n, 0, 0))],
        out_shape=[jax.ShapeDtypeStruct((nt, nh * HD), F32), jax.ShapeDtypeStruct((nh, nch, HD, HD), F32),
                   jax.ShapeDtypeStruct((nh, nch, CHUNK, CHUNK), F32)],
        scratch_shapes=[pltpu.VMEM((hb, HD, HD), F32)], name="delta_fwd_rev" if rev else "delta_fwd",
        compiler_params=_cparams(("arbitrary", "arbitrary"), 0),
    )(act, gates_hm, alog_b, dtb_b)


def _delta_bwd(act, gates_hm, alog_b, dtb_b, states, minvs, do, *, n_lat, n_ctx, nh, hb, rev):
    nt = n_lat + n_ctx
    nch, nlc = nt // CHUNK, n_lat // CHUNK
    gcol = 2 if rev else 0

    def chunk(m):
        return _delta_chunk_of_step(nch - 1 - m, nch, nlc, rev)

    def body(a_ref, g_ref, al_ref, dt_ref, st_ref, mi_ref, do_ref, da_ref, dg_ref, dal_ref, ddt_ref, ds_ref):
        m = pl.program_id(1)

        @pl.when(m == 0)
        def _():
            ds_ref[...] = jnp.zeros_like(ds_ref)
            dal_ref[...] = jnp.zeros_like(dal_ref)
            ddt_ref[...] = jnp.zeros_like(ddt_ref)

        is_lat = chunk(m) < nlc
        args = _delta_args(a_ref, g_ref, al_ref, dt_ref, st_ref, hb, gcol)
        _, vjp = jax.vjp(functools.partial(_delta_chunk, rev=rev, minv_saved=[mi_ref[hh] for hh in range(hb)]), *args)
        do_h = [jnp.where(is_lat, do_ref[:, hh * HD:(hh + 1) * HD], 0.0) for hh in range(hb)]
        grads = vjp((do_h, [ds_ref[hh] for hh in range(hb)]))
        for hh in range(hb):
            dq, dk, dv, dgr, dbr, dal, ddt, ds = [gr[hh] for gr in grads]
            da_ref[0, hh] = dq
            da_ref[1, hh] = dk
            da_ref[2, hh] = dv
            lane = lax.broadcasted_iota(jnp.int32, (CHUNK, 4), 1)
            dg_ref[hh] = jnp.where(lane == 0, dgr, jnp.where(lane == 1, dbr, 0.0))
            dal_ref[hh] += jnp.broadcast_to(dal, (8, LANES))
            ddt_ref[hh] += jnp.broadcast_to(ddt, (8, LANES))
            ds_ref[hh] = ds

    par = pl.BlockSpec((hb, 8, LANES), lambda h, m: (h, 0, 0))
    return pl.pallas_call(
        body, grid=(nh // hb, nch),
        in_specs=[pl.BlockSpec((3, hb, CHUNK, HD), lambda h, m: (0, h, chunk(m), 0)),
                  pl.BlockSpec((hb, CHUNK, 4), lambda h, m: (h, chunk(m), 0)), par, par,
                  pl.BlockSpec((hb, None, HD, HD), lambda h, m: (h, nch - 1 - m, 0, 0)),
                  pl.BlockSpec((hb, None, CHUNK, CHUNK), lambda h, m: (h, nch - 1 - m, 0, 0)),
                  pl.BlockSpec((CHUNK, hb * HD), lambda h, m: (jnp.minimum(chunk(m), nlc - 1), h))],
        out_specs=[pl.BlockSpec((3, hb, CHUNK, HD), lambda h, m: (0, h, chunk(m), 0)),
                   pl.BlockSpec((hb, CHUNK, 4), lambda h, m: (h, chunk(m), 0)), par, par],
        out_shape=[jax.ShapeDtypeStruct((3, nh, nt, HD), F32), jax.ShapeDtypeStruct((nh, nt, 4), F32),
                   jax.ShapeDtypeStruct((nh, 8, LANES), F32), jax.ShapeDtypeStruct((nh, 8, LANES), F32)],
        scratch_shapes=[pltpu.VMEM((hb, HD, HD), F32)], name="delta_bwd_rev" if rev else "delta_bwd",
        compiler_params=_cparams(("arbitrary", "arbitrary"), 0),
    )(act, gates_hm, alog_b, dtb_b, states, minvs, do)


def _my_pos():
    return lax.axis_index("x"), lax.axis_index("y"), lax.axis_index("c")


def _flip(v, d):
    return 1 - v if d else v


def _bcast8(block, name):
    r, w = block.shape
    assert r % 8 == 0 and w % LANES == 0

    def body(x_ref, o_ref, send_sems, recv_sems, local_sem):
        x, y, c = _my_pos()

        def rows(px, py, pc):
            return o_ref.at[pl.ds((4 * px + 2 * py + pc) * r, r), :]

        mine = pltpu.make_async_copy(x_ref, rows(x, y, c), local_sem)
        mine.start()
        copies = []
        for kk in range(1, 8):
            dx, dy, dc = (kk >> 2) & 1, (kk >> 1) & 1, kk & 1
            peer = (_flip(x, dx), _flip(y, dy), _flip(c, dc))
            cp = pltpu.make_async_remote_copy(src_ref=x_ref, dst_ref=rows(x, y, c), send_sem=send_sems.at[kk - 1],
                                              recv_sem=recv_sems.at[kk - 1], device_id=peer, device_id_type=pl.DeviceIdType.MESH)
            cp.start()
            copies.append((cp, peer))
        for kk, (cp, peer) in enumerate(copies):
            pltpu.make_async_remote_copy(src_ref=x_ref, dst_ref=rows(*peer), send_sem=send_sems.at[kk], recv_sem=recv_sems.at[kk],
                                         device_id=peer, device_id_type=pl.DeviceIdType.MESH).wait_recv()
        for cp, _ in copies:
            cp.wait_send()
        mine.wait()

    return pl.pallas_call(
        body, out_shape=jax.ShapeDtypeStruct((8 * r, w), block.dtype),
        in_specs=[pl.BlockSpec(memory_space=pltpu.VMEM)], out_specs=pl.BlockSpec(memory_space=pltpu.VMEM),
        scratch_shapes=[pltpu.SemaphoreType.DMA((7,)), pltpu.SemaphoreType.DMA((7,)), pltpu.SemaphoreType.DMA], name=name,
        compiler_params=pltpu.CompilerParams(vmem_limit_bytes=int(max(32 * 1024 * 1024, 12 * _nbytes((r, w), block.dtype)))),
    )(block)


_CHIP_PEERS = ((1, 0), (0, 1), (1, 1))


def _col_half(ref, c, lead):
    hc = ref.shape[-1] // 2
    return ref.at[(*lead, slice(None), pl.ds(pl.multiple_of(c * hc, LANES), hc))]


def _into_slot(w, chip_idx, after, name):
    r, cc = w.shape
    tc = _pick(cc, 512)

    def body(i_ref, w_ref, after_ref, o_ref):
        o_ref[...] = w_ref[...].astype(o_ref.dtype)

    grid_spec = pltpu.PrefetchScalarGridSpec(
        num_scalar_prefetch=1, grid=(cc // tc,), in_specs=[pl.BlockSpec((r, tc), lambda j, s: (0, j)), ANY],
        out_specs=pl.BlockSpec((None, r, tc), lambda j, s: (s[0], 0, j)))
    return pl.pallas_call(body, grid_spec=grid_spec, out_shape=jax.ShapeDtypeStruct((4, r, cc), MXU_DTYPE), name=name,
                          compiler_params=_cparams(("arbitrary",), 6 * _nbytes((r, tc), F32)))(chip_idx, w, after)


_HBM = pl.BlockSpec(memory_space=pltpu.HBM)
_SEM = pl.BlockSpec(memory_space=pltpu.SEMAPHORE)
_DATAFLOW = pltpu.SideEffectType.DATAFLOW_SIDE_EFFECTING


def _in_hbm(a):
    return pltpu.with_memory_space_constraint(a, pltpu.HBM)


def _gather_start(bufs, after, name):
    n = len(bufs)

    def body(*refs):
        ins, send_sems, recv_sems, token = refs[:n], refs[n + 1], refs[n + 2], refs[-1]
        x, y, c = _my_pos()
        for a in range(n):
            piece = _col_half(ins[a], c, (2 * x + y,))
            for kk, (dx, dy) in enumerate(_CHIP_PEERS):
                pltpu.make_async_remote_copy(src_ref=piece, dst_ref=piece, send_sem=send_sems.at[3 * a + kk], recv_sem=recv_sems.at[3 * a + kk],
                                             device_id=(_flip(x, dx), _flip(y, dy), c), device_id_type=pl.DeviceIdType.MESH).start()
        token[...] = jnp.zeros_like(token)

    res = pl.pallas_call(
        body, name=name,
        out_shape=(pltpu.SemaphoreType.DMA((3 * n,)), pltpu.SemaphoreType.DMA((3 * n,)), *[pltpu.HBM(b.shape, b.dtype) for b in bufs],
                   jax.ShapeDtypeStruct((8, LANES), F32)),
        in_specs=[_HBM] * n + [ANY], out_specs=(_SEM, _SEM, *[_HBM] * n, pl.BlockSpec(memory_space=pltpu.VMEM)),
        input_output_aliases={a: 2 + a for a in range(n)}, compiler_params=pltpu.CompilerParams(has_side_effects=_DATAFLOW),
    )(*[_in_hbm(b) for b in bufs], after)
    return res[0], res[1], list(res[2:2 + n]), res[-1]


def _gather_wait(send_sems, recv_sems, bufs, after, name):
    n = len(bufs)

    def body(*refs):
        ins, s_sems, r_sems = refs[:n], refs[n], refs[n + 1]
        x, y, c = _my_pos()
        for a in range(n):
            for kk, (dx, dy) in enumerate(_CHIP_PEERS):
                px, py = _flip(x, dx), _flip(y, dy)
                cp = pltpu.make_async_remote_copy(src_ref=_col_half(ins[a], c, (2 * x + y,)), dst_ref=_col_half(ins[a], c, (2 * px + py,)),
                                                  send_sem=s_sems.at[3 * a + kk], recv_sem=r_sems.at[3 * a + kk], device_id=(px, py, c),
                                                  device_id_type=pl.DeviceIdType.MESH)
                cp.wait_send()
                cp.wait_recv()

    return pl.pallas_call(
        body, name=name, out_shape=[pltpu.HBM(b.shape, b.dtype) for b in bufs],
        in_specs=[_HBM] * n + [_SEM, _SEM] + [ANY] * len(after), out_specs=[_HBM] * n, input_output_aliases={a: a for a in range(n)},
        compiler_params=pltpu.CompilerParams(has_side_effects=_DATAFLOW),
    )(*bufs, send_sems, recv_sems, *after)


def _gather_forward(bufs, name):
    n = len(bufs)

    def body(*refs):
        outs = refs[n:2 * n]
        send_sems, recv_sems = refs[2 * n:]
        x, y, c = _my_pos()
        sib = (x, y, 1 - c)

        def remote(a, kk, half):
            dx, dy = _CHIP_PEERS[kk]
            piece = _col_half(outs[a], half, (2 * _flip(x, dx) + _flip(y, dy),))
            return pltpu.make_async_remote_copy(src_ref=piece, dst_ref=piece, send_sem=send_sems.at[3 * a + kk], recv_sem=recv_sems.at[3 * a + kk],
                                                device_id=sib, device_id_type=pl.DeviceIdType.MESH)

        sends = [remote(a, kk, c) for a in range(n) for kk in range(3)]
        for cp in sends:
            cp.start()
        for a in range(n):
            for kk in range(3):
                remote(a, kk, 1 - c).wait_recv()
        for cp in sends:
            cp.wait_send()

    return pl.pallas_call(
        body, out_shape=[jax.ShapeDtypeStruct(b.shape, b.dtype) for b in bufs],
        in_specs=[ANY] * n, out_specs=[ANY] * n, input_output_aliases={a: a for a in range(n)},
        scratch_shapes=[pltpu.SemaphoreType.DMA((3 * n,))] * 2, name=name,
    )(*bufs)


def _forward_start(bufs, after, name):
    n = len(bufs)

    def body(*refs):
        ins, send_sems, recv_sems, token = refs[:n], refs[n + 1], refs[n + 2], refs[-1]
        x, y, c = _my_pos()
        for a in range(n):
            for kk, (dx, dy) in enumerate(_CHIP_PEERS):
                piece = _col_half(ins[a], c, (2 * _flip(x, dx) + _flip(y, dy),))
                pltpu.make_async_remote_copy(src_ref=piece, dst_ref=piece, send_sem=send_sems.at[3 * a + kk], recv_sem=recv_sems.at[3 * a + kk],
                                             device_id=(x, y, 1 - c), device_id_type=pl.DeviceIdType.MESH).start()
        token[...] = jnp.zeros_like(token)

    res = pl.pallas_call(
        body, name=name,
        out_shape=(pltpu.SemaphoreType.DMA((3 * n,)), pltpu.SemaphoreType.DMA((3 * n,)), *[pltpu.HBM(b.shape, b.dtype) for b in bufs],
                   jax.ShapeDtypeStruct((8, LANES), F32)),
        in_specs=[_HBM] * n + [ANY], out_specs=(_SEM, _SEM, *[_HBM] * n, pl.BlockSpec(memory_space=pltpu.VMEM)),
        input_output_aliases={a: 2 + a for a in range(n)}, compiler_params=pltpu.CompilerParams(has_side_effects=_DATAFLOW),
    )(*[_in_hbm(b) for b in bufs], after)
    return res[0], res[1], list(res[2:2 + n]), res[-1]


def _forward_wait(send_sems, recv_sems, bufs, after, name):
    n = len(bufs)

    def body(*refs):
        ins, s_sems, r_sems = refs[:n], refs[n], refs[n + 1]
        x, y, c = _my_pos()
        for a in range(n):
            for kk, (dx, dy) in enumerate(_CHIP_PEERS):
                slot = 2 * _flip(x, dx) + _flip(y, dy)
                cp = pltpu.make_async_remote_copy(src_ref=_col_half(ins[a], c, (slot,)), dst_ref=_col_half(ins[a], 1 - c, (slot,)),
                                                  send_sem=s_sems.at[3 * a + kk], recv_sem=r_sems.at[3 * a + kk], device_id=(x, y, 1 - c),
                                                  device_id_type=pl.DeviceIdType.MESH)
                cp.wait_send()
                cp.wait_recv()

    return pl.pallas_call(
        body, name=name, out_shape=[pltpu.HBM(b.shape, b.dtype) for b in bufs],
        in_specs=[_HBM] * n + [_SEM, _SEM] + [ANY] * len(after), out_specs=[_HBM] * n, input_output_aliases={a: a for a in range(n)},
        compiler_params=pltpu.CompilerParams(has_side_effects=_DATAFLOW),
    )(*bufs, send_sems, recv_sems, *after)


def _scatter_start(parts, after, name):
    n = len(parts)

    def body(*refs):
        ins, lands, send_sems, recv_sems, token = refs[:n], refs[n:2 * n], refs[2 * n + 1], refs[2 * n + 2], refs[-1]
        x, y, c = _my_pos()
        for a in range(n):
            for kk, (dx, dy) in enumerate(_CHIP_PEERS):
                px, py = _flip(x, dx), _flip(y, dy)
                pltpu.make_async_remote_copy(src_ref=ins[a].at[2 * px + py], dst_ref=lands[a].at[2 * x + y], send_sem=send_sems.at[3 * a + kk],
                                             recv_sem=recv_sems.at[3 * a + kk], device_id=(px, py, c), device_id_type=pl.DeviceIdType.MESH).start()
        token[...] = jnp.zeros_like(token)

    thru = [pltpu.HBM(p.shape, p.dtype) for p in parts]
    res = pl.pallas_call(
        body, name=name,
        out_shape=(pltpu.SemaphoreType.DMA((3 * n,)), pltpu.SemaphoreType.DMA((3 * n,)), *thru, *thru, jax.ShapeDtypeStruct((8, LANES), F32)),
        in_specs=[_HBM] * (2 * n) + [ANY], out_specs=(_SEM, _SEM, *[_HBM] * (2 * n), pl.BlockSpec(memory_space=pltpu.VMEM)),
        input_output_aliases={a: 2 + a for a in range(2 * n)}, compiler_params=pltpu.CompilerParams(has_side_effects=_DATAFLOW),
    )(*[_in_hbm(p) for p in parts], *[_in_hbm(lax.empty(p.shape, p.dtype)) for p in parts], after)
    return res[0], res[1], list(res[2:2 + n]), list(res[2 + n:2 + 2 * n]), res[-1]


def _scatter_wait(send_sems, recv_sems, parts, lands, after, name):
    n = len(parts)

    def body(*refs):
        ins, lnd, s_sems, r_sems = refs[:n], refs[n:2 * n], refs[2 * n], refs[2 * n + 1]
        x, y, c = _my_pos()
        for a in range(n):
            for kk, (dx, dy) in enumerate(_CHIP_PEERS):
                px, py = _flip(x, dx), _flip(y, dy)
                cp = pltpu.make_async_remote_copy(src_ref=ins[a].at[2 * px + py], dst_ref=lnd[a].at[2 * px + py], send_sem=s_sems.at[3 * a + kk],
                                                  recv_sem=r_sems.at[3 * a + kk], device_id=(px, py, c), device_id_type=pl.DeviceIdType.MESH)
                cp.wait_send()
                cp.wait_recv()

    thru = [pltpu.HBM(p.shape, p.dtype) for p in parts]
    res = pl.pallas_call(
        body, name=name, out_shape=[*thru, *thru],
        in_specs=[_HBM] * (2 * n) + [_SEM, _SEM] + [ANY] * len(after), out_specs=[_HBM] * (2 * n),
        input_output_aliases={a: a for a in range(2 * n)}, compiler_params=pltpu.CompilerParams(has_side_effects=_DATAFLOW),
    )(*parts, *lands, send_sems, recv_sems, *after)
    return list(res[:n]), list(res[n:])


def _pair_send_start(grads, after, name):
    n = len(grads)

    def body(*refs):
        ins, lands, send_sems, recv_sems, token = refs[:n], refs[n:2 * n], refs[2 * n + 1], refs[2 * n + 2], refs[-1]
        x, y, c = _my_pos()
        for a in range(n):
            pltpu.make_async_remote_copy(src_ref=_col_half(ins[a], 1 - c, (slice(None),)), dst_ref=lands[a], send_sem=send_sems.at[a],
                                         recv_sem=recv_sems.at[a], device_id=(x, y, 1 - c), device_id_type=pl.DeviceIdType.MESH).start()
        token[...] = jnp.zeros_like(token)

    land_shapes = [(4, g.shape[1], g.shape[2] // 2) for g in grads]
    res = pl.pallas_call(
        body, name=name,
        out_shape=(pltpu.SemaphoreType.DMA((n,)), pltpu.SemaphoreType.DMA((n,)), *[pltpu.HBM(g.shape, g.dtype) for g in grads],
                   *[pltpu.HBM(s, g.dtype) for s, g in zip(land_shapes, grads)], jax.ShapeDtypeStruct((8, LANES), F32)),
        in_specs=[_HBM] * (2 * n) + [ANY], out_specs=(_SEM, _SEM, *[_HBM] * (2 * n), pl.BlockSpec(memory_space=pltpu.VMEM)),
        input_output_aliases={a: 2 + a for a in range(2 * n)}, compiler_params=pltpu.CompilerParams(has_side_effects=_DATAFLOW),
    )(*[_in_hbm(g) for g in grads], *[_in_hbm(lax.empty(s, g.dtype)) for s, g in zip(land_shapes, grads)], after)
    return res[0], res[1], list(res[2:2 + n]), list(res[2 + n:2 + 2 * n]), res[-1]


def _pair_send_wait(send_sems, recv_sems, grads, lands, after, name):
    n = len(grads)

    def body(*refs):
        ins, lnd, s_sems, r_sems = refs[:n], refs[n:2 * n], refs[2 * n], refs[2 * n + 1]
        x, y, c = _my_pos()
        for a in range(n):
            cp = pltpu.make_async_remote_copy(src_ref=_col_half(ins[a], 1 - c, (slice(None),)), dst_ref=lnd[a], send_sem=s_sems.at[a],
                                              recv_sem=r_sems.at[a], device_id=(x, y, 1 - c), device_id_type=pl.DeviceIdType.MESH)
            cp.wait_send()
            cp.wait_recv()

    res = pl.pallas_call(
        body, name=name, out_shape=[*[pltpu.HBM(g.shape, g.dtype) for g in grads], *[pltpu.HBM(l.shape, l.dtype) for l in lands]],
        in_specs=[_HBM] * (2 * n) + [_SEM, _SEM, ANY], out_specs=[_HBM] * (2 * n), input_output_aliases={a: a for a in range(2 * n)},
        compiler_params=pltpu.CompilerParams(has_side_effects=_DATAFLOW),
    )(*grads, *lands, send_sems, recv_sems, after)
    return list(res[:n]), list(res[n:])


def _pair_join_halves(bufs, name):
    n = len(bufs)

    def body(*refs):
        outs = refs[n:2 * n]
        send_sems, recv_sems = refs[2 * n:]
        x, y, c = _my_pos()
        sib = (x, y, 1 - c)
        sends = []
        for a in range(n):
            mine = _col_half(outs[a], c, ())
            cp = pltpu.make_async_remote_copy(src_ref=mine, dst_ref=mine, send_sem=send_sems.at[a], recv_sem=recv_sems.at[a],
                                              device_id=sib, device_id_type=pl.DeviceIdType.MESH)
            cp.start()
            sends.append(cp)
        for a in range(n):
            other = _col_half(outs[a], 1 - c, ())
            pltpu.make_async_remote_copy(src_ref=other, dst_ref=other, send_sem=send_sems.at[a], recv_sem=recv_sems.at[a],
                                         device_id=sib, device_id_type=pl.DeviceIdType.MESH).wait_recv()
        for cp in sends:
            cp.wait_send()

    return pl.pallas_call(
        body, out_shape=[jax.ShapeDtypeStruct(b.shape, b.dtype) for b in bufs],
        in_specs=[ANY] * n, out_specs=[ANY] * n, input_output_aliases={a: a for a in range(n)},
        scratch_shapes=[pltpu.SemaphoreType.DMA((n,)), pltpu.SemaphoreType.DMA((n,))], name=name,
    )(*bufs)


def _add_my_half(g, recv, c_idx, name):
    _, r, cc = g.shape
    hc = cc // 2
    tc = _pick(hc, 256)
    per = hc // tc

    def body(c_ref, g_ref, r_ref, o_ref):
        o_ref[...] = (g_ref[...] + r_ref[...]).astype(o_ref.dtype)

    grid_spec = pltpu.PrefetchScalarGridSpec(
        num_scalar_prefetch=1, grid=(4, per),
        in_specs=[pl.BlockSpec((None, r, tc), lambda s, j, c_ref: (s, 0, c_ref[0] * per + j)),
                  pl.BlockSpec((None, r, tc), lambda s, j, c_ref: (s, 0, j))],
        out_specs=pl.BlockSpec((None, r, tc), lambda s, j, c_ref: (s, 0, j)))
    return pl.pallas_call(body, grid_spec=grid_spec, out_shape=jax.ShapeDtypeStruct((4, r, hc), WIRE_DTYPE), name=name,
                          compiler_params=_cparams(("arbitrary", "arbitrary"), 6 * _nbytes((r, tc), F32)))(c_idx, g, recv)


def _sum_slots(own, recv, slots, name):
    _, r, hc = own.shape
    tc = _pick(hc, 256)
    per = hc // tc

    def body(s0, s1, s2, s3, s4, a_ref, b_ref, c_ref, d_ref, o_ref):
        f = lambda ref: ref[...].astype(F32)
        o_ref[...] = ((f(a_ref) + f(b_ref)) + f(c_ref)) + f(d_ref)

    slot = lambda i: pl.BlockSpec((None, r, tc), lambda j, *s: (s[i][0], 0, j))
    grid_spec = pltpu.PrefetchScalarGridSpec(
        num_scalar_prefetch=5, grid=(per,), in_specs=[slot(0), slot(1), slot(2), slot(3)],
        out_specs=pl.BlockSpec((r, tc), lambda j, *s: (0, s[4][0] * per + j)))
    return pl.pallas_call(body, grid_spec=grid_spec, out_shape=jax.ShapeDtypeStruct((r, 2 * hc), F32), name=name,
                          compiler_params=_cparams(("arbitrary",), 14 * _nbytes((r, tc), F32)))(*slots, own, recv, recv, recv)


def _adamw_math(w, g, m, v):
    m = ADAM_B1 * m + (1.0 - ADAM_B1) * g
    v = ADAM_B2 * v + (1.0 - ADAM_B2) * (g * g)
    m_hat = m / (1.0 - ADAM_B1 ** ADAM_STEP)
    v_hat = v / (1.0 - ADAM_B2 ** ADAM_STEP)
    delta = -ADAM_LR * (m_hat / (jnp.sqrt(v_hat) + ADAM_EPS) + ADAM_WD * w)
    return delta, m, v


def _adamw(w, g, m, v, name):
    r, cc = w.shape[-2:]
    block_elems = 512 * 1024
    tr = _pick(r, max(8, block_elems // max(cc, 1)), 8)
    if w.ndim == 2 and cc % LANES == 0 and tr * cc < block_elems // 2:
        tr, tc = r, _pick(cc, max(LANES, block_elems // r))
    else:
        tc = cc

    def body(w_ref, g_ref, m_ref, v_ref, d_ref, nm_ref, nv_ref):
        d, nm, nv = _adamw_math(w_ref[...], g_ref[...], m_ref[...], v_ref[...])
        d_ref[...] = d
        nm_ref[...] = nm
        nv_ref[...] = nv

    if w.ndim == 3:
        spec = pl.BlockSpec((None, tr, cc), lambda i: (0, i, 0))
    elif tc == cc:
        spec = pl.BlockSpec((tr, cc), lambda i: (i, 0))
    else:
        spec = pl.BlockSpec((r, tc), lambda j: (0, j))
    return pl.pallas_call(
        body, grid=((r // tr) * (cc // tc),), in_specs=[spec] * 4, out_specs=[spec] * 3,
        out_shape=[jax.ShapeDtypeStruct(w.shape, F32)] * 3, name=name,
        compiler_params=_cparams(("arbitrary",), 16 * _nbytes((tr, tc), F32)))(w, g, m, v)


def _sum8(allv, name):
    r8, w = allv.shape
    r = r8 // 8

    def body(a_ref, o_ref):
        acc = a_ref[0:r, :]
        for d in range(1, 8):
            acc = acc + a_ref[d * r:(d + 1) * r, :]
        o_ref[...] = acc

    return pl.pallas_call(body, out_shape=jax.ShapeDtypeStruct((r, w), F32), name=name,
                          compiler_params=_cparams((), 12 * _nbytes((r8, w), F32)))(allv)


def _dmod_prep(alld, name):
    _, w = alld.shape
    tw = _pick(w, 2048)

    def body(a_ref, d_ref, b_ref):
        ctx = a_ref[1:2, :]
        tot = a_ref[0:1, :]
        d_ref[0:1, :] = tot
        for d in range(1, 8):
            ctx = ctx + a_ref[8 * d + 1:8 * d + 2, :]
            tot = tot + a_ref[8 * d:8 * d + 1, :]
            d_ref[d:d + 1, :] = a_ref[8 * d:8 * d + 1, :]
        d_ref[8:9, :] = ctx
        d_ref[9:16, :] = jnp.zeros((7, tw), F32)
        b_ref[...] = tot + ctx

    return pl.pallas_call(
        body, grid=(w // tw,), in_specs=[pl.BlockSpec((64, tw), lambda j: (0, j))],
        out_specs=[pl.BlockSpec((16, tw), lambda j: (0, j)), pl.BlockSpec((1, tw), lambda j: (0, j))],
        out_shape=[jax.ShapeDtypeStruct((16, w), F32), jax.ShapeDtypeStruct((1, w), F32)], name=name,
        compiler_params=_cparams(("arbitrary",), 0))(alld)


def _pack(arrs, rows=8):
    flat = jnp.concatenate([a.reshape(-1).astype(F32) for a in arrs])
    per = rows * LANES
    n = flat.shape[0]
    padded = -(-n // per) * per
    flat = jnp.pad(flat, (0, padded - n))
    return flat.reshape(rows, padded // rows)


def _unpack(slab, shapes):
    flat = slab.reshape(-1)
    out, off = [], 0
    for s in shapes:
        n = math.prod(s)
        out.append(flat[off:off + n].reshape(s))
        off += n
    return out


def kernel(x, c, ctx, c_ctx, ln_in_g, ln_in_b, w_mod, b_mod, w_in, w_qkv_conv, a_log_f, dt_bias_f, a_log_b, dt_bias_b, dn_norm_g, conf_dw_w, conf_dw_b, conf_ln_g, conf_ln_b, w_out, ln1_g, ln1_b, w_mlp1, b_mlp1, w_mlp2, b_mlp2, ln2_g, ln2_b, loss_target, m_c_ctx, m_ln_in_g, m_ln_in_b, m_w_mod, m_b_mod, m_w_in, m_w_qkv_conv, m_a_log_f, m_dt_bias_f, m_a_log_b, m_dt_bias_b, m_dn_norm_g, m_conf_dw_w, m_conf_dw_b, m_conf_ln_g, m_conf_ln_b, m_w_out, m_ln1_g, m_ln1_b, m_w_mlp1, m_b_mlp1, m_w_mlp2, m_b_mlp2, m_ln2_g, m_ln2_b, v_c_ctx, v_ln_in_g, v_ln_in_b, v_w_mod, v_b_mod, v_w_in, v_w_qkv_conv, v_a_log_f, v_dt_bias_f, v_a_log_b, v_dt_bias_b, v_dn_norm_g, v_conf_dw_w, v_conf_dw_b, v_conf_ln_g, v_conf_ln_b, v_w_out, v_ln1_g, v_ln1_b, v_w_mlp1, v_b_mlp1, v_w_mlp2, v_b_mlp2, v_ln2_g, v_ln2_b):
    n_lat, d = x.shape[1], x.shape[2]
    n_ctx = ctx.shape[1]
    nt = n_lat + n_ctx
    dn = d // 2
    nh = dn // HD
    conf = d - dn
    half = conf // 2
    gw = math.isqrt(n_lat)
    ff = 4 * w_mlp1.shape[2]
    fl = w_mlp1.shape[2]
    in_cols = 4 * w_in.shape[2]
    r_in = w_in.shape[2]
    n_gate = 4 * nh
    gb = max(LANES, d // 8)
    g0 = 4 * dn + 2 * conf
    in_pad = g0 + gb
    ml = w_mod.shape[2]
    assert gw * gw == n_lat and nt % gw == 0 and in_cols == 4 * dn + n_gate + 2 * conf and nh * HD == dn
    hb = 8 if nh % 8 == 0 else (2 if nh % 2 == 0 else 1)

    mx, my, mc = _my_pos()
    chip = 2 * mx + my
    dev = 4 * mx + 2 * my + mc
    c_idx = jnp.reshape(mc, (1,)).astype(jnp.int32)

    x2, ctx2, tgt2 = x[0], ctx[0], loss_target[0]
    row = lambda a: a.reshape(1, -1)

    e0_shapes = [(d,), w_qkv_conv.shape[1:], conf_dw_w.shape[1:]]
    e0 = _bcast8(_pack([c[0], w_qkv_conv[0], conf_dw_w[0]]), "bcast_inputs")
    e0 = e0.reshape(8, -1)
    per_dev = [_unpack(e0[dd], e0_shapes) for dd in range(8)]
    all_c = jnp.stack([p[0] for p in per_dev])
    w_conv = jnp.concatenate([per_dev[2 * j][1] for j in range(4)], axis=1)
    w_dw = jnp.concatenate([per_dev[2 * j][2] for j in range(4)], axis=1)

    s16 = jnp.concatenate([all_c, c_ctx[None, :], jnp.zeros((7, d), F32)], axis=0)
    b_mod_loc = lax.dynamic_slice(b_mod, (0, chip * ml), (1, ml))
    to_mxu = lambda t: t.astype(MXU_DTYPE)
    modblk = _matmul(s16, w_mod[0], form="nn", m=16, n=ml, k=d, tm=16, tn=_pick(ml, 1024), tk=_pick(d, 1024),
                     out_dtype=F32, name="mod_fwd", bias=b_mod_loc, a_fn=lambda t: to_mxu(_silu(t)), b_fn=to_mxu)
    allmod = _bcast8(modblk, "bcast_mod").reshape(8, 16, ml)[0::2]
    mod_mine = lax.dynamic_index_in_dim(allmod, dev, axis=1, keepdims=False).reshape(1, 4 * ml)
    mod_ctx = allmod[:, 8, :].reshape(1, 4 * ml)
    sh_a, sc_a, g_a, sh_m, sc_m, g_m = [mod_mine[:, i * d:(i + 1) * d] for i in range(6)]
    csh_a, csc_a = mod_ctx[:, 0:d], mod_ctx[:, d:2 * d]

    chip_idx = jnp.reshape(chip, (1,)).astype(jnp.int32)
    win_t_loc = jnp.transpose(w_in[0]).astype(MXU_DTYPE)
    slot_in = _into_slot(win_t_loc, chip_idx, all_c, "weights_into_slot_0")
    gi_send, gi_recv, gi_bufs, gi_token = _gather_start([slot_in], allmod, "gather_w_in_start")
    slots_rest = [_into_slot(w, chip_idx, gi_token, f"weights_into_slot_{i + 1}") for i, w in enumerate([w_out[0], w_mlp1[0], w_mlp2[0]])]
    sh_a, csh_a = sh_a + gi_token[0:1, 0:1], csh_a + gi_token[0:1, 0:1]

    tmr = _pick(n_lat, 128, 8)
    tmc = _pick(n_ctx, 128, 8)
    ln_in = [row(ln_in_g), row(ln_in_b)]

    def stage_a(rv, pv):
        x0 = _ln(rv[0], pv[0], pv[1])
        return [x0 * (1.0 + pv[3]) + pv[2]], []

    xm_lat, = _rowwise(stage_a, [(x2, d, 0, 0)], ln_in + [sh_a, sc_a], [(d, MXU_DTYPE)], [], n_rows=n_lat, tm=tmr, name="ln_mod_lat")
    xm_ctx, = _rowwise(stage_a, [(ctx2, d, 0, 0)], ln_in + [csh_a, csc_a], [(d, MXU_DTYPE)], [], n_rows=n_ctx, tm=tmc, name="ln_mod_ctx")
    xm = jnp.concatenate([xm_lat, xm_ctx], axis=0)

    gw_in, = _gather_forward(_gather_wait(gi_send, gi_recv, gi_bufs, [xm] + slots_rest, "gather_w_in_wait"), "gather_w_in_forward")
    gs_send, gs_recv, gs_bufs, gs_token = _gather_start(slots_rest, gw_in, "gather_rest_start")
    win_t = gw_in.reshape(in_cols, d)
    win_t = jnp.concatenate([win_t[:4 * dn], win_t[4 * dn + n_gate:], win_t[4 * dn:4 * dn + n_gate],
                             jnp.zeros((gb - n_gate, d), MXU_DTYPE)], axis=0)

    tm_nt = _pick(nt, 1280, 16)
    tn_in = _pick(in_pad, 1280)
    tk_d = _pick(d, 4096)
    tn_in_s = _pick(in_pad, 640)
    h = _matmul(xm, win_t, form="nt", m=nt, n=in_pad, k=d, tm=tm_nt, tn=tn_in, tk=_pick(d, 2048), out_dtype=F32, name="in_proj",
                bias=jnp.zeros((1, in_pad), F32) + gs_token[0:1, 0:1])

    act = _conv7_fwd(h, w_conv, n_lat=n_lat, n_ctx=n_ctx, nh=nh)
    gates_hm = jnp.transpose(h[:, g0:g0 + n_gate].reshape(nt, 4, nh), (2, 0, 1))
    bc = lambda a: jnp.broadcast_to(a.reshape(nh, 1, 1), (nh, 8, LANES))
    dk = dict(n_lat=n_lat, n_ctx=n_ctx, nh=nh, hb=hb)
    o_f, st_f, mi_f = _delta_fwd(act, gates_hm, bc(a_log_f), bc(dt_bias_f), rev=False, **dk)
    o_b, st_b, mi_b = _delta_fwd(act, gates_hm, bc(a_log_b), bc(dt_bias_b), rev=True, **dk)

    h3 = h.reshape(nt // gw, gw, in_pad)
    ck = dict(gw=gw, col0=4 * dn, half=half)
    yh = _conf_conv_fwd(h3, w_dw, vertical=False, **ck).reshape(n_lat, half)
    yv = _conf_conv_fwd(h3, w_dw, vertical=True, **ck).reshape(n_lat, half)

    def mix_fn(o_heads, z_heads, ya, yb, dng, ba, bb, ga, gbb, la, lb):
        outs = []
        for o, z in zip(o_heads, z_heads):
            ms = jnp.mean(o * o, axis=-1, keepdims=True)
            outs.append(o * lax.rsqrt(ms + 1e-6) * dng * _silu(z))
        ya, yb = ya + ba, yb + bb
        mu = (jnp.sum(ya, axis=-1, keepdims=True) + jnp.sum(yb, axis=-1, keepdims=True)) / conf
        ya, yb = ya - mu, yb - mu
        var = (jnp.sum(ya * ya, axis=-1, keepdims=True) + jnp.sum(yb * yb, axis=-1, keepdims=True)) / conf
        rs = lax.rsqrt(var + LN_EPS)
        ca, cb = ya * rs * ga + la, yb * rs * gbb + lb
        return outs, _silu(ca), _silu(cb)

    def heads(t):
        return [t[:, i * HD:(i + 1) * HD] for i in range(nh)]

    def halves(p):
        return p[:, :half], p[:, half:]

    mix_params = [row(dn_norm_g), row(conf_dw_b), row(conf_ln_g), row(conf_ln_b)]

    def mix_args(rv, pv):
        (ba, bb), (ga, gbb), (la, lb) = halves(pv[1]), halves(pv[2]), halves(pv[3])
        return (heads(rv[0] + rv[1]), heads(rv[2]), rv[3], rv[4], pv[0], ba, bb, ga, gbb, la, lb)

    def stage_e(rv, pv):
        outs, ca, cb = mix_fn(*mix_args(rv, pv))
        return [jnp.concatenate(outs + [ca, cb], axis=-1)], []

    mix_rows = [(o_f, dn, 0, 0), (o_b, dn, 0, 0), (h, dn, 3, 0), (yh, half, 0, 0), (yv, half, 0, 0)]
    mix, = _rowwise(stage_e, mix_rows, mix_params, [(d, MXU_DTYPE)], [], n_rows=n_lat, tm=tmr, name="mixer_fwd")

    gs_bufs = _gather_wait(gs_send, gs_recv, gs_bufs, [mix], "gather_rest_wait")
    gw_out, = _gather_forward(gs_bufs[:1], "gather_w_out_forward")
    gf_send, gf_recv, gf_bufs, gf_token = _forward_start(gs_bufs[1:], gw_out, "gather_mlp_forward_start")
    wout_full = gw_out.reshape(d, d)

    tm_l = _pick(n_lat, 1024, 16)
    tn_d = _pick(d, 1024)
    y1 = _matmul(mix, wout_full, form="nn", m=n_lat, n=d, k=d, tm=tm_l, tn=tn_d, tk=tk_d, out_dtype=F32, name="out_proj",
                 bias=jnp.zeros((1, d), F32) + gf_token[0:1, 0:1])

    def res1(x0, y1v, ga, l1g, l1b, shm, scm):
        x1 = _ln(ALPHA * x0 + ga * y1v, l1g, l1b)
        return x1, x1 * (1.0 + scm) + shm

    f_params = ln_in + [g_a, row(ln1_g), row(ln1_b), sh_m, sc_m]

    def stage_f(rv, pv):
        x1, u = res1(_ln(rv[0], pv[0], pv[1]), rv[1], *pv[2:])
        return [x1, u], []

    x1, u = _rowwise(stage_f, [(x2, d, 0, 0), (y1, d, 0, 0)], f_params, [(d, F32), (d, MXU_DTYPE)], [], n_rows=n_lat, tm=tmr, name="res1_fwd")

    gw_1, gw_2 = _forward_wait(gf_send, gf_recv, gf_bufs, [u], "gather_mlp_forward_wait")
    w2_full = gw_2.reshape(ff, d)
    w1_sh = gw_1
    tn_f = _pick(fl, 1024)
    relu_h = _matmul(u, w1_sh, form="nn", m=n_lat, n=ff, k=d, tm=tm_l, tn=tn_f, tk=tk_d, out_dtype=MXU_DTYPE, name="mlp1",
                     bias=row(b_mlp1), epi=lambda r: jnp.maximum(r, 0.0), b_split=fl)
    sq = lambda t: (t.astype(F32) * t.astype(F32)).astype(MXU_DTYPE)
    tk_f = _pick(ff, 4096)
    y2 = _matmul(relu_h, w2_full, form="nn", m=n_lat, n=d, k=ff, tm=tm_l, tn=tn_d, tk=tk_f, out_dtype=F32, name="mlp2",
                 bias=row(b_mlp2), a_fn=sq)

    def loss_fn(x1v, y2v, gm, l2g, l2b, tgt):
        x2v = _ln(ALPHA * x1v + gm * y2v, l2g, l2b)
        return (0.5 / d) * jnp.sum(jnp.square(x2v - tgt))

    def stage_g(rv, pv):
        loss, grads = jax.value_and_grad(loss_fn, argnums=(0, 1, 2, 3, 4))(rv[0], rv[1], pv[0], pv[1], pv[2], rv[2])
        dx1, dy2r, dgm, dl2g, dl2b = grads
        dy2 = dy2r
        return [dx1, dy2], [jnp.reshape(loss, (1, 1)), dgm, dl2g, dl2b, jnp.sum(dy2, axis=0, keepdims=True)]

    dx1p, dy2, loss_acc, d_g_m, d_ln2_g, d_ln2_b, d_b_mlp2 = _rowwise(
        stage_g, [(x1, d, 0, 0), (y2, d, 0, 0), (tgt2, d, 0, 0)], [g_m, row(ln2_g), row(ln2_b)],
        [(d, F32), (d, MXU_DTYPE)], [LANES, d, d, d, d], n_rows=n_lat, tm=tmr, name="loss_res2_bwd")

    dhid, d_b_mlp1 = _matmul(dy2, w2_full, form="nt", m=n_lat, n=ff, k=d, tm=tm_l, tn=tn_f, tk=tk_d, out_dtype=MXU_DTYPE, name="mlp2_bwd_x",
                             epi=lambda r, rh: r * (2.0 * rh.astype(F32)), epi_in=relu_h, colsum=True)
    tk_l = _pick(n_lat, 4096, 16)
    d_w2 = _matmul(relu_h, dy2, form="tn", m=ff, n=d, k=n_lat, tm=tn_f, tn=tn_d, tk=tk_l, out_dtype=F32, name="mlp2_bwd_w", a_fn=sq)
    du = _matmul(dhid, w1_sh, form="nt", m=n_lat, n=d, k=ff, tm=tm_l, tn=tn_d, tk=_pick(fl, 4096), out_dtype=F32, name="mlp1_bwd_x", b_split=fl)
    d_w1 = _matmul(u, dhid, form="tn", m=d, n=ff, k=n_lat, tm=tn_d, tn=tn_f, tk=tk_l, out_dtype=F32, name="mlp1_bwd_w", o_split=fl)

    as_idx = lambda v: jnp.reshape(v, (1,)).astype(jnp.int32)
    slot_idx = [chip_idx, as_idx(2 * (1 - mx) + my), as_idx(2 * mx + (1 - my)), as_idx(2 * (1 - mx) + (1 - my)), c_idx]
    mp_send, mp_recv, mlp_grads, mlp_half, mp_token = _pair_send_start([d_w1, d_w2.reshape(4, fl, d)], loss_acc, "mlp_grads_pair_start")

    def stage_h(rv, pv):
        x0 = _ln(rv[0], pv[0], pv[1])
        _, vjp = jax.vjp(res1, x0, rv[1], *pv[2:])
        dx0, dy1r, dga, dl1g, dl1b, dshm, dscm = vjp((rv[3], rv[2]))
        return [dx0, dy1r], [dga, dl1g, dl1b, dshm, dscm]

    h_params = ln_in + [g_a + mp_token[0:1, 0:1]] + f_params[3:]
    dx0p, dy1, d_g_a, d_ln1_g, d_ln1_b, d_sh_m, d_sc_m = _rowwise(
        stage_h, [(x2, d, 0, 0), (y1, d, 0, 0), (du, d, 0, 0), (dx1p, d, 0, 0)], h_params,
        [(d, F32), (d, MXU_DTYPE)], [d, d, d, d, d], n_rows=n_lat, tm=tmr, name="res1_bwd")

    dmix = _matmul(dy1, wout_full, form="nt", m=n_lat, n=d, k=d, tm=tm_l, tn=tn_d, tk=tk_d, out_dtype=F32, name="out_proj_bwd_x")
    d_wout = _matmul(mix, dy1, form="tn", m=d, n=d, k=n_lat, tm=tn_d, tn=tn_d, tk=tk_l, out_dtype=F32, name="out_proj_bwd_w")

    def stage_i(rv, pv):
        args = mix_args(rv, pv)
        _, vjp = jax.vjp(mix_fn, *args)
        dm = rv[5]
        d_outs = [dm[:, i * HD:(i + 1) * HD] for i in range(nh)]
        dca, dcb = dm[:, dn:dn + half], dm[:, dn + half:]
        do_h, dz_h, dya, dyb, ddng, dba, dbb, dga, dgbb, dla, dlb = vjp((d_outs, dca, dcb))
        cat = lambda *p: jnp.concatenate(p, axis=-1)
        return ([cat(*do_h), cat(*dz_h), dya, dyb], [ddng, cat(dba, dbb), cat(dga, dgbb), cat(dla, dlb)])

    do, dz, dyh, dyv, d_dn_norm_g, d_conf_dw_b, d_conf_ln_g, d_conf_ln_b = _rowwise(
        stage_i, mix_rows + [(dmix, d, 0, 0)], mix_params, [(dn, F32), (dn, MXU_DTYPE), (half, F32), (half, F32)],
        [HD, conf, conf, conf], n_rows=n_lat, tm=tmr, name="mixer_bwd")

    mlp_grads, mlp_half = _pair_send_wait(mp_send, mp_recv, mlp_grads, mlp_half, do, "mlp_grads_pair_wait")
    mlp_sums = [_add_my_half(g, r, c_idx, f"mlp_grads_pair_add_{i}") for i, (g, r) in enumerate(zip(mlp_grads, mlp_half))]
    ms_send, ms_recv, mlp_sums, mlp_lands, ms_token = _scatter_start(mlp_sums, loss_acc, "mlp_grads_scatter_start")
    tie = ms_token[0:1, 0:1].reshape(1, 1, 1)

    dval_h, dgate_h, d_wdw_h = _conf_conv_bwd(h3, w_dw, dyh.reshape(gw, gw, half), vertical=False, **ck)
    dval_v, dgate_v, d_wdw_v = _conf_conv_bwd(h3, w_dw, dyv.reshape(gw, gw, half), vertical=True, **ck)
    dact_f, dgt_f, d_alog_f, d_dt_f = _delta_bwd(act, gates_hm, bc(a_log_f) + tie, bc(dt_bias_f), st_f, mi_f, do, rev=False, **dk)
    dact_b, dgt_b, d_alog_b, d_dt_b = _delta_bwd(act, gates_hm, bc(a_log_b) + tie, bc(dt_bias_b), st_b, mi_b, do, rev=True, **dk)
    dh_qkv, d_wconv = _conv7_bwd(h, w_conv, dact_f, dact_b, n_lat=n_lat, n_ctx=n_ctx, nh=nh)

    dgates = jnp.stack([dgt_f[..., 0], dgt_f[..., 1], dgt_b[..., 0], dgt_b[..., 1]], axis=0)
    dgates = jnp.transpose(dgates, (2, 0, 1)).reshape(nt, n_gate)
    dgates = jnp.pad(dgates, ((0, 0), (0, gb - n_gate))).astype(MXU_DTYPE)
    zrows = lambda t: jnp.pad(t, ((0, n_ctx), (0, 0)))
    dh = jnp.concatenate([dh_qkv, zrows(dz), zrows(dval_h.reshape(n_lat, half)), zrows(dval_v.reshape(n_lat, half)),
                          zrows(dgate_h.reshape(n_lat, half)), zrows(dgate_v.reshape(n_lat, half)), dgates], axis=1)

    tk_in = _pick(in_pad, 2560)
    d_win_t = _matmul(dh, xm, form="tn", m=in_pad, n=d, k=nt, tm=tn_in_s, tn=tn_d, tk=nt, out_dtype=F32, name="in_proj_bwd_w")
    d_win_t = jnp.concatenate([d_win_t[:4 * dn], d_win_t[g0:g0 + n_gate], d_win_t[4 * dn:g0]], axis=0)

    proj_grads = [d_win_t.reshape(4, r_in, d), d_wout.reshape(4, d // 4, d)]
    pp_send, pp_recv, proj_grads, proj_half, pp_token = _pair_send_start(proj_grads, loss_acc, "proj_grads_pair_start")
    dxm = _matmul(dh, win_t, form="nn", m=nt, n=d, k=in_pad, tm=tm_nt, tn=tn_d, tk=tk_in, out_dtype=F32, name="in_proj_bwd_x",
                  bias=jnp.zeros((1, d), F32) + pp_token[0:1, 0:1])
    proj_grads, proj_half = _pair_send_wait(pp_send, pp_recv, proj_grads, proj_half, dxm, "proj_grads_pair_wait")
    proj_sums = [_add_my_half(g, r, c_idx, f"proj_grads_pair_add_{i}") for i, (g, r) in enumerate(zip(proj_grads, proj_half))]
    ps_send, ps_recv, proj_sums, proj_lands, ps_token = _scatter_start(proj_sums, loss_acc, "proj_grads_scatter_start")
    mlp_sums, mlp_lands = _scatter_wait(ms_send, ms_recv, mlp_sums, mlp_lands, [ps_token], "mlp_grads_scatter_wait")
    mlp_mine = [_sum_slots(own, rcv, slot_idx, f"mlp_grads_chip_sum_{i}") for i, (own, rcv) in enumerate(zip(mlp_sums, mlp_lands))]
    g_w_mlp1, g_w_mlp2 = _pair_join_halves(mlp_mine, "mlp_grads_pair_join")
    mlp_adam = {"w_mlp1": _adamw(w_mlp1, g_w_mlp1[None], m_w_mlp1, v_w_mlp1, "adamw_w_mlp1"),
                "w_mlp2": _adamw(w_mlp2, g_w_mlp2[None], m_w_mlp2, v_w_mlp2, "adamw_w_mlp2")}
    mlp_done = [mlp_adam["w_mlp1"][2], mlp_adam["w_mlp2"][2]]
    sc_a_tied = sc_a + ps_token[0:1, 0:1]

    def mod_in(xr, lg, lb, sh, sc):
        x0 = _ln(xr, lg, lb)
        return x0, x0 * (1.0 + sc) + sh

    def stage_j(rv, pv):
        _, vjp = jax.vjp(mod_in, rv[0], *pv)
        dx0 = rv[2] if len(rv) > 2 else jnp.zeros_like(rv[1])
        dxr, dlg, dlb, dsh, dsc = vjp((dx0, rv[1]))
        return [dxr], [dlg, dlb, dsh, dsc]

    grad_x, dlg_l, dlb_l, d_sh_a, d_sc_a = _rowwise(
        stage_j, [(x2, d, 0, 0), (dxm, d, 0, 0), (dx0p, d, 0, 0)], ln_in + [sh_a, sc_a_tied], [(d, F32)], [d, d, d, d],
        n_rows=n_lat, tm=tmr, name="ln_mod_bwd_lat", after=mlp_done)
    _, dlg_c, dlb_c, d_csh_a, d_csc_a = _rowwise(
        stage_j, [(ctx2, d, 0, 0), (dxm, d, 0, n_lat // tmc)], ln_in + [csh_a, csc_a], [(d, F32)], [d, d, d, d],
        n_rows=n_ctx, tm=tmc, name="ln_mod_bwd_ctx", after=mlp_done)

    zd = jnp.zeros((1, d), F32)
    dmod_rows = jnp.concatenate([jnp.concatenate([d_sh_a, d_sc_a, d_g_a, d_sh_m, d_sc_m, d_g_m], axis=1),
                                 jnp.concatenate([d_csh_a, d_csc_a, zd, zd, zd, zd], axis=1), jnp.zeros((6, 6 * d), F32)], axis=0)
    d16, g_b_mod = _dmod_prep(_bcast8(dmod_rows, "bcast_dmod"), "dmod_prep")
    d16_loc = lax.dynamic_slice(d16, (0, chip * ml), (16, ml))
    g_w_mod = _matmul(s16, d16_loc, form="tn", m=d, n=ml, k=16, tm=_pick(d, 512), tn=_pick(ml, 1024), tk=16, out_dtype=F32,
                      name="mod_bwd_w", a_fn=_silu, kind="hi")
    dsilu = _matmul(d16_loc, w_mod[0], form="nt", m=16, n=d, k=ml, tm=16, tn=_pick(d, 1024), tk=_pick(ml, 1024), out_dtype=F32,
                    name="mod_bwd_c", a_fn=to_mxu, b_fn=to_mxu)
    d_cctx_part = dsilu[8:9, :] * (1 - mc).astype(F32)

    head_sum = lambda t: t[:, 0, 0]
    small = [d_cctx_part, dlg_l + dlg_c, dlb_l + dlb_c, d_wconv, head_sum(d_alog_f), head_sum(d_dt_f), head_sum(d_alog_b),
             head_sum(d_dt_b), d_dn_norm_g, jnp.concatenate([d_wdw_h, d_wdw_v], axis=1), d_conf_dw_b, d_conf_ln_g, d_conf_ln_b,
             d_ln1_g, d_ln1_b, d_b_mlp1, d_b_mlp2, d_ln2_g, d_ln2_b]
    small_shapes = [(d,), (d,), (d,), (SHORT_CONV, 3 * dn), (nh,), (nh,), (nh,), (nh,), (HD,), (CONF_K, conf), (conf,), (conf,), (conf,),
                    (d,), (d,), (ff,), (d,), (d,), (d,)]
    ssum = _sum8(_bcast8(_pack(small), "bcast_small_grads"), "sum_small_grads")
    (t_cctx, g_ln_in_g, g_ln_in_b, g_wconv_full, g_a_log_f, g_dt_bias_f, g_a_log_b, g_dt_bias_b, g_dn_norm_g, g_wdw_full, g_conf_dw_b,
     g_conf_ln_g, g_conf_ln_b, g_ln1_g, g_ln1_b, g_b_mlp1, g_b_mlp2, g_ln2_g, g_ln2_b) = _unpack(ssum, small_shapes)
    sg = jax.nn.sigmoid(c_ctx)
    g_c_ctx = t_cctx * (sg * (1.0 + c_ctx * (1.0 - sg)))
    g_w_qkv_conv = lax.dynamic_slice(g_wconv_full, (0, chip * w_qkv_conv.shape[2]), w_qkv_conv.shape[1:])
    g_conf_dw_w = lax.dynamic_slice(g_wdw_full, (0, chip * conf_dw_w.shape[2]), conf_dw_w.shape[1:])

    grads = dict(c_ctx=g_c_ctx, ln_in_g=g_ln_in_g, ln_in_b=g_ln_in_b, w_mod=g_w_mod[None], b_mod=g_b_mod, w_in=None,
                 w_qkv_conv=g_w_qkv_conv[None], a_log_f=g_a_log_f[None], dt_bias_f=g_dt_bias_f[None], a_log_b=g_a_log_b[None],
                 dt_bias_b=g_dt_bias_b[None], dn_norm_g=g_dn_norm_g[None], conf_dw_w=g_conf_dw_w[None], conf_dw_b=g_conf_dw_b[None],
                 conf_ln_g=g_conf_ln_g[None], conf_ln_b=g_conf_ln_b[None], w_out=None, ln1_g=g_ln1_g[None], ln1_b=g_ln1_b[None],
                 w_mlp1=g_w_mlp1[None], b_mlp1=g_b_mlp1[None], w_mlp2=g_w_mlp2[None], b_mlp2=g_b_mlp2[None], ln2_g=g_ln2_g[None],
                 ln2_b=g_ln2_b[None])
    weights = dict(c_ctx=c_ctx, ln_in_g=ln_in_g, ln_in_b=ln_in_b, w_mod=w_mod, b_mod=b_mod, w_in=w_in, w_qkv_conv=w_qkv_conv,
                   a_log_f=a_log_f, dt_bias_f=dt_bias_f, a_log_b=a_log_b, dt_bias_b=dt_bias_b, dn_norm_g=dn_norm_g, conf_dw_w=conf_dw_w,
                   conf_dw_b=conf_dw_b, conf_ln_g=conf_ln_g, conf_ln_b=conf_ln_b, w_out=w_out, ln1_g=ln1_g, ln1_b=ln1_b, w_mlp1=w_mlp1,
                   b_mlp1=b_mlp1, w_mlp2=w_mlp2, b_mlp2=b_mlp2, ln2_g=ln2_g, ln2_b=ln2_b)
    m_in = dict(c_ctx=m_c_ctx, ln_in_g=m_ln_in_g, ln_in_b=m_ln_in_b, w_mod=m_w_mod, b_mod=m_b_mod, w_in=m_w_in, w_qkv_conv=m_w_qkv_conv,
                a_log_f=m_a_log_f, dt_bias_f=m_dt_bias_f, a_log_b=m_a_log_b, dt_bias_b=m_dt_bias_b, dn_norm_g=m_dn_norm_g,
                conf_dw_w=m_conf_dw_w, conf_dw_b=m_conf_dw_b, conf_ln_g=m_conf_ln_g, conf_ln_b=m_conf_ln_b, w_out=m_w_out, ln1_g=m_ln1_g,
                ln1_b=m_ln1_b, w_mlp1=m_w_mlp1, b_mlp1=m_b_mlp1, w_mlp2=m_w_mlp2, b_mlp2=m_b_mlp2, ln2_g=m_ln2_g, ln2_b=m_ln2_b)
    v_in = dict(c_ctx=v_c_ctx, ln_in_g=v_ln_in_g, ln_in_b=v_ln_in_b, w_mod=v_w_mod, b_mod=v_b_mod, w_in=v_w_in, w_qkv_conv=v_w_qkv_conv,
                a_log_f=v_a_log_f, dt_bias_f=v_dt_bias_f, a_log_b=v_a_log_b, dt_bias_b=v_dt_bias_b, dn_norm_g=v_dn_norm_g,
                conf_dw_w=v_conf_dw_w, conf_dw_b=v_conf_dw_b, conf_ln_g=v_conf_ln_g, conf_ln_b=v_conf_ln_b, w_out=v_w_out, ln1_g=v_ln1_g,
                ln1_b=v_ln1_b, w_mlp1=v_w_mlp1, b_mlp1=v_b_mlp1, w_mlp2=v_w_mlp2, b_mlp2=v_b_mlp2, ln2_g=v_ln2_g, ln2_b=v_ln2_b)
    names = list(weights)
    big_names = ("w_mod", "w_mlp1", "w_mlp2", "w_in", "w_out")
    delta, new_m, new_v = {}, {}, {}

    def big_adamw(nm):
        delta[nm], new_m[nm], new_v[nm] = mlp_adam[nm] if nm in mlp_adam else _adamw(weights[nm], grads[nm], m_in[nm], v_in[nm], f"adamw_{nm}")

    for nm in big_names[:3]:
        big_adamw(nm)
    small_names = [nm for nm in names if nm not in big_names]
    shapes = [weights[nm].shape for nm in small_names]
    for nm in small_names:
        grads[nm] = grads[nm].reshape(weights[nm].shape)
    packed = [_pack([src[nm] for nm in small_names]) for src in (weights, grads, m_in, v_in)]
    dl, mm, vv = _adamw(*packed, "adamw_small")
    for nm, a, b_, c_ in zip(small_names, _unpack(dl, shapes), _unpack(mm, shapes), _unpack(vv, shapes)):
        delta[nm], new_m[nm], new_v[nm] = a, b_, c_

    done = [new_v[nm] for nm in big_names[:3]] + [vv]
    proj_sums, proj_lands = _scatter_wait(ps_send, ps_recv, proj_sums, proj_lands, done, "proj_grads_scatter_wait")
    proj_mine = [_sum_slots(own, rcv, slot_idx, f"proj_grads_chip_sum_{i}") for i, (own, rcv) in enumerate(zip(proj_sums, proj_lands))]
    g_win_t, g_w_out = _pair_join_halves(proj_mine, "proj_grads_pair_join")
    grads["w_in"], grads["w_out"] = jnp.transpose(g_win_t)[None], g_w_out[None]
    tr_in = lambda a: jnp.transpose(a[0])
    tr_out = lambda a: jnp.transpose(a)[None]
    dl_t, mm_t, vv_t = _adamw(tr_in(w_in), g_win_t, tr_in(m_w_in), tr_in(v_w_in), "adamw_w_in")
    delta["w_in"], new_m["w_in"], new_v["w_in"] = tr_out(dl_t), tr_out(mm_t), tr_out(vv_t)
    big_adamw("w_out")

    loss = lax.psum(loss_acc[0, 0], MESH_AXES)
    return (loss, grad_x[None], *[grads[nm] for nm in names], *[delta[nm] for nm in names],
            *[new_m[nm] for nm in names], *[new_v[nm] for nm in names])
```

```python
import functools
import math

import jax
import jax.numpy as jnp
from jax import lax
from jax.experimental import pallas as pl
from jax.experimental.pallas import tpu as pltpu

F32 = jnp.float32
BF16 = jnp.bfloat16
MXU_DTYPE = BF16
WIRE_DTYPE = BF16
HIGHEST = lax.Precision.HIGHEST

HD = 128
CHUNK = 128
SHORT_CONV = 7
CONF_K = 31
ALPHA = 2.0 ** 0.25
LN_EPS = 1e-5
V7X_VMEM_BYTES = 64 * 1024 * 1024
LANES = 128

ADAM_LR = 0.001
ADAM_B1 = 0.9
ADAM_B2 = 0.999
ADAM_EPS = 1e-08
ADAM_WD = 0.01
ADAM_STEP = 10

MESH_AXES = ("x", "y", "c")
ANY = pl.BlockSpec(memory_space=pl.ANY)


def _pick(dim, pref, mult=LANES):
    best = None
    t = mult
    while t <= min(dim, pref):
        if dim % t == 0:
            best = t
        t += mult
    return best if best is not None else dim


def _cparams(sem, vmem_est):
    limit = int(min(V7X_VMEM_BYTES - 6 * 1024 * 1024, max(32 * 1024 * 1024, vmem_est + 8 * 1024 * 1024)))
    return pltpu.CompilerParams(dimension_semantics=sem, vmem_limit_bytes=limit)


def _nbytes(shape, dtype):
    return math.prod(shape) * jnp.dtype(dtype).itemsize


def _ln(x, g, b):
    mu = jnp.mean(x, axis=-1, keepdims=True)
    xc = x - mu
    var = jnp.mean(xc * xc, axis=-1, keepdims=True)
    return xc * lax.rsqrt(var + LN_EPS) * g + b


def _silu(x):
    return x * jax.nn.sigmoid(x)


def _softplus(x):
    return jnp.maximum(x, 0.0) + jnp.log1p(jnp.exp(-jnp.abs(x)))


_DIMS = {"nn": (((1,), (0,)), ((), ())), "nt": (((1,), (1,)), ((), ())), "tn": (((0,), (0,)), ((), ()))}


def _split_bf16(x, parts):
    out, rest = [], x.astype(F32)
    for _ in range(parts):
        bits = lax.bitcast_convert_type(rest, jnp.uint32) & jnp.uint32(0xFFFF0000)
        p = lax.bitcast_convert_type(bits, F32)
        out.append(p.astype(BF16))
        rest = rest - p
    return out


def _raw_dot(a, b, form, kind):
    if MXU_DTYPE != F32:
        if kind == "mxu":
            return lax.dot_general(a.astype(MXU_DTYPE), b.astype(MXU_DTYPE), _DIMS[form], preferred_element_type=F32)
        if kind == "hi3":
            (a0, a1), (b0, b1) = _split_bf16(a, 2), _split_bf16(b, 2)
            d = lambda p, q: lax.dot_general(p, q, _DIMS[form], preferred_element_type=F32)
            return d(a0, b0) + (d(a1, b0) + d(a0, b1))
    return lax.dot_general(a.astype(F32), b.astype(F32), _DIMS[form], precision=HIGHEST, preferred_element_type=F32)


@functools.partial(jax.custom_vjp, nondiff_argnums=(2, 3))
def _dot(a, b, form, kind):
    return _raw_dot(a, b, form, kind)


def _dot_fwd(a, b, form, kind):
    return _raw_dot(a, b, form, kind), (a, b)


def _dot_bwd(form, kind, res, dc):
    a, b = res
    if form == "nn":
        return _dot(dc, b, "nt", kind), _dot(a, dc, "tn", kind)
    if form == "nt":
        return _dot(dc, b, "nn", kind), _dot(dc, a, "tn", kind)
    return _dot(b, dc, "nt", kind), _dot(a, dc, "nn", kind)


_dot.defvjp(_dot_fwd, _dot_bwd)


def _raw_xdot(a, b, form, exact):
    if MXU_DTYPE == F32:
        return _raw_dot(a, b, form, "hi")
    d = lambda p, q: lax.dot_general(p, q, _DIMS[form], preferred_element_type=F32)
    if exact == "a":
        a16, (b0, b1, b2) = a.astype(BF16), _split_bf16(b, 3)
        return d(a16, b0) + (d(a16, b1) + d(a16, b2))
    b16, (a0, a1, a2) = b.astype(BF16), _split_bf16(a, 3)
    return d(a0, b16) + (d(a1, b16) + d(a2, b16))


@functools.partial(jax.custom_vjp, nondiff_argnums=(2, 3))
def _xdot(a, b, form, exact):
    return _raw_xdot(a, b, form, exact)


def _xdot_fwd(a, b, form, exact):
    return _raw_xdot(a, b, form, exact), (a, b)


def _xdot_bwd(form, exact, res, dc):
    a, b = res
    if form == "nn" and exact == "a":
        return jnp.zeros_like(a), _xdot(a, dc, "tn", "a")
    assert form == "tn" and exact == "b"
    return _xdot(b, dc, "nt", "a"), jnp.zeros_like(b)


_xdot.defvjp(_xdot_fwd, _xdot_bwd)


def _spec2(block, idx, split=None):
    if split is None:
        return pl.BlockSpec(tuple(block), idx)
    per = split // block[1]

    def idx3(*g):
        r, cblk = idx(*g)
        return (cblk // per, r, cblk % per)

    return pl.BlockSpec((None,) + tuple(block), idx3)


def _matmul(a, b, *, form, m, n, k, tm, tn, tk, out_dtype, name, bias=None, a_fn=None, b_fn=None,
            epi=None, epi_in=None, colsum=False, kind="mxu", b_split=None, o_split=None):
    assert m % tm == 0 and n % tn == 0 and k % tk == 0, (name, m, n, k, tm, tn, tk)
    nk = k // tk
    grid = (n // tn, m // tm, nk)
    if form == "tn":
        a_spec = pl.BlockSpec((tk, tm), lambda j, i, kk: (kk, i))
    else:
        a_spec = pl.BlockSpec((tm, tk), lambda j, i, kk: (i, kk))
    if form == "nt":
        b_spec = _spec2((tn, tk), lambda j, i, kk: (j, kk), b_split)
    else:
        b_spec = _spec2((tk, tn), lambda j, i, kk: (kk, j), b_split)
    in_specs = [a_spec, b_spec]
    operands = [a, b]
    if bias is not None:
        in_specs.append(pl.BlockSpec((1, tn), lambda j, i, kk: (0, j)))
        operands.append(bias)
    if epi_in is not None:
        in_specs.append(pl.BlockSpec((tm, tn), lambda j, i, kk: (i, j)))
        operands.append(epi_in)
    out_specs = [_spec2((tm, tn), lambda j, i, kk: (i, j), o_split)]
    if o_split is None:
        out_shape = [jax.ShapeDtypeStruct((m, n), out_dtype)]
    else:
        out_shape = [jax.ShapeDtypeStruct((n // o_split, m, o_split), out_dtype)]
    if colsum:
        out_specs.append(pl.BlockSpec((1, tn), lambda j, i, kk: (0, j)))
        out_shape.append(jax.ShapeDtypeStruct((1, n), F32))
    has_bias, has_epi_in = bias is not None, epi_in is not None

    def body(*refs):
        refs = list(refs)
        a_ref, b_ref = refs[0], refs[1]
        pos = 2
        bias_ref = epi_ref = cs_ref = None
        if has_bias:
            bias_ref = refs[pos]
            pos += 1
        if has_epi_in:
            epi_ref = refs[pos]
            pos += 1
        o_ref = refs[pos]
        pos += 1
        if colsum:
            cs_ref = refs[pos]
            pos += 1
        acc_ref = refs[pos] if nk > 1 else None
        i, kk = pl.program_id(1), pl.program_id(2)

        av = a_ref[...]
        if a_fn is not None:
            av = a_fn(av)
        bv = b_ref[...]
        if b_fn is not None:
            bv = b_fn(bv)
        prod = _raw_dot(av, bv, form, kind)

        if nk > 1:
            @pl.when(kk == 0)
            def _():
                acc_ref[...] = prod

            @pl.when(jnp.logical_and(kk > 0, kk < nk - 1))
            def _():
                acc_ref[...] += prod

        @pl.when(kk == nk - 1)
        def _():
            r = acc_ref[...] + prod if nk > 1 else prod
            if has_bias:
                r = r + bias_ref[...]
            if epi is not None:
                r = epi(r, epi_ref[...]) if has_epi_in else epi(r)
            o_ref[...] = r.astype(out_dtype)
            if colsum:
                s = jnp.sum(r, axis=0, keepdims=True)

                @pl.when(i == 0)
                def _():
                    cs_ref[...] = s

                @pl.when(i > 0)
                def _():
                    cs_ref[...] += s

    est = 2 * (_nbytes((tm, tk), a.dtype) + _nbytes((tk, tn), b.dtype) + _nbytes((tm, tn), out_dtype))
    est += _nbytes((tm, tn), F32) * 2 + (2 * _nbytes((tm, tn), epi_in.dtype) if has_epi_in else 0)
    res = pl.pallas_call(
        body, grid=grid, in_specs=in_specs, out_specs=out_specs, out_shape=out_shape,
        scratch_shapes=[pltpu.VMEM((tm, tn), F32)] if nk > 1 else [], name=name,
        compiler_params=_cparams(("arbitrary", "arbitrary", "arbitrary"), est),
    )(*operands)
    return res if colsum else res[0]


def _rowwise(fn, rows, params, row_outs, acc_outs, *, n_rows, tm, name, after=()):
    assert n_rows % tm == 0, (name, n_rows, tm)
    nr, npar, nro, nac, naf = len(rows), len(params), len(row_outs), len(acc_outs), len(after)
    in_specs = [pl.BlockSpec((tm, w), functools.partial(lambda i, cb, ro: (i + ro, cb), cb=cb, ro=ro))
                for (_, w, cb, ro) in rows]
    in_specs += [pl.BlockSpec((1, p.shape[1]), lambda i: (0, 0)) for p in params] + [ANY] * naf
    out_specs = [pl.BlockSpec((tm, w), lambda i: (i, 0)) for (w, _) in row_outs]
    out_specs += [pl.BlockSpec((1, w), lambda i: (0, 0)) for w in acc_outs]
    out_shape = [jax.ShapeDtypeStruct((n_rows, w), dt) for (w, dt) in row_outs]
    out_shape += [jax.ShapeDtypeStruct((1, w), F32) for w in acc_outs]

    def body(*refs):
        rv = [r[...] for r in refs[:nr]]
        pv = [r[...] for r in refs[nr:nr + npar]]
        ro_refs = refs[nr + npar + naf:nr + npar + naf + nro]
        ac_refs = refs[nr + npar + naf + nro:]
        ro, ac = fn(rv, pv)
        for ref, val in zip(ro_refs, ro, strict=True):
            ref[...] = val.astype(ref.dtype)
        first = pl.program_id(0) == 0
        for ref, val in zip(ac_refs, ac, strict=True):
            val = jnp.broadcast_to(val.astype(F32), ref.shape)

            @pl.when(first)
            def _(ref=ref, val=val):
                ref[...] = val

            @pl.when(jnp.logical_not(first))
            def _(ref=ref, val=val):
                ref[...] += val

    io = sum(_nbytes((tm, w), a.dtype) for (a, w, _, _) in rows) + sum(_nbytes((tm, w), dt) for (w, dt) in row_outs)
    widest = max([w for (_, w, _, _) in rows] + [w for (w, _) in row_outs])
    est = 2 * io + 12 * _nbytes((tm, widest), F32)
    return pl.pallas_call(
        body, grid=(n_rows // tm,), in_specs=in_specs, out_specs=out_specs, out_shape=out_shape, name=name,
        compiler_params=_cparams(("arbitrary",), est),
    )(*[a for (a, _, _, _) in rows], *params, *after)


def _conv7_tiles(n_lat, n_ctx):
    tt = min(256, n_ctx)
    assert n_lat % tt == 0 and n_ctx % tt == 0
    return tt, [(t0, 8 + t0) for t0 in range(0, n_lat, tt)] + [(n_lat + t0, 16 + n_lat + t0) for t0 in range(0, n_ctx, tt)]


def _conv7_fill(p_ref, x_ref, n_lat, n_ctx):
    z8 = jnp.zeros((8, LANES), F32)
    p_ref[0:8, :] = z8
    p_ref[8:8 + n_lat, :] = x_ref[0:n_lat, :]
    p_ref[8 + n_lat:16 + n_lat, :] = z8
    p_ref[16 + n_lat:16 + n_lat + n_ctx, :] = x_ref[n_lat:n_lat + n_ctx, :]
    p_ref[16 + n_lat + n_ctx:24 + n_lat + n_ctx, :] = z8


def _conv7_fwd(h, w_conv, *, n_lat, n_ctx, nh):
    nt = n_lat + n_ctx
    tt, tiles = _conv7_tiles(n_lat, n_ctx)
    half = SHORT_CONV // 2

    def body(x_ref, w_ref, o_ref, p_ref):
        _conv7_fill(p_ref, x_ref, n_lat, n_ctx)
        for (row, prow) in tiles:
            acc = jnp.zeros((tt, LANES), F32)
            for kk in range(SHORT_CONV):
                acc = acc + w_ref[kk:kk + 1, :] * p_ref[pl.ds(prow + kk - half, tt), :]
            o_ref[pl.ds(row, tt), :] = _silu(acc)

    return pl.pallas_call(
        body, grid=(3 * nh,),
        in_specs=[pl.BlockSpec((nt, LANES), lambda j: (0, j)), pl.BlockSpec((SHORT_CONV, LANES), lambda j: (0, j))],
        out_specs=pl.BlockSpec((None, None, nt, LANES), lambda j: (j // nh, j % nh, 0, 0)),
        out_shape=jax.ShapeDtypeStruct((3, nh, nt, LANES), F32),
        scratch_shapes=[pltpu.VMEM((nt + 24, LANES), F32)], name="conv7_fwd",
        compiler_params=_cparams(("arbitrary",), 5 * _nbytes((nt + 24, LANES), F32)),
    )(h, w_conv)


def _conv7_bwd(h, w_conv, dact_f, dact_b, *, n_lat, n_ctx, nh):
    nt = n_lat + n_ctx
    tt, tiles = _conv7_tiles(n_lat, n_ctx)
    half = SHORT_CONV // 2

    def body(x_ref, w_ref, df_ref, db_ref, dx_ref, dw_ref, p_ref, q_ref):
        _conv7_fill(p_ref, x_ref, n_lat, n_ctx)
        z8 = jnp.zeros((8, LANES), F32)
        q_ref[0:8, :] = z8
        q_ref[8 + n_lat:16 + n_lat, :] = z8
        q_ref[16 + n_lat + n_ctx:24 + n_lat + n_ctx, :] = z8
        dw = [jnp.zeros((1, LANES), F32) for _ in range(SHORT_CONV)]
        for (row, prow) in tiles:
            pre = jnp.zeros((tt, LANES), F32)
            for kk in range(SHORT_CONV):
                pre = pre + w_ref[kk:kk + 1, :] * p_ref[pl.ds(prow + kk - half, tt), :]
            s = jax.nn.sigmoid(pre)
            dpre = (df_ref[pl.ds(row, tt), :] + db_ref[pl.ds(row, tt), :]) * (s * (1.0 + pre * (1.0 - s)))
            q_ref[pl.ds(prow, tt), :] = dpre
            for kk in range(SHORT_CONV):
                dw[kk] = dw[kk] + jnp.sum(dpre * p_ref[pl.ds(prow + kk - half, tt), :], axis=0, keepdims=True)
        for (row, prow) in tiles:
            acc = jnp.zeros((tt, LANES), F32)
            for kk in range(SHORT_CONV):
                acc = acc + w_ref[kk:kk + 1, :] * q_ref[pl.ds(prow + half - kk, tt), :]
            dx_ref[pl.ds(row, tt), :] = acc.astype(dx_ref.dtype)
        for kk in range(SHORT_CONV):
            dw_ref[kk:kk + 1, :] = dw[kk]

    dspec = pl.BlockSpec((None, None, nt, LANES), lambda j: (j // nh, j % nh, 0, 0))
    return pl.pallas_call(
        body, grid=(3 * nh,),
        in_specs=[pl.BlockSpec((nt, LANES), lambda j: (0, j)), pl.BlockSpec((SHORT_CONV, LANES), lambda j: (0, j)), dspec, dspec],
        out_specs=[pl.BlockSpec((nt, LANES), lambda j: (0, j)), pl.BlockSpec((SHORT_CONV, LANES), lambda j: (0, j))],
        out_shape=[jax.ShapeDtypeStruct((nt, 3 * nh * LANES), MXU_DTYPE), jax.ShapeDtypeStruct((SHORT_CONV, 3 * nh * LANES), F32)],
        scratch_shapes=[pltpu.VMEM((nt + 24, LANES), F32), pltpu.VMEM((nt + 24, LANES), F32)], name="conv7_bwd",
        compiler_params=_cparams(("arbitrary",), 10 * _nbytes((nt + 24, LANES), F32)),
    )(h, w_conv, dact_f, dact_b)


_RT = 4
_CP = CONF_K // 2


def _conf_pad_shape(gw, vertical):
    return (gw + 2 * _CP, gw, LANES) if vertical else (gw, gw + 32, LANES)


def _conf_fill(p_ref, val, gw, vertical):
    if vertical:
        p_ref[0:_CP] = jnp.zeros((_CP, gw, LANES), F32)
        p_ref[_CP + gw:2 * _CP + gw] = jnp.zeros((_CP, gw, LANES), F32)
        p_ref[_CP:_CP + gw] = val
    else:
        p_ref[:, 0:16, :] = jnp.zeros((gw, 16, LANES), F32)
        p_ref[:, 16 + gw:32 + gw, :] = jnp.zeros((gw, 16, LANES), F32)
        p_ref[:, 16:16 + gw, :] = val


def _conf_window(p_ref, r0, shift, gw, vertical):
    if vertical:
        return p_ref[pl.ds(r0 + _CP + shift, _RT), :, :]
    return p_ref[pl.ds(r0, _RT), pl.ds(16 + shift, gw), :]


def _conf_conv_fwd(h3, w_dw, *, gw, col0, half, vertical):
    cb_val = (col0 + (half if vertical else 0)) // LANES
    cb_gate = cb_val + 2 * half // LANES
    cb_w = (half if vertical else 0) // LANES

    def body(v_ref, g_ref, w_ref, o_ref, p_ref):
        _conf_fill(p_ref, v_ref[...] * jax.nn.sigmoid(g_ref[...]), gw, vertical)

        def step(t, carry):
            r0 = t * _RT
            acc = jnp.zeros((_RT, gw, LANES), F32)
            for kk in range(CONF_K):
                acc = acc + w_ref[kk:kk + 1, :] * _conf_window(p_ref, r0, kk - _CP, gw, vertical)
            o_ref[pl.ds(r0, _RT)] = acc
            return carry

        lax.fori_loop(0, gw // _RT, step, 0)

    blk = (gw, gw, LANES)
    return pl.pallas_call(
        body, grid=(half // LANES,),
        in_specs=[pl.BlockSpec(blk, lambda j: (0, 0, cb_val + j)), pl.BlockSpec(blk, lambda j: (0, 0, cb_gate + j)),
                  pl.BlockSpec((CONF_K, LANES), lambda j: (0, cb_w + j))],
        out_specs=pl.BlockSpec(blk, lambda j: (0, 0, j)),
        out_shape=jax.ShapeDtypeStruct((gw, gw, half), F32),
        scratch_shapes=[pltpu.VMEM(_conf_pad_shape(gw, vertical), F32)],
        name="conf_conv_fwd_v" if vertical else "conf_conv_fwd_h",
        compiler_params=_cparams(("arbitrary",), 8 * _nbytes(_conf_pad_shape(gw, vertical), F32)),
    )(h3, h3, w_dw)


def _conf_conv_bwd(h3, w_dw, dyc, *, gw, col0, half, vertical):
    cb_val = (col0 + (half if vertical else 0)) // LANES
    cb_gate = cb_val + 2 * half // LANES
    cb_w = (half if vertical else 0) // LANES

    def body(v_ref, g_ref, w_ref, d_ref, dv_ref, dg_ref, dw_ref, py_ref, pd_ref):
        _conf_fill(py_ref, v_ref[...] * jax.nn.sigmoid(g_ref[...]), gw, vertical)
        _conf_fill(pd_ref, d_ref[...], gw, vertical)

        def step(t, carry):
            r0 = t * _RT
            acc = jnp.zeros((_RT, gw, LANES), F32)
            for kk in range(CONF_K):
                acc = acc + w_ref[kk:kk + 1, :] * _conf_window(pd_ref, r0, _CP - kk, gw, vertical)
            val = v_ref[pl.ds(r0, _RT)]
            sg = jax.nn.sigmoid(g_ref[pl.ds(r0, _RT)])
            dv_ref[pl.ds(r0, _RT)] = (acc * sg).astype(dv_ref.dtype)
            dg_ref[pl.ds(r0, _RT)] = (acc * val * sg * (1.0 - sg)).astype(dg_ref.dtype)
            return carry

        lax.fori_loop(0, gw // _RT, step, 0)

        for kk in range(CONF_K):
            def wstep(t, acc, kk=kk):
                r0 = t * _RT
                prod = d_ref[pl.ds(r0, _RT)] * _conf_window(py_ref, r0, kk - _CP, gw, vertical)
                return acc + jnp.sum(prod, axis=0)

            tot = lax.fori_loop(0, gw // _RT, wstep, jnp.zeros((gw, LANES), F32))
            dw_ref[kk:kk + 1, :] = jnp.sum(tot, axis=0, keepdims=True)

    blk = (gw, gw, LANES)
    pshape = _conf_pad_shape(gw, vertical)
    return pl.pallas_call(
        body, grid=(half // LANES,),
        in_specs=[pl.BlockSpec(blk, lambda j: (0, 0, cb_val + j)), pl.BlockSpec(blk, lambda j: (0, 0, cb_gate + j)),
                  pl.BlockSpec((CONF_K, LANES), lambda j: (0, cb_w + j)), pl.BlockSpec(blk, lambda j: (0, 0, j))],
        out_specs=[pl.BlockSpec(blk, lambda j: (0, 0, j)), pl.BlockSpec(blk, lambda j: (0, 0, j)),
                   pl.BlockSpec((CONF_K, LANES), lambda j: (0, j))],
        out_shape=[jax.ShapeDtypeStruct((gw, gw, half), MXU_DTYPE), jax.ShapeDtypeStruct((gw, gw, half), MXU_DTYPE),
                   jax.ShapeDtypeStruct((CONF_K, half), F32)],
        scratch_shapes=[pltpu.VMEM(pshape, F32), pltpu.VMEM(pshape, F32)],
        name="conf_conv_bwd_v" if vertical else "conf_conv_bwd_h",
        compiler_params=_cparams(("arbitrary",), 12 * _nbytes(pshape, F32)),
    )(h3, h3, w_dw, dyc)


_INV_BASE = 8


def _tri_inv_raw(mats):
    c = mats[0].shape[0]
    ri = lax.broadcasted_iota(jnp.int32, (c, c), 0)
    ci = lax.broadcasted_iota(jnp.int32, (c, c), 1)
    eye = (ri == ci).astype(F32)
    base = min(_INV_BASE, c)
    same = lambda size: (ri // size) == (ci // size)
    dmat = [jnp.where(same(base), a, 0.0) for a in mats]
    x = [eye - dm for dm in dmat]
    pw = [_raw_dot(dm, dm, "nn", "hi3") for dm in dmat]
    span = 2
    while span < base:
        x = [xi + _raw_dot(xi, p, "nn", "hi3") for xi, p in zip(x, pw)]
        span *= 2
        if span < base:
            pw = [_raw_dot(p, p, "nn", "hi3") for p in pw]
    size = base
    while size < c:
        off = jnp.logical_and(jnp.logical_not(same(size)), same(2 * size))
        t = [_raw_dot(xi, jnp.where(off, a, 0.0), "nn", "hi3") for xi, a in zip(x, mats)]
        x = [xi - _raw_dot(ti, xi, "nn", "hi3") for xi, ti in zip(x, t)]
        size *= 2
    return x


@jax.custom_vjp
def _tri_inv(mats):
    return _tri_inv_raw(mats)


def _tri_inv_fwd(mats):
    x = _tri_inv_raw(mats)
    return x, x


def _tri_inv_bwd(x, dx):
    t = [_raw_dot(xi, di, "tn", "hi3") for xi, di in zip(x, dx)]
    return ([-_raw_dot(ti, xi, "nt", "hi3") for ti, xi in zip(t, x)],)


_tri_inv.defvjp(_tri_inv_fwd, _tri_inv_bwd)


@jax.custom_vjp
def _tri_inv_saved(mats, saved):
    return saved


def _tri_inv_saved_fwd(mats, saved):
    return saved, saved


def _tri_inv_saved_bwd(x, dx):
    return _tri_inv_bwd(x, dx)[0], [jnp.zeros_like(xi) for xi in x]


_tri_inv_saved.defvjp(_tri_inv_saved_fwd, _tri_inv_saved_bwd)


def _delta_chunk(qa, ka, va, graw, braw, alog, dtb, s, *, rev, minv_saved=None, with_minv=False):
    c = qa[0].shape[0]

    def each(f, *lists):
        return [f(*xs) for xs in zip(*lists, strict=True)]

    def l2n(t):
        return t * lax.rsqrt(jnp.sum(t * t, axis=-1, keepdims=True) + 1e-6)

    ri = lax.broadcasted_iota(jnp.int32, (c, c), 0)
    ci = lax.broadcasted_iota(jnp.int32, (c, c), 1)
    incl = (ci >= ri) if rev else (ci <= ri)
    strict = (ci > ri) if rev else (ci < ri)
    tmat = incl.astype(F32)
    tmat_t = ((ri >= ci) if rev else (ri <= ci)).astype(F32)

    q = each(lambda t: l2n(t) * (HD ** -0.5), qa)
    k = each(l2n, ka)
    g = each(lambda al, gr, dt: -jnp.exp(al) * _softplus(gr + dt), alog, graw, dtb)
    beta = each(jax.nn.sigmoid, braw)
    gc_wide = each(lambda t: _xdot(tmat, jnp.broadcast_to(t, (c, HD)), "nn", "a"), g)
    gc_rows = gc_wide if c == HD else each(lambda t: _xdot(tmat, jnp.broadcast_to(t, (c, c)), "nn", "a"), g)
    gc_cols = each(lambda t: _xdot(jnp.broadcast_to(t, (c, c)), tmat_t, "tn", "b"), g)
    gamma = each(lambda r, cc: jnp.where(incl, jnp.exp(jnp.where(incl, r - cc, 0.0)), 0.0), gc_rows, gc_cols)
    kb = each(lambda a, b: a * b, k, beta)
    a_mat = each(lambda a, b, gm: jnp.where(strict, _dot(a, b, "nt", "mxu") * gm, 0.0), kb, k, gamma)
    minv = _tri_inv(a_mat) if minv_saved is None else _tri_inv_saved(a_mat, minv_saved)
    eg = each(jnp.exp, gc_wide)
    u = each(lambda mi, v, b: _dot(mi, v * b, "nn", "hi3"), minv, va, beta)
    w = each(lambda mi, a, e: _dot(mi, a * e, "nn", "hi3"), minv, kb, eg)
    attn = each(lambda a, b, gm: _dot(a, b, "nt", "mxu") * gm, q, k, gamma)
    tot = each(lambda t: jnp.sum(t, axis=0, keepdims=True), g)
    q_dec = each(lambda a, e: a * e, q, eg)
    k_dec = each(lambda a, t, gw_: a * jnp.exp(t - gw_), k, tot, gc_wide)
    v_new = each(lambda uu, ww, ss: uu - _dot(ww, ss, "nn", "mxu"), u, w, s)
    o = each(lambda qd, ss, at, vn: _dot(qd, ss, "nn", "mxu") + _dot(at, vn, "nn", "mxu"), q_dec, s, attn, v_new)
    s_new = each(lambda ss, t, kd, vn: ss * jnp.exp(t) + _dot(kd, vn, "tn", "mxu"), s, tot, k_dec, v_new)
    return (o, s_new, minv) if with_minv else (o, s_new)


def _delta_args(a_ref, g_ref, al_ref, dt_ref, s_ref, hb, gcol):
    gts = [g_ref[hh] for hh in range(hb)]
    return ([a_ref[0, hh] for hh in range(hb)], [a_ref[1, hh] for hh in range(hb)], [a_ref[2, hh] for hh in range(hb)],
            [t[:, gcol:gcol + 1] for t in gts], [t[:, gcol + 1:gcol + 2] for t in gts],
            [al_ref[hh, 0:1, 0:1] for hh in range(hb)], [dt_ref[hh, 0:1, 0:1] for hh in range(hb)],
            [s_ref[hh] for hh in range(hb)])


def _delta_chunk_of_step(step, nch, nlc, rev):
    return (nch - 1 - step) if rev else (step + nlc) % nch


def _delta_fwd(act, gates_hm, alog_b, dtb_b, *, n_lat, n_ctx, nh, hb, rev):
    nt = n_lat + n_ctx
    nch, nlc = nt // CHUNK, n_lat // CHUNK
    gcol = 2 if rev else 0
    chunk = functools.partial(_delta_chunk_of_step, nch=nch, nlc=nlc, rev=rev)

    def body(a_ref, g_ref, al_ref, dt_ref, o_ref, st_ref, mi_ref, s_ref):
        n = pl.program_id(1)

        @pl.when(n == 0)
        def _():
            s_ref[...] = jnp.zeros_like(s_ref)

        args = _delta_args(a_ref, g_ref, al_ref, dt_ref, s_ref, hb, gcol)
        o, s_new, minv = _delta_chunk(*args, rev=rev, with_minv=True)
        for hh in range(hb):
            st_ref[hh] = args[7][hh]
            mi_ref[hh] = minv[hh]
            o_ref[:, hh * HD:(hh + 1) * HD] = o[hh]
            s_ref[hh] = s_new[hh]

    par = pl.BlockSpec((hb, 8, LANES), lambda h, n: (h, 0, 0))
    return pl.pallas_call(
        body, grid=(nh // hb, nch),
        in_specs=[pl.BlockSpec((3, hb, CHUNK, HD), lambda h, n: (0, h, chunk(n), 0)),
                  pl.BlockSpec((hb, CHUNK, 4), lambda h, n: (h, chunk(n), 0)), par, par],
        out_specs=[pl.BlockSpec((CHUNK, hb * HD), lambda h, n: (chunk(n), h)),
                   pl.BlockSpec((hb, None, HD, HD), lambda h, n: (h, n, 0, 0)),
                   pl.BlockSpec((hb, None, CHUNK, CHUNK), lambda h, n: (h, n, 0, 0))],
        out_shape=[jax.ShapeDtypeStruct((nt, nh * HD), F32), jax.ShapeDtypeStruct((nh, nch, HD, HD), F32),
                   jax.ShapeDtypeStruct((nh, nch, CHUNK, CHUNK), F32)],
        scratch_shapes=[pltpu.VMEM((hb, HD, HD), F32)], name="delta_fwd_rev" if rev else "delta_fwd",
        compiler_params=_cparams(("arbitrary", "arbitrary"), 0),
    )(act, gates_hm, alog_b, dtb_b)


def _delta_bwd(act, gates_hm, alog_b, dtb_b, states, minvs, do, *, n_lat, n_ctx, nh, hb, rev):
    nt = n_lat + n_ctx
    nch, nlc = nt // CHUNK, n_lat // CHUNK
    gcol = 2 if rev else 0

    def chunk(m):
        return _delta_chunk_of_step(nch - 1 - m, nch, nlc, rev)

    def body(a_ref, g_ref, al_ref, dt_ref, st_ref, mi_ref, do_ref, da_ref, dg_ref, dal_ref, ddt_ref, ds_ref):
        m = pl.program_id(1)

        @pl.when(m == 0)
        def _():
            ds_ref[...] = jnp.zeros_like(ds_ref)
            dal_ref[...] = jnp.zeros_like(dal_ref)
            ddt_ref[...] = jnp.zeros_like(ddt_ref)

        is_lat = chunk(m) < nlc
        args = _delta_args(a_ref, g_ref, al_ref, dt_ref, st_ref, hb, gcol)
        _, vjp = jax.vjp(functools.partial(_delta_chunk, rev=rev, minv_saved=[mi_ref[hh] for hh in range(hb)]), *args)
        do_h = [jnp.where(is_lat, do_ref[:, hh * HD:(hh + 1) * HD], 0.0) for hh in range(hb)]
        grads = vjp((do_h, [ds_ref[hh] for hh in range(hb)]))
        for hh in range(hb):
            dq, dk, dv, dgr, dbr, dal, ddt, ds = [gr[hh] for gr in grads]
            da_ref[0, hh] = dq
            da_ref[1, hh] = dk
            da_ref[2, hh] = dv
            lane = lax.broadcasted_iota(jnp.int32, (CHUNK, 4), 1)
            dg_ref[hh] = jnp.where(lane == 0, dgr, jnp.where(lane == 1, dbr, 0.0))
            dal_ref[hh] += jnp.broadcast_to(dal, (8, LANES))
            ddt_ref[hh] += jnp.broadcast_to(ddt, (8, LANES))
            ds_ref[hh] = ds

    par = pl.BlockSpec((hb, 8, LANES), lambda h, m: (h, 0, 0))
    return pl.pallas_call(
        body, grid=(nh // hb, nch),
        in_specs=[pl.BlockSpec((3, hb, CHUNK, HD), lambda h, m: (0, h, chunk(m), 0)),
                  pl.BlockSpec((hb, CHUNK, 4), lambda h, m: (h, chunk(m), 0)), par, par,
                  pl.BlockSpec((hb, None, HD, HD), lambda h, m: (h, nch - 1 - m, 0, 0)),
                  pl.BlockSpec((hb, None, CHUNK, CHUNK), lambda h, m: (h, nch - 1 - m, 0, 0)),
                  pl.BlockSpec((CHUNK, hb * HD), lambda h, m: (jnp.minimum(chunk(m), nlc - 1), h))],
        out_specs=[pl.BlockSpec((3, hb, CHUNK, HD), lambda h, m: (0, h, chunk(m), 0)),
                   pl.BlockSpec((hb, CHUNK, 4), lambda h, m: (h, chunk(m), 0)), par, par],
        out_shape=[jax.ShapeDtypeStruct((3, nh, nt, HD), F32), jax.ShapeDtypeStruct((nh, nt, 4), F32),
                   jax.ShapeDtypeStruct((nh, 8, LANES), F32), jax.ShapeDtypeStruct((nh, 8, LANES), F32)],
        scratch_shapes=[pltpu.VMEM((hb, HD, HD), F32)], name="delta_bwd_rev" if rev else "delta_bwd",
        compiler_params=_cparams(("arbitrary", "arbitrary"), 0),
    )(act, gates_hm, alog_b, dtb_b, states, minvs, do)


def _my_pos():
    return lax.axis_index("x"), lax.axis_index("y"), lax.axis_index("c")


def _flip(v, d):
    return 1 - v if d else v


def _bcast8(block, name):
    r, w = block.shape
    assert r % 8 == 0 and w % LANES == 0

    def body(x_ref, o_ref, send_sems, recv_sems, local_sem):
        x, y, c = _my_pos()

        def rows(px, py, pc):
            return o_ref.at[pl.ds((4 * px + 2 * py + pc) * r, r), :]

        mine = pltpu.make_async_copy(x_ref, rows(x, y, c), local_sem)
        mine.start()
        copies = []
        for kk in range(1, 8):
            dx, dy, dc = (kk >> 2) & 1, (kk >> 1) & 1, kk & 1
            peer = (_flip(x, dx), _flip(y, dy), _flip(c, dc))
            cp = pltpu.make_async_remote_copy(src_ref=x_ref, dst_ref=rows(x, y, c), send_sem=send_sems.at[kk - 1],
                                              recv_sem=recv_sems.at[kk - 1], device_id=peer, device_id_type=pl.DeviceIdType.MESH)
            cp.start()
            copies.append((cp, peer))
        for kk, (cp, peer) in enumerate(copies):
            pltpu.make_async_remote_copy(src_ref=x_ref, dst_ref=rows(*peer), send_sem=send_sems.at[kk], recv_sem=recv_sems.at[kk],
                                         device_id=peer, device_id_type=pl.DeviceIdType.MESH).wait_recv()
        for cp, _ in copies:
            cp.wait_send()
        mine.wait()

    return pl.pallas_call(
        body, out_shape=jax.ShapeDtypeStruct((8 * r, w), block.dtype),
        in_specs=[pl.BlockSpec(memory_space=pltpu.VMEM)], out_specs=pl.BlockSpec(memory_space=pltpu.VMEM),
        scratch_shapes=[pltpu.SemaphoreType.DMA((7,)), pltpu.SemaphoreType.DMA((7,)), pltpu.SemaphoreType.DMA], name=name,
        compiler_params=pltpu.CompilerParams(vmem_limit_bytes=int(max(32 * 1024 * 1024, 12 * _nbytes((r, w), block.dtype)))),
    )(block)


_CHIP_PEERS = ((1, 0), (0, 1), (1, 1))


def _col_half(ref, c, lead):
    hc = ref.shape[-1] // 2
    return ref.at[(*lead, slice(None), pl.ds(pl.multiple_of(c * hc, LANES), hc))]


def _into_slot(w, chip_idx, after, name):
    r, cc = w.shape
    tc = _pick(cc, 512)

    def body(i_ref, w_ref, after_ref, o_ref):
        o_ref[...] = w_ref[...].astype(o_ref.dtype)

    grid_spec = pltpu.PrefetchScalarGridSpec(
        num_scalar_prefetch=1, grid=(cc // tc,), in_specs=[pl.BlockSpec((r, tc), lambda j, s: (0, j)), ANY],
        out_specs=pl.BlockSpec((None, r, tc), lambda j, s: (s[0], 0, j)))
    return pl.pallas_call(body, grid_spec=grid_spec, out_shape=jax.ShapeDtypeStruct((4, r, cc), MXU_DTYPE), name=name,
                          compiler_params=_cparams(("arbitrary",), 6 * _nbytes((r, tc), F32)))(chip_idx, w, after)


_HBM = pl.BlockSpec(memory_space=pltpu.HBM)
_SEM = pl.BlockSpec(memory_space=pltpu.SEMAPHORE)
_DATAFLOW = pltpu.SideEffectType.DATAFLOW_SIDE_EFFECTING


def _in_hbm(a):
    return pltpu.with_memory_space_constraint(a, pltpu.HBM)


def _gather_start(bufs, after, name):
    n = len(bufs)

    def body(*refs):
        ins, send_sems, recv_sems, token = refs[:n], refs[n + 1], refs[n + 2], refs[-1]
        x, y, c = _my_pos()
        for a in range(n):
            piece = _col_half(ins[a], c, (2 * x + y,))
            for kk, (dx, dy) in enumerate(_CHIP_PEERS):
                pltpu.make_async_remote_copy(src_ref=piece, dst_ref=piece, send_sem=send_sems.at[3 * a + kk], recv_sem=recv_sems.at[3 * a + kk],
                                             device_id=(_flip(x, dx), _flip(y, dy), c), device_id_type=pl.DeviceIdType.MESH).start()
        token[...] = jnp.zeros_like(token)

    res = pl.pallas_call(
        body, name=name,
        out_shape=(pltpu.SemaphoreType.DMA((3 * n,)), pltpu.SemaphoreType.DMA((3 * n,)), *[pltpu.HBM(b.shape, b.dtype) for b in bufs],
                   jax.ShapeDtypeStruct((8, LANES), F32)),
        in_specs=[_HBM] * n + [ANY], out_specs=(_SEM, _SEM, *[_HBM] * n, pl.BlockSpec(memory_space=pltpu.VMEM)),
        input_output_aliases={a: 2 + a for a in range(n)}, compiler_params=pltpu.CompilerParams(has_side_effects=_DATAFLOW),
    )(*[_in_hbm(b) for b in bufs], after)
    return res[0], res[1], list(res[2:2 + n]), res[-1]


def _gather_wait(send_sems, recv_sems, bufs, after, name):
    n = len(bufs)

    def body(*refs):
        ins, s_sems, r_sems = refs[:n], refs[n], refs[n + 1]
        x, y, c = _my_pos()
        for a in range(n):
            for kk, (dx, dy) in enumerate(_CHIP_PEERS):
                px, py = _flip(x, dx), _flip(y, dy)
                cp = pltpu.make_async_remote_copy(src_ref=_col_half(ins[a], c, (2 * x + y,)), dst_ref=_col_half(ins[a], c, (2 * px + py,)),
                                                  send_sem=s_sems.at[3 * a + kk], recv_sem=r_sems.at[3 * a + kk], device_id=(px, py, c),
                                                  device_id_type=pl.DeviceIdType.MESH)
                cp.wait_send()
                cp.wait_recv()

    return pl.pallas_call(
        body, name=name, out_shape=[pltpu.HBM(b.shape, b.dtype) for b in bufs],
        in_specs=[_HBM] * n + [_SEM, _SEM] + [ANY] * len(after), out_specs=[_HBM] * n, input_output_aliases={a: a for a in range(n)},
        compiler_params=pltpu.CompilerParams(has_side_effects=_DATAFLOW),
    )(*bufs, send_sems, recv_sems, *after)


def _gather_forward(bufs, name):
    n = len(bufs)

    def body(*refs):
        outs = refs[n:2 * n]
        send_sems, recv_sems = refs[2 * n:]
        x, y, c = _my_pos()
        sib = (x, y, 1 - c)

        def remote(a, kk, half):
            dx, dy = _CHIP_PEERS[kk]
            piece = _col_half(outs[a], half, (2 * _flip(x, dx) + _flip(y, dy),))
            return pltpu.make_async_remote_copy(src_ref=piece, dst_ref=piece, send_sem=send_sems.at[3 * a + kk], recv_sem=recv_sems.at[3 * a + kk],
                                                device_id=sib, device_id_type=pl.DeviceIdType.MESH)

        sends = [remote(a, kk, c) for a in range(n) for kk in range(3)]
        for cp in sends:
            cp.start()
        for a in range(n):
            for kk in range(3):
                remote(a, kk, 1 - c).wait_recv()
        for cp in sends:
            cp.wait_send()

    return pl.pallas_call(
        body, out_shape=[jax.ShapeDtypeStruct(b.shape, b.dtype) for b in bufs],
        in_specs=[ANY] * n, out_specs=[ANY] * n, input_output_aliases={a: a for a in range(n)},
        scratch_shapes=[pltpu.SemaphoreType.DMA((3 * n,))] * 2, name=name,
    )(*bufs)


def _forward_start(bufs, after, name):
    n = len(bufs)

    def body(*refs):
        ins, send_sems, recv_sems, token = refs[:n], refs[n + 1], refs[n + 2], refs[-1]
        x, y, c = _my_pos()
        for a in range(n):
            for kk, (dx, dy) in enumerate(_CHIP_PEERS):
                piece = _col_half(ins[a], c, (2 * _flip(x, dx) + _flip(y, dy),))
                pltpu.make_async_remote_copy(src_ref=piece, dst_ref=piece, send_sem=send_sems.at[3 * a + kk], recv_sem=recv_sems.at[3 * a + kk],
                                             device_id=(x, y, 1 - c), device_id_type=pl.DeviceIdType.MESH).start()
        token[...] = jnp.zeros_like(token)

    res = pl.pallas_call(
        body, name=name,
        out_shape=(pltpu.SemaphoreType.DMA((3 * n,)), pltpu.SemaphoreType.DMA((3 * n,)), *[pltpu.HBM(b.shape, b.dtype) for b in bufs],
                   jax.ShapeDtypeStruct((8, LANES), F32)),
        in_specs=[_HBM] * n + [ANY], out_specs=(_SEM, _SEM, *[_HBM] * n, pl.BlockSpec(memory_space=pltpu.VMEM)),
        input_output_aliases={a: 2 + a for a in range(n)}, compiler_params=pltpu.CompilerParams(has_side_effects=_DATAFLOW),
    )(*[_in_hbm(b) for b in bufs], after)
    return res[0], res[1], list(res[2:2 + n]), res[-1]


def _forward_wait(send_sems, recv_sems, bufs, after, name):
    n = len(bufs)

    def body(*refs):
        ins, s_sems, r_sems = refs[:n], refs[n], refs[n + 1]
        x, y, c = _my_pos()
        for a in range(n):
            for kk, (dx, dy) in enumerate(_CHIP_PEERS):
                slot = 2 * _flip(x, dx) + _flip(y, dy)
                cp = pltpu.make_async_remote_copy(src_ref=_col_half(ins[a], c, (slot,)), dst_ref=_col_half(ins[a], 1 - c, (slot,)),
                                                  send_sem=s_sems.at[3 * a + kk], recv_sem=r_sems.at[3 * a + kk], device_id=(x, y, 1 - c),
                                                  device_id_type=pl.DeviceIdType.MESH)
                cp.wait_send()
                cp.wait_recv()

    return pl.pallas_call(
        body, name=name, out_shape=[pltpu.HBM(b.shape, b.dtype) for b in bufs],
        in_specs=[_HBM] * n + [_SEM, _SEM] + [ANY] * len(after), out_specs=[_HBM] * n, input_output_aliases={a: a for a in range(n)},
        compiler_params=pltpu.CompilerParams(has_side_effects=_DATAFLOW),
    )(*bufs, send_sems, recv_sems, *after)


def _scatter_start(parts, after, name):
    n = len(parts)

    def body(*refs):
        ins, lands, send_sems, recv_sems, token = refs[:n], refs[n:2 * n], refs[2 * n + 1], refs[2 * n + 2], refs[-1]
        x, y, c = _my_pos()
        for a in range(n):
            for kk, (dx, dy) in enumerate(_CHIP_PEERS):
                px, py = _flip(x, dx), _flip(y, dy)
                pltpu.make_async_remote_copy(src_ref=ins[a].at[2 * px + py], dst_ref=lands[a].at[2 * x + y], send_sem=send_sems.at[3 * a + kk],
                                             recv_sem=recv_sems.at[3 * a + kk], device_id=(px, py, c), device_id_type=pl.DeviceIdType.MESH).start()
        token[...] = jnp.zeros_like(token)

    thru = [pltpu.HBM(p.shape, p.dtype) for p in parts]
    res = pl.pallas_call(
        body, name=name,
        out_shape=(pltpu.SemaphoreType.DMA((3 * n,)), pltpu.SemaphoreType.DMA((3 * n,)), *thru, *thru, jax.ShapeDtypeStruct((8, LANES), F32)),
        in_specs=[_HBM] * (2 * n) + [ANY], out_specs=(_SEM, _SEM, *[_HBM] * (2 * n), pl.BlockSpec(memory_space=pltpu.VMEM)),
        input_output_aliases={a: 2 + a for a in range(2 * n)}, compiler_params=pltpu.CompilerParams(has_side_effects=_DATAFLOW),
    )(*[_in_hbm(p) for p in parts], *[_in_hbm(lax.empty(p.shape, p.dtype)) for p in parts], after)
    return res[0], res[1], list(res[2:2 + n]), list(res[2 + n:2 + 2 * n]), res[-1]


def _scatter_wait(send_sems, recv_sems, parts, lands, after, name):
    n = len(parts)

    def body(*refs):
        ins, lnd, s_sems, r_sems = refs[:n], refs[n:2 * n], refs[2 * n], refs[2 * n + 1]
        x, y, c = _my_pos()
        for a in range(n):
            for kk, (dx, dy) in enumerate(_CHIP_PEERS):
                px, py = _flip(x, dx), _flip(y, dy)
                cp = pltpu.make_async_remote_copy(src_ref=ins[a].at[2 * px + py], dst_ref=lnd[a].at[2 * px + py], send_sem=s_sems.at[3 * a + kk],
                                                  recv_sem=r_sems.at[3 * a + kk], device_id=(px, py, c), device_id_type=pl.DeviceIdType.MESH)
                cp.wait_send()
                cp.wait_recv()

    thru = [pltpu.HBM(p.shape, p.dtype) for p in parts]
    res = pl.pallas_call(
        body, name=name, out_shape=[*thru, *thru],
        in_specs=[_HBM] * (2 * n) + [_SEM, _SEM] + [ANY] * len(after), out_specs=[_HBM] * (2 * n),
        input_output_aliases={a: a for a in range(2 * n)}, compiler_params=pltpu.CompilerParams(has_side_effects=_DATAFLOW),
    )(*parts, *lands, send_sems, recv_sems, *after)
    return list(res[:n]), list(res[n:])


def _pair_send_start(grads, after, name):
    n = len(grads)

    def body(*refs):
        ins, lands, send_sems, recv_sems, token = refs[:n], refs[n:2 * n], refs[2 * n + 1], refs[2 * n + 2], refs[-1]
        x, y, c = _my_pos()
        for a in range(n):
            pltpu.make_async_remote_copy(src_ref=_col_half(ins[a], 1 - c, (slice(None),)), dst_ref=lands[a], send_sem=send_sems.at[a],
                                         recv_sem=recv_sems.at[a], device_id=(x, y, 1 - c), device_id_type=pl.DeviceIdType.MESH).start()
        token[...] = jnp.zeros_like(token)

    land_shapes = [(4, g.shape[1], g.shape[2] // 2) for g in grads]
    res = pl.pallas_call(
        body, name=name,
        out_shape=(pltpu.SemaphoreType.DMA((n,)), pltpu.SemaphoreType.DMA((n,)), *[pltpu.HBM(g.shape, g.dtype) for g in grads],
                   *[pltpu.HBM(s, g.dtype) for s, g in zip(land_shapes, grads)], jax.ShapeDtypeStruct((8, LANES), F32)),
        in_specs=[_HBM] * (2 * n) + [ANY], out_specs=(_SEM, _SEM, *[_HBM] * (2 * n), pl.BlockSpec(memory_space=pltpu.VMEM)),
        input_output_aliases={a: 2 + a for a in range(2 * n)}, compiler_params=pltpu.CompilerParams(has_side_effects=_DATAFLOW),
    )(*[_in_hbm(g) for g in grads], *[_in_hbm(lax.empty(s, g.dtype)) for s, g in zip(land_shapes, grads)], after)
    return res[0], res[1], list(res[2:2 + n]), list(res[2 + n:2 + 2 * n]), res[-1]


def _pair_send_wait(send_sems, recv_sems, grads, lands, after, name):
    n = len(grads)

    def body(*refs):
        ins, lnd, s_sems, r_sems = refs[:n], refs[n:2 * n], refs[2 * n], refs[2 * n + 1]
        x, y, c = _my_pos()
        for a in range(n):
            cp = pltpu.make_async_remote_copy(src_ref=_col_half(ins[a], 1 - c, (slice(None),)), dst_ref=lnd[a], send_sem=s_sems.at[a],
                                              recv_sem=r_sems.at[a], device_id=(x, y, 1 - c), device_id_type=pl.DeviceIdType.MESH)
            cp.wait_send()
            cp.wait_recv()

    res = pl.pallas_call(
        body, name=name, out_shape=[*[pltpu.HBM(g.shape, g.dtype) for g in grads], *[pltpu.HBM(l.shape, l.dtype) for l in lands]],
        in_specs=[_HBM] * (2 * n) + [_SEM, _SEM, ANY], out_specs=[_HBM] * (2 * n), input_output_aliases={a: a for a in range(2 * n)},
        compiler_params=pltpu.CompilerParams(has_side_effects=_DATAFLOW),
    )(*grads, *lands, send_sems, recv_sems, after)
    return list(res[:n]), list(res[n:])


def _pair_join_halves(bufs, name):
    n = len(bufs)

    def body(*refs):
        outs = refs[n:2 * n]
        send_sems, recv_sems = refs[2 * n:]
        x, y, c = _my_pos()
        sib = (x, y, 1 - c)
        sends = []
        for a in range(n):
            mine = _col_half(outs[a], c, ())
            cp = pltpu.make_async_remote_copy(src_ref=mine, dst_ref=mine, send_sem=send_sems.at[a], recv_sem=recv_sems.at[a],
                                              device_id=sib, device_id_type=pl.DeviceIdType.MESH)
            cp.start()
            sends.append(cp)
        for a in range(n):
            other = _col_half(outs[a], 1 - c, ())
            pltpu.make_async_remote_copy(src_ref=other, dst_ref=other, send_sem=send_sems.at[a], recv_sem=recv_sems.at[a],
                                         device_id=sib, device_id_type=pl.DeviceIdType.MESH).wait_recv()
        for cp in sends:
            cp.wait_send()

    return pl.pallas_call(
        body, out_shape=[jax.ShapeDtypeStruct(b.shape, b.dtype) for b in bufs],
        in_specs=[ANY] * n, out_specs=[ANY] * n, input_output_aliases={a: a for a in range(n)},
        scratch_shapes=[pltpu.SemaphoreType.DMA((n,)), pltpu.SemaphoreType.DMA((n,))], name=name,
    )(*bufs)


def _add_my_half(g, recv, c_idx, name):
    _, r, cc = g.shape
    hc = cc // 2
    tc = _pick(hc, 256)
    per = hc // tc

    def body(c_ref, g_ref, r_ref, o_ref):
        o_ref[...] = (g_ref[...] + r_ref[...]).astype(o_ref.dtype)

    grid_spec = pltpu.PrefetchScalarGridSpec(
        num_scalar_prefetch=1, grid=(4, per),
        in_specs=[pl.BlockSpec((None, r, tc), lambda s, j, c_ref: (s, 0, c_ref[0] * per + j)),
                  pl.BlockSpec((None, r, tc), lambda s, j, c_ref: (s, 0, j))],
        out_specs=pl.BlockSpec((None, r, tc), lambda s, j, c_ref: (s, 0, j)))
    return pl.pallas_call(body, grid_spec=grid_spec, out_shape=jax.ShapeDtypeStruct((4, r, hc), WIRE_DTYPE), name=name,
                          compiler_params=_cparams(("arbitrary", "arbitrary"), 6 * _nbytes((r, tc), F32)))(c_idx, g, recv)


def _sum_slots(own, recv, slots, name):
    _, r, hc = own.shape
    tc = _pick(hc, 256)
    per = hc // tc

    def body(s0, s1, s2, s3, s4, a_ref, b_ref, c_ref, d_ref, o_ref):
        f = lambda ref: ref[...].astype(F32)
        o_ref[...] = ((f(a_ref) + f(b_ref)) + f(c_ref)) + f(d_ref)

    slot = lambda i: pl.BlockSpec((None, r, tc), lambda j, *s: (s[i][0], 0, j))
    grid_spec = pltpu.PrefetchScalarGridSpec(
        num_scalar_prefetch=5, grid=(per,), in_specs=[slot(0), slot(1), slot(2), slot(3)],
        out_specs=pl.BlockSpec((r, tc), lambda j, *s: (0, s[4][0] * per + j)))
    return pl.pallas_call(body, grid_spec=grid_spec, out_shape=jax.ShapeDtypeStruct((r, 2 * hc), F32), name=name,
                          compiler_params=_cparams(("arbitrary",), 14 * _nbytes((r, tc), F32)))(*slots, own, recv, recv, recv)


def _adamw_math(w, g, m, v):
    m = ADAM_B1 * m + (1.0 - ADAM_B1) * g
    v = ADAM_B2 * v + (1.0 - ADAM_B2) * (g * g)
    m_hat = m / (1.0 - ADAM_B1 ** ADAM_STEP)
    v_hat = v / (1.0 - ADAM_B2 ** ADAM_STEP)
    delta = -ADAM_LR * (m_hat / (jnp.sqrt(v_hat) + ADAM_EPS) + ADAM_WD * w)
    return delta, m, v


def _adamw(w, g, m, v, name, emit_g=False):
    r, cc = w.shape[-2:]
    block_elems = 512 * 1024
    tr = _pick(r, max(8, block_elems // max(cc, 1)), 8)
    if w.ndim == 2 and cc % LANES == 0 and tr * cc < block_elems // 2:
        tr, tc = r, _pick(cc, max(LANES, block_elems // r))
    else:
        tc = cc

    n_out = 4 if emit_g else 3

    def body(w_ref, g_ref, m_ref, v_ref, d_ref, nm_ref, nv_ref, *go_ref):
        gv = g_ref[...]
        d, nm, nv = _adamw_math(w_ref[...], gv, m_ref[...], v_ref[...])
        d_ref[...] = d
        nm_ref[...] = nm
        nv_ref[...] = nv
        if emit_g:
            go_ref[0][...] = gv

    if w.ndim == 3:
        spec = pl.BlockSpec((None, tr, cc), lambda i: (0, i, 0))
    elif tc == cc:
        spec = pl.BlockSpec((tr, cc), lambda i: (i, 0))
    else:
        spec = pl.BlockSpec((r, tc), lambda j: (0, j))
    return pl.pallas_call(
        body, grid=((r // tr) * (cc // tc),), in_specs=[spec] * 4, out_specs=[spec] * n_out,
        out_shape=[jax.ShapeDtypeStruct(w.shape, F32)] * n_out, name=name,
        compiler_params=_cparams(("arbitrary",), 18 * _nbytes((tr, tc), F32)))(w, g, m, v)


def _sum8(allv, name):
    r8, w = allv.shape
    r = r8 // 8

    def body(a_ref, o_ref):
        acc = a_ref[0:r, :]
        for d in range(1, 8):
            acc = acc + a_ref[d * r:(d + 1) * r, :]
        o_ref[...] = acc

    return pl.pallas_call(body, out_shape=jax.ShapeDtypeStruct((r, w), F32), name=name,
                          compiler_params=_cparams((), 12 * _nbytes((r8, w), F32)))(allv)


def _dmod_prep(alld, name):
    _, w = alld.shape
    tw = _pick(w, 2048)

    def body(a_ref, d_ref, b_ref):
        ctx = a_ref[1:2, :]
        tot = a_ref[0:1, :]
        d_ref[0:1, :] = tot
        for d in range(1, 8):
            ctx = ctx + a_ref[8 * d + 1:8 * d + 2, :]
            tot = tot + a_ref[8 * d:8 * d + 1, :]
            d_ref[d:d + 1, :] = a_ref[8 * d:8 * d + 1, :]
        d_ref[8:9, :] = ctx
        d_ref[9:16, :] = jnp.zeros((7, tw), F32)
        b_ref[...] = tot + ctx

    return pl.pallas_call(
        body, grid=(w // tw,), in_specs=[pl.BlockSpec((64, tw), lambda j: (0, j))],
        out_specs=[pl.BlockSpec((16, tw), lambda j: (0, j)), pl.BlockSpec((1, tw), lambda j: (0, j))],
        out_shape=[jax.ShapeDtypeStruct((16, w), F32), jax.ShapeDtypeStruct((1, w), F32)], name=name,
        compiler_params=_cparams(("arbitrary",), 0))(alld)


def _pack(arrs, rows=8):
    flat = jnp.concatenate([a.reshape(-1).astype(F32) for a in arrs])
    per = rows * LANES
    n = flat.shape[0]
    padded = -(-n // per) * per
    flat = jnp.pad(flat, (0, padded - n))
    return flat.reshape(rows, padded // rows)


def _unpack(slab, shapes):
    flat = slab.reshape(-1)
    out, off = [], 0
    for s in shapes:
        n = math.prod(s)
        out.append(flat[off:off + n].reshape(s))
        off += n
    return out


def kernel(x, c, ctx, c_ctx, ln_in_g, ln_in_b, w_mod, b_mod, w_in, w_qkv_conv, a_log_f, dt_bias_f, a_log_b, dt_bias_b, dn_norm_g, conf_dw_w, conf_dw_b, conf_ln_g, conf_ln_b, w_out, ln1_g, ln1_b, w_mlp1, b_mlp1, w_mlp2, b_mlp2, ln2_g, ln2_b, loss_target, m_c_ctx, m_ln_in_g, m_ln_in_b, m_w_mod, m_b_mod, m_w_in, m_w_qkv_conv, m_a_log_f, m_dt_bias_f, m_a_log_b, m_dt_bias_b, m_dn_norm_g, m_conf_dw_w, m_conf_dw_b, m_conf_ln_g, m_conf_ln_b, m_w_out, m_ln1_g, m_ln1_b, m_w_mlp1, m_b_mlp1, m_w_mlp2, m_b_mlp2, m_ln2_g, m_ln2_b, v_c_ctx, v_ln_in_g, v_ln_in_b, v_w_mod, v_b_mod, v_w_in, v_w_qkv_conv, v_a_log_f, v_dt_bias_f, v_a_log_b, v_dt_bias_b, v_dn_norm_g, v_conf_dw_w, v_conf_dw_b, v_conf_ln_g, v_conf_ln_b, v_w_out, v_ln1_g, v_ln1_b, v_w_mlp1, v_b_mlp1, v_w_mlp2, v_b_mlp2, v_ln2_g, v_ln2_b):
    n_lat, d = x.shape[1], x.shape[2]
    n_ctx = ctx.shape[1]
    nt = n_lat + n_ctx
    dn = d // 2
    nh = dn // HD
    conf = d - dn
    half = conf // 2
    gw = math.isqrt(n_lat)
    ff = 4 * w_mlp1.shape[2]
    fl = w_mlp1.shape[2]
    in_cols = 4 * w_in.shape[2]
    r_in = w_in.shape[2]
    n_gate = 4 * nh
    gb = max(LANES, d // 8)
    g0 = 4 * dn + 2 * conf
    in_pad = g0 + gb
    ml = w_mod.shape[2]
    assert gw * gw == n_lat and nt % gw == 0 and in_cols == 4 * dn + n_gate + 2 * conf and nh * HD == dn
    hb = 8 if nh % 8 == 0 else (2 if nh % 2 == 0 else 1)

    mx, my, mc = _my_pos()
    chip = 2 * mx + my
    dev = 4 * mx + 2 * my + mc
    c_idx = jnp.reshape(mc, (1,)).astype(jnp.int32)

    x2, ctx2, tgt2 = x[0], ctx[0], loss_target[0]
    row = lambda a: a.reshape(1, -1)

    e0_shapes = [(d,), w_qkv_conv.shape[1:], conf_dw_w.shape[1:]]
    e0 = _bcast8(_pack([c[0], w_qkv_conv[0], conf_dw_w[0]]), "bcast_inputs")
    e0 = e0.reshape(8, -1)
    per_dev = [_unpack(e0[dd], e0_shapes) for dd in range(8)]
    all_c = jnp.stack([p[0] for p in per_dev])
    w_conv = jnp.concatenate([per_dev[2 * j][1] for j in range(4)], axis=1)
    w_dw = jnp.concatenate([per_dev[2 * j][2] for j in range(4)], axis=1)

    s16 = jnp.concatenate([all_c, c_ctx[None, :], jnp.zeros((7, d), F32)], axis=0)
    b_mod_loc = lax.dynamic_slice(b_mod, (0, chip * ml), (1, ml))
    to_mxu = lambda t: t.astype(MXU_DTYPE)
    modblk = _matmul(s16, w_mod[0], form="nn", m=16, n=ml, k=d, tm=16, tn=_pick(ml, 1024), tk=_pick(d, 1024),
                     out_dtype=F32, name="mod_fwd", bias=b_mod_loc, a_fn=lambda t: to_mxu(_silu(t)), b_fn=to_mxu)
    allmod = _bcast8(modblk, "bcast_mod").reshape(8, 16, ml)[0::2]
    mod_mine = lax.dynamic_index_in_dim(allmod, dev, axis=1, keepdims=False).reshape(1, 4 * ml)
    mod_ctx = allmod[:, 8, :].reshape(1, 4 * ml)
    sh_a, sc_a, g_a, sh_m, sc_m, g_m = [mod_mine[:, i * d:(i + 1) * d] for i in range(6)]
    csh_a, csc_a = mod_ctx[:, 0:d], mod_ctx[:, d:2 * d]

    chip_idx = jnp.reshape(chip, (1,)).astype(jnp.int32)
    win_t_loc = jnp.transpose(w_in[0]).astype(MXU_DTYPE)
    slot_in = _into_slot(win_t_loc, chip_idx, all_c, "weights_into_slot_0")
    gi_send, gi_recv, gi_bufs, gi_token = _gather_start([slot_in], allmod, "gather_w_in_start")
    slots_rest = [_into_slot(w, chip_idx, gi_token, f"weights_into_slot_{i + 1}") for i, w in enumerate([w_out[0], w_mlp1[0], w_mlp2[0]])]
    sh_a, csh_a = sh_a + gi_token[0:1, 0:1], csh_a + gi_token[0:1, 0:1]

    tmr = _pick(n_lat, 128, 8)
    tmc = _pick(n_ctx, 128, 8)
    ln_in = [row(ln_in_g), row(ln_in_b)]

    def stage_a(rv, pv):
        x0 = _ln(rv[0], pv[0], pv[1])
        return [x0 * (1.0 + pv[3]) + pv[2]], []

    xm_lat, = _rowwise(stage_a, [(x2, d, 0, 0)], ln_in + [sh_a, sc_a], [(d, MXU_DTYPE)], [], n_rows=n_lat, tm=tmr, name="ln_mod_lat")
    xm_ctx, = _rowwise(stage_a, [(ctx2, d, 0, 0)], ln_in + [csh_a, csc_a], [(d, MXU_DTYPE)], [], n_rows=n_ctx, tm=tmc, name="ln_mod_ctx")
    xm = jnp.concatenate([xm_lat, xm_ctx], axis=0)

    gw_in, = _gather_forward(_gather_wait(gi_send, gi_recv, gi_bufs, [xm] + slots_rest, "gather_w_in_wait"), "gather_w_in_forward")
    gs_send, gs_recv, gs_bufs, gs_token = _gather_start(slots_rest, gw_in, "gather_rest_start")
    win_t = gw_in.reshape(in_cols, d)
    win_t = jnp.concatenate([win_t[:4 * dn], win_t[4 * dn + n_gate:], win_t[4 * dn:4 * dn + n_gate],
                             jnp.zeros((gb - n_gate, d), MXU_DTYPE)], axis=0)

    tm_nt = _pick(nt, 1280, 16)
    tn_in = _pick(in_pad, 1280)
    tk_d = _pick(d, 4096)
    tn_in_s = _pick(in_pad, 640)
    h = _matmul(xm, win_t, form="nt", m=nt, n=in_pad, k=d, tm=tm_nt, tn=tn_in, tk=_pick(d, 2048), out_dtype=F32, name="in_proj",
                bias=jnp.zeros((1, in_pad), F32) + gs_token[0:1, 0:1])

    act = _conv7_fwd(h, w_conv, n_lat=n_lat, n_ctx=n_ctx, nh=nh)
    gates_hm = jnp.transpose(h[:, g0:g0 + n_gate].reshape(nt, 4, nh), (2, 0, 1))
    bc = lambda a: jnp.broadcast_to(a.reshape(nh, 1, 1), (nh, 8, LANES))
    dk = dict(n_lat=n_lat, n_ctx=n_ctx, nh=nh, hb=hb)
    o_f, st_f, mi_f = _delta_fwd(act, gates_hm, bc(a_log_f), bc(dt_bias_f), rev=False, **dk)
    o_b, st_b, mi_b = _delta_fwd(act, gates_hm, bc(a_log_b), bc(dt_bias_b), rev=True, **dk)

    h3 = h.reshape(nt // gw, gw, in_pad)
    ck = dict(gw=gw, col0=4 * dn, half=half)
    yh = _conf_conv_fwd(h3, w_dw, vertical=False, **ck).reshape(n_lat, half)
    yv = _conf_conv_fwd(h3, w_dw, vertical=True, **ck).reshape(n_lat, half)

    def mix_fn(o_heads, z_heads, ya, yb, dng, ba, bb, ga, gbb, la, lb):
        outs = []
        for o, z in zip(o_heads, z_heads):
            ms = jnp.mean(o * o, axis=-1, keepdims=True)
            outs.append(o * lax.rsqrt(ms + 1e-6) * dng * _silu(z))
        ya, yb = ya + ba, yb + bb
        mu = (jnp.sum(ya, axis=-1, keepdims=True) + jnp.sum(yb, axis=-1, keepdims=True)) / conf
        ya, yb = ya - mu, yb - mu
        var = (jnp.sum(ya * ya, axis=-1, keepdims=True) + jnp.sum(yb * yb, axis=-1, keepdims=True)) / conf
        rs = lax.rsqrt(var + LN_EPS)
        ca, cb = ya * rs * ga + la, yb * rs * gbb + lb
        return outs, _silu(ca), _silu(cb)

    def heads(t):
        return [t[:, i * HD:(i + 1) * HD] for i in range(nh)]

    def halves(p):
        return p[:, :half], p[:, half:]

    mix_params = [row(dn_norm_g), row(conf_dw_b), row(conf_ln_g), row(conf_ln_b)]

    def mix_args(rv, pv):
        (ba, bb), (ga, gbb), (la, lb) = halves(pv[1]), halves(pv[2]), halves(pv[3])
        return (heads(rv[0] + rv[1]), heads(rv[2]), rv[3], rv[4], pv[0], ba, bb, ga, gbb, la, lb)

    def stage_e(rv, pv):
        outs, ca, cb = mix_fn(*mix_args(rv, pv))
        return [jnp.concatenate(outs + [ca, cb], axis=-1)], []

    mix_rows = [(o_f, dn, 0, 0), (o_b, dn, 0, 0), (h, dn, 3, 0), (yh, half, 0, 0), (yv, half, 0, 0)]
    mix, = _rowwise(stage_e, mix_rows, mix_params, [(d, MXU_DTYPE)], [], n_rows=n_lat, tm=tmr, name="mixer_fwd")

    gs_bufs = _gather_wait(gs_send, gs_recv, gs_bufs, [mix], "gather_rest_wait")
    gw_out, = _gather_forward(gs_bufs[:1], "gather_w_out_forward")
    gf_send, gf_recv, gf_bufs, gf_token = _forward_start(gs_bufs[1:], gw_out, "gather_mlp_forward_start")
    wout_full = gw_out.reshape(d, d)

    tm_l = _pick(n_lat, 1024, 16)
    tn_d = _pick(d, 1024)
    y1 = _matmul(mix, wout_full, form="nn", m=n_lat, n=d, k=d, tm=tm_l, tn=tn_d, tk=tk_d, out_dtype=F32, name="out_proj",
                 bias=jnp.zeros((1, d), F32) + gf_token[0:1, 0:1])

    def res1(x0, y1v, ga, l1g, l1b, shm, scm):
        x1 = _ln(ALPHA * x0 + ga * y1v, l1g, l1b)
        return x1, x1 * (1.0 + scm) + shm

    f_params = ln_in + [g_a, row(ln1_g), row(ln1_b), sh_m, sc_m]

    def stage_f(rv, pv):
        x1, u = res1(_ln(rv[0], pv[0], pv[1]), rv[1], *pv[2:])
        return [x1, u], []

    x1, u = _rowwise(stage_f, [(x2, d, 0, 0), (y1, d, 0, 0)], f_params, [(d, F32), (d, MXU_DTYPE)], [], n_rows=n_lat, tm=tmr, name="res1_fwd")

    gw_1, gw_2 = _forward_wait(gf_send, gf_recv, gf_bufs, [u], "gather_mlp_forward_wait")
    w2_full = gw_2.reshape(ff, d)
    w1_sh = gw_1
    tn_f = _pick(fl, 1024)
    relu_h = _matmul(u, w1_sh, form="nn", m=n_lat, n=ff, k=d, tm=tm_l, tn=tn_f, tk=tk_d, out_dtype=MXU_DTYPE, name="mlp1",
                     bias=row(b_mlp1), epi=lambda r: jnp.maximum(r, 0.0), b_split=fl)
    sq = lambda t: (t.astype(F32) * t.astype(F32)).astype(MXU_DTYPE)
    tk_f = _pick(ff, 4096)
    y2 = _matmul(relu_h, w2_full, form="nn", m=n_lat, n=d, k=ff, tm=tm_l, tn=tn_d, tk=tk_f, out_dtype=F32, name="mlp2",
                 bias=row(b_mlp2), a_fn=sq)

    def loss_fn(x1v, y2v, gm, l2g, l2b, tgt):
        x2v = _ln(ALPHA * x1v + gm * y2v, l2g, l2b)
        return (0.5 / d) * jnp.sum(jnp.square(x2v - tgt))

    def stage_g(rv, pv):
        loss, grads = jax.value_and_grad(loss_fn, argnums=(0, 1, 2, 3, 4))(rv[0], rv[1], pv[0], pv[1], pv[2], rv[2])
        dx1, dy2r, dgm, dl2g, dl2b = grads
        dy2 = dy2r
        return [dx1, dy2], [jnp.reshape(loss, (1, 1)), dgm, dl2g, dl2b, jnp.sum(dy2, axis=0, keepdims=True)]

    dx1p, dy2, loss_acc, d_g_m, d_ln2_g, d_ln2_b, d_b_mlp2 = _rowwise(
        stage_g, [(x1, d, 0, 0), (y2, d, 0, 0), (tgt2, d, 0, 0)], [g_m, row(ln2_g), row(ln2_b)],
        [(d, F32), (d, MXU_DTYPE)], [LANES, d, d, d, d], n_rows=n_lat, tm=tmr, name="loss_res2_bwd")

    dhid, d_b_mlp1 = _matmul(dy2, w2_full, form="nt", m=n_lat, n=ff, k=d, tm=tm_l, tn=tn_f, tk=tk_d, out_dtype=MXU_DTYPE, name="mlp2_bwd_x",
                             epi=lambda r, rh: r * (2.0 * rh.astype(F32)), epi_in=relu_h, colsum=True)
    tk_l = _pick(n_lat, 4096, 16)
    d_w2 = _matmul(relu_h, dy2, form="tn", m=ff, n=d, k=n_lat, tm=tn_f, tn=tn_d, tk=tk_l, out_dtype=F32, name="mlp2_bwd_w", a_fn=sq)
    du = _matmul(dhid, w1_sh, form="nt", m=n_lat, n=d, k=ff, tm=tm_l, tn=tn_d, tk=_pick(fl, 4096), out_dtype=F32, name="mlp1_bwd_x", b_split=fl)
    d_w1 = _matmul(u, dhid, form="tn", m=d, n=ff, k=n_lat, tm=tn_d, tn=tn_f, tk=tk_l, out_dtype=F32, name="mlp1_bwd_w", o_split=fl)

    as_idx = lambda v: jnp.reshape(v, (1,)).astype(jnp.int32)
    slot_idx = [chip_idx, as_idx(2 * (1 - mx) + my), as_idx(2 * mx + (1 - my)), as_idx(2 * (1 - mx) + (1 - my)), c_idx]
    mp_send, mp_recv, mlp_grads, mlp_half, mp_token = _pair_send_start([d_w1, d_w2.reshape(4, fl, d)], loss_acc, "mlp_grads_pair_start")

    def stage_h(rv, pv):
        x0 = _ln(rv[0], pv[0], pv[1])
        _, vjp = jax.vjp(res1, x0, rv[1], *pv[2:])
        dx0, dy1r, dga, dl1g, dl1b, dshm, dscm = vjp((rv[3], rv[2]))
        return [dx0, dy1r], [dga, dl1g, dl1b, dshm, dscm]

    h_params = ln_in + [g_a + mp_token[0:1, 0:1]] + f_params[3:]
    dx0p, dy1, d_g_a, d_ln1_g, d_ln1_b, d_sh_m, d_sc_m = _rowwise(
        stage_h, [(x2, d, 0, 0), (y1, d, 0, 0), (du, d, 0, 0), (dx1p, d, 0, 0)], h_params,
        [(d, F32), (d, MXU_DTYPE)], [d, d, d, d, d], n_rows=n_lat, tm=tmr, name="res1_bwd")

    dmix = _matmul(dy1, wout_full, form="nt", m=n_lat, n=d, k=d, tm=tm_l, tn=tn_d, tk=tk_d, out_dtype=F32, name="out_proj_bwd_x")
    d_wout = _matmul(mix, dy1, form="tn", m=d, n=d, k=n_lat, tm=tn_d, tn=tn_d, tk=tk_l, out_dtype=F32, name="out_proj_bwd_w")

    def stage_i(rv, pv):
        args = mix_args(rv, pv)
        _, vjp = jax.vjp(mix_fn, *args)
        dm = rv[5]
        d_outs = [dm[:, i * HD:(i + 1) * HD] for i in range(nh)]
        dca, dcb = dm[:, dn:dn + half], dm[:, dn + half:]
        do_h, dz_h, dya, dyb, ddng, dba, dbb, dga, dgbb, dla, dlb = vjp((d_outs, dca, dcb))
        cat = lambda *p: jnp.concatenate(p, axis=-1)
        return ([cat(*do_h), cat(*dz_h), dya, dyb], [ddng, cat(dba, dbb), cat(dga, dgbb), cat(dla, dlb)])

    do, dz, dyh, dyv, d_dn_norm_g, d_conf_dw_b, d_conf_ln_g, d_conf_ln_b = _rowwise(
        stage_i, mix_rows + [(dmix, d, 0, 0)], mix_params, [(dn, F32), (dn, MXU_DTYPE), (half, F32), (half, F32)],
        [HD, conf, conf, conf], n_rows=n_lat, tm=tmr, name="mixer_bwd")

    mlp_grads, mlp_half = _pair_send_wait(mp_send, mp_recv, mlp_grads, mlp_half, do, "mlp_grads_pair_wait")
    mlp_sums = [_add_my_half(g, r, c_idx, f"mlp_grads_pair_add_{i}") for i, (g, r) in enumerate(zip(mlp_grads, mlp_half))]
    ms_send, ms_recv, mlp_sums, mlp_lands, ms_token = _scatter_start(mlp_sums, loss_acc, "mlp_grads_scatter_start")
    tie = ms_token[0:1, 0:1].reshape(1, 1, 1)

    dval_h, dgate_h, d_wdw_h = _conf_conv_bwd(h3, w_dw, dyh.reshape(gw, gw, half), vertical=False, **ck)
    dval_v, dgate_v, d_wdw_v = _conf_conv_bwd(h3, w_dw, dyv.reshape(gw, gw, half), vertical=True, **ck)
    dact_f, dgt_f, d_alog_f, d_dt_f = _delta_bwd(act, gates_hm, bc(a_log_f) + tie, bc(dt_bias_f), st_f, mi_f, do, rev=False, **dk)
    dact_b, dgt_b, d_alog_b, d_dt_b = _delta_bwd(act, gates_hm, bc(a_log_b) + tie, bc(dt_bias_b), st_b, mi_b, do, rev=True, **dk)
    dh_qkv, d_wconv = _conv7_bwd(h, w_conv, dact_f, dact_b, n_lat=n_lat, n_ctx=n_ctx, nh=nh)

    dgates = jnp.stack([dgt_f[..., 0], dgt_f[..., 1], dgt_b[..., 0], dgt_b[..., 1]], axis=0)
    dgates = jnp.transpose(dgates, (2, 0, 1)).reshape(nt, n_gate)
    dgates = jnp.pad(dgates, ((0, 0), (0, gb - n_gate))).astype(MXU_DTYPE)
    zrows = lambda t: jnp.pad(t, ((0, n_ctx), (0, 0)))
    dh = jnp.concatenate([dh_qkv, zrows(dz), zrows(dval_h.reshape(n_lat, half)), zrows(dval_v.reshape(n_lat, half)),
                          zrows(dgate_h.reshape(n_lat, half)), zrows(dgate_v.reshape(n_lat, half)), dgates], axis=1)

    tk_in = _pick(in_pad, 2560)
    d_win_t = _matmul(dh, xm, form="tn", m=in_pad, n=d, k=nt, tm=tn_in_s, tn=tn_d, tk=nt, out_dtype=F32, name="in_proj_bwd_w")
    d_win_t = jnp.concatenate([d_win_t[:4 * dn], d_win_t[g0:g0 + n_gate], d_win_t[4 * dn:g0]], axis=0)

    proj_grads = [d_win_t.reshape(4, r_in, d), d_wout.reshape(4, d // 4, d)]
    pp_send, pp_recv, proj_grads, proj_half, pp_token = _pair_send_start(proj_grads, loss_acc, "proj_grads_pair_start")
    dxm = _matmul(dh, win_t, form="nn", m=nt, n=d, k=in_pad, tm=tm_nt, tn=tn_d, tk=tk_in, out_dtype=F32, name="in_proj_bwd_x",
                  bias=jnp.zeros((1, d), F32) + pp_token[0:1, 0:1])
    proj_grads, proj_half = _pair_send_wait(pp_send, pp_recv, proj_grads, proj_half, dxm, "proj_grads_pair_wait")
    proj_sums = [_add_my_half(g, r, c_idx, f"proj_grads_pair_add_{i}") for i, (g, r) in enumerate(zip(proj_grads, proj_half))]
    ps_send, ps_recv, proj_sums, proj_lands, ps_token = _scatter_start(proj_sums, loss_acc, "proj_grads_scatter_start")
    mlp_sums, mlp_lands = _scatter_wait(ms_send, ms_recv, mlp_sums, mlp_lands, [ps_token], "mlp_grads_scatter_wait")
    mlp_mine = [_sum_slots(own, rcv, slot_idx, f"mlp_grads_chip_sum_{i}") for i, (own, rcv) in enumerate(zip(mlp_sums, mlp_lands))]
    g_w_mlp1, g_w_mlp2 = _pair_join_halves(mlp_mine, "mlp_grads_pair_join")
    mlp_adam = {"w_mlp1": _adamw(w_mlp1, g_w_mlp1[None], m_w_mlp1, v_w_mlp1, "adamw_w_mlp1", emit_g=True),
                "w_mlp2": _adamw(w_mlp2, g_w_mlp2[None], m_w_mlp2, v_w_mlp2, "adamw_w_mlp2", emit_g=True)}
    mlp_done = [mlp_adam["w_mlp1"][2], mlp_adam["w_mlp2"][2]]
    sc_a_tied = sc_a + ps_token[0:1, 0:1]

    def mod_in(xr, lg, lb, sh, sc):
        x0 = _ln(xr, lg, lb)
        return x0, x0 * (1.0 + sc) + sh

    def stage_j(rv, pv):
        _, vjp = jax.vjp(mod_in, rv[0], *pv)
        dx0 = rv[2] if len(rv) > 2 else jnp.zeros_like(rv[1])
        dxr, dlg, dlb, dsh, dsc = vjp((dx0, rv[1]))
        return [dxr], [dlg, dlb, dsh, dsc]

    grad_x, dlg_l, dlb_l, d_sh_a, d_sc_a = _rowwise(
        stage_j, [(x2, d, 0, 0), (dxm, d, 0, 0), (dx0p, d, 0, 0)], ln_in + [sh_a, sc_a_tied], [(d, F32)], [d, d, d, d],
        n_rows=n_lat, tm=tmr, name="ln_mod_bwd_lat", after=mlp_done)
    _, dlg_c, dlb_c, d_csh_a, d_csc_a = _rowwise(
        stage_j, [(ctx2, d, 0, 0), (dxm, d, 0, n_lat // tmc)], ln_in + [csh_a, csc_a], [(d, F32)], [d, d, d, d],
        n_rows=n_ctx, tm=tmc, name="ln_mod_bwd_ctx", after=mlp_done)

    zd = jnp.zeros((1, d), F32)
    dmod_rows = jnp.concatenate([jnp.concatenate([d_sh_a, d_sc_a, d_g_a, d_sh_m, d_sc_m, d_g_m], axis=1),
                                 jnp.concatenate([d_csh_a, d_csc_a, zd, zd, zd, zd], axis=1), jnp.zeros((6, 6 * d), F32)], axis=0)
    d16, g_b_mod = _dmod_prep(_bcast8(dmod_rows, "bcast_dmod"), "dmod_prep")
    d16_loc = lax.dynamic_slice(d16, (0, chip * ml), (16, ml))
    g_w_mod = _matmul(s16, d16_loc, form="tn", m=d, n=ml, k=16, tm=_pick(d, 512), tn=_pick(ml, 1024), tk=16, out_dtype=F32,
                      name="mod_bwd_w", a_fn=_silu, kind="hi")
    dsilu = _matmul(d16_loc, w_mod[0], form="nt", m=16, n=d, k=ml, tm=16, tn=_pick(d, 1024), tk=_pick(ml, 1024), out_dtype=F32,
                    name="mod_bwd_c", a_fn=to_mxu, b_fn=to_mxu)
    d_cctx_part = dsilu[8:9, :] * (1 - mc).astype(F32)

    head_sum = lambda t: t[:, 0, 0]
    small = [d_cctx_part, dlg_l + dlg_c, dlb_l + dlb_c, d_wconv, head_sum(d_alog_f), head_sum(d_dt_f), head_sum(d_alog_b),
             head_sum(d_dt_b), d_dn_norm_g, jnp.concatenate([d_wdw_h, d_wdw_v], axis=1), d_conf_dw_b, d_conf_ln_g, d_conf_ln_b,
             d_ln1_g, d_ln1_b, d_b_mlp1, d_b_mlp2, d_ln2_g, d_ln2_b]
    small_shapes = [(d,), (d,), (d,), (SHORT_CONV, 3 * dn), (nh,), (nh,), (nh,), (nh,), (HD,), (CONF_K, conf), (conf,), (conf,), (conf,),
                    (d,), (d,), (ff,), (d,), (d,), (d,)]
    ssum = _sum8(_bcast8(_pack(small), "bcast_small_grads"), "sum_small_grads")
    (t_cctx, g_ln_in_g, g_ln_in_b, g_wconv_full, g_a_log_f, g_dt_bias_f, g_a_log_b, g_dt_bias_b, g_dn_norm_g, g_wdw_full, g_conf_dw_b,
     g_conf_ln_g, g_conf_ln_b, g_ln1_g, g_ln1_b, g_b_mlp1, g_b_mlp2, g_ln2_g, g_ln2_b) = _unpack(ssum, small_shapes)
    sg = jax.nn.sigmoid(c_ctx)
    g_c_ctx = t_cctx * (sg * (1.0 + c_ctx * (1.0 - sg)))
    g_w_qkv_conv = lax.dynamic_slice(g_wconv_full, (0, chip * w_qkv_conv.shape[2]), w_qkv_conv.shape[1:])
    g_conf_dw_w = lax.dynamic_slice(g_wdw_full, (0, chip * conf_dw_w.shape[2]), conf_dw_w.shape[1:])

    grads = dict(c_ctx=g_c_ctx, ln_in_g=g_ln_in_g, ln_in_b=g_ln_in_b, w_mod=g_w_mod[None], b_mod=g_b_mod, w_in=None,
                 w_qkv_conv=g_w_qkv_conv[None], a_log_f=g_a_log_f[None], dt_bias_f=g_dt_bias_f[None], a_log_b=g_a_log_b[None],
                 dt_bias_b=g_dt_bias_b[None], dn_norm_g=g_dn_norm_g[None], conf_dw_w=g_conf_dw_w[None], conf_dw_b=g_conf_dw_b[None],
                 conf_ln_g=g_conf_ln_g[None], conf_ln_b=g_conf_ln_b[None], w_out=None, ln1_g=g_ln1_g[None], ln1_b=g_ln1_b[None],
                 w_mlp1=g_w_mlp1[None], b_mlp1=g_b_mlp1[None], w_mlp2=g_w_mlp2[None], b_mlp2=g_b_mlp2[None], ln2_g=g_ln2_g[None],
                 ln2_b=g_ln2_b[None])
    weights = dict(c_ctx=c_ctx, ln_in_g=ln_in_g, ln_in_b=ln_in_b, w_mod=w_mod, b_mod=b_mod, w_in=w_in, w_qkv_conv=w_qkv_conv,
                   a_log_f=a_log_f, dt_bias_f=dt_bias_f, a_log_b=a_log_b, dt_bias_b=dt_bias_b, dn_norm_g=dn_norm_g, conf_dw_w=conf_dw_w,
                   conf_dw_b=conf_dw_b, conf_ln_g=conf_ln_g, conf_ln_b=conf_ln_b, w_out=w_out, ln1_g=ln1_g, ln1_b=ln1_b, w_mlp1=w_mlp1,
                   b_mlp1=b_mlp1, w_mlp2=w_mlp2, b_mlp2=b_mlp2, ln2_g=ln2_g, ln2_b=ln2_b)
    m_in = dict(c_ctx=m_c_ctx, ln_in_g=m_ln_in_g, ln_in_b=m_ln_in_b, w_mod=m_w_mod, b_mod=m_b_mod, w_in=m_w_in, w_qkv_conv=m_w_qkv_conv,
                a_log_f=m_a_log_f, dt_bias_f=m_dt_bias_f, a_log_b=m_a_log_b, dt_bias_b=m_dt_bias_b, dn_norm_g=m_dn_norm_g,
                conf_dw_w=m_conf_dw_w, conf_dw_b=m_conf_dw_b, conf_ln_g=m_conf_ln_g, conf_ln_b=m_conf_ln_b, w_out=m_w_out, ln1_g=m_ln1_g,
                ln1_b=m_ln1_b, w_mlp1=m_w_mlp1, b_mlp1=m_b_mlp1, w_mlp2=m_w_mlp2, b_mlp2=m_b_mlp2, ln2_g=m_ln2_g, ln2_b=m_ln2_b)
    v_in = dict(c_ctx=v_c_ctx, ln_in_g=v_ln_in_g, ln_in_b=v_ln_in_b, w_mod=v_w_mod, b_mod=v_b_mod, w_in=v_w_in, w_qkv_conv=v_w_qkv_conv,
                a_log_f=v_a_log_f, dt_bias_f=v_dt_bias_f, a_log_b=v_a_log_b, dt_bias_b=v_dt_bias_b, dn_norm_g=v_dn_norm_g,
                conf_dw_w=v_conf_dw_w, conf_dw_b=v_conf_dw_b, conf_ln_g=v_conf_ln_g, conf_ln_b=v_conf_ln_b, w_out=v_w_out, ln1_g=v_ln1_g,
                ln1_b=v_ln1_b, w_mlp1=v_w_mlp1, b_mlp1=v_b_mlp1, w_mlp2=v_w_mlp2, b_mlp2=v_b_mlp2, ln2_g=v_ln2_g, ln2_b=v_ln2_b)
    names = list(weights)
    big_names = ("w_mod", "w_mlp1", "w_mlp2", "w_in", "w_out")
    delta, new_m, new_v = {}, {}, {}

    def big_adamw(nm):
        emit = nm != "w_mod"
        res = mlp_adam[nm] if nm in mlp_adam else _adamw(weights[nm], grads[nm], m_in[nm], v_in[nm], f"adamw_{nm}", emit_g=emit)
        delta[nm], new_m[nm], new_v[nm] = res[:3]
        if emit:
            grads[nm] = res[3]

    for nm in big_names[:3]:
        big_adamw(nm)
    small_names = [nm for nm in names if nm not in big_names]
    shapes = [weights[nm].shape for nm in small_names]
    for nm in small_names:
        grads[nm] = grads[nm].reshape(weights[nm].shape)
    packed = [_pack([src[nm] for nm in small_names]) for src in (weights, grads, m_in, v_in)]
    dl, mm, vv = _adamw(*packed, "adamw_small")
    for nm, a, b_, c_ in zip(small_names, _unpack(dl, shapes), _unpack(mm, shapes), _unpack(vv, shapes)):
        delta[nm], new_m[nm], new_v[nm] = a, b_, c_

    done = [new_v[nm] for nm in big_names[:3]] + [vv]
    proj_sums, proj_lands = _scatter_wait(ps_send, ps_recv, proj_sums, proj_lands, done, "proj_grads_scatter_wait")
    proj_mine = [_sum_slots(own, rcv, slot_idx, f"proj_grads_chip_sum_{i}") for i, (own, rcv) in enumerate(zip(proj_sums, proj_lands))]
    g_win_t, g_w_out = _pair_join_halves(proj_mine, "proj_grads_pair_join")
    grads["w_out"] = g_w_out[None]
    tr_in = lambda a: jnp.transpose(a[0])
    tr_out = lambda a: jnp.transpose(a)[None]
    dl_t, mm_t, vv_t, g_t = _adamw(tr_in(w_in), g_win_t, tr_in(m_w_in), tr_in(v_w_in), "adamw_w_in", emit_g=True)
    delta["w_in"], new_m["w_in"], new_v["w_in"], grads["w_in"] = tr_out(dl_t), tr_out(mm_t), tr_out(vv_t), tr_out(g_t)
    big_adamw("w_out")

    loss = lax.psum(loss_acc[0, 0], MESH_AXES)
    return (loss, grad_x[None], *[grads[nm] for nm in names], *[delta[nm] for nm in names],
            *[new_m[nm] for nm in names], *[new_v[nm] for nm in names])
```
